```python
import jax, jax.numpy as jnp
from jax import lax
import numpy as np

D_MODEL = 1024
BATCH = 8
SEQ = 2048
DEPTH = 2

FNET_GROUPS = 4
FNET_GROUP_DIM = 128
FNET_WIDTH = FNET_GROUPS * FNET_GROUP_DIM
ATTN_HEADS = 8
ATTN_KV_HEADS = 2
HEAD_DIM = 64
ATTN_WIDTH = ATTN_HEADS * HEAD_DIM
KV_WIDTH = ATTN_KV_HEADS * HEAD_DIM
WINDOW = 128
BLOCK = 128
REL_BUCKETS = 32
REL_MAX_DIST = 128
MIX_IN_WIDTH = FNET_WIDTH + ATTN_WIDTH + 2 * KV_WIDTH
MIX_OUT_WIDTH = FNET_WIDTH + ATTN_WIDTH
D_FF = 2816
D_INNER = 2 * D_MODEL
SSM_HEAD_DIM = 64
SSM_HEADS = D_INNER // SSM_HEAD_DIM
SSM_GROUPS = 4
D_STATE = 128
CONV_WIDTH = 5
SSD_CHUNK = 128
CONV_DIM = D_INNER + 2 * SSM_GROUPS * D_STATE
SSM_IN_WIDTH = D_INNER + CONV_DIM + 2 * SSM_HEADS
N_EXPERTS = 8
TOP_K = 2
D_FF_EXPERT = 3584
N_EVEN = (DEPTH + 1) // 2
N_ODD = DEPTH // 2
EPS = 1e-6

kernel_name = "hybrid_fnet_swa_ssd_moe_adaln"


def rms_norm(x, w):
    xf = x.astype(jnp.float32)
    y = xf * lax.rsqrt(jnp.mean(xf * xf, axis=-1, keepdims=True) + EPS)
    return (y * w.astype(jnp.float32)).astype(x.dtype)


def modulate(x, w, shift, scale):
    return rms_norm(x, w) * (1 + scale[:, None, :]) + shift[:, None, :]


def swiglu(t, w1, w3, w2):
    return (jax.nn.silu(t @ w1) * (t @ w3)) @ w2


def t5_band_buckets():
    i = np.arange(BLOCK)[:, None]
    j = np.arange(3 * BLOCK)[None, :]
    rel = (j - BLOCK) - i
    half = REL_BUCKETS // 2
    max_exact = half // 2
    n = np.abs(rel)
    large = max_exact + (np.log(np.maximum(n, 1) / max_exact)
                         / np.log(REL_MAX_DIST / max_exact) * (half - max_exact)).astype(np.int32)
    large = np.minimum(large, half - 1)
    bucket = (rel > 0).astype(np.int32) * half + np.where(n < max_exact, n, large)
    return bucket.astype(np.int32), rel


def band_mask(n_blk):
    _, rel = t5_band_buckets()
    kpos = (np.arange(n_blk)[:, None, None] * BLOCK
            + np.arange(3 * BLOCK)[None, None, :] - BLOCK)
    return (np.abs(rel)[None] <= WINDOW) & (kpos >= 0) & (kpos < n_blk * BLOCK)


def fourier_mix(u):
    b, s, _ = u.shape
    ug = u.astype(jnp.float32).reshape(b, s, FNET_GROUPS, FNET_GROUP_DIM)
    y = jnp.fft.fft2(ug, axes=(1, 3), norm="ortho").real
    return y.reshape(b, s, FNET_WIDTH).astype(u.dtype)


def window_gqa(q, k, v, rel_bias, sink):
    b, s = q.shape[0], q.shape[1]
    nb = s // BLOCK
    g = ATTN_HEADS // ATTN_KV_HEADS
    qb = q.reshape(b, nb, BLOCK, ATTN_KV_HEADS, g, HEAD_DIM)

    def band(t):
        tp = jnp.pad(t, ((0, 0), (BLOCK, BLOCK), (0, 0), (0, 0)))
        tp = tp.reshape(b, nb + 2, BLOCK, ATTN_KV_HEADS, HEAD_DIM)
        return jnp.concatenate([tp[:, :-2], tp[:, 1:-1], tp[:, 2:]], axis=2)

    kw, vw = band(k), band(v)
    bucket, _ = t5_band_buckets()
    bias = jnp.transpose(rel_bias[bucket].astype(jnp.float32), (2, 0, 1))
    bias = bias.reshape(ATTN_KV_HEADS, g, BLOCK, 3 * BLOCK)
    logits = jnp.einsum('bnqkgd,bnskd->bnkgqs', qb, kw).astype(jnp.float32) * (HEAD_DIM ** -0.5) + bias
    logits = jnp.where(band_mask(nb)[None, :, None, None], logits, -jnp.inf)
    sk = sink.astype(jnp.float32).reshape(ATTN_KV_HEADS, g)[None, None, :, :, None, None]
    m = jnp.maximum(jnp.max(logits, axis=-1, keepdims=True), sk)
    p = jnp.exp(logits - m)
    probs = p / (jnp.sum(p, axis=-1, keepdims=True) + jnp.exp(sk - m))
    out = jnp.einsum('bnkgqs,bnskd->bnqkgd', probs.astype(v.dtype), vw)
    return out.reshape(b, s, ATTN_WIDTH)


def even_mixer(h, in_w, q_norm_w, k_norm_w, sink, out_w, rel_bias):
    b, s, _ = h.shape
    proj = h @ in_w
    u, q, k, v = jnp.split(proj, [FNET_WIDTH, FNET_WIDTH + ATTN_WIDTH,
                                  FNET_WIDTH + ATTN_WIDTH + KV_WIDTH], axis=-1)
    q = rms_norm(q.reshape(b, s, ATTN_HEADS, HEAD_DIM), q_norm_w)
    k = rms_norm(k.reshape(b, s, ATTN_KV_HEADS, HEAD_DIM), k_norm_w)
    v = v.reshape(b, s, ATTN_KV_HEADS, HEAD_DIM)
    y = jnp.concatenate([fourier_mix(u), window_gqa(q, k, v, rel_bias, sink)], axis=-1)
    return y @ out_w


def ssd_scan(x, dt, A, Bm, Cm):
    b, s, h, p = x.shape
    nc = s // SSD_CHUNK
    r = SSM_HEADS // SSM_GROUPS
    X = (x * dt[..., None]).reshape(b, nc, SSD_CHUNK, SSM_GROUPS, r, p)
    a = (A * dt).reshape(b, nc, SSD_CHUNK, SSM_GROUPS, r).transpose(0, 3, 4, 1, 2)
    Bc = Bm.reshape(b, nc, SSD_CHUNK, SSM_GROUPS, D_STATE)
    Cc = Cm.reshape(b, nc, SSD_CHUNK, SSM_GROUPS, D_STATE)
    a_cum = jnp.cumsum(a, axis=-1)
    tril = np.tril(np.ones((SSD_CHUNK, SSD_CHUNK), dtype=bool))
    Lmat = jnp.exp(jnp.where(tril, a_cum[..., :, None] - a_cum[..., None, :], -jnp.inf))
    CB = jnp.einsum('bclgn,bcsgn->bgcls', Cc, Bc)
    y_diag = jnp.einsum('bgrcls,bcsgrp->bclgrp', CB[:, :, None] * Lmat, X)
    decay_states = jnp.exp(a_cum[..., -1:] - a_cum)
    states = jnp.einsum('bclgn,bgrcl,bclgrp->cbgrpn', Bc, decay_states, X)
    chunk_decay = jnp.exp(a_cum[..., -1]).transpose(3, 0, 1, 2)

    def step(hc, inp):
        st, dec = inp
        return hc * dec[..., None, None] + st, hc

    h0 = jnp.zeros(states.shape[1:], jnp.float32)
    _, prev = lax.scan(step, h0, (states, chunk_decay))
    y_off = jnp.einsum('bclgn,cbgrpn,bgrcl->bclgrp', Cc, prev, jnp.exp(a_cum))
    return (y_diag + y_off).reshape(b, s, h, p)


def ssd_mixer(h, in_w, conv_w, conv_b, dt_bias_f, dt_bias_b, A_log_f, A_log_b, D, gnorm_w, out_w):
    b, s, _ = h.shape
    proj = h @ in_w
    z, xbc, dt = jnp.split(proj, [D_INNER, D_INNER + CONV_DIM], axis=-1)
    xbc = lax.conv_general_dilated(xbc, conv_w[:, None, :], (1,),
                                   ((CONV_WIDTH // 2, CONV_WIDTH // 2),),
                                   dimension_numbers=('NWC', 'WIO', 'NWC'),
                                   feature_group_count=CONV_DIM)
    xbc = jax.nn.silu(xbc + conv_b)
    xs, Bm, Cm = jnp.split(xbc, [D_INNER, D_INNER + SSM_GROUPS * D_STATE], axis=-1)
    xs = xs.reshape(b, s, SSM_HEADS, SSM_HEAD_DIM).astype(jnp.float32)
    Bm = Bm.reshape(b, s, SSM_GROUPS, D_STATE).astype(jnp.float32)
    Cm = Cm.reshape(b, s, SSM_GROUPS, D_STATE).astype(jnp.float32)
    dt = dt.astype(jnp.float32)
    dt_f = jax.nn.softplus(dt[..., :SSM_HEADS] + dt_bias_f.astype(jnp.float32))
    dt_b = jax.nn.softplus(dt[..., SSM_HEADS:] + dt_bias_b.astype(jnp.float32))
    A_f = -jnp.exp(A_log_f.astype(jnp.float32))
    A_b = -jnp.exp(A_log_b.astype(jnp.float32))
    flip = lambda t: jnp.flip(t, axis=1)
    y_f = ssd_scan(xs, dt_f, A_f, Bm, Cm)
    y_b = flip(ssd_scan(flip(xs), flip(dt_b), A_b, flip(Bm), flip(Cm)))
    y = (y_f + y_b + D.astype(jnp.float32)[:, None] * xs).reshape(b, s, D_INNER)
    y = rms_norm(y * jax.nn.silu(z.astype(jnp.float32)), gnorm_w).astype(h.dtype)
    return y @ out_w


def moe_swiglu(h, router_w, router_b, w1, w3, w2):
    b, s, d = h.shape
    t = h.reshape(b * s, d)
    logits = (t @ router_w).astype(jnp.float32) + router_b.astype(jnp.float32)
    top_v, top_i = lax.top_k(logits, TOP_K)
    top_w = jax.nn.softmax(top_v, axis=-1)
    gates = jnp.sum(jax.nn.one_hot(top_i, N_EXPERTS, dtype=jnp.float32) * top_w[..., None], axis=1)
    out = jnp.zeros((b * s, d), jnp.float32)
    for e in range(N_EXPERTS):
        out = out + gates[:, e:e + 1] * swiglu(t, w1[e], w3[e], w2[e]).astype(jnp.float32)
    return out.reshape(b, s, d).astype(h.dtype)


def setup_inputs(seed: int = 0) -> dict:
    key = jax.random.key(seed)
    ks = iter(jax.random.split(key, 64))
    f32 = jnp.float32

    def nrm(shape, scale):
        return jax.random.normal(next(ks), shape, f32) * scale

    def gain(shape):
        return 1.0 + nrm(shape, 0.05)

    def dt_bias(shape):
        u = jax.random.uniform(next(ks), shape, f32, np.log(1e-3), np.log(1e-1))
        dt = jnp.exp(u)
        return dt + jnp.log(-jnp.expm1(-dt))

    def a_log(shape):
        return jnp.log(jax.random.uniform(next(ks), shape, f32, 1.0, 16.0))

    d = D_MODEL
    return {
        "x": nrm((BATCH, SEQ, d), 1.0),
        "c": nrm((BATCH, d), 1.0),
        "rel_bias": nrm((REL_BUCKETS, ATTN_HEADS), 0.5),
        "ev_ada_w": nrm((N_EVEN, d, 6 * d), 0.5 * d ** -0.5),
        "ev_ada_b": nrm((N_EVEN, 6 * d), 0.02),
        "ev_norm1_w": gain((N_EVEN, d)),
        "ev_in_w": nrm((N_EVEN, d, MIX_IN_WIDTH), d ** -0.5),
        "ev_q_norm_w": gain((N_EVEN, HEAD_DIM)),
        "ev_k_norm_w": gain((N_EVEN, HEAD_DIM)),
        "ev_sink": nrm((N_EVEN, ATTN_HEADS), 0.5),
        "ev_out_w": nrm((N_EVEN, MIX_OUT_WIDTH, d), MIX_OUT_WIDTH ** -0.5),
        "ev_norm2_w": gain((N_EVEN, d)),
        "ev_ffn_w1": nrm((N_EVEN, d, D_FF), d ** -0.5),
        "ev_ffn_w3": nrm((N_EVEN, d, D_FF), d ** -0.5),
        "ev_ffn_w2": nrm((N_EVEN, D_FF, d), D_FF ** -0.5),
        "od_ada_w": nrm((N_ODD, d, 6 * d), 0.5 * d ** -0.5),
        "od_ada_b": nrm((N_ODD, 6 * d), 0.02),
        "od_norm1_w": gain((N_ODD, d)),
        "od_in_w": nrm((N_ODD, d, SSM_IN_WIDTH), d ** -0.5),
        "od_conv_w": nrm((N_ODD, CONV_WIDTH, CONV_DIM), CONV_WIDTH ** -0.5),
        "od_conv_b": nrm((N_ODD, CONV_DIM), 0.02),
        "od_dt_bias_f": dt_bias((N_ODD, SSM_HEADS)),
        "od_dt_bias_b": dt_bias((N_ODD, SSM_HEADS)),
        "od_A_log_f": a_log((N_ODD, SSM_HEADS)),
        "od_A_log_b": a_log((N_ODD, SSM_HEADS)),
        "od_D": gain((N_ODD, SSM_HEADS)),
        "od_gnorm_w": gain((N_ODD, D_INNER)),
        "od_out_w": nrm((N_ODD, D_INNER, d), D_INNER ** -0.5),
        "od_norm2_w": gain((N_ODD, d)),
        "od_router_w": nrm((N_ODD, d, N_EXPERTS), d ** -0.5),
        "od_router_b": nrm((N_ODD, N_EXPERTS), 0.01),
        "od_moe_w1": nrm((N_ODD, N_EXPERTS, d, D_FF_EXPERT), d ** -0.5),
        "od_moe_w3": nrm((N_ODD, N_EXPERTS, d, D_FF_EXPERT), d ** -0.5),
        "od_moe_w2": nrm((N_ODD, N_EXPERTS, D_FF_EXPERT, d), D_FF_EXPERT ** -0.5),
    }


def reference(x, c, rel_bias,
              ev_ada_w, ev_ada_b, ev_norm1_w, ev_in_w, ev_q_norm_w, ev_k_norm_w, ev_sink,
              ev_out_w, ev_norm2_w, ev_ffn_w1, ev_ffn_w3, ev_ffn_w2,
              od_ada_w, od_ada_b, od_norm1_w, od_in_w, od_conv_w, od_conv_b,
              od_dt_bias_f, od_dt_bias_b, od_A_log_f, od_A_log_b, od_D, od_gnorm_w, od_out_w,
              od_norm2_w, od_router_w, od_router_b, od_moe_w1, od_moe_w3, od_moe_w2):
    cs = jax.nn.silu(c)
    for i in range(DEPTH):
        j = i // 2
        if i % 2 == 0:
            mod = cs @ ev_ada_w[j] + ev_ada_b[j]
            sh1, sc1, g1, sh2, sc2, g2 = jnp.split(mod, 6, axis=-1)
            hm = even_mixer(modulate(x, ev_norm1_w[j], sh1, sc1), ev_in_w[j], ev_q_norm_w[j],
                            ev_k_norm_w[j], ev_sink[j], ev_out_w[j], rel_bias)
            x = x + g1[:, None, :] * hm
            hf = swiglu(modulate(x, ev_norm2_w[j], sh2, sc2), ev_ffn_w1[j], ev_ffn_w3[j], ev_ffn_w2[j])
            x = x + g2[:, None, :] * hf
        else:
            mod = cs @ od_ada_w[j] + od_ada_b[j]
            sh1, sc1, g1, sh2, sc2, g2 = jnp.split(mod, 6, axis=-1)
            hm = ssd_mixer(modulate(x, od_norm1_w[j], sh1, sc1), od_in_w[j], od_conv_w[j], od_conv_b[j],
                           od_dt_bias_f[j], od_dt_bias_b[j], od_A_log_f[j], od_A_log_b[j], od_D[j],
                           od_gnorm_w[j], od_out_w[j])
            x = x + g1[:, None, :] * hm
            hf = moe_swiglu(modulate(x, od_norm2_w[j], sh2, sc2), od_router_w[j], od_router_b[j],
                            od_moe_w1[j], od_moe_w3[j], od_moe_w2[j])
            x = x + g2[:, None, :] * hf
    return x
```

```python
import functools

import numpy as np
import jax
import jax.numpy as jnp
from jax import lax
from jax.experimental import pallas as pl
from jax.experimental.pallas import tpu as pltpu

F32 = jnp.float32
BF16 = jnp.bfloat16
HIGHEST = lax.Precision.HIGHEST

EPS = 1e-6
FNET_GROUPS = 4
FNET_GROUP_DIM = 128
FNET_WIDTH = FNET_GROUPS * FNET_GROUP_DIM
ATTN_HEADS = 8
ATTN_KV_HEADS = 2
HEAD_DIM = 64
ATTN_WIDTH = ATTN_HEADS * HEAD_DIM
KV_WIDTH = ATTN_KV_HEADS * HEAD_DIM
WINDOW = 128
BLOCK = 128
REL_BUCKETS = 32
REL_MAX_DIST = 128
SSM_HEAD_DIM = 64
SSM_GROUPS = 4
D_STATE = 128
CONV_WIDTH = 5
SSD_CHUNK = 128
N_EXPERTS = 8
TOP_K = 2
NEG_BIG = -1e30

V7X_VMEM_LIMIT_BYTES = 56 * 1024 * 1024
MOE_TILE_ROWS = 512


def _cparams(*sem):
    return pltpu.CompilerParams(dimension_semantics=sem, vmem_limit_bytes=V7X_VMEM_LIMIT_BYTES)


def _modnorm(x, nw, sh, sc):
    ms = jnp.mean(x * x, axis=-1, keepdims=True)
    return x * lax.rsqrt(ms + EPS) * nw * (1.0 + sc) + sh


def _silu(x):
    return x * (1.0 / (1.0 + jnp.exp(-x)))


def _ada_kernel(c_ref, w_ref, b_ref, o_ref):
    cs = _silu(c_ref[...]).astype(BF16)
    o_ref[...] = jnp.dot(cs, w_ref[...].astype(BF16), preferred_element_type=F32) + b_ref[...]


def ada_mod(c, w, b):
    bsz, d = c.shape
    n = w.shape[1]
    tn = 1536
    return pl.pallas_call(
        _ada_kernel,
        grid=(n // tn,),
        in_specs=[pl.BlockSpec((bsz, d), lambda j: (0, 0)),
                  pl.BlockSpec((d, tn), lambda j: (0, j)),
                  pl.BlockSpec((1, tn), lambda j: (0, j))],
        out_specs=pl.BlockSpec((bsz, tn), lambda j: (0, j)),
        out_shape=jax.ShapeDtypeStruct((bsz, n), F32),
        compiler_params=_cparams("arbitrary"),
        name="ada_mod",
    )(c, w, b.reshape(1, n))


def _nmm_kernel(x_ref, nw_ref, sh_ref, sc_ref, w_ref, o_ref, h_ref):
    @pl.when(pl.program_id(2) == 0)
    def _():
        h_ref[...] = _modnorm(x_ref[...], nw_ref[...], sh_ref[...], sc_ref[...]).astype(BF16)

    o_ref[...] = jnp.dot(h_ref[...], w_ref[...], preferred_element_type=F32).astype(o_ref.dtype)


def norm_mod_matmul(x, nw, sh, sc, w, *, ts, tn, name):
    bsz, s, d = x.shape
    n = w.shape[1]
    return pl.pallas_call(
        _nmm_kernel,
        grid=(bsz, s // ts, n // tn),
        in_specs=[pl.BlockSpec((None, ts, d), lambda b, i, j: (b, i, 0)),
                  pl.BlockSpec((1, d), lambda b, i, j: (0, 0)),
                  pl.BlockSpec((None, 1, d), lambda b, i, j: (b, 0, 0)),
                  pl.BlockSpec((None, 1, d), lambda b, i, j: (b, 0, 0)),
                  pl.BlockSpec((d, tn), lambda b, i, j: (0, j))],
        out_specs=pl.BlockSpec((None, ts, tn), lambda b, i, j: (b, i, j)),
        out_shape=jax.ShapeDtypeStruct((bsz, s, n), F32),
        scratch_shapes=[pltpu.VMEM((ts, d), BF16)],
        compiler_params=_cparams("parallel", "parallel", "arbitrary"),
        name=name,
    )(x, nw.reshape(1, d), sh, sc, w)


def _dft_cos_sin(n):
    k = np.arange(n, dtype=np.int64)
    ang = ((k[:, None] * k[None, :]) % n).astype(np.float64) * (2.0 * np.pi / n)
    scale = 1.0 / np.sqrt(n)
    return np.cos(ang) * scale, np.sin(ang) * scale


def _fourier_kernel(u_ref, chan_ref, seq_ref, o_ref, ab_ref):
    s = u_ref.shape[0]

    @pl.when(pl.program_id(1) == 0)
    def _():
        for g in range(FNET_GROUPS):
            lo, hi = g * FNET_GROUP_DIM, (g + 1) * FNET_GROUP_DIM
            ug = u_ref[:, lo:hi].astype(BF16)
            cs = jnp.dot(ug, chan_ref[...], preferred_element_type=F32)
            ab_ref[0:s, lo:hi] = cs[:, :FNET_GROUP_DIM].astype(BF16)
            ab_ref[s:2 * s, lo:hi] = cs[:, FNET_GROUP_DIM:].astype(BF16)

    o_ref[...] = jnp.dot(seq_ref[...], ab_ref[...], preferred_element_type=F32)


def fourier_mix(proj, *, tq):
    bsz, s, _ = proj.shape
    cc, sc = _dft_cos_sin(FNET_GROUP_DIM)
    chan = jnp.asarray(np.concatenate([cc, sc], axis=1), BF16)
    cs, ss = _dft_cos_sin(s)
    seq = jnp.asarray(np.concatenate([cs, -ss], axis=1), BF16)
    return pl.pallas_call(
        _fourier_kernel,
        grid=(bsz, s // tq),
        in_specs=[pl.BlockSpec((None, s, FNET_WIDTH), lambda b, i: (b, 0, 0)),
                  pl.BlockSpec((FNET_GROUP_DIM, 2 * FNET_GROUP_DIM), lambda b, i: (0, 0)),
                  pl.BlockSpec((tq, 2 * s), lambda b, i: (i, 0))],
        out_specs=pl.BlockSpec((None, tq, FNET_WIDTH), lambda b, i: (b, i, 0)),
        out_shape=jax.ShapeDtypeStruct((bsz, s, FNET_WIDTH), F32),
        scratch_shapes=[pltpu.VMEM((2 * s, FNET_WIDTH), BF16)],
        compiler_params=_cparams("parallel", "arbitrary"),
        name="fourier_mix",
    )(proj, chan, seq)


def _band_bucket_table():
    i = np.arange(BLOCK)[:, None]
    j = np.arange(3 * BLOCK)[None, :]
    rel = (j - BLOCK) - i
    half = REL_BUCKETS // 2
    max_exact = half // 2
    n = np.abs(rel)
    large = max_exact + (np.log(np.maximum(n, 1) / max_exact)
                         / np.log(REL_MAX_DIST / max_exact) * (half - max_exact)).astype(np.int32)
    large = np.minimum(large, half - 1)
    bucket = (rel > 0).astype(np.int32) * half + np.where(n < max_exact, n, large)
    return np.where(n <= WINDOW, bucket, -1).astype(np.int32)


def _bias_kernel(rb_ref, bucket_ref, o_ref):
    h = pl.program_id(0)
    bucket = bucket_ref[...]
    acc = jnp.full(bucket.shape, NEG_BIG, F32)
    for bkt in range(REL_BUCKETS):
        acc = jnp.where(bucket == bkt, rb_ref[bkt * ATTN_HEADS + h], acc)
    o_ref[...] = acc


def band_bias(rel_bias):
    bucket = jnp.asarray(_band_bucket_table())
    return pl.pallas_call(
        _bias_kernel,
        grid=(ATTN_HEADS,),
        in_specs=[pl.BlockSpec(memory_space=pltpu.SMEM),
                  pl.BlockSpec((BLOCK, 3 * BLOCK), lambda h: (0, 0))],
        out_specs=pl.BlockSpec((None, BLOCK, 3 * BLOCK), lambda h: (h, 0, 0)),
        out_shape=jax.ShapeDtypeStruct((ATTN_HEADS, BLOCK, 3 * BLOCK), F32),
        compiler_params=_cparams("arbitrary"),
        name="band_bias",
    )(rel_bias.reshape(-1), bucket)


def _head_rms(t, w):
    ms = jnp.mean(t * t, axis=-1, keepdims=True)
    return t * lax.rsqrt(ms + EPS) * w


def _attn_kernel(sink_ref, q_ref, kl_ref, kc_ref, kr_ref, vl_ref, vc_ref, vr_ref,
                 bias_ref, qnw_ref, knw_ref, o_ref):
    n = pl.program_id(1)
    nb = pl.num_programs(1)
    k = jnp.concatenate([kl_ref[...], kc_ref[...], kr_ref[...]], axis=0)
    v = jnp.concatenate([vl_ref[...], vc_ref[...], vr_ref[...]], axis=0)
    col = lax.broadcasted_iota(jnp.int32, (1, 3 * BLOCK), 1)
    first_key = jnp.where(n == 0, BLOCK, 0)
    end_key = jnp.where(n == nb - 1, 2 * BLOCK, 3 * BLOCK)
    outside = (col < first_key) | (col >= end_key)
    q = q_ref[...]
    qnw = qnw_ref[...]
    knw = knw_ref[...]
    g = ATTN_HEADS // ATTN_KV_HEADS
    outs = []
    for j in range(ATTN_KV_HEADS):
        kj = _head_rms(k[:, j * HEAD_DIM:(j + 1) * HEAD_DIM], knw).astype(BF16)
        vj = v[:, j * HEAD_DIM:(j + 1) * HEAD_DIM].astype(BF16)
        for gi in range(g):
            h = j * g + gi
            qh = _head_rms(q[:, h * HEAD_DIM:(h + 1) * HEAD_DIM], qnw).astype(BF16)
            logits = lax.dot_general(qh, kj, (((1,), (1,)), ((), ())), preferred_element_type=F32)
            logits = logits * (HEAD_DIM ** -0.5) + bias_ref[h]
            logits = jnp.where(outside, NEG_BIG, logits)
            sk = sink_ref[h]
            m = jnp.maximum(jnp.max(logits, axis=-1, keepdims=True), sk)
            p = jnp.exp(logits - m)
            denom = jnp.sum(p, axis=-1, keepdims=True) + jnp.exp(sk - m)
            pv = jnp.dot(p.astype(BF16), vj, preferred_element_type=F32)
            outs.append(pv / denom)
    o_ref[...] = jnp.concatenate(outs, axis=-1)


def window_attention(proj, bias, q_norm_w, k_norm_w, sink):
    bsz, s, _ = proj.shape
    nb = s // BLOCK
    qcol = FNET_WIDTH // ATTN_WIDTH
    kcol = (FNET_WIDTH + ATTN_WIDTH) // KV_WIDTH
    vcol = kcol + 1

    def kv_spec(col, shift):
        return pl.BlockSpec((None, BLOCK, KV_WIDTH),
                            lambda b, n: (b, jnp.clip(n + shift, 0, nb - 1), col))

    return pl.pallas_call(
        _attn_kernel,
        grid=(bsz, nb),
        in_specs=[pl.BlockSpec(memory_space=pltpu.SMEM),
                  pl.BlockSpec((None, BLOCK, ATTN_WIDTH), lambda b, n: (b, n, qcol)),
                  kv_spec(kcol, -1), kv_spec(kcol, 0), kv_spec(kcol, 1),
                  kv_spec(vcol, -1), kv_spec(vcol, 0), kv_spec(vcol, 1),
                  pl.BlockSpec((ATTN_HEADS, BLOCK, 3 * BLOCK), lambda b, n: (0, 0, 0)),
                  pl.BlockSpec((1, HEAD_DIM), lambda b, n: (0, 0)),
                  pl.BlockSpec((1, HEAD_DIM), lambda b, n: (0, 0))],
        out_specs=pl.BlockSpec((None, BLOCK, ATTN_WIDTH), lambda b, n: (b, n, 0)),
        out_shape=jax.ShapeDtypeStruct((bsz, s, ATTN_WIDTH), F32),
        compiler_params=_cparams("parallel", "arbitrary"),
        name="window_attention",
    )(sink, proj, proj, proj, proj, proj, proj, proj, bias,
      q_norm_w.reshape(1, HEAD_DIM), k_norm_w.reshape(1, HEAD_DIM))


def _cat_proj_kernel(a1_ref, a2_ref, w_ref, x_ref, g_ref, o_ref):
    k1 = a1_ref.shape[1]
    y = jnp.dot(a1_ref[...].astype(BF16), w_ref[0:k1, :], preferred_element_type=F32)
    y = y + jnp.dot(a2_ref[...].astype(BF16), w_ref[k1:, :], preferred_element_type=F32)
    o_ref[...] = x_ref[...] + g_ref[...] * y


def cat_proj_residual(a1, a2, w, x, g, *, ts):
    bsz, s, d = x.shape
    k1, k2 = a1.shape[2], a2.shape[2]
    return pl.pallas_call(
        _cat_proj_kernel,
        grid=(bsz, s // ts),
        in_specs=[pl.BlockSpec((None, ts, k1), lambda b, i: (b, i, 0)),
                  pl.BlockSpec((None, ts, k2), lambda b, i: (b, i, 0)),
                  pl.BlockSpec((k1 + k2, d), lambda b, i: (0, 0)),
                  pl.BlockSpec((None, ts, d), lambda b, i: (b, i, 0)),
                  pl.BlockSpec((None, 1, d), lambda b, i: (b, 0, 0))],
        out_specs=pl.BlockSpec((None, ts, d), lambda b, i: (b, i, 0)),
        out_shape=jax.ShapeDtypeStruct((bsz, s, d), F32),
        compiler_params=_cparams("parallel", "parallel"),
        name="mixer_out_proj",
    )(a1, a2, w, x, g)


def _ffn_kernel(x_ref, nw_ref, sh_ref, sc_ref, g_ref, w1_ref, w3_ref, w2_ref, o_ref, h_ref, acc_ref):
    f = pl.program_id(2)

    @pl.when(f == 0)
    def _():
        h_ref[...] = _modnorm(x_ref[...], nw_ref[...], sh_ref[...], sc_ref[...]).astype(BF16)

    h = h_ref[...]
    a = jnp.dot(h, w1_ref[...], preferred_element_type=F32)
    b = jnp.dot(h, w3_ref[...], preferred_element_type=F32)
    t = (_silu(a) * b).astype(BF16)
    contrib = jnp.dot(t, w2_ref[...], preferred_element_type=F32)

    @pl.when(f == 0)
    def _():
        acc_ref[...] = contrib

    @pl.when(f > 0)
    def _():
        acc_ref[...] += contrib

    @pl.when(f == pl.num_programs(2) - 1)
    def _():
        o_ref[...] = x_ref[...] + g_ref[...] * acc_ref[...]


def ffn_residual(x, nw, sh, sc, g, w1, w3, w2, *, ts, tf):
    bsz, s, d = x.shape
    dff = w1.shape[1]
    vec = pl.BlockSpec((None, 1, d), lambda b, i, f: (b, 0, 0))
    return pl.pallas_call(
        _ffn_kernel,
        grid=(bsz, s // ts, dff // tf),
        in_specs=[pl.BlockSpec((None, ts, d), lambda b, i, f: (b, i, 0)),
                  pl.BlockSpec((1, d), lambda b, i, f: (0, 0)),
                  vec, vec, vec,
                  pl.BlockSpec((d, tf), lambda b, i, f: (0, f)),
                  pl.BlockSpec((d, tf), lambda b, i, f: (0, f)),
                  pl.BlockSpec((tf, d), lambda b, i, f: (f, 0))],
        out_specs=pl.BlockSpec((None, ts, d), lambda b, i, f: (b, i, 0)),
        out_shape=jax.ShapeDtypeStruct((bsz, s, d), F32),
        scratch_shapes=[pltpu.VMEM((ts, d), BF16), pltpu.VMEM((ts, d), F32)],
        compiler_params=_cparams("parallel", "parallel", "arbitrary"),
        name="ffn_swiglu",
    )(x, nw.reshape(1, d), sh, sc, g, w1, w3, w2)


CONV_PAD = 8
CONV_ROWS = 256


def _conv_kernel(x_ref, w_ref, b_ref, o_ref, pad_ref):
    s, tc = x_ref.shape
    zeros = jnp.zeros((CONV_PAD, tc), F32)
    pad_ref[0:CONV_PAD, :] = zeros
    pad_ref[CONV_PAD + s:, :] = zeros
    pad_ref[CONV_PAD:CONV_PAD + s, :] = x_ref[...]
    half = CONV_WIDTH // 2
    for r in range(s // CONV_ROWS):
        base = CONV_PAD + r * CONV_ROWS - half
        acc = jnp.zeros((CONV_ROWS, tc), F32) + b_ref[...]
        for kk in range(CONV_WIDTH):
            acc = acc + pad_ref[base + kk:base + kk + CONV_ROWS, :] * w_ref[kk:kk + 1, :]
        o_ref[r * CONV_ROWS:(r + 1) * CONV_ROWS, :] = _silu(acc)


def conv_silu(proj, conv_w, conv_b, *, col0, tc):
    bsz, s, _ = proj.shape
    cdim = conv_w.shape[1]
    cb0 = col0 // tc
    return pl.pallas_call(
        _conv_kernel,
        grid=(bsz, cdim // tc),
        in_specs=[pl.BlockSpec((None, s, tc), lambda b, j: (b, 0, cb0 + j)),
                  pl.BlockSpec((CONV_WIDTH, tc), lambda b, j: (0, j)),
                  pl.BlockSpec((1, tc), lambda b, j: (0, j))],
        out_specs=pl.BlockSpec((None, s, tc), lambda b, j: (b, 0, j)),
        out_shape=jax.ShapeDtypeStruct((bsz, s, cdim), F32),
        scratch_shapes=[pltpu.VMEM((s + 2 * CONV_PAD, tc), F32)],
        compiler_params=_cparams("parallel", "parallel"),
        name="conv_silu",
    )(proj, conv_w, conv_b.reshape(1, cdim))


def _softplus(x):
    return jnp.maximum(x, 0.0) + jnp.log(1.0 + jnp.exp(-jnp.abs(x)))


def _expand_matrix(n_in, width):
    e = np.zeros((n_in, n_in * width), np.float32)
    for h in range(n_in):
        e[h, h * width:(h + 1) * width] = 1.0
    return e


def _ssd_kernel(x_ref, b_ref, c_ref, dtc_f_ref, dtc_b_ref, dtr_f_ref, dtr_b_ref,
                pc_f_ref, pc_b_ref, pr_f_ref, pr_b_ref, dx_ref,
                tri_ref, trit_ref, e128_ref, e64_ref, y_ref,
                ac_ref, dc_ref, ar_ref, dr_ref, hf_ref, hb_ref):
    s = x_ref.shape[0]
    q = SSD_CHUNK
    nh = pc_f_ref.shape[1]
    nc = s // q

    def col_params(raw_ref, p_ref):
        dt = _softplus(raw_ref[...] + p_ref[0:1, :])
        return dt, -jnp.exp(p_ref[1:2, :]) * dt

    def row_params(raw_ref, p_ref):
        dt = _softplus(raw_ref[...] + p_ref[:, 0:1])
        return dt, -jnp.exp(p_ref[:, 1:2]) * dt

    dcf, acf = col_params(dtc_f_ref, pc_f_ref)
    dcb, acb = col_params(dtc_b_ref, pc_b_ref)
    dc_ref[:, 0:nh] = dcf
    dc_ref[:, nh:] = dcb
    ac_ref[:, 0:nh] = acf
    ac_ref[:, nh:] = acb
    drf, arf = row_params(dtr_f_ref, pr_f_ref)
    drb, arb = row_params(dtr_b_ref, pr_b_ref)
    dr_ref[0:nh, :] = drf
    dr_ref[nh:, :] = drb
    ar_ref[0:nh, :] = arf
    ar_ref[nh:, :] = arb

    li = lax.broadcasted_iota(jnp.int32, (q, q), 0)
    si = lax.broadcasted_iota(jnp.int32, (q, q), 1)
    lower = li >= si
    upper = li <= si

    def chunk_cumsums(sl):
        a_col = ac_ref[sl, :]
        i_col = jnp.dot(tri_ref[...], a_col, precision=HIGHEST, preferred_element_type=F32)
        return a_col, i_col

    def expand64(v):
        return jnp.dot(v, e64_ref[...], precision=HIGHEST, preferred_element_type=F32)

    hf_ref[...] = jnp.zeros_like(hf_ref)
    hb_ref[...] = jnp.zeros_like(hb_ref)

    def fwd_body(c, carry):
        sl = pl.ds(pl.multiple_of(c * q, q), q)
        xc = x_ref[sl, :]
        bc = b_ref[sl, :].astype(BF16)
        cc = c_ref[sl, :].astype(BF16)
        a_col, i_col = chunk_cumsums(sl)
        d_col = dc_ref[sl, :]
        a_row = ar_ref[:, sl]
        d_row = dr_ref[:, sl]
        i_row = jnp.dot(a_row, trit_ref[...], precision=HIGHEST, preferred_element_type=F32)
        if_col = i_col[:, 0:nh]
        eb_col = i_col[:, nh:] - a_col[:, nh:]
        if_row = i_row[0:nh, :]
        eb_row = i_row[nh:, :] - a_row[nh:, :]
        tot_f = i_col[q - 1:q, 0:nh]

        cb = lax.dot_general(cc, bc, (((1,), (1,)), ((), ())), preferred_element_type=F32)
        colx = jnp.dot(jnp.concatenate([if_col, eb_col], axis=1), e128_ref[...],
                       precision=HIGHEST, preferred_element_type=F32)
        ys = []
        for h in range(nh):
            ifc = colx[:, h * q:(h + 1) * q]
            ebc = colx[:, (nh + h) * q:(nh + h + 1) * q]
            arg = jnp.where(lower, ifc - if_row[h:h + 1, :], eb_row[h:h + 1, :] - ebc)
            wgt = jnp.where(lower, d_row[h:h + 1, :], 0.0) + jnp.where(upper, d_row[nh + h:nh + h + 1, :], 0.0)
            m = (cb * jnp.exp(arg) * wgt).astype(BF16)
            xh = xc[:, h * SSM_HEAD_DIM:(h + 1) * SSM_HEAD_DIM].astype(BF16)
            ys.append(jnp.dot(m, xh, preferred_element_type=F32))
        y = jnp.concatenate(ys, axis=1) + dx_ref[...] * xc

        sf = jnp.exp(tot_f - if_col) * d_col[:, 0:nh]
        sfx = expand64(sf)
        eifx = expand64(jnp.exp(if_col))
        states = lax.dot_general(bc, (xc * sfx).astype(BF16), (((0,), (0,)), ((), ())),
                                 preferred_element_type=F32)
        h_prev = hf_ref[...]
        y = y + jnp.dot(cc, h_prev.astype(BF16), preferred_element_type=F32) * eifx
        hf_ref[...] = h_prev * expand64(jnp.exp(tot_f)) + states
        y_ref[sl, :] = y
        return carry

    lax.fori_loop(0, nc, fwd_body, 0)

    def bwd_body(t, carry):
        c = nc - 1 - t
        sl = pl.ds(pl.multiple_of(c * q, q), q)
        xc = x_ref[sl, :]
        bc = b_ref[sl, :].astype(BF16)
        cc = c_ref[sl, :].astype(BF16)
        a_col, i_col = chunk_cumsums(sl)
        d_col = dc_ref[sl, :]
        eb_col = i_col[:, nh:] - a_col[:, nh:]
        tot_b = i_col[q - 1:q, nh:]
        sbx = expand64(jnp.exp(eb_col) * d_col[:, nh:])
        erx = expand64(jnp.exp(tot_b - eb_col))
        states = lax.dot_general(bc, (xc * sbx).astype(BF16), (((0,), (0,)), ((), ())),
                                 preferred_element_type=F32)
        h_prev = hb_ref[...]
        y_ref[sl, :] += jnp.dot(cc, h_prev.astype(BF16), preferred_element_type=F32) * erx
        hb_ref[...] = h_prev * expand64(jnp.exp(tot_b)) + states
        return carry

    lax.fori_loop(0, nc, bwd_body, 0)


def ssd_scan_bidir(xbc, dt, dt_bias_f, dt_bias_b, a_log_f, a_log_b, d_skip):
    bsz, s, _ = xbc.shape
    nheads = dt.shape[2] // 2
    nh = nheads // SSM_GROUPS
    gw = nh * SSM_HEAD_DIM
    d_inner = nheads * SSM_HEAD_DIM
    q = SSD_CHUNK
    dt_g = dt.reshape(bsz, s, 2 * SSM_GROUPS, nh)
    dt_col = jnp.transpose(dt_g, (0, 2, 1, 3))
    dt_row = jnp.transpose(dt_g, (0, 2, 3, 1))
    prm = jnp.stack([jnp.concatenate([dt_bias_f, dt_bias_b]), jnp.concatenate([a_log_f, a_log_b])])
    prm = prm.reshape(2, 2 * SSM_GROUPS, nh)
    p_col = jnp.transpose(prm, (1, 0, 2))
    p_row = jnp.transpose(prm, (1, 2, 0))
    dx = jnp.repeat(d_skip, SSM_HEAD_DIM).reshape(SSM_GROUPS, 1, gw)
    tri = jnp.asarray(np.tril(np.ones((q, q), np.float32)))
    trit = jnp.asarray(np.triu(np.ones((q, q), np.float32)))
    e128 = jnp.asarray(_expand_matrix(2 * nh, q))
    e64 = jnp.asarray(_expand_matrix(nh, SSM_HEAD_DIM))
    bcol = d_inner // D_STATE
    G = SSM_GROUPS

    const = lambda shape: pl.BlockSpec(shape, lambda b, g: tuple(0 for _ in shape))
    return pl.pallas_call(
        _ssd_kernel,
        grid=(bsz, SSM_GROUPS),
        in_specs=[pl.BlockSpec((None, s, gw), lambda b, g: (b, 0, g)),
                  pl.BlockSpec((None, s, D_STATE), lambda b, g: (b, 0, bcol + g)),
                  pl.BlockSpec((None, s, D_STATE), lambda b, g: (b, 0, bcol + G + g)),
                  pl.BlockSpec((None, None, s, nh), lambda b, g: (b, g, 0, 0)),
                  pl.BlockSpec((None, None, s, nh), lambda b, g: (b, G + g, 0, 0)),
                  pl.BlockSpec((None, None, nh, s), lambda b, g: (b, g, 0, 0)),
                  pl.BlockSpec((None, None, nh, s), lambda b, g: (b, G + g, 0, 0)),
                  pl.BlockSpec((None, 2, nh), lambda b, g: (g, 0, 0)),
                  pl.BlockSpec((None, 2, nh), lambda b, g: (G + g, 0, 0)),
                  pl.BlockSpec((None, nh, 2), lambda b, g: (g, 0, 0)),
                  pl.BlockSpec((None, nh, 2), lambda b, g: (G + g, 0, 0)),
                  pl.BlockSpec((None, 1, gw), lambda b, g: (g, 0, 0)),
                  const((q, q)), const((q, q)), const((2 * nh, 2 * nh * q)), const((nh, gw))],
        out_specs=pl.BlockSpec((None, s, gw), lambda b, g: (b, 0, g)),
        out_shape=jax.ShapeDtypeStruct((bsz, s, d_inner), F32),
        scratch_shapes=[pltpu.VMEM((s, 2 * nh), F32), pltpu.VMEM((s, 2 * nh), F32),
                        pltpu.VMEM((2 * nh, s), F32), pltpu.VMEM((2 * nh, s), F32),
                        pltpu.VMEM((D_STATE, gw), F32), pltpu.VMEM((D_STATE, gw), F32)],
        compiler_params=_cparams("parallel", "parallel"),
        name="ssd_scan",
    )(xbc, xbc, xbc, dt_col, dt_col, dt_row, dt_row, p_col, p_col, p_row, p_row, dx, tri, trit, e128, e64)


def _gated_proj_kernel(y_ref, z_ref, gw_ref, w_ref, x_ref, g_ref, o_ref):
    t = y_ref[...] * _silu(z_ref[...])
    ms = jnp.mean(t * t, axis=-1, keepdims=True)
    t = (t * lax.rsqrt(ms + EPS) * gw_ref[...]).astype(BF16)
    o_ref[...] = x_ref[...] + g_ref[...] * jnp.dot(t, w_ref[...], preferred_element_type=F32)


def gated_proj_residual(y, zproj, gw, w, x, g, *, ts):
    bsz, s, d = x.shape
    k = y.shape[2]
    return pl.pallas_call(
        _gated_proj_kernel,
        grid=(bsz, s // ts),
        in_specs=[pl.BlockSpec((None, ts, k), lambda b, i: (b, i, 0)),
                  pl.BlockSpec((None, ts, k), lambda b, i: (b, i, 0)),
                  pl.BlockSpec((1, k), lambda b, i: (0, 0)),
                  pl.BlockSpec((k, d), lambda b, i: (0, 0)),
                  pl.BlockSpec((None, ts, d), lambda b, i: (b, i, 0)),
                  pl.BlockSpec((None, 1, d), lambda b, i: (b, 0, 0))],
        out_specs=pl.BlockSpec((None, ts, d), lambda b, i: (b, i, 0)),
        out_shape=jax.ShapeDtypeStruct((bsz, s, d), F32),
        compiler_params=_cparams("parallel", "parallel"),
        name="ssd_out_proj",
    )(y, zproj, gw.reshape(1, k), w, x, g)


def _router_kernel(x_ref, nw_ref, sh_ref, sc_ref, rw_ref, rb_ref, lt_ref,
                   eidx_ref, rank_ref, wgt_ref, cnt_ref, run_ref):
    first = (pl.program_id(0) == 0) & (pl.program_id(1) == 0)

    @pl.when(first)
    def _():
        run_ref[...] = jnp.zeros_like(run_ref)

    h = _modnorm(x_ref[...], nw_ref[...], sh_ref[...], sc_ref[...])
    logits = jnp.dot(h, rw_ref[...], precision=HIGHEST, preferred_element_type=F32) + rb_ref[...]
    ts, ne = logits.shape
    eid = lax.broadcasted_iota(jnp.int32, (ts, ne), 1)
    m1 = jnp.max(logits, axis=-1, keepdims=True)
    i1 = jnp.min(jnp.where(logits == m1, eid, ne), axis=-1, keepdims=True)
    rest = jnp.where(eid == i1, -jnp.inf, logits)
    m2 = jnp.max(rest, axis=-1, keepdims=True)
    i2 = jnp.min(jnp.where(rest == m2, eid, ne), axis=-1, keepdims=True)
    e2 = jnp.exp(m2 - m1)
    w1 = 1.0 / (1.0 + e2)
    w2 = e2 / (1.0 + e2)
    oh1 = (eid == i1).astype(F32)
    oh2 = (eid == i2).astype(F32)
    chosen = oh1 + oh2
    incl = jnp.dot(lt_ref[...], chosen.astype(BF16), preferred_element_type=F32)
    before = run_ref[...] + incl - chosen
    r1 = jnp.sum(oh1 * before, axis=-1, keepdims=True)
    r2 = jnp.sum(oh2 * before, axis=-1, keepdims=True)
    run_ref[...] = run_ref[...] + incl[ts - 1:ts, :]
    eidx_ref[...] = jnp.concatenate([i1, i2], axis=1)
    rank_ref[...] = jnp.concatenate([r1, r2], axis=1).astype(jnp.int32)
    wgt_ref[...] = jnp.concatenate([w1, w2], axis=1)
    cnt_ref[...] = run_ref[...].astype(jnp.int32)


def moe_route(x, nw, sh, sc, router_w, router_b, *, ts):
    bsz, s, d = x.shape
    ne = router_w.shape[1]
    nt = s // ts
    lt = jnp.asarray(np.tril(np.ones((ts, ts), np.float32)), BF16)
    tok = lambda dt: jax.ShapeDtypeStruct((bsz * s, TOP_K), dt)
    tok_spec = pl.BlockSpec((ts, TOP_K), lambda b, i: (b * nt + i, 0))
    return pl.pallas_call(
        _router_kernel,
        grid=(bsz, nt),
        in_specs=[pl.BlockSpec((None, ts, d), lambda b, i: (b, i, 0)),
                  pl.BlockSpec((1, d), lambda b, i: (0, 0)),
                  pl.BlockSpec((None, 1, d), lambda b, i: (b, 0, 0)),
                  pl.BlockSpec((None, 1, d), lambda b, i: (b, 0, 0)),
                  pl.BlockSpec((d, ne), lambda b, i: (0, 0)),
                  pl.BlockSpec((1, ne), lambda b, i: (0, 0)),
                  pl.BlockSpec((ts, ts), lambda b, i: (0, 0))],
        out_specs=[tok_spec, tok_spec, tok_spec, pl.BlockSpec((1, ne), lambda b, i: (0, 0))],
        out_shape=[tok(jnp.int32), tok(jnp.int32), tok(F32), jax.ShapeDtypeStruct((1, ne), jnp.int32)],
        scratch_shapes=[pltpu.VMEM((1, ne), F32)],
        compiler_params=_cparams("arbitrary", "arbitrary"),
        name="moe_router",
    )(x, nw.reshape(1, d), sh, sc, router_w, router_b.reshape(1, ne), lt)


def _dispatch_kernel(dest_ref, x_ref, nw_ref, sh_ref, sc_ref, hs_in_ref, hs_ref, h_ref, sem):
    del hs_in_ref
    ts = x_ref.shape[0]
    tile = pl.program_id(0) * pl.num_programs(1) + pl.program_id(1)
    h_ref[...] = _modnorm(x_ref[...], nw_ref[...], sh_ref[...], sc_ref[...])

    def row_copy(r, kk):
        dst = dest_ref[(tile * ts + r) * TOP_K + kk]
        return pltpu.make_async_copy(h_ref.at[pl.ds(r, 1), :], hs_ref.at[pl.ds(dst, 1), :], sem)

    def issue(r, carry):
        for kk in range(TOP_K):
            row_copy(r, kk).start()
        return carry

    lax.fori_loop(0, ts, issue, 0)

    def drain(r, carry):
        for kk in range(TOP_K):
            row_copy(r, kk).wait()
        return carry

    lax.fori_loop(0, ts, drain, 0)


def moe_dispatch(x, nw, sh, sc, dest, n_rows, *, ts):
    bsz, s, d = x.shape
    nt = s // ts
    hs0 = jnp.zeros((n_rows, d), F32)
    grid_spec = pltpu.PrefetchScalarGridSpec(
        num_scalar_prefetch=1,
        grid=(bsz, nt),
        in_specs=[pl.BlockSpec((None, ts, d), lambda b, i, dref: (b, i, 0)),
                  pl.BlockSpec((1, d), lambda b, i, dref: (0, 0)),
                  pl.BlockSpec((None, 1, d), lambda b, i, dref: (b, 0, 0)),
                  pl.BlockSpec((None, 1, d), lambda b, i, dref: (b, 0, 0)),
                  pl.BlockSpec(memory_space=pl.ANY)],
        out_specs=pl.BlockSpec(memory_space=pl.ANY),
        scratch_shapes=[pltpu.VMEM((ts, d), F32), pltpu.SemaphoreType.DMA(())],
    )
    return pl.pallas_call(
        _dispatch_kernel,
        grid_spec=grid_spec,
        out_shape=jax.ShapeDtypeStruct((n_rows, d), F32),
        input_output_aliases={5: 0},
        compiler_params=_cparams("arbitrary", "arbitrary"),
        name="moe_dispatch",
    )(dest, x, nw.reshape(1, d), sh, sc, hs0)


def _moe_kernel(te_ref, nu_ref, hs_ref, w1_ref, w3_ref, w2_ref, o_ref, hb_ref, acc_ref):
    i = pl.program_id(0)
    f = pl.program_id(1)

    @pl.when(i < nu_ref[0])
    def _():
        @pl.when(f == 0)
        def _():
            hb_ref[...] = hs_ref[...].astype(BF16)

        h = hb_ref[...]
        a = jnp.dot(h, w1_ref[...], preferred_element_type=F32)
        b = jnp.dot(h, w3_ref[...], preferred_element_type=F32)
        t = (_silu(a) * b).astype(BF16)
        contrib = jnp.dot(t, w2_ref[...], preferred_element_type=F32)

        @pl.when(f == 0)
        def _():
            acc_ref[...] = contrib

        @pl.when(f > 0)
        def _():
            acc_ref[...] += contrib

        @pl.when(f == pl.num_programs(1) - 1)
        def _():
            o_ref[...] = acc_ref[...]

    @pl.when((i >= nu_ref[0]) & (f == 0))
    def _():
        o_ref[...] = jnp.zeros_like(o_ref)


def moe_experts(hs, tile_expert, n_used, w1, w3, w2, *, tm, tf):
    n_rows, d = hs.shape
    dff = w1.shape[2]
    nf = dff // tf
    n_tiles = n_rows // tm

    def row_map(i, f, te, nu):
        return (jnp.minimum(i, nu[0] - 1), 0)

    def w_in_map(i, f, te, nu):
        return (te[jnp.minimum(i, nu[0] - 1)], 0, jnp.where(i < nu[0], f, nf - 1))

    def w_out_map(i, f, te, nu):
        return (te[jnp.minimum(i, nu[0] - 1)], jnp.where(i < nu[0], f, nf - 1), 0)

    grid_spec = pltpu.PrefetchScalarGridSpec(
        num_scalar_prefetch=2,
        grid=(n_tiles, nf),
        in_specs=[pl.BlockSpec((tm, d), row_map),
                  pl.BlockSpec((None, d, tf), w_in_map),
                  pl.BlockSpec((None, d, tf), w_in_map),
                  pl.BlockSpec((None, tf, d), w_out_map)],
        out_specs=pl.BlockSpec((tm, d), lambda i, f, te, nu: (i, 0)),
        scratch_shapes=[pltpu.VMEM((tm, d), BF16), pltpu.VMEM((tm, d), F32)],
    )
    return pl.pallas_call(
        _moe_kernel,
        grid_spec=grid_spec,
        out_shape=jax.ShapeDtypeStruct((n_rows, d), F32),
        compiler_params=_cparams("arbitrary", "arbitrary"),
        name="moe_experts",
    )(tile_expert, n_used, hs, w1, w3, w2)


def _combine_kernel(dest_ref, ys_ref, x_ref, g_ref, wgt_ref, o_ref, buf_ref, sem):
    tc = x_ref.shape[0]
    tile = pl.program_id(0) * pl.num_programs(1) + pl.program_id(1)

    def row_copy(r, kk):
        src = dest_ref[(tile * tc + r) * TOP_K + kk]
        return pltpu.make_async_copy(ys_ref.at[pl.ds(src, 1), :], buf_ref.at[kk, pl.ds(r, 1), :], sem)

    def issue(r, carry):
        for kk in range(TOP_K):
            row_copy(r, kk).start()
        return carry

    lax.fori_loop(0, tc, issue, 0)

    def drain(r, carry):
        for kk in range(TOP_K):
            row_copy(r, kk).wait()
        return carry

    lax.fori_loop(0, tc, drain, 0)
    w = wgt_ref[...]
    mix = w[:, 0:1] * buf_ref[0] + w[:, 1:2] * buf_ref[1]
    o_ref[...] = x_ref[...] + g_ref[...] * mix


def moe_combine(ys, dest, wgt, x, g, *, tc):
    bsz, s, d = x.shape
    nt = s // tc
    grid_spec = pltpu.PrefetchScalarGridSpec(
        num_scalar_prefetch=1,
        grid=(bsz, nt),
        in_specs=[pl.BlockSpec(memory_space=pl.ANY),
                  pl.BlockSpec((None, tc, d), lambda b, i, dref: (b, i, 0)),
                  pl.BlockSpec((None, 1, d), lambda b, i, dref: (b, 0, 0)),
                  pl.BlockSpec((tc, TOP_K), lambda b, i, dref: (b * nt + i, 0))],
        out_specs=pl.BlockSpec((None, tc, d), lambda b, i, dref: (b, i, 0)),
        scratch_shapes=[pltpu.VMEM((TOP_K, tc, d), F32), pltpu.SemaphoreType.DMA(())],
    )
    return pl.pallas_call(
        _combine_kernel,
        grid_spec=grid_spec,
        out_shape=jax.ShapeDtypeStruct((bsz, s, d), F32),
        compiler_params=_cparams("arbitrary", "arbitrary"),
        name="moe_combine",
    )(dest, ys, x, g, wgt)


def moe_residual(x, nw, sh, sc, g, router_w, router_b, w1, w3, w2, *, tm=MOE_TILE_ROWS):
    bsz, s, d = x.shape
    n_tok = bsz * s
    ne = router_w.shape[1]
    eidx, rank, wgt, counts = moe_route(x, nw, sh, sc, router_w, router_b, ts=min(512, s))
    counts = counts.reshape(ne)
    padded = ((counts + tm - 1) // tm) * tm
    ends = jnp.cumsum(padded)
    starts = ends - padded
    dest = (starts[eidx] + rank).reshape(-1).astype(jnp.int32)
    n_tiles = (n_tok * TOP_K) // tm + ne
    tile_start = jnp.arange(n_tiles, dtype=jnp.int32) * tm
    tile_expert = jnp.minimum(jnp.sum(tile_start[:, None] >= ends[None, :], axis=1), ne - 1).astype(jnp.int32)
    n_used = (ends[ne - 1:ne] // tm).astype(jnp.int32)
    hs = moe_dispatch(x, nw, sh, sc, dest, n_tiles * tm, ts=min(256, s))
    ys = moe_experts(hs, tile_expert, n_used, w1, w3, w2, tm=tm, tf=w1.shape[2] // 4)
    return moe_combine(ys, dest, wgt, x, g, tc=min(256, s))


def _split_mod(mod):
    return [m[:, None, :] for m in jnp.split(mod, 6, axis=-1)]


def even_layer(x, c, rel_bias, ada_w, ada_b, norm1_w, in_w, q_norm_w, k_norm_w, sink, out_w,
               norm2_w, w1, w3, w2):
    s = x.shape[1]
    sh1, sc1, g1, sh2, sc2, g2 = _split_mod(ada_mod(c, ada_w, ada_b))
    proj = norm_mod_matmul(x, norm1_w, sh1, sc1, in_w.astype(BF16), ts=min(1024, s), tn=640, name="even_in_proj")
    yf = fourier_mix(proj, tq=min(512, s))
    ya = window_attention(proj, band_bias(rel_bias), q_norm_w, k_norm_w, sink)
    x = cat_proj_residual(yf, ya, out_w.astype(BF16), x, g1, ts=min(1024, s))
    dff = w1.shape[1]
    return ffn_residual(x, norm2_w, sh2, sc2, g2, w1.astype(BF16), w3.astype(BF16), w2.astype(BF16),
                        ts=min(1024, s), tf=dff // 2)


def odd_layer(x, c, ada_w, ada_b, norm1_w, in_w, conv_w, conv_b, dt_bias_f, dt_bias_b, a_log_f, a_log_b,
              d_skip, gnorm_w, out_w, norm2_w, router_w, router_b, w1, w3, w2):
    s = x.shape[1]
    sh1, sc1, g1, sh2, sc2, g2 = _split_mod(ada_mod(c, ada_w, ada_b))
    d_inner = gnorm_w.shape[0]
    cdim = conv_w.shape[1]
    wide = d_inner + cdim
    in_w = in_w.astype(BF16)
    zx = norm_mod_matmul(x, norm1_w, sh1, sc1, in_w[:, :wide], ts=min(1024, s), tn=1024, name="odd_in_proj")
    dt = norm_mod_matmul(x, norm1_w, sh1, sc1, in_w[:, wide:], ts=min(1024, s), tn=in_w.shape[1] - wide,
                         name="odd_dt_proj")
    xbc = conv_silu(zx, conv_w, conv_b, col0=d_inner, tc=512)
    y = ssd_scan_bidir(xbc, dt, dt_bias_f, dt_bias_b, a_log_f, a_log_b, d_skip)
    x = gated_proj_residual(y, zx, gnorm_w, out_w.astype(BF16), x, g1, ts=min(512, s))
    return moe_residual(x, norm2_w, sh2, sc2, g2, router_w, router_b,
                        w1.astype(BF16), w3.astype(BF16), w2.astype(BF16))


def kernel(x, c, rel_bias, ev_ada_w, ev_ada_b, ev_norm1_w, ev_in_w, ev_q_norm_w, ev_k_norm_w, ev_sink, ev_out_w, ev_norm2_w, ev_ffn_w1, ev_ffn_w3, ev_ffn_w2, od_ada_w, od_ada_b, od_norm1_w, od_in_w, od_conv_w, od_conv_b, od_dt_bias_f, od_dt_bias_b, od_A_log_f, od_A_log_b, od_D, od_gnorm_w, od_out_w, od_norm2_w, od_router_w, od_router_b, od_moe_w1, od_moe_w3, od_moe_w2):
    depth = ev_ada_w.shape[0] + od_ada_w.shape[0]
    for i in range(depth):
        j = i // 2
        if i % 2 == 0:
            x = even_layer(x, c, rel_bias, ev_ada_w[j], ev_ada_b[j], ev_norm1_w[j], ev_in_w[j],
                           ev_q_norm_w[j], ev_k_norm_w[j], ev_sink[j], ev_out_w[j], ev_norm2_w[j],
                           ev_ffn_w1[j], ev_ffn_w3[j], ev_ffn_w2[j])
        else:
            x = odd_layer(x, c, od_ada_w[j], od_ada_b[j], od_norm1_w[j], od_in_w[j], od_conv_w[j],
                          od_conv_b[j], od_dt_bias_f[j], od_dt_bias_b[j], od_A_log_f[j], od_A_log_b[j],
                          od_D[j], od_gnorm_w[j], od_out_w[j], od_norm2_w[j], od_router_w[j],
                          od_router_b[j], od_moe_w1[j], od_moe_w3[j], od_moe_w2[j])
    return x
```

```python
import functools

import numpy as np
import jax
import jax.numpy as jnp
from jax import lax
from jax.experimental import pallas as pl
from jax.experimental.pallas import tpu as pltpu

F32 = jnp.float32
BF16 = jnp.bfloat16
HIGHEST = lax.Precision.HIGHEST

EPS = 1e-6
FNET_GROUPS = 4
FNET_GROUP_DIM = 128
FNET_WIDTH = FNET_GROUPS * FNET_GROUP_DIM
ATTN_HEADS = 8
ATTN_KV_HEADS = 2
HEAD_DIM = 64
ATTN_WIDTH = ATTN_HEADS * HEAD_DIM
KV_WIDTH = ATTN_KV_HEADS * HEAD_DIM
WINDOW = 128
BLOCK = 128
REL_BUCKETS = 32
REL_MAX_DIST = 128
SSM_HEAD_DIM = 64
SSM_GROUPS = 4
D_STATE = 128
CONV_WIDTH = 5
SSD_CHUNK = 128
N_EXPERTS = 8
TOP_K = 2
NEG_BIG = -1e30

V7X_VMEM_LIMIT_BYTES = 56 * 1024 * 1024
MOE_TILE_ROWS = 512


def _cparams(*sem):
    return pltpu.CompilerParams(dimension_semantics=sem, vmem_limit_bytes=V7X_VMEM_LIMIT_BYTES)


def _modnorm(x, nw, sh, sc):
    ms = jnp.mean(x * x, axis=-1, keepdims=True)
    return x * lax.rsqrt(ms + EPS) * nw * (1.0 + sc) + sh


def _silu(x):
    return x * (1.0 / (1.0 + jnp.exp(-x)))


def _ada_kernel(c_ref, w_ref, b_ref, o_ref):
    cs = _silu(c_ref[...]).astype(BF16)
    o_ref[...] = jnp.dot(cs, w_ref[...].astype(BF16), preferred_element_type=F32) + b_ref[...]


def ada_mod(c, w, b):
    bsz, d = c.shape
    n = w.shape[1]
    tn = 1536
    return pl.pallas_call(
        _ada_kernel,
        grid=(n // tn,),
        in_specs=[pl.BlockSpec((bsz, d), lambda j: (0, 0)),
                  pl.BlockSpec((d, tn), lambda j: (0, j)),
                  pl.BlockSpec((1, tn), lambda j: (0, j))],
        out_specs=pl.BlockSpec((bsz, tn), lambda j: (0, j)),
        out_shape=jax.ShapeDtypeStruct((bsz, n), F32),
        compiler_params=_cparams("arbitrary"),
        name="ada_mod",
    )(c, w, b.reshape(1, n))


def _nmm_kernel(x_ref, nw_ref, sh_ref, sc_ref, w_ref, o_ref, h_ref):
    @pl.when(pl.program_id(2) == 0)
    def _():
        h_ref[...] = _modnorm(x_ref[...], nw_ref[...], sh_ref[...], sc_ref[...]).astype(BF16)

    o_ref[...] = jnp.dot(h_ref[...], w_ref[...], preferred_element_type=F32).astype(o_ref.dtype)


def norm_mod_matmul(x, nw, sh, sc, w, *, ts, tn, name):
    bsz, s, d = x.shape
    n = w.shape[1]
    return pl.pallas_call(
        _nmm_kernel,
        grid=(bsz, s // ts, n // tn),
        in_specs=[pl.BlockSpec((None, ts, d), lambda b, i, j: (b, i, 0)),
                  pl.BlockSpec((1, d), lambda b, i, j: (0, 0)),
                  pl.BlockSpec((None, 1, d), lambda b, i, j: (b, 0, 0)),
                  pl.BlockSpec((None, 1, d), lambda b, i, j: (b, 0, 0)),
                  pl.BlockSpec((d, tn), lambda b, i, j: (0, j))],
        out_specs=pl.BlockSpec((None, ts, tn), lambda b, i, j: (b, i, j)),
        out_shape=jax.ShapeDtypeStruct((bsz, s, n), F32),
        scratch_shapes=[pltpu.VMEM((ts, d), BF16)],
        compiler_params=_cparams("parallel", "parallel", "arbitrary"),
        name=name,
    )(x, nw.reshape(1, d), sh, sc, w)


def _dft_cos_sin(n):
    k = np.arange(n, dtype=np.int64)
    ang = ((k[:, None] * k[None, :]) % n).astype(np.float64) * (2.0 * np.pi / n)
    scale = 1.0 / np.sqrt(n)
    return np.cos(ang) * scale, np.sin(ang) * scale


def _fourier_kernel(u_ref, chan_ref, seq_ref, o_ref, ab_ref):
    s = u_ref.shape[0]

    @pl.when(pl.program_id(1) == 0)
    def _():
        for g in range(FNET_GROUPS):
            lo, hi = g * FNET_GROUP_DIM, (g + 1) * FNET_GROUP_DIM
            ug = u_ref[:, lo:hi].astype(BF16)
            cs = jnp.dot(ug, chan_ref[...], preferred_element_type=F32)
            ab_ref[0:s, lo:hi] = cs[:, :FNET_GROUP_DIM].astype(BF16)
            ab_ref[s:2 * s, lo:hi] = cs[:, FNET_GROUP_DIM:].astype(BF16)

    o_ref[...] = jnp.dot(seq_ref[...], ab_ref[...], preferred_element_type=F32)


def fourier_mix(proj, *, tq):
    bsz, s, _ = proj.shape
    cc, sc = _dft_cos_sin(FNET_GROUP_DIM)
    chan = jnp.asarray(np.concatenate([cc, sc], axis=1), BF16)
    cs, ss = _dft_cos_sin(s)
    seq = jnp.asarray(np.concatenate([cs, -ss], axis=1), BF16)
    return pl.pallas_call(
        _fourier_kernel,
        grid=(bsz, s // tq),
        in_specs=[pl.BlockSpec((None, s, FNET_WIDTH), lambda b, i: (b, 0, 0)),
                  pl.BlockSpec((FNET_GROUP_DIM, 2 * FNET_GROUP_DIM), lambda b, i: (0, 0)),
                  pl.BlockSpec((tq, 2 * s), lambda b, i: (i, 0))],
        out_specs=pl.BlockSpec((None, tq, FNET_WIDTH), lambda b, i: (b, i, 0)),
        out_shape=jax.ShapeDtypeStruct((bsz, s, FNET_WIDTH), F32),
        scratch_shapes=[pltpu.VMEM((2 * s, FNET_WIDTH), BF16)],
        compiler_params=_cparams("parallel", "arbitrary"),
        name="fourier_mix",
    )(proj, chan, seq)


def _band_bucket_table():
    i = np.arange(BLOCK)[:, None]
    j = np.arange(3 * BLOCK)[None, :]
    rel = (j - BLOCK) - i
    half = REL_BUCKETS // 2
    max_exact = half // 2
    n = np.abs(rel)
    large = max_exact + (np.log(np.maximum(n, 1) / max_exact)
                         / np.log(REL_MAX_DIST / max_exact) * (half - max_exact)).astype(np.int32)
    large = np.minimum(large, half - 1)
    bucket = (rel > 0).astype(np.int32) * half + np.where(n < max_exact, n, large)
    return np.where(n <= WINDOW, bucket, -1).astype(np.int32)


def _bias_kernel(rb_ref, bucket_ref, o_ref):
    h = pl.program_id(0)
    bucket = bucket_ref[...]
    acc = jnp.full(bucket.shape, NEG_BIG, F32)
    for bkt in range(REL_BUCKETS):
        acc = jnp.where(bucket == bkt, rb_ref[bkt * ATTN_HEADS + h], acc)
    o_ref[...] = acc


def band_bias(rel_bias):
    bucket = jnp.asarray(_band_bucket_table())
    return pl.pallas_call(
        _bias_kernel,
        grid=(ATTN_HEADS,),
        in_specs=[pl.BlockSpec(memory_space=pltpu.SMEM),
                  pl.BlockSpec((BLOCK, 3 * BLOCK), lambda h: (0, 0))],
        out_specs=pl.BlockSpec((None, BLOCK, 3 * BLOCK), lambda h: (h, 0, 0)),
        out_shape=jax.ShapeDtypeStruct((ATTN_HEADS, BLOCK, 3 * BLOCK), F32),
        compiler_params=_cparams("arbitrary"),
        name="band_bias",
    )(rel_bias.reshape(-1), bucket)


def _head_rms(t, w):
    ms = jnp.mean(t * t, axis=-1, keepdims=True)
    return t * lax.rsqrt(ms + EPS) * w


def _attn_kernel(sink_ref, q_ref, kl_ref, kc_ref, kr_ref, vl_ref, vc_ref, vr_ref,
                 bias_ref, qnw_ref, knw_ref, o_ref):
    n = pl.program_id(1)
    nb = pl.num_programs(1)
    k = jnp.concatenate([kl_ref[...], kc_ref[...], kr_ref[...]], axis=0)
    v = jnp.concatenate([vl_ref[...], vc_ref[...], vr_ref[...]], axis=0)
    col = lax.broadcasted_iota(jnp.int32, (1, 3 * BLOCK), 1)
    first_key = jnp.where(n == 0, BLOCK, 0)
    end_key = jnp.where(n == nb - 1, 2 * BLOCK, 3 * BLOCK)
    outside = (col < first_key) | (col >= end_key)
    q = q_ref[...]
    qnw = qnw_ref[...]
    knw = knw_ref[...]
    g = ATTN_HEADS // ATTN_KV_HEADS
    outs = []
    for j in range(ATTN_KV_HEADS):
        kj = _head_rms(k[:, j * HEAD_DIM:(j + 1) * HEAD_DIM], knw).astype(BF16)
        vj = v[:, j * HEAD_DIM:(j + 1) * HEAD_DIM].astype(BF16)
        for gi in range(g):
            h = j * g + gi
            qh = _head_rms(q[:, h * HEAD_DIM:(h + 1) * HEAD_DIM], qnw).astype(BF16)
            logits = lax.dot_general(qh, kj, (((1,), (1,)), ((), ())), preferred_element_type=F32)
            logits = logits * (HEAD_DIM ** -0.5) + bias_ref[h]
            logits = jnp.where(outside, NEG_BIG, logits)
            sk = sink_ref[h]
            m = jnp.maximum(jnp.max(logits, axis=-1, keepdims=True), sk)
            p = jnp.exp(logits - m)
            denom = jnp.sum(p, axis=-1, keepdims=True) + jnp.exp(sk - m)
            pv = jnp.dot(p.astype(BF16), vj, preferred_element_type=F32)
            outs.append(pv / denom)
    o_ref[...] = jnp.concatenate(outs, axis=-1)


def window_attention(proj, bias, q_norm_w, k_norm_w, sink):
    bsz, s, _ = proj.shape
    nb = s // BLOCK
    qcol = FNET_WIDTH // ATTN_WIDTH
    kcol = (FNET_WIDTH + ATTN_WIDTH) // KV_WIDTH
    vcol = kcol + 1

    def kv_spec(col, shift):
        return pl.BlockSpec((None, BLOCK, KV_WIDTH),
                            lambda b, n: (b, jnp.clip(n + shift, 0, nb - 1), col))

    return pl.pallas_call(
        _attn_kernel,
        grid=(bsz, nb),
        in_specs=[pl.BlockSpec(memory_space=pltpu.SMEM),
                  pl.BlockSpec((None, BLOCK, ATTN_WIDTH), lambda b, n: (b, n, qcol)),
                  kv_spec(kcol, -1), kv_spec(kcol, 0), kv_spec(kcol, 1),
                  kv_spec(vcol, -1), kv_spec(vcol, 0), kv_spec(vcol, 1),
                  pl.BlockSpec((ATTN_HEADS, BLOCK, 3 * BLOCK), lambda b, n: (0, 0, 0)),
                  pl.BlockSpec((1, HEAD_DIM), lambda b, n: (0, 0)),
                  pl.BlockSpec((1, HEAD_DIM), lambda b, n: (0, 0))],
        out_specs=pl.BlockSpec((None, BLOCK, ATTN_WIDTH), lambda b, n: (b, n, 0)),
        out_shape=jax.ShapeDtypeStruct((bsz, s, ATTN_WIDTH), F32),
        compiler_params=_cparams("parallel", "arbitrary"),
        name="window_attention",
    )(sink, proj, proj, proj, proj, proj, proj, proj, bias,
      q_norm_w.reshape(1, HEAD_DIM), k_norm_w.reshape(1, HEAD_DIM))


def _cat_proj_kernel(a1_ref, a2_ref, w_ref, x_ref, g_ref, o_ref):
    k1 = a1_ref.shape[1]
    y = jnp.dot(a1_ref[...].astype(BF16), w_ref[0:k1, :], preferred_element_type=F32)
    y = y + jnp.dot(a2_ref[...].astype(BF16), w_ref[k1:, :], preferred_element_type=F32)
    o_ref[...] = x_ref[...] + g_ref[...] * y


def cat_proj_residual(a1, a2, w, x, g, *, ts):
    bsz, s, d = x.shape
    k1, k2 = a1.shape[2], a2.shape[2]
    return pl.pallas_call(
        _cat_proj_kernel,
        grid=(bsz, s // ts),
        in_specs=[pl.BlockSpec((None, ts, k1), lambda b, i: (b, i, 0)),
                  pl.BlockSpec((None, ts, k2), lambda b, i: (b, i, 0)),
                  pl.BlockSpec((k1 + k2, d), lambda b, i: (0, 0)),
                  pl.BlockSpec((None, ts, d), lambda b, i: (b, i, 0)),
                  pl.BlockSpec((None, 1, d), lambda b, i: (b, 0, 0))],
        out_specs=pl.BlockSpec((None, ts, d), lambda b, i: (b, i, 0)),
        out_shape=jax.ShapeDtypeStruct((bsz, s, d), F32),
        compiler_params=_cparams("parallel", "parallel"),
        name="mixer_out_proj",
    )(a1, a2, w, x, g)


def _ffn_kernel(x_ref, nw_ref, sh_ref, sc_ref, g_ref, w1_ref, w3_ref, w2_ref, o_ref, h_ref, acc_ref):
    f = pl.program_id(2)

    @pl.when(f == 0)
    def _():
        h_ref[...] = _modnorm(x_ref[...], nw_ref[...], sh_ref[...], sc_ref[...]).astype(BF16)

    h = h_ref[...]
    a = jnp.dot(h, w1_ref[...], preferred_element_type=F32)
    b = jnp.dot(h, w3_ref[...], preferred_element_type=F32)
    t = (_silu(a) * b).astype(BF16)
    contrib = jnp.dot(t, w2_ref[...], preferred_element_type=F32)

    @pl.when(f == 0)
    def _():
        acc_ref[...] = contrib

    @pl.when(f > 0)
    def _():
        acc_ref[...] += contrib

    @pl.when(f == pl.num_programs(2) - 1)
    def _():
        o_ref[...] = x_ref[...] + g_ref[...] * acc_ref[...]


def ffn_residual(x, nw, sh, sc, g, w1, w3, w2, *, ts, tf):
    bsz, s, d = x.shape
    dff = w1.shape[1]
    vec = pl.BlockSpec((None, 1, d), lambda b, i, f: (b, 0, 0))
    return pl.pallas_call(
        _ffn_kernel,
        grid=(bsz, s // ts, dff // tf),
        in_specs=[pl.BlockSpec((None, ts, d), lambda b, i, f: (b, i, 0)),
                  pl.BlockSpec((1, d), lambda b, i, f: (0, 0)),
                  vec, vec, vec,
                  pl.BlockSpec((d, tf), lambda b, i, f: (0, f)),
                  pl.BlockSpec((d, tf), lambda b, i, f: (0, f)),
                  pl.BlockSpec((tf, d), lambda b, i, f: (f, 0))],
        out_specs=pl.BlockSpec((None, ts, d), lambda b, i, f: (b, i, 0)),
        out_shape=jax.ShapeDtypeStruct((bsz, s, d), F32),
        scratch_shapes=[pltpu.VMEM((ts, d), BF16), pltpu.VMEM((ts, d), F32)],
        compiler_params=_cparams("parallel", "parallel", "arbitrary"),
        name="ffn_swiglu",
    )(x, nw.reshape(1, d), sh, sc, g, w1, w3, w2)


CONV_PAD = 8
CONV_ROWS = 256


def _conv_kernel(x_ref, w_ref, b_ref, o_ref, pad_ref):
    s, tc = x_ref.shape
    zeros = jnp.zeros((CONV_PAD, tc), F32)
    pad_ref[0:CONV_PAD, :] = zeros
    pad_ref[CONV_PAD + s:, :] = zeros
    pad_ref[CONV_PAD:CONV_PAD + s, :] = x_ref[...]
    half = CONV_WIDTH // 2
    for r in range(s // CONV_ROWS):
        base = CONV_PAD + r * CONV_ROWS - half
        acc = jnp.zeros((CONV_ROWS, tc), F32) + b_ref[...]
        for kk in range(CONV_WIDTH):
            acc = acc + pad_ref[base + kk:base + kk + CONV_ROWS, :] * w_ref[kk:kk + 1, :]
        o_ref[r * CONV_ROWS:(r + 1) * CONV_ROWS, :] = _silu(acc)


def conv_silu(proj, conv_w, conv_b, *, col0, tc):
    bsz, s, _ = proj.shape
    cdim = conv_w.shape[1]
    cb0 = col0 // tc
    return pl.pallas_call(
        _conv_kernel,
        grid=(bsz, cdim // tc),
        in_specs=[pl.BlockSpec((None, s, tc), lambda b, j: (b, 0, cb0 + j)),
                  pl.BlockSpec((CONV_WIDTH, tc), lambda b, j: (0, j)),
                  pl.BlockSpec((1, tc), lambda b, j: (0, j))],
        out_specs=pl.BlockSpec((None, s, tc), lambda b, j: (b, 0, j)),
        out_shape=jax.ShapeDtypeStruct((bsz, s, cdim), F32),
        scratch_shapes=[pltpu.VMEM((s + 2 * CONV_PAD, tc), F32)],
        compiler_params=_cparams("parallel", "parallel"),
        name="conv_silu",
    )(proj, conv_w, conv_b.reshape(1, cdim))


def _softplus(x):
    return jnp.maximum(x, 0.0) + jnp.log(1.0 + jnp.exp(-jnp.abs(x)))


def _expand_matrix(n_in, width):
    e = np.zeros((n_in, n_in * width), np.float32)
    for h in range(n_in):
        e[h, h * width:(h + 1) * width] = 1.0
    return e


LOG2E = 1.4426950408889634
DECAY_SLOTS = 12


def _decay_placement(nh):
    place = np.zeros((3, 2 * nh, 3 * 128), np.float32)
    const = np.zeros((1, 3 * 128), np.float32)
    for h in range(nh):
        for part in range(3):
            place[part, h, part * nh + h] = 1.0
            place[part, nh + h, (6 + part) * nh + h] = -1.0
            place[part, h, 128 + (3 + part) * nh + h] = -1.0
            place[part, nh + h, 256 + (9 + part) * nh + h] = 1.0
            const[0, (3 + part) * nh + h] = 1.0
            const[0, (9 + part) * nh + h] = 1.0
            const[0, 128 + part * nh + h] = 1.0
            const[0, 256 + (6 + part) * nh + h] = 1.0
    return place, const


def _split3(v):
    hi = v.astype(BF16)
    r = v - hi.astype(F32)
    mid = r.astype(BF16)
    lo = (r - mid.astype(F32)).astype(BF16)
    return hi, mid, lo


def _ssd_kernel(x_ref, b_ref, c_ref, dtc_f_ref, dtc_b_ref, dtr_f_ref, dtr_b_ref,
                pc_f_ref, pc_b_ref, pr_f_ref, pr_b_ref, dx_ref,
                tri_ref, place_ref, pconst_ref, e64_ref, y_ref,
                ac_ref, dc_ref, dr_ref, pq_ref, xdf_ref, xdb_ref, eif_ref, er_ref, decf_ref, decb_ref,
                hf_ref, hb_ref):
    s = x_ref.shape[0]
    q = SSD_CHUNK
    nh = pc_f_ref.shape[1]
    nc = s // q
    hd = SSM_HEAD_DIM

    def col_params(raw_ref, p_ref):
        dt = _softplus(raw_ref[...] + p_ref[0:1, :])
        return dt, (-LOG2E) * jnp.exp(p_ref[1:2, :]) * dt

    dcf, acf = col_params(dtc_f_ref, pc_f_ref)
    dcb, acb = col_params(dtc_b_ref, pc_b_ref)
    dc_ref[:, 0:nh] = dcf
    dc_ref[:, nh:] = dcb
    ac_ref[:, 0:nh] = acf
    ac_ref[:, nh:] = acb
    dr_ref[0:nh, :] = _softplus(dtr_f_ref[...] + pr_f_ref[:, 0:1])
    dr_ref[nh:, :] = _softplus(dtr_b_ref[...] + pr_b_ref[:, 0:1])

    li = lax.broadcasted_iota(jnp.int32, (q, q), 0)
    si = lax.broadcasted_iota(jnp.int32, (q, q), 1)
    lower = li >= si
    upper = li <= si
    slot_head = lax.broadcasted_iota(jnp.int32, (1, 128), 1) % nh
    pair_lo = lax.broadcasted_iota(jnp.int32, (1, 2 * hd), 1) < hd

    def bdot(a, b):
        return jnp.dot(a, b, preferred_element_type=F32)

    def chunk_cumsums(sl):
        a_col = ac_ref[sl, :]
        tri = tri_ref[...]
        hi, mid, lo = _split3(a_col)
        return a_col, bdot(tri, hi) + bdot(tri, mid) + bdot(tri, lo)

    def expand64(v, parts=2):
        e = e64_ref[...]
        hi = v.astype(BF16)
        out = bdot(hi, e)
        if parts == 2:
            out = out + bdot((v - hi.astype(F32)).astype(BF16), e)
        return out

    hf_ref[...] = jnp.zeros_like(hf_ref)
    hb_ref[...] = jnp.zeros_like(hb_ref)

    def prep_body(c, carry):
        sl = pl.ds(pl.multiple_of(c * q, q), q)
        a_col, i_col = chunk_cumsums(sl)
        d_col = dc_ref[sl, :]
        if_col = i_col[:, 0:nh]
        tot_f = i_col[q - 1:q, 0:nh]
        tot_b = i_col[q - 1:q, nh:]
        ie = i_col - jnp.where(lax.broadcasted_iota(jnp.int32, (1, 2 * nh), 1) < nh, 0.0, a_col)
        eb_col = ie[:, nh:]
        parts = _split3(ie)
        placed = pconst_ref[...]
        for part in range(3):
            placed = placed + bdot(parts[part], place_ref[part])
        pq_ref[sl, :] = placed.astype(BF16)
        xc = x_ref[sl, :]
        xdf_ref[sl, :] = (xc * expand64(jnp.exp2(tot_f - if_col) * d_col[:, 0:nh], parts=1)).astype(BF16)
        xdb_ref[sl, :] = (xc * expand64(jnp.exp2(eb_col) * d_col[:, nh:], parts=1)).astype(BF16)
        eif_ref[sl, :] = expand64(jnp.exp2(if_col))
        er_ref[sl, :] = expand64(jnp.exp2(tot_b - eb_col))
        dec = expand64(jnp.exp2(jnp.concatenate([tot_f, tot_b], axis=0)))
        decf_ref[c] = jnp.broadcast_to(dec[0:1, :], decf_ref.shape[1:])
        decb_ref[c] = jnp.broadcast_to(dec[1:2, :], decb_ref.shape[1:])
        return carry

    lax.fori_loop(0, nc, prep_body, 0, unroll=2)

    def fwd_body(c, carry):
        sl = pl.ds(pl.multiple_of(c * q, q), q)
        xc = x_ref[sl, :]
        bc = b_ref[sl, :].astype(BF16)
        cc = c_ref[sl, :].astype(BF16)
        d_row = dr_ref[:, sl]
        placed = pq_ref[sl, :]
        p_all = placed[:, 0:128]
        q_cat = jnp.concatenate([placed[:, 128:256], placed[:, 256:384]], axis=0)

        cb = lax.dot_general(cc, bc, (((1,), (1,)), ((), ())), preferred_element_type=F32)
        xb = xc.astype(BF16)
        ys = []
        for h0 in range(0, nh, 2):
            ms = []
            for h in (h0, h0 + 1):
                ph = jnp.where(slot_head == h, p_all, jnp.zeros_like(p_all))
                g2 = lax.dot_general(ph, q_cat, (((1,), (1,)), ((), ())), preferred_element_type=F32)
                arg = jnp.where(lower, g2[:, 0:q], g2[:, q:])
                wgt = (jnp.where(lower, d_row[h:h + 1, :], 0.0)
                       + jnp.where(upper, d_row[nh + h:nh + h + 1, :], 0.0))
                ms.append((cb * jnp.exp2(arg) * wgt).astype(BF16))
            xp = xb[:, h0 * hd:(h0 + 2) * hd]
            zero = jnp.zeros_like(xp)
            rhs = jnp.concatenate([jnp.where(pair_lo, xp, zero), jnp.where(pair_lo, zero, xp)], axis=0)
            ys.append(bdot(jnp.concatenate(ms, axis=1), rhs))
        y = jnp.concatenate(ys, axis=1) + dx_ref[...] * xc

        states = lax.dot_general(bc, xdf_ref[sl, :], (((0,), (0,)), ((), ())),
                                 preferred_element_type=F32)
        h_prev = hf_ref[...]
        y = y + bdot(cc, h_prev.astype(BF16)) * eif_ref[sl, :]
        hf_ref[...] = h_prev * decf_ref[c][0:1, :] + states
        y_ref[sl, :] = y
        return carry

    lax.fori_loop(0, nc, fwd_body, 0, unroll=2)

    def bwd_body(t, carry):
        c = nc - 1 - t
        sl = pl.ds(pl.multiple_of(c * q, q), q)
        bc = b_ref[sl, :].astype(BF16)
        cc = c_ref[sl, :].astype(BF16)
        states = lax.dot_general(bc, xdb_ref[sl, :], (((0,), (0,)), ((), ())),
                                 preferred_element_type=F32)
        h_prev = hb_ref[...]
        y_ref[sl, :] += bdot(cc, h_prev.astype(BF16)) * er_ref[sl, :]
        hb_ref[...] = h_prev * decb_ref[c][0:1, :] + states
        return carry

    lax.fori_loop(0, nc, bwd_body, 0, unroll=2)


def ssd_scan_bidir(xbc, dt, dt_bias_f, dt_bias_b, a_log_f, a_log_b, d_skip):
    bsz, s, _ = xbc.shape
    nheads = dt.shape[2] // 2
    nh = nheads // SSM_GROUPS
    gw = nh * SSM_HEAD_DIM
    d_inner = nheads * SSM_HEAD_DIM
    q = SSD_CHUNK
    dt_g = dt.reshape(bsz, s, 2 * SSM_GROUPS, nh)
    dt_col = jnp.transpose(dt_g, (0, 2, 1, 3))
    dt_row = jnp.transpose(dt_g, (0, 2, 3, 1))
    prm = jnp.stack([jnp.concatenate([dt_bias_f, dt_bias_b]), jnp.concatenate([a_log_f, a_log_b])])
    prm = prm.reshape(2, 2 * SSM_GROUPS, nh)
    p_col = jnp.transpose(prm, (1, 0, 2))
    p_row = jnp.transpose(prm, (1, 2, 0))
    dx = jnp.repeat(d_skip, SSM_HEAD_DIM).reshape(SSM_GROUPS, 1, gw)
    assert DECAY_SLOTS * nh <= 128
    tri = jnp.asarray(np.tril(np.ones((q, q), np.float32)), BF16)
    place_np, pconst_np = _decay_placement(nh)
    place = jnp.asarray(place_np, BF16)
    pconst = jnp.asarray(pconst_np)
    e64 = jnp.asarray(_expand_matrix(nh, SSM_HEAD_DIM), BF16)
    bcol = d_inner // D_STATE
    G = SSM_GROUPS

    const = lambda shape: pl.BlockSpec(shape, lambda b, g: tuple(0 for _ in shape))
    return pl.pallas_call(
        _ssd_kernel,
        grid=(bsz, SSM_GROUPS),
        in_specs=[pl.BlockSpec((None, s, gw), lambda b, g: (b, 0, g)),
                  pl.BlockSpec((None, s, D_STATE), lambda b, g: (b, 0, bcol + g)),
                  pl.BlockSpec((None, s, D_STATE), lambda b, g: (b, 0, bcol + G + g)),
                  pl.BlockSpec((None, None, s, nh), lambda b, g: (b, g, 0, 0)),
                  pl.BlockSpec((None, None, s, nh), lambda b, g: (b, G + g, 0, 0)),
                  pl.BlockSpec((None, None, nh, s), lambda b, g: (b, g, 0, 0)),
                  pl.BlockSpec((None, None, nh, s), lambda b, g: (b, G + g, 0, 0)),
                  pl.BlockSpec((None, 2, nh), lambda b, g: (g, 0, 0)),
                  pl.BlockSpec((None, 2, nh), lambda b, g: (G + g, 0, 0)),
                  pl.BlockSpec((None, nh, 2), lambda b, g: (g, 0, 0)),
                  pl.BlockSpec((None, nh, 2), lambda b, g: (G + g, 0, 0)),
                  pl.BlockSpec((None, 1, gw), lambda b, g: (g, 0, 0)),
                  const((q, q)), const(place.shape), const(pconst.shape), const((nh, gw))],
        out_specs=pl.BlockSpec((None, s, gw), lambda b, g: (b, 0, g)),
        out_shape=jax.ShapeDtypeStruct((bsz, s, d_inner), F32),
        scratch_shapes=[pltpu.VMEM((s, 2 * nh), F32), pltpu.VMEM((s, 2 * nh), F32),
                        pltpu.VMEM((2 * nh, s), F32),
                        pltpu.VMEM((s, 3 * 128), BF16),
                        pltpu.VMEM((s, gw), BF16), pltpu.VMEM((s, gw), BF16),
                        pltpu.VMEM((s, gw), F32), pltpu.VMEM((s, gw), F32),
                        pltpu.VMEM((s // q, 8, gw), F32), pltpu.VMEM((s // q, 8, gw), F32),
                        pltpu.VMEM((D_STATE, gw), F32), pltpu.VMEM((D_STATE, gw), F32)],
        compiler_params=_cparams("parallel", "parallel"),
        name="ssd_scan",
    )(xbc, xbc, xbc, dt_col, dt_col, dt_row, dt_row, p_col, p_col, p_row, p_row, dx, tri, place, pconst, e64)


def _gated_proj_kernel(y_ref, z_ref, gw_ref, w_ref, x_ref, g_ref, o_ref):
    t = y_ref[...] * _silu(z_ref[...])
    ms = jnp.mean(t * t, axis=-1, keepdims=True)
    t = (t * lax.rsqrt(ms + EPS) * gw_ref[...]).astype(BF16)
    o_ref[...] = x_ref[...] + g_ref[...] * jnp.dot(t, w_ref[...], preferred_element_type=F32)


def gated_proj_residual(y, zproj, gw, w, x, g, *, ts):
    bsz, s, d = x.shape
    k = y.shape[2]
    return pl.pallas_call(
        _gated_proj_kernel,
        grid=(bsz, s // ts),
        in_specs=[pl.BlockSpec((None, ts, k), lambda b, i: (b, i, 0)),
                  pl.BlockSpec((None, ts, k), lambda b, i: (b, i, 0)),
                  pl.BlockSpec((1, k), lambda b, i: (0, 0)),
                  pl.BlockSpec((k, d), lambda b, i: (0, 0)),
                  pl.BlockSpec((None, ts, d), lambda b, i: (b, i, 0)),
                  pl.BlockSpec((None, 1, d), lambda b, i: (b, 0, 0))],
        out_specs=pl.BlockSpec((None, ts, d), lambda b, i: (b, i, 0)),
        out_shape=jax.ShapeDtypeStruct((bsz, s, d), F32),
        compiler_params=_cparams("parallel", "parallel"),
        name="ssd_out_proj",
    )(y, zproj, gw.reshape(1, k), w, x, g)


def _router_kernel(x_ref, nw_ref, sh_ref, sc_ref, rw_ref, rb_ref, lt_ref,
                   eidx_ref, rank_ref, wgt_ref, cnt_ref, run_ref):
    first = (pl.program_id(0) == 0) & (pl.program_id(1) == 0)

    @pl.when(first)
    def _():
        run_ref[...] = jnp.zeros_like(run_ref)

    h = _modnorm(x_ref[...], nw_ref[...], sh_ref[...], sc_ref[...])
    logits = jnp.dot(h, rw_ref[...], precision=HIGHEST, preferred_element_type=F32) + rb_ref[...]
    ts, ne = logits.shape
    eid = lax.broadcasted_iota(jnp.int32, (ts, ne), 1)
    m1 = jnp.max(logits, axis=-1, keepdims=True)
    i1 = jnp.min(jnp.where(logits == m1, eid, ne), axis=-1, keepdims=True)
    rest = jnp.where(eid == i1, -jnp.inf, logits)
    m2 = jnp.max(rest, axis=-1, keepdims=True)
    i2 = jnp.min(jnp.where(rest == m2, eid, ne), axis=-1, keepdims=True)
    e2 = jnp.exp(m2 - m1)
    w1 = 1.0 / (1.0 + e2)
    w2 = e2 / (1.0 + e2)
    oh1 = (eid == i1).astype(F32)
    oh2 = (eid == i2).astype(F32)
    chosen = oh1 + oh2
    incl = jnp.dot(lt_ref[...], chosen.astype(BF16), preferred_element_type=F32)
    before = run_ref[...] + incl - chosen
    r1 = jnp.sum(oh1 * before, axis=-1, keepdims=True)
    r2 = jnp.sum(oh2 * before, axis=-1, keepdims=True)
    run_ref[...] = run_ref[...] + incl[ts - 1:ts, :]
    eidx_ref[...] = jnp.concatenate([i1, i2], axis=1)
    rank_ref[...] = jnp.concatenate([r1, r2], axis=1).astype(jnp.int32)
    wgt_ref[...] = jnp.concatenate([w1, w2], axis=1)
    cnt_ref[...] = run_ref[...].astype(jnp.int32)


def moe_route(x, nw, sh, sc, router_w, router_b, *, ts):
    bsz, s, d = x.shape
    ne = router_w.shape[1]
    nt = s // ts
    lt = jnp.asarray(np.tril(np.ones((ts, ts), np.float32)), BF16)
    tok = lambda dt: jax.ShapeDtypeStruct((bsz * s, TOP_K), dt)
    tok_spec = pl.BlockSpec((ts, TOP_K), lambda b, i: (b * nt + i, 0))
    return pl.pallas_call(
        _router_kernel,
        grid=(bsz, nt),
        in_specs=[pl.BlockSpec((None, ts, d), lambda b, i: (b, i, 0)),
                  pl.BlockSpec((1, d), lambda b, i: (0, 0)),
                  pl.BlockSpec((None, 1, d), lambda b, i: (b, 0, 0)),
                  pl.BlockSpec((None, 1, d), lambda b, i: (b, 0, 0)),
                  pl.BlockSpec((d, ne), lambda b, i: (0, 0)),
                  pl.BlockSpec((1, ne), lambda b, i: (0, 0)),
                  pl.BlockSpec((ts, ts), lambda b, i: (0, 0))],
        out_specs=[tok_spec, tok_spec, tok_spec, pl.BlockSpec((1, ne), lambda b, i: (0, 0))],
        out_shape=[tok(jnp.int32), tok(jnp.int32), tok(F32), jax.ShapeDtypeStruct((1, ne), jnp.int32)],
        scratch_shapes=[pltpu.VMEM((1, ne), F32)],
        compiler_params=_cparams("arbitrary", "arbitrary"),
        name="moe_router",
    )(x, nw.reshape(1, d), sh, sc, router_w, router_b.reshape(1, ne), lt)


def _dispatch_kernel(dest_ref, x_ref, nw_ref, sh_ref, sc_ref, hs_in_ref, hs_ref, h_ref, sem):
    del hs_in_ref
    ts = x_ref.shape[0]
    tile = pl.program_id(0) * pl.num_programs(1) + pl.program_id(1)
    h_ref[...] = _modnorm(x_ref[...], nw_ref[...], sh_ref[...], sc_ref[...])

    def row_copy(r, kk):
        dst = dest_ref[(tile * ts + r) * TOP_K + kk]
        return pltpu.make_async_copy(h_ref.at[pl.ds(r, 1), :], hs_ref.at[pl.ds(dst, 1), :], sem)

    def issue(r, carry):
        for kk in range(TOP_K):
            row_copy(r, kk).start()
        return carry

    lax.fori_loop(0, ts, issue, 0)

    def drain(r, carry):
        for kk in range(TOP_K):
            row_copy(r, kk).wait()
        return carry

    lax.fori_loop(0, ts, drain, 0)


def moe_dispatch(x, nw, sh, sc, dest, n_rows, *, ts):
    bsz, s, d = x.shape
    nt = s // ts
    hs0 = jnp.zeros((n_rows, d), F32)
    grid_spec = pltpu.PrefetchScalarGridSpec(
        num_scalar_prefetch=1,
        grid=(bsz, nt),
        in_specs=[pl.BlockSpec((None, ts, d), lambda b, i, dref: (b, i, 0)),
                  pl.BlockSpec((1, d), lambda b, i, dref: (0, 0)),
                  pl.BlockSpec((None, 1, d), lambda b, i, dref: (b, 0, 0)),
                  pl.BlockSpec((None, 1, d), lambda b, i, dref: (b, 0, 0)),
                  pl.BlockSpec(memory_space=pl.ANY)],
        out_specs=pl.BlockSpec(memory_space=pl.ANY),
        scratch_shapes=[pltpu.VMEM((ts, d), F32), pltpu.SemaphoreType.DMA(())],
    )
    return pl.pallas_call(
        _dispatch_kernel,
        grid_spec=grid_spec,
        out_shape=jax.ShapeDtypeStruct((n_rows, d), F32),
        input_output_aliases={5: 0},
        compiler_params=_cparams("arbitrary", "arbitrary"),
        name="moe_dispatch",
    )(dest, x, nw.reshape(1, d), sh, sc, hs0)


def _moe_kernel(te_ref, nu_ref, hs_ref, w1_ref, w3_ref, w2_ref, o_ref, hb_ref, acc_ref):
    i = pl.program_id(0)
    f = pl.program_id(1)

    @pl.when(i < nu_ref[0])
    def _():
        @pl.when(f == 0)
        def _():
            hb_ref[...] = hs_ref[...].astype(BF16)

        h = hb_ref[...]
        a = jnp.dot(h, w1_ref[...], preferred_element_type=F32)
        b = jnp.dot(h, w3_ref[...], preferred_element_type=F32)
        t = (_silu(a) * b).astype(BF16)
        contrib = jnp.dot(t, w2_ref[...], preferred_element_type=F32)

        @pl.when(f == 0)
        def _():
            acc_ref[...] = contrib

        @pl.when(f > 0)
        def _():
            acc_ref[...] += contrib

        @pl.when(f == pl.num_programs(1) - 1)
        def _():
            o_ref[...] = acc_ref[...]

    @pl.when((i >= nu_ref[0]) & (f == 0))
    def _():
        o_ref[...] = jnp.zeros_like(o_ref)


def moe_experts(hs, tile_expert, n_used, w1, w3, w2, *, tm, tf):
    n_rows, d = hs.shape
    dff = w1.shape[2]
    nf = dff // tf
    n_tiles = n_rows // tm

    def row_map(i, f, te, nu):
        return (jnp.minimum(i, nu[0] - 1), 0)

    def w_in_map(i, f, te, nu):
        return (te[jnp.minimum(i, nu[0] - 1)], 0, jnp.where(i < nu[0], f, nf - 1))

    def w_out_map(i, f, te, nu):
        return (te[jnp.minimum(i, nu[0] - 1)], jnp.where(i < nu[0], f, nf - 1), 0)

    grid_spec = pltpu.PrefetchScalarGridSpec(
        num_scalar_prefetch=2,
        grid=(n_tiles, nf),
        in_specs=[pl.BlockSpec((tm, d), row_map),
                  pl.BlockSpec((None, d, tf), w_in_map),
                  pl.BlockSpec((None, d, tf), w_in_map),
                  pl.BlockSpec((None, tf, d), w_out_map)],
        out_specs=pl.BlockSpec((tm, d), lambda i, f, te, nu: (i, 0)),
        scratch_shapes=[pltpu.VMEM((tm, d), BF16), pltpu.VMEM((tm, d), F32)],
    )
    return pl.pallas_call(
        _moe_kernel,
        grid_spec=grid_spec,
        out_shape=jax.ShapeDtypeStruct((n_rows, d), F32),
        compiler_params=_cparams("arbitrary", "arbitrary"),
        name="moe_experts",
    )(tile_expert, n_used, hs, w1, w3, w2)


def _combine_kernel(dest_ref, ys_ref, x_ref, g_ref, wgt_ref, o_ref, buf_ref, sem):
    tc = x_ref.shape[0]
    tile = pl.program_id(0) * pl.num_programs(1) + pl.program_id(1)

    def row_copy(r, kk):
        src = dest_ref[(tile * tc + r) * TOP_K + kk]
        return pltpu.make_async_copy(ys_ref.at[pl.ds(src, 1), :], buf_ref.at[kk, pl.ds(r, 1), :], sem)

    def issue(r, carry):
        for kk in range(TOP_K):
            row_copy(r, kk).start()
        return carry

    lax.fori_loop(0, tc, issue, 0)

    def drain(r, carry):
        for kk in range(TOP_K):
            row_copy(r, kk).wait()
        return carry

    lax.fori_loop(0, tc, drain, 0)
    w = wgt_ref[...]
    mix = w[:, 0:1] * buf_ref[0] + w[:, 1:2] * buf_ref[1]
    o_ref[...] = x_ref[...] + g_ref[...] * mix


def moe_combine(ys, dest, wgt, x, g, *, tc):
    bsz, s, d = x.shape
    nt = s // tc
    grid_spec = pltpu.PrefetchScalarGridSpec(
        num_scalar_prefetch=1,
        grid=(bsz, nt),
        in_specs=[pl.BlockSpec(memory_space=pl.ANY),
                  pl.BlockSpec((None, tc, d), lambda b, i, dref: (b, i, 0)),
                  pl.BlockSpec((None, 1, d), lambda b, i, dref: (b, 0, 0)),
                  pl.BlockSpec((tc, TOP_K), lambda b, i, dref: (b * nt + i, 0))],
        out_specs=pl.BlockSpec((None, tc, d), lambda b, i, dref: (b, i, 0)),
        scratch_shapes=[pltpu.VMEM((TOP_K, tc, d), F32), pltpu.SemaphoreType.DMA(())],
    )
    return pl.pallas_call(
        _combine_kernel,
        grid_spec=grid_spec,
        out_shape=jax.ShapeDtypeStruct((bsz, s, d), F32),
        compiler_params=_cparams("arbitrary", "arbitrary"),
        name="moe_combine",
    )(dest, ys, x, g, wgt)


def moe_residual(x, nw, sh, sc, g, router_w, router_b, w1, w3, w2, *, tm=MOE_TILE_ROWS):
    bsz, s, d = x.shape
    n_tok = bsz * s
    ne = router_w.shape[1]
    eidx, rank, wgt, counts = moe_route(x, nw, sh, sc, router_w, router_b, ts=min(512, s))
    counts = counts.reshape(ne)
    padded = ((counts + tm - 1) // tm) * tm
    ends = jnp.cumsum(padded)
    starts = ends - padded
    dest = (starts[eidx] + rank).reshape(-1).astype(jnp.int32)
    n_tiles = (n_tok * TOP_K) // tm + ne
    tile_start = jnp.arange(n_tiles, dtype=jnp.int32) * tm
    tile_expert = jnp.minimum(jnp.sum(tile_start[:, None] >= ends[None, :], axis=1), ne - 1).astype(jnp.int32)
    n_used = (ends[ne - 1:ne] // tm).astype(jnp.int32)
    hs = moe_dispatch(x, nw, sh, sc, dest, n_tiles * tm, ts=min(256, s))
    ys = moe_experts(hs, tile_expert, n_used, w1, w3, w2, tm=tm, tf=w1.shape[2] // 4)
    return moe_combine(ys, dest, wgt, x, g, tc=min(256, s))


def _split_mod(mod):
    return [m[:, None, :] for m in jnp.split(mod, 6, axis=-1)]


def even_layer(x, c, rel_bias, ada_w, ada_b, norm1_w, in_w, q_norm_w, k_norm_w, sink, out_w,
               norm2_w, w1, w3, w2):
    s = x.shape[1]
    sh1, sc1, g1, sh2, sc2, g2 = _split_mod(ada_mod(c, ada_w, ada_b))
    proj = norm_mod_matmul(x, norm1_w, sh1, sc1, in_w.astype(BF16), ts=min(1024, s), tn=640, name="even_in_proj")
    yf = fourier_mix(proj, tq=min(512, s))
    ya = window_attention(proj, band_bias(rel_bias), q_norm_w, k_norm_w, sink)
    x = cat_proj_residual(yf, ya, out_w.astype(BF16), x, g1, ts=min(1024, s))
    dff = w1.shape[1]
    return ffn_residual(x, norm2_w, sh2, sc2, g2, w1.astype(BF16), w3.astype(BF16), w2.astype(BF16),
                        ts=min(1024, s), tf=dff // 2)


def odd_layer(x, c, ada_w, ada_b, norm1_w, in_w, conv_w, conv_b, dt_bias_f, dt_bias_b, a_log_f, a_log_b,
              d_skip, gnorm_w, out_w, norm2_w, router_w, router_b, w1, w3, w2):
    s = x.shape[1]
    sh1, sc1, g1, sh2, sc2, g2 = _split_mod(ada_mod(c, ada_w, ada_b))
    d_inner = gnorm_w.shape[0]
    cdim = conv_w.shape[1]
    wide = d_inner + cdim
    in_w = in_w.astype(BF16)
    zx = norm_mod_matmul(x, norm1_w, sh1, sc1, in_w[:, :wide], ts=min(1024, s), tn=1024, name="odd_in_proj")
    dt = norm_mod_matmul(x, norm1_w, sh1, sc1, in_w[:, wide:], ts=min(1024, s), tn=in_w.shape[1] - wide,
                         name="odd_dt_proj")
    xbc = conv_silu(zx, conv_w, conv_b, col0=d_inner, tc=512)
    y = ssd_scan_bidir(xbc, dt, dt_bias_f, dt_bias_b, a_log_f, a_log_b, d_skip)
    x = gated_proj_residual(y, zx, gnorm_w, out_w.astype(BF16), x, g1, ts=min(512, s))
    return moe_residual(x, norm2_w, sh2, sc2, g2, router_w, router_b,
                        w1.astype(BF16), w3.astype(BF16), w2.astype(BF16))


def kernel(x, c, rel_bias, ev_ada_w, ev_ada_b, ev_norm1_w, ev_in_w, ev_q_norm_w, ev_k_norm_w, ev_sink, ev_out_w, ev_norm2_w, ev_ffn_w1, ev_ffn_w3, ev_ffn_w2, od_ada_w, od_ada_b, od_norm1_w, od_in_w, od_conv_w, od_conv_b, od_dt_bias_f, od_dt_bias_b, od_A_log_f, od_A_log_b, od_D, od_gnorm_w, od_out_w, od_norm2_w, od_router_w, od_router_b, od_moe_w1, od_moe_w3, od_moe_w2):
    depth = ev_ada_w.shape[0] + od_ada_w.shape[0]
    for i in range(depth):
        j = i // 2
        if i % 2 == 0:
            x = even_layer(x, c, rel_bias, ev_ada_w[j], ev_ada_b[j], ev_norm1_w[j], ev_in_w[j],
                           ev_q_norm_w[j], ev_k_norm_w[j], ev_sink[j], ev_out_w[j], ev_norm2_w[j],
                           ev_ffn_w1[j], ev_ffn_w3[j], ev_ffn_w2[j])
        else:
            x = odd_layer(x, c, od_ada_w[j], od_ada_b[j], od_norm1_w[j], od_in_w[j], od_conv_w[j],
                          od_conv_b[j], od_dt_bias_f[j], od_dt_bias_b[j], od_A_log_f[j], od_A_log_b[j],
                          od_D[j], od_gnorm_w[j], od_out_w[j], od_norm2_w[j], od_router_w[j],
                          od_router_b[j], od_moe_w1[j], od_moe_w3[j], od_moe_w2[j])
    return x
```

```python
import functools

import numpy as np
import jax
import jax.numpy as jnp
from jax import lax
from jax.experimental import pallas as pl
from jax.experimental.pallas import tpu as pltpu

F32 = jnp.float32
BF16 = jnp.bfloat16
HIGHEST = lax.Precision.HIGHEST

EPS = 1e-6
FNET_GROUPS = 4
FNET_GROUP_DIM = 128
FNET_WIDTH = FNET_GROUPS * FNET_GROUP_DIM
ATTN_HEADS = 8
ATTN_KV_HEADS = 2
HEAD_DIM = 64
ATTN_WIDTH = ATTN_HEADS * HEAD_DIM
KV_WIDTH = ATTN_KV_HEADS * HEAD_DIM
WINDOW = 128
BLOCK = 128
REL_BUCKETS = 32
REL_MAX_DIST = 128
SSM_HEAD_DIM = 64
SSM_GROUPS = 4
D_STATE = 128
CONV_WIDTH = 5
SSD_CHUNK = 128
N_EXPERTS = 8
TOP_K = 2
NEG_BIG = -1e30

V7X_VMEM_LIMIT_BYTES = 56 * 1024 * 1024
MOE_TILE_ROWS = 512


def _cparams(*sem):
    return pltpu.CompilerParams(dimension_semantics=sem, vmem_limit_bytes=V7X_VMEM_LIMIT_BYTES)


def _modnorm(x, nw, sh, sc):
    ms = jnp.mean(x * x, axis=-1, keepdims=True)
    return x * lax.rsqrt(ms + EPS) * nw * (1.0 + sc) + sh


def _silu(x):
    return x * (1.0 / (1.0 + jnp.exp(-x)))


def _ada_kernel(c_ref, w_ref, b_ref, o_ref):
    cs = _silu(c_ref[...]).astype(BF16)
    o_ref[...] = jnp.dot(cs, w_ref[...].astype(BF16), preferred_element_type=F32) + b_ref[...]


def ada_mod(c, w, b):
    bsz, d = c.shape
    n = w.shape[1]
    tn = 1536
    return pl.pallas_call(
        _ada_kernel,
        grid=(n // tn,),
        in_specs=[pl.BlockSpec((bsz, d), lambda j: (0, 0)),
                  pl.BlockSpec((d, tn), lambda j: (0, j)),
                  pl.BlockSpec((1, tn), lambda j: (0, j))],
        out_specs=pl.BlockSpec((bsz, tn), lambda j: (0, j)),
        out_shape=jax.ShapeDtypeStruct((bsz, n), F32),
        compiler_params=_cparams("arbitrary"),
        name="ada_mod",
    )(c, w, b.reshape(1, n))


def _nmm_kernel(x_ref, nw_ref, sh_ref, sc_ref, w_ref, o_ref, h_ref):
    @pl.when(pl.program_id(2) == 0)
    def _():
        h_ref[...] = _modnorm(x_ref[...], nw_ref[...], sh_ref[...], sc_ref[...]).astype(BF16)

    o_ref[...] = jnp.dot(h_ref[...], w_ref[...], preferred_element_type=F32).astype(o_ref.dtype)


def norm_mod_matmul(x, nw, sh, sc, w, *, ts, tn, name):
    bsz, s, d = x.shape
    n = w.shape[1]
    return pl.pallas_call(
        _nmm_kernel,
        grid=(bsz, s // ts, n // tn),
        in_specs=[pl.BlockSpec((None, ts, d), lambda b, i, j: (b, i, 0)),
                  pl.BlockSpec((1, d), lambda b, i, j: (0, 0)),
                  pl.BlockSpec((None, 1, d), lambda b, i, j: (b, 0, 0)),
                  pl.BlockSpec((None, 1, d), lambda b, i, j: (b, 0, 0)),
                  pl.BlockSpec((d, tn), lambda b, i, j: (0, j))],
        out_specs=pl.BlockSpec((None, ts, tn), lambda b, i, j: (b, i, j)),
        out_shape=jax.ShapeDtypeStruct((bsz, s, n), F32),
        scratch_shapes=[pltpu.VMEM((ts, d), BF16)],
        compiler_params=_cparams("parallel", "parallel", "arbitrary"),
        name=name,
    )(x, nw.reshape(1, d), sh, sc, w)


def _dft_cos_sin(n):
    k = np.arange(n, dtype=np.int64)
    ang = ((k[:, None] * k[None, :]) % n).astype(np.float64) * (2.0 * np.pi / n)
    scale = 1.0 / np.sqrt(n)
    return np.cos(ang) * scale, np.sin(ang) * scale


def _fourier_kernel(u_ref, chan_ref, seq_ref, o_ref, ab_ref):
    s = u_ref.shape[0]

    @pl.when(pl.program_id(1) == 0)
    def _():
        for g in range(FNET_GROUPS):
            lo, hi = g * FNET_GROUP_DIM, (g + 1) * FNET_GROUP_DIM
            ug = u_ref[:, lo:hi].astype(BF16)
            cs = jnp.dot(ug, chan_ref[...], preferred_element_type=F32)
            ab_ref[0:s, lo:hi] = cs[:, :FNET_GROUP_DIM].astype(BF16)
            ab_ref[s:2 * s, lo:hi] = cs[:, FNET_GROUP_DIM:].astype(BF16)

    o_ref[...] = jnp.dot(seq_ref[...], ab_ref[...], preferred_element_type=F32)


def fourier_mix(proj, *, tq):
    bsz, s, _ = proj.shape
    cc, sc = _dft_cos_sin(FNET_GROUP_DIM)
    chan = jnp.asarray(np.concatenate([cc, sc], axis=1), BF16)
    cs, ss = _dft_cos_sin(s)
    seq = jnp.asarray(np.concatenate([cs, -ss], axis=1), BF16)
    return pl.pallas_call(
        _fourier_kernel,
        grid=(bsz, s // tq),
        in_specs=[pl.BlockSpec((None, s, FNET_WIDTH), lambda b, i: (b, 0, 0)),
                  pl.BlockSpec((FNET_GROUP_DIM, 2 * FNET_GROUP_DIM), lambda b, i: (0, 0)),
                  pl.BlockSpec((tq, 2 * s), lambda b, i: (i, 0))],
        out_specs=pl.BlockSpec((None, tq, FNET_WIDTH), lambda b, i: (b, i, 0)),
        out_shape=jax.ShapeDtypeStruct((bsz, s, FNET_WIDTH), F32),
        scratch_shapes=[pltpu.VMEM((2 * s, FNET_WIDTH), BF16)],
        compiler_params=_cparams("parallel", "arbitrary"),
        name="fourier_mix",
    )(proj, chan, seq)


def _band_bucket_table():
    i = np.arange(BLOCK)[:, None]
    j = np.arange(3 * BLOCK)[None, :]
    rel = (j - BLOCK) - i
    half = REL_BUCKETS // 2
    max_exact = half // 2
    n = np.abs(rel)
    large = max_exact + (np.log(np.maximum(n, 1) / max_exact)
                         / np.log(REL_MAX_DIST / max_exact) * (half - max_exact)).astype(np.int32)
    large = np.minimum(large, half - 1)
    bucket = (rel > 0).astype(np.int32) * half + np.where(n < max_exact, n, large)
    return np.where(n <= WINDOW, bucket, -1).astype(np.int32)


def _bias_kernel(rb_ref, bucket_ref, o_ref):
    h = pl.program_id(0)
    bucket = bucket_ref[...]
    acc = jnp.full(bucket.shape, NEG_BIG, F32)
    for bkt in range(REL_BUCKETS):
        acc = jnp.where(bucket == bkt, rb_ref[bkt * ATTN_HEADS + h], acc)
    o_ref[...] = acc


def band_bias(rel_bias):
    bucket = jnp.asarray(_band_bucket_table())
    return pl.pallas_call(
        _bias_kernel,
        grid=(ATTN_HEADS,),
        in_specs=[pl.BlockSpec(memory_space=pltpu.SMEM),
                  pl.BlockSpec((BLOCK, 3 * BLOCK), lambda h: (0, 0))],
        out_specs=pl.BlockSpec((None, BLOCK, 3 * BLOCK), lambda h: (h, 0, 0)),
        out_shape=jax.ShapeDtypeStruct((ATTN_HEADS, BLOCK, 3 * BLOCK), F32),
        compiler_params=_cparams("arbitrary"),
        name="band_bias",
    )(rel_bias.reshape(-1), bucket)


def _head_rms(t, w):
    ms = jnp.mean(t * t, axis=-1, keepdims=True)
    return t * lax.rsqrt(ms + EPS) * w


def _attn_kernel(sink_ref, q_ref, kl_ref, kc_ref, kr_ref, vl_ref, vc_ref, vr_ref,
                 bias_ref, qnw_ref, knw_ref, o_ref):
    n = pl.program_id(1)
    nb = pl.num_programs(1)
    k = jnp.concatenate([kl_ref[...], kc_ref[...], kr_ref[...]], axis=0)
    v = jnp.concatenate([vl_ref[...], vc_ref[...], vr_ref[...]], axis=0)
    col = lax.broadcasted_iota(jnp.int32, (1, 3 * BLOCK), 1)
    first_key = jnp.where(n == 0, BLOCK, 0)
    end_key = jnp.where(n == nb - 1, 2 * BLOCK, 3 * BLOCK)
    outside = (col < first_key) | (col >= end_key)
    q = q_ref[...]
    qnw = qnw_ref[...]
    knw = knw_ref[...]
    g = ATTN_HEADS // ATTN_KV_HEADS
    outs = []
    for j in range(ATTN_KV_HEADS):
        kj = _head_rms(k[:, j * HEAD_DIM:(j + 1) * HEAD_DIM], knw).astype(BF16)
        vj = v[:, j * HEAD_DIM:(j + 1) * HEAD_DIM].astype(BF16)
        for gi in range(g):
            h = j * g + gi
            qh = _head_rms(q[:, h * HEAD_DIM:(h + 1) * HEAD_DIM], qnw).astype(BF16)
            logits = lax.dot_general(qh, kj, (((1,), (1,)), ((), ())), preferred_element_type=F32)
            logits = logits * (HEAD_DIM ** -0.5) + bias_ref[h]
            logits = jnp.where(outside, NEG_BIG, logits)
            sk = sink_ref[h]
            m = jnp.maximum(jnp.max(logits, axis=-1, keepdims=True), sk)
            p = jnp.exp(logits - m)
            denom = jnp.sum(p, axis=-1, keepdims=True) + jnp.exp(sk - m)
            pv = jnp.dot(p.astype(BF16), vj, preferred_element_type=F32)
            outs.append(pv / denom)
    o_ref[...] = jnp.concatenate(outs, axis=-1)


def window_attention(proj, bias, q_norm_w, k_norm_w, sink):
    bsz, s, _ = proj.shape
    nb = s // BLOCK
    qcol = FNET_WIDTH // ATTN_WIDTH
    kcol = (FNET_WIDTH + ATTN_WIDTH) // KV_WIDTH
    vcol = kcol + 1

    def kv_spec(col, shift):
        return pl.BlockSpec((None, BLOCK, KV_WIDTH),
                            lambda b, n: (b, jnp.clip(n + shift, 0, nb - 1), col))

    return pl.pallas_call(
        _attn_kernel,
        grid=(bsz, nb),
        in_specs=[pl.BlockSpec(memory_space=pltpu.SMEM),
                  pl.BlockSpec((None, BLOCK, ATTN_WIDTH), lambda b, n: (b, n, qcol)),
                  kv_spec(kcol, -1), kv_spec(kcol, 0), kv_spec(kcol, 1),
                  kv_spec(vcol, -1), kv_spec(vcol, 0), kv_spec(vcol, 1),
                  pl.BlockSpec((ATTN_HEADS, BLOCK, 3 * BLOCK), lambda b, n: (0, 0, 0)),
                  pl.BlockSpec((1, HEAD_DIM), lambda b, n: (0, 0)),
                  pl.BlockSpec((1, HEAD_DIM), lambda b, n: (0, 0))],
        out_specs=pl.BlockSpec((None, BLOCK, ATTN_WIDTH), lambda b, n: (b, n, 0)),
        out_shape=jax.ShapeDtypeStruct((bsz, s, ATTN_WIDTH), F32),
        compiler_params=_cparams("parallel", "arbitrary"),
        name="window_attention",
    )(sink, proj, proj, proj, proj, proj, proj, proj, bias,
      q_norm_w.reshape(1, HEAD_DIM), k_norm_w.reshape(1, HEAD_DIM))


def _cat_proj_kernel(a1_ref, a2_ref, w_ref, x_ref, g_ref, o_ref):
    k1 = a1_ref.shape[1]
    y = jnp.dot(a1_ref[...].astype(BF16), w_ref[0:k1, :], preferred_element_type=F32)
    y = y + jnp.dot(a2_ref[...].astype(BF16), w_ref[k1:, :], preferred_element_type=F32)
    o_ref[...] = x_ref[...] + g_ref[...] * y


def cat_proj_residual(a1, a2, w, x, g, *, ts):
    bsz, s, d = x.shape
    k1, k2 = a1.shape[2], a2.shape[2]
    return pl.pallas_call(
        _cat_proj_kernel,
        grid=(bsz, s // ts),
        in_specs=[pl.BlockSpec((None, ts, k1), lambda b, i: (b, i, 0)),
                  pl.BlockSpec((None, ts, k2), lambda b, i: (b, i, 0)),
                  pl.BlockSpec((k1 + k2, d), lambda b, i: (0, 0)),
                  pl.BlockSpec((None, ts, d), lambda b, i: (b, i, 0)),
                  pl.BlockSpec((None, 1, d), lambda b, i: (b, 0, 0))],
        out_specs=pl.BlockSpec((None, ts, d), lambda b, i: (b, i, 0)),
        out_shape=jax.ShapeDtypeStruct((bsz, s, d), F32),
        compiler_params=_cparams("parallel", "parallel"),
        name="mixer_out_proj",
    )(a1, a2, w, x, g)


def _ffn_kernel(x_ref, nw_ref, sh_ref, sc_ref, g_ref, w1_ref, w3_ref, w2_ref, o_ref, h_ref, acc_ref):
    f = pl.program_id(2)

    @pl.when(f == 0)
    def _():
        h_ref[...] = _modnorm(x_ref[...], nw_ref[...], sh_ref[...], sc_ref[...]).astype(BF16)

    h = h_ref[...]
    a = jnp.dot(h, w1_ref[...], preferred_element_type=F32)
    b = jnp.dot(h, w3_ref[...], preferred_element_type=F32)
    t = (_silu(a) * b).astype(BF16)
    contrib = jnp.dot(t, w2_ref[...], preferred_element_type=F32)

    @pl.when(f == 0)
    def _():
        acc_ref[...] = contrib

    @pl.when(f > 0)
    def _():
        acc_ref[...] += contrib

    @pl.when(f == pl.num_programs(2) - 1)
    def _():
        o_ref[...] = x_ref[...] + g_ref[...] * acc_ref[...]


def ffn_residual(x, nw, sh, sc, g, w1, w3, w2, *, ts, tf):
    bsz, s, d = x.shape
    dff = w1.shape[1]
    vec = pl.BlockSpec((None, 1, d), lambda b, i, f: (b, 0, 0))
    return pl.pallas_call(
        _ffn_kernel,
        grid=(bsz, s // ts, dff // tf),
        in_specs=[pl.BlockSpec((None, ts, d), lambda b, i, f: (b, i, 0)),
                  pl.BlockSpec((1, d), lambda b, i, f: (0, 0)),
                  vec, vec, vec,
                  pl.BlockSpec((d, tf), lambda b, i, f: (0, f)),
                  pl.BlockSpec((d, tf), lambda b, i, f: (0, f)),
                  pl.BlockSpec((tf, d), lambda b, i, f: (f, 0))],
        out_specs=pl.BlockSpec((None, ts, d), lambda b, i, f: (b, i, 0)),
        out_shape=jax.ShapeDtypeStruct((bsz, s, d), F32),
        scratch_shapes=[pltpu.VMEM((ts, d), BF16), pltpu.VMEM((ts, d), F32)],
        compiler_params=_cparams("parallel", "parallel", "arbitrary"),
        name="ffn_swiglu",
    )(x, nw.reshape(1, d), sh, sc, g, w1, w3, w2)


CONV_PAD = 8
CONV_ROWS = 256


def _conv_kernel(x_ref, w_ref, b_ref, o_ref, pad_ref):
    s, tc = x_ref.shape
    zeros = jnp.zeros((CONV_PAD, tc), F32)
    pad_ref[0:CONV_PAD, :] = zeros
    pad_ref[CONV_PAD + s:, :] = zeros
    pad_ref[CONV_PAD:CONV_PAD + s, :] = x_ref[...]
    half = CONV_WIDTH // 2
    for r in range(s // CONV_ROWS):
        base = CONV_PAD + r * CONV_ROWS - half
        acc = jnp.zeros((CONV_ROWS, tc), F32) + b_ref[...]
        for kk in range(CONV_WIDTH):
            acc = acc + pad_ref[base + kk:base + kk + CONV_ROWS, :] * w_ref[kk:kk + 1, :]
        o_ref[r * CONV_ROWS:(r + 1) * CONV_ROWS, :] = _silu(acc)


def conv_silu(proj, conv_w, conv_b, *, col0, tc):
    bsz, s, _ = proj.shape
    cdim = conv_w.shape[1]
    cb0 = col0 // tc
    return pl.pallas_call(
        _conv_kernel,
        grid=(bsz, cdim // tc),
        in_specs=[pl.BlockSpec((None, s, tc), lambda b, j: (b, 0, cb0 + j)),
                  pl.BlockSpec((CONV_WIDTH, tc), lambda b, j: (0, j)),
                  pl.BlockSpec((1, tc), lambda b, j: (0, j))],
        out_specs=pl.BlockSpec((None, s, tc), lambda b, j: (b, 0, j)),
        out_shape=jax.ShapeDtypeStruct((bsz, s, cdim), F32),
        scratch_shapes=[pltpu.VMEM((s + 2 * CONV_PAD, tc), F32)],
        compiler_params=_cparams("parallel", "parallel"),
        name="conv_silu",
    )(proj, conv_w, conv_b.reshape(1, cdim))


def _softplus(x):
    return jnp.maximum(x, 0.0) + jnp.log(1.0 + jnp.exp(-jnp.abs(x)))


def _expand_matrix(n_in, width):
    e = np.zeros((n_in, n_in * width), np.float32)
    for h in range(n_in):
        e[h, h * width:(h + 1) * width] = 1.0
    return e


LOG2E = 1.4426950408889634
DECAY_SLOTS = 12


def _decay_placement(nh):
    place = np.zeros((3, 2 * nh, 3 * 128), np.float32)
    const = np.zeros((1, 3 * 128), np.float32)
    for h in range(nh):
        for part in range(3):
            place[part, h, part * nh + h] = 1.0
            place[part, nh + h, (6 + part) * nh + h] = -1.0
            place[part, h, 128 + (3 + part) * nh + h] = -1.0
            place[part, nh + h, 256 + (9 + part) * nh + h] = 1.0
            const[0, (3 + part) * nh + h] = 1.0
            const[0, (9 + part) * nh + h] = 1.0
            const[0, 128 + part * nh + h] = 1.0
            const[0, 256 + (6 + part) * nh + h] = 1.0
    return place, const


def _split3(v):
    hi = v.astype(BF16)
    r = v - hi.astype(F32)
    mid = r.astype(BF16)
    lo = (r - mid.astype(F32)).astype(BF16)
    return hi, mid, lo


def _ssd_kernel(x_ref, b_ref, c_ref, dtc_f_ref, dtc_b_ref, dtr_f_ref, dtr_b_ref,
                pc_f_ref, pc_b_ref, pr_f_ref, pr_b_ref, dx_ref,
                tri_ref, place_ref, pconst_ref, e64_ref, y_ref,
                ac_ref, dc_ref, dr_ref, pq_ref, xdf_ref, xdb_ref, eif_ref, er_ref, decf_ref, decb_ref,
                hf_ref, hb_ref):
    s = x_ref.shape[0]
    q = SSD_CHUNK
    nh = pc_f_ref.shape[1]
    nc = s // q
    hd = SSM_HEAD_DIM

    def col_params(raw_ref, p_ref):
        dt = _softplus(raw_ref[...] + p_ref[0:1, :])
        return dt, (-LOG2E) * jnp.exp(p_ref[1:2, :]) * dt

    dcf, acf = col_params(dtc_f_ref, pc_f_ref)
    dcb, acb = col_params(dtc_b_ref, pc_b_ref)
    dc_ref[:, 0:nh] = dcf
    dc_ref[:, nh:] = dcb
    ac_ref[:, 0:nh] = acf
    ac_ref[:, nh:] = acb
    dr_ref[0:nh, :] = _softplus(dtr_f_ref[...] + pr_f_ref[:, 0:1])
    dr_ref[nh:, :] = _softplus(dtr_b_ref[...] + pr_b_ref[:, 0:1])

    li = lax.broadcasted_iota(jnp.int32, (q, q), 0)
    si = lax.broadcasted_iota(jnp.int32, (q, q), 1)
    lower = li >= si
    upper = li <= si
    slot_head = lax.broadcasted_iota(jnp.int32, (1, 128), 1) % nh
    pair_lo = lax.broadcasted_iota(jnp.int32, (1, 2 * hd), 1) < hd

    def bdot(a, b):
        return jnp.dot(a, b, preferred_element_type=F32)

    def chunk_cumsums(sl):
        a_col = ac_ref[sl, :]
        tri = tri_ref[...]
        hi, mid, lo = _split3(a_col)
        return a_col, bdot(tri, hi) + bdot(tri, mid) + bdot(tri, lo)

    def expand64(v, parts=2):
        e = e64_ref[...]
        hi = v.astype(BF16)
        out = bdot(hi, e)
        if parts == 2:
            out = out + bdot((v - hi.astype(F32)).astype(BF16), e)
        return out

    hf_ref[...] = jnp.zeros_like(hf_ref)
    hb_ref[...] = jnp.zeros_like(hb_ref)

    def prep_body(c, carry):
        sl = pl.ds(pl.multiple_of(c * q, q), q)
        a_col, i_col = chunk_cumsums(sl)
        d_col = dc_ref[sl, :]
        if_col = i_col[:, 0:nh]
        tot_f = i_col[q - 1:q, 0:nh]
        tot_b = i_col[q - 1:q, nh:]
        ie = i_col - jnp.where(lax.broadcasted_iota(jnp.int32, (1, 2 * nh), 1) < nh, 0.0, a_col)
        eb_col = ie[:, nh:]
        parts = _split3(ie)
        placed = pconst_ref[...]
        for part in range(3):
            placed = placed + bdot(parts[part], place_ref[part])
        pq_ref[sl, :] = placed.astype(BF16)
        xc = x_ref[sl, :]
        xdf_ref[sl, :] = (xc * expand64(jnp.exp2(tot_f - if_col) * d_col[:, 0:nh], parts=1)).astype(BF16)
        xdb_ref[sl, :] = (xc * expand64(jnp.exp2(eb_col) * d_col[:, nh:], parts=1)).astype(BF16)
        eif_ref[sl, :] = expand64(jnp.exp2(if_col))
        er_ref[sl, :] = expand64(jnp.exp2(tot_b - eb_col))
        dec = expand64(jnp.exp2(jnp.concatenate([tot_f, tot_b], axis=0)))
        decf_ref[c] = jnp.broadcast_to(dec[0:1, :], decf_ref.shape[1:])
        decb_ref[c] = jnp.broadcast_to(dec[1:2, :], decb_ref.shape[1:])
        return carry

    lax.fori_loop(0, nc, prep_body, 0, unroll=2)

    def fwd_body(c, carry):
        sl = pl.ds(pl.multiple_of(c * q, q), q)
        xc = x_ref[sl, :]
        bc = b_ref[sl, :].astype(BF16)
        cc = c_ref[sl, :].astype(BF16)
        d_row = dr_ref[:, sl]
        placed = pq_ref[sl, :]
        p_all = placed[:, 0:128]
        q_cat = jnp.concatenate([placed[:, 128:256], placed[:, 256:384]], axis=0)

        cb = lax.dot_general(cc, bc, (((1,), (1,)), ((), ())), preferred_element_type=F32)
        xb = xc.astype(BF16)
        ys = []
        for h0 in range(0, nh, 2):
            ms = []
            for h in (h0, h0 + 1):
                ph = jnp.where(slot_head == h, p_all, jnp.zeros_like(p_all))
                g2 = lax.dot_general(ph, q_cat, (((1,), (1,)), ((), ())), preferred_element_type=F32)
                arg = jnp.where(lower, g2[:, 0:q], g2[:, q:])
                wgt = (jnp.where(lower, d_row[h:h + 1, :], 0.0)
                       + jnp.where(upper, d_row[nh + h:nh + h + 1, :], 0.0))
                ms.append((cb * jnp.exp2(arg) * wgt).astype(BF16))
            xp = xb[:, h0 * hd:(h0 + 2) * hd]
            zero = jnp.zeros_like(xp)
            rhs = jnp.concatenate([jnp.where(pair_lo, xp, zero), jnp.where(pair_lo, zero, xp)], axis=0)
            ys.append(bdot(jnp.concatenate(ms, axis=1), rhs))
        y = jnp.concatenate(ys, axis=1) + dx_ref[...] * xc

        states = lax.dot_general(bc, xdf_ref[sl, :], (((0,), (0,)), ((), ())),
                                 preferred_element_type=F32)
        h_prev = hf_ref[...]
        y = y + bdot(cc, h_prev.astype(BF16)) * eif_ref[sl, :]
        hf_ref[...] = h_prev * decf_ref[c][0:1, :] + states
        y_ref[sl, :] = y
        return carry

    lax.fori_loop(0, nc, fwd_body, 0, unroll=2)

    def bwd_body(t, carry):
        c = nc - 1 - t
        sl = pl.ds(pl.multiple_of(c * q, q), q)
        bc = b_ref[sl, :].astype(BF16)
        cc = c_ref[sl, :].astype(BF16)
        states = lax.dot_general(bc, xdb_ref[sl, :], (((0,), (0,)), ((), ())),
                                 preferred_element_type=F32)
        h_prev = hb_ref[...]
        y_ref[sl, :] += bdot(cc, h_prev.astype(BF16)) * er_ref[sl, :]
        hb_ref[...] = h_prev * decb_ref[c][0:1, :] + states
        return carry

    lax.fori_loop(0, nc, bwd_body, 0, unroll=2)


def ssd_scan_bidir(xbc, dt, dt_bias_f, dt_bias_b, a_log_f, a_log_b, d_skip):
    bsz, s, _ = xbc.shape
    nheads = dt.shape[2] // 2
    nh = nheads // SSM_GROUPS
    gw = nh * SSM_HEAD_DIM
    d_inner = nheads * SSM_HEAD_DIM
    q = SSD_CHUNK
    dt_g = dt.reshape(bsz, s, 2 * SSM_GROUPS, nh)
    dt_col = jnp.transpose(dt_g, (0, 2, 1, 3))
    dt_row = jnp.transpose(dt_g, (0, 2, 3, 1))
    prm = jnp.stack([jnp.concatenate([dt_bias_f, dt_bias_b]), jnp.concatenate([a_log_f, a_log_b])])
    prm = prm.reshape(2, 2 * SSM_GROUPS, nh)
    p_col = jnp.transpose(prm, (1, 0, 2))
    p_row = jnp.transpose(prm, (1, 2, 0))
    dx = jnp.repeat(d_skip, SSM_HEAD_DIM).reshape(SSM_GROUPS, 1, gw)
    assert DECAY_SLOTS * nh <= 128
    tri = jnp.asarray(np.tril(np.ones((q, q), np.float32)), BF16)
    place_np, pconst_np = _decay_placement(nh)
    place = jnp.asarray(place_np, BF16)
    pconst = jnp.asarray(pconst_np)
    e64 = jnp.asarray(_expand_matrix(nh, SSM_HEAD_DIM), BF16)
    bcol = d_inner // D_STATE
    G = SSM_GROUPS

    const = lambda shape: pl.BlockSpec(shape, lambda b, g: tuple(0 for _ in shape))
    return pl.pallas_call(
        _ssd_kernel,
        grid=(bsz, SSM_GROUPS),
        in_specs=[pl.BlockSpec((None, s, gw), lambda b, g: (b, 0, g)),
                  pl.BlockSpec((None, s, D_STATE), lambda b, g: (b, 0, bcol + g)),
                  pl.BlockSpec((None, s, D_STATE), lambda b, g: (b, 0, bcol + G + g)),
                  pl.BlockSpec((None, None, s, nh), lambda b, g: (b, g, 0, 0)),
                  pl.BlockSpec((None, None, s, nh), lambda b, g: (b, G + g, 0, 0)),
                  pl.BlockSpec((None, None, nh, s), lambda b, g: (b, g, 0, 0)),
                  pl.BlockSpec((None, None, nh, s), lambda b, g: (b, G + g, 0, 0)),
                  pl.BlockSpec((None, 2, nh), lambda b, g: (g, 0, 0)),
                  pl.BlockSpec((None, 2, nh), lambda b, g: (G + g, 0, 0)),
                  pl.BlockSpec((None, nh, 2), lambda b, g: (g, 0, 0)),
                  pl.BlockSpec((None, nh, 2), lambda b, g: (G + g, 0, 0)),
                  pl.BlockSpec((None, 1, gw), lambda b, g: (g, 0, 0)),
                  const((q, q)), const(place.shape), const(pconst.shape), const((nh, gw))],
        out_specs=pl.BlockSpec((None, s, gw), lambda b, g: (b, 0, g)),
        out_shape=jax.ShapeDtypeStruct((bsz, s, d_inner), F32),
        scratch_shapes=[pltpu.VMEM((s, 2 * nh), F32), pltpu.VMEM((s, 2 * nh), F32),
                        pltpu.VMEM((2 * nh, s), F32),
                        pltpu.VMEM((s, 3 * 128), BF16),
                        pltpu.VMEM((s, gw), BF16), pltpu.VMEM((s, gw), BF16),
                        pltpu.VMEM((s, gw), F32), pltpu.VMEM((s, gw), F32),
                        pltpu.VMEM((s // q, 8, gw), F32), pltpu.VMEM((s // q, 8, gw), F32),
                        pltpu.VMEM((D_STATE, gw), F32), pltpu.VMEM((D_STATE, gw), F32)],
        compiler_params=_cparams("parallel", "parallel"),
        name="ssd_scan",
    )(xbc, xbc, xbc, dt_col, dt_col, dt_row, dt_row, p_col, p_col, p_row, p_row, dx, tri, place, pconst, e64)


def _gated_proj_kernel(y_ref, z_ref, gw_ref, w_ref, x_ref, g_ref, o_ref):
    t = y_ref[...] * _silu(z_ref[...])
    ms = jnp.mean(t * t, axis=-1, keepdims=True)
    t = (t * lax.rsqrt(ms + EPS) * gw_ref[...]).astype(BF16)
    o_ref[...] = x_ref[...] + g_ref[...] * jnp.dot(t, w_ref[...], preferred_element_type=F32)


def gated_proj_residual(y, zproj, gw, w, x, g, *, ts):
    bsz, s, d = x.shape
    k = y.shape[2]
    return pl.pallas_call(
        _gated_proj_kernel,
        grid=(bsz, s // ts),
        in_specs=[pl.BlockSpec((None, ts, k), lambda b, i: (b, i, 0)),
                  pl.BlockSpec((None, ts, k), lambda b, i: (b, i, 0)),
                  pl.BlockSpec((1, k), lambda b, i: (0, 0)),
                  pl.BlockSpec((k, d), lambda b, i: (0, 0)),
                  pl.BlockSpec((None, ts, d), lambda b, i: (b, i, 0)),
                  pl.BlockSpec((None, 1, d), lambda b, i: (b, 0, 0))],
        out_specs=pl.BlockSpec((None, ts, d), lambda b, i: (b, i, 0)),
        out_shape=jax.ShapeDtypeStruct((bsz, s, d), F32),
        compiler_params=_cparams("parallel", "parallel"),
        name="ssd_out_proj",
    )(y, zproj, gw.reshape(1, k), w, x, g)


def _router_kernel(x_ref, nw_ref, sh_ref, sc_ref, rw_ref, rb_ref, lt_ref,
                   eidx_ref, rank_ref, wgt_ref, cnt_ref):
    h = _modnorm(x_ref[...], nw_ref[...], sh_ref[...], sc_ref[...])
    logits = jnp.dot(h, rw_ref[...], precision=HIGHEST, preferred_element_type=F32) + rb_ref[...]
    ts, ne = logits.shape
    eid = lax.broadcasted_iota(jnp.int32, (ts, ne), 1)
    m1 = jnp.max(logits, axis=-1, keepdims=True)
    i1 = jnp.min(jnp.where(logits == m1, eid, ne), axis=-1, keepdims=True)
    rest = jnp.where(eid == i1, -jnp.inf, logits)
    m2 = jnp.max(rest, axis=-1, keepdims=True)
    i2 = jnp.min(jnp.where(rest == m2, eid, ne), axis=-1, keepdims=True)
    e2 = jnp.exp(m2 - m1)
    w1 = 1.0 / (1.0 + e2)
    w2 = e2 / (1.0 + e2)
    oh1 = (eid == i1).astype(F32)
    oh2 = (eid == i2).astype(F32)
    chosen = oh1 + oh2
    incl = jnp.dot(lt_ref[...], chosen.astype(BF16), preferred_element_type=F32)
    before = incl - chosen
    r1 = jnp.sum(oh1 * before, axis=-1, keepdims=True)
    r2 = jnp.sum(oh2 * before, axis=-1, keepdims=True)
    eidx_ref[...] = jnp.concatenate([i1, i2], axis=1)
    rank_ref[...] = jnp.concatenate([r1, r2], axis=1).astype(jnp.int32)
    wgt_ref[...] = jnp.concatenate([w1, w2], axis=1)
    cnt_ref[...] = incl[ts - 1:ts, :].astype(jnp.int32)


def moe_route(x, nw, sh, sc, router_w, router_b, *, ts):
    bsz, s, d = x.shape
    ne = router_w.shape[1]
    nt = s // ts
    lt = jnp.asarray(np.tril(np.ones((ts, ts), np.float32)), BF16)
    tok = lambda dt: jax.ShapeDtypeStruct((bsz * s, TOP_K), dt)
    tok_spec = pl.BlockSpec((ts, TOP_K), lambda b, i: (b * nt + i, 0))
    return pl.pallas_call(
        _router_kernel,
        grid=(bsz, nt),
        in_specs=[pl.BlockSpec((None, ts, d), lambda b, i: (b, i, 0)),
                  pl.BlockSpec((1, d), lambda b, i: (0, 0)),
                  pl.BlockSpec((None, 1, d), lambda b, i: (b, 0, 0)),
                  pl.BlockSpec((None, 1, d), lambda b, i: (b, 0, 0)),
                  pl.BlockSpec((d, ne), lambda b, i: (0, 0)),
                  pl.BlockSpec((1, ne), lambda b, i: (0, 0)),
                  pl.BlockSpec((ts, ts), lambda b, i: (0, 0))],
        out_specs=[tok_spec, tok_spec, tok_spec,
                   pl.BlockSpec((None, 1, ne), lambda b, i: (b * nt + i, 0, 0))],
        out_shape=[tok(jnp.int32), tok(jnp.int32), tok(F32),
                   jax.ShapeDtypeStruct((bsz * nt, 1, ne), jnp.int32)],
        compiler_params=_cparams("parallel", "parallel"),
        name="moe_router",
    )(x, nw.reshape(1, d), sh, sc, router_w, router_b.reshape(1, ne), lt)


SEG_ROWS = 16
SEG_FIELDS = 3


def _segment_copies(seg_ref, tile, n_experts, make_copy, *, wait):
    for e in range(n_experts):
        base = (tile * n_experts + e) * SEG_FIELDS
        local0 = seg_ref[base]
        global0 = seg_ref[base + 1]

        def body(i, carry, local0=local0, global0=global0):
            cp = make_copy(pl.multiple_of(local0 + i * SEG_ROWS, SEG_ROWS),
                           pl.multiple_of(global0 + i * SEG_ROWS, SEG_ROWS))
            if wait:
                cp.wait()
            else:
                cp.start()
            return carry

        lax.fori_loop(0, seg_ref[base + 2], body, 0)


def _dispatch_kernel(seg_ref, x_ref, nw_ref, sh_ref, sc_ref, ld_ref, hs_in_ref, hs_ref, buf_ref, sem):
    del hs_in_ref
    tt = x_ref.shape[0]
    lc = buf_ref.shape[1]
    ne = N_EXPERTS
    tile = pl.program_id(0) * pl.num_programs(1) + pl.program_id(1)
    last = pl.num_programs(0) * pl.num_programs(1) - 1
    slot = tile % 2

    def copies(t, sl, wait):
        def make_copy(lo, go):
            return pltpu.make_async_copy(buf_ref.at[sl, pl.ds(lo, SEG_ROWS), :],
                                         hs_ref.at[pl.ds(go, SEG_ROWS), :], sem.at[sl])
        _segment_copies(seg_ref, t, ne, make_copy, wait=wait)

    h = _modnorm(x_ref[...], nw_ref[...], sh_ref[...], sc_ref[...]).astype(BF16)
    ld = ld_ref[...]
    rows = lax.broadcasted_iota(jnp.int32, (lc, tt), 0)
    perm = jnp.where(rows == ld[0:1, :], 1.0, jnp.where(rows == ld[1:2, :], 1.0, 0.0)).astype(BF16)
    buf_ref[slot] = jnp.dot(perm, h, preferred_element_type=F32).astype(BF16)
    copies(tile, slot, wait=False)

    @pl.when(tile > 0)
    def _():
        copies(tile - 1, 1 - slot, wait=True)

    @pl.when(tile == last)
    def _():
        copies(tile, slot, wait=True)


def moe_dispatch(x, nw, sh, sc, seg, ldest_rows, n_rows, *, tt, lc):
    bsz, s, d = x.shape
    nt = s // tt
    hs0 = jnp.zeros((n_rows, d), BF16)
    grid_spec = pltpu.PrefetchScalarGridSpec(
        num_scalar_prefetch=1,
        grid=(bsz, nt),
        in_specs=[pl.BlockSpec((None, tt, d), lambda b, i, sref: (b, i, 0)),
                  pl.BlockSpec((1, d), lambda b, i, sref: (0, 0)),
                  pl.BlockSpec((None, 1, d), lambda b, i, sref: (b, 0, 0)),
                  pl.BlockSpec((None, 1, d), lambda b, i, sref: (b, 0, 0)),
                  pl.BlockSpec((TOP_K, tt), lambda b, i, sref: (0, b * nt + i)),
                  pl.BlockSpec(memory_space=pl.ANY)],
        out_specs=pl.BlockSpec(memory_space=pl.ANY),
        scratch_shapes=[pltpu.VMEM((2, lc, d), BF16), pltpu.SemaphoreType.DMA((2,))],
    )
    return pl.pallas_call(
        _dispatch_kernel,
        grid_spec=grid_spec,
        out_shape=jax.ShapeDtypeStruct((n_rows, d), BF16),
        input_output_aliases={6: 0},
        compiler_params=_cparams("arbitrary", "arbitrary"),
        name="moe_dispatch",
    )(seg, x, nw.reshape(1, d), sh, sc, ldest_rows, hs0)


def _moe_kernel(te_ref, nu_ref, hs_ref, w1_ref, w3_ref, w2_ref, o_ref, acc_ref):
    i = pl.program_id(0)
    f = pl.program_id(1)

    @pl.when(i < nu_ref[0])
    def _():
        h = hs_ref[...]
        a = jnp.dot(h, w1_ref[...], preferred_element_type=F32)
        b = jnp.dot(h, w3_ref[...], preferred_element_type=F32)
        t = (_silu(a) * b).astype(BF16)
        contrib = jnp.dot(t, w2_ref[...], preferred_element_type=F32)

        @pl.when(f == 0)
        def _():
            acc_ref[...] = contrib

        @pl.when(f > 0)
        def _():
            acc_ref[...] += contrib

        @pl.when(f == pl.num_programs(1) - 1)
        def _():
            o_ref[...] = acc_ref[...].astype(o_ref.dtype)

    @pl.when((i >= nu_ref[0]) & (f == 0))
    def _():
        o_ref[...] = jnp.zeros_like(o_ref)


def moe_experts(hs, tile_expert, n_used, w1, w3, w2, *, tm, tf):
    n_rows, d = hs.shape
    dff = w1.shape[2]
    nf = dff // tf
    n_tiles = n_rows // tm

    def last_used(i, nu):
        return jnp.maximum(jnp.minimum(i, nu[0] - 1), 0)

    def row_map(i, f, te, nu):
        return (last_used(i, nu), 0)

    def w_in_map(i, f, te, nu):
        return (te[last_used(i, nu)], 0, jnp.where(i < nu[0], f, nf - 1))

    def w_out_map(i, f, te, nu):
        return (te[last_used(i, nu)], jnp.where(i < nu[0], f, nf - 1), 0)

    grid_spec = pltpu.PrefetchScalarGridSpec(
        num_scalar_prefetch=2,
        grid=(n_tiles, nf),
        in_specs=[pl.BlockSpec((tm, d), row_map),
                  pl.BlockSpec((None, d, tf), w_in_map),
                  pl.BlockSpec((None, d, tf), w_in_map),
                  pl.BlockSpec((None, tf, d), w_out_map)],
        out_specs=pl.BlockSpec((tm, d), lambda i, f, te, nu: (i, 0)),
        scratch_shapes=[pltpu.VMEM((tm, d), F32)],
    )
    return pl.pallas_call(
        _moe_kernel,
        grid_spec=grid_spec,
        out_shape=jax.ShapeDtypeStruct((n_rows, d), BF16),
        compiler_params=_cparams("arbitrary", "arbitrary"),
        name="moe_experts",
    )(tile_expert, n_used, hs, w1, w3, w2)


def _combine_kernel(seg_ref, ys_ref, x_ref, g_ref, wgt_ref, ld_ref, o_ref, buf_ref, sem):
    tt = x_ref.shape[0]
    lc = buf_ref.shape[1]
    ne = N_EXPERTS
    tile = pl.program_id(0) * pl.num_programs(1) + pl.program_id(1)
    last = pl.num_programs(0) * pl.num_programs(1) - 1
    slot = tile % 2

    def copies(t, sl, wait):
        def make_copy(lo, go):
            return pltpu.make_async_copy(ys_ref.at[pl.ds(go, SEG_ROWS), :],
                                         buf_ref.at[sl, pl.ds(lo, SEG_ROWS), :], sem.at[sl])
        _segment_copies(seg_ref, t, ne, make_copy, wait=wait)

    @pl.when(tile == 0)
    def _():
        buf_ref[...] = jnp.zeros_like(buf_ref)
        copies(tile, slot, wait=False)

    @pl.when(tile < last)
    def _():
        copies(tile + 1, 1 - slot, wait=False)

    copies(tile, slot, wait=True)

    ld = ld_ref[...]
    cols = lax.broadcasted_iota(jnp.int32, (tt, lc), 1)
    pick = jnp.concatenate([jnp.where(cols == ld[:, k:k + 1], 1.0, 0.0) for k in range(TOP_K)],
                           axis=0).astype(BF16)
    z = jnp.dot(pick, buf_ref[slot], preferred_element_type=F32)
    w = wgt_ref[...]
    mix = w[:, 0:1] * z[0:tt] + w[:, 1:2] * z[tt:]
    o_ref[...] = x_ref[...] + g_ref[...] * mix


def moe_combine(ys, seg, ldest, wgt, x, g, *, tt, lc):
    bsz, s, d = x.shape
    nt = s // tt
    tok_spec = pl.BlockSpec((tt, TOP_K), lambda b, i, sref: (b * nt + i, 0))
    grid_spec = pltpu.PrefetchScalarGridSpec(
        num_scalar_prefetch=1,
        grid=(bsz, nt),
        in_specs=[pl.BlockSpec(memory_space=pl.ANY),
                  pl.BlockSpec((None, tt, d), lambda b, i, sref: (b, i, 0)),
                  pl.BlockSpec((None, 1, d), lambda b, i, sref: (b, 0, 0)),
                  tok_spec, tok_spec],
        out_specs=pl.BlockSpec((None, tt, d), lambda b, i, sref: (b, i, 0)),
        scratch_shapes=[pltpu.VMEM((2, lc, d), BF16), pltpu.SemaphoreType.DMA((2,))],
    )
    return pl.pallas_call(
        _combine_kernel,
        grid_spec=grid_spec,
        out_shape=jax.ShapeDtypeStruct((bsz, s, d), F32),
        compiler_params=_cparams("arbitrary", "arbitrary"),
        name="moe_combine",
    )(seg, ys, x, g, wgt, ldest)


def _round_up(v, m):
    return ((v + m - 1) // m) * m


def moe_residual(x, nw, sh, sc, g, router_w, router_b, w1, w3, w2, *, tm=MOE_TILE_ROWS):
    bsz, s, d = x.shape
    n_tok = bsz * s
    ne = router_w.shape[1]
    tt = min(512, s)
    n_tt = n_tok // tt
    lc = _round_up(TOP_K * tt + ne * SEG_ROWS, 128)
    eidx, rank, wgt, cnt = moe_route(x, nw, sh, sc, router_w, router_b, ts=tt)
    seg_len = _round_up(cnt.reshape(n_tt, ne), SEG_ROWS)
    local_start = jnp.cumsum(seg_len, axis=1) - seg_len
    padded = _round_up(jnp.sum(seg_len, axis=0), tm)
    ends = jnp.cumsum(padded)
    global_start = (ends - padded)[None, :] + jnp.cumsum(seg_len, axis=0) - seg_len
    seg = jnp.stack([local_start, global_start, seg_len // SEG_ROWS], axis=-1).reshape(-1).astype(jnp.int32)
    onehot = eidx[:, :, None] == jnp.arange(ne, dtype=jnp.int32)
    start_tok = jnp.repeat(local_start, tt, axis=0)[:, None, :]
    ldest = (jnp.sum(jnp.where(onehot, start_tok, 0), axis=-1) + rank).astype(jnp.int32)
    n_rows = _round_up(n_tok * TOP_K + n_tt * ne * SEG_ROWS + ne * tm, tm)
    n_tiles = n_rows // tm
    tile_start = jnp.arange(n_tiles, dtype=jnp.int32) * tm
    tile_expert = jnp.minimum(jnp.sum(tile_start[:, None] >= ends[None, :], axis=1), ne - 1).astype(jnp.int32)
    n_used = (ends[ne - 1:ne] // tm).astype(jnp.int32)
    hs = moe_dispatch(x, nw, sh, sc, seg, ldest.T, n_rows, tt=tt, lc=lc)
    ys = moe_experts(hs, tile_expert, n_used, w1, w3, w2, tm=tm, tf=w1.shape[2] // 2)
    return moe_combine(ys, seg, ldest, wgt, x, g, tt=tt, lc=lc)


def _split_mod(mod):
    return [m[:, None, :] for m in jnp.split(mod, 6, axis=-1)]


def even_layer(x, c, rel_bias, ada_w, ada_b, norm1_w, in_w, q_norm_w, k_norm_w, sink, out_w,
               norm2_w, w1, w3, w2):
    s = x.shape[1]
    sh1, sc1, g1, sh2, sc2, g2 = _split_mod(ada_mod(c, ada_w, ada_b))
    proj = norm_mod_matmul(x, norm1_w, sh1, sc1, in_w.astype(BF16), ts=min(1024, s), tn=640, name="even_in_proj")
    yf = fourier_mix(proj, tq=min(512, s))
    ya = window_attention(proj, band_bias(rel_bias), q_norm_w, k_norm_w, sink)
    x = cat_proj_residual(yf, ya, out_w.astype(BF16), x, g1, ts=min(1024, s))
    dff = w1.shape[1]
    return ffn_residual(x, norm2_w, sh2, sc2, g2, w1.astype(BF16), w3.astype(BF16), w2.astype(BF16),
                        ts=min(1024, s), tf=dff // 2)


def odd_layer(x, c, ada_w, ada_b, norm1_w, in_w, conv_w, conv_b, dt_bias_f, dt_bias_b, a_log_f, a_log_b,
              d_skip, gnorm_w, out_w, norm2_w, router_w, router_b, w1, w3, w2):
    s = x.shape[1]
    sh1, sc1, g1, sh2, sc2, g2 = _split_mod(ada_mod(c, ada_w, ada_b))
    d_inner = gnorm_w.shape[0]
    cdim = conv_w.shape[1]
    wide = d_inner + cdim
    in_w = in_w.astype(BF16)
    zx = norm_mod_matmul(x, norm1_w, sh1, sc1, in_w[:, :wide], ts=min(1024, s), tn=1024, name="odd_in_proj")
    dt = norm_mod_matmul(x, norm1_w, sh1, sc1, in_w[:, wide:], ts=min(1024, s), tn=in_w.shape[1] - wide,
                         name="odd_dt_proj")
    xbc = conv_silu(zx, conv_w, conv_b, col0=d_inner, tc=512)
    y = ssd_scan_bidir(xbc, dt, dt_bias_f, dt_bias_b, a_log_f, a_log_b, d_skip)
    x = gated_proj_residual(y, zx, gnorm_w, out_w.astype(BF16), x, g1, ts=min(512, s))
    return moe_residual(x, norm2_w, sh2, sc2, g2, router_w, router_b,
                        w1.astype(BF16), w3.astype(BF16), w2.astype(BF16))


def kernel(x, c, rel_bias, ev_ada_w, ev_ada_b, ev_norm1_w, ev_in_w, ev_q_norm_w, ev_k_norm_w, ev_sink, ev_out_w, ev_norm2_w, ev_ffn_w1, ev_ffn_w3, ev_ffn_w2, od_ada_w, od_ada_b, od_norm1_w, od_in_w, od_conv_w, od_conv_b, od_dt_bias_f, od_dt_bias_b, od_A_log_f, od_A_log_b, od_D, od_gnorm_w, od_out_w, od_norm2_w, od_router_w, od_router_b, od_moe_w1, od_moe_w3, od_moe_w2):
    depth = ev_ada_w.shape[0] + od_ada_w.shape[0]
    for i in range(depth):
        j = i // 2
        if i % 2 == 0:
            x = even_layer(x, c, rel_bias, ev_ada_w[j], ev_ada_b[j], ev_norm1_w[j], ev_in_w[j],
                           ev_q_norm_w[j], ev_k_norm_w[j], ev_sink[j], ev_out_w[j], ev_norm2_w[j],
                           ev_ffn_w1[j], ev_ffn_w3[j], ev_ffn_w2[j])
        else:
            x = odd_layer(x, c, od_ada_w[j], od_ada_b[j], od_norm1_w[j], od_in_w[j], od_conv_w[j],
                          od_conv_b[j], od_dt_bias_f[j], od_dt_bias_b[j], od_A_log_f[j], od_A_log_b[j],
                          od_D[j], od_gnorm_w[j], od_out_w[j], od_norm2_w[j], od_router_w[j],
                          od_router_b[j], od_moe_w1[j], od_moe_w3[j], od_moe_w2[j])
    return x
```

```python
import functools

import numpy as np
import jax
import jax.numpy as jnp
from jax import lax
from jax.experimental import pallas as pl
from jax.experimental.pallas import tpu as pltpu

F32 = jnp.float32
BF16 = jnp.bfloat16
HIGHEST = lax.Precision.HIGHEST

EPS = 1e-6
FNET_GROUPS = 4
FNET_GROUP_DIM = 128
FNET_WIDTH = FNET_GROUPS * FNET_GROUP_DIM
ATTN_HEADS = 8
ATTN_KV_HEADS = 2
HEAD_DIM = 64
ATTN_WIDTH = ATTN_HEADS * HEAD_DIM
KV_WIDTH = ATTN_KV_HEADS * HEAD_DIM
WINDOW = 128
BLOCK = 128
REL_BUCKETS = 32
REL_MAX_DIST = 128
SSM_HEAD_DIM = 64
SSM_GROUPS = 4
D_STATE = 128
CONV_WIDTH = 5
SSD_CHUNK = 128
N_EXPERTS = 8
TOP_K = 2
NEG_BIG = -1e30
ATTN_QBLOCKS = 2

V7X_VMEM_LIMIT_BYTES = 56 * 1024 * 1024
MOE_TILE_ROWS = 512


def _cparams(*sem):
    return pltpu.CompilerParams(dimension_semantics=sem, vmem_limit_bytes=V7X_VMEM_LIMIT_BYTES)


def _modnorm(x, nw, sh, sc):
    ms = jnp.mean(x * x, axis=-1, keepdims=True)
    return x * lax.rsqrt(ms + EPS) * nw * (1.0 + sc) + sh


def _silu(x):
    return x * (1.0 / (1.0 + jnp.exp(-x)))


def _ada_kernel(c_ref, w_ref, b_ref, o_ref):
    cs = _silu(c_ref[...]).astype(BF16)
    o_ref[...] = jnp.dot(cs, w_ref[...].astype(BF16), preferred_element_type=F32) + b_ref[...]


def ada_mod(c, w, b):
    bsz, d = c.shape
    n = w.shape[1]
    tn = 1536
    return pl.pallas_call(
        _ada_kernel,
        grid=(n // tn,),
        in_specs=[pl.BlockSpec((bsz, d), lambda j: (0, 0)),
                  pl.BlockSpec((d, tn), lambda j: (0, j)),
                  pl.BlockSpec((1, tn), lambda j: (0, j))],
        out_specs=pl.BlockSpec((bsz, tn), lambda j: (0, j)),
        out_shape=jax.ShapeDtypeStruct((bsz, n), F32),
        compiler_params=_cparams("arbitrary"),
        name="ada_mod",
    )(c, w, b.reshape(1, n))


def _nmm_kernel(x_ref, nw_ref, sh_ref, sc_ref, w_ref, o_ref, h_ref):
    @pl.when(pl.program_id(2) == 0)
    def _():
        h_ref[...] = _modnorm(x_ref[...], nw_ref[...], sh_ref[...], sc_ref[...]).astype(BF16)

    o_ref[...] = jnp.dot(h_ref[...], w_ref[...], preferred_element_type=F32).astype(o_ref.dtype)


def norm_mod_matmul(x, nw, sh, sc, w, *, ts, tn, name, out_dtype=F32):
    bsz, s, d = x.shape
    n = w.shape[1]
    return pl.pallas_call(
        _nmm_kernel,
        grid=(bsz, s // ts, n // tn),
        in_specs=[pl.BlockSpec((None, ts, d), lambda b, i, j: (b, i, 0)),
                  pl.BlockSpec((1, d), lambda b, i, j: (0, 0)),
                  pl.BlockSpec((None, 1, d), lambda b, i, j: (b, 0, 0)),
                  pl.BlockSpec((None, 1, d), lambda b, i, j: (b, 0, 0)),
                  pl.BlockSpec((d, tn), lambda b, i, j: (0, j))],
        out_specs=pl.BlockSpec((None, ts, tn), lambda b, i, j: (b, i, j)),
        out_shape=jax.ShapeDtypeStruct((bsz, s, n), out_dtype),
        scratch_shapes=[pltpu.VMEM((ts, d), BF16)],
        compiler_params=_cparams("parallel", "parallel", "arbitrary"),
        name=name,
    )(x, nw.reshape(1, d), sh, sc, w)


def _dft_cos_sin(n):
    k = np.arange(n, dtype=np.int64)
    ang = ((k[:, None] * k[None, :]) % n).astype(np.float64) * (2.0 * np.pi / n)
    scale = 1.0 / np.sqrt(n)
    return np.cos(ang) * scale, np.sin(ang) * scale


def _fourier_kernel(u_ref, chan_ref, seq_ref, o_ref, ab_ref):
    s = u_ref.shape[0]

    @pl.when(pl.program_id(1) == 0)
    def _():
        for g in range(FNET_GROUPS):
            lo, hi = g * FNET_GROUP_DIM, (g + 1) * FNET_GROUP_DIM
            ug = u_ref[:, lo:hi].astype(BF16)
            cs = jnp.dot(ug, chan_ref[...], preferred_element_type=F32)
            ab_ref[0:s, lo:hi] = cs[:, :FNET_GROUP_DIM].astype(BF16)
            ab_ref[s:2 * s, lo:hi] = cs[:, FNET_GROUP_DIM:].astype(BF16)

    o_ref[...] = jnp.dot(seq_ref[...], ab_ref[...], preferred_element_type=F32)


def fourier_mix(proj, *, tq):
    bsz, s, _ = proj.shape
    cc, sc = _dft_cos_sin(FNET_GROUP_DIM)
    chan = jnp.asarray(np.concatenate([cc, sc], axis=1), BF16)
    cs, ss = _dft_cos_sin(s)
    seq = jnp.asarray(np.concatenate([cs, -ss], axis=1), BF16)
    return pl.pallas_call(
        _fourier_kernel,
        grid=(bsz, s // tq),
        in_specs=[pl.BlockSpec((None, s, FNET_WIDTH), lambda b, i: (b, 0, 0)),
                  pl.BlockSpec((FNET_GROUP_DIM, 2 * FNET_GROUP_DIM), lambda b, i: (0, 0)),
                  pl.BlockSpec((tq, 2 * s), lambda b, i: (i, 0))],
        out_specs=pl.BlockSpec((None, tq, FNET_WIDTH), lambda b, i: (b, i, 0)),
        out_shape=jax.ShapeDtypeStruct((bsz, s, FNET_WIDTH), F32),
        scratch_shapes=[pltpu.VMEM((2 * s, FNET_WIDTH), BF16)],
        compiler_params=_cparams("parallel", "arbitrary"),
        name="fourier_mix",
    )(proj, chan, seq)


def _band_bucket_table():
    i = np.arange(BLOCK)[:, None]
    j = np.arange(3 * BLOCK)[None, :]
    rel = (j - BLOCK) - i
    half = REL_BUCKETS // 2
    max_exact = half // 2
    n = np.abs(rel)
    large = max_exact + (np.log(np.maximum(n, 1) / max_exact)
                         / np.log(REL_MAX_DIST / max_exact) * (half - max_exact)).astype(np.int32)
    large = np.minimum(large, half - 1)
    bucket = (rel > 0).astype(np.int32) * half + np.where(n < max_exact, n, large)
    return np.where(n <= WINDOW, bucket, -1).astype(np.int32)


def _bias_kernel(rb_ref, bucket_ref, o_ref):
    h = pl.program_id(0)
    bucket = bucket_ref[...]
    acc = jnp.full(bucket.shape, NEG_BIG, F32)
    for bkt in range(REL_BUCKETS):
        acc = jnp.where(bucket == bkt, rb_ref[bkt * ATTN_HEADS + h], acc)
    o_ref[...] = acc


def band_bias(rel_bias):
    bucket = jnp.asarray(_band_bucket_table())
    return pl.pallas_call(
        _bias_kernel,
        grid=(ATTN_HEADS,),
        in_specs=[pl.BlockSpec(memory_space=pltpu.SMEM),
                  pl.BlockSpec((BLOCK, 3 * BLOCK), lambda h: (0, 0))],
        out_specs=pl.BlockSpec((None, BLOCK, 3 * BLOCK), lambda h: (h, 0, 0)),
        out_shape=jax.ShapeDtypeStruct((ATTN_HEADS, BLOCK, 3 * BLOCK), F32),
        compiler_params=_cparams("arbitrary"),
        name="band_bias",
    )(rel_bias.reshape(-1), bucket)


def _head_mean_matrix(width):
    m = np.zeros((width, width), np.float32)
    for h in range(width // HEAD_DIM):
        m[h * HEAD_DIM:(h + 1) * HEAD_DIM, h * HEAD_DIM:(h + 1) * HEAD_DIM] = 1.0 / HEAD_DIM
    return m


def _heads_rms(t, mean_mat, w):
    sq = t * t
    hi = sq.astype(BF16)
    lo = (sq - hi.astype(F32)).astype(BF16)
    ms = (jnp.dot(hi, mean_mat, preferred_element_type=F32)
          + jnp.dot(lo, mean_mat, preferred_element_type=F32))
    return t * lax.rsqrt(ms + EPS) * w


def _attn_kernel(sink_ref, q_ref, kl_ref, kc_ref, kr_ref, vl_ref, vc_ref, vr_ref,
                 bias_ref, qnw_ref, knw_ref, qmean_ref, kmean_ref, o_ref):
    n = pl.program_id(1)
    nb = pl.num_programs(1) * ATTN_QBLOCKS
    k = jnp.concatenate([kl_ref[...], kc_ref[...], kr_ref[...]], axis=0).astype(F32)
    v = jnp.concatenate([vl_ref[...], vc_ref[...], vr_ref[...]], axis=0).astype(F32)
    col = lax.broadcasted_iota(jnp.int32, (1, 3 * BLOCK), 1)
    qn = _heads_rms(q_ref[...].astype(F32), qmean_ref[...], qnw_ref[...])
    kn = _heads_rms(k, kmean_ref[...], knw_ref[...])
    low = lax.broadcasted_iota(jnp.int32, (1, 2 * HEAD_DIM), 1) < HEAD_DIM
    kn_sw = pltpu.roll(kn, HEAD_DIM, axis=1)
    v_sw = pltpu.roll(v, HEAD_DIM, axis=1)
    k_dup = [jnp.where(low, kn, kn_sw).astype(BF16), jnp.where(low, kn_sw, kn).astype(BF16)]
    ones = jnp.ones((k.shape[0], 2 * HEAD_DIM), BF16)
    v_ext = [jnp.concatenate([v.astype(BF16), ones], axis=1),
             jnp.concatenate([v_sw.astype(BF16), ones], axis=1)]
    g = ATTN_HEADS // ATTN_KV_HEADS
    for qb in range(ATTN_QBLOCKS):
        blk = n * ATTN_QBLOCKS + qb
        band = slice(qb * BLOCK, (qb + 3) * BLOCK)
        first_key = jnp.where(blk == 0, BLOCK, 0)
        end_key = jnp.where(blk == nb - 1, 2 * BLOCK, 3 * BLOCK)
        outside = (col < first_key) | (col >= end_key)
        pairs = []
        for m in range(ATTN_HEADS // 2):
            j = (2 * m) // g
            qp = qn[qb * BLOCK:(qb + 1) * BLOCK, m * 2 * HEAD_DIM:(m + 1) * 2 * HEAD_DIM]
            res = []
            for idx in range(2):
                h = 2 * m + idx
                qm = jnp.where(low if idx == 0 else jnp.logical_not(low), qp, 0.0).astype(BF16)
                logits = lax.dot_general(qm, k_dup[j][band], (((1,), (1,)), ((), ())),
                                         preferred_element_type=F32)
                logits = jnp.where(outside, NEG_BIG, logits + bias_ref[h])
                sk = sink_ref[h]
                mx = jnp.maximum(jnp.max(logits, axis=-1, keepdims=True), sk)
                p = jnp.exp(logits - mx).astype(BF16)
                r = jnp.dot(p, v_ext[idx if j == 0 else 1 - idx][band], preferred_element_type=F32)
                denom = r[:, 2 * HEAD_DIM:] + jnp.exp(sk - mx)
                res.append(r[:, :2 * HEAD_DIM] / denom)
            pairs.append(jnp.where(low, res[0], res[1]))
        o_ref[qb * BLOCK:(qb + 1) * BLOCK, :] = jnp.concatenate(pairs, axis=-1)


def window_attention(proj, bias, q_norm_w, k_norm_w, sink):
    bsz, s, _ = proj.shape
    nb = s // BLOCK
    qcol = FNET_WIDTH // ATTN_WIDTH
    kcol = (FNET_WIDTH + ATTN_WIDTH) // KV_WIDTH
    vcol = kcol + 1

    qb = ATTN_QBLOCKS
    assert nb % qb == 0

    def kv_specs(col):
        return [pl.BlockSpec((None, BLOCK, KV_WIDTH), lambda b, n: (b, jnp.maximum(n * qb - 1, 0), col)),
                pl.BlockSpec((None, qb * BLOCK, KV_WIDTH), lambda b, n: (b, n, col)),
                pl.BlockSpec((None, BLOCK, KV_WIDTH), lambda b, n: (b, jnp.minimum((n + 1) * qb, nb - 1), col))]

    return pl.pallas_call(
        _attn_kernel,
        grid=(bsz, nb // qb),
        in_specs=[pl.BlockSpec(memory_space=pltpu.SMEM),
                  pl.BlockSpec((None, qb * BLOCK, ATTN_WIDTH), lambda b, n: (b, n, qcol)),
                  *kv_specs(kcol), *kv_specs(vcol),
                  pl.BlockSpec((ATTN_HEADS, BLOCK, 3 * BLOCK), lambda b, n: (0, 0, 0)),
                  pl.BlockSpec((1, ATTN_WIDTH), lambda b, n: (0, 0)),
                  pl.BlockSpec((1, KV_WIDTH), lambda b, n: (0, 0)),
                  pl.BlockSpec((ATTN_WIDTH, ATTN_WIDTH), lambda b, n: (0, 0)),
                  pl.BlockSpec((KV_WIDTH, KV_WIDTH), lambda b, n: (0, 0))],
        out_specs=pl.BlockSpec((None, qb * BLOCK, ATTN_WIDTH), lambda b, n: (b, n, 0)),
        out_shape=jax.ShapeDtypeStruct((bsz, s, ATTN_WIDTH), F32),
        compiler_params=_cparams("parallel", "arbitrary"),
        name="window_attention",
    )(sink, proj, proj, proj, proj, proj, proj, proj, bias,
      (jnp.tile(q_norm_w, ATTN_HEADS) * (HEAD_DIM ** -0.5)).reshape(1, ATTN_WIDTH),
      jnp.tile(k_norm_w, ATTN_KV_HEADS).reshape(1, KV_WIDTH),
      jnp.asarray(_head_mean_matrix(ATTN_WIDTH), BF16), jnp.asarray(_head_mean_matrix(KV_WIDTH), BF16))


def _cat_proj_kernel(a1_ref, a2_ref, w_ref, x_ref, g_ref, o_ref):
    k1 = a1_ref.shape[1]
    y = jnp.dot(a1_ref[...].astype(BF16), w_ref[0:k1, :], preferred_element_type=F32)
    y = y + jnp.dot(a2_ref[...].astype(BF16), w_ref[k1:, :], preferred_element_type=F32)
    o_ref[...] = x_ref[...] + g_ref[...] * y


def cat_proj_residual(a1, a2, w, x, g, *, ts):
    bsz, s, d = x.shape
    k1, k2 = a1.shape[2], a2.shape[2]
    return pl.pallas_call(
        _cat_proj_kernel,
        grid=(bsz, s // ts),
        in_specs=[pl.BlockSpec((None, ts, k1), lambda b, i: (b, i, 0)),
                  pl.BlockSpec((None, ts, k2), lambda b, i: (b, i, 0)),
                  pl.BlockSpec((k1 + k2, d), lambda b, i: (0, 0)),
                  pl.BlockSpec((None, ts, d), lambda b, i: (b, i, 0)),
                  pl.BlockSpec((None, 1, d), lambda b, i: (b, 0, 0))],
        out_specs=pl.BlockSpec((None, ts, d), lambda b, i: (b, i, 0)),
        out_shape=jax.ShapeDtypeStruct((bsz, s, d), F32),
        compiler_params=_cparams("parallel", "parallel"),
        name="mixer_out_proj",
    )(a1, a2, w, x, g)


def _ffn_kernel(x_ref, nw_ref, sh_ref, sc_ref, g_ref, w1_ref, w3_ref, w2_ref, o_ref, h_ref, acc_ref):
    f = pl.program_id(2)

    @pl.when(f == 0)
    def _():
        h_ref[...] = _modnorm(x_ref[...], nw_ref[...], sh_ref[...], sc_ref[...]).astype(BF16)

    h = h_ref[...]
    a = jnp.dot(h, w1_ref[...], preferred_element_type=F32)
    b = jnp.dot(h, w3_ref[...], preferred_element_type=F32)
    t = (_silu(a) * b).astype(BF16)
    contrib = jnp.dot(t, w2_ref[...], preferred_element_type=F32)

    @pl.when(f == 0)
    def _():
        acc_ref[...] = contrib

    @pl.when(f > 0)
    def _():
        acc_ref[...] += contrib

    @pl.when(f == pl.num_programs(2) - 1)
    def _():
        o_ref[...] = x_ref[...] + g_ref[...] * acc_ref[...]


def ffn_residual(x, nw, sh, sc, g, w1, w3, w2, *, ts, tf):
    bsz, s, d = x.shape
    dff = w1.shape[1]
    vec = pl.BlockSpec((None, 1, d), lambda b, i, f: (b, 0, 0))
    return pl.pallas_call(
        _ffn_kernel,
        grid=(bsz, s // ts, dff // tf),
        in_specs=[pl.BlockSpec((None, ts, d), lambda b, i, f: (b, i, 0)),
                  pl.BlockSpec((1, d), lambda b, i, f: (0, 0)),
                  vec, vec, vec,
                  pl.BlockSpec((d, tf), lambda b, i, f: (0, f)),
                  pl.BlockSpec((d, tf), lambda b, i, f: (0, f)),
                  pl.BlockSpec((tf, d), lambda b, i, f: (f, 0))],
        out_specs=pl.BlockSpec((None, ts, d), lambda b, i, f: (b, i, 0)),
        out_shape=jax.ShapeDtypeStruct((bsz, s, d), F32),
        scratch_shapes=[pltpu.VMEM((ts, d), BF16), pltpu.VMEM((ts, d), F32)],
        compiler_params=_cparams("parallel", "parallel", "arbitrary"),
        name="ffn_swiglu",
    )(x, nw.reshape(1, d), sh, sc, g, w1, w3, w2)


CONV_PAD = 8
CONV_ROWS = 256


def _conv_kernel(x_ref, w_ref, b_ref, o_ref, pad_ref):
    s, tc = x_ref.shape
    zeros = jnp.zeros((CONV_PAD, tc), F32)
    pad_ref[0:CONV_PAD, :] = zeros
    pad_ref[CONV_PAD + s:, :] = zeros
    pad_ref[CONV_PAD:CONV_PAD + s, :] = x_ref[...].astype(F32)
    half = CONV_WIDTH // 2
    for r in range(s // CONV_ROWS):
        base = CONV_PAD + r * CONV_ROWS - half
        acc = jnp.zeros((CONV_ROWS, tc), F32) + b_ref[...]
        for kk in range(CONV_WIDTH):
            acc = acc + pad_ref[base + kk:base + kk + CONV_ROWS, :] * w_ref[kk:kk + 1, :]
        o_ref[r * CONV_ROWS:(r + 1) * CONV_ROWS, :] = _silu(acc)


def conv_silu(proj, conv_w, conv_b, *, col0, tc):
    bsz, s, _ = proj.shape
    cdim = conv_w.shape[1]
    cb0 = col0 // tc
    return pl.pallas_call(
        _conv_kernel,
        grid=(bsz, cdim // tc),
        in_specs=[pl.BlockSpec((None, s, tc), lambda b, j: (b, 0, cb0 + j)),
                  pl.BlockSpec((CONV_WIDTH, tc), lambda b, j: (0, j)),
                  pl.BlockSpec((1, tc), lambda b, j: (0, j))],
        out_specs=pl.BlockSpec((None, s, tc), lambda b, j: (b, 0, j)),
        out_shape=jax.ShapeDtypeStruct((bsz, s, cdim), F32),
        scratch_shapes=[pltpu.VMEM((s + 2 * CONV_PAD, tc), F32)],
        compiler_params=_cparams("parallel", "parallel"),
        name="conv_silu",
    )(proj, conv_w, conv_b.reshape(1, cdim))


def _softplus(x):
    return jnp.maximum(x, 0.0) + jnp.log(1.0 + jnp.exp(-jnp.abs(x)))


def _expand_matrix(n_in, width):
    e = np.zeros((n_in, n_in * width), np.float32)
    for h in range(n_in):
        e[h, h * width:(h + 1) * width] = 1.0
    return e


LOG2E = 1.4426950408889634
DECAY_SLOTS = 12


def _decay_placement(nh):
    place = np.zeros((3, 2 * nh, 3 * 128), np.float32)
    const = np.zeros((1, 3 * 128), np.float32)
    for h in range(nh):
        for part in range(3):
            place[part, h, part * nh + h] = 1.0
            place[part, nh + h, (6 + part) * nh + h] = -1.0
            place[part, h, 128 + (3 + part) * nh + h] = -1.0
            place[part, nh + h, 256 + (9 + part) * nh + h] = 1.0
            const[0, (3 + part) * nh + h] = 1.0
            const[0, (9 + part) * nh + h] = 1.0
            const[0, 128 + part * nh + h] = 1.0
            const[0, 256 + (6 + part) * nh + h] = 1.0
    return place, const


def _split3(v):
    hi = v.astype(BF16)
    r = v - hi.astype(F32)
    mid = r.astype(BF16)
    lo = (r - mid.astype(F32)).astype(BF16)
    return hi, mid, lo


def _ssd_kernel(x_ref, b_ref, c_ref, dtc_f_ref, dtc_b_ref, dtr_f_ref, dtr_b_ref,
                pc_f_ref, pc_b_ref, pr_f_ref, pr_b_ref, dx_ref,
                tri_ref, place_ref, pconst_ref, e64_ref, y_ref,
                ac_ref, dc_ref, dr_ref, pq_ref, xdf_ref, xdb_ref, eif_ref, er_ref, decf_ref, decb_ref,
                hf_ref, hb_ref):
    s = x_ref.shape[0]
    q = SSD_CHUNK
    nh = pc_f_ref.shape[1]
    nc = s // q
    hd = SSM_HEAD_DIM

    def col_params(raw_ref, p_ref):
        dt = _softplus(raw_ref[...] + p_ref[0:1, :])
        return dt, (-LOG2E) * jnp.exp(p_ref[1:2, :]) * dt

    dcf, acf = col_params(dtc_f_ref, pc_f_ref)
    dcb, acb = col_params(dtc_b_ref, pc_b_ref)
    dc_ref[:, 0:nh] = dcf
    dc_ref[:, nh:] = dcb
    ac_ref[:, 0:nh] = acf
    ac_ref[:, nh:] = acb
    dr_ref[0:nh, :] = _softplus(dtr_f_ref[...] + pr_f_ref[:, 0:1])
    dr_ref[nh:, :] = _softplus(dtr_b_ref[...] + pr_b_ref[:, 0:1])

    li = lax.broadcasted_iota(jnp.int32, (q, q), 0)
    si = lax.broadcasted_iota(jnp.int32, (q, q), 1)
    lower = li >= si
    upper = li <= si
    slot_head = lax.broadcasted_iota(jnp.int32, (1, 128), 1) % nh
    pair_lo = lax.broadcasted_iota(jnp.int32, (1, 2 * hd), 1) < hd

    def bdot(a, b):
        return jnp.dot(a, b, preferred_element_type=F32)

    def chunk_cumsums(sl):
        a_col = ac_ref[sl, :]
        tri = tri_ref[...]
        hi, mid, lo = _split3(a_col)
        return a_col, bdot(tri, hi) + bdot(tri, mid) + bdot(tri, lo)

    def expand64(v, parts=2):
        e = e64_ref[...]
        hi = v.astype(BF16)
        out = bdot(hi, e)
        if parts == 2:
            out = out + bdot((v - hi.astype(F32)).astype(BF16), e)
        return out

    hf_ref[...] = jnp.zeros_like(hf_ref)
    hb_ref[...] = jnp.zeros_like(hb_ref)

    def prep_body(c, carry):
        sl = pl.ds(pl.multiple_of(c * q, q), q)
        a_col, i_col = chunk_cumsums(sl)
        d_col = dc_ref[sl, :]
        if_col = i_col[:, 0:nh]
        tot_f = i_col[q - 1:q, 0:nh]
        tot_b = i_col[q - 1:q, nh:]
        ie = i_col - jnp.where(lax.broadcasted_iota(jnp.int32, (1, 2 * nh), 1) < nh, 0.0, a_col)
        eb_col = ie[:, nh:]
        parts = _split3(ie)
        placed = pconst_ref[...]
        for part in range(3):
            placed = placed + bdot(parts[part], place_ref[part])
        pq_ref[sl, :] = placed.astype(BF16)
        xc = x_ref[sl, :]
        xdf_ref[sl, :] = (xc * expand64(jnp.exp2(tot_f - if_col) * d_col[:, 0:nh], parts=1)).astype(BF16)
        xdb_ref[sl, :] = (xc * expand64(jnp.exp2(eb_col) * d_col[:, nh:], parts=1)).astype(BF16)
        eif_ref[sl, :] = expand64(jnp.exp2(if_col))
        er_ref[sl, :] = expand64(jnp.exp2(tot_b - eb_col))
        dec = expand64(jnp.exp2(jnp.concatenate([tot_f, tot_b], axis=0)))
        decf_ref[c] = jnp.broadcast_to(dec[0:1, :], decf_ref.shape[1:])
        decb_ref[c] = jnp.broadcast_to(dec[1:2, :], decb_ref.shape[1:])
        return carry

    lax.fori_loop(0, nc, prep_body, 0, unroll=4)

    def fwd_body(c, carry):
        sl = pl.ds(pl.multiple_of(c * q, q), q)
        xc = x_ref[sl, :]
        bc = b_ref[sl, :].astype(BF16)
        cc = c_ref[sl, :].astype(BF16)
        d_row = dr_ref[:, sl]
        placed = pq_ref[sl, :]
        p_all = placed[:, 0:128]
        q_cat = jnp.concatenate([placed[:, 128:256], placed[:, 256:384]], axis=0)

        cb = lax.dot_general(cc, bc, (((1,), (1,)), ((), ())), preferred_element_type=F32)
        xb = xc.astype(BF16)
        ys = []
        for h0 in range(0, nh, 2):
            ms = []
            for h in (h0, h0 + 1):
                ph = jnp.where(slot_head == h, p_all, jnp.zeros_like(p_all))
                g2 = lax.dot_general(ph, q_cat, (((1,), (1,)), ((), ())), preferred_element_type=F32)
                arg = jnp.where(lower, g2[:, 0:q], g2[:, q:])
                wgt = (jnp.where(lower, d_row[h:h + 1, :], 0.0)
                       + jnp.where(upper, d_row[nh + h:nh + h + 1, :], 0.0))
                ms.append((cb * jnp.exp2(arg) * wgt).astype(BF16))
            xp = xb[:, h0 * hd:(h0 + 2) * hd]
            zero = jnp.zeros_like(xp)
            rhs = jnp.concatenate([jnp.where(pair_lo, xp, zero), jnp.where(pair_lo, zero, xp)], axis=0)
            ys.append(bdot(jnp.concatenate(ms, axis=1), rhs))
        y = jnp.concatenate(ys, axis=1) + dx_ref[...] * xc

        states = lax.dot_general(bc, xdf_ref[sl, :], (((0,), (0,)), ((), ())),
                                 preferred_element_type=F32)
        h_prev = hf_ref[...]
        y = y + bdot(cc, h_prev.astype(BF16)) * eif_ref[sl, :]
        hf_ref[...] = h_prev * decf_ref[c][0:1, :] + states
        y_ref[sl, :] = y
        return carry

    lax.fori_loop(0, nc, fwd_body, 0, unroll=2)

    def bwd_body(t, carry):
        c = nc - 1 - t
        sl = pl.ds(pl.multiple_of(c * q, q), q)
        bc = b_ref[sl, :].astype(BF16)
        cc = c_ref[sl, :].astype(BF16)
        states = lax.dot_general(bc, xdb_ref[sl, :], (((0,), (0,)), ((), ())),
                                 preferred_element_type=F32)
        h_prev = hb_ref[...]
        y_ref[sl, :] += bdot(cc, h_prev.astype(BF16)) * er_ref[sl, :]
        hb_ref[...] = h_prev * decb_ref[c][0:1, :] + states
        return carry

    lax.fori_loop(0, nc, bwd_body, 0, unroll=4)


def ssd_scan_bidir(xbc, dt, dt_bias_f, dt_bias_b, a_log_f, a_log_b, d_skip):
    bsz, s, _ = xbc.shape
    nheads = dt.shape[2] // 2
    nh = nheads // SSM_GROUPS
    gw = nh * SSM_HEAD_DIM
    d_inner = nheads * SSM_HEAD_DIM
    q = SSD_CHUNK
    dt_g = dt.reshape(bsz, s, 2 * SSM_GROUPS, nh)
    dt_col = jnp.transpose(dt_g, (0, 2, 1, 3))
    dt_row = jnp.transpose(dt_g, (0, 2, 3, 1))
    prm = jnp.stack([jnp.concatenate([dt_bias_f, dt_bias_b]), jnp.concatenate([a_log_f, a_log_b])])
    prm = prm.reshape(2, 2 * SSM_GROUPS, nh)
    p_col = jnp.transpose(prm, (1, 0, 2))
    p_row = jnp.transpose(prm, (1, 2, 0))
    dx = jnp.repeat(d_skip, SSM_HEAD_DIM).reshape(SSM_GROUPS, 1, gw)
    assert DECAY_SLOTS * nh <= 128
    tri = jnp.asarray(np.tril(np.ones((q, q), np.float32)), BF16)
    place_np, pconst_np = _decay_placement(nh)
    place = jnp.asarray(place_np, BF16)
    pconst = jnp.asarray(pconst_np)
    e64 = jnp.asarray(_expand_matrix(nh, SSM_HEAD_DIM), BF16)
    bcol = d_inner // D_STATE
    G = SSM_GROUPS

    const = lambda shape: pl.BlockSpec(shape, lambda b, g: tuple(0 for _ in shape))
    return pl.pallas_call(
        _ssd_kernel,
        grid=(bsz, SSM_GROUPS),
        in_specs=[pl.BlockSpec((None, s, gw), lambda b, g: (b, 0, g)),
                  pl.BlockSpec((None, s, D_STATE), lambda b, g: (b, 0, bcol + g)),
                  pl.BlockSpec((None, s, D_STATE), lambda b, g: (b, 0, bcol + G + g)),
                  pl.BlockSpec((None, None, s, nh), lambda b, g: (b, g, 0, 0)),
                  pl.BlockSpec((None, None, s, nh), lambda b, g: (b, G + g, 0, 0)),
                  pl.BlockSpec((None, None, nh, s), lambda b, g: (b, g, 0, 0)),
                  pl.BlockSpec((None, None, nh, s), lambda b, g: (b, G + g, 0, 0)),
                  pl.BlockSpec((None, 2, nh), lambda b, g: (g, 0, 0)),
                  pl.BlockSpec((None, 2, nh), lambda b, g: (G + g, 0, 0)),
                  pl.BlockSpec((None, nh, 2), lambda b, g: (g, 0, 0)),
                  pl.BlockSpec((None, nh, 2), lambda b, g: (G + g, 0, 0)),
                  pl.BlockSpec((None, 1, gw), lambda b, g: (g, 0, 0)),
                  const((q, q)), const(place.shape), const(pconst.shape), const((nh, gw))],
        out_specs=pl.BlockSpec((None, s, gw), lambda b, g: (b, 0, g)),
        out_shape=jax.ShapeDtypeStruct((bsz, s, d_inner), F32),
        scratch_shapes=[pltpu.VMEM((s, 2 * nh), F32), pltpu.VMEM((s, 2 * nh), F32),
                        pltpu.VMEM((2 * nh, s), F32),
                        pltpu.VMEM((s, 3 * 128), BF16),
                        pltpu.VMEM((s, gw), BF16), pltpu.VMEM((s, gw), BF16),
                        pltpu.VMEM((s, gw), F32), pltpu.VMEM((s, gw), F32),
                        pltpu.VMEM((s // q, 8, gw), F32), pltpu.VMEM((s // q, 8, gw), F32),
                        pltpu.VMEM((D_STATE, gw), F32), pltpu.VMEM((D_STATE, gw), F32)],
        compiler_params=_cparams("parallel", "parallel"),
        name="ssd_scan",
    )(xbc, xbc, xbc, dt_col, dt_col, dt_row, dt_row, p_col, p_col, p_row, p_row, dx, tri, place, pconst, e64)


def _gated_proj_kernel(y_ref, z_ref, gw_ref, w_ref, x_ref, g_ref, o_ref):
    t = y_ref[...] * _silu(z_ref[...].astype(F32))
    ms = jnp.mean(t * t, axis=-1, keepdims=True)
    t = (t * lax.rsqrt(ms + EPS) * gw_ref[...]).astype(BF16)
    o_ref[...] = x_ref[...] + g_ref[...] * jnp.dot(t, w_ref[...], preferred_element_type=F32)


def gated_proj_residual(y, zproj, gw, w, x, g, *, ts):
    bsz, s, d = x.shape
    k = y.shape[2]
    return pl.pallas_call(
        _gated_proj_kernel,
        grid=(bsz, s // ts),
        in_specs=[pl.BlockSpec((None, ts, k), lambda b, i: (b, i, 0)),
                  pl.BlockSpec((None, ts, k), lambda b, i: (b, i, 0)),
                  pl.BlockSpec((1, k), lambda b, i: (0, 0)),
                  pl.BlockSpec((k, d), lambda b, i: (0, 0)),
                  pl.BlockSpec((None, ts, d), lambda b, i: (b, i, 0)),
                  pl.BlockSpec((None, 1, d), lambda b, i: (b, 0, 0))],
        out_specs=pl.BlockSpec((None, ts, d), lambda b, i: (b, i, 0)),
        out_shape=jax.ShapeDtypeStruct((bsz, s, d), F32),
        compiler_params=_cparams("parallel", "parallel"),
        name="ssd_out_proj",
    )(y, zproj, gw.reshape(1, k), w, x, g)


def _router_kernel(x_ref, nw_ref, sh_ref, sc_ref, rw_ref, rb_ref, lt_ref,
                   eidx_ref, rank_ref, wgt_ref, cnt_ref):
    h = _modnorm(x_ref[...], nw_ref[...], sh_ref[...], sc_ref[...])
    logits = jnp.dot(h, rw_ref[...], precision=HIGHEST, preferred_element_type=F32) + rb_ref[...]
    ts, ne = logits.shape
    eid = lax.broadcasted_iota(jnp.int32, (ts, ne), 1)
    m1 = jnp.max(logits, axis=-1, keepdims=True)
    i1 = jnp.min(jnp.where(logits == m1, eid, ne), axis=-1, keepdims=True)
    rest = jnp.where(eid == i1, -jnp.inf, logits)
    m2 = jnp.max(rest, axis=-1, keepdims=True)
    i2 = jnp.min(jnp.where(rest == m2, eid, ne), axis=-1, keepdims=True)
    e2 = jnp.exp(m2 - m1)
    w1 = 1.0 / (1.0 + e2)
    w2 = e2 / (1.0 + e2)
    oh1 = (eid == i1).astype(F32)
    oh2 = (eid == i2).astype(F32)
    chosen = oh1 + oh2
    incl = jnp.dot(lt_ref[...], chosen.astype(BF16), preferred_element_type=F32)
    before = incl - chosen
    r1 = jnp.sum(oh1 * before, axis=-1, keepdims=True)
    r2 = jnp.sum(oh2 * before, axis=-1, keepdims=True)
    eidx_ref[...] = jnp.concatenate([i1, i2], axis=1)
    rank_ref[...] = jnp.concatenate([r1, r2], axis=1).astype(jnp.int32)
    wgt_ref[...] = jnp.concatenate([w1, w2], axis=1)
    cnt_ref[...] = incl[ts - 1:ts, :].astype(jnp.int32)


def moe_route(x, nw, sh, sc, router_w, router_b, *, ts):
    bsz, s, d = x.shape
    ne = router_w.shape[1]
    nt = s // ts
    lt = jnp.asarray(np.tril(np.ones((ts, ts), np.float32)), BF16)
    tok = lambda dt: jax.ShapeDtypeStruct((bsz * s, TOP_K), dt)
    tok_spec = pl.BlockSpec((ts, TOP_K), lambda b, i: (b * nt + i, 0))
    return pl.pallas_call(
        _router_kernel,
        grid=(bsz, nt),
        in_specs=[pl.BlockSpec((None, ts, d), lambda b, i: (b, i, 0)),
                  pl.BlockSpec((1, d), lambda b, i: (0, 0)),
                  pl.BlockSpec((None, 1, d), lambda b, i: (b, 0, 0)),
                  pl.BlockSpec((None, 1, d), lambda b, i: (b, 0, 0)),
                  pl.BlockSpec((d, ne), lambda b, i: (0, 0)),
                  pl.BlockSpec((1, ne), lambda b, i: (0, 0)),
                  pl.BlockSpec((ts, ts), lambda b, i: (0, 0))],
        out_specs=[tok_spec, tok_spec, tok_spec,
                   pl.BlockSpec((None, 1, ne), lambda b, i: (b * nt + i, 0, 0))],
        out_shape=[tok(jnp.int32), tok(jnp.int32), tok(F32),
                   jax.ShapeDtypeStruct((bsz * nt, 1, ne), jnp.int32)],
        compiler_params=_cparams("parallel", "parallel"),
        name="moe_router",
    )(x, nw.reshape(1, d), sh, sc, router_w, router_b.reshape(1, ne), lt)


SEG_ROWS = 16
SEG_FIELDS = 3


def _segment_copies(seg_ref, tile, n_experts, make_copy, *, wait):
    for e in range(n_experts):
        base = (tile * n_experts + e) * SEG_FIELDS
        local0 = seg_ref[base]
        global0 = seg_ref[base + 1]

        def body(i, carry, local0=local0, global0=global0):
            cp = make_copy(pl.multiple_of(local0 + i * SEG_ROWS, SEG_ROWS),
                           pl.multiple_of(global0 + i * SEG_ROWS, SEG_ROWS))
            if wait:
                cp.wait()
            else:
                cp.start()
            return carry

        lax.fori_loop(0, seg_ref[base + 2], body, 0)


def _dispatch_kernel(seg_ref, x_ref, nw_ref, sh_ref, sc_ref, ld_ref, hs_in_ref, hs_ref, buf_ref, sem):
    del hs_in_ref
    tt = x_ref.shape[0]
    lc = buf_ref.shape[1]
    ne = N_EXPERTS
    tile = pl.program_id(0) * pl.num_programs(1) + pl.program_id(1)
    last = pl.num_programs(0) * pl.num_programs(1) - 1
    slot = tile % 2

    def copies(t, sl, wait):
        def make_copy(lo, go):
            return pltpu.make_async_copy(buf_ref.at[sl, pl.ds(lo, SEG_ROWS), :],
                                         hs_ref.at[pl.ds(go, SEG_ROWS), :], sem.at[sl])
        _segment_copies(seg_ref, t, ne, make_copy, wait=wait)

    h = _modnorm(x_ref[...], nw_ref[...], sh_ref[...], sc_ref[...]).astype(BF16)
    ld = ld_ref[...]
    rows = lax.broadcasted_iota(jnp.int32, (lc, tt), 0)
    perm = jnp.where(rows == ld[0:1, :], 1.0, jnp.where(rows == ld[1:2, :], 1.0, 0.0)).astype(BF16)
    buf_ref[slot] = jnp.dot(perm, h, preferred_element_type=F32).astype(BF16)
    copies(tile, slot, wait=False)

    @pl.when(tile > 0)
    def _():
        copies(tile - 1, 1 - slot, wait=True)

    @pl.when(tile == last)
    def _():
        copies(tile, slot, wait=True)


def moe_dispatch(x, nw, sh, sc, seg, ldest_rows, n_rows, *, tt, lc):
    bsz, s, d = x.shape
    nt = s // tt
    hs0 = jnp.zeros((n_rows, d), BF16)
    grid_spec = pltpu.PrefetchScalarGridSpec(
        num_scalar_prefetch=1,
        grid=(bsz, nt),
        in_specs=[pl.BlockSpec((None, tt, d), lambda b, i, sref: (b, i, 0)),
                  pl.BlockSpec((1, d), lambda b, i, sref: (0, 0)),
                  pl.BlockSpec((None, 1, d), lambda b, i, sref: (b, 0, 0)),
                  pl.BlockSpec((None, 1, d), lambda b, i, sref: (b, 0, 0)),
                  pl.BlockSpec((TOP_K, tt), lambda b, i, sref: (0, b * nt + i)),
                  pl.BlockSpec(memory_space=pl.ANY)],
        out_specs=pl.BlockSpec(memory_space=pl.ANY),
        scratch_shapes=[pltpu.VMEM((2, lc, d), BF16), pltpu.SemaphoreType.DMA((2,))],
    )
    return pl.pallas_call(
        _dispatch_kernel,
        grid_spec=grid_spec,
        out_shape=jax.ShapeDtypeStruct((n_rows, d), BF16),
        input_output_aliases={6: 0},
        compiler_params=_cparams("arbitrary", "arbitrary"),
        name="moe_dispatch",
    )(seg, x, nw.reshape(1, d), sh, sc, ldest_rows, hs0)


def _moe_kernel(te_ref, nu_ref, hs_ref, w1_ref, w3_ref, w2_ref, o_ref, acc_ref):
    i = pl.program_id(0)
    f = pl.program_id(1)

    @pl.when(i < nu_ref[0])
    def _():
        h = hs_ref[...]
        a = jnp.dot(h, w1_ref[...], preferred_element_type=F32)
        b = jnp.dot(h, w3_ref[...], preferred_element_type=F32)
        t = (_silu(a) * b).astype(BF16)
        contrib = jnp.dot(t, w2_ref[...], preferred_element_type=F32)

        @pl.when(f == 0)
        def _():
            acc_ref[...] = contrib

        @pl.when(f > 0)
        def _():
            acc_ref[...] += contrib

        @pl.when(f == pl.num_programs(1) - 1)
        def _():
            o_ref[...] = acc_ref[...].astype(o_ref.dtype)

    @pl.when((i >= nu_ref[0]) & (f == 0))
    def _():
        o_ref[...] = jnp.zeros_like(o_ref)


def moe_experts(hs, tile_expert, n_used, w1, w3, w2, *, tm, tf):
    n_rows, d = hs.shape
    dff = w1.shape[2]
    nf = dff // tf
    n_tiles = n_rows // tm

    def last_used(i, nu):
        return jnp.maximum(jnp.minimum(i, nu[0] - 1), 0)

    def row_map(i, f, te, nu):
        return (last_used(i, nu), 0)

    def w_in_map(i, f, te, nu):
        return (te[last_used(i, nu)], 0, jnp.where(i < nu[0], f, nf - 1))

    def w_out_map(i, f, te, nu):
        return (te[last_used(i, nu)], jnp.where(i < nu[0], f, nf - 1), 0)

    grid_spec = pltpu.PrefetchScalarGridSpec(
        num_scalar_prefetch=2,
        grid=(n_tiles, nf),
        in_specs=[pl.BlockSpec((tm, d), row_map),
                  pl.BlockSpec((None, d, tf), w_in_map),
                  pl.BlockSpec((None, d, tf), w_in_map),
                  pl.BlockSpec((None, tf, d), w_out_map)],
        out_specs=pl.BlockSpec((tm, d), lambda i, f, te, nu: (i, 0)),
        scratch_shapes=[pltpu.VMEM((tm, d), F32)],
    )
    return pl.pallas_call(
        _moe_kernel,
        grid_spec=grid_spec,
        out_shape=jax.ShapeDtypeStruct((n_rows, d), BF16),
        compiler_params=_cparams("arbitrary", "arbitrary"),
        name="moe_experts",
    )(tile_expert, n_used, hs, w1, w3, w2)


def _combine_kernel(seg_ref, ys_ref, x_ref, g_ref, wgt_ref, ld_ref, o_ref, buf_ref, sem):
    tt = x_ref.shape[0]
    lc = buf_ref.shape[1]
    ne = N_EXPERTS
    tile = pl.program_id(0) * pl.num_programs(1) + pl.program_id(1)
    last = pl.num_programs(0) * pl.num_programs(1) - 1
    slot = tile % 2

    def copies(t, sl, wait):
        def make_copy(lo, go):
            return pltpu.make_async_copy(ys_ref.at[pl.ds(go, SEG_ROWS), :],
                                         buf_ref.at[sl, pl.ds(lo, SEG_ROWS), :], sem.at[sl])
        _segment_copies(seg_ref, t, ne, make_copy, wait=wait)

    @pl.when(tile == 0)
    def _():
        buf_ref[...] = jnp.zeros_like(buf_ref)
        copies(tile, slot, wait=False)

    @pl.when(tile < last)
    def _():
        copies(tile + 1, 1 - slot, wait=False)

    copies(tile, slot, wait=True)

    ld = ld_ref[...]
    cols = lax.broadcasted_iota(jnp.int32, (tt, lc), 1)
    pick = jnp.concatenate([jnp.where(cols == ld[:, k:k + 1], 1.0, 0.0) for k in range(TOP_K)],
                           axis=0).astype(BF16)
    z = jnp.dot(pick, buf_ref[slot], preferred_element_type=F32)
    w = wgt_ref[...]
    mix = w[:, 0:1] * z[0:tt] + w[:, 1:2] * z[tt:]
    o_ref[...] = x_ref[...] + g_ref[...] * mix


def moe_combine(ys, seg, ldest, wgt, x, g, *, tt, lc):
    bsz, s, d = x.shape
    nt = s // tt
    tok_spec = pl.BlockSpec((tt, TOP_K), lambda b, i, sref: (b * nt + i, 0))
    grid_spec = pltpu.PrefetchScalarGridSpec(
        num_scalar_prefetch=1,
        grid=(bsz, nt),
        in_specs=[pl.BlockSpec(memory_space=pl.ANY),
                  pl.BlockSpec((None, tt, d), lambda b, i, sref: (b, i, 0)),
                  pl.BlockSpec((None, 1, d), lambda b, i, sref: (b, 0, 0)),
                  tok_spec, tok_spec],
        out_specs=pl.BlockSpec((None, tt, d), lambda b, i, sref: (b, i, 0)),
        scratch_shapes=[pltpu.VMEM((2, lc, d), BF16), pltpu.SemaphoreType.DMA((2,))],
    )
    return pl.pallas_call(
        _combine_kernel,
        grid_spec=grid_spec,
        out_shape=jax.ShapeDtypeStruct((bsz, s, d), F32),
        compiler_params=_cparams("arbitrary", "arbitrary"),
        name="moe_combine",
    )(seg, ys, x, g, wgt, ldest)


def _round_up(v, m):
    return ((v + m - 1) // m) * m


def moe_residual(x, nw, sh, sc, g, router_w, router_b, w1, w3, w2, *, tm=MOE_TILE_ROWS):
    bsz, s, d = x.shape
    n_tok = bsz * s
    ne = router_w.shape[1]
    tt = min(512, s)
    n_tt = n_tok // tt
    lc = _round_up(TOP_K * tt + ne * SEG_ROWS, 128)
    eidx, rank, wgt, cnt = moe_route(x, nw, sh, sc, router_w, router_b, ts=tt)
    seg_len = _round_up(cnt.reshape(n_tt, ne), SEG_ROWS)
    local_start = jnp.cumsum(seg_len, axis=1) - seg_len
    padded = _round_up(jnp.sum(seg_len, axis=0), tm)
    ends = jnp.cumsum(padded)
    global_start = (ends - padded)[None, :] + jnp.cumsum(seg_len, axis=0) - seg_len
    seg = jnp.stack([local_start, global_start, seg_len // SEG_ROWS], axis=-1).reshape(-1).astype(jnp.int32)
    onehot = eidx[:, :, None] == jnp.arange(ne, dtype=jnp.int32)
    start_tok = jnp.repeat(local_start, tt, axis=0)[:, None, :]
    ldest = (jnp.sum(jnp.where(onehot, start_tok, 0), axis=-1) + rank).astype(jnp.int32)
    n_rows = _round_up(n_tok * TOP_K + n_tt * ne * SEG_ROWS + ne * tm, tm)
    n_tiles = n_rows // tm
    tile_start = jnp.arange(n_tiles, dtype=jnp.int32) * tm
    tile_expert = jnp.minimum(jnp.sum(tile_start[:, None] >= ends[None, :], axis=1), ne - 1).astype(jnp.int32)
    n_used = (ends[ne - 1:ne] // tm).astype(jnp.int32)
    hs = moe_dispatch(x, nw, sh, sc, seg, ldest.T, n_rows, tt=tt, lc=lc)
    ys = moe_experts(hs, tile_expert, n_used, w1, w3, w2, tm=tm, tf=w1.shape[2] // 2)
    return moe_combine(ys, seg, ldest, wgt, x, g, tt=tt, lc=lc)


def _split_mod(mod):
    return [m[:, None, :] for m in jnp.split(mod, 6, axis=-1)]


def even_layer(x, c, rel_bias, ada_w, ada_b, norm1_w, in_w, q_norm_w, k_norm_w, sink, out_w,
               norm2_w, w1, w3, w2):
    s = x.shape[1]
    sh1, sc1, g1, sh2, sc2, g2 = _split_mod(ada_mod(c, ada_w, ada_b))
    proj = norm_mod_matmul(x, norm1_w, sh1, sc1, in_w.astype(BF16), ts=min(1024, s), tn=640, name="even_in_proj",
                           out_dtype=BF16)
    yf = fourier_mix(proj, tq=min(512, s))
    ya = window_attention(proj, band_bias(rel_bias), q_norm_w, k_norm_w, sink)
    x = cat_proj_residual(yf, ya, out_w.astype(BF16), x, g1, ts=min(1024, s))
    dff = w1.shape[1]
    return ffn_residual(x, norm2_w, sh2, sc2, g2, w1.astype(BF16), w3.astype(BF16), w2.astype(BF16),
                        ts=min(1024, s), tf=dff // 2)


def odd_layer(x, c, ada_w, ada_b, norm1_w, in_w, conv_w, conv_b, dt_bias_f, dt_bias_b, a_log_f, a_log_b,
              d_skip, gnorm_w, out_w, norm2_w, router_w, router_b, w1, w3, w2):
    s = x.shape[1]
    sh1, sc1, g1, sh2, sc2, g2 = _split_mod(ada_mod(c, ada_w, ada_b))
    d_inner = gnorm_w.shape[0]
    cdim = conv_w.shape[1]
    wide = d_inner + cdim
    in_w = in_w.astype(BF16)
    zx = norm_mod_matmul(x, norm1_w, sh1, sc1, in_w[:, :wide], ts=min(1024, s), tn=1024, name="odd_in_proj",
                         out_dtype=BF16)
    dt = norm_mod_matmul(x, norm1_w, sh1, sc1, in_w[:, wide:], ts=min(1024, s), tn=in_w.shape[1] - wide,
                         name="odd_dt_proj")
    xbc = conv_silu(zx, conv_w, conv_b, col0=d_inner, tc=512)
    y = ssd_scan_bidir(xbc, dt, dt_bias_f, dt_bias_b, a_log_f, a_log_b, d_skip)
    x = gated_proj_residual(y, zx, gnorm_w, out_w.astype(BF16), x, g1, ts=min(512, s))
    return moe_residual(x, norm2_w, sh2, sc2, g2, router_w, router_b,
                        w1.astype(BF16), w3.astype(BF16), w2.astype(BF16))


def kernel(x, c, rel_bias, ev_ada_w, ev_ada_b, ev_norm1_w, ev_in_w, ev_q_norm_w, ev_k_norm_w, ev_sink, ev_out_w, ev_norm2_w, ev_ffn_w1, ev_ffn_w3, ev_ffn_w2, od_ada_w, od_ada_b, od_norm1_w, od_in_w, od_conv_w, od_conv_b, od_dt_bias_f, od_dt_bias_b, od_A_log_f, od_A_log_b, od_D, od_gnorm_w, od_out_w, od_norm2_w, od_router_w, od_router_b, od_moe_w1, od_moe_w3, od_moe_w2):
    depth = ev_ada_w.shape[0] + od_ada_w.shape[0]
    for i in range(depth):
        j = i // 2
        if i % 2 == 0:
            x = even_layer(x, c, rel_bias, ev_ada_w[j], ev_ada_b[j], ev_norm1_w[j], ev_in_w[j],
                           ev_q_norm_w[j], ev_k_norm_w[j], ev_sink[j], ev_out_w[j], ev_norm2_w[j],
                           ev_ffn_w1[j], ev_ffn_w3[j], ev_ffn_w2[j])
        else:
            x = odd_layer(x, c, od_ada_w[j], od_ada_b[j], od_norm1_w[j], od_in_w[j], od_conv_w[j],
                          od_conv_b[j], od_dt_bias_f[j], od_dt_bias_b[j], od_A_log_f[j], od_A_log_b[j],
                          od_D[j], od_gnorm_w[j], od_out_w[j], od_norm2_w[j], od_router_w[j],
                          od_router_b[j], od_moe_w1[j], od_moe_w3[j], od_moe_w2[j])
    return x
```

```python
import functools

import numpy as np
import jax
import jax.numpy as jnp
from jax import lax
from jax.experimental import pallas as pl
from jax.experimental.pallas import tpu as pltpu

F32 = jnp.float32
BF16 = jnp.bfloat16
HIGHEST = lax.Precision.HIGHEST

EPS = 1e-6
FNET_GROUPS = 4
FNET_GROUP_DIM = 128
FNET_WIDTH = FNET_GROUPS * FNET_GROUP_DIM
ATTN_HEADS = 8
ATTN_KV_HEADS = 2
HEAD_DIM = 64
ATTN_WIDTH = ATTN_HEADS * HEAD_DIM
KV_WIDTH = ATTN_KV_HEADS * HEAD_DIM
WINDOW = 128
BLOCK = 128
REL_BUCKETS = 32
REL_MAX_DIST = 128
SSM_HEAD_DIM = 64
SSM_GROUPS = 4
D_STATE = 128
CONV_WIDTH = 5
SSD_CHUNK = 128
N_EXPERTS = 8
TOP_K = 2
NEG_BIG = -1e30
ATTN_QBLOCKS = 2

V7X_VMEM_LIMIT_BYTES = 56 * 1024 * 1024
MOE_TILE_ROWS = 512


def _cparams(*sem):
    return pltpu.CompilerParams(dimension_semantics=sem, vmem_limit_bytes=V7X_VMEM_LIMIT_BYTES)


def _modnorm(x, nw, sh, sc):
    ms = jnp.mean(x * x, axis=-1, keepdims=True)
    return x * lax.rsqrt(ms + EPS) * nw * (1.0 + sc) + sh


def _silu(x):
    return x * (1.0 / (1.0 + jnp.exp(-x)))


def _ada_kernel(c_ref, w_ref, b_ref, o_ref):
    cs = _silu(c_ref[...]).astype(BF16)
    o_ref[...] = jnp.dot(cs, w_ref[...].astype(BF16), preferred_element_type=F32) + b_ref[...]


def ada_mod(c, w, b):
    bsz, d = c.shape
    n = w.shape[1]
    tn = 1536
    return pl.pallas_call(
        _ada_kernel,
        grid=(n // tn,),
        in_specs=[pl.BlockSpec((bsz, d), lambda j: (0, 0)),
                  pl.BlockSpec((d, tn), lambda j: (0, j)),
                  pl.BlockSpec((1, tn), lambda j: (0, j))],
        out_specs=pl.BlockSpec((bsz, tn), lambda j: (0, j)),
        out_shape=jax.ShapeDtypeStruct((bsz, n), F32),
        compiler_params=_cparams("arbitrary"),
        name="ada_mod",
    )(c, w, b.reshape(1, n))


def _nmm_kernel(x_ref, nw_ref, sh_ref, sc_ref, w_ref, o_ref, h_ref):
    @pl.when(pl.program_id(2) == 0)
    def _():
        h_ref[...] = _modnorm(x_ref[...], nw_ref[...], sh_ref[...], sc_ref[...]).astype(BF16)

    o_ref[...] = jnp.dot(h_ref[...], w_ref[...], preferred_element_type=F32).astype(o_ref.dtype)


def norm_mod_matmul(x, nw, sh, sc, w, *, ts, tn, name, out_dtype=F32):
    bsz, s, d = x.shape
    n = w.shape[1]
    return pl.pallas_call(
        _nmm_kernel,
        grid=(bsz, s // ts, n // tn),
        in_specs=[pl.BlockSpec((None, ts, d), lambda b, i, j: (b, i, 0)),
                  pl.BlockSpec((1, d), lambda b, i, j: (0, 0)),
                  pl.BlockSpec((None, 1, d), lambda b, i, j: (b, 0, 0)),
                  pl.BlockSpec((None, 1, d), lambda b, i, j: (b, 0, 0)),
                  pl.BlockSpec((d, tn), lambda b, i, j: (0, j))],
        out_specs=pl.BlockSpec((None, ts, tn), lambda b, i, j: (b, i, j)),
        out_shape=jax.ShapeDtypeStruct((bsz, s, n), out_dtype),
        scratch_shapes=[pltpu.VMEM((ts, d), BF16)],
        compiler_params=_cparams("parallel", "parallel", "arbitrary"),
        name=name,
    )(x, nw.reshape(1, d), sh, sc, w)


def _dft_cos_sin(n):
    k = np.arange(n, dtype=np.int64)
    ang = ((k[:, None] * k[None, :]) % n).astype(np.float64) * (2.0 * np.pi / n)
    scale = 1.0 / np.sqrt(n)
    return np.cos(ang) * scale, np.sin(ang) * scale


def _fourier_kernel(u_ref, chan_ref, seq_ref, o_ref, ab_ref):
    s = u_ref.shape[0]

    @pl.when(pl.program_id(1) == 0)
    def _():
        for g in range(FNET_GROUPS):
            lo, hi = g * FNET_GROUP_DIM, (g + 1) * FNET_GROUP_DIM
            ug = u_ref[:, lo:hi].astype(BF16)
            cs = jnp.dot(ug, chan_ref[...], preferred_element_type=F32)
            ab_ref[0:s, lo:hi] = cs[:, :FNET_GROUP_DIM].astype(BF16)
            ab_ref[s:2 * s, lo:hi] = cs[:, FNET_GROUP_DIM:].astype(BF16)

    o_ref[...] = jnp.dot(seq_ref[...], ab_ref[...], preferred_element_type=F32)


def fourier_mix(proj, *, tq):
    bsz, s, _ = proj.shape
    cc, sc = _dft_cos_sin(FNET_GROUP_DIM)
    chan = jnp.asarray(np.concatenate([cc, sc], axis=1), BF16)
    cs, ss = _dft_cos_sin(s)
    seq = jnp.asarray(np.concatenate([cs, -ss], axis=1), BF16)
    return pl.pallas_call(
        _fourier_kernel,
        grid=(bsz, s // tq),
        in_specs=[pl.BlockSpec((None, s, FNET_WIDTH), lambda b, i: (b, 0, 0)),
                  pl.BlockSpec((FNET_GROUP_DIM, 2 * FNET_GROUP_DIM), lambda b, i: (0, 0)),
                  pl.BlockSpec((tq, 2 * s), lambda b, i: (i, 0))],
        out_specs=pl.BlockSpec((None, tq, FNET_WIDTH), lambda b, i: (b, i, 0)),
        out_shape=jax.ShapeDtypeStruct((bsz, s, FNET_WIDTH), F32),
        scratch_shapes=[pltpu.VMEM((2 * s, FNET_WIDTH), BF16)],
        compiler_params=_cparams("parallel", "arbitrary"),
        name="fourier_mix",
    )(proj, chan, seq)


def _band_bucket_table():
    i = np.arange(BLOCK)[:, None]
    j = np.arange(3 * BLOCK)[None, :]
    rel = (j - BLOCK) - i
    half = REL_BUCKETS // 2
    max_exact = half // 2
    n = np.abs(rel)
    large = max_exact + (np.log(np.maximum(n, 1) / max_exact)
                         / np.log(REL_MAX_DIST / max_exact) * (half - max_exact)).astype(np.int32)
    large = np.minimum(large, half - 1)
    bucket = (rel > 0).astype(np.int32) * half + np.where(n < max_exact, n, large)
    return np.where(n <= WINDOW, bucket, -1).astype(np.int32)


def _bias_kernel(rb_ref, bucket_ref, o_ref):
    h = pl.program_id(0)
    bucket = bucket_ref[...]
    acc = jnp.full(bucket.shape, NEG_BIG, F32)
    for bkt in range(REL_BUCKETS):
        acc = jnp.where(bucket == bkt, rb_ref[bkt * ATTN_HEADS + h], acc)
    o_ref[...] = acc


def band_bias(rel_bias):
    bucket = jnp.asarray(_band_bucket_table())
    return pl.pallas_call(
        _bias_kernel,
        grid=(ATTN_HEADS,),
        in_specs=[pl.BlockSpec(memory_space=pltpu.SMEM),
                  pl.BlockSpec((BLOCK, 3 * BLOCK), lambda h: (0, 0))],
        out_specs=pl.BlockSpec((None, BLOCK, 3 * BLOCK), lambda h: (h, 0, 0)),
        out_shape=jax.ShapeDtypeStruct((ATTN_HEADS, BLOCK, 3 * BLOCK), F32),
        compiler_params=_cparams("arbitrary"),
        name="band_bias",
    )(rel_bias.reshape(-1), bucket)


def _head_mean_matrix(width):
    m = np.zeros((width, width), np.float32)
    for h in range(width // HEAD_DIM):
        m[h * HEAD_DIM:(h + 1) * HEAD_DIM, h * HEAD_DIM:(h + 1) * HEAD_DIM] = 1.0 / HEAD_DIM
    return m


def _heads_rms(t, mean_mat, w):
    sq = t * t
    hi = sq.astype(BF16)
    lo = (sq - hi.astype(F32)).astype(BF16)
    ms = (jnp.dot(hi, mean_mat, preferred_element_type=F32)
          + jnp.dot(lo, mean_mat, preferred_element_type=F32))
    return t * lax.rsqrt(ms + EPS) * w


def _attn_kernel(sink_ref, q_ref, kl_ref, kc_ref, kr_ref, vl_ref, vc_ref, vr_ref,
                 bias_ref, qnw_ref, knw_ref, qmean_ref, kmean_ref, o_ref):
    n = pl.program_id(1)
    nb = pl.num_programs(1) * ATTN_QBLOCKS
    k = jnp.concatenate([kl_ref[...], kc_ref[...], kr_ref[...]], axis=0).astype(F32)
    v = jnp.concatenate([vl_ref[...], vc_ref[...], vr_ref[...]], axis=0).astype(F32)
    col = lax.broadcasted_iota(jnp.int32, (1, 3 * BLOCK), 1)
    qn = _heads_rms(q_ref[...].astype(F32), qmean_ref[...], qnw_ref[...])
    kn = _heads_rms(k, kmean_ref[...], knw_ref[...])
    low = lax.broadcasted_iota(jnp.int32, (1, 2 * HEAD_DIM), 1) < HEAD_DIM
    kn_sw = pltpu.roll(kn, HEAD_DIM, axis=1)
    v_sw = pltpu.roll(v, HEAD_DIM, axis=1)
    k_dup = [jnp.where(low, kn, kn_sw).astype(BF16), jnp.where(low, kn_sw, kn).astype(BF16)]
    ones = jnp.ones((k.shape[0], 2 * HEAD_DIM), BF16)
    v_ext = [jnp.concatenate([v.astype(BF16), ones], axis=1),
             jnp.concatenate([v_sw.astype(BF16), ones], axis=1)]
    g = ATTN_HEADS // ATTN_KV_HEADS
    for qb in range(ATTN_QBLOCKS):
        blk = n * ATTN_QBLOCKS + qb
        band = slice(qb * BLOCK, (qb + 3) * BLOCK)
        first_key = jnp.where(blk == 0, BLOCK, 0)
        end_key = jnp.where(blk == nb - 1, 2 * BLOCK, 3 * BLOCK)
        outside = (col < first_key) | (col >= end_key)
        pairs = []
        for m in range(ATTN_HEADS // 2):
            j = (2 * m) // g
            qp = qn[qb * BLOCK:(qb + 1) * BLOCK, m * 2 * HEAD_DIM:(m + 1) * 2 * HEAD_DIM]
            res = []
            for idx in range(2):
                h = 2 * m + idx
                qm = jnp.where(low if idx == 0 else jnp.logical_not(low), qp, 0.0).astype(BF16)
                logits = lax.dot_general(qm, k_dup[j][band], (((1,), (1,)), ((), ())),
                                         preferred_element_type=F32)
                logits = jnp.where(outside, NEG_BIG, logits + bias_ref[h])
                sk = sink_ref[h]
                mx = jnp.maximum(jnp.max(logits, axis=-1, keepdims=True), sk)
                p = jnp.exp(logits - mx).astype(BF16)
                r = jnp.dot(p, v_ext[idx if j == 0 else 1 - idx][band], preferred_element_type=F32)
                denom = r[:, 2 * HEAD_DIM:] + jnp.exp(sk - mx)
                res.append(r[:, :2 * HEAD_DIM] / denom)
            pairs.append(jnp.where(low, res[0], res[1]))
        o_ref[qb * BLOCK:(qb + 1) * BLOCK, :] = jnp.concatenate(pairs, axis=-1)


def window_attention(proj, bias, q_norm_w, k_norm_w, sink):
    bsz, s, _ = proj.shape
    nb = s // BLOCK
    qcol = FNET_WIDTH // ATTN_WIDTH
    kcol = (FNET_WIDTH + ATTN_WIDTH) // KV_WIDTH
    vcol = kcol + 1

    qb = ATTN_QBLOCKS
    assert nb % qb == 0

    def kv_specs(col):
        return [pl.BlockSpec((None, BLOCK, KV_WIDTH), lambda b, n: (b, jnp.maximum(n * qb - 1, 0), col)),
                pl.BlockSpec((None, qb * BLOCK, KV_WIDTH), lambda b, n: (b, n, col)),
                pl.BlockSpec((None, BLOCK, KV_WIDTH), lambda b, n: (b, jnp.minimum((n + 1) * qb, nb - 1), col))]

    return pl.pallas_call(
        _attn_kernel,
        grid=(bsz, nb // qb),
        in_specs=[pl.BlockSpec(memory_space=pltpu.SMEM),
                  pl.BlockSpec((None, qb * BLOCK, ATTN_WIDTH), lambda b, n: (b, n, qcol)),
                  *kv_specs(kcol), *kv_specs(vcol),
                  pl.BlockSpec((ATTN_HEADS, BLOCK, 3 * BLOCK), lambda b, n: (0, 0, 0)),
                  pl.BlockSpec((1, ATTN_WIDTH), lambda b, n: (0, 0)),
                  pl.BlockSpec((1, KV_WIDTH), lambda b, n: (0, 0)),
                  pl.BlockSpec((ATTN_WIDTH, ATTN_WIDTH), lambda b, n: (0, 0)),
                  pl.BlockSpec((KV_WIDTH, KV_WIDTH), lambda b, n: (0, 0))],
        out_specs=pl.BlockSpec((None, qb * BLOCK, ATTN_WIDTH), lambda b, n: (b, n, 0)),
        out_shape=jax.ShapeDtypeStruct((bsz, s, ATTN_WIDTH), F32),
        compiler_params=_cparams("parallel", "arbitrary"),
        name="window_attention",
    )(sink, proj, proj, proj, proj, proj, proj, proj, bias,
      (jnp.tile(q_norm_w, ATTN_HEADS) * (HEAD_DIM ** -0.5)).reshape(1, ATTN_WIDTH),
      jnp.tile(k_norm_w, ATTN_KV_HEADS).reshape(1, KV_WIDTH),
      jnp.asarray(_head_mean_matrix(ATTN_WIDTH), BF16), jnp.asarray(_head_mean_matrix(KV_WIDTH), BF16))


def _cat_proj_kernel(a1_ref, a2_ref, w_ref, x_ref, g_ref, o_ref):
    k1 = a1_ref.shape[1]
    y = jnp.dot(a1_ref[...].astype(BF16), w_ref[0:k1, :], preferred_element_type=F32)
    y = y + jnp.dot(a2_ref[...].astype(BF16), w_ref[k1:, :], preferred_element_type=F32)
    o_ref[...] = x_ref[...] + g_ref[...] * y


def cat_proj_residual(a1, a2, w, x, g, *, ts):
    bsz, s, d = x.shape
    k1, k2 = a1.shape[2], a2.shape[2]
    return pl.pallas_call(
        _cat_proj_kernel,
        grid=(bsz, s // ts),
        in_specs=[pl.BlockSpec((None, ts, k1), lambda b, i: (b, i, 0)),
                  pl.BlockSpec((None, ts, k2), lambda b, i: (b, i, 0)),
                  pl.BlockSpec((k1 + k2, d), lambda b, i: (0, 0)),
                  pl.BlockSpec((None, ts, d), lambda b, i: (b, i, 0)),
                  pl.BlockSpec((None, 1, d), lambda b, i: (b, 0, 0))],
        out_specs=pl.BlockSpec((None, ts, d), lambda b, i: (b, i, 0)),
        out_shape=jax.ShapeDtypeStruct((bsz, s, d), F32),
        compiler_params=_cparams("parallel", "parallel"),
        name="mixer_out_proj",
    )(a1, a2, w, x, g)


def _ffn_kernel(x_ref, nw_ref, sh_ref, sc_ref, g_ref, w1_ref, w3_ref, w2_ref, o_ref, h_ref, acc_ref):
    f = pl.program_id(2)

    @pl.when(f == 0)
    def _():
        h_ref[...] = _modnorm(x_ref[...], nw_ref[...], sh_ref[...], sc_ref[...]).astype(BF16)

    h = h_ref[...]
    a = jnp.dot(h, w1_ref[...], preferred_element_type=F32)
    b = jnp.dot(h, w3_ref[...], preferred_element_type=F32)
    t = (_silu(a) * b).astype(BF16)
    contrib = jnp.dot(t, w2_ref[...], preferred_element_type=F32)

    @pl.when(f == 0)
    def _():
        acc_ref[...] = contrib

    @pl.when(f > 0)
    def _():
        acc_ref[...] += contrib

    @pl.when(f == pl.num_programs(2) - 1)
    def _():
        o_ref[...] = x_ref[...] + g_ref[...] * acc_ref[...]


def ffn_residual(x, nw, sh, sc, g, w1, w3, w2, *, ts, tf):
    bsz, s, d = x.shape
    dff = w1.shape[1]
    vec = pl.BlockSpec((None, 1, d), lambda b, i, f: (b, 0, 0))
    return pl.pallas_call(
        _ffn_kernel,
        grid=(bsz, s // ts, dff // tf),
        in_specs=[pl.BlockSpec((None, ts, d), lambda b, i, f: (b, i, 0)),
                  pl.BlockSpec((1, d), lambda b, i, f: (0, 0)),
                  vec, vec, vec,
                  pl.BlockSpec((d, tf), lambda b, i, f: (0, f)),
                  pl.BlockSpec((d, tf), lambda b, i, f: (0, f)),
                  pl.BlockSpec((tf, d), lambda b, i, f: (f, 0))],
        out_specs=pl.BlockSpec((None, ts, d), lambda b, i, f: (b, i, 0)),
        out_shape=jax.ShapeDtypeStruct((bsz, s, d), F32),
        scratch_shapes=[pltpu.VMEM((ts, d), BF16), pltpu.VMEM((ts, d), F32)],
        compiler_params=_cparams("parallel", "parallel", "arbitrary"),
        name="ffn_swiglu",
    )(x, nw.reshape(1, d), sh, sc, g, w1, w3, w2)


CONV_PAD = 8
CONV_ROWS = 256


def _conv_kernel(x_ref, w_ref, b_ref, o_ref, pad_ref):
    s, tc = x_ref.shape
    zeros = jnp.zeros((CONV_PAD, tc), F32)
    pad_ref[0:CONV_PAD, :] = zeros
    pad_ref[CONV_PAD + s:, :] = zeros
    pad_ref[CONV_PAD:CONV_PAD + s, :] = x_ref[...].astype(F32)
    half = CONV_WIDTH // 2
    for r in range(s // CONV_ROWS):
        base = CONV_PAD + r * CONV_ROWS - half
        acc = jnp.zeros((CONV_ROWS, tc), F32) + b_ref[...]
        for kk in range(CONV_WIDTH):
            acc = acc + pad_ref[base + kk:base + kk + CONV_ROWS, :] * w_ref[kk:kk + 1, :]
        o_ref[r * CONV_ROWS:(r + 1) * CONV_ROWS, :] = _silu(acc)


def conv_silu(proj, conv_w, conv_b, *, col0, tc):
    bsz, s, _ = proj.shape
    cdim = conv_w.shape[1]
    cb0 = col0 // tc
    return pl.pallas_call(
        _conv_kernel,
        grid=(bsz, cdim // tc),
        in_specs=[pl.BlockSpec((None, s, tc), lambda b, j: (b, 0, cb0 + j)),
                  pl.BlockSpec((CONV_WIDTH, tc), lambda b, j: (0, j)),
                  pl.BlockSpec((1, tc), lambda b, j: (0, j))],
        out_specs=pl.BlockSpec((None, s, tc), lambda b, j: (b, 0, j)),
        out_shape=jax.ShapeDtypeStruct((bsz, s, cdim), F32),
        scratch_shapes=[pltpu.VMEM((s + 2 * CONV_PAD, tc), F32)],
        compiler_params=_cparams("parallel", "parallel"),
        name="conv_silu",
    )(proj, conv_w, conv_b.reshape(1, cdim))


def _softplus(x):
    return jnp.maximum(x, 0.0) + jnp.log(1.0 + jnp.exp(-jnp.abs(x)))


def _expand_matrix(n_in, width):
    e = np.zeros((n_in, n_in * width), np.float32)
    for h in range(n_in):
        e[h, h * width:(h + 1) * width] = 1.0
    return e


LOG2E = 1.4426950408889634
DECAY_SLOTS = 12


def _decay_placement(nh):
    place = np.zeros((3, 2 * nh, 3 * 128), np.float32)
    const = np.zeros((1, 3 * 128), np.float32)
    for h in range(nh):
        for part in range(3):
            place[part, h, part * nh + h] = 1.0
            place[part, nh + h, (6 + part) * nh + h] = -1.0
            place[part, h, 128 + (3 + part) * nh + h] = -1.0
            place[part, nh + h, 256 + (9 + part) * nh + h] = 1.0
            const[0, (3 + part) * nh + h] = 1.0
            const[0, (9 + part) * nh + h] = 1.0
            const[0, 128 + part * nh + h] = 1.0
            const[0, 256 + (6 + part) * nh + h] = 1.0
    return place, const


def _split3(v):
    hi = v.astype(BF16)
    r = v - hi.astype(F32)
    mid = r.astype(BF16)
    lo = (r - mid.astype(F32)).astype(BF16)
    return hi, mid, lo


def _ssd_kernel(x_ref, b_ref, c_ref, dtc_f_ref, dtc_b_ref, dtr_f_ref, dtr_b_ref,
                pc_f_ref, pc_b_ref, pr_f_ref, pr_b_ref, dx_ref,
                tri_ref, place_ref, pconst_ref, e64_ref, y_ref,
                ac_ref, dc_ref, dr_ref, pq_ref, xdf_ref, xdb_ref, eif_ref, er_ref, decf_ref, decb_ref,
                hf_ref, hb_ref):
    s = x_ref.shape[0]
    q = SSD_CHUNK
    nh = pc_f_ref.shape[1]
    nc = s // q
    hd = SSM_HEAD_DIM

    def col_params(raw_ref, p_ref):
        dt = _softplus(raw_ref[...] + p_ref[0:1, :])
        return dt, (-LOG2E) * jnp.exp(p_ref[1:2, :]) * dt

    dcf, acf = col_params(dtc_f_ref, pc_f_ref)
    dcb, acb = col_params(dtc_b_ref, pc_b_ref)
    dc_ref[:, 0:nh] = dcf
    dc_ref[:, nh:] = dcb
    ac_ref[:, 0:nh] = acf
    ac_ref[:, nh:] = acb
    dr_ref[0:nh, :] = _softplus(dtr_f_ref[...] + pr_f_ref[:, 0:1])
    dr_ref[nh:, :] = _softplus(dtr_b_ref[...] + pr_b_ref[:, 0:1])

    li = lax.broadcasted_iota(jnp.int32, (q, q), 0)
    si = lax.broadcasted_iota(jnp.int32, (q, q), 1)
    lower = li >= si
    upper = li <= si
    slot_head = lax.broadcasted_iota(jnp.int32, (1, 128), 1) % nh
    pair_lo = lax.broadcasted_iota(jnp.int32, (1, 2 * hd), 1) < hd

    def bdot(a, b):
        return jnp.dot(a, b, preferred_element_type=F32)

    def chunk_cumsums(sl):
        a_col = ac_ref[sl, :]
        tri = tri_ref[...]
        hi, mid, lo = _split3(a_col)
        return a_col, bdot(tri, hi) + bdot(tri, mid) + bdot(tri, lo)

    def expand64(v, parts=2):
        e = e64_ref[...]
        hi = v.astype(BF16)
        out = bdot(hi, e)
        if parts == 2:
            out = out + bdot((v - hi.astype(F32)).astype(BF16), e)
        return out

    hf_ref[...] = jnp.zeros_like(hf_ref)
    hb_ref[...] = jnp.zeros_like(hb_ref)

    def prep_body(c, carry):
        sl = pl.ds(pl.multiple_of(c * q, q), q)
        a_col, i_col = chunk_cumsums(sl)
        d_col = dc_ref[sl, :]
        if_col = i_col[:, 0:nh]
        tot_f = i_col[q - 1:q, 0:nh]
        tot_b = i_col[q - 1:q, nh:]
        ie = i_col - jnp.where(lax.broadcasted_iota(jnp.int32, (1, 2 * nh), 1) < nh, 0.0, a_col)
        eb_col = ie[:, nh:]
        parts = _split3(ie)
        placed = pconst_ref[...]
        for part in range(3):
            placed = placed + bdot(parts[part], place_ref[part])
        pq_ref[sl, :] = placed.astype(BF16)
        xc = x_ref[sl, :]
        xdf_ref[sl, :] = (xc * expand64(jnp.exp2(tot_f - if_col) * d_col[:, 0:nh], parts=1)).astype(BF16)
        xdb_ref[sl, :] = (xc * expand64(jnp.exp2(eb_col) * d_col[:, nh:], parts=1)).astype(BF16)
        eif_ref[sl, :] = expand64(jnp.exp2(if_col))
        er_ref[sl, :] = expand64(jnp.exp2(tot_b - eb_col))
        dec = expand64(jnp.exp2(jnp.concatenate([tot_f, tot_b], axis=0)))
        decf_ref[c] = jnp.broadcast_to(dec[0:1, :], decf_ref.shape[1:])
        decb_ref[c] = jnp.broadcast_to(dec[1:2, :], decb_ref.shape[1:])
        return carry

    lax.fori_loop(0, nc, prep_body, 0, unroll=4)

    def fwd_body(c, carry):
        sl = pl.ds(pl.multiple_of(c * q, q), q)
        xc = x_ref[sl, :]
        bc = b_ref[sl, :].astype(BF16)
        cc = c_ref[sl, :].astype(BF16)
        d_row = dr_ref[:, sl]
        placed = pq_ref[sl, :]
        p_all = placed[:, 0:128]
        q_cat = jnp.concatenate([placed[:, 128:256], placed[:, 256:384]], axis=0)

        cb = lax.dot_general(cc, bc, (((1,), (1,)), ((), ())), preferred_element_type=F32)
        xb = xc.astype(BF16)
        ys = []
        for h0 in range(0, nh, 2):
            ms = []
            for h in (h0, h0 + 1):
                ph = jnp.where(slot_head == h, p_all, jnp.zeros_like(p_all))
                g2 = lax.dot_general(ph, q_cat, (((1,), (1,)), ((), ())), preferred_element_type=F32)
                arg = jnp.where(lower, g2[:, 0:q], g2[:, q:])
                wgt = (jnp.where(lower, d_row[h:h + 1, :], 0.0)
                       + jnp.where(upper, d_row[nh + h:nh + h + 1, :], 0.0))
                ms.append((cb * jnp.exp2(arg) * wgt).astype(BF16))
            xp = xb[:, h0 * hd:(h0 + 2) * hd]
            zero = jnp.zeros_like(xp)
            rhs = jnp.concatenate([jnp.where(pair_lo, xp, zero), jnp.where(pair_lo, zero, xp)], axis=0)
            ys.append(bdot(jnp.concatenate(ms, axis=1), rhs))
        y = jnp.concatenate(ys, axis=1) + dx_ref[...] * xc

        states = lax.dot_general(bc, xdf_ref[sl, :], (((0,), (0,)), ((), ())),
                                 preferred_element_type=F32)
        h_prev = hf_ref[...]
        y = y + bdot(cc, h_prev.astype(BF16)) * eif_ref[sl, :]
        hf_ref[...] = h_prev * decf_ref[c][0:1, :] + states
        y_ref[sl, :] = y
        return carry

    lax.fori_loop(0, nc, fwd_body, 0, unroll=2)

    def bwd_body(t, carry):
        c = nc - 1 - t
        sl = pl.ds(pl.multiple_of(c * q, q), q)
        bc = b_ref[sl, :].astype(BF16)
        cc = c_ref[sl, :].astype(BF16)
        states = lax.dot_general(bc, xdb_ref[sl, :], (((0,), (0,)), ((), ())),
                                 preferred_element_type=F32)
        h_prev = hb_ref[...]
        y_ref[sl, :] += bdot(cc, h_prev.astype(BF16)) * er_ref[sl, :]
        hb_ref[...] = h_prev * decb_ref[c][0:1, :] + states
        return carry

    lax.fori_loop(0, nc, bwd_body, 0, unroll=4)


def _bf16_parts3(v):
    hi = v.astype(BF16).astype(F32)
    r = v - hi
    mid = r.astype(BF16).astype(F32)
    lo = (r - mid).astype(BF16).astype(F32)
    return hi, mid, lo


def _ssd_t_kernel(x_ref, b_ref, c_ref, dt_f_ref, dt_b_ref, p_f_ref, p_b_ref, dx_ref, trio_ref, y_ref,
                  ar_ref, dr_ref, sc_ref, pq_ref, qt_ref, xtb_ref, xdf_ref, xdb_ref, yt_ref, hf_ref, hb_ref):
    s = x_ref.shape[0]
    q = SSD_CHUNK
    nh = p_f_ref.shape[0]
    nc = s // q
    hd = SSM_HEAD_DIM

    def row_params(raw_ref, p_ref):
        dt = _softplus(raw_ref[...] + p_ref[:, 0:1])
        return dt, (-LOG2E) * jnp.exp(p_ref[:, 1:2]) * dt

    dtf, af = row_params(dt_f_ref, p_f_ref)
    dtb, ab = row_params(dt_b_ref, p_b_ref)
    dr_ref[0:nh, :] = dtf
    dr_ref[nh:, :] = dtb
    ar_ref[0:nh, :] = af
    ar_ref[nh:, :] = ab

    li = lax.broadcasted_iota(jnp.int32, (q, q), 0)
    si = lax.broadcasted_iota(jnp.int32, (q, q), 1)
    lower = li >= si
    upper = li <= si
    slot_head = lax.broadcasted_iota(jnp.int32, (1, q), 1) % nh

    def bdot(a, b):
        return jnp.dot(a, b, preferred_element_type=F32)

    def ntdot(a, b):
        return lax.dot_general(a, b, (((1,), (1,)), ((), ())), preferred_element_type=F32)

    def head_rows(v):
        return jnp.concatenate([jnp.broadcast_to(v[h:h + 1, :], (hd, q)) for h in range(nh)], axis=0)

    ones = jnp.ones((nh, q), F32)
    zeros = jnp.zeros((nh, q), F32)

    def prep_body(c, carry):
        sl = pl.ds(pl.multiple_of(c * q, q), q)
        a_row = ar_ref[:, sl]
        d_row = dr_ref[:, sl]
        parts = jnp.concatenate([p.astype(BF16) for p in _bf16_parts3(a_row)], axis=0)
        cs = bdot(parts, trio_ref[...])
        cs = cs[0:2 * nh] + cs[2 * nh:4 * nh] + cs[4 * nh:6 * nh]
        i_f = cs[0:nh, 0:q]
        e_b = cs[nh:, 0:q] - a_row[nh:]
        tot_f = cs[0:nh, q:]
        tot_b = cs[nh:, q:]
        ih, im, il = _bf16_parts3(i_f)
        eh, em, el = _bf16_parts3(e_b)
        pad = [zeros] * (q // nh - DECAY_SLOTS)
        p_t = jnp.concatenate([ih, im, il, ones, ones, ones, -eh, -em, -el, ones, ones, ones] + pad, axis=0)
        qf_t = jnp.concatenate([ones, ones, ones, -ih, -im, -il] + [zeros] * 6 + pad, axis=0)
        qb_t = jnp.concatenate([zeros] * 6 + [ones, ones, ones, eh, em, el] + pad, axis=0)
        pq_ref[sl, :] = p_t.T.astype(BF16)
        qt_ref[c] = jnp.concatenate([qf_t, qb_t], axis=1).astype(BF16)
        sc_ref[0 * nh:1 * nh, sl] = jnp.exp2(i_f)
        sc_ref[1 * nh:2 * nh, sl] = jnp.exp2(tot_b - e_b)
        sc_ref[2 * nh:3 * nh, sl] = jnp.exp2(tot_f)
        sc_ref[3 * nh:4 * nh, sl] = jnp.exp2(tot_b)
        xt = x_ref[sl, :].T
        xtb_ref[:, sl] = xt.astype(BF16)
        xdf_ref[:, sl] = (xt * head_rows(jnp.exp2(tot_f - i_f) * d_row[0:nh])).astype(BF16)
        xdb_ref[:, sl] = (xt * head_rows(jnp.exp2(e_b) * d_row[nh:])).astype(BF16)
        return carry

    lax.fori_loop(0, nc, prep_body, 0, unroll=2)

    hf_ref[...] = jnp.zeros_like(hf_ref)
    hb_ref[...] = jnp.zeros_like(hb_ref)
    zero_half = jnp.zeros((hd, q), BF16)

    def fwd_body(c, carry):
        sl = pl.ds(pl.multiple_of(c * q, q), q)
        bc = b_ref[sl, :].astype(BF16)
        cc = c_ref[sl, :].astype(BF16)
        d_row = dr_ref[:, sl]
        p_all = pq_ref[sl, :]
        q_t = qt_ref[c]
        xtb = xtb_ref[:, sl]
        cb = ntdot(cc, bc)
        g2s = [bdot(jnp.where(slot_head == h, p_all, jnp.zeros_like(p_all)), q_t) for h in range(nh)]
        ms = []
        for h in range(nh):
            arg = jnp.where(lower, g2s[h][:, 0:q], g2s[h][:, q:])
            wgt = (jnp.where(lower, d_row[h:h + 1, :], 0.0)
                   + jnp.where(upper, d_row[nh + h:nh + h + 1, :], 0.0))
            ms.append((cb * jnp.exp2(arg) * wgt).astype(BF16))
        yd = []
        for h0 in range(0, nh, 2):
            lhs = jnp.concatenate(
                [jnp.concatenate([xtb[h0 * hd:(h0 + 1) * hd], zero_half], axis=0),
                 jnp.concatenate([zero_half, xtb[(h0 + 1) * hd:(h0 + 2) * hd]], axis=0)], axis=1)
            yd.append(ntdot(lhs, jnp.concatenate(ms[h0:h0 + 2], axis=1)))
        states = bdot(xdf_ref[:, sl], bc)
        h_prev = hf_ref[...]
        y_off = ntdot(h_prev.astype(BF16), cc) * head_rows(sc_ref[0 * nh:1 * nh, sl])
        hf_ref[...] = h_prev * head_rows(sc_ref[2 * nh:3 * nh, sl]) + states
        yt_ref[:, sl] = jnp.concatenate(yd, axis=0) + y_off
        return carry

    lax.fori_loop(0, nc, fwd_body, 0, unroll=4)

    def bwd_body(t, carry):
        c = nc - 1 - t
        sl = pl.ds(pl.multiple_of(c * q, q), q)
        bc = b_ref[sl, :].astype(BF16)
        cc = c_ref[sl, :].astype(BF16)
        states = bdot(xdb_ref[:, sl], bc)
        h_prev = hb_ref[...]
        y_off = ntdot(h_prev.astype(BF16), cc) * head_rows(sc_ref[1 * nh:2 * nh, sl])
        hb_ref[...] = h_prev * head_rows(sc_ref[3 * nh:4 * nh, sl]) + states
        y_ref[sl, :] = (yt_ref[:, sl] + y_off).T + dx_ref[...] * x_ref[sl, :]
        return carry

    lax.fori_loop(0, nc, bwd_body, 0, unroll=4)


def ssd_scan_bidir(xbc, dt, dt_bias_f, dt_bias_b, a_log_f, a_log_b, d_skip):
    bsz, s, _ = xbc.shape
    nheads = dt.shape[2] // 2
    nh = nheads // SSM_GROUPS
    gw = nh * SSM_HEAD_DIM
    d_inner = nheads * SSM_HEAD_DIM
    q = SSD_CHUNK
    dt_row = jnp.transpose(dt.reshape(bsz, s, 2 * SSM_GROUPS, nh), (0, 2, 3, 1))
    prm = jnp.stack([jnp.concatenate([dt_bias_f, dt_bias_b]), jnp.concatenate([a_log_f, a_log_b])])
    p_row = jnp.transpose(prm.reshape(2, 2 * SSM_GROUPS, nh), (1, 2, 0))
    dx = jnp.repeat(d_skip, SSM_HEAD_DIM).reshape(SSM_GROUPS, 1, gw)
    assert DECAY_SLOTS * nh <= q and q % nh == 0 and D_STATE == q
    trio = jnp.asarray(np.concatenate([np.triu(np.ones((q, q), np.float32)), np.ones((q, q), np.float32)],
                                      axis=1), BF16)
    bcol = d_inner // D_STATE
    G = SSM_GROUPS
    nc = s // q

    return pl.pallas_call(
        _ssd_t_kernel,
        grid=(bsz, SSM_GROUPS),
        in_specs=[pl.BlockSpec((None, s, gw), lambda b, g: (b, 0, g)),
                  pl.BlockSpec((None, s, D_STATE), lambda b, g: (b, 0, bcol + g)),
                  pl.BlockSpec((None, s, D_STATE), lambda b, g: (b, 0, bcol + G + g)),
                  pl.BlockSpec((None, None, nh, s), lambda b, g: (b, g, 0, 0)),
                  pl.BlockSpec((None, None, nh, s), lambda b, g: (b, G + g, 0, 0)),
                  pl.BlockSpec((None, nh, 2), lambda b, g: (g, 0, 0)),
                  pl.BlockSpec((None, nh, 2), lambda b, g: (G + g, 0, 0)),
                  pl.BlockSpec((None, 1, gw), lambda b, g: (g, 0, 0)),
                  pl.BlockSpec((q, 2 * q), lambda b, g: (0, 0))],
        out_specs=pl.BlockSpec((None, s, gw), lambda b, g: (b, 0, g)),
        out_shape=jax.ShapeDtypeStruct((bsz, s, d_inner), F32),
        scratch_shapes=[pltpu.VMEM((2 * nh, s), F32), pltpu.VMEM((2 * nh, s), F32),
                        pltpu.VMEM((4 * nh, s), F32),
                        pltpu.VMEM((s, q), BF16), pltpu.VMEM((nc, q, 2 * q), BF16),
                        pltpu.VMEM((gw, s), BF16), pltpu.VMEM((gw, s), BF16), pltpu.VMEM((gw, s), BF16),
                        pltpu.VMEM((gw, s), F32),
                        pltpu.VMEM((gw, D_STATE), F32), pltpu.VMEM((gw, D_STATE), F32)],
        compiler_params=_cparams("parallel", "parallel"),
        name="ssd_scan",
    )(xbc, xbc, xbc, dt_row, dt_row, p_row, p_row, dx, trio)


def _gated_proj_kernel(y_ref, z_ref, gw_ref, w_ref, x_ref, g_ref, o_ref):
    t = y_ref[...] * _silu(z_ref[...].astype(F32))
    ms = jnp.mean(t * t, axis=-1, keepdims=True)
    t = (t * lax.rsqrt(ms + EPS) * gw_ref[...]).astype(BF16)
    o_ref[...] = x_ref[...] + g_ref[...] * jnp.dot(t, w_ref[...], preferred_element_type=F32)


def gated_proj_residual(y, zproj, gw, w, x, g, *, ts):
    bsz, s, d = x.shape
    k = y.shape[2]
    return pl.pallas_call(
        _gated_proj_kernel,
        grid=(bsz, s // ts),
        in_specs=[pl.BlockSpec((None, ts, k), lambda b, i: (b, i, 0)),
                  pl.BlockSpec((None, ts, k), lambda b, i: (b, i, 0)),
                  pl.BlockSpec((1, k), lambda b, i: (0, 0)),
                  pl.BlockSpec((k, d), lambda b, i: (0, 0)),
                  pl.BlockSpec((None, ts, d), lambda b, i: (b, i, 0)),
                  pl.BlockSpec((None, 1, d), lambda b, i: (b, 0, 0))],
        out_specs=pl.BlockSpec((None, ts, d), lambda b, i: (b, i, 0)),
        out_shape=jax.ShapeDtypeStruct((bsz, s, d), F32),
        compiler_params=_cparams("parallel", "parallel"),
        name="ssd_out_proj",
    )(y, zproj, gw.reshape(1, k), w, x, g)


def _router_kernel(x_ref, nw_ref, sh_ref, sc_ref, rw_ref, rb_ref, lt_ref,
                   eidx_ref, rank_ref, wgt_ref, cnt_ref):
    h = _modnorm(x_ref[...], nw_ref[...], sh_ref[...], sc_ref[...])
    logits = jnp.dot(h, rw_ref[...], precision=HIGHEST, preferred_element_type=F32) + rb_ref[...]
    ts, ne = logits.shape
    eid = lax.broadcasted_iota(jnp.int32, (ts, ne), 1)
    m1 = jnp.max(logits, axis=-1, keepdims=True)
    i1 = jnp.min(jnp.where(logits == m1, eid, ne), axis=-1, keepdims=True)
    rest = jnp.where(eid == i1, -jnp.inf, logits)
    m2 = jnp.max(rest, axis=-1, keepdims=True)
    i2 = jnp.min(jnp.where(rest == m2, eid, ne), axis=-1, keepdims=True)
    e2 = jnp.exp(m2 - m1)
    w1 = 1.0 / (1.0 + e2)
    w2 = e2 / (1.0 + e2)
    oh1 = (eid == i1).astype(F32)
    oh2 = (eid == i2).astype(F32)
    chosen = oh1 + oh2
    incl = jnp.dot(lt_ref[...], chosen.astype(BF16), preferred_element_type=F32)
    before = incl - chosen
    r1 = jnp.sum(oh1 * before, axis=-1, keepdims=True)
    r2 = jnp.sum(oh2 * before, axis=-1, keepdims=True)
    eidx_ref[...] = jnp.concatenate([i1, i2], axis=1)
    rank_ref[...] = jnp.concatenate([r1, r2], axis=1).astype(jnp.int32)
    wgt_ref[...] = jnp.concatenate([w1, w2], axis=1)
    cnt_ref[...] = incl[ts - 1:ts, :].astype(jnp.int32)


def moe_route(x, nw, sh, sc, router_w, router_b, *, ts):
    bsz, s, d = x.shape
    ne = router_w.shape[1]
    nt = s // ts
    lt = jnp.asarray(np.tril(np.ones((ts, ts), np.float32)), BF16)
    tok = lambda dt: jax.ShapeDtypeStruct((bsz * s, TOP_K), dt)
    tok_spec = pl.BlockSpec((ts, TOP_K), lambda b, i: (b * nt + i, 0))
    return pl.pallas_call(
        _router_kernel,
        grid=(bsz, nt),
        in_specs=[pl.BlockSpec((None, ts, d), lambda b, i: (b, i, 0)),
                  pl.BlockSpec((1, d), lambda b, i: (0, 0)),
                  pl.BlockSpec((None, 1, d), lambda b, i: (b, 0, 0)),
                  pl.BlockSpec((None, 1, d), lambda b, i: (b, 0, 0)),
                  pl.BlockSpec((d, ne), lambda b, i: (0, 0)),
                  pl.BlockSpec((1, ne), lambda b, i: (0, 0)),
                  pl.BlockSpec((ts, ts), lambda b, i: (0, 0))],
        out_specs=[tok_spec, tok_spec, tok_spec,
                   pl.BlockSpec((None, 1, ne), lambda b, i: (b * nt + i, 0, 0))],
        out_shape=[tok(jnp.int32), tok(jnp.int32), tok(F32),
                   jax.ShapeDtypeStruct((bsz * nt, 1, ne), jnp.int32)],
        compiler_params=_cparams("parallel", "parallel"),
        name="moe_router",
    )(x, nw.reshape(1, d), sh, sc, router_w, router_b.reshape(1, ne), lt)


SEG_ROWS = 16
SEG_FIELDS = 3


def _segment_copies(seg_ref, tile, n_experts, make_copy, *, wait):
    for e in range(n_experts):
        base = (tile * n_experts + e) * SEG_FIELDS
        local0 = seg_ref[base]
        global0 = seg_ref[base + 1]

        def body(i, carry, local0=local0, global0=global0):
            cp = make_copy(pl.multiple_of(local0 + i * SEG_ROWS, SEG_ROWS),
                           pl.multiple_of(global0 + i * SEG_ROWS, SEG_ROWS))
            if wait:
                cp.wait()
            else:
                cp.start()
            return carry

        lax.fori_loop(0, seg_ref[base + 2], body, 0)


def _dispatch_kernel(seg_ref, x_ref, nw_ref, sh_ref, sc_ref, ld_ref, hs_in_ref, hs_ref, buf_ref, sem):
    del hs_in_ref
    tt = x_ref.shape[0]
    lc = buf_ref.shape[1]
    ne = N_EXPERTS
    tile = pl.program_id(0) * pl.num_programs(1) + pl.program_id(1)
    last = pl.num_programs(0) * pl.num_programs(1) - 1
    slot = tile % 2

    def copies(t, sl, wait):
        def make_copy(lo, go):
            return pltpu.make_async_copy(buf_ref.at[sl, pl.ds(lo, SEG_ROWS), :],
                                         hs_ref.at[pl.ds(go, SEG_ROWS), :], sem.at[sl])
        _segment_copies(seg_ref, t, ne, make_copy, wait=wait)

    h = _modnorm(x_ref[...], nw_ref[...], sh_ref[...], sc_ref[...]).astype(BF16)
    ld = ld_ref[...]
    rows = lax.broadcasted_iota(jnp.int32, (lc, tt), 0)
    perm = jnp.where(rows == ld[0:1, :], 1.0, jnp.where(rows == ld[1:2, :], 1.0, 0.0)).astype(BF16)
    buf_ref[slot] = jnp.dot(perm, h, preferred_element_type=F32).astype(BF16)
    copies(tile, slot, wait=False)

    @pl.when(tile > 0)
    def _():
        copies(tile - 1, 1 - slot, wait=True)

    @pl.when(tile == last)
    def _():
        copies(tile, slot, wait=True)


def moe_dispatch(x, nw, sh, sc, seg, ldest_rows, n_rows, *, tt, lc):
    bsz, s, d = x.shape
    nt = s // tt
    hs0 = jnp.zeros((n_rows, d), BF16)
    grid_spec = pltpu.PrefetchScalarGridSpec(
        num_scalar_prefetch=1,
        grid=(bsz, nt),
        in_specs=[pl.BlockSpec((None, tt, d), lambda b, i, sref: (b, i, 0)),
                  pl.BlockSpec((1, d), lambda b, i, sref: (0, 0)),
                  pl.BlockSpec((None, 1, d), lambda b, i, sref: (b, 0, 0)),
                  pl.BlockSpec((None, 1, d), lambda b, i, sref: (b, 0, 0)),
                  pl.BlockSpec((TOP_K, tt), lambda b, i, sref: (0, b * nt + i)),
                  pl.BlockSpec(memory_space=pl.ANY)],
        out_specs=pl.BlockSpec(memory_space=pl.ANY),
        scratch_shapes=[pltpu.VMEM((2, lc, d), BF16), pltpu.SemaphoreType.DMA((2,))],
    )
    return pl.pallas_call(
        _dispatch_kernel,
        grid_spec=grid_spec,
        out_shape=jax.ShapeDtypeStruct((n_rows, d), BF16),
        input_output_aliases={6: 0},
        compiler_params=_cparams("arbitrary", "arbitrary"),
        name="moe_dispatch",
    )(seg, x, nw.reshape(1, d), sh, sc, ldest_rows, hs0)


def _moe_kernel(te_ref, nu_ref, hs_ref, w1_ref, w3_ref, w2_ref, o_ref, acc_ref):
    i = pl.program_id(0)
    f = pl.program_id(1)

    @pl.when(i < nu_ref[0])
    def _():
        h = hs_ref[...]
        a = jnp.dot(h, w1_ref[...], preferred_element_type=F32)
        b = jnp.dot(h, w3_ref[...], preferred_element_type=F32)
        t = (_silu(a) * b).astype(BF16)
        contrib = jnp.dot(t, w2_ref[...], preferred_element_type=F32)

        @pl.when(f == 0)
        def _():
            acc_ref[...] = contrib

        @pl.when(f > 0)
        def _():
            acc_ref[...] += contrib

        @pl.when(f == pl.num_programs(1) - 1)
        def _():
            o_ref[...] = acc_ref[...].astype(o_ref.dtype)

    @pl.when((i >= nu_ref[0]) & (f == 0))
    def _():
        o_ref[...] = jnp.zeros_like(o_ref)


def moe_experts(hs, tile_expert, n_used, w1, w3, w2, *, tm, tf):
    n_rows, d = hs.shape
    dff = w1.shape[2]
    nf = dff // tf
    n_tiles = n_rows // tm

    def last_used(i, nu):
        return jnp.maximum(jnp.minimum(i, nu[0] - 1), 0)

    def row_map(i, f, te, nu):
        return (last_used(i, nu), 0)

    def w_in_map(i, f, te, nu):
        return (te[last_used(i, nu)], 0, jnp.where(i < nu[0], f, nf - 1))

    def w_out_map(i, f, te, nu):
        return (te[last_used(i, nu)], jnp.where(i < nu[0], f, nf - 1), 0)

    grid_spec = pltpu.PrefetchScalarGridSpec(
        num_scalar_prefetch=2,
        grid=(n_tiles, nf),
        in_specs=[pl.BlockSpec((tm, d), row_map),
                  pl.BlockSpec((None, d, tf), w_in_map),
                  pl.BlockSpec((None, d, tf), w_in_map),
                  pl.BlockSpec((None, tf, d), w_out_map)],
        out_specs=pl.BlockSpec((tm, d), lambda i, f, te, nu: (i, 0)),
        scratch_shapes=[pltpu.VMEM((tm, d), F32)],
    )
    return pl.pallas_call(
        _moe_kernel,
        grid_spec=grid_spec,
        out_shape=jax.ShapeDtypeStruct((n_rows, d), BF16),
        compiler_params=_cparams("arbitrary", "arbitrary"),
        name="moe_experts",
    )(tile_expert, n_used, hs, w1, w3, w2)


def _combine_kernel(seg_ref, ys_ref, x_ref, g_ref, wgt_ref, ld_ref, o_ref, buf_ref, sem):
    tt = x_ref.shape[0]
    lc = buf_ref.shape[1]
    ne = N_EXPERTS
    tile = pl.program_id(0) * pl.num_programs(1) + pl.program_id(1)
    last = pl.num_programs(0) * pl.num_programs(1) - 1
    slot = tile % 2

    def copies(t, sl, wait):
        def make_copy(lo, go):
            return pltpu.make_async_copy(ys_ref.at[pl.ds(go, SEG_ROWS), :],
                                         buf_ref.at[sl, pl.ds(lo, SEG_ROWS), :], sem.at[sl])
        _segment_copies(seg_ref, t, ne, make_copy, wait=wait)

    @pl.when(tile == 0)
    def _():
        buf_ref[...] = jnp.zeros_like(buf_ref)
        copies(tile, slot, wait=False)

    @pl.when(tile < last)
    def _():
        copies(tile + 1, 1 - slot, wait=False)

    copies(tile, slot, wait=True)

    ld = ld_ref[...]
    cols = lax.broadcasted_iota(jnp.int32, (tt, lc), 1)
    pick = jnp.concatenate([jnp.where(cols == ld[:, k:k + 1], 1.0, 0.0) for k in range(TOP_K)],
                           axis=0).astype(BF16)
    z = jnp.dot(pick, buf_ref[slot], preferred_element_type=F32)
    w = wgt_ref[...]
    mix = w[:, 0:1] * z[0:tt] + w[:, 1:2] * z[tt:]
    o_ref[...] = x_ref[...] + g_ref[...] * mix


def moe_combine(ys, seg, ldest, wgt, x, g, *, tt, lc):
    bsz, s, d = x.shape
    nt = s // tt
    tok_spec = pl.BlockSpec((tt, TOP_K), lambda b, i, sref: (b * nt + i, 0))
    grid_spec = pltpu.PrefetchScalarGridSpec(
        num_scalar_prefetch=1,
        grid=(bsz, nt),
        in_specs=[pl.BlockSpec(memory_space=pl.ANY),
                  pl.BlockSpec((None, tt, d), lambda b, i, sref: (b, i, 0)),
                  pl.BlockSpec((None, 1, d), lambda b, i, sref: (b, 0, 0)),
                  tok_spec, tok_spec],
        out_specs=pl.BlockSpec((None, tt, d), lambda b, i, sref: (b, i, 0)),
        scratch_shapes=[pltpu.VMEM((2, lc, d), BF16), pltpu.SemaphoreType.DMA((2,))],
    )
    return pl.pallas_call(
        _combine_kernel,
        grid_spec=grid_spec,
        out_shape=jax.ShapeDtypeStruct((bsz, s, d), F32),
        compiler_params=_cparams("arbitrary", "arbitrary"),
        name="moe_combine",
    )(seg, ys, x, g, wgt, ldest)


def _round_up(v, m):
    return ((v + m - 1) // m) * m


def moe_residual(x, nw, sh, sc, g, router_w, router_b, w1, w3, w2, *, tm=MOE_TILE_ROWS):
    bsz, s, d = x.shape
    n_tok = bsz * s
    ne = router_w.shape[1]
    tt = min(512, s)
    n_tt = n_tok // tt
    lc = _round_up(TOP_K * tt + ne * SEG_ROWS, 128)
    eidx, rank, wgt, cnt = moe_route(x, nw, sh, sc, router_w, router_b, ts=tt)
    seg_len = _round_up(cnt.reshape(n_tt, ne), SEG_ROWS)
    local_start = jnp.cumsum(seg_len, axis=1) - seg_len
    padded = _round_up(jnp.sum(seg_len, axis=0), tm)
    ends = jnp.cumsum(padded)
    global_start = (ends - padded)[None, :] + jnp.cumsum(seg_len, axis=0) - seg_len
    seg = jnp.stack([local_start, global_start, seg_len // SEG_ROWS], axis=-1).reshape(-1).astype(jnp.int32)
    onehot = eidx[:, :, None] == jnp.arange(ne, dtype=jnp.int32)
    start_tok = jnp.repeat(local_start, tt, axis=0)[:, None, :]
    ldest = (jnp.sum(jnp.where(onehot, start_tok, 0), axis=-1) + rank).astype(jnp.int32)
    n_rows = _round_up(n_tok * TOP_K + n_tt * ne * SEG_ROWS + ne * tm, tm)
    n_tiles = n_rows // tm
    tile_start = jnp.arange(n_tiles, dtype=jnp.int32) * tm
    tile_expert = jnp.minimum(jnp.sum(tile_start[:, None] >= ends[None, :], axis=1), ne - 1).astype(jnp.int32)
    n_used = (ends[ne - 1:ne] // tm).astype(jnp.int32)
    hs = moe_dispatch(x, nw, sh, sc, seg, ldest.T, n_rows, tt=tt, lc=lc)
    ys = moe_experts(hs, tile_expert, n_used, w1, w3, w2, tm=tm, tf=w1.shape[2] // 2)
    return moe_combine(ys, seg, ldest, wgt, x, g, tt=tt, lc=lc)


def _split_mod(mod):
    return [m[:, None, :] for m in jnp.split(mod, 6, axis=-1)]


def even_layer(x, c, rel_bias, ada_w, ada_b, norm1_w, in_w, q_norm_w, k_norm_w, sink, out_w,
               norm2_w, w1, w3, w2):
    s = x.shape[1]
    sh1, sc1, g1, sh2, sc2, g2 = _split_mod(ada_mod(c, ada_w, ada_b))
    proj = norm_mod_matmul(x, norm1_w, sh1, sc1, in_w.astype(BF16), ts=min(1024, s), tn=640, name="even_in_proj",
                           out_dtype=BF16)
    yf = fourier_mix(proj, tq=min(512, s))
    ya = window_attention(proj, band_bias(rel_bias), q_norm_w, k_norm_w, sink)
    x = cat_proj_residual(yf, ya, out_w.astype(BF16), x, g1, ts=min(1024, s))
    dff = w1.shape[1]
    return ffn_residual(x, norm2_w, sh2, sc2, g2, w1.astype(BF16), w3.astype(BF16), w2.astype(BF16),
                        ts=min(1024, s), tf=dff // 2)


def odd_layer(x, c, ada_w, ada_b, norm1_w, in_w, conv_w, conv_b, dt_bias_f, dt_bias_b, a_log_f, a_log_b,
              d_skip, gnorm_w, out_w, norm2_w, router_w, router_b, w1, w3, w2):
    s = x.shape[1]
    sh1, sc1, g1, sh2, sc2, g2 = _split_mod(ada_mod(c, ada_w, ada_b))
    d_inner = gnorm_w.shape[0]
    cdim = conv_w.shape[1]
    wide = d_inner + cdim
    in_w = in_w.astype(BF16)
    zx = norm_mod_matmul(x, norm1_w, sh1, sc1, in_w[:, :wide], ts=min(1024, s), tn=1024, name="odd_in_proj",
                         out_dtype=BF16)
    dt = norm_mod_matmul(x, norm1_w, sh1, sc1, in_w[:, wide:], ts=min(1024, s), tn=in_w.shape[1] - wide,
                         name="odd_dt_proj")
    xbc = conv_silu(zx, conv_w, conv_b, col0=d_inner, tc=512)
    y = ssd_scan_bidir(xbc, dt, dt_bias_f, dt_bias_b, a_log_f, a_log_b, d_skip)
    x = gated_proj_residual(y, zx, gnorm_w, out_w.astype(BF16), x, g1, ts=min(512, s))
    return moe_residual(x, norm2_w, sh2, sc2, g2, router_w, router_b,
                        w1.astype(BF16), w3.astype(BF16), w2.astype(BF16))


def kernel(x, c, rel_bias, ev_ada_w, ev_ada_b, ev_norm1_w, ev_in_w, ev_q_norm_w, ev_k_norm_w, ev_sink, ev_out_w, ev_norm2_w, ev_ffn_w1, ev_ffn_w3, ev_ffn_w2, od_ada_w, od_ada_b, od_norm1_w, od_in_w, od_conv_w, od_conv_b, od_dt_bias_f, od_dt_bias_b, od_A_log_f, od_A_log_b, od_D, od_gnorm_w, od_out_w, od_norm2_w, od_router_w, od_router_b, od_moe_w1, od_moe_w3, od_moe_w2):
    depth = ev_ada_w.shape[0] + od_ada_w.shape[0]
    for i in range(depth):
        j = i // 2
        if i % 2 == 0:
            x = even_layer(x, c, rel_bias, ev_ada_w[j], ev_ada_b[j], ev_norm1_w[j], ev_in_w[j],
                           ev_q_norm_w[j], ev_k_norm_w[j], ev_sink[j], ev_out_w[j], ev_norm2_w[j],
                           ev_ffn_w1[j], ev_ffn_w3[j], ev_ffn_w2[j])
        else:
            x = odd_layer(x, c, od_ada_w[j], od_ada_b[j], od_norm1_w[j], od_in_w[j], od_conv_w[j],
                          od_conv_b[j], od_dt_bias_f[j], od_dt_bias_b[j], od_A_log_f[j], od_A_log_b[j],
                          od_D[j], od_gnorm_w[j], od_out_w[j], od_norm2_w[j], od_router_w[j],
                          od_router_b[j], od_moe_w1[j], od_moe_w3[j], od_moe_w2[j])
    return x
```

```python
import functools

import numpy as np
import jax
import jax.numpy as jnp
from jax import lax
from jax.experimental import pallas as pl
from jax.experimental.pallas import tpu as pltpu

F32 = jnp.float32
BF16 = jnp.bfloat16
HIGHEST = lax.Precision.HIGHEST

EPS = 1e-6
FNET_GROUPS = 4
FNET_GROUP_DIM = 128
FNET_WIDTH = FNET_GROUPS * FNET_GROUP_DIM
ATTN_HEADS = 8
ATTN_KV_HEADS = 2
HEAD_DIM = 64
ATTN_WIDTH = ATTN_HEADS * HEAD_DIM
KV_WIDTH = ATTN_KV_HEADS * HEAD_DIM
WINDOW = 128
BLOCK = 128
REL_BUCKETS = 32
REL_MAX_DIST = 128
SSM_HEAD_DIM = 64
SSM_GROUPS = 4
D_STATE = 128
CONV_WIDTH = 5
SSD_CHUNK = 128
N_EXPERTS = 8
TOP_K = 2
NEG_BIG = -1e30
ATTN_QBLOCKS = 2

V7X_VMEM_LIMIT_BYTES = 56 * 1024 * 1024
MOE_TILE_ROWS = 512


def _cparams(*sem):
    return pltpu.CompilerParams(dimension_semantics=sem, vmem_limit_bytes=V7X_VMEM_LIMIT_BYTES)


def _modnorm(x, nw, sh, sc):
    ms = jnp.mean(x * x, axis=-1, keepdims=True)
    return x * lax.rsqrt(ms + EPS) * nw * (1.0 + sc) + sh


def _silu(x):
    return x * (1.0 / (1.0 + jnp.exp(-x)))


def _ada_kernel(c_ref, w_ref, b_ref, o_ref):
    cs = _silu(c_ref[...]).astype(BF16)
    o_ref[...] = jnp.dot(cs, w_ref[...].astype(BF16), preferred_element_type=F32) + b_ref[...]


def ada_mod(c, w, b):
    bsz, d = c.shape
    n = w.shape[1]
    tn = 1536
    return pl.pallas_call(
        _ada_kernel,
        grid=(n // tn,),
        in_specs=[pl.BlockSpec((bsz, d), lambda j: (0, 0)),
                  pl.BlockSpec((d, tn), lambda j: (0, j)),
                  pl.BlockSpec((1, tn), lambda j: (0, j))],
        out_specs=pl.BlockSpec((bsz, tn), lambda j: (0, j)),
        out_shape=jax.ShapeDtypeStruct((bsz, n), F32),
        compiler_params=_cparams("arbitrary"),
        name="ada_mod",
    )(c, w, b.reshape(1, n))


def _nmm_kernel(x_ref, nw_ref, sh_ref, sc_ref, w_ref, o_ref, *, tn):
    h = _modnorm(x_ref[...], nw_ref[...], sh_ref[...], sc_ref[...]).astype(BF16)
    n = w_ref.shape[1]
    for lo in range(0, n, tn):
        o_ref[:, lo:lo + tn] = jnp.dot(h, w_ref[:, lo:lo + tn],
                                       preferred_element_type=F32).astype(o_ref.dtype)


def _resident(shape):
    return pl.BlockSpec(shape, lambda *_: tuple(0 for _ in shape), pipeline_mode=pl.Buffered(1))


def norm_mod_matmul(x, nw, sh, sc, w, *, ts, tn, name, out_dtype=F32):
    bsz, s, d = x.shape
    n = w.shape[1]
    assert n % tn == 0
    return pl.pallas_call(
        functools.partial(_nmm_kernel, tn=tn),
        grid=(bsz, s // ts),
        in_specs=[pl.BlockSpec((None, ts, d), lambda b, i: (b, i, 0)),
                  pl.BlockSpec((1, d), lambda b, i: (0, 0)),
                  pl.BlockSpec((None, 1, d), lambda b, i: (b, 0, 0)),
                  pl.BlockSpec((None, 1, d), lambda b, i: (b, 0, 0)),
                  _resident((d, n))],
        out_specs=pl.BlockSpec((None, ts, n), lambda b, i: (b, i, 0)),
        out_shape=jax.ShapeDtypeStruct((bsz, s, n), out_dtype),
        compiler_params=_cparams("parallel", "parallel"),
        name=name,
    )(x, nw.reshape(1, d), sh, sc, w)


def _dft_cos_sin(n):
    k = np.arange(n, dtype=np.int64)
    ang = ((k[:, None] * k[None, :]) % n).astype(np.float64) * (2.0 * np.pi / n)
    scale = 1.0 / np.sqrt(n)
    return np.cos(ang) * scale, np.sin(ang) * scale


def _fourier_kernel(u_ref, chan_ref, seq_ref, o_ref, ab_ref):
    s = u_ref.shape[0]

    @pl.when(pl.program_id(1) == 0)
    def _():
        for g in range(FNET_GROUPS):
            lo, hi = g * FNET_GROUP_DIM, (g + 1) * FNET_GROUP_DIM
            ug = u_ref[:, lo:hi].astype(BF16)
            cs = jnp.dot(ug, chan_ref[...], preferred_element_type=F32)
            ab_ref[0:s, lo:hi] = cs[:, :FNET_GROUP_DIM].astype(BF16)
            ab_ref[s:2 * s, lo:hi] = cs[:, FNET_GROUP_DIM:].astype(BF16)

    o_ref[...] = jnp.dot(seq_ref[...], ab_ref[...], preferred_element_type=F32)


def fourier_mix(proj, *, tq):
    bsz, s, _ = proj.shape
    cc, sc = _dft_cos_sin(FNET_GROUP_DIM)
    chan = jnp.asarray(np.concatenate([cc, sc], axis=1), BF16)
    cs, ss = _dft_cos_sin(s)
    seq = jnp.asarray(np.concatenate([cs, -ss], axis=1), BF16)
    return pl.pallas_call(
        _fourier_kernel,
        grid=(bsz, s // tq),
        in_specs=[pl.BlockSpec((None, s, FNET_WIDTH), lambda b, i: (b, 0, 0)),
                  pl.BlockSpec((FNET_GROUP_DIM, 2 * FNET_GROUP_DIM), lambda b, i: (0, 0)),
                  pl.BlockSpec((tq, 2 * s), lambda b, i: (i, 0))],
        out_specs=pl.BlockSpec((None, tq, FNET_WIDTH), lambda b, i: (b, i, 0)),
        out_shape=jax.ShapeDtypeStruct((bsz, s, FNET_WIDTH), F32),
        scratch_shapes=[pltpu.VMEM((2 * s, FNET_WIDTH), BF16)],
        compiler_params=_cparams("parallel", "arbitrary"),
        name="fourier_mix",
    )(proj, chan, seq)


def _band_bucket_table():
    i = np.arange(BLOCK)[:, None]
    j = np.arange(3 * BLOCK)[None, :]
    rel = (j - BLOCK) - i
    half = REL_BUCKETS // 2
    max_exact = half // 2
    n = np.abs(rel)
    large = max_exact + (np.log(np.maximum(n, 1) / max_exact)
                         / np.log(REL_MAX_DIST / max_exact) * (half - max_exact)).astype(np.int32)
    large = np.minimum(large, half - 1)
    bucket = (rel > 0).astype(np.int32) * half + np.where(n < max_exact, n, large)
    return np.where(n <= WINDOW, bucket, -1).astype(np.int32)


def _bias_kernel(rb_ref, bucket_ref, o_ref):
    h = pl.program_id(0)
    bucket = bucket_ref[...]
    acc = jnp.full(bucket.shape, NEG_BIG, F32)
    for bkt in range(REL_BUCKETS):
        acc = jnp.where(bucket == bkt, rb_ref[bkt * ATTN_HEADS + h], acc)
    o_ref[...] = acc


def band_bias(rel_bias):
    bucket = jnp.asarray(_band_bucket_table())
    return pl.pallas_call(
        _bias_kernel,
        grid=(ATTN_HEADS,),
        in_specs=[pl.BlockSpec(memory_space=pltpu.SMEM),
                  pl.BlockSpec((BLOCK, 3 * BLOCK), lambda h: (0, 0))],
        out_specs=pl.BlockSpec((None, BLOCK, 3 * BLOCK), lambda h: (h, 0, 0)),
        out_shape=jax.ShapeDtypeStruct((ATTN_HEADS, BLOCK, 3 * BLOCK), F32),
        compiler_params=_cparams("arbitrary"),
        name="band_bias",
    )(rel_bias.reshape(-1), bucket)


def _head_mean_matrix(width):
    m = np.zeros((width, width), np.float32)
    for h in range(width // HEAD_DIM):
        m[h * HEAD_DIM:(h + 1) * HEAD_DIM, h * HEAD_DIM:(h + 1) * HEAD_DIM] = 1.0 / HEAD_DIM
    return m


def _heads_rms(t, mean_mat, w):
    sq = t * t
    hi = sq.astype(BF16)
    lo = (sq - hi.astype(F32)).astype(BF16)
    ms = (jnp.dot(hi, mean_mat, preferred_element_type=F32)
          + jnp.dot(lo, mean_mat, preferred_element_type=F32))
    return t * lax.rsqrt(ms + EPS) * w


def _attn_kernel(sink_ref, q_ref, kl_ref, kc_ref, kr_ref, vl_ref, vc_ref, vr_ref,
                 bias_ref, qnw_ref, knw_ref, qmean_ref, kmean_ref, o_ref):
    n = pl.program_id(1)
    nb = pl.num_programs(1) * ATTN_QBLOCKS
    k = jnp.concatenate([kl_ref[...], kc_ref[...], kr_ref[...]], axis=0).astype(F32)
    v = jnp.concatenate([vl_ref[...], vc_ref[...], vr_ref[...]], axis=0).astype(F32)
    col = lax.broadcasted_iota(jnp.int32, (1, 3 * BLOCK), 1)
    qn = _heads_rms(q_ref[...].astype(F32), qmean_ref[...], qnw_ref[...])
    kn = _heads_rms(k, kmean_ref[...], knw_ref[...])
    low = lax.broadcasted_iota(jnp.int32, (1, 2 * HEAD_DIM), 1) < HEAD_DIM
    kn_sw = pltpu.roll(kn, HEAD_DIM, axis=1)
    v_sw = pltpu.roll(v, HEAD_DIM, axis=1)
    k_dup = [jnp.where(low, kn, kn_sw).astype(BF16), jnp.where(low, kn_sw, kn).astype(BF16)]
    ones = jnp.ones((k.shape[0], 2 * HEAD_DIM), BF16)
    v_ext = [jnp.concatenate([v.astype(BF16), ones], axis=1),
             jnp.concatenate([v_sw.astype(BF16), ones], axis=1)]
    g = ATTN_HEADS // ATTN_KV_HEADS
    for qb in range(ATTN_QBLOCKS):
        blk = n * ATTN_QBLOCKS + qb
        band = slice(qb * BLOCK, (qb + 3) * BLOCK)
        first_key = jnp.where(blk == 0, BLOCK, 0)
        end_key = jnp.where(blk == nb - 1, 2 * BLOCK, 3 * BLOCK)
        outside = (col < first_key) | (col >= end_key)
        pairs = []
        for m in range(ATTN_HEADS // 2):
            j = (2 * m) // g
            qp = qn[qb * BLOCK:(qb + 1) * BLOCK, m * 2 * HEAD_DIM:(m + 1) * 2 * HEAD_DIM]
            res = []
            for idx in range(2):
                h = 2 * m + idx
                qm = jnp.where(low if idx == 0 else jnp.logical_not(low), qp, 0.0).astype(BF16)
                logits = lax.dot_general(qm, k_dup[j][band], (((1,), (1,)), ((), ())),
                                         preferred_element_type=F32)
                logits = jnp.where(outside, NEG_BIG, logits + bias_ref[h])
                sk = sink_ref[h]
                mx = jnp.maximum(jnp.max(logits, axis=-1, keepdims=True), sk)
                p = jnp.exp(logits - mx).astype(BF16)
                r = jnp.dot(p, v_ext[idx if j == 0 else 1 - idx][band], preferred_element_type=F32)
                denom = r[:, 2 * HEAD_DIM:] + jnp.exp(sk - mx)
                res.append(r[:, :2 * HEAD_DIM] / denom)
            pairs.append(jnp.where(low, res[0], res[1]))
        o_ref[qb * BLOCK:(qb + 1) * BLOCK, :] = jnp.concatenate(pairs, axis=-1)


def window_attention(proj, bias, q_norm_w, k_norm_w, sink):
    bsz, s, _ = proj.shape
    nb = s // BLOCK
    qcol = FNET_WIDTH // ATTN_WIDTH
    kcol = (FNET_WIDTH + ATTN_WIDTH) // KV_WIDTH
    vcol = kcol + 1

    qb = ATTN_QBLOCKS
    assert nb % qb == 0

    def kv_specs(col):
        return [pl.BlockSpec((None, BLOCK, KV_WIDTH), lambda b, n: (b, jnp.maximum(n * qb - 1, 0), col)),
                pl.BlockSpec((None, qb * BLOCK, KV_WIDTH), lambda b, n: (b, n, col)),
                pl.BlockSpec((None, BLOCK, KV_WIDTH), lambda b, n: (b, jnp.minimum((n + 1) * qb, nb - 1), col))]

    return pl.pallas_call(
        _attn_kernel,
        grid=(bsz, nb // qb),
        in_specs=[pl.BlockSpec(memory_space=pltpu.SMEM),
                  pl.BlockSpec((None, qb * BLOCK, ATTN_WIDTH), lambda b, n: (b, n, qcol)),
                  *kv_specs(kcol), *kv_specs(vcol),
                  pl.BlockSpec((ATTN_HEADS, BLOCK, 3 * BLOCK), lambda b, n: (0, 0, 0)),
                  pl.BlockSpec((1, ATTN_WIDTH), lambda b, n: (0, 0)),
                  pl.BlockSpec((1, KV_WIDTH), lambda b, n: (0, 0)),
                  pl.BlockSpec((ATTN_WIDTH, ATTN_WIDTH), lambda b, n: (0, 0)),
                  pl.BlockSpec((KV_WIDTH, KV_WIDTH), lambda b, n: (0, 0))],
        out_specs=pl.BlockSpec((None, qb * BLOCK, ATTN_WIDTH), lambda b, n: (b, n, 0)),
        out_shape=jax.ShapeDtypeStruct((bsz, s, ATTN_WIDTH), F32),
        compiler_params=_cparams("parallel", "arbitrary"),
        name="window_attention",
    )(sink, proj, proj, proj, proj, proj, proj, proj, bias,
      (jnp.tile(q_norm_w, ATTN_HEADS) * (HEAD_DIM ** -0.5)).reshape(1, ATTN_WIDTH),
      jnp.tile(k_norm_w, ATTN_KV_HEADS).reshape(1, KV_WIDTH),
      jnp.asarray(_head_mean_matrix(ATTN_WIDTH), BF16), jnp.asarray(_head_mean_matrix(KV_WIDTH), BF16))


def _cat_proj_kernel(a1_ref, a2_ref, w_ref, x_ref, g_ref, o_ref):
    k1 = a1_ref.shape[1]
    y = jnp.dot(a1_ref[...].astype(BF16), w_ref[0:k1, :], preferred_element_type=F32)
    y = y + jnp.dot(a2_ref[...].astype(BF16), w_ref[k1:, :], preferred_element_type=F32)
    o_ref[...] = x_ref[...] + g_ref[...] * y


def cat_proj_residual(a1, a2, w, x, g, *, ts):
    bsz, s, d = x.shape
    k1, k2 = a1.shape[2], a2.shape[2]
    return pl.pallas_call(
        _cat_proj_kernel,
        grid=(bsz, s // ts),
        in_specs=[pl.BlockSpec((None, ts, k1), lambda b, i: (b, i, 0)),
                  pl.BlockSpec((None, ts, k2), lambda b, i: (b, i, 0)),
                  pl.BlockSpec((k1 + k2, d), lambda b, i: (0, 0)),
                  pl.BlockSpec((None, ts, d), lambda b, i: (b, i, 0)),
                  pl.BlockSpec((None, 1, d), lambda b, i: (b, 0, 0))],
        out_specs=pl.BlockSpec((None, ts, d), lambda b, i: (b, i, 0)),
        out_shape=jax.ShapeDtypeStruct((bsz, s, d), F32),
        compiler_params=_cparams("parallel", "parallel"),
        name="mixer_out_proj",
    )(a1, a2, w, x, g)


def _ffn_kernel(x_ref, nw_ref, sh_ref, sc_ref, g_ref, w1_ref, w3_ref, w2_ref, o_ref, acc_ref, *, tf):
    h = _modnorm(x_ref[...], nw_ref[...], sh_ref[...], sc_ref[...]).astype(BF16)
    dff = w1_ref.shape[1]
    for lo in range(0, dff, tf):
        a = jnp.dot(h, w1_ref[:, lo:lo + tf], preferred_element_type=F32)
        b = jnp.dot(h, w3_ref[:, lo:lo + tf], preferred_element_type=F32)
        t = (_silu(a) * b).astype(BF16)
        contrib = jnp.dot(t, w2_ref[lo:lo + tf, :], preferred_element_type=F32)
        if lo == 0:
            acc_ref[...] = contrib
        else:
            acc_ref[...] += contrib
    o_ref[...] = x_ref[...] + g_ref[...] * acc_ref[...]


def ffn_residual(x, nw, sh, sc, g, w1, w3, w2, *, ts, tf):
    bsz, s, d = x.shape
    dff = w1.shape[1]
    assert dff % tf == 0
    vec = pl.BlockSpec((None, 1, d), lambda b, i: (b, 0, 0))
    return pl.pallas_call(
        functools.partial(_ffn_kernel, tf=tf),
        grid=(bsz, s // ts),
        in_specs=[pl.BlockSpec((None, ts, d), lambda b, i: (b, i, 0)),
                  pl.BlockSpec((1, d), lambda b, i: (0, 0)),
                  vec, vec, vec,
                  _resident((d, dff)), _resident((d, dff)), _resident((dff, d))],
        out_specs=pl.BlockSpec((None, ts, d), lambda b, i: (b, i, 0)),
        out_shape=jax.ShapeDtypeStruct((bsz, s, d), F32),
        scratch_shapes=[pltpu.VMEM((ts, d), F32)],
        compiler_params=_cparams("parallel", "parallel"),
        name="ffn_swiglu",
    )(x, nw.reshape(1, d), sh, sc, g, w1, w3, w2)


CONV_PAD = 8
CONV_ROWS = 256


def _conv_kernel(x_ref, w_ref, b_ref, o_ref, pad_ref):
    s, tc = x_ref.shape
    zeros = jnp.zeros((CONV_PAD, tc), F32)
    pad_ref[0:CONV_PAD, :] = zeros
    pad_ref[CONV_PAD + s:, :] = zeros
    pad_ref[CONV_PAD:CONV_PAD + s, :] = x_ref[...].astype(F32)
    half = CONV_WIDTH // 2
    for r in range(s // CONV_ROWS):
        base = CONV_PAD + r * CONV_ROWS - half
        acc = jnp.zeros((CONV_ROWS, tc), F32) + b_ref[...]
        for kk in range(CONV_WIDTH):
            acc = acc + pad_ref[base + kk:base + kk + CONV_ROWS, :] * w_ref[kk:kk + 1, :]
        o_ref[r * CONV_ROWS:(r + 1) * CONV_ROWS, :] = _silu(acc)


def conv_silu(proj, conv_w, conv_b, *, col0, tc):
    bsz, s, _ = proj.shape
    cdim = conv_w.shape[1]
    cb0 = col0 // tc
    return pl.pallas_call(
        _conv_kernel,
        grid=(bsz, cdim // tc),
        in_specs=[pl.BlockSpec((None, s, tc), lambda b, j: (b, 0, cb0 + j)),
                  pl.BlockSpec((CONV_WIDTH, tc), lambda b, j: (0, j)),
                  pl.BlockSpec((1, tc), lambda b, j: (0, j))],
        out_specs=pl.BlockSpec((None, s, tc), lambda b, j: (b, 0, j)),
        out_shape=jax.ShapeDtypeStruct((bsz, s, cdim), F32),
        scratch_shapes=[pltpu.VMEM((s + 2 * CONV_PAD, tc), F32)],
        compiler_params=_cparams("parallel", "parallel"),
        name="conv_silu",
    )(proj, conv_w, conv_b.reshape(1, cdim))


def _softplus(x):
    return jnp.maximum(x, 0.0) + jnp.log(1.0 + jnp.exp(-jnp.abs(x)))


def _expand_matrix(n_in, width):
    e = np.zeros((n_in, n_in * width), np.float32)
    for h in range(n_in):
        e[h, h * width:(h + 1) * width] = 1.0
    return e


LOG2E = 1.4426950408889634
DECAY_SLOTS = 12


def _decay_placement(nh):
    place = np.zeros((3, 2 * nh, 3 * 128), np.float32)
    const = np.zeros((1, 3 * 128), np.float32)
    for h in range(nh):
        for part in range(3):
            place[part, h, part * nh + h] = 1.0
            place[part, nh + h, (6 + part) * nh + h] = -1.0
            place[part, h, 128 + (3 + part) * nh + h] = -1.0
            place[part, nh + h, 256 + (9 + part) * nh + h] = 1.0
            const[0, (3 + part) * nh + h] = 1.0
            const[0, (9 + part) * nh + h] = 1.0
            const[0, 128 + part * nh + h] = 1.0
            const[0, 256 + (6 + part) * nh + h] = 1.0
    return place, const


def _split3(v):
    hi = v.astype(BF16)
    r = v - hi.astype(F32)
    mid = r.astype(BF16)
    lo = (r - mid.astype(F32)).astype(BF16)
    return hi, mid, lo


def _ssd_kernel(x_ref, b_ref, c_ref, dtc_f_ref, dtc_b_ref, dtr_f_ref, dtr_b_ref,
                pc_f_ref, pc_b_ref, pr_f_ref, pr_b_ref, dx_ref,
                tri_ref, place_ref, pconst_ref, e64_ref, y_ref,
                ac_ref, dc_ref, dr_ref, pq_ref, xdf_ref, xdb_ref, eif_ref, er_ref, decf_ref, decb_ref,
                hf_ref, hb_ref):
    s = x_ref.shape[0]
    q = SSD_CHUNK
    nh = pc_f_ref.shape[1]
    nc = s // q
    hd = SSM_HEAD_DIM

    def col_params(raw_ref, p_ref):
        dt = _softplus(raw_ref[...] + p_ref[0:1, :])
        return dt, (-LOG2E) * jnp.exp(p_ref[1:2, :]) * dt

    dcf, acf = col_params(dtc_f_ref, pc_f_ref)
    dcb, acb = col_params(dtc_b_ref, pc_b_ref)
    dc_ref[:, 0:nh] = dcf
    dc_ref[:, nh:] = dcb
    ac_ref[:, 0:nh] = acf
    ac_ref[:, nh:] = acb
    dr_ref[0:nh, :] = _softplus(dtr_f_ref[...] + pr_f_ref[:, 0:1])
    dr_ref[nh:, :] = _softplus(dtr_b_ref[...] + pr_b_ref[:, 0:1])

    li = lax.broadcasted_iota(jnp.int32, (q, q), 0)
    si = lax.broadcasted_iota(jnp.int32, (q, q), 1)
    lower = li >= si
    upper = li <= si
    slot_head = lax.broadcasted_iota(jnp.int32, (1, 128), 1) % nh
    pair_lo = lax.broadcasted_iota(jnp.int32, (1, 2 * hd), 1) < hd

    def bdot(a, b):
        return jnp.dot(a, b, preferred_element_type=F32)

    def chunk_cumsums(sl):
        a_col = ac_ref[sl, :]
        tri = tri_ref[...]
        hi, mid, lo = _split3(a_col)
        return a_col, bdot(tri, hi) + bdot(tri, mid) + bdot(tri, lo)

    def expand64(v, parts=2):
        e = e64_ref[...]
        hi = v.astype(BF16)
        out = bdot(hi, e)
        if parts == 2:
            out = out + bdot((v - hi.astype(F32)).astype(BF16), e)
        return out

    hf_ref[...] = jnp.zeros_like(hf_ref)
    hb_ref[...] = jnp.zeros_like(hb_ref)

    def prep_body(c, carry):
        sl = pl.ds(pl.multiple_of(c * q, q), q)
        a_col, i_col = chunk_cumsums(sl)
        d_col = dc_ref[sl, :]
        if_col = i_col[:, 0:nh]
        tot_f = i_col[q - 1:q, 0:nh]
        tot_b = i_col[q - 1:q, nh:]
        ie = i_col - jnp.where(lax.broadcasted_iota(jnp.int32, (1, 2 * nh), 1) < nh, 0.0, a_col)
        eb_col = ie[:, nh:]
        parts = _split3(ie)
        placed = pconst_ref[...]
        for part in range(3):
            placed = placed + bdot(parts[part], place_ref[part])
        pq_ref[sl, :] = placed.astype(BF16)
        xc = x_ref[sl, :]
        xdf_ref[sl, :] = (xc * expand64(jnp.exp2(tot_f - if_col) * d_col[:, 0:nh], parts=1)).astype(BF16)
        xdb_ref[sl, :] = (xc * expand64(jnp.exp2(eb_col) * d_col[:, nh:], parts=1)).astype(BF16)
        eif_ref[sl, :] = expand64(jnp.exp2(if_col))
        er_ref[sl, :] = expand64(jnp.exp2(tot_b - eb_col))
        dec = expand64(jnp.exp2(jnp.concatenate([tot_f, tot_b], axis=0)))
        decf_ref[c] = jnp.broadcast_to(dec[0:1, :], decf_ref.shape[1:])
        decb_ref[c] = jnp.broadcast_to(dec[1:2, :], decb_ref.shape[1:])
        return carry

    lax.fori_loop(0, nc, prep_body, 0, unroll=4)

    def fwd_body(c, carry):
        sl = pl.ds(pl.multiple_of(c * q, q), q)
        xc = x_ref[sl, :]
        bc = b_ref[sl, :].astype(BF16)
        cc = c_ref[sl, :].astype(BF16)
        d_row = dr_ref[:, sl]
        placed = pq_ref[sl, :]
        p_all = placed[:, 0:128]
        q_cat = jnp.concatenate([placed[:, 128:256], placed[:, 256:384]], axis=0)

        cb = lax.dot_general(cc, bc, (((1,), (1,)), ((), ())), preferred_element_type=F32)
        xb = xc.astype(BF16)
        ys = []
        for h0 in range(0, nh, 2):
            ms = []
            for h in (h0, h0 + 1):
                ph = jnp.where(slot_head == h, p_all, jnp.zeros_like(p_all))
                g2 = lax.dot_general(ph, q_cat, (((1,), (1,)), ((), ())), preferred_element_type=F32)
                arg = jnp.where(lower, g2[:, 0:q], g2[:, q:])
                wgt = (jnp.where(lower, d_row[h:h + 1, :], 0.0)
                       + jnp.where(upper, d_row[nh + h:nh + h + 1, :], 0.0))
                ms.append((cb * jnp.exp2(arg) * wgt).astype(BF16))
            xp = xb[:, h0 * hd:(h0 + 2) * hd]
            zero = jnp.zeros_like(xp)
            rhs = jnp.concatenate([jnp.where(pair_lo, xp, zero), jnp.where(pair_lo, zero, xp)], axis=0)
            ys.append(bdot(jnp.concatenate(ms, axis=1), rhs))
        y = jnp.concatenate(ys, axis=1) + dx_ref[...] * xc

        states = lax.dot_general(bc, xdf_ref[sl, :], (((0,), (0,)), ((), ())),
                                 preferred_element_type=F32)
        h_prev = hf_ref[...]
        y = y + bdot(cc, h_prev.astype(BF16)) * eif_ref[sl, :]
        hf_ref[...] = h_prev * decf_ref[c][0:1, :] + states
        y_ref[sl, :] = y
        return carry

    lax.fori_loop(0, nc, fwd_body, 0, unroll=2)

    def bwd_body(t, carry):
        c = nc - 1 - t
        sl = pl.ds(pl.multiple_of(c * q, q), q)
        bc = b_ref[sl, :].astype(BF16)
        cc = c_ref[sl, :].astype(BF16)
        states = lax.dot_general(bc, xdb_ref[sl, :], (((0,), (0,)), ((), ())),
                                 preferred_element_type=F32)
        h_prev = hb_ref[...]
        y_ref[sl, :] += bdot(cc, h_prev.astype(BF16)) * er_ref[sl, :]
        hb_ref[...] = h_prev * decb_ref[c][0:1, :] + states
        return carry

    lax.fori_loop(0, nc, bwd_body, 0, unroll=4)


def _bf16_parts3(v):
    hi = v.astype(BF16).astype(F32)
    r = v - hi
    mid = r.astype(BF16).astype(F32)
    lo = (r - mid).astype(BF16).astype(F32)
    return hi, mid, lo


def _ssd_t_kernel(x_ref, b_ref, c_ref, dt_f_ref, dt_b_ref, p_f_ref, p_b_ref, dx_ref, trio_ref, y_ref,
                  ar_ref, dr_ref, sc_ref, pq_ref, qt_ref, xtb_ref, xdf_ref, xdb_ref, yt_ref, hf_ref, hb_ref):
    s = x_ref.shape[0]
    q = SSD_CHUNK
    nh = p_f_ref.shape[0]
    nc = s // q
    hd = SSM_HEAD_DIM

    def row_params(raw_ref, p_ref):
        dt = _softplus(raw_ref[...] + p_ref[:, 0:1])
        return dt, (-LOG2E) * jnp.exp(p_ref[:, 1:2]) * dt

    dtf, af = row_params(dt_f_ref, p_f_ref)
    dtb, ab = row_params(dt_b_ref, p_b_ref)
    dr_ref[0:nh, :] = dtf
    dr_ref[nh:, :] = dtb
    ar_ref[0:nh, :] = af
    ar_ref[nh:, :] = ab

    li = lax.broadcasted_iota(jnp.int32, (q, q), 0)
    si = lax.broadcasted_iota(jnp.int32, (q, q), 1)
    lower = li >= si
    upper = li <= si
    slot_head = lax.broadcasted_iota(jnp.int32, (1, q), 1) % nh

    def bdot(a, b):
        return jnp.dot(a, b, preferred_element_type=F32)

    def ntdot(a, b):
        return lax.dot_general(a, b, (((1,), (1,)), ((), ())), preferred_element_type=F32)

    def head_rows(v):
        return jnp.concatenate([jnp.broadcast_to(v[h:h + 1, :], (hd, q)) for h in range(nh)], axis=0)

    ones = jnp.ones((nh, q), F32)
    zeros = jnp.zeros((nh, q), F32)

    def prep_body(c, carry):
        sl = pl.ds(pl.multiple_of(c * q, q), q)
        a_row = ar_ref[:, sl]
        d_row = dr_ref[:, sl]
        parts = jnp.concatenate([p.astype(BF16) for p in _bf16_parts3(a_row)], axis=0)
        cs = bdot(parts, trio_ref[...])
        cs = cs[0:2 * nh] + cs[2 * nh:4 * nh] + cs[4 * nh:6 * nh]
        i_f = cs[0:nh, 0:q]
        e_b = cs[nh:, 0:q] - a_row[nh:]
        tot_f = cs[0:nh, q:]
        tot_b = cs[nh:, q:]
        ih, im, il = _bf16_parts3(i_f)
        eh, em, el = _bf16_parts3(e_b)
        pad = [zeros] * (q // nh - DECAY_SLOTS)
        p_t = jnp.concatenate([ih, im, il, ones, ones, ones, -eh, -em, -el, ones, ones, ones] + pad, axis=0)
        qf_t = jnp.concatenate([ones, ones, ones, -ih, -im, -il] + [zeros] * 6 + pad, axis=0)
        qb_t = jnp.concatenate([zeros] * 6 + [ones, ones, ones, eh, em, el] + pad, axis=0)
        pq_ref[sl, :] = p_t.T.astype(BF16)
        qt_ref[c] = jnp.concatenate([qf_t, qb_t], axis=1).astype(BF16)
        sc_ref[0 * nh:1 * nh, sl] = jnp.exp2(i_f)
        sc_ref[1 * nh:2 * nh, sl] = jnp.exp2(tot_b - e_b)
        sc_ref[2 * nh:3 * nh, sl] = jnp.exp2(tot_f)
        sc_ref[3 * nh:4 * nh, sl] = jnp.exp2(tot_b)
        xt = x_ref[sl, :].T
        xtb_ref[:, sl] = xt.astype(BF16)
        xdf_ref[:, sl] = (xt * head_rows(jnp.exp2(tot_f - i_f) * d_row[0:nh])).astype(BF16)
        xdb_ref[:, sl] = (xt * head_rows(jnp.exp2(e_b) * d_row[nh:])).astype(BF16)
        return carry

    lax.fori_loop(0, nc, prep_body, 0, unroll=2)

    hf_ref[...] = jnp.zeros_like(hf_ref)
    hb_ref[...] = jnp.zeros_like(hb_ref)
    zero_half = jnp.zeros((hd, q), BF16)

    def fwd_body(c, carry):
        sl = pl.ds(pl.multiple_of(c * q, q), q)
        bc = b_ref[sl, :].astype(BF16)
        cc = c_ref[sl, :].astype(BF16)
        d_row = dr_ref[:, sl]
        p_all = pq_ref[sl, :]
        q_t = qt_ref[c]
        xtb = xtb_ref[:, sl]
        cb = ntdot(cc, bc)
        g2s = [bdot(jnp.where(slot_head == h, p_all, jnp.zeros_like(p_all)), q_t) for h in range(nh)]
        ms = []
        for h in range(nh):
            arg = jnp.where(lower, g2s[h][:, 0:q], g2s[h][:, q:])
            wgt = (jnp.where(lower, d_row[h:h + 1, :], 0.0)
                   + jnp.where(upper, d_row[nh + h:nh + h + 1, :], 0.0))
            ms.append((cb * jnp.exp2(arg) * wgt).astype(BF16))
        yd = []
        for h0 in range(0, nh, 2):
            lhs = jnp.concatenate(
                [jnp.concatenate([xtb[h0 * hd:(h0 + 1) * hd], zero_half], axis=0),
                 jnp.concatenate([zero_half, xtb[(h0 + 1) * hd:(h0 + 2) * hd]], axis=0)], axis=1)
            yd.append(ntdot(lhs, jnp.concatenate(ms[h0:h0 + 2], axis=1)))
        states = bdot(xdf_ref[:, sl], bc)
        h_prev = hf_ref[...]
        y_off = ntdot(h_prev.astype(BF16), cc) * head_rows(sc_ref[0 * nh:1 * nh, sl])
        hf_ref[...] = h_prev * head_rows(sc_ref[2 * nh:3 * nh, sl]) + states
        yt_ref[:, sl] = jnp.concatenate(yd, axis=0) + y_off
        return carry

    lax.fori_loop(0, nc, fwd_body, 0, unroll=4)

    def bwd_body(t, carry):
        c = nc - 1 - t
        sl = pl.ds(pl.multiple_of(c * q, q), q)
        bc = b_ref[sl, :].astype(BF16)
        cc = c_ref[sl, :].astype(BF16)
        states = bdot(xdb_ref[:, sl], bc)
        h_prev = hb_ref[...]
        y_off = ntdot(h_prev.astype(BF16), cc) * head_rows(sc_ref[1 * nh:2 * nh, sl])
        hb_ref[...] = h_prev * head_rows(sc_ref[3 * nh:4 * nh, sl]) + states
        y_ref[sl, :] = (yt_ref[:, sl] + y_off).T + dx_ref[...] * x_ref[sl, :]
        return carry

    lax.fori_loop(0, nc, bwd_body, 0, unroll=4)


def ssd_scan_bidir(xbc, dt, dt_bias_f, dt_bias_b, a_log_f, a_log_b, d_skip):
    bsz, s, _ = xbc.shape
    nheads = dt.shape[2] // 2
    nh = nheads // SSM_GROUPS
    gw = nh * SSM_HEAD_DIM
    d_inner = nheads * SSM_HEAD_DIM
    q = SSD_CHUNK
    dt_row = jnp.transpose(dt.reshape(bsz, s, 2 * SSM_GROUPS, nh), (0, 2, 3, 1))
    prm = jnp.stack([jnp.concatenate([dt_bias_f, dt_bias_b]), jnp.concatenate([a_log_f, a_log_b])])
    p_row = jnp.transpose(prm.reshape(2, 2 * SSM_GROUPS, nh), (1, 2, 0))
    dx = jnp.repeat(d_skip, SSM_HEAD_DIM).reshape(SSM_GROUPS, 1, gw)
    assert DECAY_SLOTS * nh <= q and q % nh == 0 and D_STATE == q
    trio = jnp.asarray(np.concatenate([np.triu(np.ones((q, q), np.float32)), np.ones((q, q), np.float32)],
                                      axis=1), BF16)
    bcol = d_inner // D_STATE
    G = SSM_GROUPS
    nc = s // q

    return pl.pallas_call(
        _ssd_t_kernel,
        grid=(bsz, SSM_GROUPS),
        in_specs=[pl.BlockSpec((None, s, gw), lambda b, g: (b, 0, g)),
                  pl.BlockSpec((None, s, D_STATE), lambda b, g: (b, 0, bcol + g)),
                  pl.BlockSpec((None, s, D_STATE), lambda b, g: (b, 0, bcol + G + g)),
                  pl.BlockSpec((None, None, nh, s), lambda b, g: (b, g, 0, 0)),
                  pl.BlockSpec((None, None, nh, s), lambda b, g: (b, G + g, 0, 0)),
                  pl.BlockSpec((None, nh, 2), lambda b, g: (g, 0, 0)),
                  pl.BlockSpec((None, nh, 2), lambda b, g: (G + g, 0, 0)),
                  pl.BlockSpec((None, 1, gw), lambda b, g: (g, 0, 0)),
                  pl.BlockSpec((q, 2 * q), lambda b, g: (0, 0))],
        out_specs=pl.BlockSpec((None, s, gw), lambda b, g: (b, 0, g)),
        out_shape=jax.ShapeDtypeStruct((bsz, s, d_inner), F32),
        scratch_shapes=[pltpu.VMEM((2 * nh, s), F32), pltpu.VMEM((2 * nh, s), F32),
                        pltpu.VMEM((4 * nh, s), F32),
                        pltpu.VMEM((s, q), BF16), pltpu.VMEM((nc, q, 2 * q), BF16),
                        pltpu.VMEM((gw, s), BF16), pltpu.VMEM((gw, s), BF16), pltpu.VMEM((gw, s), BF16),
                        pltpu.VMEM((gw, s), F32),
                        pltpu.VMEM((gw, D_STATE), F32), pltpu.VMEM((gw, D_STATE), F32)],
        compiler_params=_cparams("parallel", "parallel"),
        name="ssd_scan",
    )(xbc, xbc, xbc, dt_row, dt_row, p_row, p_row, dx, trio)


def _gated_proj_kernel(y_ref, z_ref, gw_ref, w_ref, x_ref, g_ref, o_ref):
    t = y_ref[...] * _silu(z_ref[...].astype(F32))
    ms = jnp.mean(t * t, axis=-1, keepdims=True)
    t = (t * lax.rsqrt(ms + EPS) * gw_ref[...]).astype(BF16)
    o_ref[...] = x_ref[...] + g_ref[...] * jnp.dot(t, w_ref[...], preferred_element_type=F32)


def gated_proj_residual(y, zproj, gw, w, x, g, *, ts):
    bsz, s, d = x.shape
    k = y.shape[2]
    return pl.pallas_call(
        _gated_proj_kernel,
        grid=(bsz, s // ts),
        in_specs=[pl.BlockSpec((None, ts, k), lambda b, i: (b, i, 0)),
                  pl.BlockSpec((None, ts, k), lambda b, i: (b, i, 0)),
                  pl.BlockSpec((1, k), lambda b, i: (0, 0)),
                  pl.BlockSpec((k, d), lambda b, i: (0, 0)),
                  pl.BlockSpec((None, ts, d), lambda b, i: (b, i, 0)),
                  pl.BlockSpec((None, 1, d), lambda b, i: (b, 0, 0))],
        out_specs=pl.BlockSpec((None, ts, d), lambda b, i: (b, i, 0)),
        out_shape=jax.ShapeDtypeStruct((bsz, s, d), F32),
        compiler_params=_cparams("parallel", "parallel"),
        name="ssd_out_proj",
    )(y, zproj, gw.reshape(1, k), w, x, g)


def _router_kernel(x_ref, nw_ref, sh_ref, sc_ref, rw_ref, rb_ref, lt_ref,
                   eidx_ref, rank_ref, wgt_ref, cnt_ref):
    h = _modnorm(x_ref[...], nw_ref[...], sh_ref[...], sc_ref[...])
    logits = jnp.dot(h, rw_ref[...], precision=HIGHEST, preferred_element_type=F32) + rb_ref[...]
    ts, ne = logits.shape
    eid = lax.broadcasted_iota(jnp.int32, (ts, ne), 1)
    m1 = jnp.max(logits, axis=-1, keepdims=True)
    i1 = jnp.min(jnp.where(logits == m1, eid, ne), axis=-1, keepdims=True)
    rest = jnp.where(eid == i1, -jnp.inf, logits)
    m2 = jnp.max(rest, axis=-1, keepdims=True)
    i2 = jnp.min(jnp.where(rest == m2, eid, ne), axis=-1, keepdims=True)
    e2 = jnp.exp(m2 - m1)
    w1 = 1.0 / (1.0 + e2)
    w2 = e2 / (1.0 + e2)
    oh1 = (eid == i1).astype(F32)
    oh2 = (eid == i2).astype(F32)
    chosen = oh1 + oh2
    incl = jnp.dot(lt_ref[...], chosen.astype(BF16), preferred_element_type=F32)
    before = incl - chosen
    r1 = jnp.sum(oh1 * before, axis=-1, keepdims=True)
    r2 = jnp.sum(oh2 * before, axis=-1, keepdims=True)
    eidx_ref[...] = jnp.concatenate([i1, i2], axis=1)
    rank_ref[...] = jnp.concatenate([r1, r2], axis=1).astype(jnp.int32)
    wgt_ref[...] = jnp.concatenate([w1, w2], axis=1)
    cnt_ref[...] = incl[ts - 1:ts, :].astype(jnp.int32)


def moe_route(x, nw, sh, sc, router_w, router_b, *, ts):
    bsz, s, d = x.shape
    ne = router_w.shape[1]
    nt = s // ts
    lt = jnp.asarray(np.tril(np.ones((ts, ts), np.float32)), BF16)
    tok = lambda dt: jax.ShapeDtypeStruct((bsz * s, TOP_K), dt)
    tok_spec = pl.BlockSpec((ts, TOP_K), lambda b, i: (b * nt + i, 0))
    return pl.pallas_call(
        _router_kernel,
        grid=(bsz, nt),
        in_specs=[pl.BlockSpec((None, ts, d), lambda b, i: (b, i, 0)),
                  pl.BlockSpec((1, d), lambda b, i: (0, 0)),
                  pl.BlockSpec((None, 1, d), lambda b, i: (b, 0, 0)),
                  pl.BlockSpec((None, 1, d), lambda b, i: (b, 0, 0)),
                  pl.BlockSpec((d, ne), lambda b, i: (0, 0)),
                  pl.BlockSpec((1, ne), lambda b, i: (0, 0)),
                  pl.BlockSpec((ts, ts), lambda b, i: (0, 0))],
        out_specs=[tok_spec, tok_spec, tok_spec,
                   pl.BlockSpec((None, 1, ne), lambda b, i: (b * nt + i, 0, 0))],
        out_shape=[tok(jnp.int32), tok(jnp.int32), tok(F32),
                   jax.ShapeDtypeStruct((bsz * nt, 1, ne), jnp.int32)],
        compiler_params=_cparams("parallel", "parallel"),
        name="moe_router",
    )(x, nw.reshape(1, d), sh, sc, router_w, router_b.reshape(1, ne), lt)


SEG_ROWS = 16
SEG_FIELDS = 3


def _segment_copies(seg_ref, tile, n_experts, make_copy, *, wait):
    for e in range(n_experts):
        base = (tile * n_experts + e) * SEG_FIELDS
        local0 = seg_ref[base]
        global0 = seg_ref[base + 1]

        def body(i, carry, local0=local0, global0=global0):
            cp = make_copy(pl.multiple_of(local0 + i * SEG_ROWS, SEG_ROWS),
                           pl.multiple_of(global0 + i * SEG_ROWS, SEG_ROWS))
            if wait:
                cp.wait()
            else:
                cp.start()
            return carry

        lax.fori_loop(0, seg_ref[base + 2], body, 0)


def _dispatch_kernel(seg_ref, x_ref, nw_ref, sh_ref, sc_ref, ld_ref, hs_in_ref, hs_ref, buf_ref, sem):
    del hs_in_ref
    tt = x_ref.shape[0]
    lc = buf_ref.shape[1]
    ne = N_EXPERTS
    tile = pl.program_id(0) * pl.num_programs(1) + pl.program_id(1)
    last = pl.num_programs(0) * pl.num_programs(1) - 1
    slot = tile % 2

    def copies(t, sl, wait):
        def make_copy(lo, go):
            return pltpu.make_async_copy(buf_ref.at[sl, pl.ds(lo, SEG_ROWS), :],
                                         hs_ref.at[pl.ds(go, SEG_ROWS), :], sem.at[sl])
        _segment_copies(seg_ref, t, ne, make_copy, wait=wait)

    h = _modnorm(x_ref[...], nw_ref[...], sh_ref[...], sc_ref[...]).astype(BF16)
    ld = ld_ref[...]
    rows = lax.broadcasted_iota(jnp.int32, (lc, tt), 0)
    perm = jnp.where(rows == ld[0:1, :], 1.0, jnp.where(rows == ld[1:2, :], 1.0, 0.0)).astype(BF16)
    buf_ref[slot] = jnp.dot(perm, h, preferred_element_type=F32).astype(BF16)
    copies(tile, slot, wait=False)

    @pl.when(tile > 0)
    def _():
        copies(tile - 1, 1 - slot, wait=True)

    @pl.when(tile == last)
    def _():
        copies(tile, slot, wait=True)


def moe_dispatch(x, nw, sh, sc, seg, ldest_rows, n_rows, *, tt, lc):
    bsz, s, d = x.shape
    nt = s // tt
    hs0 = jnp.zeros((n_rows, d), BF16)
    grid_spec = pltpu.PrefetchScalarGridSpec(
        num_scalar_prefetch=1,
        grid=(bsz, nt),
        in_specs=[pl.BlockSpec((None, tt, d), lambda b, i, sref: (b, i, 0)),
                  pl.BlockSpec((1, d), lambda b, i, sref: (0, 0)),
                  pl.BlockSpec((None, 1, d), lambda b, i, sref: (b, 0, 0)),
                  pl.BlockSpec((None, 1, d), lambda b, i, sref: (b, 0, 0)),
                  pl.BlockSpec((TOP_K, tt), lambda b, i, sref: (0, b * nt + i)),
                  pl.BlockSpec(memory_space=pl.ANY)],
        out_specs=pl.BlockSpec(memory_space=pl.ANY),
        scratch_shapes=[pltpu.VMEM((2, lc, d), BF16), pltpu.SemaphoreType.DMA((2,))],
    )
    return pl.pallas_call(
        _dispatch_kernel,
        grid_spec=grid_spec,
        out_shape=jax.ShapeDtypeStruct((n_rows, d), BF16),
        input_output_aliases={6: 0},
        compiler_params=_cparams("arbitrary", "arbitrary"),
        name="moe_dispatch",
    )(seg, x, nw.reshape(1, d), sh, sc, ldest_rows, hs0)


def _moe_kernel(te_ref, nu_ref, hs_ref, w1_ref, w3_ref, w2_ref, o_ref, acc_ref):
    i = pl.program_id(0)
    f = pl.program_id(1)

    @pl.when(i < nu_ref[0])
    def _():
        h = hs_ref[...]
        a = jnp.dot(h, w1_ref[...], preferred_element_type=F32)
        b = jnp.dot(h, w3_ref[...], preferred_element_type=F32)
        t = (_silu(a) * b).astype(BF16)
        contrib = jnp.dot(t, w2_ref[...], preferred_element_type=F32)

        @pl.when(f == 0)
        def _():
            acc_ref[...] = contrib

        @pl.when(f > 0)
        def _():
            acc_ref[...] += contrib

        @pl.when(f == pl.num_programs(1) - 1)
        def _():
            o_ref[...] = acc_ref[...].astype(o_ref.dtype)

    @pl.when((i >= nu_ref[0]) & (f == 0))
    def _():
        o_ref[...] = jnp.zeros_like(o_ref)


def moe_experts(hs, tile_expert, n_used, w1, w3, w2, *, tm, tf):
    n_rows, d = hs.shape
    dff = w1.shape[2]
    nf = dff // tf
    n_tiles = n_rows // tm

    def last_used(i, nu):
        return jnp.maximum(jnp.minimum(i, nu[0] - 1), 0)

    def row_map(i, f, te, nu):
        return (last_used(i, nu), 0)

    def hidden_block(i, f, nu):
        t = last_used(i, nu)
        step = jnp.where(i < nu[0], f, nf - 1)
        return jnp.where(t % 2 == 0, step, nf - 1 - step)

    def w_in_map(i, f, te, nu):
        return (te[last_used(i, nu)], 0, hidden_block(i, f, nu))

    def w_out_map(i, f, te, nu):
        return (te[last_used(i, nu)], hidden_block(i, f, nu), 0)

    grid_spec = pltpu.PrefetchScalarGridSpec(
        num_scalar_prefetch=2,
        grid=(n_tiles, nf),
        in_specs=[pl.BlockSpec((tm, d), row_map),
                  pl.BlockSpec((None, d, tf), w_in_map),
                  pl.BlockSpec((None, d, tf), w_in_map),
                  pl.BlockSpec((None, tf, d), w_out_map)],
        out_specs=pl.BlockSpec((tm, d), lambda i, f, te, nu: (i, 0)),
        scratch_shapes=[pltpu.VMEM((tm, d), F32)],
    )
    return pl.pallas_call(
        _moe_kernel,
        grid_spec=grid_spec,
        out_shape=jax.ShapeDtypeStruct((n_rows, d), BF16),
        compiler_params=_cparams("arbitrary", "arbitrary"),
        name="moe_experts",
    )(tile_expert, n_used, hs, w1, w3, w2)


def _combine_kernel(seg_ref, ys_ref, x_ref, g_ref, wgt_ref, ld_ref, o_ref, buf_ref, sem):
    tt = x_ref.shape[0]
    lc = buf_ref.shape[1]
    ne = N_EXPERTS
    tile = pl.program_id(0) * pl.num_programs(1) + pl.program_id(1)
    last = pl.num_programs(0) * pl.num_programs(1) - 1
    slot = tile % 2

    def copies(t, sl, wait):
        def make_copy(lo, go):
            return pltpu.make_async_copy(ys_ref.at[pl.ds(go, SEG_ROWS), :],
                                         buf_ref.at[sl, pl.ds(lo, SEG_ROWS), :], sem.at[sl])
        _segment_copies(seg_ref, t, ne, make_copy, wait=wait)

    @pl.when(tile == 0)
    def _():
        buf_ref[...] = jnp.zeros_like(buf_ref)
        copies(tile, slot, wait=False)

    @pl.when(tile < last)
    def _():
        copies(tile + 1, 1 - slot, wait=False)

    copies(tile, slot, wait=True)

    ld = ld_ref[...]
    cols = lax.broadcasted_iota(jnp.int32, (tt, lc), 1)
    pick = jnp.concatenate([jnp.where(cols == ld[:, k:k + 1], 1.0, 0.0) for k in range(TOP_K)],
                           axis=0).astype(BF16)
    z = jnp.dot(pick, buf_ref[slot], preferred_element_type=F32)
    w = wgt_ref[...]
    mix = w[:, 0:1] * z[0:tt] + w[:, 1:2] * z[tt:]
    o_ref[...] = x_ref[...] + g_ref[...] * mix


def moe_combine(ys, seg, ldest, wgt, x, g, *, tt, lc):
    bsz, s, d = x.shape
    nt = s // tt
    tok_spec = pl.BlockSpec((tt, TOP_K), lambda b, i, sref: (b * nt + i, 0))
    grid_spec = pltpu.PrefetchScalarGridSpec(
        num_scalar_prefetch=1,
        grid=(bsz, nt),
        in_specs=[pl.BlockSpec(memory_space=pl.ANY),
                  pl.BlockSpec((None, tt, d), lambda b, i, sref: (b, i, 0)),
                  pl.BlockSpec((None, 1, d), lambda b, i, sref: (b, 0, 0)),
                  tok_spec, tok_spec],
        out_specs=pl.BlockSpec((None, tt, d), lambda b, i, sref: (b, i, 0)),
        scratch_shapes=[pltpu.VMEM((2, lc, d), BF16), pltpu.SemaphoreType.DMA((2,))],
    )
    return pl.pallas_call(
        _combine_kernel,
        grid_spec=grid_spec,
        out_shape=jax.ShapeDtypeStruct((bsz, s, d), F32),
        compiler_params=_cparams("arbitrary", "arbitrary"),
        name="moe_combine",
    )(seg, ys, x, g, wgt, ldest)


def _round_up(v, m):
    return ((v + m - 1) // m) * m


def moe_residual(x, nw, sh, sc, g, router_w, router_b, w1, w3, w2, *, tm=MOE_TILE_ROWS):
    bsz, s, d = x.shape
    n_tok = bsz * s
    ne = router_w.shape[1]
    tt = min(512, s)
    n_tt = n_tok // tt
    lc = _round_up(TOP_K * tt + ne * SEG_ROWS, 128)
    eidx, rank, wgt, cnt = moe_route(x, nw, sh, sc, router_w, router_b, ts=tt)
    seg_len = _round_up(cnt.reshape(n_tt, ne), SEG_ROWS)
    local_start = jnp.cumsum(seg_len, axis=1) - seg_len
    padded = _round_up(jnp.sum(seg_len, axis=0), tm)
    ends = jnp.cumsum(padded)
    global_start = (ends - padded)[None, :] + jnp.cumsum(seg_len, axis=0) - seg_len
    seg = jnp.stack([local_start, global_start, seg_len // SEG_ROWS], axis=-1).reshape(-1).astype(jnp.int32)
    onehot = eidx[:, :, None] == jnp.arange(ne, dtype=jnp.int32)
    start_tok = jnp.repeat(local_start, tt, axis=0)[:, None, :]
    ldest = (jnp.sum(jnp.where(onehot, start_tok, 0), axis=-1) + rank).astype(jnp.int32)
    n_rows = _round_up(n_tok * TOP_K + n_tt * ne * SEG_ROWS + ne * tm, tm)
    n_tiles = n_rows // tm
    tile_start = jnp.arange(n_tiles, dtype=jnp.int32) * tm
    tile_expert = jnp.minimum(jnp.sum(tile_start[:, None] >= ends[None, :], axis=1), ne - 1).astype(jnp.int32)
    n_used = (ends[ne - 1:ne] // tm).astype(jnp.int32)
    hs = moe_dispatch(x, nw, sh, sc, seg, ldest.T, n_rows, tt=tt, lc=lc)
    ys = moe_experts(hs, tile_expert, n_used, w1, w3, w2, tm=tm, tf=w1.shape[2] // 2)
    return moe_combine(ys, seg, ldest, wgt, x, g, tt=tt, lc=lc)


def _split_mod(mod):
    return [m[:, None, :] for m in jnp.split(mod, 6, axis=-1)]


def even_layer(x, c, rel_bias, ada_w, ada_b, norm1_w, in_w, q_norm_w, k_norm_w, sink, out_w,
               norm2_w, w1, w3, w2):
    s = x.shape[1]
    sh1, sc1, g1, sh2, sc2, g2 = _split_mod(ada_mod(c, ada_w, ada_b))
    proj = norm_mod_matmul(x, norm1_w, sh1, sc1, in_w.astype(BF16), ts=min(1024, s), tn=256, name="even_in_proj",
                           out_dtype=BF16)
    yf = fourier_mix(proj, tq=min(512, s))
    ya = window_attention(proj, band_bias(rel_bias), q_norm_w, k_norm_w, sink)
    x = cat_proj_residual(yf, ya, out_w.astype(BF16), x, g1, ts=min(1024, s))
    dff = w1.shape[1]
    return ffn_residual(x, norm2_w, sh2, sc2, g2, w1.astype(BF16), w3.astype(BF16), w2.astype(BF16),
                        ts=min(1024, s), tf=256)


def odd_layer(x, c, ada_w, ada_b, norm1_w, in_w, conv_w, conv_b, dt_bias_f, dt_bias_b, a_log_f, a_log_b,
              d_skip, gnorm_w, out_w, norm2_w, router_w, router_b, w1, w3, w2):
    s = x.shape[1]
    sh1, sc1, g1, sh2, sc2, g2 = _split_mod(ada_mod(c, ada_w, ada_b))
    d_inner = gnorm_w.shape[0]
    cdim = conv_w.shape[1]
    wide = d_inner + cdim
    in_w = in_w.astype(BF16)
    zx = norm_mod_matmul(x, norm1_w, sh1, sc1, in_w[:, :wide], ts=min(1024, s), tn=1024, name="odd_in_proj",
                         out_dtype=BF16)
    dt = norm_mod_matmul(x, norm1_w, sh1, sc1, in_w[:, wide:], ts=min(1024, s), tn=in_w.shape[1] - wide,
                         name="odd_dt_proj")
    xbc = conv_silu(zx, conv_w, conv_b, col0=d_inner, tc=512)
    y = ssd_scan_bidir(xbc, dt, dt_bias_f, dt_bias_b, a_log_f, a_log_b, d_skip)
    x = gated_proj_residual(y, zx, gnorm_w, out_w.astype(BF16), x, g1, ts=min(512, s))
    return moe_residual(x, norm2_w, sh2, sc2, g2, router_w, router_b,
                        w1.astype(BF16), w3.astype(BF16), w2.astype(BF16))


def kernel(x, c, rel_bias, ev_ada_w, ev_ada_b, ev_norm1_w, ev_in_w, ev_q_norm_w, ev_k_norm_w, ev_sink, ev_out_w, ev_norm2_w, ev_ffn_w1, ev_ffn_w3, ev_ffn_w2, od_ada_w, od_ada_b, od_norm1_w, od_in_w, od_conv_w, od_conv_b, od_dt_bias_f, od_dt_bias_b, od_A_log_f, od_A_log_b, od_D, od_gnorm_w, od_out_w, od_norm2_w, od_router_w, od_router_b, od_moe_w1, od_moe_w3, od_moe_w2):
    depth = ev_ada_w.shape[0] + od_ada_w.shape[0]
    for i in range(depth):
        j = i // 2
        if i % 2 == 0:
            x = even_layer(x, c, rel_bias, ev_ada_w[j], ev_ada_b[j], ev_norm1_w[j], ev_in_w[j],
                           ev_q_norm_w[j], ev_k_norm_w[j], ev_sink[j], ev_out_w[j], ev_norm2_w[j],
                           ev_ffn_w1[j], ev_ffn_w3[j], ev_ffn_w2[j])
        else:
            x = odd_layer(x, c, od_ada_w[j], od_ada_b[j], od_norm1_w[j], od_in_w[j], od_conv_w[j],
                          od_conv_b[j], od_dt_bias_f[j], od_dt_bias_b[j], od_A_log_f[j], od_A_log_b[j],
                          od_D[j], od_gnorm_w[j], od_out_w[j], od_norm2_w[j], od_router_w[j],
                          od_router_b[j], od_moe_w1[j], od_moe_w3[j], od_moe_w2[j])
    return x
```

```python
import functools

import numpy as np
import jax
import jax.numpy as jnp
from jax import lax
from jax.experimental import pallas as pl
from jax.experimental.pallas import tpu as pltpu

F32 = jnp.float32
BF16 = jnp.bfloat16
HIGHEST = lax.Precision.HIGHEST

EPS = 1e-6
FNET_GROUPS = 4
FNET_GROUP_DIM = 128
FNET_WIDTH = FNET_GROUPS * FNET_GROUP_DIM
ATTN_HEADS = 8
ATTN_KV_HEADS = 2
HEAD_DIM = 64
ATTN_WIDTH = ATTN_HEADS * HEAD_DIM
KV_WIDTH = ATTN_KV_HEADS * HEAD_DIM
WINDOW = 128
BLOCK = 128
REL_BUCKETS = 32
REL_MAX_DIST = 128
SSM_HEAD_DIM = 64
SSM_GROUPS = 4
D_STATE = 128
CONV_WIDTH = 5
SSD_CHUNK = 128
N_EXPERTS = 8
TOP_K = 2
NEG_BIG = -1e30
ATTN_QBLOCKS = 2

V7X_VMEM_LIMIT_BYTES = 56 * 1024 * 1024
MOE_TILE_ROWS = 512
MOE_HIDDEN_CHUNK = 256


def _cparams(*sem):
    return pltpu.CompilerParams(dimension_semantics=sem, vmem_limit_bytes=V7X_VMEM_LIMIT_BYTES)


def _modnorm(x, nw, sh, sc):
    ms = jnp.mean(x * x, axis=-1, keepdims=True)
    return x * lax.rsqrt(ms + EPS) * nw * (1.0 + sc) + sh


def _silu(x):
    return x * (1.0 / (1.0 + jnp.exp(-x)))


def _ada_kernel(c_ref, w_ref, b_ref, o_ref):
    cs = _silu(c_ref[...]).astype(BF16)
    o_ref[...] = jnp.dot(cs, w_ref[...].astype(BF16), preferred_element_type=F32) + b_ref[...]


def ada_mod(c, w, b):
    bsz, d = c.shape
    n = w.shape[1]
    tn = 1536
    return pl.pallas_call(
        _ada_kernel,
        grid=(n // tn,),
        in_specs=[pl.BlockSpec((bsz, d), lambda j: (0, 0)),
                  pl.BlockSpec((d, tn), lambda j: (0, j)),
                  pl.BlockSpec((1, tn), lambda j: (0, j))],
        out_specs=pl.BlockSpec((bsz, tn), lambda j: (0, j)),
        out_shape=jax.ShapeDtypeStruct((bsz, n), F32),
        compiler_params=_cparams("arbitrary"),
        name="ada_mod",
    )(c, w, b.reshape(1, n))


def _nmm_kernel(x_ref, nw_ref, sh_ref, sc_ref, w_ref, o_ref, *, tn):
    h = _modnorm(x_ref[...], nw_ref[...], sh_ref[...], sc_ref[...]).astype(BF16)
    n = w_ref.shape[1]
    for lo in range(0, n, tn):
        o_ref[:, lo:lo + tn] = jnp.dot(h, w_ref[:, lo:lo + tn],
                                       preferred_element_type=F32).astype(o_ref.dtype)


def _resident(shape):
    return pl.BlockSpec(shape, lambda *_: tuple(0 for _ in shape), pipeline_mode=pl.Buffered(1))


def norm_mod_matmul(x, nw, sh, sc, w, *, ts, tn, name, out_dtype=F32):
    bsz, s, d = x.shape
    n = w.shape[1]
    assert n % tn == 0
    return pl.pallas_call(
        functools.partial(_nmm_kernel, tn=tn),
        grid=(bsz, s // ts),
        in_specs=[pl.BlockSpec((None, ts, d), lambda b, i: (b, i, 0)),
                  pl.BlockSpec((1, d), lambda b, i: (0, 0)),
                  pl.BlockSpec((None, 1, d), lambda b, i: (b, 0, 0)),
                  pl.BlockSpec((None, 1, d), lambda b, i: (b, 0, 0)),
                  _resident((d, n))],
        out_specs=pl.BlockSpec((None, ts, n), lambda b, i: (b, i, 0)),
        out_shape=jax.ShapeDtypeStruct((bsz, s, n), out_dtype),
        compiler_params=_cparams("parallel", "parallel"),
        name=name,
    )(x, nw.reshape(1, d), sh, sc, w)


def _dft_cos_sin(n):
    k = np.arange(n, dtype=np.int64)
    ang = ((k[:, None] * k[None, :]) % n).astype(np.float64) * (2.0 * np.pi / n)
    scale = 1.0 / np.sqrt(n)
    return np.cos(ang) * scale, np.sin(ang) * scale


def _fourier_kernel(u_ref, chan_ref, seq_ref, o_ref, ab_ref):
    s = u_ref.shape[0]

    @pl.when(pl.program_id(1) == 0)
    def _():
        for g in range(FNET_GROUPS):
            lo, hi = g * FNET_GROUP_DIM, (g + 1) * FNET_GROUP_DIM
            ug = u_ref[:, lo:hi].astype(BF16)
            cs = jnp.dot(ug, chan_ref[...], preferred_element_type=F32)
            ab_ref[0:s, lo:hi] = cs[:, :FNET_GROUP_DIM].astype(BF16)
            ab_ref[s:2 * s, lo:hi] = cs[:, FNET_GROUP_DIM:].astype(BF16)

    o_ref[...] = jnp.dot(seq_ref[...], ab_ref[...], preferred_element_type=F32).astype(o_ref.dtype)


def fourier_mix(proj, *, tq):
    bsz, s, _ = proj.shape
    cc, sc = _dft_cos_sin(FNET_GROUP_DIM)
    chan = jnp.asarray(np.concatenate([cc, sc], axis=1), BF16)
    cs, ss = _dft_cos_sin(s)
    seq = jnp.asarray(np.concatenate([cs, -ss], axis=1), BF16)
    return pl.pallas_call(
        _fourier_kernel,
        grid=(bsz, s // tq),
        in_specs=[pl.BlockSpec((None, s, FNET_WIDTH), lambda b, i: (b, 0, 0)),
                  pl.BlockSpec((FNET_GROUP_DIM, 2 * FNET_GROUP_DIM), lambda b, i: (0, 0)),
                  pl.BlockSpec((tq, 2 * s), lambda b, i: (i, 0))],
        out_specs=pl.BlockSpec((None, tq, FNET_WIDTH), lambda b, i: (b, i, 0)),
        out_shape=jax.ShapeDtypeStruct((bsz, s, FNET_WIDTH), BF16),
        scratch_shapes=[pltpu.VMEM((2 * s, FNET_WIDTH), BF16)],
        compiler_params=_cparams("parallel", "arbitrary"),
        name="fourier_mix",
    )(proj, chan, seq)


def _band_bucket_table():
    i = np.arange(BLOCK)[:, None]
    j = np.arange(3 * BLOCK)[None, :]
    rel = (j - BLOCK) - i
    half = REL_BUCKETS // 2
    max_exact = half // 2
    n = np.abs(rel)
    large = max_exact + (np.log(np.maximum(n, 1) / max_exact)
                         / np.log(REL_MAX_DIST / max_exact) * (half - max_exact)).astype(np.int32)
    large = np.minimum(large, half - 1)
    bucket = (rel > 0).astype(np.int32) * half + np.where(n < max_exact, n, large)
    return np.where(n <= WINDOW, bucket, -1).astype(np.int32)


def _bias_kernel(rb_ref, bucket_ref, o_ref):
    h = pl.program_id(0)
    bucket = bucket_ref[...]
    acc = jnp.full(bucket.shape, NEG_BIG, F32)
    for bkt in range(REL_BUCKETS):
        acc = jnp.where(bucket == bkt, rb_ref[bkt * ATTN_HEADS + h], acc)
    o_ref[...] = acc


def band_bias(rel_bias):
    bucket = jnp.asarray(_band_bucket_table())
    return pl.pallas_call(
        _bias_kernel,
        grid=(ATTN_HEADS,),
        in_specs=[pl.BlockSpec(memory_space=pltpu.SMEM),
                  pl.BlockSpec((BLOCK, 3 * BLOCK), lambda h: (0, 0))],
        out_specs=pl.BlockSpec((None, BLOCK, 3 * BLOCK), lambda h: (h, 0, 0)),
        out_shape=jax.ShapeDtypeStruct((ATTN_HEADS, BLOCK, 3 * BLOCK), F32),
        compiler_params=_cparams("arbitrary"),
        name="band_bias",
    )(rel_bias.reshape(-1), bucket)


def _head_mean_matrix(width):
    m = np.zeros((width, width), np.float32)
    for h in range(width // HEAD_DIM):
        m[h * HEAD_DIM:(h + 1) * HEAD_DIM, h * HEAD_DIM:(h + 1) * HEAD_DIM] = 1.0 / HEAD_DIM
    return m


def _heads_rms(t, mean_mat, w):
    sq = t * t
    hi = sq.astype(BF16)
    lo = (sq - hi.astype(F32)).astype(BF16)
    ms = (jnp.dot(hi, mean_mat, preferred_element_type=F32)
          + jnp.dot(lo, mean_mat, preferred_element_type=F32))
    return t * lax.rsqrt(ms + EPS) * w


def _attn_kernel(sink_ref, q_ref, kl_ref, kc_ref, kr_ref, vl_ref, vc_ref, vr_ref,
                 bias_ref, qnw_ref, knw_ref, qmean_ref, kmean_ref, o_ref):
    n = pl.program_id(1)
    nb = pl.num_programs(1) * ATTN_QBLOCKS
    k = jnp.concatenate([kl_ref[...], kc_ref[...], kr_ref[...]], axis=0).astype(F32)
    v = jnp.concatenate([vl_ref[...], vc_ref[...], vr_ref[...]], axis=0).astype(F32)
    col = lax.broadcasted_iota(jnp.int32, (1, 3 * BLOCK), 1)
    qn = _heads_rms(q_ref[...].astype(F32), qmean_ref[...], qnw_ref[...])
    kn = _heads_rms(k, kmean_ref[...], knw_ref[...])
    low = lax.broadcasted_iota(jnp.int32, (1, 2 * HEAD_DIM), 1) < HEAD_DIM
    kn_sw = pltpu.roll(kn, HEAD_DIM, axis=1)
    v_sw = pltpu.roll(v, HEAD_DIM, axis=1)
    k_dup = [jnp.where(low, kn, kn_sw).astype(BF16), jnp.where(low, kn_sw, kn).astype(BF16)]
    ones = jnp.ones((k.shape[0], 2 * HEAD_DIM), BF16)
    v_ext = [jnp.concatenate([v.astype(BF16), ones], axis=1),
             jnp.concatenate([v_sw.astype(BF16), ones], axis=1)]
    g = ATTN_HEADS // ATTN_KV_HEADS
    for qb in range(ATTN_QBLOCKS):
        blk = n * ATTN_QBLOCKS + qb
        band = slice(qb * BLOCK, (qb + 3) * BLOCK)
        first_key = jnp.where(blk == 0, BLOCK, 0)
        end_key = jnp.where(blk == nb - 1, 2 * BLOCK, 3 * BLOCK)
        outside = (col < first_key) | (col >= end_key)
        pairs = []
        for m in range(ATTN_HEADS // 2):
            j = (2 * m) // g
            qp = qn[qb * BLOCK:(qb + 1) * BLOCK, m * 2 * HEAD_DIM:(m + 1) * 2 * HEAD_DIM]
            res = []
            for idx in range(2):
                h = 2 * m + idx
                qm = jnp.where(low if idx == 0 else jnp.logical_not(low), qp, 0.0).astype(BF16)
                logits = lax.dot_general(qm, k_dup[j][band], (((1,), (1,)), ((), ())),
                                         preferred_element_type=F32)
                logits = jnp.where(outside, NEG_BIG, logits + bias_ref[h])
                sk = sink_ref[h]
                mx = jnp.maximum(jnp.max(logits, axis=-1, keepdims=True), sk)
                p = jnp.exp(logits - mx).astype(BF16)
                r = jnp.dot(p, v_ext[idx if j == 0 else 1 - idx][band], preferred_element_type=F32)
                denom = r[:, 2 * HEAD_DIM:] + jnp.exp(sk - mx)
                res.append(r[:, :2 * HEAD_DIM] / denom)
            pairs.append(jnp.where(low, res[0], res[1]))
        o_ref[qb * BLOCK:(qb + 1) * BLOCK, :] = jnp.concatenate(pairs, axis=-1).astype(o_ref.dtype)


def window_attention(proj, bias, q_norm_w, k_norm_w, sink):
    bsz, s, _ = proj.shape
    nb = s // BLOCK
    qcol = FNET_WIDTH // ATTN_WIDTH
    kcol = (FNET_WIDTH + ATTN_WIDTH) // KV_WIDTH
    vcol = kcol + 1

    qb = ATTN_QBLOCKS
    assert nb % qb == 0

    def kv_specs(col):
        return [pl.BlockSpec((None, BLOCK, KV_WIDTH), lambda b, n: (b, jnp.maximum(n * qb - 1, 0), col)),
                pl.BlockSpec((None, qb * BLOCK, KV_WIDTH), lambda b, n: (b, n, col)),
                pl.BlockSpec((None, BLOCK, KV_WIDTH), lambda b, n: (b, jnp.minimum((n + 1) * qb, nb - 1), col))]

    return pl.pallas_call(
        _attn_kernel,
        grid=(bsz, nb // qb),
        in_specs=[pl.BlockSpec(memory_space=pltpu.SMEM),
                  pl.BlockSpec((None, qb * BLOCK, ATTN_WIDTH), lambda b, n: (b, n, qcol)),
                  *kv_specs(kcol), *kv_specs(vcol),
                  pl.BlockSpec((ATTN_HEADS, BLOCK, 3 * BLOCK), lambda b, n: (0, 0, 0)),
                  pl.BlockSpec((1, ATTN_WIDTH), lambda b, n: (0, 0)),
                  pl.BlockSpec((1, KV_WIDTH), lambda b, n: (0, 0)),
                  pl.BlockSpec((ATTN_WIDTH, ATTN_WIDTH), lambda b, n: (0, 0)),
                  pl.BlockSpec((KV_WIDTH, KV_WIDTH), lambda b, n: (0, 0))],
        out_specs=pl.BlockSpec((None, qb * BLOCK, ATTN_WIDTH), lambda b, n: (b, n, 0)),
        out_shape=jax.ShapeDtypeStruct((bsz, s, ATTN_WIDTH), BF16),
        compiler_params=_cparams("parallel", "arbitrary"),
        name="window_attention",
    )(sink, proj, proj, proj, proj, proj, proj, proj, bias,
      (jnp.tile(q_norm_w, ATTN_HEADS) * (HEAD_DIM ** -0.5)).reshape(1, ATTN_WIDTH),
      jnp.tile(k_norm_w, ATTN_KV_HEADS).reshape(1, KV_WIDTH),
      jnp.asarray(_head_mean_matrix(ATTN_WIDTH), BF16), jnp.asarray(_head_mean_matrix(KV_WIDTH), BF16))


def _cat_proj_kernel(a1_ref, a2_ref, w_ref, x_ref, g_ref, o_ref):
    k1 = a1_ref.shape[1]
    y = jnp.dot(a1_ref[...].astype(BF16), w_ref[0:k1, :], preferred_element_type=F32)
    y = y + jnp.dot(a2_ref[...].astype(BF16), w_ref[k1:, :], preferred_element_type=F32)
    o_ref[...] = x_ref[...] + g_ref[...] * y


def cat_proj_residual(a1, a2, w, x, g, *, ts):
    bsz, s, d = x.shape
    k1, k2 = a1.shape[2], a2.shape[2]
    return pl.pallas_call(
        _cat_proj_kernel,
        grid=(bsz, s // ts),
        in_specs=[pl.BlockSpec((None, ts, k1), lambda b, i: (b, i, 0)),
                  pl.BlockSpec((None, ts, k2), lambda b, i: (b, i, 0)),
                  pl.BlockSpec((k1 + k2, d), lambda b, i: (0, 0)),
                  pl.BlockSpec((None, ts, d), lambda b, i: (b, i, 0)),
                  pl.BlockSpec((None, 1, d), lambda b, i: (b, 0, 0))],
        out_specs=pl.BlockSpec((None, ts, d), lambda b, i: (b, i, 0)),
        out_shape=jax.ShapeDtypeStruct((bsz, s, d), F32),
        compiler_params=_cparams("parallel", "parallel"),
        name="mixer_out_proj",
    )(a1, a2, w, x, g)


def _ffn_kernel(x_ref, nw_ref, sh_ref, sc_ref, g_ref, w1_ref, w3_ref, w2_ref, o_ref, acc_ref, *, tf):
    h = _modnorm(x_ref[...], nw_ref[...], sh_ref[...], sc_ref[...]).astype(BF16)
    dff = w1_ref.shape[1]
    for lo in range(0, dff, tf):
        a = jnp.dot(h, w1_ref[:, lo:lo + tf], preferred_element_type=F32)
        b = jnp.dot(h, w3_ref[:, lo:lo + tf], preferred_element_type=F32)
        t = (_silu(a) * b).astype(BF16)
        contrib = jnp.dot(t, w2_ref[lo:lo + tf, :], preferred_element_type=F32)
        if lo == 0:
            acc_ref[...] = contrib
        else:
            acc_ref[...] += contrib
    o_ref[...] = x_ref[...] + g_ref[...] * acc_ref[...]


def ffn_residual(x, nw, sh, sc, g, w1, w3, w2, *, ts, tf):
    bsz, s, d = x.shape
    dff = w1.shape[1]
    assert dff % tf == 0
    vec = pl.BlockSpec((None, 1, d), lambda b, i: (b, 0, 0))
    return pl.pallas_call(
        functools.partial(_ffn_kernel, tf=tf),
        grid=(bsz, s // ts),
        in_specs=[pl.BlockSpec((None, ts, d), lambda b, i: (b, i, 0)),
                  pl.BlockSpec((1, d), lambda b, i: (0, 0)),
                  vec, vec, vec,
                  _resident((d, dff)), _resident((d, dff)), _resident((dff, d))],
        out_specs=pl.BlockSpec((None, ts, d), lambda b, i: (b, i, 0)),
        out_shape=jax.ShapeDtypeStruct((bsz, s, d), F32),
        scratch_shapes=[pltpu.VMEM((ts, d), F32)],
        compiler_params=_cparams("parallel", "parallel"),
        name="ffn_swiglu",
    )(x, nw.reshape(1, d), sh, sc, g, w1, w3, w2)


CONV_PAD = 8
CONV_ROWS = 256


def _conv_kernel(x_ref, w_ref, b_ref, o_ref, pad_ref):
    s, tc = x_ref.shape
    zeros = jnp.zeros((CONV_PAD, tc), F32)
    pad_ref[0:CONV_PAD, :] = zeros
    pad_ref[CONV_PAD + s:, :] = zeros
    pad_ref[CONV_PAD:CONV_PAD + s, :] = x_ref[...].astype(F32)
    half = CONV_WIDTH // 2
    for r in range(s // CONV_ROWS):
        base = CONV_PAD + r * CONV_ROWS - half
        acc = jnp.zeros((CONV_ROWS, tc), F32) + b_ref[...]
        for kk in range(CONV_WIDTH):
            acc = acc + pad_ref[base + kk:base + kk + CONV_ROWS, :] * w_ref[kk:kk + 1, :]
        o_ref[r * CONV_ROWS:(r + 1) * CONV_ROWS, :] = _silu(acc)


def conv_silu(proj, conv_w, conv_b, *, col0, tc):
    bsz, s, _ = proj.shape
    cdim = conv_w.shape[1]
    cb0 = col0 // tc
    return pl.pallas_call(
        _conv_kernel,
        grid=(bsz, cdim // tc),
        in_specs=[pl.BlockSpec((None, s, tc), lambda b, j: (b, 0, cb0 + j)),
                  pl.BlockSpec((CONV_WIDTH, tc), lambda b, j: (0, j)),
                  pl.BlockSpec((1, tc), lambda b, j: (0, j))],
        out_specs=pl.BlockSpec((None, s, tc), lambda b, j: (b, 0, j)),
        out_shape=jax.ShapeDtypeStruct((bsz, s, cdim), F32),
        scratch_shapes=[pltpu.VMEM((s + 2 * CONV_PAD, tc), F32)],
        compiler_params=_cparams("parallel", "parallel"),
        name="conv_silu",
    )(proj, conv_w, conv_b.reshape(1, cdim))


def _softplus(x):
    return jnp.maximum(x, 0.0) + jnp.log(1.0 + jnp.exp(-jnp.abs(x)))


def _expand_matrix(n_in, width):
    e = np.zeros((n_in, n_in * width), np.float32)
    for h in range(n_in):
        e[h, h * width:(h + 1) * width] = 1.0
    return e


LOG2E = 1.4426950408889634
DECAY_SLOTS = 12


def _decay_placement(nh):
    place = np.zeros((3, 2 * nh, 3 * 128), np.float32)
    const = np.zeros((1, 3 * 128), np.float32)
    for h in range(nh):
        for part in range(3):
            place[part, h, part * nh + h] = 1.0
            place[part, nh + h, (6 + part) * nh + h] = -1.0
            place[part, h, 128 + (3 + part) * nh + h] = -1.0
            place[part, nh + h, 256 + (9 + part) * nh + h] = 1.0
            const[0, (3 + part) * nh + h] = 1.0
            const[0, (9 + part) * nh + h] = 1.0
            const[0, 128 + part * nh + h] = 1.0
            const[0, 256 + (6 + part) * nh + h] = 1.0
    return place, const


def _split3(v):
    hi = v.astype(BF16)
    r = v - hi.astype(F32)
    mid = r.astype(BF16)
    lo = (r - mid.astype(F32)).astype(BF16)
    return hi, mid, lo


def _ssd_kernel(x_ref, b_ref, c_ref, dtc_f_ref, dtc_b_ref, dtr_f_ref, dtr_b_ref,
                pc_f_ref, pc_b_ref, pr_f_ref, pr_b_ref, dx_ref,
                tri_ref, place_ref, pconst_ref, e64_ref, y_ref,
                ac_ref, dc_ref, dr_ref, pq_ref, xdf_ref, xdb_ref, eif_ref, er_ref, decf_ref, decb_ref,
                hf_ref, hb_ref):
    s = x_ref.shape[0]
    q = SSD_CHUNK
    nh = pc_f_ref.shape[1]
    nc = s // q
    hd = SSM_HEAD_DIM

    def col_params(raw_ref, p_ref):
        dt = _softplus(raw_ref[...] + p_ref[0:1, :])
        return dt, (-LOG2E) * jnp.exp(p_ref[1:2, :]) * dt

    dcf, acf = col_params(dtc_f_ref, pc_f_ref)
    dcb, acb = col_params(dtc_b_ref, pc_b_ref)
    dc_ref[:, 0:nh] = dcf
    dc_ref[:, nh:] = dcb
    ac_ref[:, 0:nh] = acf
    ac_ref[:, nh:] = acb
    dr_ref[0:nh, :] = _softplus(dtr_f_ref[...] + pr_f_ref[:, 0:1])
    dr_ref[nh:, :] = _softplus(dtr_b_ref[...] + pr_b_ref[:, 0:1])

    li = lax.broadcasted_iota(jnp.int32, (q, q), 0)
    si = lax.broadcasted_iota(jnp.int32, (q, q), 1)
    lower = li >= si
    upper = li <= si
    slot_head = lax.broadcasted_iota(jnp.int32, (1, 128), 1) % nh
    pair_lo = lax.broadcasted_iota(jnp.int32, (1, 2 * hd), 1) < hd

    def bdot(a, b):
        return jnp.dot(a, b, preferred_element_type=F32)

    def chunk_cumsums(sl):
        a_col = ac_ref[sl, :]
        tri = tri_ref[...]
        hi, mid, lo = _split3(a_col)
        return a_col, bdot(tri, hi) + bdot(tri, mid) + bdot(tri, lo)

    def expand64(v, parts=2):
        e = e64_ref[...]
        hi = v.astype(BF16)
        out = bdot(hi, e)
        if parts == 2:
            out = out + bdot((v - hi.astype(F32)).astype(BF16), e)
        return out

    hf_ref[...] = jnp.zeros_like(hf_ref)
    hb_ref[...] = jnp.zeros_like(hb_ref)

    def prep_body(c, carry):
        sl = pl.ds(pl.multiple_of(c * q, q), q)
        a_col, i_col = chunk_cumsums(sl)
        d_col = dc_ref[sl, :]
        if_col = i_col[:, 0:nh]
        tot_f = i_col[q - 1:q, 0:nh]
        tot_b = i_col[q - 1:q, nh:]
        ie = i_col - jnp.where(lax.broadcasted_iota(jnp.int32, (1, 2 * nh), 1) < nh, 0.0, a_col)
        eb_col = ie[:, nh:]
        parts = _split3(ie)
        placed = pconst_ref[...]
        for part in range(3):
            placed = placed + bdot(parts[part], place_ref[part])
        pq_ref[sl, :] = placed.astype(BF16)
        xc = x_ref[sl, :]
        xdf_ref[sl, :] = (xc * expand64(jnp.exp2(tot_f - if_col) * d_col[:, 0:nh], parts=1)).astype(BF16)
        xdb_ref[sl, :] = (xc * expand64(jnp.exp2(eb_col) * d_col[:, nh:], parts=1)).astype(BF16)
        eif_ref[sl, :] = expand64(jnp.exp2(if_col))
        er_ref[sl, :] = expand64(jnp.exp2(tot_b - eb_col))
        dec = expand64(jnp.exp2(jnp.concatenate([tot_f, tot_b], axis=0)))
        decf_ref[c] = jnp.broadcast_to(dec[0:1, :], decf_ref.shape[1:])
        decb_ref[c] = jnp.broadcast_to(dec[1:2, :], decb_ref.shape[1:])
        return carry

    lax.fori_loop(0, nc, prep_body, 0, unroll=4)

    def fwd_body(c, carry):
        sl = pl.ds(pl.multiple_of(c * q, q), q)
        xc = x_ref[sl, :]
        bc = b_ref[sl, :].astype(BF16)
        cc = c_ref[sl, :].astype(BF16)
        d_row = dr_ref[:, sl]
        placed = pq_ref[sl, :]
        p_all = placed[:, 0:128]
        q_cat = jnp.concatenate([placed[:, 128:256], placed[:, 256:384]], axis=0)

        cb = lax.dot_general(cc, bc, (((1,), (1,)), ((), ())), preferred_element_type=F32)
        xb = xc.astype(BF16)
        ys = []
        for h0 in range(0, nh, 2):
            ms = []
            for h in (h0, h0 + 1):
                ph = jnp.where(slot_head == h, p_all, jnp.zeros_like(p_all))
                g2 = lax.dot_general(ph, q_cat, (((1,), (1,)), ((), ())), preferred_element_type=F32)
                arg = jnp.where(lower, g2[:, 0:q], g2[:, q:])
                wgt = (jnp.where(lower, d_row[h:h + 1, :], 0.0)
                       + jnp.where(upper, d_row[nh + h:nh + h + 1, :], 0.0))
                ms.append((cb * jnp.exp2(arg) * wgt).astype(BF16))
            xp = xb[:, h0 * hd:(h0 + 2) * hd]
            zero = jnp.zeros_like(xp)
            rhs = jnp.concatenate([jnp.where(pair_lo, xp, zero), jnp.where(pair_lo, zero, xp)], axis=0)
            ys.append(bdot(jnp.concatenate(ms, axis=1), rhs))
        y = jnp.concatenate(ys, axis=1) + dx_ref[...] * xc

        states = lax.dot_general(bc, xdf_ref[sl, :], (((0,), (0,)), ((), ())),
                                 preferred_element_type=F32)
        h_prev = hf_ref[...]
        y = y + bdot(cc, h_prev.astype(BF16)) * eif_ref[sl, :]
        hf_ref[...] = h_prev * decf_ref[c][0:1, :] + states
        y_ref[sl, :] = y
        return carry

    lax.fori_loop(0, nc, fwd_body, 0, unroll=2)

    def bwd_body(t, carry):
        c = nc - 1 - t
        sl = pl.ds(pl.multiple_of(c * q, q), q)
        bc = b_ref[sl, :].astype(BF16)
        cc = c_ref[sl, :].astype(BF16)
        states = lax.dot_general(bc, xdb_ref[sl, :], (((0,), (0,)), ((), ())),
                                 preferred_element_type=F32)
        h_prev = hb_ref[...]
        y_ref[sl, :] += bdot(cc, h_prev.astype(BF16)) * er_ref[sl, :]
        hb_ref[...] = h_prev * decb_ref[c][0:1, :] + states
        return carry

    lax.fori_loop(0, nc, bwd_body, 0, unroll=4)


def _bf16_parts3(v):
    hi = v.astype(BF16).astype(F32)
    r = v - hi
    mid = r.astype(BF16).astype(F32)
    lo = (r - mid).astype(BF16).astype(F32)
    return hi, mid, lo


def _ssd_t_kernel(x_ref, b_ref, c_ref, dt_f_ref, dt_b_ref, p_f_ref, p_b_ref, dx_ref, trio_ref, y_ref,
                  ar_ref, dr_ref, sc_ref, pq_ref, qt_ref, xtb_ref, xdf_ref, xdb_ref, yt_ref, hf_ref, hb_ref):
    s = x_ref.shape[0]
    q = SSD_CHUNK
    nh = p_f_ref.shape[0]
    nc = s // q
    hd = SSM_HEAD_DIM

    def row_params(raw_ref, p_ref):
        dt = _softplus(raw_ref[...] + p_ref[:, 0:1])
        return dt, (-LOG2E) * jnp.exp(p_ref[:, 1:2]) * dt

    dtf, af = row_params(dt_f_ref, p_f_ref)
    dtb, ab = row_params(dt_b_ref, p_b_ref)
    dr_ref[0:nh, :] = dtf
    dr_ref[nh:, :] = dtb
    ar_ref[0:nh, :] = af
    ar_ref[nh:, :] = ab

    li = lax.broadcasted_iota(jnp.int32, (q, q), 0)
    si = lax.broadcasted_iota(jnp.int32, (q, q), 1)
    lower = li >= si
    upper = li <= si
    slot_head = lax.broadcasted_iota(jnp.int32, (1, q), 1) % nh

    def bdot(a, b):
        return jnp.dot(a, b, preferred_element_type=F32)

    def ntdot(a, b):
        return lax.dot_general(a, b, (((1,), (1,)), ((), ())), preferred_element_type=F32)

    def head_rows(v):
        return jnp.concatenate([jnp.broadcast_to(v[h:h + 1, :], (hd, q)) for h in range(nh)], axis=0)

    ones = jnp.ones((nh, q), F32)
    zeros = jnp.zeros((nh, q), F32)

    def prep_body(c, carry):
        sl = pl.ds(pl.multiple_of(c * q, q), q)
        a_row = ar_ref[:, sl]
        d_row = dr_ref[:, sl]
        parts = jnp.concatenate([p.astype(BF16) for p in _bf16_parts3(a_row)], axis=0)
        cs = bdot(parts, trio_ref[...])
        cs = cs[0:2 * nh] + cs[2 * nh:4 * nh] + cs[4 * nh:6 * nh]
        i_f = cs[0:nh, 0:q]
        e_b = cs[nh:, 0:q] - a_row[nh:]
        tot_f = cs[0:nh, q:]
        tot_b = cs[nh:, q:]
        ih, im, il = _bf16_parts3(i_f)
        eh, em, el = _bf16_parts3(e_b)
        pad = [zeros] * (q // nh - DECAY_SLOTS)
        p_t = jnp.concatenate([ih, im, il, ones, ones, ones, -eh, -em, -el, ones, ones, ones] + pad, axis=0)
        qf_t = jnp.concatenate([ones, ones, ones, -ih, -im, -il] + [zeros] * 6 + pad, axis=0)
        qb_t = jnp.concatenate([zeros] * 6 + [ones, ones, ones, eh, em, el] + pad, axis=0)
        pq_ref[sl, :] = p_t.T.astype(BF16)
        qt_ref[c] = jnp.concatenate([qf_t, qb_t], axis=1).astype(BF16)
        sc_ref[0 * nh:1 * nh, sl] = jnp.exp2(i_f)
        sc_ref[1 * nh:2 * nh, sl] = jnp.exp2(tot_b - e_b)
        sc_ref[2 * nh:3 * nh, sl] = jnp.exp2(tot_f)
        sc_ref[3 * nh:4 * nh, sl] = jnp.exp2(tot_b)
        xt = x_ref[sl, :].T
        xtb_ref[:, sl] = xt.astype(BF16)
        xdf_ref[:, sl] = (xt * head_rows(jnp.exp2(tot_f - i_f) * d_row[0:nh])).astype(BF16)
        xdb_ref[:, sl] = (xt * head_rows(jnp.exp2(e_b) * d_row[nh:])).astype(BF16)
        return carry

    lax.fori_loop(0, nc, prep_body, 0, unroll=2)

    hf_ref[...] = jnp.zeros_like(hf_ref)
    hb_ref[...] = jnp.zeros_like(hb_ref)
    zero_half = jnp.zeros((hd, q), BF16)

    def fwd_body(c, carry):
        sl = pl.ds(pl.multiple_of(c * q, q), q)
        bc = b_ref[sl, :].astype(BF16)
        cc = c_ref[sl, :].astype(BF16)
        d_row = dr_ref[:, sl]
        p_all = pq_ref[sl, :]
        q_t = qt_ref[c]
        xtb = xtb_ref[:, sl]
        cb = ntdot(cc, bc)
        g2s = [bdot(jnp.where(slot_head == h, p_all, jnp.zeros_like(p_all)), q_t) for h in range(nh)]
        ms = []
        for h in range(nh):
            arg = jnp.where(lower, g2s[h][:, 0:q], g2s[h][:, q:])
            wgt = (jnp.where(lower, d_row[h:h + 1, :], 0.0)
                   + jnp.where(upper, d_row[nh + h:nh + h + 1, :], 0.0))
            ms.append((cb * jnp.exp2(arg) * wgt).astype(BF16))
        yd = []
        for h0 in range(0, nh, 2):
            lhs = jnp.concatenate(
                [jnp.concatenate([xtb[h0 * hd:(h0 + 1) * hd], zero_half], axis=0),
                 jnp.concatenate([zero_half, xtb[(h0 + 1) * hd:(h0 + 2) * hd]], axis=0)], axis=1)
            yd.append(ntdot(lhs, jnp.concatenate(ms[h0:h0 + 2], axis=1)))
        states = bdot(xdf_ref[:, sl], bc)
        h_prev = hf_ref[...]
        y_off = ntdot(h_prev.astype(BF16), cc) * head_rows(sc_ref[0 * nh:1 * nh, sl])
        hf_ref[...] = h_prev * head_rows(sc_ref[2 * nh:3 * nh, sl]) + states
        yt_ref[:, sl] = jnp.concatenate(yd, axis=0) + y_off
        return carry

    lax.fori_loop(0, nc, fwd_body, 0, unroll=4)

    def bwd_body(t, carry):
        c = nc - 1 - t
        sl = pl.ds(pl.multiple_of(c * q, q), q)
        bc = b_ref[sl, :].astype(BF16)
        cc = c_ref[sl, :].astype(BF16)
        states = bdot(xdb_ref[:, sl], bc)
        h_prev = hb_ref[...]
        y_off = ntdot(h_prev.astype(BF16), cc) * head_rows(sc_ref[1 * nh:2 * nh, sl])
        hb_ref[...] = h_prev * head_rows(sc_ref[3 * nh:4 * nh, sl]) + states
        y_ref[sl, :] = ((yt_ref[:, sl] + y_off).T + dx_ref[...] * x_ref[sl, :]).astype(y_ref.dtype)
        return carry

    lax.fori_loop(0, nc, bwd_body, 0, unroll=4)


def ssd_scan_bidir(xbc, dt, dt_bias_f, dt_bias_b, a_log_f, a_log_b, d_skip):
    bsz, s, _ = xbc.shape
    nheads = dt.shape[2] // 2
    nh = nheads // SSM_GROUPS
    gw = nh * SSM_HEAD_DIM
    d_inner = nheads * SSM_HEAD_DIM
    q = SSD_CHUNK
    dt_row = jnp.transpose(dt.reshape(bsz, s, 2 * SSM_GROUPS, nh), (0, 2, 3, 1))
    prm = jnp.stack([jnp.concatenate([dt_bias_f, dt_bias_b]), jnp.concatenate([a_log_f, a_log_b])])
    p_row = jnp.transpose(prm.reshape(2, 2 * SSM_GROUPS, nh), (1, 2, 0))
    dx = jnp.repeat(d_skip, SSM_HEAD_DIM).reshape(SSM_GROUPS, 1, gw)
    assert DECAY_SLOTS * nh <= q and q % nh == 0 and D_STATE == q
    trio = jnp.asarray(np.concatenate([np.triu(np.ones((q, q), np.float32)), np.ones((q, q), np.float32)],
                                      axis=1), BF16)
    bcol = d_inner // D_STATE
    G = SSM_GROUPS
    nc = s // q

    return pl.pallas_call(
        _ssd_t_kernel,
        grid=(bsz, SSM_GROUPS),
        in_specs=[pl.BlockSpec((None, s, gw), lambda b, g: (b, 0, g)),
                  pl.BlockSpec((None, s, D_STATE), lambda b, g: (b, 0, bcol + g)),
                  pl.BlockSpec((None, s, D_STATE), lambda b, g: (b, 0, bcol + G + g)),
                  pl.BlockSpec((None, None, nh, s), lambda b, g: (b, g, 0, 0)),
                  pl.BlockSpec((None, None, nh, s), lambda b, g: (b, G + g, 0, 0)),
                  pl.BlockSpec((None, nh, 2), lambda b, g: (g, 0, 0)),
                  pl.BlockSpec((None, nh, 2), lambda b, g: (G + g, 0, 0)),
                  pl.BlockSpec((None, 1, gw), lambda b, g: (g, 0, 0)),
                  pl.BlockSpec((q, 2 * q), lambda b, g: (0, 0))],
        out_specs=pl.BlockSpec((None, s, gw), lambda b, g: (b, 0, g)),
        out_shape=jax.ShapeDtypeStruct((bsz, s, d_inner), BF16),
        scratch_shapes=[pltpu.VMEM((2 * nh, s), F32), pltpu.VMEM((2 * nh, s), F32),
                        pltpu.VMEM((4 * nh, s), F32),
                        pltpu.VMEM((s, q), BF16), pltpu.VMEM((nc, q, 2 * q), BF16),
                        pltpu.VMEM((gw, s), BF16), pltpu.VMEM((gw, s), BF16), pltpu.VMEM((gw, s), BF16),
                        pltpu.VMEM((gw, s), F32),
                        pltpu.VMEM((gw, D_STATE), F32), pltpu.VMEM((gw, D_STATE), F32)],
        compiler_params=_cparams("parallel", "parallel"),
        name="ssd_scan",
    )(xbc, xbc, xbc, dt_row, dt_row, p_row, p_row, dx, trio)


def _gated_proj_kernel(y_ref, z_ref, gw_ref, w_ref, x_ref, g_ref, o_ref):
    t = y_ref[...].astype(F32) * _silu(z_ref[...].astype(F32))
    ms = jnp.mean(t * t, axis=-1, keepdims=True)
    t = (t * lax.rsqrt(ms + EPS) * gw_ref[...]).astype(BF16)
    o_ref[...] = x_ref[...] + g_ref[...] * jnp.dot(t, w_ref[...], preferred_element_type=F32)


def gated_proj_residual(y, zproj, gw, w, x, g, *, ts):
    bsz, s, d = x.shape
    k = y.shape[2]
    return pl.pallas_call(
        _gated_proj_kernel,
        grid=(bsz, s // ts),
        in_specs=[pl.BlockSpec((None, ts, k), lambda b, i: (b, i, 0)),
                  pl.BlockSpec((None, ts, k), lambda b, i: (b, i, 0)),
                  pl.BlockSpec((1, k), lambda b, i: (0, 0)),
                  pl.BlockSpec((k, d), lambda b, i: (0, 0)),
                  pl.BlockSpec((None, ts, d), lambda b, i: (b, i, 0)),
                  pl.BlockSpec((None, 1, d), lambda b, i: (b, 0, 0))],
        out_specs=pl.BlockSpec((None, ts, d), lambda b, i: (b, i, 0)),
        out_shape=jax.ShapeDtypeStruct((bsz, s, d), F32),
        compiler_params=_cparams("parallel", "parallel"),
        name="ssd_out_proj",
    )(y, zproj, gw.reshape(1, k), w, x, g)


def _router_kernel(x_ref, nw_ref, sh_ref, sc_ref, rw_ref, rb_ref, lt_ref,
                   eidx_ref, rank_ref, wgt_ref, cnt_ref):
    h = _modnorm(x_ref[...], nw_ref[...], sh_ref[...], sc_ref[...])
    rw = rw_ref[...]
    h_hi = h.astype(BF16)
    h_lo = (h - h_hi.astype(F32)).astype(BF16)
    rw_hi = rw.astype(BF16)
    rw_lo = (rw - rw_hi.astype(F32)).astype(BF16)
    logits = (jnp.dot(h_hi, rw_hi, preferred_element_type=F32)
              + (jnp.dot(h_lo, rw_hi, preferred_element_type=F32)
                 + jnp.dot(h_hi, rw_lo, preferred_element_type=F32))) + rb_ref[...]
    ts, ne = logits.shape
    eid = lax.broadcasted_iota(jnp.int32, (ts, ne), 1)
    m1 = jnp.max(logits, axis=-1, keepdims=True)
    i1 = jnp.min(jnp.where(logits == m1, eid, ne), axis=-1, keepdims=True)
    rest = jnp.where(eid == i1, -jnp.inf, logits)
    m2 = jnp.max(rest, axis=-1, keepdims=True)
    i2 = jnp.min(jnp.where(rest == m2, eid, ne), axis=-1, keepdims=True)
    e2 = jnp.exp(m2 - m1)
    w1 = 1.0 / (1.0 + e2)
    w2 = e2 / (1.0 + e2)
    oh1 = (eid == i1).astype(F32)
    oh2 = (eid == i2).astype(F32)
    chosen = oh1 + oh2
    incl = jnp.dot(lt_ref[...], chosen.astype(BF16), preferred_element_type=F32)
    before = incl - chosen
    r1 = jnp.sum(oh1 * before, axis=-1, keepdims=True)
    r2 = jnp.sum(oh2 * before, axis=-1, keepdims=True)
    eidx_ref[...] = jnp.concatenate([i1, i2], axis=1)
    rank_ref[...] = jnp.concatenate([r1, r2], axis=1).astype(jnp.int32)
    wgt_ref[...] = jnp.concatenate([w1, w2], axis=1)
    cnt_ref[...] = incl[ts - 1:ts, :].astype(jnp.int32)


def moe_route(x, nw, sh, sc, router_w, router_b, *, ts):
    bsz, s, d = x.shape
    ne = router_w.shape[1]
    nt = s // ts
    lt = jnp.asarray(np.tril(np.ones((ts, ts), np.float32)), BF16)
    tok = lambda dt: jax.ShapeDtypeStruct((bsz * s, TOP_K), dt)
    tok_spec = pl.BlockSpec((ts, TOP_K), lambda b, i: (b * nt + i, 0))
    return pl.pallas_call(
        _router_kernel,
        grid=(bsz, nt),
        in_specs=[pl.BlockSpec((None, ts, d), lambda b, i: (b, i, 0)),
                  pl.BlockSpec((1, d), lambda b, i: (0, 0)),
                  pl.BlockSpec((None, 1, d), lambda b, i: (b, 0, 0)),
                  pl.BlockSpec((None, 1, d), lambda b, i: (b, 0, 0)),
                  pl.BlockSpec((d, ne), lambda b, i: (0, 0)),
                  pl.BlockSpec((1, ne), lambda b, i: (0, 0)),
                  pl.BlockSpec((ts, ts), lambda b, i: (0, 0))],
        out_specs=[tok_spec, tok_spec, tok_spec,
                   pl.BlockSpec((None, 1, ne), lambda b, i: (b * nt + i, 0, 0))],
        out_shape=[tok(jnp.int32), tok(jnp.int32), tok(F32),
                   jax.ShapeDtypeStruct((bsz * nt, 1, ne), jnp.int32)],
        compiler_params=_cparams("parallel", "parallel"),
        name="moe_router",
    )(x, nw.reshape(1, d), sh, sc, router_w, router_b.reshape(1, ne), lt)


SEG_ROWS = 16
SEG_FIELDS = 3


def _segment_copies(seg_ref, tile, n_experts, make_copy, *, wait):
    for e in range(n_experts):
        base = (tile * n_experts + e) * SEG_FIELDS
        local0 = seg_ref[base]
        global0 = seg_ref[base + 1]

        def body(i, carry, local0=local0, global0=global0):
            cp = make_copy(pl.multiple_of(local0 + i * SEG_ROWS, SEG_ROWS),
                           pl.multiple_of(global0 + i * SEG_ROWS, SEG_ROWS))
            if wait:
                cp.wait()
            else:
                cp.start()
            return carry

        lax.fori_loop(0, seg_ref[base + 2], body, 0)


def _dispatch_kernel(seg_ref, x_ref, nw_ref, sh_ref, sc_ref, ld_ref, hs_in_ref, hs_ref, buf_ref, sem):
    del hs_in_ref
    tt = x_ref.shape[0]
    lc = buf_ref.shape[1]
    ne = N_EXPERTS
    tile = pl.program_id(0) * pl.num_programs(1) + pl.program_id(1)
    last = pl.num_programs(0) * pl.num_programs(1) - 1
    slot = tile % 2

    def copies(t, sl, wait):
        def make_copy(lo, go):
            return pltpu.make_async_copy(buf_ref.at[sl, pl.ds(lo, SEG_ROWS), :],
                                         hs_ref.at[pl.ds(go, SEG_ROWS), :], sem.at[sl])
        _segment_copies(seg_ref, t, ne, make_copy, wait=wait)

    h = _modnorm(x_ref[...], nw_ref[...], sh_ref[...], sc_ref[...]).astype(BF16)
    ld = ld_ref[...]
    rows = lax.broadcasted_iota(jnp.int32, (lc, tt), 0)
    perm = jnp.where(rows == ld[0:1, :], 1.0, jnp.where(rows == ld[1:2, :], 1.0, 0.0)).astype(BF16)
    buf_ref[slot] = jnp.dot(perm, h, preferred_element_type=F32).astype(BF16)
    copies(tile, slot, wait=False)

    @pl.when(tile > 0)
    def _():
        copies(tile - 1, 1 - slot, wait=True)

    @pl.when(tile == last)
    def _():
        copies(tile, slot, wait=True)


def moe_dispatch(x, nw, sh, sc, seg, ldest_rows, n_rows, *, tt, lc):
    bsz, s, d = x.shape
    nt = s // tt
    hs0 = jnp.zeros((n_rows, d), BF16)
    grid_spec = pltpu.PrefetchScalarGridSpec(
        num_scalar_prefetch=1,
        grid=(bsz, nt),
        in_specs=[pl.BlockSpec((None, tt, d), lambda b, i, sref: (b, i, 0)),
                  pl.BlockSpec((1, d), lambda b, i, sref: (0, 0)),
                  pl.BlockSpec((None, 1, d), lambda b, i, sref: (b, 0, 0)),
                  pl.BlockSpec((None, 1, d), lambda b, i, sref: (b, 0, 0)),
                  pl.BlockSpec((TOP_K, tt), lambda b, i, sref: (0, b * nt + i)),
                  pl.BlockSpec(memory_space=pl.ANY)],
        out_specs=pl.BlockSpec(memory_space=pl.ANY),
        scratch_shapes=[pltpu.VMEM((2, lc, d), BF16), pltpu.SemaphoreType.DMA((2,))],
    )
    return pl.pallas_call(
        _dispatch_kernel,
        grid_spec=grid_spec,
        out_shape=jax.ShapeDtypeStruct((n_rows, d), BF16),
        input_output_aliases={6: 0},
        compiler_params=_cparams("arbitrary", "arbitrary"),
        name="moe_dispatch",
    )(seg, x, nw.reshape(1, d), sh, sc, ldest_rows, hs0)


def _moe_kernel(te_ref, nu_ref, hs_ref, w1_ref, w3_ref, w2_ref, o_ref, acc_ref):
    i = pl.program_id(0)
    f = pl.program_id(1)

    @pl.when(i < nu_ref[0])
    def _():
        @pl.when(f == 0)
        def _():
            acc_ref[...] = jnp.zeros_like(acc_ref)

        h = hs_ref[...]
        for lo in range(0, w1_ref.shape[1], MOE_HIDDEN_CHUNK):
            hi = lo + MOE_HIDDEN_CHUNK
            a = jnp.dot(h, w1_ref[:, lo:hi], preferred_element_type=F32)
            b = jnp.dot(h, w3_ref[:, lo:hi], preferred_element_type=F32)
            t = (_silu(a) * b).astype(BF16)
            acc_ref[...] += jnp.dot(t, w2_ref[lo:hi, :], preferred_element_type=F32)

        @pl.when(f == pl.num_programs(1) - 1)
        def _():
            o_ref[...] = acc_ref[...].astype(o_ref.dtype)

    @pl.when((i >= nu_ref[0]) & (f == 0))
    def _():
        o_ref[...] = jnp.zeros_like(o_ref)


def moe_experts(hs, tile_expert, n_used, w1, w3, w2, *, tm, tf):
    n_rows, d = hs.shape
    dff = w1.shape[2]
    nf = dff // tf
    n_tiles = n_rows // tm

    def last_used(i, nu):
        return jnp.maximum(jnp.minimum(i, nu[0] - 1), 0)

    def row_map(i, f, te, nu):
        return (last_used(i, nu), 0)

    def hidden_block(i, f, nu):
        t = last_used(i, nu)
        step = jnp.where(i < nu[0], f, nf - 1)
        return jnp.where(t % 2 == 0, step, nf - 1 - step)

    def w_in_map(i, f, te, nu):
        return (te[last_used(i, nu)], 0, hidden_block(i, f, nu))

    def w_out_map(i, f, te, nu):
        return (te[last_used(i, nu)], hidden_block(i, f, nu), 0)

    grid_spec = pltpu.PrefetchScalarGridSpec(
        num_scalar_prefetch=2,
        grid=(n_tiles, nf),
        in_specs=[pl.BlockSpec((tm, d), row_map),
                  pl.BlockSpec((None, d, tf), w_in_map),
                  pl.BlockSpec((None, d, tf), w_in_map),
                  pl.BlockSpec((None, tf, d), w_out_map)],
        out_specs=pl.BlockSpec((tm, d), lambda i, f, te, nu: (i, 0)),
        scratch_shapes=[pltpu.VMEM((tm, d), F32)],
    )
    return pl.pallas_call(
        _moe_kernel,
        grid_spec=grid_spec,
        out_shape=jax.ShapeDtypeStruct((n_rows, d), BF16),
        compiler_params=_cparams("arbitrary", "arbitrary"),
        name="moe_experts",
    )(tile_expert, n_used, hs, w1, w3, w2)


def _combine_kernel(seg_ref, ys_ref, x_ref, g_ref, wgt_ref, ld_ref, o_ref, buf_ref, sem):
    tt = x_ref.shape[0]
    lc = buf_ref.shape[1]
    ne = N_EXPERTS
    tile = pl.program_id(0) * pl.num_programs(1) + pl.program_id(1)
    last = pl.num_programs(0) * pl.num_programs(1) - 1
    slot = tile % 2

    def copies(t, sl, wait):
        def make_copy(lo, go):
            return pltpu.make_async_copy(ys_ref.at[pl.ds(go, SEG_ROWS), :],
                                         buf_ref.at[sl, pl.ds(lo, SEG_ROWS), :], sem.at[sl])
        _segment_copies(seg_ref, t, ne, make_copy, wait=wait)

    @pl.when(tile == 0)
    def _():
        buf_ref[...] = jnp.zeros_like(buf_ref)
        copies(tile, slot, wait=False)

    @pl.when(tile < last)
    def _():
        copies(tile + 1, 1 - slot, wait=False)

    copies(tile, slot, wait=True)

    ld = ld_ref[...]
    cols = lax.broadcasted_iota(jnp.int32, (tt, lc), 1)
    pick = jnp.concatenate([jnp.where(cols == ld[:, k:k + 1], 1.0, 0.0) for k in range(TOP_K)],
                           axis=0).astype(BF16)
    z = jnp.dot(pick, buf_ref[slot], preferred_element_type=F32)
    w = wgt_ref[...]
    mix = w[:, 0:1] * z[0:tt] + w[:, 1:2] * z[tt:]
    o_ref[...] = x_ref[...] + g_ref[...] * mix


def moe_combine(ys, seg, ldest, wgt, x, g, *, tt, lc):
    bsz, s, d = x.shape
    nt = s // tt
    tok_spec = pl.BlockSpec((tt, TOP_K), lambda b, i, sref: (b * nt + i, 0))
    grid_spec = pltpu.PrefetchScalarGridSpec(
        num_scalar_prefetch=1,
        grid=(bsz, nt),
        in_specs=[pl.BlockSpec(memory_space=pl.ANY),
                  pl.BlockSpec((None, tt, d), lambda b, i, sref: (b, i, 0)),
                  pl.BlockSpec((None, 1, d), lambda b, i, sref: (b, 0, 0)),
                  tok_spec, tok_spec],
        out_specs=pl.BlockSpec((None, tt, d), lambda b, i, sref: (b, i, 0)),
        scratch_shapes=[pltpu.VMEM((2, lc, d), BF16), pltpu.SemaphoreType.DMA((2,))],
    )
    return pl.pallas_call(
        _combine_kernel,
        grid_spec=grid_spec,
        out_shape=jax.ShapeDtypeStruct((bsz, s, d), F32),
        compiler_params=_cparams("arbitrary", "arbitrary"),
        name="moe_combine",
    )(seg, ys, x, g, wgt, ldest)


def _round_up(v, m):
    return ((v + m - 1) // m) * m


def moe_residual(x, nw, sh, sc, g, router_w, router_b, w1, w3, w2, *, tm=MOE_TILE_ROWS):
    bsz, s, d = x.shape
    n_tok = bsz * s
    ne = router_w.shape[1]
    tt = min(512, s)
    n_tt = n_tok // tt
    lc = _round_up(TOP_K * tt + ne * SEG_ROWS, 128)
    eidx, rank, wgt, cnt = moe_route(x, nw, sh, sc, router_w, router_b, ts=tt)
    seg_len = _round_up(cnt.reshape(n_tt, ne), SEG_ROWS)
    local_start = jnp.cumsum(seg_len, axis=1) - seg_len
    padded = _round_up(jnp.sum(seg_len, axis=0), tm)
    ends = jnp.cumsum(padded)
    global_start = (ends - padded)[None, :] + jnp.cumsum(seg_len, axis=0) - seg_len
    seg = jnp.stack([local_start, global_start, seg_len // SEG_ROWS], axis=-1).reshape(-1).astype(jnp.int32)
    onehot = eidx[:, :, None] == jnp.arange(ne, dtype=jnp.int32)
    start_tok = jnp.repeat(local_start, tt, axis=0)[:, None, :]
    ldest = (jnp.sum(jnp.where(onehot, start_tok, 0), axis=-1) + rank).astype(jnp.int32)
    n_rows = _round_up(n_tok * TOP_K + n_tt * ne * SEG_ROWS + ne * tm, tm)
    n_tiles = n_rows // tm
    tile_start = jnp.arange(n_tiles, dtype=jnp.int32) * tm
    tile_expert = jnp.minimum(jnp.sum(tile_start[:, None] >= ends[None, :], axis=1), ne - 1).astype(jnp.int32)
    n_used = (ends[ne - 1:ne] // tm).astype(jnp.int32)
    hs = moe_dispatch(x, nw, sh, sc, seg, ldest.T, n_rows, tt=tt, lc=lc)
    ys = moe_experts(hs, tile_expert, n_used, w1, w3, w2, tm=tm, tf=w1.shape[2] // 2)
    return moe_combine(ys, seg, ldest, wgt, x, g, tt=tt, lc=lc)


def _split_mod(mod):
    return [m[:, None, :] for m in jnp.split(mod, 6, axis=-1)]


def even_layer(x, c, rel_bias, ada_w, ada_b, norm1_w, in_w, q_norm_w, k_norm_w, sink, out_w,
               norm2_w, w1, w3, w2):
    s = x.shape[1]
    sh1, sc1, g1, sh2, sc2, g2 = _split_mod(ada_mod(c, ada_w, ada_b))
    proj = norm_mod_matmul(x, norm1_w, sh1, sc1, in_w.astype(BF16), ts=min(1024, s), tn=256, name="even_in_proj",
                           out_dtype=BF16)
    yf = fourier_mix(proj, tq=min(512, s))
    ya = window_attention(proj, band_bias(rel_bias), q_norm_w, k_norm_w, sink)
    x = cat_proj_residual(yf, ya, out_w.astype(BF16), x, g1, ts=min(1024, s))
    dff = w1.shape[1]
    return ffn_residual(x, norm2_w, sh2, sc2, g2, w1.astype(BF16), w3.astype(BF16), w2.astype(BF16),
                        ts=min(1024, s), tf=256)


def odd_layer(x, c, ada_w, ada_b, norm1_w, in_w, conv_w, conv_b, dt_bias_f, dt_bias_b, a_log_f, a_log_b,
              d_skip, gnorm_w, out_w, norm2_w, router_w, router_b, w1, w3, w2):
    s = x.shape[1]
    sh1, sc1, g1, sh2, sc2, g2 = _split_mod(ada_mod(c, ada_w, ada_b))
    d_inner = gnorm_w.shape[0]
    cdim = conv_w.shape[1]
    wide = d_inner + cdim
    in_w = in_w.astype(BF16)
    zx = norm_mod_matmul(x, norm1_w, sh1, sc1, in_w[:, :wide], ts=min(1024, s), tn=1024, name="odd_in_proj",
                         out_dtype=BF16)
    dt = norm_mod_matmul(x, norm1_w, sh1, sc1, in_w[:, wide:], ts=min(1024, s), tn=in_w.shape[1] - wide,
                         name="odd_dt_proj")
    xbc = conv_silu(zx, conv_w, conv_b, col0=d_inner, tc=512)
    y = ssd_scan_bidir(xbc, dt, dt_bias_f, dt_bias_b, a_log_f, a_log_b, d_skip)
    x = gated_proj_residual(y, zx, gnorm_w, out_w.astype(BF16), x, g1, ts=min(512, s))
    return moe_residual(x, norm2_w, sh2, sc2, g2, router_w, router_b,
                        w1.astype(BF16), w3.astype(BF16), w2.astype(BF16))


def kernel(x, c, rel_bias, ev_ada_w, ev_ada_b, ev_norm1_w, ev_in_w, ev_q_norm_w, ev_k_norm_w, ev_sink, ev_out_w, ev_norm2_w, ev_ffn_w1, ev_ffn_w3, ev_ffn_w2, od_ada_w, od_ada_b, od_norm1_w, od_in_w, od_conv_w, od_conv_b, od_dt_bias_f, od_dt_bias_b, od_A_log_f, od_A_log_b, od_D, od_gnorm_w, od_out_w, od_norm2_w, od_router_w, od_router_b, od_moe_w1, od_moe_w3, od_moe_w2):
    depth = ev_ada_w.shape[0] + od_ada_w.shape[0]
    for i in range(depth):
        j = i // 2
        if i % 2 == 0:
            x = even_layer(x, c, rel_bias, ev_ada_w[j], ev_ada_b[j], ev_norm1_w[j], ev_in_w[j],
                           ev_q_norm_w[j], ev_k_norm_w[j], ev_sink[j], ev_out_w[j], ev_norm2_w[j],
                           ev_ffn_w1[j], ev_ffn_w3[j], ev_ffn_w2[j])
        else:
            x = odd_layer(x, c, od_ada_w[j], od_ada_b[j], od_norm1_w[j], od_in_w[j], od_conv_w[j],
                          od_conv_b[j], od_dt_bias_f[j], od_dt_bias_b[j], od_A_log_f[j], od_A_log_b[j],
                          od_D[j], od_gnorm_w[j], od_out_w[j], od_norm2_w[j], od_router_w[j],
                          od_router_b[j], od_moe_w1[j], od_moe_w3[j], od_moe_w2[j])
    return x
```

```python
import functools

import numpy as np
import jax
import jax.numpy as jnp
from jax import lax
from jax.experimental import pallas as pl
from jax.experimental.pallas import tpu as pltpu

F32 = jnp.float32
BF16 = jnp.bfloat16
HIGHEST = lax.Precision.HIGHEST

EPS = 1e-6
FNET_GROUPS = 4
FNET_GROUP_DIM = 128
FNET_WIDTH = FNET_GROUPS * FNET_GROUP_DIM
ATTN_HEADS = 8
ATTN_KV_HEADS = 2
HEAD_DIM = 64
ATTN_WIDTH = ATTN_HEADS * HEAD_DIM
KV_WIDTH = ATTN_KV_HEADS * HEAD_DIM
WINDOW = 128
BLOCK = 128
REL_BUCKETS = 32
REL_MAX_DIST = 128
SSM_HEAD_DIM = 64
SSM_GROUPS = 4
D_STATE = 128
CONV_WIDTH = 5
SSD_CHUNK = 128
N_EXPERTS = 8
TOP_K = 2
NEG_BIG = -1e30
ATTN_QBLOCKS = 2

V7X_VMEM_LIMIT_BYTES = 56 * 1024 * 1024
MOE_TILE_ROWS = 512
MOE_HIDDEN_CHUNK = 256


def _cparams(*sem):
    return pltpu.CompilerParams(dimension_semantics=sem, vmem_limit_bytes=V7X_VMEM_LIMIT_BYTES)


def _modnorm(x, nw, sh, sc):
    ms = jnp.mean(x * x, axis=-1, keepdims=True)
    return x * lax.rsqrt(ms + EPS) * nw * (1.0 + sc) + sh


def _silu(x):
    return x * (1.0 / (1.0 + jnp.exp(-x)))


def _ada_kernel(c_ref, w_ref, b_ref, o_ref):
    cs = _silu(c_ref[...]).astype(BF16)
    o_ref[...] = jnp.dot(cs, w_ref[...].astype(BF16), preferred_element_type=F32) + b_ref[...]


def ada_mod(c, w, b):
    bsz, d = c.shape
    n = w.shape[1]
    tn = 1536
    return pl.pallas_call(
        _ada_kernel,
        grid=(n // tn,),
        in_specs=[pl.BlockSpec((bsz, d), lambda j: (0, 0)),
                  pl.BlockSpec((d, tn), lambda j: (0, j)),
                  pl.BlockSpec((1, tn), lambda j: (0, j))],
        out_specs=pl.BlockSpec((bsz, tn), lambda j: (0, j)),
        out_shape=jax.ShapeDtypeStruct((bsz, n), F32),
        compiler_params=_cparams("arbitrary"),
        name="ada_mod",
    )(c, w, b.reshape(1, n))


def _nmm_kernel(x_ref, nw_ref, sh_ref, sc_ref, w_ref, o_ref, *, tn):
    h = _modnorm(x_ref[...], nw_ref[...], sh_ref[...], sc_ref[...]).astype(BF16)
    n = w_ref.shape[1]
    for lo in range(0, n, tn):
        o_ref[:, lo:lo + tn] = jnp.dot(h, w_ref[:, lo:lo + tn],
                                       preferred_element_type=F32).astype(o_ref.dtype)


def _resident(shape):
    return pl.BlockSpec(shape, lambda *_: tuple(0 for _ in shape), pipeline_mode=pl.Buffered(1))


def norm_mod_matmul(x, nw, sh, sc, w, *, ts, tn, name, out_dtype=F32):
    bsz, s, d = x.shape
    n = w.shape[1]
    assert n % tn == 0
    return pl.pallas_call(
        functools.partial(_nmm_kernel, tn=tn),
        grid=(bsz, s // ts),
        in_specs=[pl.BlockSpec((None, ts, d), lambda b, i: (b, i, 0)),
                  pl.BlockSpec((1, d), lambda b, i: (0, 0)),
                  pl.BlockSpec((None, 1, d), lambda b, i: (b, 0, 0)),
                  pl.BlockSpec((None, 1, d), lambda b, i: (b, 0, 0)),
                  _resident((d, n))],
        out_specs=pl.BlockSpec((None, ts, n), lambda b, i: (b, i, 0)),
        out_shape=jax.ShapeDtypeStruct((bsz, s, n), out_dtype),
        compiler_params=_cparams("parallel", "parallel"),
        name=name,
    )(x, nw.reshape(1, d), sh, sc, w)


def _dft_cos_sin(n):
    k = np.arange(n, dtype=np.int64)
    ang = ((k[:, None] * k[None, :]) % n).astype(np.float64) * (2.0 * np.pi / n)
    scale = 1.0 / np.sqrt(n)
    return np.cos(ang) * scale, np.sin(ang) * scale


def _fourier_kernel(u_ref, chan_ref, seq_ref, o_ref, ab_ref):
    s = u_ref.shape[0]

    @pl.when(pl.program_id(1) == 0)
    def _():
        for g in range(FNET_GROUPS):
            lo, hi = g * FNET_GROUP_DIM, (g + 1) * FNET_GROUP_DIM
            ug = u_ref[:, lo:hi].astype(BF16)
            cs = jnp.dot(ug, chan_ref[...], preferred_element_type=F32)
            ab_ref[0:s, lo:hi] = cs[:, :FNET_GROUP_DIM].astype(BF16)
            ab_ref[s:2 * s, lo:hi] = cs[:, FNET_GROUP_DIM:].astype(BF16)

    o_ref[...] = jnp.dot(seq_ref[...], ab_ref[...], preferred_element_type=F32).astype(o_ref.dtype)


def fourier_mix(proj, *, tq):
    bsz, s, _ = proj.shape
    cc, sc = _dft_cos_sin(FNET_GROUP_DIM)
    chan = jnp.asarray(np.concatenate([cc, sc], axis=1), BF16)
    cs, ss = _dft_cos_sin(s)
    seq = jnp.asarray(np.concatenate([cs, -ss], axis=1), BF16)
    return pl.pallas_call(
        _fourier_kernel,
        grid=(bsz, s // tq),
        in_specs=[pl.BlockSpec((None, s, FNET_WIDTH), lambda b, i: (b, 0, 0)),
                  pl.BlockSpec((FNET_GROUP_DIM, 2 * FNET_GROUP_DIM), lambda b, i: (0, 0)),
                  pl.BlockSpec((tq, 2 * s), lambda b, i: (i, 0))],
        out_specs=pl.BlockSpec((None, tq, FNET_WIDTH), lambda b, i: (b, i, 0)),
        out_shape=jax.ShapeDtypeStruct((bsz, s, FNET_WIDTH), BF16),
        scratch_shapes=[pltpu.VMEM((2 * s, FNET_WIDTH), BF16)],
        compiler_params=_cparams("parallel", "arbitrary"),
        name="fourier_mix",
    )(proj, chan, seq)


def _band_bucket_table():
    i = np.arange(BLOCK)[:, None]
    j = np.arange(3 * BLOCK)[None, :]
    rel = (j - BLOCK) - i
    half = REL_BUCKETS // 2
    max_exact = half // 2
    n = np.abs(rel)
    large = max_exact + (np.log(np.maximum(n, 1) / max_exact)
                         / np.log(REL_MAX_DIST / max_exact) * (half - max_exact)).astype(np.int32)
    large = np.minimum(large, half - 1)
    bucket = (rel > 0).astype(np.int32) * half + np.where(n < max_exact, n, large)
    return np.where(n <= WINDOW, bucket, -1).astype(np.int32)


def _bias_kernel(rb_ref, bucket_ref, o_ref):
    h = pl.program_id(0)
    bucket = bucket_ref[...]
    acc = jnp.full(bucket.shape, NEG_BIG, F32)
    for bkt in range(REL_BUCKETS):
        acc = jnp.where(bucket == bkt, rb_ref[bkt * ATTN_HEADS + h], acc)
    o_ref[...] = acc


def band_bias(rel_bias):
    bucket = jnp.asarray(_band_bucket_table())
    return pl.pallas_call(
        _bias_kernel,
        grid=(ATTN_HEADS,),
        in_specs=[pl.BlockSpec(memory_space=pltpu.SMEM),
                  pl.BlockSpec((BLOCK, 3 * BLOCK), lambda h: (0, 0))],
        out_specs=pl.BlockSpec((None, BLOCK, 3 * BLOCK), lambda h: (h, 0, 0)),
        out_shape=jax.ShapeDtypeStruct((ATTN_HEADS, BLOCK, 3 * BLOCK), F32),
        compiler_params=_cparams("arbitrary"),
        name="band_bias",
    )(rel_bias.reshape(-1), bucket)


def _head_mean_matrix(width):
    m = np.zeros((width, width), np.float32)
    for h in range(width // HEAD_DIM):
        m[h * HEAD_DIM:(h + 1) * HEAD_DIM, h * HEAD_DIM:(h + 1) * HEAD_DIM] = 1.0 / HEAD_DIM
    return m


def _heads_rms(t, mean_mat, w):
    sq = t * t
    hi = sq.astype(BF16)
    lo = (sq - hi.astype(F32)).astype(BF16)
    ms = (jnp.dot(hi, mean_mat, preferred_element_type=F32)
          + jnp.dot(lo, mean_mat, preferred_element_type=F32))
    return t * lax.rsqrt(ms + EPS) * w


def _attn_kernel(sink_ref, q_ref, kl_ref, kc_ref, kr_ref, vl_ref, vc_ref, vr_ref,
                 bias_ref, qnw_ref, knw_ref, qmean_ref, kmean_ref, o_ref):
    n = pl.program_id(1)
    nb = pl.num_programs(1) * ATTN_QBLOCKS
    k = jnp.concatenate([kl_ref[...], kc_ref[...], kr_ref[...]], axis=0).astype(F32)
    v = jnp.concatenate([vl_ref[...], vc_ref[...], vr_ref[...]], axis=0).astype(F32)
    col = lax.broadcasted_iota(jnp.int32, (1, 3 * BLOCK), 1)
    qn = _heads_rms(q_ref[...].astype(F32), qmean_ref[...], qnw_ref[...])
    kn = _heads_rms(k, kmean_ref[...], knw_ref[...])
    low = lax.broadcasted_iota(jnp.int32, (1, 2 * HEAD_DIM), 1) < HEAD_DIM
    kn_sw = pltpu.roll(kn, HEAD_DIM, axis=1)
    v_sw = pltpu.roll(v, HEAD_DIM, axis=1)
    k_dup = [jnp.where(low, kn, kn_sw).astype(BF16), jnp.where(low, kn_sw, kn).astype(BF16)]
    ones = jnp.ones((k.shape[0], 2 * HEAD_DIM), BF16)
    v_ext = [jnp.concatenate([v.astype(BF16), ones], axis=1),
             jnp.concatenate([v_sw.astype(BF16), ones], axis=1)]
    g = ATTN_HEADS // ATTN_KV_HEADS
    for qb in range(ATTN_QBLOCKS):
        blk = n * ATTN_QBLOCKS + qb
        band = slice(qb * BLOCK, (qb + 3) * BLOCK)
        first_key = jnp.where(blk == 0, BLOCK, 0)
        end_key = jnp.where(blk == nb - 1, 2 * BLOCK, 3 * BLOCK)
        outside = (col < first_key) | (col >= end_key)
        pairs = []
        for m in range(ATTN_HEADS // 2):
            j = (2 * m) // g
            qp = qn[qb * BLOCK:(qb + 1) * BLOCK, m * 2 * HEAD_DIM:(m + 1) * 2 * HEAD_DIM]
            res = []
            for idx in range(2):
                h = 2 * m + idx
                qm = jnp.where(low if idx == 0 else jnp.logical_not(low), qp, 0.0).astype(BF16)
                logits = lax.dot_general(qm, k_dup[j][band], (((1,), (1,)), ((), ())),
                                         preferred_element_type=F32)
                logits = jnp.where(outside, NEG_BIG, logits + bias_ref[h])
                sk = sink_ref[h]
                mx = jnp.maximum(jnp.max(logits, axis=-1, keepdims=True), sk)
                p = jnp.exp(logits - mx).astype(BF16)
                r = jnp.dot(p, v_ext[idx if j == 0 else 1 - idx][band], preferred_element_type=F32)
                denom = r[:, 2 * HEAD_DIM:] + jnp.exp(sk - mx)
                res.append(r[:, :2 * HEAD_DIM] / denom)
            pairs.append(jnp.where(low, res[0], res[1]))
        o_ref[qb * BLOCK:(qb + 1) * BLOCK, :] = jnp.concatenate(pairs, axis=-1).astype(o_ref.dtype)


def window_attention(proj, bias, q_norm_w, k_norm_w, sink):
    bsz, s, _ = proj.shape
    nb = s // BLOCK
    qcol = FNET_WIDTH // ATTN_WIDTH
    kcol = (FNET_WIDTH + ATTN_WIDTH) // KV_WIDTH
    vcol = kcol + 1

    qb = ATTN_QBLOCKS
    assert nb % qb == 0

    def kv_specs(col):
        return [pl.BlockSpec((None, BLOCK, KV_WIDTH), lambda b, n: (b, jnp.maximum(n * qb - 1, 0), col)),
                pl.BlockSpec((None, qb * BLOCK, KV_WIDTH), lambda b, n: (b, n, col)),
                pl.BlockSpec((None, BLOCK, KV_WIDTH), lambda b, n: (b, jnp.minimum((n + 1) * qb, nb - 1), col))]

    return pl.pallas_call(
        _attn_kernel,
        grid=(bsz, nb // qb),
        in_specs=[pl.BlockSpec(memory_space=pltpu.SMEM),
                  pl.BlockSpec((None, qb * BLOCK, ATTN_WIDTH), lambda b, n: (b, n, qcol)),
                  *kv_specs(kcol), *kv_specs(vcol),
                  pl.BlockSpec((ATTN_HEADS, BLOCK, 3 * BLOCK), lambda b, n: (0, 0, 0)),
                  pl.BlockSpec((1, ATTN_WIDTH), lambda b, n: (0, 0)),
                  pl.BlockSpec((1, KV_WIDTH), lambda b, n: (0, 0)),
                  pl.BlockSpec((ATTN_WIDTH, ATTN_WIDTH), lambda b, n: (0, 0)),
                  pl.BlockSpec((KV_WIDTH, KV_WIDTH), lambda b, n: (0, 0))],
        out_specs=pl.BlockSpec((None, qb * BLOCK, ATTN_WIDTH), lambda b, n: (b, n, 0)),
        out_shape=jax.ShapeDtypeStruct((bsz, s, ATTN_WIDTH), BF16),
        compiler_params=_cparams("parallel", "arbitrary"),
        name="window_attention",
    )(sink, proj, proj, proj, proj, proj, proj, proj, bias,
      (jnp.tile(q_norm_w, ATTN_HEADS) * (HEAD_DIM ** -0.5)).reshape(1, ATTN_WIDTH),
      jnp.tile(k_norm_w, ATTN_KV_HEADS).reshape(1, KV_WIDTH),
      jnp.asarray(_head_mean_matrix(ATTN_WIDTH), BF16), jnp.asarray(_head_mean_matrix(KV_WIDTH), BF16))


def _cat_proj_kernel(a1_ref, a2_ref, w_ref, x_ref, g_ref, o_ref):
    k1 = a1_ref.shape[1]
    y = jnp.dot(a1_ref[...].astype(BF16), w_ref[0:k1, :], preferred_element_type=F32)
    y = y + jnp.dot(a2_ref[...].astype(BF16), w_ref[k1:, :], preferred_element_type=F32)
    o_ref[...] = x_ref[...] + g_ref[...] * y


def cat_proj_residual(a1, a2, w, x, g, *, ts):
    bsz, s, d = x.shape
    k1, k2 = a1.shape[2], a2.shape[2]
    return pl.pallas_call(
        _cat_proj_kernel,
        grid=(bsz, s // ts),
        in_specs=[pl.BlockSpec((None, ts, k1), lambda b, i: (b, i, 0)),
                  pl.BlockSpec((None, ts, k2), lambda b, i: (b, i, 0)),
                  pl.BlockSpec((k1 + k2, d), lambda b, i: (0, 0)),
                  pl.BlockSpec((None, ts, d), lambda b, i: (b, i, 0)),
                  pl.BlockSpec((None, 1, d), lambda b, i: (b, 0, 0))],
        out_specs=pl.BlockSpec((None, ts, d), lambda b, i: (b, i, 0)),
        out_shape=jax.ShapeDtypeStruct((bsz, s, d), F32),
        compiler_params=_cparams("parallel", "parallel"),
        name="mixer_out_proj",
    )(a1, a2, w, x, g)


def _ffn_kernel(x_ref, nw_ref, sh_ref, sc_ref, g_ref, w1_ref, w3_ref, w2_ref, o_ref, acc_ref, *, tf):
    h = _modnorm(x_ref[...], nw_ref[...], sh_ref[...], sc_ref[...]).astype(BF16)
    dff = w1_ref.shape[1]
    for lo in range(0, dff, tf):
        a = jnp.dot(h, w1_ref[:, lo:lo + tf], preferred_element_type=F32)
        b = jnp.dot(h, w3_ref[:, lo:lo + tf], preferred_element_type=F32)
        t = (_silu(a) * b).astype(BF16)
        contrib = jnp.dot(t, w2_ref[lo:lo + tf, :], preferred_element_type=F32)
        if lo == 0:
            acc_ref[...] = contrib
        else:
            acc_ref[...] += contrib
    o_ref[...] = x_ref[...] + g_ref[...] * acc_ref[...]


def ffn_residual(x, nw, sh, sc, g, w1, w3, w2, *, ts, tf):
    bsz, s, d = x.shape
    dff = w1.shape[1]
    assert dff % tf == 0
    vec = pl.BlockSpec((None, 1, d), lambda b, i: (b, 0, 0))
    return pl.pallas_call(
        functools.partial(_ffn_kernel, tf=tf),
        grid=(bsz, s // ts),
        in_specs=[pl.BlockSpec((None, ts, d), lambda b, i: (b, i, 0)),
                  pl.BlockSpec((1, d), lambda b, i: (0, 0)),
                  vec, vec, vec,
                  _resident((d, dff)), _resident((d, dff)), _resident((dff, d))],
        out_specs=pl.BlockSpec((None, ts, d), lambda b, i: (b, i, 0)),
        out_shape=jax.ShapeDtypeStruct((bsz, s, d), F32),
        scratch_shapes=[pltpu.VMEM((ts, d), F32)],
        compiler_params=_cparams("parallel", "parallel"),
        name="ffn_swiglu",
    )(x, nw.reshape(1, d), sh, sc, g, w1, w3, w2)


CONV_PAD = 8
CONV_ROWS = 256


def _conv_kernel(x_ref, w_ref, b_ref, o_ref, pad_ref):
    s, tc = x_ref.shape
    zeros = jnp.zeros((CONV_PAD, tc), F32)
    pad_ref[0:CONV_PAD, :] = zeros
    pad_ref[CONV_PAD + s:, :] = zeros
    pad_ref[CONV_PAD:CONV_PAD + s, :] = x_ref[...].astype(F32)
    half = CONV_WIDTH // 2
    for r in range(s // CONV_ROWS):
        base = CONV_PAD + r * CONV_ROWS - half
        acc = jnp.zeros((CONV_ROWS, tc), F32) + b_ref[...]
        for kk in range(CONV_WIDTH):
            acc = acc + pad_ref[base + kk:base + kk + CONV_ROWS, :] * w_ref[kk:kk + 1, :]
        o_ref[r * CONV_ROWS:(r + 1) * CONV_ROWS, :] = _silu(acc)


def conv_silu(proj, conv_w, conv_b, *, col0, tc):
    bsz, s, _ = proj.shape
    cdim = conv_w.shape[1]
    cb0 = col0 // tc
    return pl.pallas_call(
        _conv_kernel,
        grid=(bsz, cdim // tc),
        in_specs=[pl.BlockSpec((None, s, tc), lambda b, j: (b, 0, cb0 + j)),
                  pl.BlockSpec((CONV_WIDTH, tc), lambda b, j: (0, j)),
                  pl.BlockSpec((1, tc), lambda b, j: (0, j))],
        out_specs=pl.BlockSpec((None, s, tc), lambda b, j: (b, 0, j)),
        out_shape=jax.ShapeDtypeStruct((bsz, s, cdim), F32),
        scratch_shapes=[pltpu.VMEM((s + 2 * CONV_PAD, tc), F32)],
        compiler_params=_cparams("parallel", "parallel"),
        name="conv_silu",
    )(proj, conv_w, conv_b.reshape(1, cdim))


HALO_ROWS = 16
PROJ_CHUNK = 256


def _proj_conv_kernel(x_ref, xp_ref, xn_ref, nw_ref, sh_ref, sc_ref, wz_ref, wc_ref, cw_ref, cb_ref,
                      z_ref, xs_ref, bc_ref, p_ref):
    i = pl.program_id(1)
    ts = x_ref.shape[0]
    half = CONV_WIDTH // 2

    def hnorm(ref):
        return _modnorm(ref[...], nw_ref[...], sh_ref[...], sc_ref[...])

    h = hnorm(x_ref).astype(BF16)
    h_prev = jnp.where(i == 0, 0.0, hnorm(xp_ref)).astype(BF16)
    h_next = jnp.where(i == pl.num_programs(1) - 1, 0.0, hnorm(xn_ref)).astype(BF16)
    h_ext = jnp.concatenate([h_prev, h, h_next], axis=0)

    n_x = xs_ref.shape[1]
    for lo in range(0, wc_ref.shape[1], PROJ_CHUNK):
        hi = lo + PROJ_CHUNK
        p_ref[...] = jnp.dot(h_ext, wc_ref[:, lo:hi], preferred_element_type=F32)
        acc = jnp.zeros((ts, PROJ_CHUNK), F32) + cb_ref[:, lo:hi]
        for kk in range(CONV_WIDTH):
            start = HALO_ROWS - half + kk
            acc = acc + p_ref[start:start + ts, :] * cw_ref[kk:kk + 1, lo:hi]
        out = _silu(acc)
        if lo < n_x:
            xs_ref[:, lo:hi] = out
        else:
            bc_ref[:, lo - n_x:hi - n_x] = out.astype(bc_ref.dtype)
    for lo in range(0, wz_ref.shape[1], PROJ_CHUNK):
        hi = lo + PROJ_CHUNK
        z_ref[:, lo:hi] = jnp.dot(h, wz_ref[:, lo:hi], preferred_element_type=F32).astype(z_ref.dtype)


def ssm_in_proj_conv(x, nw, sh, sc, w_z, w_xbc, conv_w, conv_b, *, n_x, ts):
    bsz, s, d = x.shape
    dz = w_z.shape[1]
    dc = w_xbc.shape[1]
    assert ts % HALO_ROWS == 0 and n_x % PROJ_CHUNK == 0 and dc % PROJ_CHUNK == 0 and dz % PROJ_CHUNK == 0
    r = ts // HALO_ROWS
    last = s // HALO_ROWS - 1
    vec = pl.BlockSpec((None, 1, d), lambda b, i: (b, 0, 0))
    return pl.pallas_call(
        _proj_conv_kernel,
        grid=(bsz, s // ts),
        in_specs=[pl.BlockSpec((None, ts, d), lambda b, i: (b, i, 0)),
                  pl.BlockSpec((None, HALO_ROWS, d), lambda b, i: (b, jnp.maximum(i * r - 1, 0), 0)),
                  pl.BlockSpec((None, HALO_ROWS, d), lambda b, i: (b, jnp.minimum((i + 1) * r, last), 0)),
                  pl.BlockSpec((1, d), lambda b, i: (0, 0)), vec, vec,
                  _resident((d, dz)), _resident((d, dc)),
                  _resident((CONV_WIDTH, dc)), _resident((1, dc))],
        out_specs=[pl.BlockSpec((None, ts, dz), lambda b, i: (b, i, 0)),
                   pl.BlockSpec((None, ts, n_x), lambda b, i: (b, i, 0)),
                   pl.BlockSpec((None, ts, dc - n_x), lambda b, i: (b, i, 0))],
        out_shape=[jax.ShapeDtypeStruct((bsz, s, dz), BF16),
                   jax.ShapeDtypeStruct((bsz, s, n_x), F32),
                   jax.ShapeDtypeStruct((bsz, s, dc - n_x), BF16)],
        scratch_shapes=[pltpu.VMEM((ts + 2 * HALO_ROWS, PROJ_CHUNK), F32)],
        compiler_params=_cparams("parallel", "parallel"),
        name="odd_in_proj_conv",
    )(x, x, x, nw.reshape(1, d), sh, sc, w_z, w_xbc, conv_w, conv_b.reshape(1, dc))


def _softplus(x):
    return jnp.maximum(x, 0.0) + jnp.log(1.0 + jnp.exp(-jnp.abs(x)))


def _expand_matrix(n_in, width):
    e = np.zeros((n_in, n_in * width), np.float32)
    for h in range(n_in):
        e[h, h * width:(h + 1) * width] = 1.0
    return e


LOG2E = 1.4426950408889634
DECAY_SLOTS = 12


def _decay_placement(nh):
    place = np.zeros((3, 2 * nh, 3 * 128), np.float32)
    const = np.zeros((1, 3 * 128), np.float32)
    for h in range(nh):
        for part in range(3):
            place[part, h, part * nh + h] = 1.0
            place[part, nh + h, (6 + part) * nh + h] = -1.0
            place[part, h, 128 + (3 + part) * nh + h] = -1.0
            place[part, nh + h, 256 + (9 + part) * nh + h] = 1.0
            const[0, (3 + part) * nh + h] = 1.0
            const[0, (9 + part) * nh + h] = 1.0
            const[0, 128 + part * nh + h] = 1.0
            const[0, 256 + (6 + part) * nh + h] = 1.0
    return place, const


def _split3(v):
    hi = v.astype(BF16)
    r = v - hi.astype(F32)
    mid = r.astype(BF16)
    lo = (r - mid.astype(F32)).astype(BF16)
    return hi, mid, lo


def _ssd_kernel(x_ref, b_ref, c_ref, dtc_f_ref, dtc_b_ref, dtr_f_ref, dtr_b_ref,
                pc_f_ref, pc_b_ref, pr_f_ref, pr_b_ref, dx_ref,
                tri_ref, place_ref, pconst_ref, e64_ref, y_ref,
                ac_ref, dc_ref, dr_ref, pq_ref, xdf_ref, xdb_ref, eif_ref, er_ref, decf_ref, decb_ref,
                hf_ref, hb_ref):
    s = x_ref.shape[0]
    q = SSD_CHUNK
    nh = pc_f_ref.shape[1]
    nc = s // q
    hd = SSM_HEAD_DIM

    def col_params(raw_ref, p_ref):
        dt = _softplus(raw_ref[...] + p_ref[0:1, :])
        return dt, (-LOG2E) * jnp.exp(p_ref[1:2, :]) * dt

    dcf, acf = col_params(dtc_f_ref, pc_f_ref)
    dcb, acb = col_params(dtc_b_ref, pc_b_ref)
    dc_ref[:, 0:nh] = dcf
    dc_ref[:, nh:] = dcb
    ac_ref[:, 0:nh] = acf
    ac_ref[:, nh:] = acb
    dr_ref[0:nh, :] = _softplus(dtr_f_ref[...] + pr_f_ref[:, 0:1])
    dr_ref[nh:, :] = _softplus(dtr_b_ref[...] + pr_b_ref[:, 0:1])

    li = lax.broadcasted_iota(jnp.int32, (q, q), 0)
    si = lax.broadcasted_iota(jnp.int32, (q, q), 1)
    lower = li >= si
    upper = li <= si
    slot_head = lax.broadcasted_iota(jnp.int32, (1, 128), 1) % nh
    pair_lo = lax.broadcasted_iota(jnp.int32, (1, 2 * hd), 1) < hd

    def bdot(a, b):
        return jnp.dot(a, b, preferred_element_type=F32)

    def chunk_cumsums(sl):
        a_col = ac_ref[sl, :]
        tri = tri_ref[...]
        hi, mid, lo = _split3(a_col)
        return a_col, bdot(tri, hi) + bdot(tri, mid) + bdot(tri, lo)

    def expand64(v, parts=2):
        e = e64_ref[...]
        hi = v.astype(BF16)
        out = bdot(hi, e)
        if parts == 2:
            out = out + bdot((v - hi.astype(F32)).astype(BF16), e)
        return out

    hf_ref[...] = jnp.zeros_like(hf_ref)
    hb_ref[...] = jnp.zeros_like(hb_ref)

    def prep_body(c, carry):
        sl = pl.ds(pl.multiple_of(c * q, q), q)
        a_col, i_col = chunk_cumsums(sl)
        d_col = dc_ref[sl, :]
        if_col = i_col[:, 0:nh]
        tot_f = i_col[q - 1:q, 0:nh]
        tot_b = i_col[q - 1:q, nh:]
        ie = i_col - jnp.where(lax.broadcasted_iota(jnp.int32, (1, 2 * nh), 1) < nh, 0.0, a_col)
        eb_col = ie[:, nh:]
        parts = _split3(ie)
        placed = pconst_ref[...]
        for part in range(3):
            placed = placed + bdot(parts[part], place_ref[part])
        pq_ref[sl, :] = placed.astype(BF16)
        xc = x_ref[sl, :]
        xdf_ref[sl, :] = (xc * expand64(jnp.exp2(tot_f - if_col) * d_col[:, 0:nh], parts=1)).astype(BF16)
        xdb_ref[sl, :] = (xc * expand64(jnp.exp2(eb_col) * d_col[:, nh:], parts=1)).astype(BF16)
        eif_ref[sl, :] = expand64(jnp.exp2(if_col))
        er_ref[sl, :] = expand64(jnp.exp2(tot_b - eb_col))
        dec = expand64(jnp.exp2(jnp.concatenate([tot_f, tot_b], axis=0)))
        decf_ref[c] = jnp.broadcast_to(dec[0:1, :], decf_ref.shape[1:])
        decb_ref[c] = jnp.broadcast_to(dec[1:2, :], decb_ref.shape[1:])
        return carry

    lax.fori_loop(0, nc, prep_body, 0, unroll=4)

    def fwd_body(c, carry):
        sl = pl.ds(pl.multiple_of(c * q, q), q)
        xc = x_ref[sl, :]
        bc = b_ref[sl, :].astype(BF16)
        cc = c_ref[sl, :].astype(BF16)
        d_row = dr_ref[:, sl]
        placed = pq_ref[sl, :]
        p_all = placed[:, 0:128]
        q_cat = jnp.concatenate([placed[:, 128:256], placed[:, 256:384]], axis=0)

        cb = lax.dot_general(cc, bc, (((1,), (1,)), ((), ())), preferred_element_type=F32)
        xb = xc.astype(BF16)
        ys = []
        for h0 in range(0, nh, 2):
            ms = []
            for h in (h0, h0 + 1):
                ph = jnp.where(slot_head == h, p_all, jnp.zeros_like(p_all))
                g2 = lax.dot_general(ph, q_cat, (((1,), (1,)), ((), ())), preferred_element_type=F32)
                arg = jnp.where(lower, g2[:, 0:q], g2[:, q:])
                wgt = (jnp.where(lower, d_row[h:h + 1, :], 0.0)
                       + jnp.where(upper, d_row[nh + h:nh + h + 1, :], 0.0))
                ms.append((cb * jnp.exp2(arg) * wgt).astype(BF16))
            xp = xb[:, h0 * hd:(h0 + 2) * hd]
            zero = jnp.zeros_like(xp)
            rhs = jnp.concatenate([jnp.where(pair_lo, xp, zero), jnp.where(pair_lo, zero, xp)], axis=0)
            ys.append(bdot(jnp.concatenate(ms, axis=1), rhs))
        y = jnp.concatenate(ys, axis=1) + dx_ref[...] * xc

        states = lax.dot_general(bc, xdf_ref[sl, :], (((0,), (0,)), ((), ())),
                                 preferred_element_type=F32)
        h_prev = hf_ref[...]
        y = y + bdot(cc, h_prev.astype(BF16)) * eif_ref[sl, :]
        hf_ref[...] = h_prev * decf_ref[c][0:1, :] + states
        y_ref[sl, :] = y
        return carry

    lax.fori_loop(0, nc, fwd_body, 0, unroll=2)

    def bwd_body(t, carry):
        c = nc - 1 - t
        sl = pl.ds(pl.multiple_of(c * q, q), q)
        bc = b_ref[sl, :].astype(BF16)
        cc = c_ref[sl, :].astype(BF16)
        states = lax.dot_general(bc, xdb_ref[sl, :], (((0,), (0,)), ((), ())),
                                 preferred_element_type=F32)
        h_prev = hb_ref[...]
        y_ref[sl, :] += bdot(cc, h_prev.astype(BF16)) * er_ref[sl, :]
        hb_ref[...] = h_prev * decb_ref[c][0:1, :] + states
        return carry

    lax.fori_loop(0, nc, bwd_body, 0, unroll=4)


def _bf16_parts3(v):
    hi = v.astype(BF16).astype(F32)
    r = v - hi
    mid = r.astype(BF16).astype(F32)
    lo = (r - mid).astype(BF16).astype(F32)
    return hi, mid, lo


def _ssd_t_kernel(x_ref, b_ref, c_ref, dt_f_ref, dt_b_ref, p_f_ref, p_b_ref, dx_ref, trio_ref, y_ref,
                  ar_ref, dr_ref, sc_ref, pq_ref, qt_ref, xtb_ref, xdf_ref, xdb_ref, yt_ref, hf_ref, hb_ref):
    s = x_ref.shape[0]
    q = SSD_CHUNK
    nh = p_f_ref.shape[0]
    nc = s // q
    hd = SSM_HEAD_DIM

    def row_params(raw_ref, p_ref):
        dt = _softplus(raw_ref[...] + p_ref[:, 0:1])
        return dt, (-LOG2E) * jnp.exp(p_ref[:, 1:2]) * dt

    dtf, af = row_params(dt_f_ref, p_f_ref)
    dtb, ab = row_params(dt_b_ref, p_b_ref)
    dr_ref[0:nh, :] = dtf
    dr_ref[nh:, :] = dtb
    ar_ref[0:nh, :] = af
    ar_ref[nh:, :] = ab

    li = lax.broadcasted_iota(jnp.int32, (q, q), 0)
    si = lax.broadcasted_iota(jnp.int32, (q, q), 1)
    lower = li >= si
    upper = li <= si
    slot_head = lax.broadcasted_iota(jnp.int32, (1, q), 1) % nh

    def bdot(a, b):
        return jnp.dot(a, b, preferred_element_type=F32)

    def ntdot(a, b):
        return lax.dot_general(a, b, (((1,), (1,)), ((), ())), preferred_element_type=F32)

    def head_rows(v):
        return jnp.concatenate([jnp.broadcast_to(v[h:h + 1, :], (hd, q)) for h in range(nh)], axis=0)

    ones = jnp.ones((nh, q), F32)
    zeros = jnp.zeros((nh, q), F32)

    def prep_body(c, carry):
        sl = pl.ds(pl.multiple_of(c * q, q), q)
        a_row = ar_ref[:, sl]
        d_row = dr_ref[:, sl]
        parts = jnp.concatenate([p.astype(BF16) for p in _bf16_parts3(a_row)], axis=0)
        cs = bdot(parts, trio_ref[...])
        cs = cs[0:2 * nh] + cs[2 * nh:4 * nh] + cs[4 * nh:6 * nh]
        i_f = cs[0:nh, 0:q]
        e_b = cs[nh:, 0:q] - a_row[nh:]
        tot_f = cs[0:nh, q:]
        tot_b = cs[nh:, q:]
        ih, im, il = _bf16_parts3(i_f)
        eh, em, el = _bf16_parts3(e_b)
        pad = [zeros] * (q // nh - DECAY_SLOTS)
        p_t = jnp.concatenate([ih, im, il, ones, ones, ones, -eh, -em, -el, ones, ones, ones] + pad, axis=0)
        qf_t = jnp.concatenate([ones, ones, ones, -ih, -im, -il] + [zeros] * 6 + pad, axis=0)
        qb_t = jnp.concatenate([zeros] * 6 + [ones, ones, ones, eh, em, el] + pad, axis=0)
        pq_ref[sl, :] = p_t.T.astype(BF16)
        qt_ref[c] = jnp.concatenate([qf_t, qb_t], axis=1).astype(BF16)
        sc_ref[0 * nh:1 * nh, sl] = jnp.exp2(i_f)
        sc_ref[1 * nh:2 * nh, sl] = jnp.exp2(tot_b - e_b)
        sc_ref[2 * nh:3 * nh, sl] = jnp.exp2(tot_f)
        sc_ref[3 * nh:4 * nh, sl] = jnp.exp2(tot_b)
        xt = x_ref[sl, :].T
        xtb_ref[:, sl] = xt.astype(BF16)
        xdf_ref[:, sl] = (xt * head_rows(jnp.exp2(tot_f - i_f) * d_row[0:nh])).astype(BF16)
        xdb_ref[:, sl] = (xt * head_rows(jnp.exp2(e_b) * d_row[nh:])).astype(BF16)
        return carry

    lax.fori_loop(0, nc, prep_body, 0, unroll=2)

    hf_ref[...] = jnp.zeros_like(hf_ref)
    hb_ref[...] = jnp.zeros_like(hb_ref)
    zero_half = jnp.zeros((hd, q), BF16)

    def fwd_body(c, carry):
        sl = pl.ds(pl.multiple_of(c * q, q), q)
        bc = b_ref[sl, :].astype(BF16)
        cc = c_ref[sl, :].astype(BF16)
        d_row = dr_ref[:, sl]
        p_all = pq_ref[sl, :]
        q_t = qt_ref[c]
        xtb = xtb_ref[:, sl]
        cb = ntdot(cc, bc)
        g2s = [bdot(jnp.where(slot_head == h, p_all, jnp.zeros_like(p_all)), q_t) for h in range(nh)]
        ms = []
        for h in range(nh):
            arg = jnp.where(lower, g2s[h][:, 0:q], g2s[h][:, q:])
            wgt = (jnp.where(lower, d_row[h:h + 1, :], 0.0)
                   + jnp.where(upper, d_row[nh + h:nh + h + 1, :], 0.0))
            ms.append((cb * jnp.exp2(arg) * wgt).astype(BF16))
        yd = []
        for h0 in range(0, nh, 2):
            lhs = jnp.concatenate(
                [jnp.concatenate([xtb[h0 * hd:(h0 + 1) * hd], zero_half], axis=0),
                 jnp.concatenate([zero_half, xtb[(h0 + 1) * hd:(h0 + 2) * hd]], axis=0)], axis=1)
            yd.append(ntdot(lhs, jnp.concatenate(ms[h0:h0 + 2], axis=1)))
        states = bdot(xdf_ref[:, sl], bc)
        h_prev = hf_ref[...]
        y_off = ntdot(h_prev.astype(BF16), cc) * head_rows(sc_ref[0 * nh:1 * nh, sl])
        hf_ref[...] = h_prev * head_rows(sc_ref[2 * nh:3 * nh, sl]) + states
        yt_ref[:, sl] = jnp.concatenate(yd, axis=0) + y_off
        return carry

    lax.fori_loop(0, nc, fwd_body, 0, unroll=4)

    def bwd_body(t, carry):
        c = nc - 1 - t
        sl = pl.ds(pl.multiple_of(c * q, q), q)
        bc = b_ref[sl, :].astype(BF16)
        cc = c_ref[sl, :].astype(BF16)
        states = bdot(xdb_ref[:, sl], bc)
        h_prev = hb_ref[...]
        y_off = ntdot(h_prev.astype(BF16), cc) * head_rows(sc_ref[1 * nh:2 * nh, sl])
        hb_ref[...] = h_prev * head_rows(sc_ref[3 * nh:4 * nh, sl]) + states
        y_ref[sl, :] = ((yt_ref[:, sl] + y_off).T + dx_ref[...] * x_ref[sl, :]).astype(y_ref.dtype)
        return carry

    lax.fori_loop(0, nc, bwd_body, 0, unroll=4)


def ssd_scan_bidir(xs, bc, dt, dt_bias_f, dt_bias_b, a_log_f, a_log_b, d_skip):
    bsz, s, _ = xs.shape
    nheads = dt.shape[2] // 2
    nh = nheads // SSM_GROUPS
    gw = nh * SSM_HEAD_DIM
    d_inner = nheads * SSM_HEAD_DIM
    q = SSD_CHUNK
    dt_row = jnp.transpose(dt.reshape(bsz, s, 2 * SSM_GROUPS, nh), (0, 2, 3, 1))
    prm = jnp.stack([jnp.concatenate([dt_bias_f, dt_bias_b]), jnp.concatenate([a_log_f, a_log_b])])
    p_row = jnp.transpose(prm.reshape(2, 2 * SSM_GROUPS, nh), (1, 2, 0))
    dx = jnp.repeat(d_skip, SSM_HEAD_DIM).reshape(SSM_GROUPS, 1, gw)
    assert DECAY_SLOTS * nh <= q and q % nh == 0 and D_STATE == q
    trio = jnp.asarray(np.concatenate([np.triu(np.ones((q, q), np.float32)), np.ones((q, q), np.float32)],
                                      axis=1), BF16)
    bcol = d_inner // D_STATE
    G = SSM_GROUPS
    nc = s // q

    return pl.pallas_call(
        _ssd_t_kernel,
        grid=(bsz, SSM_GROUPS),
        in_specs=[pl.BlockSpec((None, s, gw), lambda b, g: (b, 0, g)),
                  pl.BlockSpec((None, s, D_STATE), lambda b, g: (b, 0, g)),
                  pl.BlockSpec((None, s, D_STATE), lambda b, g: (b, 0, G + g)),
                  pl.BlockSpec((None, None, nh, s), lambda b, g: (b, g, 0, 0)),
                  pl.BlockSpec((None, None, nh, s), lambda b, g: (b, G + g, 0, 0)),
                  pl.BlockSpec((None, nh, 2), lambda b, g: (g, 0, 0)),
                  pl.BlockSpec((None, nh, 2), lambda b, g: (G + g, 0, 0)),
                  pl.BlockSpec((None, 1, gw), lambda b, g: (g, 0, 0)),
                  pl.BlockSpec((q, 2 * q), lambda b, g: (0, 0))],
        out_specs=pl.BlockSpec((None, s, gw), lambda b, g: (b, 0, g)),
        out_shape=jax.ShapeDtypeStruct((bsz, s, d_inner), BF16),
        scratch_shapes=[pltpu.VMEM((2 * nh, s), F32), pltpu.VMEM((2 * nh, s), F32),
                        pltpu.VMEM((4 * nh, s), F32),
                        pltpu.VMEM((s, q), BF16), pltpu.VMEM((nc, q, 2 * q), BF16),
                        pltpu.VMEM((gw, s), BF16), pltpu.VMEM((gw, s), BF16), pltpu.VMEM((gw, s), BF16),
                        pltpu.VMEM((gw, s), F32),
                        pltpu.VMEM((gw, D_STATE), F32), pltpu.VMEM((gw, D_STATE), F32)],
        compiler_params=_cparams("parallel", "parallel"),
        name="ssd_scan",
    )(xs, bc, bc, dt_row, dt_row, p_row, p_row, dx, trio)


def _gated_proj_kernel(y_ref, z_ref, gw_ref, w_ref, x_ref, g_ref, o_ref, acc_ref):
    k = y_ref.shape[1]
    ss = jnp.zeros((y_ref.shape[0], 1), F32)
    for lo in range(0, k, PROJ_CHUNK):
        hi = lo + PROJ_CHUNK
        t = y_ref[:, lo:hi].astype(F32) * _silu(z_ref[:, lo:hi].astype(F32))
        ss = ss + jnp.sum(t * t, axis=-1, keepdims=True)
        contrib = jnp.dot((t * gw_ref[:, lo:hi]).astype(BF16), w_ref[lo:hi, :], preferred_element_type=F32)
        if lo == 0:
            acc_ref[...] = contrib
        else:
            acc_ref[...] += contrib
    o_ref[...] = x_ref[...] + g_ref[...] * (acc_ref[...] * lax.rsqrt(ss * (1.0 / k) + EPS))


def gated_proj_residual(y, zproj, gw, w, x, g, *, ts):
    bsz, s, d = x.shape
    k = y.shape[2]
    return pl.pallas_call(
        _gated_proj_kernel,
        grid=(bsz, s // ts),
        in_specs=[pl.BlockSpec((None, ts, k), lambda b, i: (b, i, 0)),
                  pl.BlockSpec((None, ts, k), lambda b, i: (b, i, 0)),
                  pl.BlockSpec((1, k), lambda b, i: (0, 0)),
                  _resident((k, d)),
                  pl.BlockSpec((None, ts, d), lambda b, i: (b, i, 0)),
                  pl.BlockSpec((None, 1, d), lambda b, i: (b, 0, 0))],
        out_specs=pl.BlockSpec((None, ts, d), lambda b, i: (b, i, 0)),
        out_shape=jax.ShapeDtypeStruct((bsz, s, d), F32),
        scratch_shapes=[pltpu.VMEM((ts, d), F32)],
        compiler_params=_cparams("parallel", "parallel"),
        name="ssd_out_proj",
    )(y, zproj, gw.reshape(1, k), w, x, g)


def _router_kernel(x_ref, nw_ref, sh_ref, sc_ref, rw_ref, rb_ref, lt_ref,
                   eidx_ref, rank_ref, wgt_ref, cnt_ref):
    h = _modnorm(x_ref[...], nw_ref[...], sh_ref[...], sc_ref[...])
    rw = rw_ref[...]
    h_hi = h.astype(BF16)
    h_lo = (h - h_hi.astype(F32)).astype(BF16)
    rw_hi = rw.astype(BF16)
    rw_lo = (rw - rw_hi.astype(F32)).astype(BF16)
    logits = (jnp.dot(h_hi, rw_hi, preferred_element_type=F32)
              + (jnp.dot(h_lo, rw_hi, preferred_element_type=F32)
                 + jnp.dot(h_hi, rw_lo, preferred_element_type=F32))) + rb_ref[...]
    ts, ne = logits.shape
    eid = lax.broadcasted_iota(jnp.int32, (ts, ne), 1)
    m1 = jnp.max(logits, axis=-1, keepdims=True)
    i1 = jnp.min(jnp.where(logits == m1, eid, ne), axis=-1, keepdims=True)
    rest = jnp.where(eid == i1, -jnp.inf, logits)
    m2 = jnp.max(rest, axis=-1, keepdims=True)
    i2 = jnp.min(jnp.where(rest == m2, eid, ne), axis=-1, keepdims=True)
    e2 = jnp.exp(m2 - m1)
    w1 = 1.0 / (1.0 + e2)
    w2 = e2 / (1.0 + e2)
    oh1 = (eid == i1).astype(F32)
    oh2 = (eid == i2).astype(F32)
    chosen = oh1 + oh2
    incl = jnp.dot(lt_ref[...], chosen.astype(BF16), preferred_element_type=F32)
    before = incl - chosen
    r1 = jnp.sum(oh1 * before, axis=-1, keepdims=True)
    r2 = jnp.sum(oh2 * before, axis=-1, keepdims=True)
    eidx_ref[...] = jnp.concatenate([i1, i2], axis=1)
    rank_ref[...] = jnp.concatenate([r1, r2], axis=1).astype(jnp.int32)
    wgt_ref[...] = jnp.concatenate([w1, w2], axis=1)
    cnt_ref[...] = incl[ts - 1:ts, :].astype(jnp.int32)


def moe_route(x, nw, sh, sc, router_w, router_b, *, ts):
    bsz, s, d = x.shape
    ne = router_w.shape[1]
    nt = s // ts
    lt = jnp.asarray(np.tril(np.ones((ts, ts), np.float32)), BF16)
    tok = lambda dt: jax.ShapeDtypeStruct((bsz * s, TOP_K), dt)
    tok_spec = pl.BlockSpec((ts, TOP_K), lambda b, i: (b * nt + i, 0))
    return pl.pallas_call(
        _router_kernel,
        grid=(bsz, nt),
        in_specs=[pl.BlockSpec((None, ts, d), lambda b, i: (b, i, 0)),
                  pl.BlockSpec((1, d), lambda b, i: (0, 0)),
                  pl.BlockSpec((None, 1, d), lambda b, i: (b, 0, 0)),
                  pl.BlockSpec((None, 1, d), lambda b, i: (b, 0, 0)),
                  pl.BlockSpec((d, ne), lambda b, i: (0, 0)),
                  pl.BlockSpec((1, ne), lambda b, i: (0, 0)),
                  pl.BlockSpec((ts, ts), lambda b, i: (0, 0))],
        out_specs=[tok_spec, tok_spec, tok_spec,
                   pl.BlockSpec((None, 1, ne), lambda b, i: (b * nt + i, 0, 0))],
        out_shape=[tok(jnp.int32), tok(jnp.int32), tok(F32),
                   jax.ShapeDtypeStruct((bsz * nt, 1, ne), jnp.int32)],
        compiler_params=_cparams("parallel", "parallel"),
        name="moe_router",
    )(x, nw.reshape(1, d), sh, sc, router_w, router_b.reshape(1, ne), lt)


SEG_ROWS = 16
SEG_FIELDS = 3


def _segment_copies(seg_ref, tile, n_experts, make_copy, *, wait):
    for e in range(n_experts):
        base = (tile * n_experts + e) * SEG_FIELDS
        local0 = seg_ref[base]
        global0 = seg_ref[base + 1]

        def body(i, carry, local0=local0, global0=global0):
            cp = make_copy(pl.multiple_of(local0 + i * SEG_ROWS, SEG_ROWS),
                           pl.multiple_of(global0 + i * SEG_ROWS, SEG_ROWS))
            if wait:
                cp.wait()
            else:
                cp.start()
            return carry

        lax.fori_loop(0, seg_ref[base + 2], body, 0)


def _dispatch_kernel(seg_ref, x_ref, nw_ref, sh_ref, sc_ref, ld_ref, hs_in_ref, hs_ref, buf_ref, sem):
    del hs_in_ref
    tt = x_ref.shape[0]
    lc = buf_ref.shape[1]
    ne = N_EXPERTS
    tile = pl.program_id(0) * pl.num_programs(1) + pl.program_id(1)
    last = pl.num_programs(0) * pl.num_programs(1) - 1
    slot = tile % 2

    def copies(t, sl, wait):
        def make_copy(lo, go):
            return pltpu.make_async_copy(buf_ref.at[sl, pl.ds(lo, SEG_ROWS), :],
                                         hs_ref.at[pl.ds(go, SEG_ROWS), :], sem.at[sl])
        _segment_copies(seg_ref, t, ne, make_copy, wait=wait)

    h = _modnorm(x_ref[...], nw_ref[...], sh_ref[...], sc_ref[...]).astype(BF16)
    ld = ld_ref[...]
    rows = lax.broadcasted_iota(jnp.int32, (lc, tt), 0)
    perm = jnp.where(rows == ld[0:1, :], 1.0, jnp.where(rows == ld[1:2, :], 1.0, 0.0)).astype(BF16)
    buf_ref[slot] = jnp.dot(perm, h, preferred_element_type=F32).astype(BF16)
    copies(tile, slot, wait=False)

    @pl.when(tile > 0)
    def _():
        copies(tile - 1, 1 - slot, wait=True)

    @pl.when(tile == last)
    def _():
        copies(tile, slot, wait=True)


def moe_dispatch(x, nw, sh, sc, seg, ldest_rows, n_rows, *, tt, lc):
    bsz, s, d = x.shape
    nt = s // tt
    hs0 = jnp.zeros((n_rows, d), BF16)
    grid_spec = pltpu.PrefetchScalarGridSpec(
        num_scalar_prefetch=1,
        grid=(bsz, nt),
        in_specs=[pl.BlockSpec((None, tt, d), lambda b, i, sref: (b, i, 0)),
                  pl.BlockSpec((1, d), lambda b, i, sref: (0, 0)),
                  pl.BlockSpec((None, 1, d), lambda b, i, sref: (b, 0, 0)),
                  pl.BlockSpec((None, 1, d), lambda b, i, sref: (b, 0, 0)),
                  pl.BlockSpec((TOP_K, tt), lambda b, i, sref: (0, b * nt + i)),
                  pl.BlockSpec(memory_space=pl.ANY)],
        out_specs=pl.BlockSpec(memory_space=pl.ANY),
        scratch_shapes=[pltpu.VMEM((2, lc, d), BF16), pltpu.SemaphoreType.DMA((2,))],
    )
    return pl.pallas_call(
        _dispatch_kernel,
        grid_spec=grid_spec,
        out_shape=jax.ShapeDtypeStruct((n_rows, d), BF16),
        input_output_aliases={6: 0},
        compiler_params=_cparams("arbitrary", "arbitrary"),
        name="moe_dispatch",
    )(seg, x, nw.reshape(1, d), sh, sc, ldest_rows, hs0)


def _moe_kernel(te_ref, nu_ref, hs_ref, w1_ref, w3_ref, w2_ref, o_ref, acc_ref):
    i = pl.program_id(0)
    f = pl.program_id(1)

    @pl.when(i < nu_ref[0])
    def _():
        @pl.when(f == 0)
        def _():
            acc_ref[...] = jnp.zeros_like(acc_ref)

        h = hs_ref[...]
        for lo in range(0, w1_ref.shape[1], MOE_HIDDEN_CHUNK):
            hi = lo + MOE_HIDDEN_CHUNK
            a = jnp.dot(h, w1_ref[:, lo:hi], preferred_element_type=F32)
            b = jnp.dot(h, w3_ref[:, lo:hi], preferred_element_type=F32)
            t = (_silu(a) * b).astype(BF16)
            acc_ref[...] += jnp.dot(t, w2_ref[lo:hi, :], preferred_element_type=F32)

        @pl.when(f == pl.num_programs(1) - 1)
        def _():
            o_ref[...] = acc_ref[...].astype(o_ref.dtype)

    @pl.when((i >= nu_ref[0]) & (f == 0))
    def _():
        o_ref[...] = jnp.zeros_like(o_ref)


def moe_experts(hs, tile_expert, n_used, w1, w3, w2, *, tm, tf):
    n_rows, d = hs.shape
    dff = w1.shape[2]
    nf = dff // tf
    n_tiles = n_rows // tm

    def last_used(i, nu):
        return jnp.maximum(jnp.minimum(i, nu[0] - 1), 0)

    def row_map(i, f, te, nu):
        return (last_used(i, nu), 0)

    def hidden_block(i, f, nu):
        t = last_used(i, nu)
        step = jnp.where(i < nu[0], f, nf - 1)
        return jnp.where(t % 2 == 0, step, nf - 1 - step)

    def w_in_map(i, f, te, nu):
        return (te[last_used(i, nu)], 0, hidden_block(i, f, nu))

    def w_out_map(i, f, te, nu):
        return (te[last_used(i, nu)], hidden_block(i, f, nu), 0)

    grid_spec = pltpu.PrefetchScalarGridSpec(
        num_scalar_prefetch=2,
        grid=(n_tiles, nf),
        in_specs=[pl.BlockSpec((tm, d), row_map),
                  pl.BlockSpec((None, d, tf), w_in_map),
                  pl.BlockSpec((None, d, tf), w_in_map),
                  pl.BlockSpec((None, tf, d), w_out_map)],
        out_specs=pl.BlockSpec((tm, d), lambda i, f, te, nu: (i, 0)),
        scratch_shapes=[pltpu.VMEM((tm, d), F32)],
    )
    return pl.pallas_call(
        _moe_kernel,
        grid_spec=grid_spec,
        out_shape=jax.ShapeDtypeStruct((n_rows, d), BF16),
        compiler_params=_cparams("arbitrary", "arbitrary"),
        name="moe_experts",
    )(tile_expert, n_used, hs, w1, w3, w2)


def _combine_kernel(seg_ref, ys_ref, x_ref, g_ref, wgt_ref, ld_ref, o_ref, buf_ref, sem):
    tt = x_ref.shape[0]
    lc = buf_ref.shape[1]
    ne = N_EXPERTS
    tile = pl.program_id(0) * pl.num_programs(1) + pl.program_id(1)
    last = pl.num_programs(0) * pl.num_programs(1) - 1
    slot = tile % 2

    def copies(t, sl, wait):
        def make_copy(lo, go):
            return pltpu.make_async_copy(ys_ref.at[pl.ds(go, SEG_ROWS), :],
                                         buf_ref.at[sl, pl.ds(lo, SEG_ROWS), :], sem.at[sl])
        _segment_copies(seg_ref, t, ne, make_copy, wait=wait)

    @pl.when(tile == 0)
    def _():
        buf_ref[...] = jnp.zeros_like(buf_ref)
        copies(tile, slot, wait=False)

    @pl.when(tile < last)
    def _():
        copies(tile + 1, 1 - slot, wait=False)

    copies(tile, slot, wait=True)

    ld = ld_ref[...]
    cols = lax.broadcasted_iota(jnp.int32, (tt, lc), 1)
    pick = jnp.concatenate([jnp.where(cols == ld[:, k:k + 1], 1.0, 0.0) for k in range(TOP_K)],
                           axis=0).astype(BF16)
    z = jnp.dot(pick, buf_ref[slot], preferred_element_type=F32)
    w = wgt_ref[...]
    mix = w[:, 0:1] * z[0:tt] + w[:, 1:2] * z[tt:]
    o_ref[...] = x_ref[...] + g_ref[...] * mix


def moe_combine(ys, seg, ldest, wgt, x, g, *, tt, lc):
    bsz, s, d = x.shape
    nt = s // tt
    tok_spec = pl.BlockSpec((tt, TOP_K), lambda b, i, sref: (b * nt + i, 0))
    grid_spec = pltpu.PrefetchScalarGridSpec(
        num_scalar_prefetch=1,
        grid=(bsz, nt),
        in_specs=[pl.BlockSpec(memory_space=pl.ANY),
                  pl.BlockSpec((None, tt, d), lambda b, i, sref: (b, i, 0)),
                  pl.BlockSpec((None, 1, d), lambda b, i, sref: (b, 0, 0)),
                  tok_spec, tok_spec],
        out_specs=pl.BlockSpec((None, tt, d), lambda b, i, sref: (b, i, 0)),
        scratch_shapes=[pltpu.VMEM((2, lc, d), BF16), pltpu.SemaphoreType.DMA((2,))],
    )
    return pl.pallas_call(
        _combine_kernel,
        grid_spec=grid_spec,
        out_shape=jax.ShapeDtypeStruct((bsz, s, d), F32),
        compiler_params=_cparams("arbitrary", "arbitrary"),
        name="moe_combine",
    )(seg, ys, x, g, wgt, ldest)


def _round_up(v, m):
    return ((v + m - 1) // m) * m


def moe_residual(x, nw, sh, sc, g, router_w, router_b, w1, w3, w2, *, tm=MOE_TILE_ROWS):
    bsz, s, d = x.shape
    n_tok = bsz * s
    ne = router_w.shape[1]
    tt = min(512, s)
    n_tt = n_tok // tt
    lc = _round_up(TOP_K * tt + ne * SEG_ROWS, 128)
    eidx, rank, wgt, cnt = moe_route(x, nw, sh, sc, router_w, router_b, ts=tt)
    seg_len = _round_up(cnt.reshape(n_tt, ne), SEG_ROWS)
    local_start = jnp.cumsum(seg_len, axis=1) - seg_len
    padded = _round_up(jnp.sum(seg_len, axis=0), tm)
    ends = jnp.cumsum(padded)
    global_start = (ends - padded)[None, :] + jnp.cumsum(seg_len, axis=0) - seg_len
    seg = jnp.stack([local_start, global_start, seg_len // SEG_ROWS], axis=-1).reshape(-1).astype(jnp.int32)
    onehot = eidx[:, :, None] == jnp.arange(ne, dtype=jnp.int32)
    start_tok = jnp.repeat(local_start, tt, axis=0)[:, None, :]
    ldest = (jnp.sum(jnp.where(onehot, start_tok, 0), axis=-1) + rank).astype(jnp.int32)
    n_rows = _round_up(n_tok * TOP_K + n_tt * ne * SEG_ROWS + ne * tm, tm)
    n_tiles = n_rows // tm
    tile_start = jnp.arange(n_tiles, dtype=jnp.int32) * tm
    tile_expert = jnp.minimum(jnp.sum(tile_start[:, None] >= ends[None, :], axis=1), ne - 1).astype(jnp.int32)
    n_used = (ends[ne - 1:ne] // tm).astype(jnp.int32)
    hs = moe_dispatch(x, nw, sh, sc, seg, ldest.T, n_rows, tt=tt, lc=lc)
    ys = moe_experts(hs, tile_expert, n_used, w1, w3, w2, tm=tm, tf=w1.shape[2] // 2)
    return moe_combine(ys, seg, ldest, wgt, x, g, tt=tt, lc=lc)


def _split_mod(mod):
    return [m[:, None, :] for m in jnp.split(mod, 6, axis=-1)]


def even_layer(x, c, rel_bias, ada_w, ada_b, norm1_w, in_w, q_norm_w, k_norm_w, sink, out_w,
               norm2_w, w1, w3, w2):
    s = x.shape[1]
    sh1, sc1, g1, sh2, sc2, g2 = _split_mod(ada_mod(c, ada_w, ada_b))
    proj = norm_mod_matmul(x, norm1_w, sh1, sc1, in_w.astype(BF16), ts=min(1024, s), tn=256, name="even_in_proj",
                           out_dtype=BF16)
    yf = fourier_mix(proj, tq=min(512, s))
    ya = window_attention(proj, band_bias(rel_bias), q_norm_w, k_norm_w, sink)
    x = cat_proj_residual(yf, ya, out_w.astype(BF16), x, g1, ts=min(1024, s))
    dff = w1.shape[1]
    return ffn_residual(x, norm2_w, sh2, sc2, g2, w1.astype(BF16), w3.astype(BF16), w2.astype(BF16),
                        ts=min(1024, s), tf=256)


def odd_layer(x, c, ada_w, ada_b, norm1_w, in_w, conv_w, conv_b, dt_bias_f, dt_bias_b, a_log_f, a_log_b,
              d_skip, gnorm_w, out_w, norm2_w, router_w, router_b, w1, w3, w2):
    s = x.shape[1]
    sh1, sc1, g1, sh2, sc2, g2 = _split_mod(ada_mod(c, ada_w, ada_b))
    d_inner = gnorm_w.shape[0]
    cdim = conv_w.shape[1]
    wide = d_inner + cdim
    in_w = in_w.astype(BF16)
    z, xs, bc = ssm_in_proj_conv(x, norm1_w, sh1, sc1, in_w[:, :d_inner], in_w[:, d_inner:wide], conv_w, conv_b,
                                 n_x=d_inner, ts=min(512, s))
    dt = norm_mod_matmul(x, norm1_w, sh1, sc1, in_w[:, wide:], ts=min(1024, s), tn=in_w.shape[1] - wide,
                         name="odd_dt_proj")
    y = ssd_scan_bidir(xs, bc, dt, dt_bias_f, dt_bias_b, a_log_f, a_log_b, d_skip)
    x = gated_proj_residual(y, z, gnorm_w, out_w.astype(BF16), x, g1, ts=min(512, s))
    return moe_residual(x, norm2_w, sh2, sc2, g2, router_w, router_b,
                        w1.astype(BF16), w3.astype(BF16), w2.astype(BF16))


def kernel(x, c, rel_bias, ev_ada_w, ev_ada_b, ev_norm1_w, ev_in_w, ev_q_norm_w, ev_k_norm_w, ev_sink, ev_out_w, ev_norm2_w, ev_ffn_w1, ev_ffn_w3, ev_ffn_w2, od_ada_w, od_ada_b, od_norm1_w, od_in_w, od_conv_w, od_conv_b, od_dt_bias_f, od_dt_bias_b, od_A_log_f, od_A_log_b, od_D, od_gnorm_w, od_out_w, od_norm2_w, od_router_w, od_router_b, od_moe_w1, od_moe_w3, od_moe_w2):
    depth = ev_ada_w.shape[0] + od_ada_w.shape[0]
    for i in range(depth):
        j = i // 2
        if i % 2 == 0:
            x = even_layer(x, c, rel_bias, ev_ada_w[j], ev_ada_b[j], ev_norm1_w[j], ev_in_w[j],
                           ev_q_norm_w[j], ev_k_norm_w[j], ev_sink[j], ev_out_w[j], ev_norm2_w[j],
                           ev_ffn_w1[j], ev_ffn_w3[j], ev_ffn_w2[j])
        else:
            x = odd_layer(x, c, od_ada_w[j], od_ada_b[j], od_norm1_w[j], od_in_w[j], od_conv_w[j],
                          od_conv_b[j], od_dt_bias_f[j], od_dt_bias_b[j], od_A_log_f[j], od_A_log_b[j],
                          od_D[j], od_gnorm_w[j], od_out_w[j], od_norm2_w[j], od_router_w[j],
                          od_router_b[j], od_moe_w1[j], od_moe_w3[j], od_moe_w2[j])
    return x
```

```python
import functools

import numpy as np
import jax
import jax.numpy as jnp
from jax import lax
from jax.experimental import pallas as pl
from jax.experimental.pallas import tpu as pltpu

F32 = jnp.float32
BF16 = jnp.bfloat16
HIGHEST = lax.Precision.HIGHEST

EPS = 1e-6
FNET_GROUPS = 4
FNET_GROUP_DIM = 128
FNET_WIDTH = FNET_GROUPS * FNET_GROUP_DIM
ATTN_HEADS = 8
ATTN_KV_HEADS = 2
HEAD_DIM = 64
ATTN_WIDTH = ATTN_HEADS * HEAD_DIM
KV_WIDTH = ATTN_KV_HEADS * HEAD_DIM
WINDOW = 128
BLOCK = 128
REL_BUCKETS = 32
REL_MAX_DIST = 128
SSM_HEAD_DIM = 64
SSM_GROUPS = 4
D_STATE = 128
CONV_WIDTH = 5
SSD_CHUNK = 128
N_EXPERTS = 8
TOP_K = 2
NEG_BIG = -1e30
ATTN_QBLOCKS = 2

V7X_VMEM_LIMIT_BYTES = 56 * 1024 * 1024
MOE_TILE_ROWS = 512
MOE_HIDDEN_CHUNK = 256


def _cparams(*sem):
    return pltpu.CompilerParams(dimension_semantics=sem, vmem_limit_bytes=V7X_VMEM_LIMIT_BYTES)


def _modnorm(x, nw, sh, sc):
    ms = jnp.mean(x * x, axis=-1, keepdims=True)
    return x * lax.rsqrt(ms + EPS) * nw * (1.0 + sc) + sh


def _silu(x):
    return x * (1.0 / (1.0 + jnp.exp(-x)))


def _ada_kernel(c_ref, w_ref, b_ref, o_ref):
    cs = _silu(c_ref[...]).astype(BF16)
    o_ref[...] = jnp.dot(cs, w_ref[...].astype(BF16), preferred_element_type=F32) + b_ref[...]


def ada_mod(c, w, b):
    bsz, d = c.shape
    n = w.shape[1]
    tn = 1536
    return pl.pallas_call(
        _ada_kernel,
        grid=(n // tn,),
        in_specs=[pl.BlockSpec((bsz, d), lambda j: (0, 0)),
                  pl.BlockSpec((d, tn), lambda j: (0, j)),
                  pl.BlockSpec((1, tn), lambda j: (0, j))],
        out_specs=pl.BlockSpec((bsz, tn), lambda j: (0, j)),
        out_shape=jax.ShapeDtypeStruct((bsz, n), F32),
        compiler_params=_cparams("arbitrary"),
        name="ada_mod",
    )(c, w, b.reshape(1, n))


def _nmm_kernel(x_ref, nw_ref, sh_ref, sc_ref, w_ref, o_ref, *, tn):
    h = _modnorm(x_ref[...], nw_ref[...], sh_ref[...], sc_ref[...]).astype(BF16)
    n = w_ref.shape[1]
    for lo in range(0, n, tn):
        o_ref[:, lo:lo + tn] = jnp.dot(h, w_ref[:, lo:lo + tn],
                                       preferred_element_type=F32).astype(o_ref.dtype)


def _resident(shape):
    return pl.BlockSpec(shape, lambda *_: tuple(0 for _ in shape), pipeline_mode=pl.Buffered(1))


def norm_mod_matmul(x, nw, sh, sc, w, *, ts, tn, name, out_dtype=F32):
    bsz, s, d = x.shape
    n = w.shape[1]
    assert n % tn == 0
    return pl.pallas_call(
        functools.partial(_nmm_kernel, tn=tn),
        grid=(bsz, s // ts),
        in_specs=[pl.BlockSpec((None, ts, d), lambda b, i: (b, i, 0)),
                  pl.BlockSpec((1, d), lambda b, i: (0, 0)),
                  pl.BlockSpec((None, 1, d), lambda b, i: (b, 0, 0)),
                  pl.BlockSpec((None, 1, d), lambda b, i: (b, 0, 0)),
                  _resident((d, n))],
        out_specs=pl.BlockSpec((None, ts, n), lambda b, i: (b, i, 0)),
        out_shape=jax.ShapeDtypeStruct((bsz, s, n), out_dtype),
        compiler_params=_cparams("parallel", "parallel"),
        name=name,
    )(x, nw.reshape(1, d), sh, sc, w)


def _dft_cos_sin(n):
    k = np.arange(n, dtype=np.int64)
    ang = ((k[:, None] * k[None, :]) % n).astype(np.float64) * (2.0 * np.pi / n)
    scale = 1.0 / np.sqrt(n)
    return np.cos(ang) * scale, np.sin(ang) * scale


def _fourier_kernel(u_ref, chan_ref, seq_ref, o_ref, ab_ref):
    s = u_ref.shape[0]

    @pl.when(pl.program_id(1) == 0)
    def _():
        for g in range(FNET_GROUPS):
            lo, hi = g * FNET_GROUP_DIM, (g + 1) * FNET_GROUP_DIM
            ug = u_ref[:, lo:hi].astype(BF16)
            cs = jnp.dot(ug, chan_ref[...], preferred_element_type=F32)
            ab_ref[0:s, lo:hi] = cs[:, :FNET_GROUP_DIM].astype(BF16)
            ab_ref[s:2 * s, lo:hi] = cs[:, FNET_GROUP_DIM:].astype(BF16)

    o_ref[...] = jnp.dot(seq_ref[...], ab_ref[...], preferred_element_type=F32).astype(o_ref.dtype)


def fourier_mix(proj, *, tq):
    bsz, s, _ = proj.shape
    cc, sc = _dft_cos_sin(FNET_GROUP_DIM)
    chan = jnp.asarray(np.concatenate([cc, sc], axis=1), BF16)
    cs, ss = _dft_cos_sin(s)
    seq = jnp.asarray(np.concatenate([cs, -ss], axis=1), BF16)
    return pl.pallas_call(
        _fourier_kernel,
        grid=(bsz, s // tq),
        in_specs=[pl.BlockSpec((None, s, FNET_WIDTH), lambda b, i: (b, 0, 0)),
                  pl.BlockSpec((FNET_GROUP_DIM, 2 * FNET_GROUP_DIM), lambda b, i: (0, 0)),
                  pl.BlockSpec((tq, 2 * s), lambda b, i: (i, 0))],
        out_specs=pl.BlockSpec((None, tq, FNET_WIDTH), lambda b, i: (b, i, 0)),
        out_shape=jax.ShapeDtypeStruct((bsz, s, FNET_WIDTH), BF16),
        scratch_shapes=[pltpu.VMEM((2 * s, FNET_WIDTH), BF16)],
        compiler_params=_cparams("parallel", "arbitrary"),
        name="fourier_mix",
    )(proj, chan, seq)


def _band_bucket_table():
    i = np.arange(BLOCK)[:, None]
    j = np.arange(3 * BLOCK)[None, :]
    rel = (j - BLOCK) - i
    half = REL_BUCKETS // 2
    max_exact = half // 2
    n = np.abs(rel)
    large = max_exact + (np.log(np.maximum(n, 1) / max_exact)
                         / np.log(REL_MAX_DIST / max_exact) * (half - max_exact)).astype(np.int32)
    large = np.minimum(large, half - 1)
    bucket = (rel > 0).astype(np.int32) * half + np.where(n < max_exact, n, large)
    return np.where(n <= WINDOW, bucket, -1).astype(np.int32)


def _bias_kernel(rb_ref, bucket_ref, o_ref):
    h = pl.program_id(0)
    bucket = bucket_ref[...]
    acc = jnp.full(bucket.shape, NEG_BIG, F32)
    for bkt in range(REL_BUCKETS):
        acc = jnp.where(bucket == bkt, rb_ref[bkt * ATTN_HEADS + h], acc)
    o_ref[...] = acc


def band_bias(rel_bias):
    bucket = jnp.asarray(_band_bucket_table())
    return pl.pallas_call(
        _bias_kernel,
        grid=(ATTN_HEADS,),
        in_specs=[pl.BlockSpec(memory_space=pltpu.SMEM),
                  pl.BlockSpec((BLOCK, 3 * BLOCK), lambda h: (0, 0))],
        out_specs=pl.BlockSpec((None, BLOCK, 3 * BLOCK), lambda h: (h, 0, 0)),
        out_shape=jax.ShapeDtypeStruct((ATTN_HEADS, BLOCK, 3 * BLOCK), F32),
        compiler_params=_cparams("arbitrary"),
        name="band_bias",
    )(rel_bias.reshape(-1), bucket)


def _head_mean_matrix(width):
    m = np.zeros((width, width), np.float32)
    for h in range(width // HEAD_DIM):
        m[h * HEAD_DIM:(h + 1) * HEAD_DIM, h * HEAD_DIM:(h + 1) * HEAD_DIM] = 1.0 / HEAD_DIM
    return m


def _heads_rms(t, mean_mat, w):
    sq = t * t
    hi = sq.astype(BF16)
    lo = (sq - hi.astype(F32)).astype(BF16)
    ms = (jnp.dot(hi, mean_mat, preferred_element_type=F32)
          + jnp.dot(lo, mean_mat, preferred_element_type=F32))
    return t * lax.rsqrt(ms + EPS) * w


def _attn_kernel(sink_ref, q_ref, kl_ref, kc_ref, kr_ref, vl_ref, vc_ref, vr_ref,
                 bias_ref, qnw_ref, knw_ref, qmean_ref, kmean_ref, o_ref):
    n = pl.program_id(1)
    nb = pl.num_programs(1) * ATTN_QBLOCKS
    k = jnp.concatenate([kl_ref[...], kc_ref[...], kr_ref[...]], axis=0).astype(F32)
    v = jnp.concatenate([vl_ref[...], vc_ref[...], vr_ref[...]], axis=0).astype(F32)
    col = lax.broadcasted_iota(jnp.int32, (1, 3 * BLOCK), 1)
    qn = _heads_rms(q_ref[...].astype(F32), qmean_ref[...], qnw_ref[...])
    kn = _heads_rms(k, kmean_ref[...], knw_ref[...])
    low = lax.broadcasted_iota(jnp.int32, (1, 2 * HEAD_DIM), 1) < HEAD_DIM
    kn_sw = pltpu.roll(kn, HEAD_DIM, axis=1)
    v_sw = pltpu.roll(v, HEAD_DIM, axis=1)
    k_dup = [jnp.where(low, kn, kn_sw).astype(BF16), jnp.where(low, kn_sw, kn).astype(BF16)]
    ones = jnp.ones((k.shape[0], 2 * HEAD_DIM), BF16)
    v_ext = [jnp.concatenate([v.astype(BF16), ones], axis=1),
             jnp.concatenate([v_sw.astype(BF16), ones], axis=1)]
    g = ATTN_HEADS // ATTN_KV_HEADS
    for qb in range(ATTN_QBLOCKS):
        blk = n * ATTN_QBLOCKS + qb
        band = slice(qb * BLOCK, (qb + 3) * BLOCK)
        first_key = jnp.where(blk == 0, BLOCK, 0)
        end_key = jnp.where(blk == nb - 1, 2 * BLOCK, 3 * BLOCK)
        outside = (col < first_key) | (col >= end_key)
        pairs = []
        for m in range(ATTN_HEADS // 2):
            j = (2 * m) // g
            qp = qn[qb * BLOCK:(qb + 1) * BLOCK, m * 2 * HEAD_DIM:(m + 1) * 2 * HEAD_DIM]
            res = []
            for idx in range(2):
                h = 2 * m + idx
                qm = jnp.where(low if idx == 0 else jnp.logical_not(low), qp, 0.0).astype(BF16)
                logits = lax.dot_general(qm, k_dup[j][band], (((1,), (1,)), ((), ())),
                                         preferred_element_type=F32)
                logits = jnp.where(outside, NEG_BIG, logits + bias_ref[h])
                sk = sink_ref[h]
                mx = jnp.maximum(jnp.max(logits, axis=-1, keepdims=True), sk)
                p = jnp.exp(logits - mx).astype(BF16)
                r = jnp.dot(p, v_ext[idx if j == 0 else 1 - idx][band], preferred_element_type=F32)
                denom = r[:, 2 * HEAD_DIM:] + jnp.exp(sk - mx)
                res.append(r[:, :2 * HEAD_DIM] / denom)
            pairs.append(jnp.where(low, res[0], res[1]))
        o_ref[qb * BLOCK:(qb + 1) * BLOCK, :] = jnp.concatenate(pairs, axis=-1).astype(o_ref.dtype)


def window_attention(proj, bias, q_norm_w, k_norm_w, sink):
    bsz, s, _ = proj.shape
    nb = s // BLOCK
    qcol = FNET_WIDTH // ATTN_WIDTH
    kcol = (FNET_WIDTH + ATTN_WIDTH) // KV_WIDTH
    vcol = kcol + 1

    qb = ATTN_QBLOCKS
    assert nb % qb == 0

    def kv_specs(col):
        return [pl.BlockSpec((None, BLOCK, KV_WIDTH), lambda b, n: (b, jnp.maximum(n * qb - 1, 0), col)),
                pl.BlockSpec((None, qb * BLOCK, KV_WIDTH), lambda b, n: (b, n, col)),
                pl.BlockSpec((None, BLOCK, KV_WIDTH), lambda b, n: (b, jnp.minimum((n + 1) * qb, nb - 1), col))]

    return pl.pallas_call(
        _attn_kernel,
        grid=(bsz, nb // qb),
        in_specs=[pl.BlockSpec(memory_space=pltpu.SMEM),
                  pl.BlockSpec((None, qb * BLOCK, ATTN_WIDTH), lambda b, n: (b, n, qcol)),
                  *kv_specs(kcol), *kv_specs(vcol),
                  pl.BlockSpec((ATTN_HEADS, BLOCK, 3 * BLOCK), lambda b, n: (0, 0, 0)),
                  pl.BlockSpec((1, ATTN_WIDTH), lambda b, n: (0, 0)),
                  pl.BlockSpec((1, KV_WIDTH), lambda b, n: (0, 0)),
                  pl.BlockSpec((ATTN_WIDTH, ATTN_WIDTH), lambda b, n: (0, 0)),
                  pl.BlockSpec((KV_WIDTH, KV_WIDTH), lambda b, n: (0, 0))],
        out_specs=pl.BlockSpec((None, qb * BLOCK, ATTN_WIDTH), lambda b, n: (b, n, 0)),
        out_shape=jax.ShapeDtypeStruct((bsz, s, ATTN_WIDTH), BF16),
        compiler_params=_cparams("parallel", "arbitrary"),
        name="window_attention",
    )(sink, proj, proj, proj, proj, proj, proj, proj, bias,
      (jnp.tile(q_norm_w, ATTN_HEADS) * (HEAD_DIM ** -0.5)).reshape(1, ATTN_WIDTH),
      jnp.tile(k_norm_w, ATTN_KV_HEADS).reshape(1, KV_WIDTH),
      jnp.asarray(_head_mean_matrix(ATTN_WIDTH), BF16), jnp.asarray(_head_mean_matrix(KV_WIDTH), BF16))


def _cat_proj_kernel(a1_ref, a2_ref, w_ref, x_ref, g_ref, o_ref):
    k1 = a1_ref.shape[1]
    y = jnp.dot(a1_ref[...].astype(BF16), w_ref[0:k1, :], preferred_element_type=F32)
    y = y + jnp.dot(a2_ref[...].astype(BF16), w_ref[k1:, :], preferred_element_type=F32)
    o_ref[...] = x_ref[...] + g_ref[...] * y


def cat_proj_residual(a1, a2, w, x, g, *, ts):
    bsz, s, d = x.shape
    k1, k2 = a1.shape[2], a2.shape[2]
    return pl.pallas_call(
        _cat_proj_kernel,
        grid=(bsz, s // ts),
        in_specs=[pl.BlockSpec((None, ts, k1), lambda b, i: (b, i, 0)),
                  pl.BlockSpec((None, ts, k2), lambda b, i: (b, i, 0)),
                  pl.BlockSpec((k1 + k2, d), lambda b, i: (0, 0)),
                  pl.BlockSpec((None, ts, d), lambda b, i: (b, i, 0)),
                  pl.BlockSpec((None, 1, d), lambda b, i: (b, 0, 0))],
        out_specs=pl.BlockSpec((None, ts, d), lambda b, i: (b, i, 0)),
        out_shape=jax.ShapeDtypeStruct((bsz, s, d), F32),
        compiler_params=_cparams("parallel", "parallel"),
        name="mixer_out_proj",
    )(a1, a2, w, x, g)


def _ffn_kernel(x_ref, nw_ref, sh_ref, sc_ref, g_ref, w1_ref, w3_ref, w2_ref, o_ref, acc_ref, *, tf):
    h = _modnorm(x_ref[...], nw_ref[...], sh_ref[...], sc_ref[...]).astype(BF16)
    dff = w1_ref.shape[1]
    for lo in range(0, dff, tf):
        a = jnp.dot(h, w1_ref[:, lo:lo + tf], preferred_element_type=F32)
        b = jnp.dot(h, w3_ref[:, lo:lo + tf], preferred_element_type=F32)
        t = (_silu(a) * b).astype(BF16)
        contrib = jnp.dot(t, w2_ref[lo:lo + tf, :], preferred_element_type=F32)
        if lo == 0:
            acc_ref[...] = contrib
        else:
            acc_ref[...] += contrib
    o_ref[...] = x_ref[...] + g_ref[...] * acc_ref[...]


def ffn_residual(x, nw, sh, sc, g, w1, w3, w2, *, ts, tf):
    bsz, s, d = x.shape
    dff = w1.shape[1]
    assert dff % tf == 0
    vec = pl.BlockSpec((None, 1, d), lambda b, i: (b, 0, 0))
    return pl.pallas_call(
        functools.partial(_ffn_kernel, tf=tf),
        grid=(bsz, s // ts),
        in_specs=[pl.BlockSpec((None, ts, d), lambda b, i: (b, i, 0)),
                  pl.BlockSpec((1, d), lambda b, i: (0, 0)),
                  vec, vec, vec,
                  _resident((d, dff)), _resident((d, dff)), _resident((dff, d))],
        out_specs=pl.BlockSpec((None, ts, d), lambda b, i: (b, i, 0)),
        out_shape=jax.ShapeDtypeStruct((bsz, s, d), F32),
        scratch_shapes=[pltpu.VMEM((ts, d), F32)],
        compiler_params=_cparams("parallel", "parallel"),
        name="ffn_swiglu",
    )(x, nw.reshape(1, d), sh, sc, g, w1, w3, w2)


CONV_PAD = 8
CONV_ROWS = 256


def _conv_kernel(x_ref, w_ref, b_ref, o_ref, pad_ref):
    s, tc = x_ref.shape
    zeros = jnp.zeros((CONV_PAD, tc), F32)
    pad_ref[0:CONV_PAD, :] = zeros
    pad_ref[CONV_PAD + s:, :] = zeros
    pad_ref[CONV_PAD:CONV_PAD + s, :] = x_ref[...].astype(F32)
    half = CONV_WIDTH // 2
    for r in range(s // CONV_ROWS):
        base = CONV_PAD + r * CONV_ROWS - half
        acc = jnp.zeros((CONV_ROWS, tc), F32) + b_ref[...]
        for kk in range(CONV_WIDTH):
            acc = acc + pad_ref[base + kk:base + kk + CONV_ROWS, :] * w_ref[kk:kk + 1, :]
        o_ref[r * CONV_ROWS:(r + 1) * CONV_ROWS, :] = _silu(acc)


def conv_silu(proj, conv_w, conv_b, *, col0, tc):
    bsz, s, _ = proj.shape
    cdim = conv_w.shape[1]
    cb0 = col0 // tc
    return pl.pallas_call(
        _conv_kernel,
        grid=(bsz, cdim // tc),
        in_specs=[pl.BlockSpec((None, s, tc), lambda b, j: (b, 0, cb0 + j)),
                  pl.BlockSpec((CONV_WIDTH, tc), lambda b, j: (0, j)),
                  pl.BlockSpec((1, tc), lambda b, j: (0, j))],
        out_specs=pl.BlockSpec((None, s, tc), lambda b, j: (b, 0, j)),
        out_shape=jax.ShapeDtypeStruct((bsz, s, cdim), F32),
        scratch_shapes=[pltpu.VMEM((s + 2 * CONV_PAD, tc), F32)],
        compiler_params=_cparams("parallel", "parallel"),
        name="conv_silu",
    )(proj, conv_w, conv_b.reshape(1, cdim))


HALO_ROWS = 16
PROJ_CHUNK = 256


def _proj_conv_kernel(x_ref, xp_ref, xn_ref, nw_ref, sh_ref, sc_ref, wz_ref, wc_ref, wd_ref, cw_ref, cb_ref,
                      z_ref, xs_ref, bc_ref, dt_ref):
    i = pl.program_id(1)
    ts = x_ref.shape[0]
    half = CONV_WIDTH // 2

    def hnorm(ref):
        return _modnorm(ref[...], nw_ref[...], sh_ref[...], sc_ref[...])

    h = hnorm(x_ref).astype(BF16)
    h_prev = jnp.where(i == 0, 0.0, hnorm(xp_ref)).astype(BF16)
    h_next = jnp.where(i == pl.num_programs(1) - 1, 0.0, hnorm(xn_ref)).astype(BF16)
    h_ext = jnp.concatenate([h_prev, h, h_next], axis=0)

    n_x = xs_ref.shape[1]
    rows = h_ext.shape[0]

    def project(lo):
        return jnp.dot(h_ext, wc_ref[:, lo:lo + PROJ_CHUNK], preferred_element_type=F32)

    def z_chunk(lo):
        hi = lo + PROJ_CHUNK
        z_ref[:, lo:hi] = jnp.dot(h, wz_ref[:, lo:hi], preferred_element_type=F32).astype(z_ref.dtype)

    z_los = list(range(0, wz_ref.shape[1], PROJ_CHUNK))
    c_los = list(range(0, wc_ref.shape[1], PROJ_CHUNK))
    p_next = project(c_los[0])
    for n, lo in enumerate(c_los):
        hi = lo + PROJ_CHUNK
        p = p_next
        if n + 1 < len(c_los):
            p_next = project(c_los[n + 1])
        if z_los:
            z_chunk(z_los.pop(0))
        acc = jnp.zeros((ts, PROJ_CHUNK), F32) + cb_ref[:, lo:hi]
        for kk in range(CONV_WIDTH):
            shifted = p if kk == half else pltpu.roll(p, (half - kk) % rows, axis=0)
            acc = acc + shifted[HALO_ROWS:HALO_ROWS + ts, :] * cw_ref[kk:kk + 1, lo:hi]
        out = _silu(acc)
        if lo < n_x:
            xs_ref[:, lo:hi] = out
        else:
            bc_ref[:, lo - n_x:hi - n_x] = out.astype(bc_ref.dtype)
    for lo in z_los:
        z_chunk(lo)
    dt_ref[...] = jnp.dot(h, wd_ref[...], preferred_element_type=F32)


def ssm_in_proj_conv(x, nw, sh, sc, w_z, w_xbc, w_dt, conv_w, conv_b, *, n_x, ts):
    bsz, s, d = x.shape
    dz = w_z.shape[1]
    dc = w_xbc.shape[1]
    ddt = w_dt.shape[1]
    assert ts % HALO_ROWS == 0 and n_x % PROJ_CHUNK == 0 and dc % PROJ_CHUNK == 0 and dz % PROJ_CHUNK == 0
    r = ts // HALO_ROWS
    last = s // HALO_ROWS - 1
    vec = pl.BlockSpec((None, 1, d), lambda b, i: (b, 0, 0))
    return pl.pallas_call(
        _proj_conv_kernel,
        grid=(bsz, s // ts),
        in_specs=[pl.BlockSpec((None, ts, d), lambda b, i: (b, i, 0)),
                  pl.BlockSpec((None, HALO_ROWS, d), lambda b, i: (b, jnp.maximum(i * r - 1, 0), 0)),
                  pl.BlockSpec((None, HALO_ROWS, d), lambda b, i: (b, jnp.minimum((i + 1) * r, last), 0)),
                  pl.BlockSpec((1, d), lambda b, i: (0, 0)), vec, vec,
                  _resident((d, dz)), _resident((d, dc)), _resident((d, ddt)),
                  _resident((CONV_WIDTH, dc)), _resident((1, dc))],
        out_specs=[pl.BlockSpec((None, ts, dz), lambda b, i: (b, i, 0)),
                   pl.BlockSpec((None, ts, n_x), lambda b, i: (b, i, 0)),
                   pl.BlockSpec((None, ts, dc - n_x), lambda b, i: (b, i, 0)),
                   pl.BlockSpec((None, ts, ddt), lambda b, i: (b, i, 0))],
        out_shape=[jax.ShapeDtypeStruct((bsz, s, dz), BF16),
                   jax.ShapeDtypeStruct((bsz, s, n_x), F32),
                   jax.ShapeDtypeStruct((bsz, s, dc - n_x), BF16),
                   jax.ShapeDtypeStruct((bsz, s, ddt), F32)],
        compiler_params=_cparams("parallel", "parallel"),
        name="odd_in_proj_conv",
    )(x, x, x, nw.reshape(1, d), sh, sc, w_z, w_xbc, w_dt, conv_w, conv_b.reshape(1, dc))


def _softplus(x):
    return jnp.maximum(x, 0.0) + jnp.log(1.0 + jnp.exp(-jnp.abs(x)))


def _expand_matrix(n_in, width):
    e = np.zeros((n_in, n_in * width), np.float32)
    for h in range(n_in):
        e[h, h * width:(h + 1) * width] = 1.0
    return e


LOG2E = 1.4426950408889634
DECAY_SLOTS = 12


def _decay_placement(nh):
    place = np.zeros((3, 2 * nh, 3 * 128), np.float32)
    const = np.zeros((1, 3 * 128), np.float32)
    for h in range(nh):
        for part in range(3):
            place[part, h, part * nh + h] = 1.0
            place[part, nh + h, (6 + part) * nh + h] = -1.0
            place[part, h, 128 + (3 + part) * nh + h] = -1.0
            place[part, nh + h, 256 + (9 + part) * nh + h] = 1.0
            const[0, (3 + part) * nh + h] = 1.0
            const[0, (9 + part) * nh + h] = 1.0
            const[0, 128 + part * nh + h] = 1.0
            const[0, 256 + (6 + part) * nh + h] = 1.0
    return place, const


def _split3(v):
    hi = v.astype(BF16)
    r = v - hi.astype(F32)
    mid = r.astype(BF16)
    lo = (r - mid.astype(F32)).astype(BF16)
    return hi, mid, lo


def _ssd_kernel(x_ref, b_ref, c_ref, dtc_f_ref, dtc_b_ref, dtr_f_ref, dtr_b_ref,
                pc_f_ref, pc_b_ref, pr_f_ref, pr_b_ref, dx_ref,
                tri_ref, place_ref, pconst_ref, e64_ref, y_ref,
                ac_ref, dc_ref, dr_ref, pq_ref, xdf_ref, xdb_ref, eif_ref, er_ref, decf_ref, decb_ref,
                hf_ref, hb_ref):
    s = x_ref.shape[0]
    q = SSD_CHUNK
    nh = pc_f_ref.shape[1]
    nc = s // q
    hd = SSM_HEAD_DIM

    def col_params(raw_ref, p_ref):
        dt = _softplus(raw_ref[...] + p_ref[0:1, :])
        return dt, (-LOG2E) * jnp.exp(p_ref[1:2, :]) * dt

    dcf, acf = col_params(dtc_f_ref, pc_f_ref)
    dcb, acb = col_params(dtc_b_ref, pc_b_ref)
    dc_ref[:, 0:nh] = dcf
    dc_ref[:, nh:] = dcb
    ac_ref[:, 0:nh] = acf
    ac_ref[:, nh:] = acb
    dr_ref[0:nh, :] = _softplus(dtr_f_ref[...] + pr_f_ref[:, 0:1])
    dr_ref[nh:, :] = _softplus(dtr_b_ref[...] + pr_b_ref[:, 0:1])

    li = lax.broadcasted_iota(jnp.int32, (q, q), 0)
    si = lax.broadcasted_iota(jnp.int32, (q, q), 1)
    lower = li >= si
    upper = li <= si
    slot_head = lax.broadcasted_iota(jnp.int32, (1, 128), 1) % nh
    pair_lo = lax.broadcasted_iota(jnp.int32, (1, 2 * hd), 1) < hd

    def bdot(a, b):
        return jnp.dot(a, b, preferred_element_type=F32)

    def chunk_cumsums(sl):
        a_col = ac_ref[sl, :]
        tri = tri_ref[...]
        hi, mid, lo = _split3(a_col)
        return a_col, bdot(tri, hi) + bdot(tri, mid) + bdot(tri, lo)

    def expand64(v, parts=2):
        e = e64_ref[...]
        hi = v.astype(BF16)
        out = bdot(hi, e)
        if parts == 2:
            out = out + bdot((v - hi.astype(F32)).astype(BF16), e)
        return out

    hf_ref[...] = jnp.zeros_like(hf_ref)
    hb_ref[...] = jnp.zeros_like(hb_ref)

    def prep_body(c, carry):
        sl = pl.ds(pl.multiple_of(c * q, q), q)
        a_col, i_col = chunk_cumsums(sl)
        d_col = dc_ref[sl, :]
        if_col = i_col[:, 0:nh]
        tot_f = i_col[q - 1:q, 0:nh]
        tot_b = i_col[q - 1:q, nh:]
        ie = i_col - jnp.where(lax.broadcasted_iota(jnp.int32, (1, 2 * nh), 1) < nh, 0.0, a_col)
        eb_col = ie[:, nh:]
        parts = _split3(ie)
        placed = pconst_ref[...]
        for part in range(3):
            placed = placed + bdot(parts[part], place_ref[part])
        pq_ref[sl, :] = placed.astype(BF16)
        xc = x_ref[sl, :]
        xdf_ref[sl, :] = (xc * expand64(jnp.exp2(tot_f - if_col) * d_col[:, 0:nh], parts=1)).astype(BF16)
        xdb_ref[sl, :] = (xc * expand64(jnp.exp2(eb_col) * d_col[:, nh:], parts=1)).astype(BF16)
        eif_ref[sl, :] = expand64(jnp.exp2(if_col))
        er_ref[sl, :] = expand64(jnp.exp2(tot_b - eb_col))
        dec = expand64(jnp.exp2(jnp.concatenate([tot_f, tot_b], axis=0)))
        decf_ref[c] = jnp.broadcast_to(dec[0:1, :], decf_ref.shape[1:])
        decb_ref[c] = jnp.broadcast_to(dec[1:2, :], decb_ref.shape[1:])
        return carry

    lax.fori_loop(0, nc, prep_body, 0, unroll=4)

    def fwd_body(c, carry):
        sl = pl.ds(pl.multiple_of(c * q, q), q)
        xc = x_ref[sl, :]
        bc = b_ref[sl, :].astype(BF16)
        cc = c_ref[sl, :].astype(BF16)
        d_row = dr_ref[:, sl]
        placed = pq_ref[sl, :]
        p_all = placed[:, 0:128]
        q_cat = jnp.concatenate([placed[:, 128:256], placed[:, 256:384]], axis=0)

        cb = lax.dot_general(cc, bc, (((1,), (1,)), ((), ())), preferred_element_type=F32)
        xb = xc.astype(BF16)
        ys = []
        for h0 in range(0, nh, 2):
            ms = []
            for h in (h0, h0 + 1):
                ph = jnp.where(slot_head == h, p_all, jnp.zeros_like(p_all))
                g2 = lax.dot_general(ph, q_cat, (((1,), (1,)), ((), ())), preferred_element_type=F32)
                arg = jnp.where(lower, g2[:, 0:q], g2[:, q:])
                wgt = (jnp.where(lower, d_row[h:h + 1, :], 0.0)
                       + jnp.where(upper, d_row[nh + h:nh + h + 1, :], 0.0))
                ms.append((cb * jnp.exp2(arg) * wgt).astype(BF16))
            xp = xb[:, h0 * hd:(h0 + 2) * hd]
            zero = jnp.zeros_like(xp)
            rhs = jnp.concatenate([jnp.where(pair_lo, xp, zero), jnp.where(pair_lo, zero, xp)], axis=0)
            ys.append(bdot(jnp.concatenate(ms, axis=1), rhs))
        y = jnp.concatenate(ys, axis=1) + dx_ref[...] * xc

        states = lax.dot_general(bc, xdf_ref[sl, :], (((0,), (0,)), ((), ())),
                                 preferred_element_type=F32)
        h_prev = hf_ref[...]
        y = y + bdot(cc, h_prev.astype(BF16)) * eif_ref[sl, :]
        hf_ref[...] = h_prev * decf_ref[c][0:1, :] + states
        y_ref[sl, :] = y
        return carry

    lax.fori_loop(0, nc, fwd_body, 0, unroll=2)

    def bwd_body(t, carry):
        c = nc - 1 - t
        sl = pl.ds(pl.multiple_of(c * q, q), q)
        bc = b_ref[sl, :].astype(BF16)
        cc = c_ref[sl, :].astype(BF16)
        states = lax.dot_general(bc, xdb_ref[sl, :], (((0,), (0,)), ((), ())),
                                 preferred_element_type=F32)
        h_prev = hb_ref[...]
        y_ref[sl, :] += bdot(cc, h_prev.astype(BF16)) * er_ref[sl, :]
        hb_ref[...] = h_prev * decb_ref[c][0:1, :] + states
        return carry

    lax.fori_loop(0, nc, bwd_body, 0, unroll=4)


def _bf16_parts3(v):
    hi = v.astype(BF16).astype(F32)
    r = v - hi
    mid = r.astype(BF16).astype(F32)
    lo = (r - mid).astype(BF16).astype(F32)
    return hi, mid, lo


def _ssd_t_kernel(x_ref, b_ref, c_ref, dt_f_ref, dt_b_ref, p_f_ref, p_b_ref, dx_ref, trio_ref, y_ref,
                  ar_ref, dr_ref, sc_ref, pq_ref, qt_ref, xtb_ref, xdf_ref, xdb_ref, yt_ref, hf_ref, hb_ref):
    s = x_ref.shape[0]
    q = SSD_CHUNK
    nh = p_f_ref.shape[0]
    nc = s // q
    hd = SSM_HEAD_DIM

    def row_params(raw_ref, p_ref):
        dt = _softplus(raw_ref[...] + p_ref[:, 0:1])
        return dt, (-LOG2E) * jnp.exp(p_ref[:, 1:2]) * dt

    dtf, af = row_params(dt_f_ref, p_f_ref)
    dtb, ab = row_params(dt_b_ref, p_b_ref)
    dr_ref[0:nh, :] = dtf
    dr_ref[nh:, :] = dtb
    ar_ref[0:nh, :] = af
    ar_ref[nh:, :] = ab

    li = lax.broadcasted_iota(jnp.int32, (q, q), 0)
    si = lax.broadcasted_iota(jnp.int32, (q, q), 1)
    lower = li >= si
    upper = li <= si
    slot_head = lax.broadcasted_iota(jnp.int32, (1, q), 1) % nh

    def bdot(a, b):
        return jnp.dot(a, b, preferred_element_type=F32)

    def ntdot(a, b):
        return lax.dot_general(a, b, (((1,), (1,)), ((), ())), preferred_element_type=F32)

    def head_rows(v):
        return jnp.concatenate([jnp.broadcast_to(v[h:h + 1, :], (hd, q)) for h in range(nh)], axis=0)

    ones = jnp.ones((nh, q), F32)
    zeros = jnp.zeros((nh, q), F32)

    def prep_body(c, carry):
        sl = pl.ds(pl.multiple_of(c * q, q), q)
        a_row = ar_ref[:, sl]
        d_row = dr_ref[:, sl]
        parts = jnp.concatenate([p.astype(BF16) for p in _bf16_parts3(a_row)], axis=0)
        cs = bdot(parts, trio_ref[...])
        cs = cs[0:2 * nh] + cs[2 * nh:4 * nh] + cs[4 * nh:6 * nh]
        i_f = cs[0:nh, 0:q]
        e_b = cs[nh:, 0:q] - a_row[nh:]
        tot_f = cs[0:nh, q:]
        tot_b = cs[nh:, q:]
        ih, im, il = _bf16_parts3(i_f)
        eh, em, el = _bf16_parts3(e_b)
        pad = [zeros] * (q // nh - DECAY_SLOTS)
        p_t = jnp.concatenate([ih, im, il, ones, ones, ones, -eh, -em, -el, ones, ones, ones] + pad, axis=0)
        qf_t = jnp.concatenate([ones, ones, ones, -ih, -im, -il] + [zeros] * 6 + pad, axis=0)
        qb_t = jnp.concatenate([zeros] * 6 + [ones, ones, ones, eh, em, el] + pad, axis=0)
        pq_ref[sl, :] = p_t.T.astype(BF16)
        qt_ref[c] = jnp.concatenate([qf_t, qb_t], axis=1).astype(BF16)
        sc_ref[0 * nh:1 * nh, sl] = jnp.exp2(i_f)
        sc_ref[1 * nh:2 * nh, sl] = jnp.exp2(tot_b - e_b)
        sc_ref[2 * nh:3 * nh, sl] = jnp.exp2(tot_f)
        sc_ref[3 * nh:4 * nh, sl] = jnp.exp2(tot_b)
        xt = x_ref[sl, :].T
        xtb_ref[:, sl] = xt.astype(BF16)
        xdf_ref[:, sl] = (xt * head_rows(jnp.exp2(tot_f - i_f) * d_row[0:nh])).astype(BF16)
        xdb_ref[:, sl] = (xt * head_rows(jnp.exp2(e_b) * d_row[nh:])).astype(BF16)
        return carry

    lax.fori_loop(0, nc, prep_body, 0, unroll=2)

    hf_ref[...] = jnp.zeros_like(hf_ref)
    hb_ref[...] = jnp.zeros_like(hb_ref)
    zero_half = jnp.zeros((hd, q), BF16)

    def fwd_body(c, carry):
        sl = pl.ds(pl.multiple_of(c * q, q), q)
        bc = b_ref[sl, :].astype(BF16)
        cc = c_ref[sl, :].astype(BF16)
        d_row = dr_ref[:, sl]
        p_all = pq_ref[sl, :]
        q_t = qt_ref[c]
        xtb = xtb_ref[:, sl]
        cb = ntdot(cc, bc)
        g2s = [bdot(jnp.where(slot_head == h, p_all, jnp.zeros_like(p_all)), q_t) for h in range(nh)]
        ms = []
        for h in range(nh):
            arg = jnp.where(lower, g2s[h][:, 0:q], g2s[h][:, q:])
            wgt = (jnp.where(lower, d_row[h:h + 1, :], 0.0)
                   + jnp.where(upper, d_row[nh + h:nh + h + 1, :], 0.0))
            ms.append((cb * jnp.exp2(arg) * wgt).astype(BF16))
        yd = []
        for h0 in range(0, nh, 2):
            lhs = jnp.concatenate(
                [jnp.concatenate([xtb[h0 * hd:(h0 + 1) * hd], zero_half], axis=0),
                 jnp.concatenate([zero_half, xtb[(h0 + 1) * hd:(h0 + 2) * hd]], axis=0)], axis=1)
            yd.append(ntdot(lhs, jnp.concatenate(ms[h0:h0 + 2], axis=1)))
        states = bdot(xdf_ref[:, sl], bc)
        h_prev = hf_ref[...]
        y_off = ntdot(h_prev.astype(BF16), cc) * head_rows(sc_ref[0 * nh:1 * nh, sl])
        hf_ref[...] = h_prev * head_rows(sc_ref[2 * nh:3 * nh, sl]) + states
        yt_ref[:, sl] = jnp.concatenate(yd, axis=0) + y_off
        return carry

    lax.fori_loop(0, nc, fwd_body, 0, unroll=4)

    def bwd_body(t, carry):
        c = nc - 1 - t
        sl = pl.ds(pl.multiple_of(c * q, q), q)
        bc = b_ref[sl, :].astype(BF16)
        cc = c_ref[sl, :].astype(BF16)
        states = bdot(xdb_ref[:, sl], bc)
        h_prev = hb_ref[...]
        y_off = ntdot(h_prev.astype(BF16), cc) * head_rows(sc_ref[1 * nh:2 * nh, sl])
        hb_ref[...] = h_prev * head_rows(sc_ref[3 * nh:4 * nh, sl]) + states
        y_ref[sl, :] = ((yt_ref[:, sl] + y_off).T + dx_ref[...] * x_ref[sl, :]).astype(y_ref.dtype)
        return carry

    lax.fori_loop(0, nc, bwd_body, 0, unroll=4)


def ssd_scan_bidir(xs, bc, dt, dt_bias_f, dt_bias_b, a_log_f, a_log_b, d_skip):
    bsz, s, _ = xs.shape
    nheads = dt.shape[2] // 2
    nh = nheads // SSM_GROUPS
    gw = nh * SSM_HEAD_DIM
    d_inner = nheads * SSM_HEAD_DIM
    q = SSD_CHUNK
    dt_row = jnp.transpose(dt.reshape(bsz, s, 2 * SSM_GROUPS, nh), (0, 2, 3, 1))
    prm = jnp.stack([jnp.concatenate([dt_bias_f, dt_bias_b]), jnp.concatenate([a_log_f, a_log_b])])
    p_row = jnp.transpose(prm.reshape(2, 2 * SSM_GROUPS, nh), (1, 2, 0))
    dx = jnp.repeat(d_skip, SSM_HEAD_DIM).reshape(SSM_GROUPS, 1, gw)
    assert DECAY_SLOTS * nh <= q and q % nh == 0 and D_STATE == q
    trio = jnp.asarray(np.concatenate([np.triu(np.ones((q, q), np.float32)), np.ones((q, q), np.float32)],
                                      axis=1), BF16)
    bcol = d_inner // D_STATE
    G = SSM_GROUPS
    nc = s // q

    return pl.pallas_call(
        _ssd_t_kernel,
        grid=(bsz, SSM_GROUPS),
        in_specs=[pl.BlockSpec((None, s, gw), lambda b, g: (b, 0, g)),
                  pl.BlockSpec((None, s, D_STATE), lambda b, g: (b, 0, g)),
                  pl.BlockSpec((None, s, D_STATE), lambda b, g: (b, 0, G + g)),
                  pl.BlockSpec((None, None, nh, s), lambda b, g: (b, g, 0, 0)),
                  pl.BlockSpec((None, None, nh, s), lambda b, g: (b, G + g, 0, 0)),
                  pl.BlockSpec((None, nh, 2), lambda b, g: (g, 0, 0)),
                  pl.BlockSpec((None, nh, 2), lambda b, g: (G + g, 0, 0)),
                  pl.BlockSpec((None, 1, gw), lambda b, g: (g, 0, 0)),
                  pl.BlockSpec((q, 2 * q), lambda b, g: (0, 0))],
        out_specs=pl.BlockSpec((None, s, gw), lambda b, g: (b, 0, g)),
        out_shape=jax.ShapeDtypeStruct((bsz, s, d_inner), BF16),
        scratch_shapes=[pltpu.VMEM((2 * nh, s), F32), pltpu.VMEM((2 * nh, s), F32),
                        pltpu.VMEM((4 * nh, s), F32),
                        pltpu.VMEM((s, q), BF16), pltpu.VMEM((nc, q, 2 * q), BF16),
                        pltpu.VMEM((gw, s), BF16), pltpu.VMEM((gw, s), BF16), pltpu.VMEM((gw, s), BF16),
                        pltpu.VMEM((gw, s), F32),
                        pltpu.VMEM((gw, D_STATE), F32), pltpu.VMEM((gw, D_STATE), F32)],
        compiler_params=_cparams("parallel", "parallel"),
        name="ssd_scan",
    )(xs, bc, bc, dt_row, dt_row, p_row, p_row, dx, trio)


def _gated_proj_kernel(y_ref, z_ref, gw_ref, w_ref, x_ref, g_ref, o_ref, acc_ref):
    k = y_ref.shape[1]
    ss = jnp.zeros((y_ref.shape[0], 1), F32)
    for lo in range(0, k, PROJ_CHUNK):
        hi = lo + PROJ_CHUNK
        t = y_ref[:, lo:hi].astype(F32) * _silu(z_ref[:, lo:hi].astype(F32))
        ss = ss + jnp.sum(t * t, axis=-1, keepdims=True)
        contrib = jnp.dot((t * gw_ref[:, lo:hi]).astype(BF16), w_ref[lo:hi, :], preferred_element_type=F32)
        if lo == 0:
            acc_ref[...] = contrib
        else:
            acc_ref[...] += contrib
    o_ref[...] = x_ref[...] + g_ref[...] * (acc_ref[...] * lax.rsqrt(ss * (1.0 / k) + EPS))


def gated_proj_residual(y, zproj, gw, w, x, g, *, ts):
    bsz, s, d = x.shape
    k = y.shape[2]
    return pl.pallas_call(
        _gated_proj_kernel,
        grid=(bsz, s // ts),
        in_specs=[pl.BlockSpec((None, ts, k), lambda b, i: (b, i, 0)),
                  pl.BlockSpec((None, ts, k), lambda b, i: (b, i, 0)),
                  pl.BlockSpec((1, k), lambda b, i: (0, 0)),
                  _resident((k, d)),
                  pl.BlockSpec((None, ts, d), lambda b, i: (b, i, 0)),
                  pl.BlockSpec((None, 1, d), lambda b, i: (b, 0, 0))],
        out_specs=pl.BlockSpec((None, ts, d), lambda b, i: (b, i, 0)),
        out_shape=jax.ShapeDtypeStruct((bsz, s, d), F32),
        scratch_shapes=[pltpu.VMEM((ts, d), F32)],
        compiler_params=_cparams("parallel", "parallel"),
        name="ssd_out_proj",
    )(y, zproj, gw.reshape(1, k), w, x, g)


def _router_kernel(x_ref, nw_ref, sh_ref, sc_ref, rw_ref, rb_ref, lt_ref,
                   eidx_ref, rank_ref, wgt_ref, cnt_ref):
    h = _modnorm(x_ref[...], nw_ref[...], sh_ref[...], sc_ref[...])
    rw = rw_ref[...]
    h_hi = h.astype(BF16)
    h_lo = (h - h_hi.astype(F32)).astype(BF16)
    rw_hi = rw.astype(BF16)
    rw_lo = (rw - rw_hi.astype(F32)).astype(BF16)
    logits = (jnp.dot(h_hi, rw_hi, preferred_element_type=F32)
              + (jnp.dot(h_lo, rw_hi, preferred_element_type=F32)
                 + jnp.dot(h_hi, rw_lo, preferred_element_type=F32))) + rb_ref[...]
    ts, ne = logits.shape
    eid = lax.broadcasted_iota(jnp.int32, (ts, ne), 1)
    m1 = jnp.max(logits, axis=-1, keepdims=True)
    i1 = jnp.min(jnp.where(logits == m1, eid, ne), axis=-1, keepdims=True)
    rest = jnp.where(eid == i1, -jnp.inf, logits)
    m2 = jnp.max(rest, axis=-1, keepdims=True)
    i2 = jnp.min(jnp.where(rest == m2, eid, ne), axis=-1, keepdims=True)
    e2 = jnp.exp(m2 - m1)
    w1 = 1.0 / (1.0 + e2)
    w2 = e2 / (1.0 + e2)
    oh1 = (eid == i1).astype(F32)
    oh2 = (eid == i2).astype(F32)
    chosen = oh1 + oh2
    incl = jnp.dot(lt_ref[...], chosen.astype(BF16), preferred_element_type=F32)
    before = incl - chosen
    r1 = jnp.sum(oh1 * before, axis=-1, keepdims=True)
    r2 = jnp.sum(oh2 * before, axis=-1, keepdims=True)
    eidx_ref[...] = jnp.concatenate([i1, i2], axis=1)
    rank_ref[...] = jnp.concatenate([r1, r2], axis=1).astype(jnp.int32)
    wgt_ref[...] = jnp.concatenate([w1, w2], axis=1)
    cnt_ref[...] = incl[ts - 1:ts, :].astype(jnp.int32)


def moe_route(x, nw, sh, sc, router_w, router_b, *, ts):
    bsz, s, d = x.shape
    ne = router_w.shape[1]
    nt = s // ts
    lt = jnp.asarray(np.tril(np.ones((ts, ts), np.float32)), BF16)
    tok = lambda dt: jax.ShapeDtypeStruct((bsz * s, TOP_K), dt)
    tok_spec = pl.BlockSpec((ts, TOP_K), lambda b, i: (b * nt + i, 0))
    return pl.pallas_call(
        _router_kernel,
        grid=(bsz, nt),
        in_specs=[pl.BlockSpec((None, ts, d), lambda b, i: (b, i, 0)),
                  pl.BlockSpec((1, d), lambda b, i: (0, 0)),
                  pl.BlockSpec((None, 1, d), lambda b, i: (b, 0, 0)),
                  pl.BlockSpec((None, 1, d), lambda b, i: (b, 0, 0)),
                  pl.BlockSpec((d, ne), lambda b, i: (0, 0)),
                  pl.BlockSpec((1, ne), lambda b, i: (0, 0)),
                  pl.BlockSpec((ts, ts), lambda b, i: (0, 0))],
        out_specs=[tok_spec, tok_spec, tok_spec,
                   pl.BlockSpec((None, 1, ne), lambda b, i: (b * nt + i, 0, 0))],
        out_shape=[tok(jnp.int32), tok(jnp.int32), tok(F32),
                   jax.ShapeDtypeStruct((bsz * nt, 1, ne), jnp.int32)],
        compiler_params=_cparams("parallel", "parallel"),
        name="moe_router",
    )(x, nw.reshape(1, d), sh, sc, router_w, router_b.reshape(1, ne), lt)


SEG_ROWS = 16
SEG_FIELDS = 3


def _segment_copies(seg_ref, tile, n_experts, make_copy, *, wait):
    for e in range(n_experts):
        base = (tile * n_experts + e) * SEG_FIELDS
        local0 = seg_ref[base]
        global0 = seg_ref[base + 1]

        def body(i, carry, local0=local0, global0=global0):
            cp = make_copy(pl.multiple_of(local0 + i * SEG_ROWS, SEG_ROWS),
                           pl.multiple_of(global0 + i * SEG_ROWS, SEG_ROWS))
            if wait:
                cp.wait()
            else:
                cp.start()
            return carry

        lax.fori_loop(0, seg_ref[base + 2], body, 0)


def _dispatch_kernel(seg_ref, x_ref, nw_ref, sh_ref, sc_ref, ld_ref, hs_ref, buf_ref, zero_ref, sem,
                     *, n_token_tiles):
    tt = x_ref.shape[0]
    lc = buf_ref.shape[1]
    ne = N_EXPERTS
    tile = pl.program_id(0) * pl.num_programs(1) + pl.program_id(1)
    last = pl.num_programs(0) * pl.num_programs(1) - 1
    slot = tile % 2

    def copies(t, sl, wait):
        def make_copy(lo, go):
            return pltpu.make_async_copy(buf_ref.at[sl, pl.ds(lo, SEG_ROWS), :],
                                         hs_ref.at[pl.ds(go, SEG_ROWS), :], sem.at[sl])
        _segment_copies(seg_ref, t, ne, make_copy, wait=wait)

    h = _modnorm(x_ref[...], nw_ref[...], sh_ref[...], sc_ref[...]).astype(BF16)
    ld = ld_ref[...]
    rows = lax.broadcasted_iota(jnp.int32, (lc, tt), 0)
    perm = jnp.where(rows == ld[0:1, :], 1.0, jnp.where(rows == ld[1:2, :], 1.0, 0.0)).astype(BF16)
    buf_ref[slot] = jnp.dot(perm, h, preferred_element_type=F32).astype(BF16)
    copies(tile, slot, wait=False)

    @pl.when(tile > 0)
    def _():
        copies(tile - 1, 1 - slot, wait=True)

    @pl.when(tile == last)
    def _():
        copies(tile, slot, wait=True)
        zero_ref[...] = jnp.zeros_like(zero_ref)
        tails = n_token_tiles * ne * SEG_FIELDS
        for wait in (False, True):
            for e in range(ne):
                start = seg_ref[tails + 2 * e]

                def body(i, carry, start=start, wait=wait):
                    cp = pltpu.make_async_copy(
                        zero_ref, hs_ref.at[pl.ds(pl.multiple_of(start + i * SEG_ROWS, SEG_ROWS), SEG_ROWS), :],
                        sem.at[2])
                    if wait:
                        cp.wait()
                    else:
                        cp.start()
                    return carry

                lax.fori_loop(0, seg_ref[tails + 2 * e + 1], body, 0)


def moe_dispatch(x, nw, sh, sc, seg, ldest_rows, n_rows, *, tt, lc):
    bsz, s, d = x.shape
    nt = s // tt
    grid_spec = pltpu.PrefetchScalarGridSpec(
        num_scalar_prefetch=1,
        grid=(bsz, nt),
        in_specs=[pl.BlockSpec((None, tt, d), lambda b, i, sref: (b, i, 0)),
                  pl.BlockSpec((1, d), lambda b, i, sref: (0, 0)),
                  pl.BlockSpec((None, 1, d), lambda b, i, sref: (b, 0, 0)),
                  pl.BlockSpec((None, 1, d), lambda b, i, sref: (b, 0, 0)),
                  pl.BlockSpec((TOP_K, tt), lambda b, i, sref: (0, b * nt + i))],
        out_specs=pl.BlockSpec(memory_space=pl.ANY),
        scratch_shapes=[pltpu.VMEM((2, lc, d), BF16), pltpu.VMEM((SEG_ROWS, d), BF16),
                        pltpu.SemaphoreType.DMA((3,))],
    )
    return pl.pallas_call(
        functools.partial(_dispatch_kernel, n_token_tiles=bsz * nt),
        grid_spec=grid_spec,
        out_shape=jax.ShapeDtypeStruct((n_rows, d), BF16),
        compiler_params=_cparams("arbitrary", "arbitrary"),
        name="moe_dispatch",
    )(seg, x, nw.reshape(1, d), sh, sc, ldest_rows)


def _moe_kernel(te_ref, nu_ref, hs_ref, w1_ref, w3_ref, w2_ref, o_ref, acc_ref):
    i = pl.program_id(0)
    f = pl.program_id(1)

    @pl.when(i < nu_ref[0])
    def _():
        @pl.when(f == 0)
        def _():
            acc_ref[...] = jnp.zeros_like(acc_ref)

        h = hs_ref[...]
        for lo in range(0, w1_ref.shape[1], MOE_HIDDEN_CHUNK):
            hi = lo + MOE_HIDDEN_CHUNK
            a = jnp.dot(h, w1_ref[:, lo:hi], preferred_element_type=F32)
            b = jnp.dot(h, w3_ref[:, lo:hi], preferred_element_type=F32)
            t = (_silu(a) * b).astype(BF16)
            acc_ref[...] += jnp.dot(t, w2_ref[lo:hi, :], preferred_element_type=F32)

        @pl.when(f == pl.num_programs(1) - 1)
        def _():
            o_ref[...] = acc_ref[...].astype(o_ref.dtype)

    @pl.when((i >= nu_ref[0]) & (f == 0))
    def _():
        o_ref[...] = jnp.zeros_like(o_ref)


def moe_experts(hs, tile_expert, n_used, w1, w3, w2, *, tm, tf):
    n_rows, d = hs.shape
    dff = w1.shape[2]
    nf = dff // tf
    n_tiles = n_rows // tm

    def last_used(i, nu):
        return jnp.maximum(jnp.minimum(i, nu[0] - 1), 0)

    def row_map(i, f, te, nu):
        return (last_used(i, nu), 0)

    def hidden_block(i, f, nu):
        t = last_used(i, nu)
        step = jnp.where(i < nu[0], f, nf - 1)
        return jnp.where(t % 2 == 0, step, nf - 1 - step)

    def w_in_map(i, f, te, nu):
        return (te[last_used(i, nu)], 0, hidden_block(i, f, nu))

    def w_out_map(i, f, te, nu):
        return (te[last_used(i, nu)], hidden_block(i, f, nu), 0)

    grid_spec = pltpu.PrefetchScalarGridSpec(
        num_scalar_prefetch=2,
        grid=(n_tiles, nf),
        in_specs=[pl.BlockSpec((tm, d), row_map),
                  pl.BlockSpec((None, d, tf), w_in_map),
                  pl.BlockSpec((None, d, tf), w_in_map),
                  pl.BlockSpec((None, tf, d), w_out_map)],
        out_specs=pl.BlockSpec((tm, d), lambda i, f, te, nu: (i, 0)),
        scratch_shapes=[pltpu.VMEM((tm, d), F32)],
    )
    return pl.pallas_call(
        _moe_kernel,
        grid_spec=grid_spec,
        out_shape=jax.ShapeDtypeStruct((n_rows, d), BF16),
        compiler_params=_cparams("arbitrary", "arbitrary"),
        name="moe_experts",
    )(tile_expert, n_used, hs, w1, w3, w2)


def _combine_kernel(seg_ref, ys_ref, x_ref, g_ref, wgt_ref, ld_ref, o_ref, buf_ref, sem):
    tt = x_ref.shape[0]
    lc = buf_ref.shape[1]
    ne = N_EXPERTS
    tile = pl.program_id(0) * pl.num_programs(1) + pl.program_id(1)
    last = pl.num_programs(0) * pl.num_programs(1) - 1
    slot = tile % 2

    def copies(t, sl, wait):
        def make_copy(lo, go):
            return pltpu.make_async_copy(ys_ref.at[pl.ds(go, SEG_ROWS), :],
                                         buf_ref.at[sl, pl.ds(lo, SEG_ROWS), :], sem.at[sl])
        _segment_copies(seg_ref, t, ne, make_copy, wait=wait)

    @pl.when(tile == 0)
    def _():
        buf_ref[...] = jnp.zeros_like(buf_ref)
        copies(tile, slot, wait=False)

    @pl.when(tile < last)
    def _():
        copies(tile + 1, 1 - slot, wait=False)

    copies(tile, slot, wait=True)

    ld = ld_ref[...]
    cols = lax.broadcasted_iota(jnp.int32, (tt, lc), 1)
    pick = jnp.concatenate([jnp.where(cols == ld[:, k:k + 1], 1.0, 0.0) for k in range(TOP_K)],
                           axis=0).astype(BF16)
    z = jnp.dot(pick, buf_ref[slot], preferred_element_type=F32)
    w = wgt_ref[...]
    mix = w[:, 0:1] * z[0:tt] + w[:, 1:2] * z[tt:]
    o_ref[...] = x_ref[...] + g_ref[...] * mix


def moe_combine(ys, seg, ldest, wgt, x, g, *, tt, lc):
    bsz, s, d = x.shape
    nt = s // tt
    tok_spec = pl.BlockSpec((tt, TOP_K), lambda b, i, sref: (b * nt + i, 0))
    grid_spec = pltpu.PrefetchScalarGridSpec(
        num_scalar_prefetch=1,
        grid=(bsz, nt),
        in_specs=[pl.BlockSpec(memory_space=pl.ANY),
                  pl.BlockSpec((None, tt, d), lambda b, i, sref: (b, i, 0)),
                  pl.BlockSpec((None, 1, d), lambda b, i, sref: (b, 0, 0)),
                  tok_spec, tok_spec],
        out_specs=pl.BlockSpec((None, tt, d), lambda b, i, sref: (b, i, 0)),
        scratch_shapes=[pltpu.VMEM((2, lc, d), BF16), pltpu.SemaphoreType.DMA((2,))],
    )
    return pl.pallas_call(
        _combine_kernel,
        grid_spec=grid_spec,
        out_shape=jax.ShapeDtypeStruct((bsz, s, d), F32),
        compiler_params=_cparams("arbitrary", "arbitrary"),
        name="moe_combine",
    )(seg, ys, x, g, wgt, ldest)


def _round_up(v, m):
    return ((v + m - 1) // m) * m


def moe_residual(x, nw, sh, sc, g, router_w, router_b, w1, w3, w2, *, tm=MOE_TILE_ROWS):
    bsz, s, d = x.shape
    n_tok = bsz * s
    ne = router_w.shape[1]
    tt = min(512, s)
    n_tt = n_tok // tt
    lc = _round_up(TOP_K * tt + ne * SEG_ROWS, 128)
    eidx, rank, wgt, cnt = moe_route(x, nw, sh, sc, router_w, router_b, ts=tt)
    seg_len = _round_up(cnt.reshape(n_tt, ne), SEG_ROWS)
    local_start = jnp.cumsum(seg_len, axis=1) - seg_len
    padded = _round_up(jnp.sum(seg_len, axis=0), tm)
    ends = jnp.cumsum(padded)
    global_start = (ends - padded)[None, :] + jnp.cumsum(seg_len, axis=0) - seg_len
    n_rows = _round_up(n_tok * TOP_K + n_tt * ne * SEG_ROWS + ne * tm, tm)
    used_end = (ends - padded) + jnp.sum(seg_len, axis=0)
    next_start = jnp.concatenate([ends[:-1], jnp.full((1,), n_rows, ends.dtype)])
    tails = jnp.stack([used_end, (next_start - used_end) // SEG_ROWS], axis=-1)
    seg = jnp.concatenate([jnp.stack([local_start, global_start, seg_len // SEG_ROWS], axis=-1).reshape(-1),
                           tails.reshape(-1)]).astype(jnp.int32)
    onehot = eidx[:, :, None] == jnp.arange(ne, dtype=jnp.int32)
    start_tok = jnp.repeat(local_start, tt, axis=0)[:, None, :]
    ldest = (jnp.sum(jnp.where(onehot, start_tok, 0), axis=-1) + rank).astype(jnp.int32)
    n_tiles = n_rows // tm
    tile_start = jnp.arange(n_tiles, dtype=jnp.int32) * tm
    tile_expert = jnp.minimum(jnp.sum(tile_start[:, None] >= ends[None, :], axis=1), ne - 1).astype(jnp.int32)
    n_used = (ends[ne - 1:ne] // tm).astype(jnp.int32)
    hs = moe_dispatch(x, nw, sh, sc, seg, ldest.T, n_rows, tt=tt, lc=lc)
    ys = moe_experts(hs, tile_expert, n_used, w1, w3, w2, tm=tm, tf=w1.shape[2] // 2)
    return moe_combine(ys, seg, ldest, wgt, x, g, tt=tt, lc=lc)


def _split_mod(mod):
    return [m[:, None, :] for m in jnp.split(mod, 6, axis=-1)]


def even_layer(x, c, rel_bias, ada_w, ada_b, norm1_w, in_w, q_norm_w, k_norm_w, sink, out_w,
               norm2_w, w1, w3, w2):
    s = x.shape[1]
    sh1, sc1, g1, sh2, sc2, g2 = _split_mod(ada_mod(c, ada_w, ada_b))
    proj = norm_mod_matmul(x, norm1_w, sh1, sc1, in_w.astype(BF16), ts=min(1024, s), tn=256, name="even_in_proj",
                           out_dtype=BF16)
    yf = fourier_mix(proj, tq=min(512, s))
    ya = window_attention(proj, band_bias(rel_bias), q_norm_w, k_norm_w, sink)
    x = cat_proj_residual(yf, ya, out_w.astype(BF16), x, g1, ts=min(1024, s))
    dff = w1.shape[1]
    return ffn_residual(x, norm2_w, sh2, sc2, g2, w1.astype(BF16), w3.astype(BF16), w2.astype(BF16),
                        ts=min(1024, s), tf=256)


def odd_layer(x, c, ada_w, ada_b, norm1_w, in_w, conv_w, conv_b, dt_bias_f, dt_bias_b, a_log_f, a_log_b,
              d_skip, gnorm_w, out_w, norm2_w, router_w, router_b, w1, w3, w2):
    s = x.shape[1]
    sh1, sc1, g1, sh2, sc2, g2 = _split_mod(ada_mod(c, ada_w, ada_b))
    d_inner = gnorm_w.shape[0]
    cdim = conv_w.shape[1]
    wide = d_inner + cdim
    in_w = in_w.astype(BF16)
    z, xs, bc, dt = ssm_in_proj_conv(x, norm1_w, sh1, sc1, in_w[:, :d_inner], in_w[:, d_inner:wide], in_w[:, wide:],
                                     conv_w, conv_b, n_x=d_inner, ts=min(512, s))
    y = ssd_scan_bidir(xs, bc, dt, dt_bias_f, dt_bias_b, a_log_f, a_log_b, d_skip)
    x = gated_proj_residual(y, z, gnorm_w, out_w.astype(BF16), x, g1, ts=min(512, s))
    return moe_residual(x, norm2_w, sh2, sc2, g2, router_w, router_b,
                        w1.astype(BF16), w3.astype(BF16), w2.astype(BF16))


def kernel(x, c, rel_bias, ev_ada_w, ev_ada_b, ev_norm1_w, ev_in_w, ev_q_norm_w, ev_k_norm_w, ev_sink, ev_out_w, ev_norm2_w, ev_ffn_w1, ev_ffn_w3, ev_ffn_w2, od_ada_w, od_ada_b, od_norm1_w, od_in_w, od_conv_w, od_conv_b, od_dt_bias_f, od_dt_bias_b, od_A_log_f, od_A_log_b, od_D, od_gnorm_w, od_out_w, od_norm2_w, od_router_w, od_router_b, od_moe_w1, od_moe_w3, od_moe_w2):
    depth = ev_ada_w.shape[0] + od_ada_w.shape[0]
    for i in range(depth):
        j = i // 2
        if i % 2 == 0:
            x = even_layer(x, c, rel_bias, ev_ada_w[j], ev_ada_b[j], ev_norm1_w[j], ev_in_w[j],
                           ev_q_norm_w[j], ev_k_norm_w[j], ev_sink[j], ev_out_w[j], ev_norm2_w[j],
                           ev_ffn_w1[j], ev_ffn_w3[j], ev_ffn_w2[j])
        else:
            x = odd_layer(x, c, od_ada_w[j], od_ada_b[j], od_norm1_w[j], od_in_w[j], od_conv_w[j],
                          od_conv_b[j], od_dt_bias_f[j], od_dt_bias_b[j], od_A_log_f[j], od_A_log_b[j],
                          od_D[j], od_gnorm_w[j], od_out_w[j], od_norm2_w[j], od_router_w[j],
                          od_router_b[j], od_moe_w1[j], od_moe_w3[j], od_moe_w2[j])
    return x
```

```python
import functools

import numpy as np
import jax
import jax.numpy as jnp
from jax import lax
from jax.experimental import pallas as pl
from jax.experimental.pallas import tpu as pltpu

F32 = jnp.float32
BF16 = jnp.bfloat16
HIGHEST = lax.Precision.HIGHEST

EPS = 1e-6
FNET_GROUPS = 4
FNET_GROUP_DIM = 128
FNET_WIDTH = FNET_GROUPS * FNET_GROUP_DIM
ATTN_HEADS = 8
ATTN_KV_HEADS = 2
HEAD_DIM = 64
ATTN_WIDTH = ATTN_HEADS * HEAD_DIM
KV_WIDTH = ATTN_KV_HEADS * HEAD_DIM
WINDOW = 128
BLOCK = 128
REL_BUCKETS = 32
REL_MAX_DIST = 128
SSM_HEAD_DIM = 64
SSM_GROUPS = 4
D_STATE = 128
CONV_WIDTH = 5
SSD_CHUNK = 128
N_EXPERTS = 8
TOP_K = 2
NEG_BIG = -1e30
ATTN_QBLOCKS = 2

V7X_VMEM_LIMIT_BYTES = 56 * 1024 * 1024
MOE_TILE_ROWS = 512
MOE_HIDDEN_CHUNK = 256


def _cparams(*sem):
    return pltpu.CompilerParams(dimension_semantics=sem, vmem_limit_bytes=V7X_VMEM_LIMIT_BYTES)


def _modnorm(x, nw, sh, sc):
    ms = jnp.mean(x * x, axis=-1, keepdims=True)
    return x * lax.rsqrt(ms + EPS) * nw * (1.0 + sc) + sh


def _silu(x):
    return x * (1.0 / (1.0 + jnp.exp(-x)))


def _ada_kernel(c_ref, w_ref, b_ref, o_ref):
    cs = _silu(c_ref[...]).astype(BF16)
    o_ref[...] = jnp.dot(cs, w_ref[...].astype(BF16), preferred_element_type=F32) + b_ref[...]


def ada_mod(c, w, b):
    bsz, d = c.shape
    n = w.shape[1]
    tn = 1536
    return pl.pallas_call(
        _ada_kernel,
        grid=(n // tn,),
        in_specs=[pl.BlockSpec((bsz, d), lambda j: (0, 0)),
                  pl.BlockSpec((d, tn), lambda j: (0, j)),
                  pl.BlockSpec((1, tn), lambda j: (0, j))],
        out_specs=pl.BlockSpec((bsz, tn), lambda j: (0, j)),
        out_shape=jax.ShapeDtypeStruct((bsz, n), F32),
        compiler_params=_cparams("arbitrary"),
        name="ada_mod",
    )(c, w, b.reshape(1, n))


def _nmm_kernel(x_ref, nw_ref, sh_ref, sc_ref, w_ref, o_ref, *, tn):
    h = _modnorm(x_ref[...], nw_ref[...], sh_ref[...], sc_ref[...]).astype(BF16)
    n = w_ref.shape[1]
    for lo in range(0, n, tn):
        o_ref[:, lo:lo + tn] = jnp.dot(h, w_ref[:, lo:lo + tn],
                                       preferred_element_type=F32).astype(o_ref.dtype)


def _resident(shape):
    return pl.BlockSpec(shape, lambda *_: tuple(0 for _ in shape), pipeline_mode=pl.Buffered(1))


def norm_mod_matmul(x, nw, sh, sc, w, *, ts, tn, name, out_dtype=F32):
    bsz, s, d = x.shape
    n = w.shape[1]
    assert n % tn == 0
    return pl.pallas_call(
        functools.partial(_nmm_kernel, tn=tn),
        grid=(bsz, s // ts),
        in_specs=[pl.BlockSpec((None, ts, d), lambda b, i: (b, i, 0)),
                  pl.BlockSpec((1, d), lambda b, i: (0, 0)),
                  pl.BlockSpec((None, 1, d), lambda b, i: (b, 0, 0)),
                  pl.BlockSpec((None, 1, d), lambda b, i: (b, 0, 0)),
                  _resident((d, n))],
        out_specs=pl.BlockSpec((None, ts, n), lambda b, i: (b, i, 0)),
        out_shape=jax.ShapeDtypeStruct((bsz, s, n), out_dtype),
        compiler_params=_cparams("parallel", "parallel"),
        name=name,
    )(x, nw.reshape(1, d), sh, sc, w)


def _dft_cos_sin(n):
    k = np.arange(n, dtype=np.int64)
    ang = ((k[:, None] * k[None, :]) % n).astype(np.float64) * (2.0 * np.pi / n)
    scale = 1.0 / np.sqrt(n)
    return np.cos(ang) * scale, np.sin(ang) * scale


def _fourier_kernel(u_ref, chan_ref, seq_ref, o_ref, ab_ref):
    s = u_ref.shape[0]

    @pl.when(pl.program_id(1) == 0)
    def _():
        for g in range(FNET_GROUPS):
            lo, hi = g * FNET_GROUP_DIM, (g + 1) * FNET_GROUP_DIM
            ug = u_ref[:, lo:hi].astype(BF16)
            cs = jnp.dot(ug, chan_ref[...], preferred_element_type=F32)
            ab_ref[0:s, lo:hi] = cs[:, :FNET_GROUP_DIM].astype(BF16)
            ab_ref[s:2 * s, lo:hi] = cs[:, FNET_GROUP_DIM:].astype(BF16)

    o_ref[...] = jnp.dot(seq_ref[...], ab_ref[...], preferred_element_type=F32).astype(o_ref.dtype)


def fourier_mix(proj, *, tq):
    bsz, s, _ = proj.shape
    cc, sc = _dft_cos_sin(FNET_GROUP_DIM)
    chan = jnp.asarray(np.concatenate([cc, sc], axis=1), BF16)
    cs, ss = _dft_cos_sin(s)
    seq = jnp.asarray(np.concatenate([cs, -ss], axis=1), BF16)
    return pl.pallas_call(
        _fourier_kernel,
        grid=(bsz, s // tq),
        in_specs=[pl.BlockSpec((None, s, FNET_WIDTH), lambda b, i: (b, 0, 0)),
                  pl.BlockSpec((FNET_GROUP_DIM, 2 * FNET_GROUP_DIM), lambda b, i: (0, 0)),
                  pl.BlockSpec((tq, 2 * s), lambda b, i: (i, 0))],
        out_specs=pl.BlockSpec((None, tq, FNET_WIDTH), lambda b, i: (b, i, 0)),
        out_shape=jax.ShapeDtypeStruct((bsz, s, FNET_WIDTH), BF16),
        scratch_shapes=[pltpu.VMEM((2 * s, FNET_WIDTH), BF16)],
        compiler_params=_cparams("parallel", "arbitrary"),
        name="fourier_mix",
    )(proj, chan, seq)


def _band_bucket_table():
    i = np.arange(BLOCK)[:, None]
    j = np.arange(3 * BLOCK)[None, :]
    rel = (j - BLOCK) - i
    half = REL_BUCKETS // 2
    max_exact = half // 2
    n = np.abs(rel)
    large = max_exact + (np.log(np.maximum(n, 1) / max_exact)
                         / np.log(REL_MAX_DIST / max_exact) * (half - max_exact)).astype(np.int32)
    large = np.minimum(large, half - 1)
    bucket = (rel > 0).astype(np.int32) * half + np.where(n < max_exact, n, large)
    return np.where(n <= WINDOW, bucket, -1).astype(np.int32)


def _bias_kernel(rb_ref, bucket_ref, o_ref):
    h = pl.program_id(0)
    bucket = bucket_ref[...]
    acc = jnp.full(bucket.shape, NEG_BIG, F32)
    for bkt in range(REL_BUCKETS):
        acc = jnp.where(bucket == bkt, rb_ref[bkt * ATTN_HEADS + h], acc)
    o_ref[...] = acc


def band_bias(rel_bias):
    bucket = jnp.asarray(_band_bucket_table())
    return pl.pallas_call(
        _bias_kernel,
        grid=(ATTN_HEADS,),
        in_specs=[pl.BlockSpec(memory_space=pltpu.SMEM),
                  pl.BlockSpec((BLOCK, 3 * BLOCK), lambda h: (0, 0))],
        out_specs=pl.BlockSpec((None, BLOCK, 3 * BLOCK), lambda h: (h, 0, 0)),
        out_shape=jax.ShapeDtypeStruct((ATTN_HEADS, BLOCK, 3 * BLOCK), F32),
        compiler_params=_cparams("arbitrary"),
        name="band_bias",
    )(rel_bias.reshape(-1), bucket)


def _head_mean_matrix(width):
    m = np.zeros((width, width), np.float32)
    for h in range(width // HEAD_DIM):
        m[h * HEAD_DIM:(h + 1) * HEAD_DIM, h * HEAD_DIM:(h + 1) * HEAD_DIM] = 1.0 / HEAD_DIM
    return m


def _heads_rms(t, mean_mat, w):
    sq = t * t
    hi = sq.astype(BF16)
    lo = (sq - hi.astype(F32)).astype(BF16)
    ms = (jnp.dot(hi, mean_mat, preferred_element_type=F32)
          + jnp.dot(lo, mean_mat, preferred_element_type=F32))
    return t * lax.rsqrt(ms + EPS) * w


def _attn_kernel(sink_ref, q_ref, kl_ref, kc_ref, kr_ref, vl_ref, vc_ref, vr_ref,
                 bias_ref, qnw_ref, knw_ref, qmean_ref, kmean_ref, o_ref):
    n = pl.program_id(1)
    nb = pl.num_programs(1) * ATTN_QBLOCKS
    k = jnp.concatenate([kl_ref[...], kc_ref[...], kr_ref[...]], axis=0).astype(F32)
    v = jnp.concatenate([vl_ref[...], vc_ref[...], vr_ref[...]], axis=0).astype(F32)
    col = lax.broadcasted_iota(jnp.int32, (1, 3 * BLOCK), 1)
    qn = _heads_rms(q_ref[...].astype(F32), qmean_ref[...], qnw_ref[...])
    kn = _heads_rms(k, kmean_ref[...], knw_ref[...])
    low = lax.broadcasted_iota(jnp.int32, (1, 2 * HEAD_DIM), 1) < HEAD_DIM
    kn_sw = pltpu.roll(kn, HEAD_DIM, axis=1)
    v_sw = pltpu.roll(v, HEAD_DIM, axis=1)
    k_dup = [jnp.where(low, kn, kn_sw).astype(BF16), jnp.where(low, kn_sw, kn).astype(BF16)]
    ones = jnp.ones((k.shape[0], 2 * HEAD_DIM), BF16)
    v_ext = [jnp.concatenate([v.astype(BF16), ones], axis=1),
             jnp.concatenate([v_sw.astype(BF16), ones], axis=1)]
    g = ATTN_HEADS // ATTN_KV_HEADS
    for qb in range(ATTN_QBLOCKS):
        blk = n * ATTN_QBLOCKS + qb
        band = slice(qb * BLOCK, (qb + 3) * BLOCK)
        first_key = jnp.where(blk == 0, BLOCK, 0)
        end_key = jnp.where(blk == nb - 1, 2 * BLOCK, 3 * BLOCK)
        outside = (col < first_key) | (col >= end_key)
        pairs = []
        for m in range(ATTN_HEADS // 2):
            j = (2 * m) // g
            qp = qn[qb * BLOCK:(qb + 1) * BLOCK, m * 2 * HEAD_DIM:(m + 1) * 2 * HEAD_DIM]
            res = []
            for idx in range(2):
                h = 2 * m + idx
                qm = jnp.where(low if idx == 0 else jnp.logical_not(low), qp, 0.0).astype(BF16)
                logits = lax.dot_general(qm, k_dup[j][band], (((1,), (1,)), ((), ())),
                                         preferred_element_type=F32)
                logits = jnp.where(outside, NEG_BIG, logits + bias_ref[h])
                sk = sink_ref[h]
                mx = jnp.maximum(jnp.max(logits, axis=-1, keepdims=True), sk)
                p = jnp.exp(logits - mx).astype(BF16)
                r = jnp.dot(p, v_ext[idx if j == 0 else 1 - idx][band], preferred_element_type=F32)
                denom = r[:, 2 * HEAD_DIM:] + jnp.exp(sk - mx)
                res.append(r[:, :2 * HEAD_DIM] / denom)
            pairs.append(jnp.where(low, res[0], res[1]))
        o_ref[qb * BLOCK:(qb + 1) * BLOCK, :] = jnp.concatenate(pairs, axis=-1).astype(o_ref.dtype)


def window_attention(proj, bias, q_norm_w, k_norm_w, sink):
    bsz, s, _ = proj.shape
    nb = s // BLOCK
    qcol = FNET_WIDTH // ATTN_WIDTH
    kcol = (FNET_WIDTH + ATTN_WIDTH) // KV_WIDTH
    vcol = kcol + 1

    qb = ATTN_QBLOCKS
    assert nb % qb == 0

    def kv_specs(col):
        return [pl.BlockSpec((None, BLOCK, KV_WIDTH), lambda b, n: (b, jnp.maximum(n * qb - 1, 0), col)),
                pl.BlockSpec((None, qb * BLOCK, KV_WIDTH), lambda b, n: (b, n, col)),
                pl.BlockSpec((None, BLOCK, KV_WIDTH), lambda b, n: (b, jnp.minimum((n + 1) * qb, nb - 1), col))]

    return pl.pallas_call(
        _attn_kernel,
        grid=(bsz, nb // qb),
        in_specs=[pl.BlockSpec(memory_space=pltpu.SMEM),
                  pl.BlockSpec((None, qb * BLOCK, ATTN_WIDTH), lambda b, n: (b, n, qcol)),
                  *kv_specs(kcol), *kv_specs(vcol),
                  pl.BlockSpec((ATTN_HEADS, BLOCK, 3 * BLOCK), lambda b, n: (0, 0, 0)),
                  pl.BlockSpec((1, ATTN_WIDTH), lambda b, n: (0, 0)),
                  pl.BlockSpec((1, KV_WIDTH), lambda b, n: (0, 0)),
                  pl.BlockSpec((ATTN_WIDTH, ATTN_WIDTH), lambda b, n: (0, 0)),
                  pl.BlockSpec((KV_WIDTH, KV_WIDTH), lambda b, n: (0, 0))],
        out_specs=pl.BlockSpec((None, qb * BLOCK, ATTN_WIDTH), lambda b, n: (b, n, 0)),
        out_shape=jax.ShapeDtypeStruct((bsz, s, ATTN_WIDTH), BF16),
        compiler_params=_cparams("parallel", "arbitrary"),
        name="window_attention",
    )(sink, proj, proj, proj, proj, proj, proj, proj, bias,
      (jnp.tile(q_norm_w, ATTN_HEADS) * (HEAD_DIM ** -0.5)).reshape(1, ATTN_WIDTH),
      jnp.tile(k_norm_w, ATTN_KV_HEADS).reshape(1, KV_WIDTH),
      jnp.asarray(_head_mean_matrix(ATTN_WIDTH), BF16), jnp.asarray(_head_mean_matrix(KV_WIDTH), BF16))


def _cat_proj_kernel(a1_ref, a2_ref, w_ref, x_ref, g_ref, o_ref):
    k1 = a1_ref.shape[1]
    y = jnp.dot(a1_ref[...].astype(BF16), w_ref[0:k1, :], preferred_element_type=F32)
    y = y + jnp.dot(a2_ref[...].astype(BF16), w_ref[k1:, :], preferred_element_type=F32)
    o_ref[...] = x_ref[...] + g_ref[...] * y


def cat_proj_residual(a1, a2, w, x, g, *, ts):
    bsz, s, d = x.shape
    k1, k2 = a1.shape[2], a2.shape[2]
    return pl.pallas_call(
        _cat_proj_kernel,
        grid=(bsz, s // ts),
        in_specs=[pl.BlockSpec((None, ts, k1), lambda b, i: (b, i, 0)),
                  pl.BlockSpec((None, ts, k2), lambda b, i: (b, i, 0)),
                  pl.BlockSpec((k1 + k2, d), lambda b, i: (0, 0)),
                  pl.BlockSpec((None, ts, d), lambda b, i: (b, i, 0)),
                  pl.BlockSpec((None, 1, d), lambda b, i: (b, 0, 0))],
        out_specs=pl.BlockSpec((None, ts, d), lambda b, i: (b, i, 0)),
        out_shape=jax.ShapeDtypeStruct((bsz, s, d), F32),
        compiler_params=_cparams("parallel", "parallel"),
        name="mixer_out_proj",
    )(a1, a2, w, x, g)


def _ffn_kernel(x_ref, nw_ref, sh_ref, sc_ref, g_ref, w1_ref, w3_ref, w2_ref, o_ref, acc_ref, *, tf):
    h = _modnorm(x_ref[...], nw_ref[...], sh_ref[...], sc_ref[...]).astype(BF16)
    dff = w1_ref.shape[1]
    for lo in range(0, dff, tf):
        a = jnp.dot(h, w1_ref[:, lo:lo + tf], preferred_element_type=F32)
        b = jnp.dot(h, w3_ref[:, lo:lo + tf], preferred_element_type=F32)
        t = (_silu(a) * b).astype(BF16)
        contrib = jnp.dot(t, w2_ref[lo:lo + tf, :], preferred_element_type=F32)
        if lo == 0:
            acc_ref[...] = contrib
        else:
            acc_ref[...] += contrib
    o_ref[...] = x_ref[...] + g_ref[...] * acc_ref[...]


def ffn_residual(x, nw, sh, sc, g, w1, w3, w2, *, ts, tf):
    bsz, s, d = x.shape
    dff = w1.shape[1]
    assert dff % tf == 0
    vec = pl.BlockSpec((None, 1, d), lambda b, i: (b, 0, 0))
    return pl.pallas_call(
        functools.partial(_ffn_kernel, tf=tf),
        grid=(bsz, s // ts),
        in_specs=[pl.BlockSpec((None, ts, d), lambda b, i: (b, i, 0)),
                  pl.BlockSpec((1, d), lambda b, i: (0, 0)),
                  vec, vec, vec,
                  _resident((d, dff)), _resident((d, dff)), _resident((dff, d))],
        out_specs=pl.BlockSpec((None, ts, d), lambda b, i: (b, i, 0)),
        out_shape=jax.ShapeDtypeStruct((bsz, s, d), F32),
        scratch_shapes=[pltpu.VMEM((ts, d), F32)],
        compiler_params=_cparams("parallel", "parallel"),
        name="ffn_swiglu",
    )(x, nw.reshape(1, d), sh, sc, g, w1, w3, w2)


CONV_PAD = 8
CONV_ROWS = 256


def _conv_kernel(x_ref, w_ref, b_ref, o_ref, pad_ref):
    s, tc = x_ref.shape
    zeros = jnp.zeros((CONV_PAD, tc), F32)
    pad_ref[0:CONV_PAD, :] = zeros
    pad_ref[CONV_PAD + s:, :] = zeros
    pad_ref[CONV_PAD:CONV_PAD + s, :] = x_ref[...].astype(F32)
    half = CONV_WIDTH // 2
    for r in range(s // CONV_ROWS):
        base = CONV_PAD + r * CONV_ROWS - half
        acc = jnp.zeros((CONV_ROWS, tc), F32) + b_ref[...]
        for kk in range(CONV_WIDTH):
            acc = acc + pad_ref[base + kk:base + kk + CONV_ROWS, :] * w_ref[kk:kk + 1, :]
        o_ref[r * CONV_ROWS:(r + 1) * CONV_ROWS, :] = _silu(acc)


def conv_silu(proj, conv_w, conv_b, *, col0, tc):
    bsz, s, _ = proj.shape
    cdim = conv_w.shape[1]
    cb0 = col0 // tc
    return pl.pallas_call(
        _conv_kernel,
        grid=(bsz, cdim // tc),
        in_specs=[pl.BlockSpec((None, s, tc), lambda b, j: (b, 0, cb0 + j)),
                  pl.BlockSpec((CONV_WIDTH, tc), lambda b, j: (0, j)),
                  pl.BlockSpec((1, tc), lambda b, j: (0, j))],
        out_specs=pl.BlockSpec((None, s, tc), lambda b, j: (b, 0, j)),
        out_shape=jax.ShapeDtypeStruct((bsz, s, cdim), F32),
        scratch_shapes=[pltpu.VMEM((s + 2 * CONV_PAD, tc), F32)],
        compiler_params=_cparams("parallel", "parallel"),
        name="conv_silu",
    )(proj, conv_w, conv_b.reshape(1, cdim))


HALO_ROWS = 16
PROJ_CHUNK = 256


def _proj_conv_kernel(x_ref, xp_ref, xn_ref, nw_ref, sh_ref, sc_ref, wz_ref, wc_ref, wd_ref, cw_ref, cb_ref,
                      z_ref, xs_ref, bc_ref, dt_ref):
    i = pl.program_id(1)
    ts = x_ref.shape[0]
    half = CONV_WIDTH // 2

    def hnorm(ref):
        return _modnorm(ref[...], nw_ref[...], sh_ref[...], sc_ref[...])

    h = hnorm(x_ref).astype(BF16)
    h_prev = jnp.where(i == 0, 0.0, hnorm(xp_ref)).astype(BF16)
    h_next = jnp.where(i == pl.num_programs(1) - 1, 0.0, hnorm(xn_ref)).astype(BF16)
    h_ext = jnp.concatenate([h_prev, h, h_next], axis=0)

    n_x = xs_ref.shape[1]
    rows = h_ext.shape[0]

    def project(lo):
        return jnp.dot(h_ext, wc_ref[:, lo:lo + PROJ_CHUNK], preferred_element_type=F32)

    def z_chunk(lo):
        hi = lo + PROJ_CHUNK
        z_ref[:, lo:hi] = jnp.dot(h, wz_ref[:, lo:hi], preferred_element_type=F32).astype(z_ref.dtype)

    z_los = list(range(0, wz_ref.shape[1], PROJ_CHUNK))
    c_los = list(range(0, wc_ref.shape[1], PROJ_CHUNK))
    p_next = project(c_los[0])
    for n, lo in enumerate(c_los):
        hi = lo + PROJ_CHUNK
        p = p_next
        if n + 1 < len(c_los):
            p_next = project(c_los[n + 1])
        if z_los:
            z_chunk(z_los.pop(0))
        acc = jnp.zeros((ts, PROJ_CHUNK), F32) + cb_ref[:, lo:hi]
        for kk in range(CONV_WIDTH):
            shifted = p if kk == half else pltpu.roll(p, (half - kk) % rows, axis=0)
            acc = acc + shifted[HALO_ROWS:HALO_ROWS + ts, :] * cw_ref[kk:kk + 1, lo:hi]
        out = _silu(acc)
        if lo < n_x:
            xs_ref[:, lo:hi] = out
        else:
            bc_ref[:, lo - n_x:hi - n_x] = out.astype(bc_ref.dtype)
    for lo in z_los:
        z_chunk(lo)
    dt_ref[...] = jnp.dot(h, wd_ref[...], preferred_element_type=F32)


def ssm_in_proj_conv(x, nw, sh, sc, w_z, w_xbc, w_dt, conv_w, conv_b, *, n_x, ts):
    bsz, s, d = x.shape
    dz = w_z.shape[1]
    dc = w_xbc.shape[1]
    ddt = w_dt.shape[1]
    assert ts % HALO_ROWS == 0 and n_x % PROJ_CHUNK == 0 and dc % PROJ_CHUNK == 0 and dz % PROJ_CHUNK == 0
    r = ts // HALO_ROWS
    last = s // HALO_ROWS - 1
    vec = pl.BlockSpec((None, 1, d), lambda b, i: (b, 0, 0))
    return pl.pallas_call(
        _proj_conv_kernel,
        grid=(bsz, s // ts),
        in_specs=[pl.BlockSpec((None, ts, d), lambda b, i: (b, i, 0)),
                  pl.BlockSpec((None, HALO_ROWS, d), lambda b, i: (b, jnp.maximum(i * r - 1, 0), 0)),
                  pl.BlockSpec((None, HALO_ROWS, d), lambda b, i: (b, jnp.minimum((i + 1) * r, last), 0)),
                  pl.BlockSpec((1, d), lambda b, i: (0, 0)), vec, vec,
                  _resident((d, dz)), _resident((d, dc)), _resident((d, ddt)),
                  _resident((CONV_WIDTH, dc)), _resident((1, dc))],
        out_specs=[pl.BlockSpec((None, ts, dz), lambda b, i: (b, i, 0)),
                   pl.BlockSpec((None, ts, n_x), lambda b, i: (b, i, 0)),
                   pl.BlockSpec((None, ts, dc - n_x), lambda b, i: (b, i, 0)),
                   pl.BlockSpec((None, ts, ddt), lambda b, i: (b, i, 0))],
        out_shape=[jax.ShapeDtypeStruct((bsz, s, dz), BF16),
                   jax.ShapeDtypeStruct((bsz, s, n_x), F32),
                   jax.ShapeDtypeStruct((bsz, s, dc - n_x), BF16),
                   jax.ShapeDtypeStruct((bsz, s, ddt), F32)],
        compiler_params=_cparams("parallel", "parallel"),
        name="odd_in_proj_conv",
    )(x, x, x, nw.reshape(1, d), sh, sc, w_z, w_xbc, w_dt, conv_w, conv_b.reshape(1, dc))


def _softplus(x):
    return jnp.maximum(x, 0.0) + jnp.log(1.0 + jnp.exp(-jnp.abs(x)))


def _expand_matrix(n_in, width):
    e = np.zeros((n_in, n_in * width), np.float32)
    for h in range(n_in):
        e[h, h * width:(h + 1) * width] = 1.0
    return e


LOG2E = 1.4426950408889634
DECAY_SLOTS = 12


def _decay_placement(nh):
    place = np.zeros((3, 2 * nh, 3 * 128), np.float32)
    const = np.zeros((1, 3 * 128), np.float32)
    for h in range(nh):
        for part in range(3):
            place[part, h, part * nh + h] = 1.0
            place[part, nh + h, (6 + part) * nh + h] = -1.0
            place[part, h, 128 + (3 + part) * nh + h] = -1.0
            place[part, nh + h, 256 + (9 + part) * nh + h] = 1.0
            const[0, (3 + part) * nh + h] = 1.0
            const[0, (9 + part) * nh + h] = 1.0
            const[0, 128 + part * nh + h] = 1.0
            const[0, 256 + (6 + part) * nh + h] = 1.0
    return place, const


def _split3(v):
    hi = v.astype(BF16)
    r = v - hi.astype(F32)
    mid = r.astype(BF16)
    lo = (r - mid.astype(F32)).astype(BF16)
    return hi, mid, lo


def _ssd_kernel(x_ref, b_ref, c_ref, dtc_f_ref, dtc_b_ref, dtr_f_ref, dtr_b_ref,
                pc_f_ref, pc_b_ref, pr_f_ref, pr_b_ref, dx_ref,
                tri_ref, place_ref, pconst_ref, e64_ref, y_ref,
                ac_ref, dc_ref, dr_ref, pq_ref, xdf_ref, xdb_ref, eif_ref, er_ref, decf_ref, decb_ref,
                hf_ref, hb_ref):
    s = x_ref.shape[0]
    q = SSD_CHUNK
    nh = pc_f_ref.shape[1]
    nc = s // q
    hd = SSM_HEAD_DIM

    def col_params(raw_ref, p_ref):
        dt = _softplus(raw_ref[...] + p_ref[0:1, :])
        return dt, (-LOG2E) * jnp.exp(p_ref[1:2, :]) * dt

    dcf, acf = col_params(dtc_f_ref, pc_f_ref)
    dcb, acb = col_params(dtc_b_ref, pc_b_ref)
    dc_ref[:, 0:nh] = dcf
    dc_ref[:, nh:] = dcb
    ac_ref[:, 0:nh] = acf
    ac_ref[:, nh:] = acb
    dr_ref[0:nh, :] = _softplus(dtr_f_ref[...] + pr_f_ref[:, 0:1])
    dr_ref[nh:, :] = _softplus(dtr_b_ref[...] + pr_b_ref[:, 0:1])

    li = lax.broadcasted_iota(jnp.int32, (q, q), 0)
    si = lax.broadcasted_iota(jnp.int32, (q, q), 1)
    lower = li >= si
    upper = li <= si
    slot_head = lax.broadcasted_iota(jnp.int32, (1, 128), 1) % nh
    pair_lo = lax.broadcasted_iota(jnp.int32, (1, 2 * hd), 1) < hd

    def bdot(a, b):
        return jnp.dot(a, b, preferred_element_type=F32)

    def chunk_cumsums(sl):
        a_col = ac_ref[sl, :]
        tri = tri_ref[...]
        hi, mid, lo = _split3(a_col)
        return a_col, bdot(tri, hi) + bdot(tri, mid) + bdot(tri, lo)

    def expand64(v, parts=2):
        e = e64_ref[...]
        hi = v.astype(BF16)
        out = bdot(hi, e)
        if parts == 2:
            out = out + bdot((v - hi.astype(F32)).astype(BF16), e)
        return out

    hf_ref[...] = jnp.zeros_like(hf_ref)
    hb_ref[...] = jnp.zeros_like(hb_ref)

    def prep_body(c, carry):
        sl = pl.ds(pl.multiple_of(c * q, q), q)
        a_col, i_col = chunk_cumsums(sl)
        d_col = dc_ref[sl, :]
        if_col = i_col[:, 0:nh]
        tot_f = i_col[q - 1:q, 0:nh]
        tot_b = i_col[q - 1:q, nh:]
        ie = i_col - jnp.where(lax.broadcasted_iota(jnp.int32, (1, 2 * nh), 1) < nh, 0.0, a_col)
        eb_col = ie[:, nh:]
        parts = _split3(ie)
        placed = pconst_ref[...]
        for part in range(3):
            placed = placed + bdot(parts[part], place_ref[part])
        pq_ref[sl, :] = placed.astype(BF16)
        xc = x_ref[sl, :]
        xdf_ref[sl, :] = (xc * expand64(jnp.exp2(tot_f - if_col) * d_col[:, 0:nh], parts=1)).astype(BF16)
        xdb_ref[sl, :] = (xc * expand64(jnp.exp2(eb_col) * d_col[:, nh:], parts=1)).astype(BF16)
        eif_ref[sl, :] = expand64(jnp.exp2(if_col))
        er_ref[sl, :] = expand64(jnp.exp2(tot_b - eb_col))
        dec = expand64(jnp.exp2(jnp.concatenate([tot_f, tot_b], axis=0)))
        decf_ref[c] = jnp.broadcast_to(dec[0:1, :], decf_ref.shape[1:])
        decb_ref[c] = jnp.broadcast_to(dec[1:2, :], decb_ref.shape[1:])
        return carry

    lax.fori_loop(0, nc, prep_body, 0, unroll=4)

    def fwd_body(c, carry):
        sl = pl.ds(pl.multiple_of(c * q, q), q)
        xc = x_ref[sl, :]
        bc = b_ref[sl, :].astype(BF16)
        cc = c_ref[sl, :].astype(BF16)
        d_row = dr_ref[:, sl]
        placed = pq_ref[sl, :]
        p_all = placed[:, 0:128]
        q_cat = jnp.concatenate([placed[:, 128:256], placed[:, 256:384]], axis=0)

        cb = lax.dot_general(cc, bc, (((1,), (1,)), ((), ())), preferred_element_type=F32)
        xb = xc.astype(BF16)
        ys = []
        for h0 in range(0, nh, 2):
            ms = []
            for h in (h0, h0 + 1):
                ph = jnp.where(slot_head == h, p_all, jnp.zeros_like(p_all))
                g2 = lax.dot_general(ph, q_cat, (((1,), (1,)), ((), ())), preferred_element_type=F32)
                arg = jnp.where(lower, g2[:, 0:q], g2[:, q:])
                wgt = (jnp.where(lower, d_row[h:h + 1, :], 0.0)
                       + jnp.where(upper, d_row[nh + h:nh + h + 1, :], 0.0))
                ms.append((cb * jnp.exp2(arg) * wgt).astype(BF16))
            xp = xb[:, h0 * hd:(h0 + 2) * hd]
            zero = jnp.zeros_like(xp)
            rhs = jnp.concatenate([jnp.where(pair_lo, xp, zero), jnp.where(pair_lo, zero, xp)], axis=0)
            ys.append(bdot(jnp.concatenate(ms, axis=1), rhs))
        y = jnp.concatenate(ys, axis=1) + dx_ref[...] * xc

        states = lax.dot_general(bc, xdf_ref[sl, :], (((0,), (0,)), ((), ())),
                                 preferred_element_type=F32)
        h_prev = hf_ref[...]
        y = y + bdot(cc, h_prev.astype(BF16)) * eif_ref[sl, :]
        hf_ref[...] = h_prev * decf_ref[c][0:1, :] + states
        y_ref[sl, :] = y
        return carry

    lax.fori_loop(0, nc, fwd_body, 0, unroll=2)

    def bwd_body(t, carry):
        c = nc - 1 - t
        sl = pl.ds(pl.multiple_of(c * q, q), q)
        bc = b_ref[sl, :].astype(BF16)
        cc = c_ref[sl, :].astype(BF16)
        states = lax.dot_general(bc, xdb_ref[sl, :], (((0,), (0,)), ((), ())),
                                 preferred_element_type=F32)
        h_prev = hb_ref[...]
        y_ref[sl, :] += bdot(cc, h_prev.astype(BF16)) * er_ref[sl, :]
        hb_ref[...] = h_prev * decb_ref[c][0:1, :] + states
        return carry

    lax.fori_loop(0, nc, bwd_body, 0, unroll=4)


def _bf16_parts3(v):
    hi = v.astype(BF16).astype(F32)
    r = v - hi
    mid = r.astype(BF16).astype(F32)
    lo = (r - mid).astype(BF16).astype(F32)
    return hi, mid, lo


def _ssd_t_kernel(x_ref, b_ref, c_ref, dt_f_ref, dt_b_ref, p_f_ref, p_b_ref, dx_ref, trio_ref, y_ref,
                  ar_ref, dr_ref, sc_ref, pq_ref, qt_ref, xtb_ref, xdf_ref, xdb_ref, yt_ref, hf_ref, hb_ref):
    s = x_ref.shape[0]
    q = SSD_CHUNK
    nh = p_f_ref.shape[0]
    nc = s // q
    hd = SSM_HEAD_DIM

    def row_params(raw_ref, p_ref):
        dt = _softplus(raw_ref[...] + p_ref[:, 0:1])
        return dt, (-LOG2E) * jnp.exp(p_ref[:, 1:2]) * dt

    dtf, af = row_params(dt_f_ref, p_f_ref)
    dtb, ab = row_params(dt_b_ref, p_b_ref)
    dr_ref[0:nh, :] = dtf
    dr_ref[nh:, :] = dtb
    ar_ref[0:nh, :] = af
    ar_ref[nh:, :] = ab

    li = lax.broadcasted_iota(jnp.int32, (q, q), 0)
    si = lax.broadcasted_iota(jnp.int32, (q, q), 1)
    lower = li >= si
    upper = li <= si
    slot_head = lax.broadcasted_iota(jnp.int32, (1, q), 1) % nh

    def bdot(a, b):
        return jnp.dot(a, b, preferred_element_type=F32)

    def ntdot(a, b):
        return lax.dot_general(a, b, (((1,), (1,)), ((), ())), preferred_element_type=F32)

    def head_rows(v):
        return jnp.concatenate([jnp.broadcast_to(v[h:h + 1, :], (hd, q)) for h in range(nh)], axis=0)

    ones = jnp.ones((nh, q), F32)
    zeros = jnp.zeros((nh, q), F32)

    def prep_body(c, carry):
        sl = pl.ds(pl.multiple_of(c * q, q), q)
        a_row = ar_ref[:, sl]
        d_row = dr_ref[:, sl]
        parts = jnp.concatenate([p.astype(BF16) for p in _bf16_parts3(a_row)], axis=0)
        cs = bdot(parts, trio_ref[...])
        cs = cs[0:2 * nh] + cs[2 * nh:4 * nh] + cs[4 * nh:6 * nh]
        i_f = cs[0:nh, 0:q]
        e_b = cs[nh:, 0:q] - a_row[nh:]
        tot_f = cs[0:nh, q:]
        tot_b = cs[nh:, q:]
        ih, im, il = _bf16_parts3(i_f)
        eh, em, el = _bf16_parts3(e_b)
        pad = [zeros] * (q // nh - DECAY_SLOTS)
        p_t = jnp.concatenate([ih, im, il, ones, ones, ones, -eh, -em, -el, ones, ones, ones] + pad, axis=0)
        qf_t = jnp.concatenate([ones, ones, ones, -ih, -im, -il] + [zeros] * 6 + pad, axis=0)
        qb_t = jnp.concatenate([zeros] * 6 + [ones, ones, ones, eh, em, el] + pad, axis=0)
        pq_ref[sl, :] = p_t.T.astype(BF16)
        qt_ref[c] = jnp.concatenate([qf_t, qb_t], axis=1).astype(BF16)
        sc_ref[0 * nh:1 * nh, sl] = jnp.exp2(i_f)
        sc_ref[1 * nh:2 * nh, sl] = jnp.exp2(tot_b - e_b)
        sc_ref[2 * nh:3 * nh, sl] = jnp.exp2(tot_f)
        sc_ref[3 * nh:4 * nh, sl] = jnp.exp2(tot_b)
        xt = x_ref[sl, :].T
        xtb_ref[:, sl] = xt.astype(BF16)
        xdf_ref[:, sl] = (xt * head_rows(jnp.exp2(tot_f - i_f) * d_row[0:nh])).astype(BF16)
        xdb_ref[:, sl] = (xt * head_rows(jnp.exp2(e_b) * d_row[nh:])).astype(BF16)
        return carry

    lax.fori_loop(0, nc, prep_body, 0, unroll=2)

    hf_ref[...] = jnp.zeros_like(hf_ref)
    hb_ref[...] = jnp.zeros_like(hb_ref)
    zero_half = jnp.zeros((hd, q), BF16)

    def fwd_body(c, carry):
        sl = pl.ds(pl.multiple_of(c * q, q), q)
        bc = b_ref[sl, :].astype(BF16)
        cc = c_ref[sl, :].astype(BF16)
        d_row = dr_ref[:, sl]
        p_all = pq_ref[sl, :]
        q_t = qt_ref[c]
        xtb = xtb_ref[:, sl]
        cb = ntdot(cc, bc)
        g2s = [bdot(jnp.where(slot_head == h, p_all, jnp.zeros_like(p_all)), q_t) for h in range(nh)]
        ms = []
        for h in range(nh):
            arg = jnp.where(lower, g2s[h][:, 0:q], g2s[h][:, q:])
            wgt = (jnp.where(lower, d_row[h:h + 1, :], 0.0)
                   + jnp.where(upper, d_row[nh + h:nh + h + 1, :], 0.0))
            ms.append((cb * jnp.exp2(arg) * wgt).astype(BF16))
        yd = []
        for h0 in range(0, nh, 2):
            lhs = jnp.concatenate(
                [jnp.concatenate([xtb[h0 * hd:(h0 + 1) * hd], zero_half], axis=0),
                 jnp.concatenate([zero_half, xtb[(h0 + 1) * hd:(h0 + 2) * hd]], axis=0)], axis=1)
            yd.append(ntdot(lhs, jnp.concatenate(ms[h0:h0 + 2], axis=1)))
        states = bdot(xdf_ref[:, sl], bc)
        h_prev = hf_ref[...]
        y_off = ntdot(h_prev.astype(BF16), cc) * head_rows(sc_ref[0 * nh:1 * nh, sl])
        hf_ref[...] = h_prev * head_rows(sc_ref[2 * nh:3 * nh, sl]) + states
        yt_ref[:, sl] = jnp.concatenate(yd, axis=0) + y_off
        return carry

    lax.fori_loop(0, nc, fwd_body, 0, unroll=4)

    def bwd_body(t, carry):
        c = nc - 1 - t
        sl = pl.ds(pl.multiple_of(c * q, q), q)
        bc = b_ref[sl, :].astype(BF16)
        cc = c_ref[sl, :].astype(BF16)
        states = bdot(xdb_ref[:, sl], bc)
        h_prev = hb_ref[...]
        y_off = ntdot(h_prev.astype(BF16), cc) * head_rows(sc_ref[1 * nh:2 * nh, sl])
        hb_ref[...] = h_prev * head_rows(sc_ref[3 * nh:4 * nh, sl]) + states
        y_ref[sl, :] = ((yt_ref[:, sl] + y_off).T + dx_ref[...] * x_ref[sl, :]).astype(y_ref.dtype)
        return carry

    lax.fori_loop(0, nc, bwd_body, 0, unroll=4)


def ssd_scan_bidir(xs, bc, dt, dt_bias_f, dt_bias_b, a_log_f, a_log_b, d_skip):
    bsz, s, _ = xs.shape
    nheads = dt.shape[2] // 2
    nh = nheads // SSM_GROUPS
    gw = nh * SSM_HEAD_DIM
    d_inner = nheads * SSM_HEAD_DIM
    q = SSD_CHUNK
    dt_row = jnp.transpose(dt.reshape(bsz, s, 2 * SSM_GROUPS, nh), (0, 2, 3, 1))
    prm = jnp.stack([jnp.concatenate([dt_bias_f, dt_bias_b]), jnp.concatenate([a_log_f, a_log_b])])
    p_row = jnp.transpose(prm.reshape(2, 2 * SSM_GROUPS, nh), (1, 2, 0))
    dx = jnp.repeat(d_skip, SSM_HEAD_DIM).reshape(SSM_GROUPS, 1, gw)
    assert DECAY_SLOTS * nh <= q and q % nh == 0 and D_STATE == q
    trio = jnp.asarray(np.concatenate([np.triu(np.ones((q, q), np.float32)), np.ones((q, q), np.float32)],
                                      axis=1), BF16)
    bcol = d_inner // D_STATE
    G = SSM_GROUPS
    nc = s // q

    return pl.pallas_call(
        _ssd_t_kernel,
        grid=(bsz, SSM_GROUPS),
        in_specs=[pl.BlockSpec((None, s, gw), lambda b, g: (b, 0, g)),
                  pl.BlockSpec((None, s, D_STATE), lambda b, g: (b, 0, g)),
                  pl.BlockSpec((None, s, D_STATE), lambda b, g: (b, 0, G + g)),
                  pl.BlockSpec((None, None, nh, s), lambda b, g: (b, g, 0, 0)),
                  pl.BlockSpec((None, None, nh, s), lambda b, g: (b, G + g, 0, 0)),
                  pl.BlockSpec((None, nh, 2), lambda b, g: (g, 0, 0)),
                  pl.BlockSpec((None, nh, 2), lambda b, g: (G + g, 0, 0)),
                  pl.BlockSpec((None, 1, gw), lambda b, g: (g, 0, 0)),
                  pl.BlockSpec((q, 2 * q), lambda b, g: (0, 0))],
        out_specs=pl.BlockSpec((None, s, gw), lambda b, g: (b, 0, g)),
        out_shape=jax.ShapeDtypeStruct((bsz, s, d_inner), BF16),
        scratch_shapes=[pltpu.VMEM((2 * nh, s), F32), pltpu.VMEM((2 * nh, s), F32),
                        pltpu.VMEM((4 * nh, s), F32),
                        pltpu.VMEM((s, q), BF16), pltpu.VMEM((nc, q, 2 * q), BF16),
                        pltpu.VMEM((gw, s), BF16), pltpu.VMEM((gw, s), BF16), pltpu.VMEM((gw, s), BF16),
                        pltpu.VMEM((gw, s), F32),
                        pltpu.VMEM((gw, D_STATE), F32), pltpu.VMEM((gw, D_STATE), F32)],
        compiler_params=_cparams("parallel", "parallel"),
        name="ssd_scan",
    )(xs, bc, bc, dt_row, dt_row, p_row, p_row, dx, trio)


def _gated_proj_route_kernel(y_ref, z_ref, gw_ref, w_ref, x_ref, g_ref,
                             nw_ref, sh_ref, sc_ref, rw_ref, rb_ref, lt_ref,
                             o_ref, ld_ref, wgt_ref, cnt_ref, acc_ref):
    _gated_proj_kernel(y_ref, z_ref, gw_ref, w_ref, x_ref, g_ref, o_ref, acc_ref)
    h = _modnorm(o_ref[...], nw_ref[...], sh_ref[...], sc_ref[...])
    rows, gates, total = _route_tile(h, rw_ref[...], rb_ref[...], lt_ref[...])
    ld_ref[...] = rows
    wgt_ref[...] = gates
    cnt_ref[...] = total[:, 0:cnt_ref.shape[1]]


def _gated_proj_kernel(y_ref, z_ref, gw_ref, w_ref, x_ref, g_ref, o_ref, acc_ref):
    k = y_ref.shape[1]
    ss = jnp.zeros((y_ref.shape[0], 1), F32)
    for lo in range(0, k, PROJ_CHUNK):
        hi = lo + PROJ_CHUNK
        t = y_ref[:, lo:hi].astype(F32) * _silu(z_ref[:, lo:hi].astype(F32))
        ss = ss + jnp.sum(t * t, axis=-1, keepdims=True)
        contrib = jnp.dot((t * gw_ref[:, lo:hi]).astype(BF16), w_ref[lo:hi, :], preferred_element_type=F32)
        if lo == 0:
            acc_ref[...] = contrib
        else:
            acc_ref[...] += contrib
    o_ref[...] = x_ref[...] + g_ref[...] * (acc_ref[...] * lax.rsqrt(ss * (1.0 / k) + EPS))


def gated_proj_route(y, z, gw, w, x, g, nw, sh, sc, router_w, router_b, *, ts):
    bsz, s, d = x.shape
    k = y.shape[2]
    ne = router_w.shape[1]
    nt = s // ts
    cum = jnp.asarray(np.concatenate([np.triu(np.ones((ts, ts), np.float32)), np.ones((ts, ts), np.float32)],
                                     axis=1), BF16)
    vec = pl.BlockSpec((None, 1, d), lambda b, i: (b, 0, 0))
    tok_spec = pl.BlockSpec((TOP_K, ts), lambda b, i: (0, b * nt + i))
    call = pl.pallas_call(
        _gated_proj_route_kernel,
        grid=(bsz, nt),
        in_specs=[pl.BlockSpec((None, ts, k), lambda b, i: (b, i, 0)),
                  pl.BlockSpec((None, ts, k), lambda b, i: (b, i, 0)),
                  pl.BlockSpec((1, k), lambda b, i: (0, 0)),
                  _resident((k, d)),
                  pl.BlockSpec((None, ts, d), lambda b, i: (b, i, 0)),
                  vec,
                  pl.BlockSpec((1, d), lambda b, i: (0, 0)), vec, vec,
                  _resident((ne, d)), pl.BlockSpec((ne, 1), lambda b, i: (0, 0)), _resident((ts, 2 * ts))],
        out_specs=[pl.BlockSpec((None, ts, d), lambda b, i: (b, i, 0)), tok_spec, tok_spec,
                   pl.BlockSpec((None, ne, 128), lambda b, i: (b * nt + i, 0, 0))],
        out_shape=[jax.ShapeDtypeStruct((bsz, s, d), F32),
                   jax.ShapeDtypeStruct((TOP_K, bsz * s), jnp.int32),
                   jax.ShapeDtypeStruct((TOP_K, bsz * s), F32),
                   jax.ShapeDtypeStruct((bsz * nt, ne, 128), jnp.int32)],
        scratch_shapes=[pltpu.VMEM((ts, d), F32)],
        compiler_params=_cparams("parallel", "parallel"),
        name="ssd_out_proj_route",
    )
    x_new, ldest, wgt, cnt = call(y, z, gw.reshape(1, k), w, x, g, nw.reshape(1, d), sh, sc,
                                  router_w.T, router_b.reshape(ne, 1), cum)
    return x_new, ldest, wgt, cnt[:, :, 0]


SEG_ROWS = 16
SEG_FIELDS = 3


def _route_tile(h, rw_t, rb_col, cum):
    def nt(a, b):
        return lax.dot_general(a, b, (((1,), (1,)), ((), ())), preferred_element_type=F32)

    h_hi = h.astype(BF16)
    h_lo = (h - h_hi.astype(F32)).astype(BF16)
    rw_hi = rw_t.astype(BF16)
    rw_lo = (rw_t - rw_hi.astype(F32)).astype(BF16)
    logits = nt(rw_hi, h_hi) + (nt(rw_hi, h_lo) + nt(rw_lo, h_hi)) + rb_col
    ne, ts = logits.shape
    eid = lax.broadcasted_iota(jnp.int32, (ne, ts), 0)
    m1 = jnp.max(logits, axis=0, keepdims=True)
    i1 = jnp.min(jnp.where(logits == m1, eid, ne), axis=0, keepdims=True)
    rest = jnp.where(eid == i1, -jnp.inf, logits)
    m2 = jnp.max(rest, axis=0, keepdims=True)
    i2 = jnp.min(jnp.where(rest == m2, eid, ne), axis=0, keepdims=True)
    e2 = jnp.exp(m2 - m1)
    w1 = 1.0 / (1.0 + e2)
    w2 = e2 / (1.0 + e2)
    oh1 = (eid == i1).astype(F32)
    oh2 = (eid == i2).astype(F32)
    chosen = oh1 + oh2
    both = jnp.dot(chosen.astype(BF16), cum, preferred_element_type=F32)
    before = both[:, 0:ts] - chosen
    total = both[:, ts:].astype(jnp.int32)
    seg_len = jnp.bitwise_and(total + (SEG_ROWS - 1), -SEG_ROWS).astype(F32)
    rows = [jnp.sum(jnp.where(eid < idx, seg_len, 0.0) + onehot * before, axis=0, keepdims=True)
            for onehot, idx in ((oh1, i1), (oh2, i2))]
    return jnp.concatenate(rows, axis=0).astype(jnp.int32), jnp.concatenate([w1, w2], axis=0), total


def _segment_copies(seg_ref, tile, n_experts, make_copy, *, wait):
    for e in range(n_experts):
        base = (tile * n_experts + e) * SEG_FIELDS
        local0 = seg_ref[base]
        global0 = seg_ref[base + 1]

        def body(i, carry, local0=local0, global0=global0):
            cp = make_copy(pl.multiple_of(local0 + i * SEG_ROWS, SEG_ROWS),
                           pl.multiple_of(global0 + i * SEG_ROWS, SEG_ROWS))
            if wait:
                cp.wait()
            else:
                cp.start()
            return carry

        lax.fori_loop(0, seg_ref[base + 2], body, 0)


def _dispatch_kernel(seg_ref, x_ref, nw_ref, sh_ref, sc_ref, ld_ref, hs_ref, buf_ref, zero_ref, sem,
                     *, n_token_tiles):
    tt = x_ref.shape[0]
    lc = buf_ref.shape[1]
    ne = N_EXPERTS
    tile = pl.program_id(0) * pl.num_programs(1) + pl.program_id(1)
    last = pl.num_programs(0) * pl.num_programs(1) - 1
    slot = tile % 2

    def copies(t, sl, wait):
        def make_copy(lo, go):
            return pltpu.make_async_copy(buf_ref.at[sl, pl.ds(lo, SEG_ROWS), :],
                                         hs_ref.at[pl.ds(go, SEG_ROWS), :], sem.at[sl])
        _segment_copies(seg_ref, t, ne, make_copy, wait=wait)

    h = _modnorm(x_ref[...], nw_ref[...], sh_ref[...], sc_ref[...]).astype(BF16)
    ld = ld_ref[...]
    rows = lax.broadcasted_iota(jnp.int32, (lc, tt), 0)
    perm = jnp.where(rows == ld[0:1, :], 1.0, jnp.where(rows == ld[1:2, :], 1.0, 0.0)).astype(BF16)
    buf_ref[slot] = jnp.dot(perm, h, preferred_element_type=F32).astype(BF16)
    copies(tile, slot, wait=False)

    @pl.when(tile > 0)
    def _():
        copies(tile - 1, 1 - slot, wait=True)

    @pl.when(tile == last)
    def _():
        copies(tile, slot, wait=True)
        zero_ref[...] = jnp.zeros_like(zero_ref)
        tails = n_token_tiles * ne * SEG_FIELDS
        for wait in (False, True):
            for e in range(ne):
                start = seg_ref[tails + 2 * e]

                def body(i, carry, start=start, wait=wait):
                    cp = pltpu.make_async_copy(
                        zero_ref, hs_ref.at[pl.ds(pl.multiple_of(start + i * SEG_ROWS, SEG_ROWS), SEG_ROWS), :],
                        sem.at[2])
                    if wait:
                        cp.wait()
                    else:
                        cp.start()
                    return carry

                lax.fori_loop(0, seg_ref[tails + 2 * e + 1], body, 0)


def moe_dispatch(x, nw, sh, sc, seg, ldest_rows, n_rows, *, tt, lc):
    bsz, s, d = x.shape
    nt = s // tt
    grid_spec = pltpu.PrefetchScalarGridSpec(
        num_scalar_prefetch=1,
        grid=(bsz, nt),
        in_specs=[pl.BlockSpec((None, tt, d), lambda b, i, sref: (b, i, 0)),
                  pl.BlockSpec((1, d), lambda b, i, sref: (0, 0)),
                  pl.BlockSpec((None, 1, d), lambda b, i, sref: (b, 0, 0)),
                  pl.BlockSpec((None, 1, d), lambda b, i, sref: (b, 0, 0)),
                  pl.BlockSpec((TOP_K, tt), lambda b, i, sref: (0, b * nt + i))],
        out_specs=pl.BlockSpec(memory_space=pl.ANY),
        scratch_shapes=[pltpu.VMEM((2, lc, d), BF16), pltpu.VMEM((SEG_ROWS, d), BF16),
                        pltpu.SemaphoreType.DMA((3,))],
    )
    return pl.pallas_call(
        functools.partial(_dispatch_kernel, n_token_tiles=bsz * nt),
        grid_spec=grid_spec,
        out_shape=jax.ShapeDtypeStruct((n_rows, d), BF16),
        compiler_params=_cparams("arbitrary", "arbitrary"),
        name="moe_dispatch",
    )(seg, x, nw.reshape(1, d), sh, sc, ldest_rows)


def _moe_kernel(te_ref, nu_ref, hs_ref, w1_ref, w3_ref, w2_ref, o_ref, acc_ref):
    i = pl.program_id(0)
    f = pl.program_id(1)

    @pl.when(i < nu_ref[0])
    def _():
        @pl.when(f == 0)
        def _():
            acc_ref[...] = jnp.zeros_like(acc_ref)

        h = hs_ref[...]
        for lo in range(0, w1_ref.shape[1], MOE_HIDDEN_CHUNK):
            hi = lo + MOE_HIDDEN_CHUNK
            a = jnp.dot(h, w1_ref[:, lo:hi], preferred_element_type=F32)
            b = jnp.dot(h, w3_ref[:, lo:hi], preferred_element_type=F32)
            t = (_silu(a) * b).astype(BF16)
            acc_ref[...] += jnp.dot(t, w2_ref[lo:hi, :], preferred_element_type=F32)

        @pl.when(f == pl.num_programs(1) - 1)
        def _():
            o_ref[...] = acc_ref[...].astype(o_ref.dtype)

    @pl.when((i >= nu_ref[0]) & (f == 0))
    def _():
        o_ref[...] = jnp.zeros_like(o_ref)


def moe_experts(hs, tile_expert, n_used, w1, w3, w2, *, tm, tf):
    n_rows, d = hs.shape
    dff = w1.shape[2]
    nf = dff // tf
    n_tiles = n_rows // tm

    def last_used(i, nu):
        return jnp.maximum(jnp.minimum(i, nu[0] - 1), 0)

    def row_map(i, f, te, nu):
        return (last_used(i, nu), 0)

    def hidden_block(i, f, nu):
        t = last_used(i, nu)
        step = jnp.where(i < nu[0], f, nf - 1)
        return jnp.where(t % 2 == 0, step, nf - 1 - step)

    def w_in_map(i, f, te, nu):
        return (te[last_used(i, nu)], 0, hidden_block(i, f, nu))

    def w_out_map(i, f, te, nu):
        return (te[last_used(i, nu)], hidden_block(i, f, nu), 0)

    grid_spec = pltpu.PrefetchScalarGridSpec(
        num_scalar_prefetch=2,
        grid=(n_tiles, nf),
        in_specs=[pl.BlockSpec((tm, d), row_map),
                  pl.BlockSpec((None, d, tf), w_in_map),
                  pl.BlockSpec((None, d, tf), w_in_map),
                  pl.BlockSpec((None, tf, d), w_out_map)],
        out_specs=pl.BlockSpec((tm, d), lambda i, f, te, nu: (i, 0)),
        scratch_shapes=[pltpu.VMEM((tm, d), F32)],
    )
    return pl.pallas_call(
        _moe_kernel,
        grid_spec=grid_spec,
        out_shape=jax.ShapeDtypeStruct((n_rows, d), BF16),
        compiler_params=_cparams("arbitrary", "arbitrary"),
        name="moe_experts",
    )(tile_expert, n_used, hs, w1, w3, w2)


def _combine_kernel(seg_ref, ys_ref, x_ref, g_ref, wgt_ref, ld_ref, o_ref, buf_ref, sem):
    tt = x_ref.shape[0]
    lc = buf_ref.shape[1]
    ne = N_EXPERTS
    tile = pl.program_id(0) * pl.num_programs(1) + pl.program_id(1)
    last = pl.num_programs(0) * pl.num_programs(1) - 1
    slot = tile % 2

    def copies(t, sl, wait):
        def make_copy(lo, go):
            return pltpu.make_async_copy(ys_ref.at[pl.ds(go, SEG_ROWS), :],
                                         buf_ref.at[sl, pl.ds(lo, SEG_ROWS), :], sem.at[sl])
        _segment_copies(seg_ref, t, ne, make_copy, wait=wait)

    @pl.when(tile == 0)
    def _():
        buf_ref[...] = jnp.zeros_like(buf_ref)
        copies(tile, slot, wait=False)

    @pl.when(tile < last)
    def _():
        copies(tile + 1, 1 - slot, wait=False)

    copies(tile, slot, wait=True)

    ld = ld_ref[...]
    cols = lax.broadcasted_iota(jnp.int32, (tt, lc), 1)
    pick = jnp.concatenate([jnp.where(cols == ld[:, k:k + 1], 1.0, 0.0) for k in range(TOP_K)],
                           axis=0).astype(BF16)
    z = jnp.dot(pick, buf_ref[slot], preferred_element_type=F32)
    w = wgt_ref[...]
    mix = w[:, 0:1] * z[0:tt] + w[:, 1:2] * z[tt:]
    o_ref[...] = x_ref[...] + g_ref[...] * mix


def moe_combine(ys, seg, ldest, wgt, x, g, *, tt, lc):
    bsz, s, d = x.shape
    nt = s // tt
    tok_spec = pl.BlockSpec((tt, TOP_K), lambda b, i, sref: (b * nt + i, 0))
    grid_spec = pltpu.PrefetchScalarGridSpec(
        num_scalar_prefetch=1,
        grid=(bsz, nt),
        in_specs=[pl.BlockSpec(memory_space=pl.ANY),
                  pl.BlockSpec((None, tt, d), lambda b, i, sref: (b, i, 0)),
                  pl.BlockSpec((None, 1, d), lambda b, i, sref: (b, 0, 0)),
                  tok_spec, tok_spec],
        out_specs=pl.BlockSpec((None, tt, d), lambda b, i, sref: (b, i, 0)),
        scratch_shapes=[pltpu.VMEM((2, lc, d), BF16), pltpu.SemaphoreType.DMA((2,))],
    )
    return pl.pallas_call(
        _combine_kernel,
        grid_spec=grid_spec,
        out_shape=jax.ShapeDtypeStruct((bsz, s, d), F32),
        compiler_params=_cparams("arbitrary", "arbitrary"),
        name="moe_combine",
    )(seg, ys, x, g, wgt, ldest)


def _round_up(v, m):
    return ((v + m - 1) // m) * m


def moe_token_tile(s):
    return min(512, s)


def moe_residual(x, nw, sh, sc, g, ldest, wgt, cnt, w1, w3, w2, *, tm=MOE_TILE_ROWS):
    bsz, s, d = x.shape
    n_tok = bsz * s
    ne = w1.shape[0]
    tt = moe_token_tile(s)
    n_tt = n_tok // tt
    lc = _round_up(TOP_K * tt + ne * SEG_ROWS, 128)
    seg_len = _round_up(cnt.reshape(n_tt, ne), SEG_ROWS)
    local_start = jnp.cumsum(seg_len, axis=1) - seg_len
    padded = _round_up(jnp.sum(seg_len, axis=0), tm)
    ends = jnp.cumsum(padded)
    global_start = (ends - padded)[None, :] + jnp.cumsum(seg_len, axis=0) - seg_len
    n_rows = _round_up(n_tok * TOP_K + n_tt * ne * SEG_ROWS + ne * tm, tm)
    used_end = (ends - padded) + jnp.sum(seg_len, axis=0)
    next_start = jnp.concatenate([ends[:-1], jnp.full((1,), n_rows, ends.dtype)])
    tails = jnp.stack([used_end, (next_start - used_end) // SEG_ROWS], axis=-1)
    seg = jnp.concatenate([jnp.stack([local_start, global_start, seg_len // SEG_ROWS], axis=-1).reshape(-1),
                           tails.reshape(-1)]).astype(jnp.int32)
    n_tiles = n_rows // tm
    tile_start = jnp.arange(n_tiles, dtype=jnp.int32) * tm
    tile_expert = jnp.minimum(jnp.sum(tile_start[:, None] >= ends[None, :], axis=1), ne - 1).astype(jnp.int32)
    n_used = (ends[ne - 1:ne] // tm).astype(jnp.int32)
    hs = moe_dispatch(x, nw, sh, sc, seg, ldest, n_rows, tt=tt, lc=lc)
    ys = moe_experts(hs, tile_expert, n_used, w1, w3, w2, tm=tm, tf=w1.shape[2] // 2)
    return moe_combine(ys, seg, ldest.T, wgt.T, x, g, tt=tt, lc=lc)


def _split_mod(mod):
    return [m[:, None, :] for m in jnp.split(mod, 6, axis=-1)]


def even_layer(x, c, rel_bias, ada_w, ada_b, norm1_w, in_w, q_norm_w, k_norm_w, sink, out_w,
               norm2_w, w1, w3, w2):
    s = x.shape[1]
    sh1, sc1, g1, sh2, sc2, g2 = _split_mod(ada_mod(c, ada_w, ada_b))
    proj = norm_mod_matmul(x, norm1_w, sh1, sc1, in_w.astype(BF16), ts=min(1024, s), tn=256, name="even_in_proj",
                           out_dtype=BF16)
    yf = fourier_mix(proj, tq=min(512, s))
    ya = window_attention(proj, band_bias(rel_bias), q_norm_w, k_norm_w, sink)
    x = cat_proj_residual(yf, ya, out_w.astype(BF16), x, g1, ts=min(1024, s))
    dff = w1.shape[1]
    return ffn_residual(x, norm2_w, sh2, sc2, g2, w1.astype(BF16), w3.astype(BF16), w2.astype(BF16),
                        ts=min(1024, s), tf=256)


def odd_layer(x, c, ada_w, ada_b, norm1_w, in_w, conv_w, conv_b, dt_bias_f, dt_bias_b, a_log_f, a_log_b,
              d_skip, gnorm_w, out_w, norm2_w, router_w, router_b, w1, w3, w2):
    s = x.shape[1]
    sh1, sc1, g1, sh2, sc2, g2 = _split_mod(ada_mod(c, ada_w, ada_b))
    d_inner = gnorm_w.shape[0]
    cdim = conv_w.shape[1]
    wide = d_inner + cdim
    in_w = in_w.astype(BF16)
    z, xs, bc, dt = ssm_in_proj_conv(x, norm1_w, sh1, sc1, in_w[:, :d_inner], in_w[:, d_inner:wide], in_w[:, wide:],
                                     conv_w, conv_b, n_x=d_inner, ts=min(512, s))
    y = ssd_scan_bidir(xs, bc, dt, dt_bias_f, dt_bias_b, a_log_f, a_log_b, d_skip)
    x, ldest, wgt, cnt = gated_proj_route(y, z, gnorm_w, out_w.astype(BF16), x, g1, norm2_w, sh2, sc2,
                                          router_w, router_b, ts=moe_token_tile(s))
    return moe_residual(x, norm2_w, sh2, sc2, g2, ldest, wgt, cnt,
                        w1.astype(BF16), w3.astype(BF16), w2.astype(BF16))


def kernel(x, c, rel_bias, ev_ada_w, ev_ada_b, ev_norm1_w, ev_in_w, ev_q_norm_w, ev_k_norm_w, ev_sink, ev_out_w, ev_norm2_w, ev_ffn_w1, ev_ffn_w3, ev_ffn_w2, od_ada_w, od_ada_b, od_norm1_w, od_in_w, od_conv_w, od_conv_b, od_dt_bias_f, od_dt_bias_b, od_A_log_f, od_A_log_b, od_D, od_gnorm_w, od_out_w, od_norm2_w, od_router_w, od_router_b, od_moe_w1, od_moe_w3, od_moe_w2):
    depth = ev_ada_w.shape[0] + od_ada_w.shape[0]
    for i in range(depth):
        j = i // 2
        if i % 2 == 0:
            x = even_layer(x, c, rel_bias, ev_ada_w[j], ev_ada_b[j], ev_norm1_w[j], ev_in_w[j],
                           ev_q_norm_w[j], ev_k_norm_w[j], ev_sink[j], ev_out_w[j], ev_norm2_w[j],
                           ev_ffn_w1[j], ev_ffn_w3[j], ev_ffn_w2[j])
        else:
            x = odd_layer(x, c, od_ada_w[j], od_ada_b[j], od_norm1_w[j], od_in_w[j], od_conv_w[j],
                          od_conv_b[j], od_dt_bias_f[j], od_dt_bias_b[j], od_A_log_f[j], od_A_log_b[j],
                          od_D[j], od_gnorm_w[j], od_out_w[j], od_norm2_w[j], od_router_w[j],
                          od_router_b[j], od_moe_w1[j], od_moe_w3[j], od_moe_w2[j])
    return x
```

```python
import functools

import numpy as np
import jax
import jax.numpy as jnp
from jax import lax
from jax.experimental import pallas as pl
from jax.experimental.pallas import tpu as pltpu

F32 = jnp.float32
BF16 = jnp.bfloat16
HIGHEST = lax.Precision.HIGHEST

EPS = 1e-6
FNET_GROUPS = 4
FNET_GROUP_DIM = 128
FNET_WIDTH = FNET_GROUPS * FNET_GROUP_DIM
ATTN_HEADS = 8
ATTN_KV_HEADS = 2
HEAD_DIM = 64
ATTN_WIDTH = ATTN_HEADS * HEAD_DIM
KV_WIDTH = ATTN_KV_HEADS * HEAD_DIM
WINDOW = 128
BLOCK = 128
REL_BUCKETS = 32
REL_MAX_DIST = 128
SSM_HEAD_DIM = 64
SSM_GROUPS = 4
D_STATE = 128
CONV_WIDTH = 5
SSD_CHUNK = 128
N_EXPERTS = 8
TOP_K = 2
NEG_BIG = -1e30
ATTN_QBLOCKS = 2

V7X_VMEM_LIMIT_BYTES = 56 * 1024 * 1024
MOE_TILE_ROWS = 512
MOE_HIDDEN_CHUNK = 256


def _cparams(*sem):
    return pltpu.CompilerParams(dimension_semantics=sem, vmem_limit_bytes=V7X_VMEM_LIMIT_BYTES)


def _modnorm(x, nw, sh, sc):
    ms = jnp.mean(x * x, axis=-1, keepdims=True)
    return x * lax.rsqrt(ms + EPS) * nw * (1.0 + sc) + sh


def _silu(x):
    return x * (1.0 / (1.0 + jnp.exp(-x)))


def _ada_kernel(c_ref, w_ref, b_ref, o_ref):
    cs = _silu(c_ref[...]).astype(BF16)
    o_ref[...] = jnp.dot(cs, w_ref[...].astype(BF16), preferred_element_type=F32) + b_ref[...]


def ada_mod(c, w, b):
    bsz, d = c.shape
    n = w.shape[1]
    tn = 1536
    return pl.pallas_call(
        _ada_kernel,
        grid=(n // tn,),
        in_specs=[pl.BlockSpec((bsz, d), lambda j: (0, 0)),
                  pl.BlockSpec((d, tn), lambda j: (0, j)),
                  pl.BlockSpec((1, tn), lambda j: (0, j))],
        out_specs=pl.BlockSpec((bsz, tn), lambda j: (0, j)),
        out_shape=jax.ShapeDtypeStruct((bsz, n), F32),
        compiler_params=_cparams("arbitrary"),
        name="ada_mod",
    )(c, w, b.reshape(1, n))


def _nmm_kernel(x_ref, nw_ref, sh_ref, sc_ref, w_ref, o_ref, *, tn):
    h = _modnorm(x_ref[...], nw_ref[...], sh_ref[...], sc_ref[...]).astype(BF16)
    n = w_ref.shape[1]
    for lo in range(0, n, tn):
        o_ref[:, lo:lo + tn] = jnp.dot(h, w_ref[:, lo:lo + tn],
                                       preferred_element_type=F32).astype(o_ref.dtype)


def _resident(shape):
    return pl.BlockSpec(shape, lambda *_: tuple(0 for _ in shape), pipeline_mode=pl.Buffered(1))


def norm_mod_matmul(x, nw, sh, sc, w, *, ts, tn, name, out_dtype=F32):
    bsz, s, d = x.shape
    n = w.shape[1]
    assert n % tn == 0
    return pl.pallas_call(
        functools.partial(_nmm_kernel, tn=tn),
        grid=(bsz, s // ts),
        in_specs=[pl.BlockSpec((None, ts, d), lambda b, i: (b, i, 0)),
                  pl.BlockSpec((1, d), lambda b, i: (0, 0)),
                  pl.BlockSpec((None, 1, d), lambda b, i: (b, 0, 0)),
                  pl.BlockSpec((None, 1, d), lambda b, i: (b, 0, 0)),
                  _resident((d, n))],
        out_specs=pl.BlockSpec((None, ts, n), lambda b, i: (b, i, 0)),
        out_shape=jax.ShapeDtypeStruct((bsz, s, n), out_dtype),
        compiler_params=_cparams("parallel", "parallel"),
        name=name,
    )(x, nw.reshape(1, d), sh, sc, w)


def _dft_cos_sin(n):
    k = np.arange(n, dtype=np.int64)
    ang = ((k[:, None] * k[None, :]) % n).astype(np.float64) * (2.0 * np.pi / n)
    scale = 1.0 / np.sqrt(n)
    return np.cos(ang) * scale, np.sin(ang) * scale


def _fourier_kernel(u_ref, chan_ref, seq_ref, o_ref, ab_ref):
    s = u_ref.shape[0]

    @pl.when(pl.program_id(1) == 0)
    def _():
        for g in range(FNET_GROUPS):
            lo, hi = g * FNET_GROUP_DIM, (g + 1) * FNET_GROUP_DIM
            ug = u_ref[:, lo:hi].astype(BF16)
            cs = jnp.dot(ug, chan_ref[...], preferred_element_type=F32)
            ab_ref[0:s, lo:hi] = cs[:, :FNET_GROUP_DIM].astype(BF16)
            ab_ref[s:2 * s, lo:hi] = cs[:, FNET_GROUP_DIM:].astype(BF16)

    o_ref[...] = jnp.dot(seq_ref[...], ab_ref[...], preferred_element_type=F32).astype(o_ref.dtype)


def fourier_mix(proj, *, tq):
    bsz, s, _ = proj.shape
    cc, sc = _dft_cos_sin(FNET_GROUP_DIM)
    chan = jnp.asarray(np.concatenate([cc, sc], axis=1), BF16)
    cs, ss = _dft_cos_sin(s)
    seq = jnp.asarray(np.concatenate([cs, -ss], axis=1), BF16)
    return pl.pallas_call(
        _fourier_kernel,
        grid=(bsz, s // tq),
        in_specs=[pl.BlockSpec((None, s, FNET_WIDTH), lambda b, i: (b, 0, 0)),
                  pl.BlockSpec((FNET_GROUP_DIM, 2 * FNET_GROUP_DIM), lambda b, i: (0, 0)),
                  pl.BlockSpec((tq, 2 * s), lambda b, i: (i, 0))],
        out_specs=pl.BlockSpec((None, tq, FNET_WIDTH), lambda b, i: (b, i, 0)),
        out_shape=jax.ShapeDtypeStruct((bsz, s, FNET_WIDTH), BF16),
        scratch_shapes=[pltpu.VMEM((2 * s, FNET_WIDTH), BF16)],
        compiler_params=_cparams("parallel", "arbitrary"),
        name="fourier_mix",
    )(proj, chan, seq)


def _band_bucket_table():
    i = np.arange(BLOCK)[:, None]
    j = np.arange(3 * BLOCK)[None, :]
    rel = (j - BLOCK) - i
    half = REL_BUCKETS // 2
    max_exact = half // 2
    n = np.abs(rel)
    large = max_exact + (np.log(np.maximum(n, 1) / max_exact)
                         / np.log(REL_MAX_DIST / max_exact) * (half - max_exact)).astype(np.int32)
    large = np.minimum(large, half - 1)
    bucket = (rel > 0).astype(np.int32) * half + np.where(n < max_exact, n, large)
    return np.where(n <= WINDOW, bucket, -1).astype(np.int32)


def _bias_kernel(rb_ref, bucket_ref, o_ref):
    h = pl.program_id(0)
    bucket = bucket_ref[...]
    acc = jnp.full(bucket.shape, NEG_BIG, F32)
    for bkt in range(REL_BUCKETS):
        acc = jnp.where(bucket == bkt, rb_ref[bkt * ATTN_HEADS + h], acc)
    o_ref[...] = acc


def band_bias(rel_bias):
    bucket = jnp.asarray(_band_bucket_table())
    return pl.pallas_call(
        _bias_kernel,
        grid=(ATTN_HEADS,),
        in_specs=[pl.BlockSpec(memory_space=pltpu.SMEM),
                  pl.BlockSpec((BLOCK, 3 * BLOCK), lambda h: (0, 0))],
        out_specs=pl.BlockSpec((None, BLOCK, 3 * BLOCK), lambda h: (h, 0, 0)),
        out_shape=jax.ShapeDtypeStruct((ATTN_HEADS, BLOCK, 3 * BLOCK), F32),
        compiler_params=_cparams("arbitrary"),
        name="band_bias",
    )(rel_bias.reshape(-1), bucket)


def _head_mean_matrix(width):
    m = np.zeros((width, width), np.float32)
    for h in range(width // HEAD_DIM):
        m[h * HEAD_DIM:(h + 1) * HEAD_DIM, h * HEAD_DIM:(h + 1) * HEAD_DIM] = 1.0 / HEAD_DIM
    return m


def _heads_rms(t, mean_mat, w):
    sq = t * t
    hi = sq.astype(BF16)
    lo = (sq - hi.astype(F32)).astype(BF16)
    ms = (jnp.dot(hi, mean_mat, preferred_element_type=F32)
          + jnp.dot(lo, mean_mat, preferred_element_type=F32))
    return t * lax.rsqrt(ms + EPS) * w


def _attn_kernel(sink_ref, q_ref, kl_ref, kc_ref, kr_ref, vl_ref, vc_ref, vr_ref,
                 bias_ref, qnw_ref, knw_ref, qmean_ref, kmean_ref, o_ref):
    n = pl.program_id(1)
    nb = pl.num_programs(1) * ATTN_QBLOCKS
    k = jnp.concatenate([kl_ref[...], kc_ref[...], kr_ref[...]], axis=0).astype(F32)
    v = jnp.concatenate([vl_ref[...], vc_ref[...], vr_ref[...]], axis=0).astype(F32)
    col = lax.broadcasted_iota(jnp.int32, (1, 3 * BLOCK), 1)
    qn = _heads_rms(q_ref[...].astype(F32), qmean_ref[...], qnw_ref[...])
    kn = _heads_rms(k, kmean_ref[...], knw_ref[...])
    low = lax.broadcasted_iota(jnp.int32, (1, 2 * HEAD_DIM), 1) < HEAD_DIM
    kn_sw = pltpu.roll(kn, HEAD_DIM, axis=1)
    v_sw = pltpu.roll(v, HEAD_DIM, axis=1)
    k_dup = [jnp.where(low, kn, kn_sw).astype(BF16), jnp.where(low, kn_sw, kn).astype(BF16)]
    ones = jnp.ones((k.shape[0], 2 * HEAD_DIM), BF16)
    v_ext = [jnp.concatenate([v.astype(BF16), ones], axis=1),
             jnp.concatenate([v_sw.astype(BF16), ones], axis=1)]
    g = ATTN_HEADS // ATTN_KV_HEADS
    for qb in range(ATTN_QBLOCKS):
        blk = n * ATTN_QBLOCKS + qb
        band = slice(qb * BLOCK, (qb + 3) * BLOCK)
        first_key = jnp.where(blk == 0, BLOCK, 0)
        end_key = jnp.where(blk == nb - 1, 2 * BLOCK, 3 * BLOCK)
        outside = (col < first_key) | (col >= end_key)
        pairs = []
        for m in range(ATTN_HEADS // 2):
            j = (2 * m) // g
            qp = qn[qb * BLOCK:(qb + 1) * BLOCK, m * 2 * HEAD_DIM:(m + 1) * 2 * HEAD_DIM]
            res = []
            for idx in range(2):
                h = 2 * m + idx
                qm = jnp.where(low if idx == 0 else jnp.logical_not(low), qp, 0.0).astype(BF16)
                logits = lax.dot_general(qm, k_dup[j][band], (((1,), (1,)), ((), ())),
                                         preferred_element_type=F32)
                logits = jnp.where(outside, NEG_BIG, logits + bias_ref[h])
                sk = sink_ref[h]
                mx = jnp.maximum(jnp.max(logits, axis=-1, keepdims=True), sk)
                p = jnp.exp(logits - mx).astype(BF16)
                r = jnp.dot(p, v_ext[idx if j == 0 else 1 - idx][band], preferred_element_type=F32)
                denom = r[:, 2 * HEAD_DIM:] + jnp.exp(sk - mx)
                res.append(r[:, :2 * HEAD_DIM] / denom)
            pairs.append(jnp.where(low, res[0], res[1]))
        o_ref[qb * BLOCK:(qb + 1) * BLOCK, :] = jnp.concatenate(pairs, axis=-1).astype(o_ref.dtype)


def window_attention(proj, bias, q_norm_w, k_norm_w, sink):
    bsz, s, _ = proj.shape
    nb = s // BLOCK
    qcol = FNET_WIDTH // ATTN_WIDTH
    kcol = (FNET_WIDTH + ATTN_WIDTH) // KV_WIDTH
    vcol = kcol + 1

    qb = ATTN_QBLOCKS
    assert nb % qb == 0

    def kv_specs(col):
        return [pl.BlockSpec((None, BLOCK, KV_WIDTH), lambda b, n: (b, jnp.maximum(n * qb - 1, 0), col)),
                pl.BlockSpec((None, qb * BLOCK, KV_WIDTH), lambda b, n: (b, n, col)),
                pl.BlockSpec((None, BLOCK, KV_WIDTH), lambda b, n: (b, jnp.minimum((n + 1) * qb, nb - 1), col))]

    return pl.pallas_call(
        _attn_kernel,
        grid=(bsz, nb // qb),
        in_specs=[pl.BlockSpec(memory_space=pltpu.SMEM),
                  pl.BlockSpec((None, qb * BLOCK, ATTN_WIDTH), lambda b, n: (b, n, qcol)),
                  *kv_specs(kcol), *kv_specs(vcol),
                  pl.BlockSpec((ATTN_HEADS, BLOCK, 3 * BLOCK), lambda b, n: (0, 0, 0)),
                  pl.BlockSpec((1, ATTN_WIDTH), lambda b, n: (0, 0)),
                  pl.BlockSpec((1, KV_WIDTH), lambda b, n: (0, 0)),
                  pl.BlockSpec((ATTN_WIDTH, ATTN_WIDTH), lambda b, n: (0, 0)),
                  pl.BlockSpec((KV_WIDTH, KV_WIDTH), lambda b, n: (0, 0))],
        out_specs=pl.BlockSpec((None, qb * BLOCK, ATTN_WIDTH), lambda b, n: (b, n, 0)),
        out_shape=jax.ShapeDtypeStruct((bsz, s, ATTN_WIDTH), BF16),
        compiler_params=_cparams("parallel", "arbitrary"),
        name="window_attention",
    )(sink, proj, proj, proj, proj, proj, proj, proj, bias,
      (jnp.tile(q_norm_w, ATTN_HEADS) * (HEAD_DIM ** -0.5)).reshape(1, ATTN_WIDTH),
      jnp.tile(k_norm_w, ATTN_KV_HEADS).reshape(1, KV_WIDTH),
      jnp.asarray(_head_mean_matrix(ATTN_WIDTH), BF16), jnp.asarray(_head_mean_matrix(KV_WIDTH), BF16))


def _cat_proj_kernel(a1_ref, a2_ref, w_ref, x_ref, g_ref, o_ref):
    k1 = a1_ref.shape[1]
    y = jnp.dot(a1_ref[...].astype(BF16), w_ref[0:k1, :], preferred_element_type=F32)
    y = y + jnp.dot(a2_ref[...].astype(BF16), w_ref[k1:, :], preferred_element_type=F32)
    o_ref[...] = x_ref[...] + g_ref[...] * y


def cat_proj_residual(a1, a2, w, x, g, *, ts):
    bsz, s, d = x.shape
    k1, k2 = a1.shape[2], a2.shape[2]
    return pl.pallas_call(
        _cat_proj_kernel,
        grid=(bsz, s // ts),
        in_specs=[pl.BlockSpec((None, ts, k1), lambda b, i: (b, i, 0)),
                  pl.BlockSpec((None, ts, k2), lambda b, i: (b, i, 0)),
                  pl.BlockSpec((k1 + k2, d), lambda b, i: (0, 0)),
                  pl.BlockSpec((None, ts, d), lambda b, i: (b, i, 0)),
                  pl.BlockSpec((None, 1, d), lambda b, i: (b, 0, 0))],
        out_specs=pl.BlockSpec((None, ts, d), lambda b, i: (b, i, 0)),
        out_shape=jax.ShapeDtypeStruct((bsz, s, d), F32),
        compiler_params=_cparams("parallel", "parallel"),
        name="mixer_out_proj",
    )(a1, a2, w, x, g)


def _ffn_kernel(x_ref, nw_ref, sh_ref, sc_ref, g_ref, w1_ref, w3_ref, w2_ref, o_ref, acc_ref, *, tf):
    h = _modnorm(x_ref[...], nw_ref[...], sh_ref[...], sc_ref[...]).astype(BF16)
    dff = w1_ref.shape[1]
    for lo in range(0, dff, tf):
        a = jnp.dot(h, w1_ref[:, lo:lo + tf], preferred_element_type=F32)
        b = jnp.dot(h, w3_ref[:, lo:lo + tf], preferred_element_type=F32)
        t = (_silu(a) * b).astype(BF16)
        contrib = jnp.dot(t, w2_ref[lo:lo + tf, :], preferred_element_type=F32)
        if lo == 0:
            acc_ref[...] = contrib
        else:
            acc_ref[...] += contrib
    o_ref[...] = x_ref[...] + g_ref[...] * acc_ref[...]


def ffn_residual(x, nw, sh, sc, g, w1, w3, w2, *, ts, tf):
    bsz, s, d = x.shape
    dff = w1.shape[1]
    assert dff % tf == 0
    vec = pl.BlockSpec((None, 1, d), lambda b, i: (b, 0, 0))
    return pl.pallas_call(
        functools.partial(_ffn_kernel, tf=tf),
        grid=(bsz, s // ts),
        in_specs=[pl.BlockSpec((None, ts, d), lambda b, i: (b, i, 0)),
                  pl.BlockSpec((1, d), lambda b, i: (0, 0)),
                  vec, vec, vec,
                  _resident((d, dff)), _resident((d, dff)), _resident((dff, d))],
        out_specs=pl.BlockSpec((None, ts, d), lambda b, i: (b, i, 0)),
        out_shape=jax.ShapeDtypeStruct((bsz, s, d), F32),
        scratch_shapes=[pltpu.VMEM((ts, d), F32)],
        compiler_params=_cparams("parallel", "parallel"),
        name="ffn_swiglu",
    )(x, nw.reshape(1, d), sh, sc, g, w1, w3, w2)


CONV_PAD = 8
CONV_ROWS = 256


def _conv_kernel(x_ref, w_ref, b_ref, o_ref, pad_ref):
    s, tc = x_ref.shape
    zeros = jnp.zeros((CONV_PAD, tc), F32)
    pad_ref[0:CONV_PAD, :] = zeros
    pad_ref[CONV_PAD + s:, :] = zeros
    pad_ref[CONV_PAD:CONV_PAD + s, :] = x_ref[...].astype(F32)
    half = CONV_WIDTH // 2
    for r in range(s // CONV_ROWS):
        base = CONV_PAD + r * CONV_ROWS - half
        acc = jnp.zeros((CONV_ROWS, tc), F32) + b_ref[...]
        for kk in range(CONV_WIDTH):
            acc = acc + pad_ref[base + kk:base + kk + CONV_ROWS, :] * w_ref[kk:kk + 1, :]
        o_ref[r * CONV_ROWS:(r + 1) * CONV_ROWS, :] = _silu(acc)


def conv_silu(proj, conv_w, conv_b, *, col0, tc):
    bsz, s, _ = proj.shape
    cdim = conv_w.shape[1]
    cb0 = col0 // tc
    return pl.pallas_call(
        _conv_kernel,
        grid=(bsz, cdim // tc),
        in_specs=[pl.BlockSpec((None, s, tc), lambda b, j: (b, 0, cb0 + j)),
                  pl.BlockSpec((CONV_WIDTH, tc), lambda b, j: (0, j)),
                  pl.BlockSpec((1, tc), lambda b, j: (0, j))],
        out_specs=pl.BlockSpec((None, s, tc), lambda b, j: (b, 0, j)),
        out_shape=jax.ShapeDtypeStruct((bsz, s, cdim), F32),
        scratch_shapes=[pltpu.VMEM((s + 2 * CONV_PAD, tc), F32)],
        compiler_params=_cparams("parallel", "parallel"),
        name="conv_silu",
    )(proj, conv_w, conv_b.reshape(1, cdim))


HALO_ROWS = 16
PROJ_CHUNK = 256


def _proj_conv_kernel(x_ref, xp_ref, xn_ref, nw_ref, sh_ref, sc_ref, wz_ref, wc_ref, wd_ref, cw_ref, cb_ref,
                      z_ref, xs_ref, bc_ref, dt_ref):
    i = pl.program_id(1)
    ts = x_ref.shape[0]
    half = CONV_WIDTH // 2

    def hnorm(ref):
        return _modnorm(ref[...], nw_ref[...], sh_ref[...], sc_ref[...])

    h = hnorm(x_ref).astype(BF16)
    h_prev = jnp.where(i == 0, 0.0, hnorm(xp_ref)).astype(BF16)
    h_next = jnp.where(i == pl.num_programs(1) - 1, 0.0, hnorm(xn_ref)).astype(BF16)
    h_ext = jnp.concatenate([h_prev, h, h_next], axis=0)

    n_x = xs_ref.shape[1]
    rows = h_ext.shape[0]

    def project(lo):
        return jnp.dot(h_ext, wc_ref[:, lo:lo + PROJ_CHUNK], preferred_element_type=F32)

    def z_chunk(lo):
        hi = lo + PROJ_CHUNK
        z_ref[:, lo:hi] = jnp.dot(h, wz_ref[:, lo:hi], preferred_element_type=F32).astype(z_ref.dtype)

    z_los = list(range(0, wz_ref.shape[1], PROJ_CHUNK))
    c_los = list(range(0, wc_ref.shape[1], PROJ_CHUNK))
    p_next = project(c_los[0])
    for n, lo in enumerate(c_los):
        hi = lo + PROJ_CHUNK
        p = p_next
        if n + 1 < len(c_los):
            p_next = project(c_los[n + 1])
        if z_los:
            z_chunk(z_los.pop(0))
        acc = jnp.zeros((ts, PROJ_CHUNK), F32) + cb_ref[:, lo:hi]
        for kk in range(CONV_WIDTH):
            shifted = p if kk == half else pltpu.roll(p, (half - kk) % rows, axis=0)
            acc = acc + shifted[HALO_ROWS:HALO_ROWS + ts, :] * cw_ref[kk:kk + 1, lo:hi]
        out = _silu(acc)
        if lo < n_x:
            xs_ref[:, lo:hi] = out
        else:
            bc_ref[:, lo - n_x:hi - n_x] = out.astype(bc_ref.dtype)
    for lo in z_los:
        z_chunk(lo)
    dt_ref[...] = jnp.dot(h, wd_ref[...], preferred_element_type=F32)


def ssm_in_proj_conv(x, nw, sh, sc, w_z, w_xbc, w_dt, conv_w, conv_b, *, n_x, ts):
    bsz, s, d = x.shape
    dz = w_z.shape[1]
    dc = w_xbc.shape[1]
    ddt = w_dt.shape[1]
    assert ts % HALO_ROWS == 0 and n_x % PROJ_CHUNK == 0 and dc % PROJ_CHUNK == 0 and dz % PROJ_CHUNK == 0
    r = ts // HALO_ROWS
    last = s // HALO_ROWS - 1
    vec = pl.BlockSpec((None, 1, d), lambda b, i: (b, 0, 0))
    return pl.pallas_call(
        _proj_conv_kernel,
        grid=(bsz, s // ts),
        in_specs=[pl.BlockSpec((None, ts, d), lambda b, i: (b, i, 0)),
                  pl.BlockSpec((None, HALO_ROWS, d), lambda b, i: (b, jnp.maximum(i * r - 1, 0), 0)),
                  pl.BlockSpec((None, HALO_ROWS, d), lambda b, i: (b, jnp.minimum((i + 1) * r, last), 0)),
                  pl.BlockSpec((1, d), lambda b, i: (0, 0)), vec, vec,
                  _resident((d, dz)), _resident((d, dc)), _resident((d, ddt)),
                  _resident((CONV_WIDTH, dc)), _resident((1, dc))],
        out_specs=[pl.BlockSpec((None, ts, dz), lambda b, i: (b, i, 0)),
                   pl.BlockSpec((None, ts, n_x), lambda b, i: (b, i, 0)),
                   pl.BlockSpec((None, ts, dc - n_x), lambda b, i: (b, i, 0)),
                   pl.BlockSpec((None, ts, ddt), lambda b, i: (b, i, 0))],
        out_shape=[jax.ShapeDtypeStruct((bsz, s, dz), BF16),
                   jax.ShapeDtypeStruct((bsz, s, n_x), F32),
                   jax.ShapeDtypeStruct((bsz, s, dc - n_x), BF16),
                   jax.ShapeDtypeStruct((bsz, s, ddt), F32)],
        compiler_params=_cparams("parallel", "parallel"),
        name="odd_in_proj_conv",
    )(x, x, x, nw.reshape(1, d), sh, sc, w_z, w_xbc, w_dt, conv_w, conv_b.reshape(1, dc))


def _softplus(x):
    return jnp.maximum(x, 0.0) + jnp.log(1.0 + jnp.exp(-jnp.abs(x)))


def _expand_matrix(n_in, width):
    e = np.zeros((n_in, n_in * width), np.float32)
    for h in range(n_in):
        e[h, h * width:(h + 1) * width] = 1.0
    return e


LOG2E = 1.4426950408889634
DECAY_SLOTS = 12


def _decay_placement(nh):
    place = np.zeros((3, 2 * nh, 3 * 128), np.float32)
    const = np.zeros((1, 3 * 128), np.float32)
    for h in range(nh):
        for part in range(3):
            place[part, h, part * nh + h] = 1.0
            place[part, nh + h, (6 + part) * nh + h] = -1.0
            place[part, h, 128 + (3 + part) * nh + h] = -1.0
            place[part, nh + h, 256 + (9 + part) * nh + h] = 1.0
            const[0, (3 + part) * nh + h] = 1.0
            const[0, (9 + part) * nh + h] = 1.0
            const[0, 128 + part * nh + h] = 1.0
            const[0, 256 + (6 + part) * nh + h] = 1.0
    return place, const


def _split3(v):
    hi = v.astype(BF16)
    r = v - hi.astype(F32)
    mid = r.astype(BF16)
    lo = (r - mid.astype(F32)).astype(BF16)
    return hi, mid, lo


def _ssd_kernel(x_ref, b_ref, c_ref, dtc_f_ref, dtc_b_ref, dtr_f_ref, dtr_b_ref,
                pc_f_ref, pc_b_ref, pr_f_ref, pr_b_ref, dx_ref,
                tri_ref, place_ref, pconst_ref, e64_ref, y_ref,
                ac_ref, dc_ref, dr_ref, pq_ref, xdf_ref, xdb_ref, eif_ref, er_ref, decf_ref, decb_ref,
                hf_ref, hb_ref):
    s = x_ref.shape[0]
    q = SSD_CHUNK
    nh = pc_f_ref.shape[1]
    nc = s // q
    hd = SSM_HEAD_DIM

    def col_params(raw_ref, p_ref):
        dt = _softplus(raw_ref[...] + p_ref[0:1, :])
        return dt, (-LOG2E) * jnp.exp(p_ref[1:2, :]) * dt

    dcf, acf = col_params(dtc_f_ref, pc_f_ref)
    dcb, acb = col_params(dtc_b_ref, pc_b_ref)
    dc_ref[:, 0:nh] = dcf
    dc_ref[:, nh:] = dcb
    ac_ref[:, 0:nh] = acf
    ac_ref[:, nh:] = acb
    dr_ref[0:nh, :] = _softplus(dtr_f_ref[...] + pr_f_ref[:, 0:1])
    dr_ref[nh:, :] = _softplus(dtr_b_ref[...] + pr_b_ref[:, 0:1])

    li = lax.broadcasted_iota(jnp.int32, (q, q), 0)
    si = lax.broadcasted_iota(jnp.int32, (q, q), 1)
    lower = li >= si
    upper = li <= si
    slot_head = lax.broadcasted_iota(jnp.int32, (1, 128), 1) % nh
    pair_lo = lax.broadcasted_iota(jnp.int32, (1, 2 * hd), 1) < hd

    def bdot(a, b):
        return jnp.dot(a, b, preferred_element_type=F32)

    def chunk_cumsums(sl):
        a_col = ac_ref[sl, :]
        tri = tri_ref[...]
        hi, mid, lo = _split3(a_col)
        return a_col, bdot(tri, hi) + bdot(tri, mid) + bdot(tri, lo)

    def expand64(v, parts=2):
        e = e64_ref[...]
        hi = v.astype(BF16)
        out = bdot(hi, e)
        if parts == 2:
            out = out + bdot((v - hi.astype(F32)).astype(BF16), e)
        return out

    hf_ref[...] = jnp.zeros_like(hf_ref)
    hb_ref[...] = jnp.zeros_like(hb_ref)

    def prep_body(c, carry):
        sl = pl.ds(pl.multiple_of(c * q, q), q)
        a_col, i_col = chunk_cumsums(sl)
        d_col = dc_ref[sl, :]
        if_col = i_col[:, 0:nh]
        tot_f = i_col[q - 1:q, 0:nh]
        tot_b = i_col[q - 1:q, nh:]
        ie = i_col - jnp.where(lax.broadcasted_iota(jnp.int32, (1, 2 * nh), 1) < nh, 0.0, a_col)
        eb_col = ie[:, nh:]
        parts = _split3(ie)
        placed = pconst_ref[...]
        for part in range(3):
            placed = placed + bdot(parts[part], place_ref[part])
        pq_ref[sl, :] = placed.astype(BF16)
        xc = x_ref[sl, :]
        xdf_ref[sl, :] = (xc * expand64(jnp.exp2(tot_f - if_col) * d_col[:, 0:nh], parts=1)).astype(BF16)
        xdb_ref[sl, :] = (xc * expand64(jnp.exp2(eb_col) * d_col[:, nh:], parts=1)).astype(BF16)
        eif_ref[sl, :] = expand64(jnp.exp2(if_col))
        er_ref[sl, :] = expand64(jnp.exp2(tot_b - eb_col))
        dec = expand64(jnp.exp2(jnp.concatenate([tot_f, tot_b], axis=0)))
        decf_ref[c] = jnp.broadcast_to(dec[0:1, :], decf_ref.shape[1:])
        decb_ref[c] = jnp.broadcast_to(dec[1:2, :], decb_ref.shape[1:])
        return carry

    lax.fori_loop(0, nc, prep_body, 0, unroll=4)

    def fwd_body(c, carry):
        sl = pl.ds(pl.multiple_of(c * q, q), q)
        xc = x_ref[sl, :]
        bc = b_ref[sl, :].astype(BF16)
        cc = c_ref[sl, :].astype(BF16)
        d_row = dr_ref[:, sl]
        placed = pq_ref[sl, :]
        p_all = placed[:, 0:128]
        q_cat = jnp.concatenate([placed[:, 128:256], placed[:, 256:384]], axis=0)

        cb = lax.dot_general(cc, bc, (((1,), (1,)), ((), ())), preferred_element_type=F32)
        xb = xc.astype(BF16)
        ys = []
        for h0 in range(0, nh, 2):
            ms = []
            for h in (h0, h0 + 1):
                ph = jnp.where(slot_head == h, p_all, jnp.zeros_like(p_all))
                g2 = lax.dot_general(ph, q_cat, (((1,), (1,)), ((), ())), preferred_element_type=F32)
                arg = jnp.where(lower, g2[:, 0:q], g2[:, q:])
                wgt = (jnp.where(lower, d_row[h:h + 1, :], 0.0)
                       + jnp.where(upper, d_row[nh + h:nh + h + 1, :], 0.0))
                ms.append((cb * jnp.exp2(arg) * wgt).astype(BF16))
            xp = xb[:, h0 * hd:(h0 + 2) * hd]
            zero = jnp.zeros_like(xp)
            rhs = jnp.concatenate([jnp.where(pair_lo, xp, zero), jnp.where(pair_lo, zero, xp)], axis=0)
            ys.append(bdot(jnp.concatenate(ms, axis=1), rhs))
        y = jnp.concatenate(ys, axis=1) + dx_ref[...] * xc

        states = lax.dot_general(bc, xdf_ref[sl, :], (((0,), (0,)), ((), ())),
                                 preferred_element_type=F32)
        h_prev = hf_ref[...]
        y = y + bdot(cc, h_prev.astype(BF16)) * eif_ref[sl, :]
        hf_ref[...] = h_prev * decf_ref[c][0:1, :] + states
        y_ref[sl, :] = y
        return carry

    lax.fori_loop(0, nc, fwd_body, 0, unroll=2)

    def bwd_body(t, carry):
        c = nc - 1 - t
        sl = pl.ds(pl.multiple_of(c * q, q), q)
        bc = b_ref[sl, :].astype(BF16)
        cc = c_ref[sl, :].astype(BF16)
        states = lax.dot_general(bc, xdb_ref[sl, :], (((0,), (0,)), ((), ())),
                                 preferred_element_type=F32)
        h_prev = hb_ref[...]
        y_ref[sl, :] += bdot(cc, h_prev.astype(BF16)) * er_ref[sl, :]
        hb_ref[...] = h_prev * decb_ref[c][0:1, :] + states
        return carry

    lax.fori_loop(0, nc, bwd_body, 0, unroll=4)


def _bf16_parts3(v):
    hi = v.astype(BF16).astype(F32)
    r = v - hi
    mid = r.astype(BF16).astype(F32)
    lo = (r - mid).astype(BF16).astype(F32)
    return hi, mid, lo


def _ssd_t_kernel(x_ref, b_ref, c_ref, dt_f_ref, dt_b_ref, p_f_ref, p_b_ref, dx_ref, trio_ref, y_ref,
                  ar_ref, dr_ref, sc_ref, pq_ref, qt_ref, xtb_ref, xdf_ref, xdb_ref, yt_ref, hf_ref, hb_ref):
    s = x_ref.shape[0]
    q = SSD_CHUNK
    nh = p_f_ref.shape[0]
    nc = s // q
    hd = SSM_HEAD_DIM

    def row_params(raw_ref, p_ref):
        dt = _softplus(raw_ref[...] + p_ref[:, 0:1])
        return dt, (-LOG2E) * jnp.exp(p_ref[:, 1:2]) * dt

    dtf, af = row_params(dt_f_ref, p_f_ref)
    dtb, ab = row_params(dt_b_ref, p_b_ref)
    dr_ref[0:nh, :] = dtf
    dr_ref[nh:, :] = dtb
    ar_ref[0:nh, :] = af
    ar_ref[nh:, :] = ab

    li = lax.broadcasted_iota(jnp.int32, (q, q), 0)
    si = lax.broadcasted_iota(jnp.int32, (q, q), 1)
    lower = li >= si
    upper = li <= si
    slot_head = lax.broadcasted_iota(jnp.int32, (1, q), 1) % nh

    def bdot(a, b):
        return jnp.dot(a, b, preferred_element_type=F32)

    def ntdot(a, b):
        return lax.dot_general(a, b, (((1,), (1,)), ((), ())), preferred_element_type=F32)

    def head_rows(v):
        return jnp.concatenate([jnp.broadcast_to(v[h:h + 1, :], (hd, q)) for h in range(nh)], axis=0)

    ones = jnp.ones((nh, q), F32)
    zeros = jnp.zeros((nh, q), F32)

    def prep_body(c, carry):
        sl = pl.ds(pl.multiple_of(c * q, q), q)
        a_row = ar_ref[:, sl]
        d_row = dr_ref[:, sl]
        parts = jnp.concatenate([p.astype(BF16) for p in _bf16_parts3(a_row)], axis=0)
        cs = bdot(parts, trio_ref[...])
        cs = cs[0:2 * nh] + cs[2 * nh:4 * nh] + cs[4 * nh:6 * nh]
        i_f = cs[0:nh, 0:q]
        e_b = cs[nh:, 0:q] - a_row[nh:]
        tot_f = cs[0:nh, q:]
        tot_b = cs[nh:, q:]
        ih, im, il = _bf16_parts3(i_f)
        eh, em, el = _bf16_parts3(e_b)
        pad = [zeros] * (q // nh - DECAY_SLOTS)
        p_t = jnp.concatenate([ih, im, il, ones, ones, ones, -eh, -em, -el, ones, ones, ones] + pad, axis=0)
        qf_t = jnp.concatenate([ones, ones, ones, -ih, -im, -il] + [zeros] * 6 + pad, axis=0)
        qb_t = jnp.concatenate([zeros] * 6 + [ones, ones, ones, eh, em, el] + pad, axis=0)
        pq_ref[sl, :] = p_t.T.astype(BF16)
        qt_ref[c] = jnp.concatenate([qf_t, qb_t], axis=1).astype(BF16)
        sc_ref[0 * nh:1 * nh, sl] = jnp.exp2(i_f)
        sc_ref[1 * nh:2 * nh, sl] = jnp.exp2(tot_b - e_b)
        sc_ref[2 * nh:3 * nh, sl] = jnp.exp2(tot_f)
        sc_ref[3 * nh:4 * nh, sl] = jnp.exp2(tot_b)
        xt = x_ref[sl, :].T
        xtb_ref[:, sl] = xt.astype(BF16)
        xdf_ref[:, sl] = (xt * head_rows(jnp.exp2(tot_f - i_f) * d_row[0:nh])).astype(BF16)
        xdb_ref[:, sl] = (xt * head_rows(jnp.exp2(e_b) * d_row[nh:])).astype(BF16)
        return carry

    lax.fori_loop(0, nc, prep_body, 0, unroll=2)

    hf_ref[...] = jnp.zeros_like(hf_ref)
    hb_ref[...] = jnp.zeros_like(hb_ref)
    zero_half = jnp.zeros((hd, q), BF16)

    def fwd_body(c, carry):
        sl = pl.ds(pl.multiple_of(c * q, q), q)
        bc = b_ref[sl, :].astype(BF16)
        cc = c_ref[sl, :].astype(BF16)
        d_row = dr_ref[:, sl]
        p_all = pq_ref[sl, :]
        q_t = qt_ref[c]
        xtb = xtb_ref[:, sl]
        cb = ntdot(cc, bc)
        g2s = [bdot(jnp.where(slot_head == h, p_all, jnp.zeros_like(p_all)), q_t) for h in range(nh)]
        ms = []
        for h in range(nh):
            arg = jnp.where(lower, g2s[h][:, 0:q], g2s[h][:, q:])
            wgt = (jnp.where(lower, d_row[h:h + 1, :], 0.0)
                   + jnp.where(upper, d_row[nh + h:nh + h + 1, :], 0.0))
            ms.append((cb * jnp.exp2(arg) * wgt).astype(BF16))
        yd = []
        for h0 in range(0, nh, 2):
            lhs = jnp.concatenate(
                [jnp.concatenate([xtb[h0 * hd:(h0 + 1) * hd], zero_half], axis=0),
                 jnp.concatenate([zero_half, xtb[(h0 + 1) * hd:(h0 + 2) * hd]], axis=0)], axis=1)
            yd.append(ntdot(lhs, jnp.concatenate(ms[h0:h0 + 2], axis=1)))
        states = bdot(xdf_ref[:, sl], bc)
        h_prev = hf_ref[...]
        y_off = ntdot(h_prev.astype(BF16), cc) * head_rows(sc_ref[0 * nh:1 * nh, sl])
        hf_ref[...] = h_prev * head_rows(sc_ref[2 * nh:3 * nh, sl]) + states
        yt_ref[:, sl] = jnp.concatenate(yd, axis=0) + y_off
        return carry

    lax.fori_loop(0, nc, fwd_body, 0, unroll=4)

    def bwd_body(t, carry):
        c = nc - 1 - t
        sl = pl.ds(pl.multiple_of(c * q, q), q)
        bc = b_ref[sl, :].astype(BF16)
        cc = c_ref[sl, :].astype(BF16)
        states = bdot(xdb_ref[:, sl], bc)
        h_prev = hb_ref[...]
        y_off = ntdot(h_prev.astype(BF16), cc) * head_rows(sc_ref[1 * nh:2 * nh, sl])
        hb_ref[...] = h_prev * head_rows(sc_ref[3 * nh:4 * nh, sl]) + states
        y_ref[sl, :] = ((yt_ref[:, sl] + y_off).T + dx_ref[...] * x_ref[sl, :]).astype(y_ref.dtype)
        return carry

    lax.fori_loop(0, nc, bwd_body, 0, unroll=4)


def ssd_scan_bidir(xs, bc, dt, dt_bias_f, dt_bias_b, a_log_f, a_log_b, d_skip):
    bsz, s, _ = xs.shape
    nheads = dt.shape[2] // 2
    nh = nheads // SSM_GROUPS
    gw = nh * SSM_HEAD_DIM
    d_inner = nheads * SSM_HEAD_DIM
    q = SSD_CHUNK
    dt_row = jnp.transpose(dt.reshape(bsz, s, 2 * SSM_GROUPS, nh), (0, 2, 3, 1))
    prm = jnp.stack([jnp.concatenate([dt_bias_f, dt_bias_b]), jnp.concatenate([a_log_f, a_log_b])])
    p_row = jnp.transpose(prm.reshape(2, 2 * SSM_GROUPS, nh), (1, 2, 0))
    dx = jnp.repeat(d_skip, SSM_HEAD_DIM).reshape(SSM_GROUPS, 1, gw)
    assert DECAY_SLOTS * nh <= q and q % nh == 0 and D_STATE == q
    trio = jnp.asarray(np.concatenate([np.triu(np.ones((q, q), np.float32)), np.ones((q, q), np.float32)],
                                      axis=1), BF16)
    bcol = d_inner // D_STATE
    G = SSM_GROUPS
    nc = s // q

    return pl.pallas_call(
        _ssd_t_kernel,
        grid=(bsz, SSM_GROUPS),
        in_specs=[pl.BlockSpec((None, s, gw), lambda b, g: (b, 0, g)),
                  pl.BlockSpec((None, s, D_STATE), lambda b, g: (b, 0, g)),
                  pl.BlockSpec((None, s, D_STATE), lambda b, g: (b, 0, G + g)),
                  pl.BlockSpec((None, None, nh, s), lambda b, g: (b, g, 0, 0)),
                  pl.BlockSpec((None, None, nh, s), lambda b, g: (b, G + g, 0, 0)),
                  pl.BlockSpec((None, nh, 2), lambda b, g: (g, 0, 0)),
                  pl.BlockSpec((None, nh, 2), lambda b, g: (G + g, 0, 0)),
                  pl.BlockSpec((None, 1, gw), lambda b, g: (g, 0, 0)),
                  pl.BlockSpec((q, 2 * q), lambda b, g: (0, 0))],
        out_specs=pl.BlockSpec((None, s, gw), lambda b, g: (b, 0, g)),
        out_shape=jax.ShapeDtypeStruct((bsz, s, d_inner), BF16),
        scratch_shapes=[pltpu.VMEM((2 * nh, s), F32), pltpu.VMEM((2 * nh, s), F32),
                        pltpu.VMEM((4 * nh, s), F32),
                        pltpu.VMEM((s, q), BF16), pltpu.VMEM((nc, q, 2 * q), BF16),
                        pltpu.VMEM((gw, s), BF16), pltpu.VMEM((gw, s), BF16), pltpu.VMEM((gw, s), BF16),
                        pltpu.VMEM((gw, s), F32),
                        pltpu.VMEM((gw, D_STATE), F32), pltpu.VMEM((gw, D_STATE), F32)],
        compiler_params=_cparams("parallel", "parallel"),
        name="ssd_scan",
    )(xs, bc, bc, dt_row, dt_row, p_row, p_row, dx, trio)


def _gated_proj_route_kernel(y_ref, z_ref, gw_ref, w_ref, x_ref, g_ref,
                             nw_ref, sh_ref, sc_ref, rw_ref, rb_ref, lt_ref,
                             o_ref, ld_ref, wgt_ref, cnt_ref, acc_ref):
    _gated_proj_kernel(y_ref, z_ref, gw_ref, w_ref, x_ref, g_ref, o_ref, acc_ref)
    h = _modnorm(o_ref[...], nw_ref[...], sh_ref[...], sc_ref[...])
    rows, gates, total = _route_tile(h, rw_ref[...], rb_ref[...], lt_ref[...])
    ld_ref[...] = rows
    wgt_ref[...] = gates
    cnt_ref[...] = total[:, 0:cnt_ref.shape[1]]


def _gated_proj_kernel(y_ref, z_ref, gw_ref, w_ref, x_ref, g_ref, o_ref, acc_ref):
    k = y_ref.shape[1]
    ss = jnp.zeros((y_ref.shape[0], 1), F32)
    for lo in range(0, k, PROJ_CHUNK):
        hi = lo + PROJ_CHUNK
        t = y_ref[:, lo:hi].astype(F32) * _silu(z_ref[:, lo:hi].astype(F32))
        ss = ss + jnp.sum(t * t, axis=-1, keepdims=True)
        contrib = jnp.dot((t * gw_ref[:, lo:hi]).astype(BF16), w_ref[lo:hi, :], preferred_element_type=F32)
        if lo == 0:
            acc_ref[...] = contrib
        else:
            acc_ref[...] += contrib
    o_ref[...] = x_ref[...] + g_ref[...] * (acc_ref[...] * lax.rsqrt(ss * (1.0 / k) + EPS))


def gated_proj_route(y, z, gw, w, x, g, nw, sh, sc, router_w, router_b, *, ts):
    bsz, s, d = x.shape
    k = y.shape[2]
    ne = router_w.shape[1]
    nt = s // ts
    cum = jnp.asarray(np.concatenate([np.triu(np.ones((ts, ts), np.float32)), np.ones((ts, ts), np.float32)],
                                     axis=1), BF16)
    vec = pl.BlockSpec((None, 1, d), lambda b, i: (b, 0, 0))
    tok_spec = pl.BlockSpec((TOP_K, ts), lambda b, i: (0, b * nt + i))
    call = pl.pallas_call(
        _gated_proj_route_kernel,
        grid=(bsz, nt),
        in_specs=[pl.BlockSpec((None, ts, k), lambda b, i: (b, i, 0)),
                  pl.BlockSpec((None, ts, k), lambda b, i: (b, i, 0)),
                  pl.BlockSpec((1, k), lambda b, i: (0, 0)),
                  _resident((k, d)),
                  pl.BlockSpec((None, ts, d), lambda b, i: (b, i, 0)),
                  vec,
                  pl.BlockSpec((1, d), lambda b, i: (0, 0)), vec, vec,
                  _resident((ne, d)), pl.BlockSpec((ne, 1), lambda b, i: (0, 0)), _resident((ts, 2 * ts))],
        out_specs=[pl.BlockSpec((None, ts, d), lambda b, i: (b, i, 0)), tok_spec, tok_spec,
                   pl.BlockSpec((None, ne, 128), lambda b, i: (b * nt + i, 0, 0))],
        out_shape=[jax.ShapeDtypeStruct((bsz, s, d), F32),
                   jax.ShapeDtypeStruct((TOP_K, bsz * s), jnp.int32),
                   jax.ShapeDtypeStruct((TOP_K, bsz * s), F32),
                   jax.ShapeDtypeStruct((bsz * nt, ne, 128), jnp.int32)],
        scratch_shapes=[pltpu.VMEM((ts, d), F32)],
        compiler_params=_cparams("parallel", "parallel"),
        name="ssd_out_proj_route",
    )
    x_new, ldest, wgt, cnt = call(y, z, gw.reshape(1, k), w, x, g, nw.reshape(1, d), sh, sc,
                                  router_w.T, router_b.reshape(ne, 1), cum)
    return x_new, ldest, wgt, cnt[:, :, 0]


SEG_ROWS = 16
SEG_FIELDS = 3


def _route_tile(h, rw_t, rb_col, cum):
    def nt(a, b):
        return lax.dot_general(a, b, (((1,), (1,)), ((), ())), preferred_element_type=F32)

    h_hi = h.astype(BF16)
    h_lo = (h - h_hi.astype(F32)).astype(BF16)
    rw_hi = rw_t.astype(BF16)
    rw_lo = (rw_t - rw_hi.astype(F32)).astype(BF16)
    logits = nt(rw_hi, h_hi) + (nt(rw_hi, h_lo) + nt(rw_lo, h_hi)) + rb_col
    ne, ts = logits.shape
    eid = lax.broadcasted_iota(jnp.int32, (ne, ts), 0)
    m1 = jnp.max(logits, axis=0, keepdims=True)
    i1 = jnp.min(jnp.where(logits == m1, eid, ne), axis=0, keepdims=True)
    rest = jnp.where(eid == i1, -jnp.inf, logits)
    m2 = jnp.max(rest, axis=0, keepdims=True)
    i2 = jnp.min(jnp.where(rest == m2, eid, ne), axis=0, keepdims=True)
    e2 = jnp.exp(m2 - m1)
    w1 = 1.0 / (1.0 + e2)
    w2 = e2 / (1.0 + e2)
    oh1 = (eid == i1).astype(F32)
    oh2 = (eid == i2).astype(F32)
    chosen = oh1 + oh2
    both = jnp.dot(chosen.astype(BF16), cum, preferred_element_type=F32)
    before = both[:, 0:ts] - chosen
    total = both[:, ts:].astype(jnp.int32)
    seg_len = jnp.bitwise_and(total + (SEG_ROWS - 1), -SEG_ROWS).astype(F32)
    rows = [jnp.sum(jnp.where(eid < idx, seg_len, 0.0) + onehot * before, axis=0, keepdims=True)
            for onehot, idx in ((oh1, i1), (oh2, i2))]
    return jnp.concatenate(rows, axis=0).astype(jnp.int32), jnp.concatenate([w1, w2], axis=0), total


def _segment_copies(seg_ref, tile, n_experts, make_copy, *, wait):
    for e in range(n_experts):
        base = (tile * n_experts + e) * SEG_FIELDS
        local0 = seg_ref[base]
        global0 = seg_ref[base + 1]

        def body(i, carry, local0=local0, global0=global0):
            cp = make_copy(pl.multiple_of(local0 + i * SEG_ROWS, SEG_ROWS),
                           pl.multiple_of(global0 + i * SEG_ROWS, SEG_ROWS))
            if wait:
                cp.wait()
            else:
                cp.start()
            return carry

        lax.fori_loop(0, seg_ref[base + 2], body, 0)


def _dispatch_kernel(seg_ref, x_ref, nw_ref, sh_ref, sc_ref, ld_ref, hs_ref, buf_ref, zero_ref, sem,
                     *, n_token_tiles):
    tt = x_ref.shape[0]
    lc = buf_ref.shape[1]
    ne = N_EXPERTS
    tile = pl.program_id(0) * pl.num_programs(1) + pl.program_id(1)
    last = pl.num_programs(0) * pl.num_programs(1) - 1
    slot = tile % 2

    def copies(t, sl, wait):
        def make_copy(lo, go):
            return pltpu.make_async_copy(buf_ref.at[sl, pl.ds(lo, SEG_ROWS), :],
                                         hs_ref.at[pl.ds(go, SEG_ROWS), :], sem.at[sl])
        _segment_copies(seg_ref, t, ne, make_copy, wait=wait)

    h = _modnorm(x_ref[...], nw_ref[...], sh_ref[...], sc_ref[...]).astype(BF16)
    ld = ld_ref[...]
    rows = lax.broadcasted_iota(jnp.int32, (lc, tt), 0)
    perm = jnp.where(rows == ld[0:1, :], 1.0, jnp.where(rows == ld[1:2, :], 1.0, 0.0)).astype(BF16)
    buf_ref[slot] = jnp.dot(perm, h, preferred_element_type=F32).astype(BF16)
    copies(tile, slot, wait=False)

    @pl.when(tile > 0)
    def _():
        copies(tile - 1, 1 - slot, wait=True)

    @pl.when(tile == last)
    def _():
        copies(tile, slot, wait=True)
        zero_ref[...] = jnp.zeros_like(zero_ref)
        tails = n_token_tiles * ne * SEG_FIELDS
        for wait in (False, True):
            for e in range(ne):
                start = seg_ref[tails + 2 * e]

                def body(i, carry, start=start, wait=wait):
                    cp = pltpu.make_async_copy(
                        zero_ref, hs_ref.at[pl.ds(pl.multiple_of(start + i * SEG_ROWS, SEG_ROWS), SEG_ROWS), :],
                        sem.at[2])
                    if wait:
                        cp.wait()
                    else:
                        cp.start()
                    return carry

                lax.fori_loop(0, seg_ref[tails + 2 * e + 1], body, 0)


def moe_dispatch(x, nw, sh, sc, seg, ldest_rows, n_rows, *, tt, lc):
    bsz, s, d = x.shape
    nt = s // tt
    grid_spec = pltpu.PrefetchScalarGridSpec(
        num_scalar_prefetch=1,
        grid=(bsz, nt),
        in_specs=[pl.BlockSpec((None, tt, d), lambda b, i, sref: (b, i, 0)),
                  pl.BlockSpec((1, d), lambda b, i, sref: (0, 0)),
                  pl.BlockSpec((None, 1, d), lambda b, i, sref: (b, 0, 0)),
                  pl.BlockSpec((None, 1, d), lambda b, i, sref: (b, 0, 0)),
                  pl.BlockSpec((TOP_K, tt), lambda b, i, sref: (0, b * nt + i))],
        out_specs=pl.BlockSpec(memory_space=pl.ANY),
        scratch_shapes=[pltpu.VMEM((2, lc, d), BF16), pltpu.VMEM((SEG_ROWS, d), BF16),
                        pltpu.SemaphoreType.DMA((3,))],
    )
    return pl.pallas_call(
        functools.partial(_dispatch_kernel, n_token_tiles=bsz * nt),
        grid_spec=grid_spec,
        out_shape=jax.ShapeDtypeStruct((n_rows, d), BF16),
        compiler_params=_cparams("arbitrary", "arbitrary"),
        name="moe_dispatch",
    )(seg, x, nw.reshape(1, d), sh, sc, ldest_rows)


def _moe_kernel(te_ref, nu_ref, hs_ref, w1_ref, w3_ref, w2_ref, o_ref, acc_ref):
    i = pl.program_id(0)
    f = pl.program_id(1)

    @pl.when(i < nu_ref[0])
    def _():
        @pl.when(f == 0)
        def _():
            acc_ref[...] = jnp.zeros_like(acc_ref)

        h = hs_ref[...]
        for lo in range(0, w1_ref.shape[1], MOE_HIDDEN_CHUNK):
            hi = lo + MOE_HIDDEN_CHUNK
            a = jnp.dot(h, w1_ref[:, lo:hi], preferred_element_type=F32)
            b = jnp.dot(h, w3_ref[:, lo:hi], preferred_element_type=F32)
            t = (_silu(a) * b).astype(BF16)
            acc_ref[...] += jnp.dot(t, w2_ref[lo:hi, :].astype(BF16), preferred_element_type=F32)

        @pl.when(f == pl.num_programs(1) - 1)
        def _():
            o_ref[...] = acc_ref[...].astype(o_ref.dtype)

    @pl.when((i >= nu_ref[0]) & (f == 0))
    def _():
        o_ref[...] = jnp.zeros_like(o_ref)


def moe_experts(hs, tile_expert, n_used, w1, w3, w2, *, tm, tf):
    n_rows, d = hs.shape
    dff = w1.shape[2]
    nf = dff // tf
    n_tiles = n_rows // tm

    def last_used(i, nu):
        return jnp.maximum(jnp.minimum(i, nu[0] - 1), 0)

    def row_map(i, f, te, nu):
        return (last_used(i, nu), 0)

    def hidden_block(i, f, nu):
        t = last_used(i, nu)
        step = jnp.where(i < nu[0], f, nf - 1)
        return jnp.where(t % 2 == 0, step, nf - 1 - step)

    def w_in_map(i, f, te, nu):
        return (te[last_used(i, nu)], 0, hidden_block(i, f, nu))

    def w_out_map(i, f, te, nu):
        return (te[last_used(i, nu)], hidden_block(i, f, nu), 0)

    grid_spec = pltpu.PrefetchScalarGridSpec(
        num_scalar_prefetch=2,
        grid=(n_tiles, nf),
        in_specs=[pl.BlockSpec((tm, d), row_map),
                  pl.BlockSpec((None, d, tf), w_in_map),
                  pl.BlockSpec((None, d, tf), w_in_map),
                  pl.BlockSpec((None, tf, d), w_out_map)],
        out_specs=pl.BlockSpec((tm, d), lambda i, f, te, nu: (i, 0)),
        scratch_shapes=[pltpu.VMEM((tm, d), F32)],
    )
    return pl.pallas_call(
        _moe_kernel,
        grid_spec=grid_spec,
        out_shape=jax.ShapeDtypeStruct((n_rows, d), BF16),
        compiler_params=_cparams("arbitrary", "arbitrary"),
        name="moe_experts",
    )(tile_expert, n_used, hs, w1, w3, w2)


def _combine_kernel(seg_ref, ys_ref, x_ref, g_ref, wgt_ref, ld_ref, o_ref, buf_ref, sem):
    tt = x_ref.shape[0]
    lc = buf_ref.shape[1]
    ne = N_EXPERTS
    tile = pl.program_id(0) * pl.num_programs(1) + pl.program_id(1)
    last = pl.num_programs(0) * pl.num_programs(1) - 1
    slot = tile % 2

    def copies(t, sl, wait):
        def make_copy(lo, go):
            return pltpu.make_async_copy(ys_ref.at[pl.ds(go, SEG_ROWS), :],
                                         buf_ref.at[sl, pl.ds(lo, SEG_ROWS), :], sem.at[sl])
        _segment_copies(seg_ref, t, ne, make_copy, wait=wait)

    @pl.when(tile == 0)
    def _():
        buf_ref[...] = jnp.zeros_like(buf_ref)
        copies(tile, slot, wait=False)

    @pl.when(tile < last)
    def _():
        copies(tile + 1, 1 - slot, wait=False)

    copies(tile, slot, wait=True)

    ld = ld_ref[...]
    cols = lax.broadcasted_iota(jnp.int32, (tt, lc), 1)
    pick = jnp.concatenate([jnp.where(cols == ld[:, k:k + 1], 1.0, 0.0) for k in range(TOP_K)],
                           axis=0).astype(BF16)
    z = jnp.dot(pick, buf_ref[slot], preferred_element_type=F32)
    w = wgt_ref[...]
    mix = w[:, 0:1] * z[0:tt] + w[:, 1:2] * z[tt:]
    o_ref[...] = x_ref[...] + g_ref[...] * mix


def moe_combine(ys, seg, ldest, wgt, x, g, *, tt, lc):
    bsz, s, d = x.shape
    nt = s // tt
    tok_spec = pl.BlockSpec((tt, TOP_K), lambda b, i, sref: (b * nt + i, 0))
    grid_spec = pltpu.PrefetchScalarGridSpec(
        num_scalar_prefetch=1,
        grid=(bsz, nt),
        in_specs=[pl.BlockSpec(memory_space=pl.ANY),
                  pl.BlockSpec((None, tt, d), lambda b, i, sref: (b, i, 0)),
                  pl.BlockSpec((None, 1, d), lambda b, i, sref: (b, 0, 0)),
                  tok_spec, tok_spec],
        out_specs=pl.BlockSpec((None, tt, d), lambda b, i, sref: (b, i, 0)),
        scratch_shapes=[pltpu.VMEM((2, lc, d), BF16), pltpu.SemaphoreType.DMA((2,))],
    )
    return pl.pallas_call(
        _combine_kernel,
        grid_spec=grid_spec,
        out_shape=jax.ShapeDtypeStruct((bsz, s, d), F32),
        compiler_params=_cparams("arbitrary", "arbitrary"),
        name="moe_combine",
    )(seg, ys, x, g, wgt, ldest)


def _round_up(v, m):
    return ((v + m - 1) // m) * m


def moe_token_tile(s):
    return min(512, s)


def moe_residual(x, nw, sh, sc, g, ldest, wgt, cnt, w1, w3, w2, *, tm=MOE_TILE_ROWS):
    bsz, s, d = x.shape
    n_tok = bsz * s
    ne = w1.shape[0]
    tt = moe_token_tile(s)
    n_tt = n_tok // tt
    lc = _round_up(TOP_K * tt + ne * SEG_ROWS, 128)
    seg_len = _round_up(cnt.reshape(n_tt, ne), SEG_ROWS)
    local_start = jnp.cumsum(seg_len, axis=1) - seg_len
    padded = _round_up(jnp.sum(seg_len, axis=0), tm)
    ends = jnp.cumsum(padded)
    global_start = (ends - padded)[None, :] + jnp.cumsum(seg_len, axis=0) - seg_len
    n_rows = _round_up(n_tok * TOP_K + n_tt * ne * SEG_ROWS + ne * tm, tm)
    used_end = (ends - padded) + jnp.sum(seg_len, axis=0)
    next_start = jnp.concatenate([ends[:-1], jnp.full((1,), n_rows, ends.dtype)])
    tails = jnp.stack([used_end, (next_start - used_end) // SEG_ROWS], axis=-1)
    seg = jnp.concatenate([jnp.stack([local_start, global_start, seg_len // SEG_ROWS], axis=-1).reshape(-1),
                           tails.reshape(-1)]).astype(jnp.int32)
    n_tiles = n_rows // tm
    tile_start = jnp.arange(n_tiles, dtype=jnp.int32) * tm
    tile_expert = jnp.minimum(jnp.sum(tile_start[:, None] >= ends[None, :], axis=1), ne - 1).astype(jnp.int32)
    n_used = (ends[ne - 1:ne] // tm).astype(jnp.int32)
    hs = moe_dispatch(x, nw, sh, sc, seg, ldest, n_rows, tt=tt, lc=lc)
    ys = moe_experts(hs, tile_expert, n_used, w1, w3, w2, tm=tm, tf=w1.shape[2] // 2)
    return moe_combine(ys, seg, ldest.T, wgt.T, x, g, tt=tt, lc=lc)


def _split_mod(mod):
    return [m[:, None, :] for m in jnp.split(mod, 6, axis=-1)]


def even_layer(x, c, rel_bias, ada_w, ada_b, norm1_w, in_w, q_norm_w, k_norm_w, sink, out_w,
               norm2_w, w1, w3, w2):
    s = x.shape[1]
    sh1, sc1, g1, sh2, sc2, g2 = _split_mod(ada_mod(c, ada_w, ada_b))
    proj = norm_mod_matmul(x, norm1_w, sh1, sc1, in_w.astype(BF16), ts=min(1024, s), tn=256, name="even_in_proj",
                           out_dtype=BF16)
    yf = fourier_mix(proj, tq=min(512, s))
    ya = window_attention(proj, band_bias(rel_bias), q_norm_w, k_norm_w, sink)
    x = cat_proj_residual(yf, ya, out_w.astype(BF16), x, g1, ts=min(1024, s))
    dff = w1.shape[1]
    return ffn_residual(x, norm2_w, sh2, sc2, g2, w1.astype(BF16), w3.astype(BF16), w2.astype(BF16),
                        ts=min(1024, s), tf=256)


def odd_layer(x, c, ada_w, ada_b, norm1_w, in_w, conv_w, conv_b, dt_bias_f, dt_bias_b, a_log_f, a_log_b,
              d_skip, gnorm_w, out_w, norm2_w, router_w, router_b, w1, w3, w2):
    s = x.shape[1]
    sh1, sc1, g1, sh2, sc2, g2 = _split_mod(ada_mod(c, ada_w, ada_b))
    d_inner = gnorm_w.shape[0]
    cdim = conv_w.shape[1]
    wide = d_inner + cdim
    in_w = in_w.astype(BF16)
    z, xs, bc, dt = ssm_in_proj_conv(x, norm1_w, sh1, sc1, in_w[:, :d_inner], in_w[:, d_inner:wide], in_w[:, wide:],
                                     conv_w, conv_b, n_x=d_inner, ts=min(512, s))
    y = ssd_scan_bidir(xs, bc, dt, dt_bias_f, dt_bias_b, a_log_f, a_log_b, d_skip)
    x, ldest, wgt, cnt = gated_proj_route(y, z, gnorm_w, out_w.astype(BF16), x, g1, norm2_w, sh2, sc2,
                                          router_w, router_b, ts=moe_token_tile(s))
    return moe_residual(x, norm2_w, sh2, sc2, g2, ldest, wgt, cnt,
                        w1.astype(BF16), w3.astype(BF16), w2)


def kernel(x, c, rel_bias, ev_ada_w, ev_ada_b, ev_norm1_w, ev_in_w, ev_q_norm_w, ev_k_norm_w, ev_sink, ev_out_w, ev_norm2_w, ev_ffn_w1, ev_ffn_w3, ev_ffn_w2, od_ada_w, od_ada_b, od_norm1_w, od_in_w, od_conv_w, od_conv_b, od_dt_bias_f, od_dt_bias_b, od_A_log_f, od_A_log_b, od_D, od_gnorm_w, od_out_w, od_norm2_w, od_router_w, od_router_b, od_moe_w1, od_moe_w3, od_moe_w2):
    depth = ev_ada_w.shape[0] + od_ada_w.shape[0]
    for i in range(depth):
        j = i // 2
        if i % 2 == 0:
            x = even_layer(x, c, rel_bias, ev_ada_w[j], ev_ada_b[j], ev_norm1_w[j], ev_in_w[j],
                           ev_q_norm_w[j], ev_k_norm_w[j], ev_sink[j], ev_out_w[j], ev_norm2_w[j],
                           ev_ffn_w1[j], ev_ffn_w3[j], ev_ffn_w2[j])
        else:
            x = odd_layer(x, c, od_ada_w[j], od_ada_b[j], od_norm1_w[j], od_in_w[j], od_conv_w[j],
                          od_conv_b[j], od_dt_bias_f[j], od_dt_bias_b[j], od_A_log_f[j], od_A_log_b[j],
                          od_D[j], od_gnorm_w[j], od_out_w[j], od_norm2_w[j], od_router_w[j],
                          od_router_b[j], od_moe_w1[j], od_moe_w3[j], od_moe_w2[j])
    return x
```

```python
import functools

import numpy as np
import jax
import jax.numpy as jnp
from jax import lax
from jax.experimental import pallas as pl
from jax.experimental.pallas import tpu as pltpu

F32 = jnp.float32
BF16 = jnp.bfloat16

EPS = 1e-6
FNET_GROUPS = 4
FNET_GROUP_DIM = 128
FNET_WIDTH = FNET_GROUPS * FNET_GROUP_DIM
ATTN_HEADS = 8
ATTN_KV_HEADS = 2
HEAD_DIM = 64
ATTN_WIDTH = ATTN_HEADS * HEAD_DIM
KV_WIDTH = ATTN_KV_HEADS * HEAD_DIM
WINDOW = 128
BLOCK = 128
REL_BUCKETS = 32
REL_MAX_DIST = 128
SSM_HEAD_DIM = 64
SSM_GROUPS = 4
D_STATE = 128
CONV_WIDTH = 5
SSD_CHUNK = 128
N_EXPERTS = 8
TOP_K = 2
NEG_BIG = -1e30
ATTN_QBLOCKS = 2

V7X_VMEM_LIMIT_BYTES = 56 * 1024 * 1024
MOE_TILE_ROWS = 512
MOE_HIDDEN_CHUNK = 256


def _cparams(*sem):
    return pltpu.CompilerParams(dimension_semantics=sem, vmem_limit_bytes=V7X_VMEM_LIMIT_BYTES)


def _modnorm(x, nw, sh, sc):
    ms = jnp.mean(x * x, axis=-1, keepdims=True)
    return x * lax.rsqrt(ms + EPS) * nw * (1.0 + sc) + sh


def _silu(x):
    return x * (1.0 / (1.0 + jnp.exp(-x)))


def _ada_kernel(c_ref, w_ref, b_ref, o_ref):
    cs = _silu(c_ref[...]).astype(BF16)
    o_ref[...] = jnp.dot(cs, w_ref[...].astype(BF16), preferred_element_type=F32) + b_ref[...]


def ada_mod(c, w, b):
    bsz, d = c.shape
    n = w.shape[1]
    tn = 1536
    return pl.pallas_call(
        _ada_kernel,
        grid=(n // tn,),
        in_specs=[pl.BlockSpec((bsz, d), lambda j: (0, 0)),
                  pl.BlockSpec((d, tn), lambda j: (0, j)),
                  pl.BlockSpec((1, tn), lambda j: (0, j))],
        out_specs=pl.BlockSpec((bsz, tn), lambda j: (0, j)),
        out_shape=jax.ShapeDtypeStruct((bsz, n), F32),
        compiler_params=_cparams("arbitrary"),
        name="ada_mod",
    )(c, w, b.reshape(1, n))


def _nmm_kernel(x_ref, nw_ref, sh_ref, sc_ref, w_ref, o_ref, *, tn):
    h = _modnorm(x_ref[...], nw_ref[...], sh_ref[...], sc_ref[...]).astype(BF16)
    n = w_ref.shape[1]
    for lo in range(0, n, tn):
        o_ref[:, lo:lo + tn] = jnp.dot(h, w_ref[:, lo:lo + tn],
                                       preferred_element_type=F32).astype(o_ref.dtype)


def _resident(shape):
    return pl.BlockSpec(shape, lambda *_: tuple(0 for _ in shape), pipeline_mode=pl.Buffered(1))


def norm_mod_matmul(x, nw, sh, sc, w, *, ts, tn, name, out_dtype=F32):
    bsz, s, d = x.shape
    n = w.shape[1]
    assert n % tn == 0
    return pl.pallas_call(
        functools.partial(_nmm_kernel, tn=tn),
        grid=(bsz, s // ts),
        in_specs=[pl.BlockSpec((None, ts, d), lambda b, i: (b, i, 0)),
                  pl.BlockSpec((1, d), lambda b, i: (0, 0)),
                  pl.BlockSpec((None, 1, d), lambda b, i: (b, 0, 0)),
                  pl.BlockSpec((None, 1, d), lambda b, i: (b, 0, 0)),
                  _resident((d, n))],
        out_specs=pl.BlockSpec((None, ts, n), lambda b, i: (b, i, 0)),
        out_shape=jax.ShapeDtypeStruct((bsz, s, n), out_dtype),
        compiler_params=_cparams("parallel", "parallel"),
        name=name,
    )(x, nw.reshape(1, d), sh, sc, w)


def _dft_cos_sin(n):
    k = np.arange(n, dtype=np.int64)
    ang = ((k[:, None] * k[None, :]) % n).astype(np.float64) * (2.0 * np.pi / n)
    scale = 1.0 / np.sqrt(n)
    return np.cos(ang) * scale, np.sin(ang) * scale


def _fourier_kernel(u_ref, chan_ref, seq_ref, o_ref, ab_ref):
    s = u_ref.shape[0]

    @pl.when(pl.program_id(1) == 0)
    def _():
        for g in range(FNET_GROUPS):
            lo, hi = g * FNET_GROUP_DIM, (g + 1) * FNET_GROUP_DIM
            ug = u_ref[:, lo:hi].astype(BF16)
            cs = jnp.dot(ug, chan_ref[...], preferred_element_type=F32)
            ab_ref[0:s, lo:hi] = cs[:, :FNET_GROUP_DIM].astype(BF16)
            ab_ref[s:2 * s, lo:hi] = cs[:, FNET_GROUP_DIM:].astype(BF16)

    o_ref[...] = jnp.dot(seq_ref[...], ab_ref[...], preferred_element_type=F32).astype(o_ref.dtype)


def fourier_mix(proj, *, tq):
    bsz, s, _ = proj.shape
    cc, sc = _dft_cos_sin(FNET_GROUP_DIM)
    chan = jnp.asarray(np.concatenate([cc, sc], axis=1), BF16)
    cs, ss = _dft_cos_sin(s)
    seq = jnp.asarray(np.concatenate([cs, -ss], axis=1), BF16)
    return pl.pallas_call(
        _fourier_kernel,
        grid=(bsz, s // tq),
        in_specs=[pl.BlockSpec((None, s, FNET_WIDTH), lambda b, i: (b, 0, 0)),
                  pl.BlockSpec((FNET_GROUP_DIM, 2 * FNET_GROUP_DIM), lambda b, i: (0, 0)),
                  pl.BlockSpec((tq, 2 * s), lambda b, i: (i, 0))],
        out_specs=pl.BlockSpec((None, tq, FNET_WIDTH), lambda b, i: (b, i, 0)),
        out_shape=jax.ShapeDtypeStruct((bsz, s, FNET_WIDTH), BF16),
        scratch_shapes=[pltpu.VMEM((2 * s, FNET_WIDTH), BF16)],
        compiler_params=_cparams("parallel", "arbitrary"),
        name="fourier_mix",
    )(proj, chan, seq)


def _band_bucket_table():
    i = np.arange(BLOCK)[:, None]
    j = np.arange(3 * BLOCK)[None, :]
    rel = (j - BLOCK) - i
    half = REL_BUCKETS // 2
    max_exact = half // 2
    n = np.abs(rel)
    large = max_exact + (np.log(np.maximum(n, 1) / max_exact)
                         / np.log(REL_MAX_DIST / max_exact) * (half - max_exact)).astype(np.int32)
    large = np.minimum(large, half - 1)
    bucket = (rel > 0).astype(np.int32) * half + np.where(n < max_exact, n, large)
    return np.where(n <= WINDOW, bucket, -1).astype(np.int32)


def _bias_kernel(rb_ref, bucket_ref, o_ref):
    h = pl.program_id(0)
    bucket = bucket_ref[...]
    acc = jnp.full(bucket.shape, NEG_BIG, F32)
    for bkt in range(REL_BUCKETS):
        acc = jnp.where(bucket == bkt, rb_ref[bkt * ATTN_HEADS + h], acc)
    o_ref[...] = acc


def band_bias(rel_bias):
    bucket = jnp.asarray(_band_bucket_table())
    return pl.pallas_call(
        _bias_kernel,
        grid=(ATTN_HEADS,),
        in_specs=[pl.BlockSpec(memory_space=pltpu.SMEM),
                  pl.BlockSpec((BLOCK, 3 * BLOCK), lambda h: (0, 0))],
        out_specs=pl.BlockSpec((None, BLOCK, 3 * BLOCK), lambda h: (h, 0, 0)),
        out_shape=jax.ShapeDtypeStruct((ATTN_HEADS, BLOCK, 3 * BLOCK), F32),
        compiler_params=_cparams("arbitrary"),
        name="band_bias",
    )(rel_bias.reshape(-1), bucket)


def _head_mean_matrix(width):
    m = np.zeros((width, width), np.float32)
    for h in range(width // HEAD_DIM):
        m[h * HEAD_DIM:(h + 1) * HEAD_DIM, h * HEAD_DIM:(h + 1) * HEAD_DIM] = 1.0 / HEAD_DIM
    return m


def _heads_rms(t, mean_mat, w):
    sq = t * t
    hi = sq.astype(BF16)
    lo = (sq - hi.astype(F32)).astype(BF16)
    ms = (jnp.dot(hi, mean_mat, preferred_element_type=F32)
          + jnp.dot(lo, mean_mat, preferred_element_type=F32))
    return t * lax.rsqrt(ms + EPS) * w


def _attn_kernel(sink_ref, q_ref, kl_ref, kc_ref, kr_ref, vl_ref, vc_ref, vr_ref,
                 bias_ref, qnw_ref, knw_ref, qmean_ref, kmean_ref, o_ref):
    n = pl.program_id(1)
    nb = pl.num_programs(1) * ATTN_QBLOCKS
    k = jnp.concatenate([kl_ref[...], kc_ref[...], kr_ref[...]], axis=0).astype(F32)
    v = jnp.concatenate([vl_ref[...], vc_ref[...], vr_ref[...]], axis=0).astype(F32)
    col = lax.broadcasted_iota(jnp.int32, (1, 3 * BLOCK), 1)
    qn = _heads_rms(q_ref[...].astype(F32), qmean_ref[...], qnw_ref[...])
    kn = _heads_rms(k, kmean_ref[...], knw_ref[...])
    low = lax.broadcasted_iota(jnp.int32, (1, 2 * HEAD_DIM), 1) < HEAD_DIM
    kn_sw = pltpu.roll(kn, HEAD_DIM, axis=1)
    v_sw = pltpu.roll(v, HEAD_DIM, axis=1)
    k_dup = [jnp.where(low, kn, kn_sw).astype(BF16), jnp.where(low, kn_sw, kn).astype(BF16)]
    ones = jnp.ones((k.shape[0], 2 * HEAD_DIM), BF16)
    v_ext = [jnp.concatenate([v.astype(BF16), ones], axis=1),
             jnp.concatenate([v_sw.astype(BF16), ones], axis=1)]
    g = ATTN_HEADS // ATTN_KV_HEADS
    for qb in range(ATTN_QBLOCKS):
        blk = n * ATTN_QBLOCKS + qb
        band = slice(qb * BLOCK, (qb + 3) * BLOCK)
        first_key = jnp.where(blk == 0, BLOCK, 0)
        end_key = jnp.where(blk == nb - 1, 2 * BLOCK, 3 * BLOCK)
        outside = (col < first_key) | (col >= end_key)
        pairs = []
        for m in range(ATTN_HEADS // 2):
            j = (2 * m) // g
            qp = qn[qb * BLOCK:(qb + 1) * BLOCK, m * 2 * HEAD_DIM:(m + 1) * 2 * HEAD_DIM]
            res = []
            for idx in range(2):
                h = 2 * m + idx
                qm = jnp.where(low if idx == 0 else jnp.logical_not(low), qp, 0.0).astype(BF16)
                logits = lax.dot_general(qm, k_dup[j][band], (((1,), (1,)), ((), ())),
                                         preferred_element_type=F32)
                logits = jnp.where(outside, NEG_BIG, logits + bias_ref[h])
                sk = sink_ref[h]
                mx = jnp.maximum(jnp.max(logits, axis=-1, keepdims=True), sk)
                p = jnp.exp(logits - mx).astype(BF16)
                r = jnp.dot(p, v_ext[idx if j == 0 else 1 - idx][band], preferred_element_type=F32)
                denom = r[:, 2 * HEAD_DIM:] + jnp.exp(sk - mx)
                res.append(r[:, :2 * HEAD_DIM] / denom)
            pairs.append(jnp.where(low, res[0], res[1]))
        o_ref[qb * BLOCK:(qb + 1) * BLOCK, :] = jnp.concatenate(pairs, axis=-1).astype(o_ref.dtype)


def window_attention(proj, bias, q_norm_w, k_norm_w, sink):
    bsz, s, _ = proj.shape
    nb = s // BLOCK
    qcol = FNET_WIDTH // ATTN_WIDTH
    kcol = (FNET_WIDTH + ATTN_WIDTH) // KV_WIDTH
    vcol = kcol + 1

    qb = ATTN_QBLOCKS
    assert nb % qb == 0

    def kv_specs(col):
        return [pl.BlockSpec((None, BLOCK, KV_WIDTH), lambda b, n: (b, jnp.maximum(n * qb - 1, 0), col)),
                pl.BlockSpec((None, qb * BLOCK, KV_WIDTH), lambda b, n: (b, n, col)),
                pl.BlockSpec((None, BLOCK, KV_WIDTH), lambda b, n: (b, jnp.minimum((n + 1) * qb, nb - 1), col))]

    return pl.pallas_call(
        _attn_kernel,
        grid=(bsz, nb // qb),
        in_specs=[pl.BlockSpec(memory_space=pltpu.SMEM),
                  pl.BlockSpec((None, qb * BLOCK, ATTN_WIDTH), lambda b, n: (b, n, qcol)),
                  *kv_specs(kcol), *kv_specs(vcol),
                  pl.BlockSpec((ATTN_HEADS, BLOCK, 3 * BLOCK), lambda b, n: (0, 0, 0)),
                  pl.BlockSpec((1, ATTN_WIDTH), lambda b, n: (0, 0)),
                  pl.BlockSpec((1, KV_WIDTH), lambda b, n: (0, 0)),
                  pl.BlockSpec((ATTN_WIDTH, ATTN_WIDTH), lambda b, n: (0, 0)),
                  pl.BlockSpec((KV_WIDTH, KV_WIDTH), lambda b, n: (0, 0))],
        out_specs=pl.BlockSpec((None, qb * BLOCK, ATTN_WIDTH), lambda b, n: (b, n, 0)),
        out_shape=jax.ShapeDtypeStruct((bsz, s, ATTN_WIDTH), BF16),
        compiler_params=_cparams("parallel", "arbitrary"),
        name="window_attention",
    )(sink, proj, proj, proj, proj, proj, proj, proj, bias,
      (jnp.tile(q_norm_w, ATTN_HEADS) * (HEAD_DIM ** -0.5)).reshape(1, ATTN_WIDTH),
      jnp.tile(k_norm_w, ATTN_KV_HEADS).reshape(1, KV_WIDTH),
      jnp.asarray(_head_mean_matrix(ATTN_WIDTH), BF16), jnp.asarray(_head_mean_matrix(KV_WIDTH), BF16))


def _cat_proj_kernel(a1_ref, a2_ref, w_ref, x_ref, g_ref, o_ref):
    k1 = a1_ref.shape[1]
    y = jnp.dot(a1_ref[...].astype(BF16), w_ref[0:k1, :], preferred_element_type=F32)
    y = y + jnp.dot(a2_ref[...].astype(BF16), w_ref[k1:, :], preferred_element_type=F32)
    o_ref[...] = x_ref[...] + g_ref[...] * y


def cat_proj_residual(a1, a2, w, x, g, *, ts):
    bsz, s, d = x.shape
    k1, k2 = a1.shape[2], a2.shape[2]
    return pl.pallas_call(
        _cat_proj_kernel,
        grid=(bsz, s // ts),
        in_specs=[pl.BlockSpec((None, ts, k1), lambda b, i: (b, i, 0)),
                  pl.BlockSpec((None, ts, k2), lambda b, i: (b, i, 0)),
                  pl.BlockSpec((k1 + k2, d), lambda b, i: (0, 0)),
                  pl.BlockSpec((None, ts, d), lambda b, i: (b, i, 0)),
                  pl.BlockSpec((None, 1, d), lambda b, i: (b, 0, 0))],
        out_specs=pl.BlockSpec((None, ts, d), lambda b, i: (b, i, 0)),
        out_shape=jax.ShapeDtypeStruct((bsz, s, d), F32),
        compiler_params=_cparams("parallel", "parallel"),
        name="mixer_out_proj",
    )(a1, a2, w, x, g)


def _ffn_kernel(x_ref, nw_ref, sh_ref, sc_ref, g_ref, w1_ref, w3_ref, w2_ref, o_ref, acc_ref, *, tf):
    h = _modnorm(x_ref[...], nw_ref[...], sh_ref[...], sc_ref[...]).astype(BF16)
    dff = w1_ref.shape[1]
    for lo in range(0, dff, tf):
        a = jnp.dot(h, w1_ref[:, lo:lo + tf], preferred_element_type=F32)
        b = jnp.dot(h, w3_ref[:, lo:lo + tf], preferred_element_type=F32)
        t = (_silu(a) * b).astype(BF16)
        contrib = jnp.dot(t, w2_ref[lo:lo + tf, :].astype(BF16), preferred_element_type=F32)
        if lo == 0:
            acc_ref[...] = contrib
        else:
            acc_ref[...] += contrib
    o_ref[...] = x_ref[...] + g_ref[...] * acc_ref[...]


def ffn_residual(x, nw, sh, sc, g, w1, w3, w2, *, ts, tf):
    bsz, s, d = x.shape
    dff = w1.shape[1]
    assert dff % tf == 0
    vec = pl.BlockSpec((None, 1, d), lambda b, i: (b, 0, 0))
    return pl.pallas_call(
        functools.partial(_ffn_kernel, tf=tf),
        grid=(bsz, s // ts),
        in_specs=[pl.BlockSpec((None, ts, d), lambda b, i: (b, i, 0)),
                  pl.BlockSpec((1, d), lambda b, i: (0, 0)),
                  vec, vec, vec,
                  _resident((d, dff)), _resident((d, dff)), _resident((dff, d))],
        out_specs=pl.BlockSpec((None, ts, d), lambda b, i: (b, i, 0)),
        out_shape=jax.ShapeDtypeStruct((bsz, s, d), F32),
        scratch_shapes=[pltpu.VMEM((ts, d), F32)],
        compiler_params=_cparams("parallel", "parallel"),
        name="ffn_swiglu",
    )(x, nw.reshape(1, d), sh, sc, g, w1, w3, w2)


HALO_ROWS = 16
PROJ_CHUNK = 256


def _proj_conv_kernel(x_ref, xp_ref, xn_ref, nw_ref, sh_ref, sc_ref, wz_ref, wc_ref, wd_ref, cw_ref, cb_ref,
                      z_ref, xs_ref, bc_ref, dt_ref):
    i = pl.program_id(1)
    ts = x_ref.shape[0]
    half = CONV_WIDTH // 2

    def hnorm(ref):
        return _modnorm(ref[...], nw_ref[...], sh_ref[...], sc_ref[...])

    h = hnorm(x_ref).astype(BF16)
    h_prev = jnp.where(i == 0, 0.0, hnorm(xp_ref)).astype(BF16)
    h_next = jnp.where(i == pl.num_programs(1) - 1, 0.0, hnorm(xn_ref)).astype(BF16)
    h_ext = jnp.concatenate([h_prev, h, h_next], axis=0)

    n_x = xs_ref.shape[1]
    rows = h_ext.shape[0]

    def project(lo):
        return jnp.dot(h_ext, wc_ref[:, lo:lo + PROJ_CHUNK], preferred_element_type=F32)

    def z_chunk(lo):
        hi = lo + PROJ_CHUNK
        z_ref[:, lo:hi] = jnp.dot(h, wz_ref[:, lo:hi], preferred_element_type=F32).astype(z_ref.dtype)

    z_los = list(range(0, wz_ref.shape[1], PROJ_CHUNK))
    c_los = list(range(0, wc_ref.shape[1], PROJ_CHUNK))
    p_next = project(c_los[0])
    for n, lo in enumerate(c_los):
        hi = lo + PROJ_CHUNK
        p = p_next
        if n + 1 < len(c_los):
            p_next = project(c_los[n + 1])
        if z_los:
            z_chunk(z_los.pop(0))
        acc = jnp.zeros((ts, PROJ_CHUNK), F32) + cb_ref[:, lo:hi]
        for kk in range(CONV_WIDTH):
            shifted = p if kk == half else pltpu.roll(p, (half - kk) % rows, axis=0)
            acc = acc + shifted[HALO_ROWS:HALO_ROWS + ts, :] * cw_ref[kk:kk + 1, lo:hi]
        out = _silu(acc)
        if lo < n_x:
            xs_ref[:, lo:hi] = out
        else:
            bc_ref[:, lo - n_x:hi - n_x] = out.astype(bc_ref.dtype)
    for lo in z_los:
        z_chunk(lo)
    dt_ref[...] = jnp.dot(h, wd_ref[...], preferred_element_type=F32)


def ssm_in_proj_conv(x, nw, sh, sc, w_z, w_xbc, w_dt, conv_w, conv_b, *, n_x, ts):
    bsz, s, d = x.shape
    dz = w_z.shape[1]
    dc = w_xbc.shape[1]
    ddt = w_dt.shape[1]
    assert ts % HALO_ROWS == 0 and n_x % PROJ_CHUNK == 0 and dc % PROJ_CHUNK == 0 and dz % PROJ_CHUNK == 0
    r = ts // HALO_ROWS
    last = s // HALO_ROWS - 1
    vec = pl.BlockSpec((None, 1, d), lambda b, i: (b, 0, 0))
    return pl.pallas_call(
        _proj_conv_kernel,
        grid=(bsz, s // ts),
        in_specs=[pl.BlockSpec((None, ts, d), lambda b, i: (b, i, 0)),
                  pl.BlockSpec((None, HALO_ROWS, d), lambda b, i: (b, jnp.maximum(i * r - 1, 0), 0)),
                  pl.BlockSpec((None, HALO_ROWS, d), lambda b, i: (b, jnp.minimum((i + 1) * r, last), 0)),
                  pl.BlockSpec((1, d), lambda b, i: (0, 0)), vec, vec,
                  _resident((d, dz)), _resident((d, dc)), _resident((d, ddt)),
                  _resident((CONV_WIDTH, dc)), _resident((1, dc))],
        out_specs=[pl.BlockSpec((None, ts, dz), lambda b, i: (b, i, 0)),
                   pl.BlockSpec((None, ts, n_x), lambda b, i: (b, i, 0)),
                   pl.BlockSpec((None, ts, dc - n_x), lambda b, i: (b, i, 0)),
                   pl.BlockSpec((None, ts, ddt), lambda b, i: (b, i, 0))],
        out_shape=[jax.ShapeDtypeStruct((bsz, s, dz), BF16),
                   jax.ShapeDtypeStruct((bsz, s, n_x), F32),
                   jax.ShapeDtypeStruct((bsz, s, dc - n_x), BF16),
                   jax.ShapeDtypeStruct((bsz, s, ddt), F32)],
        compiler_params=_cparams("parallel", "parallel"),
        name="odd_in_proj_conv",
    )(x, x, x, nw.reshape(1, d), sh, sc, w_z, w_xbc, w_dt, conv_w, conv_b.reshape(1, dc))


def _softplus(x):
    return jnp.maximum(x, 0.0) + jnp.log(1.0 + jnp.exp(-jnp.abs(x)))


LOG2E = 1.4426950408889634
DECAY_SLOTS = 12


def _bf16_parts3(v):
    hi = v.astype(BF16).astype(F32)
    r = v - hi
    mid = r.astype(BF16).astype(F32)
    lo = (r - mid).astype(BF16).astype(F32)
    return hi, mid, lo


def _ssd_t_kernel(x_ref, b_ref, c_ref, dt_f_ref, dt_b_ref, p_f_ref, p_b_ref, dx_ref, trio_ref, y_ref,
                  ar_ref, dr_ref, sc_ref, pq_ref, qt_ref, xtb_ref, xdf_ref, xdb_ref, yt_ref, hf_ref, hb_ref):
    s = x_ref.shape[0]
    q = SSD_CHUNK
    nh = p_f_ref.shape[0]
    nc = s // q
    hd = SSM_HEAD_DIM

    def row_params(raw_ref, p_ref):
        dt = _softplus(raw_ref[...] + p_ref[:, 0:1])
        return dt, (-LOG2E) * jnp.exp(p_ref[:, 1:2]) * dt

    dtf, af = row_params(dt_f_ref, p_f_ref)
    dtb, ab = row_params(dt_b_ref, p_b_ref)
    dr_ref[0:nh, :] = dtf
    dr_ref[nh:, :] = dtb
    ar_ref[0:nh, :] = af
    ar_ref[nh:, :] = ab

    li = lax.broadcasted_iota(jnp.int32, (q, q), 0)
    si = lax.broadcasted_iota(jnp.int32, (q, q), 1)
    lower = li >= si
    upper = li <= si
    slot_head = lax.broadcasted_iota(jnp.int32, (1, q), 1) % nh

    def bdot(a, b):
        return jnp.dot(a, b, preferred_element_type=F32)

    def ntdot(a, b):
        return lax.dot_general(a, b, (((1,), (1,)), ((), ())), preferred_element_type=F32)

    def head_rows(v):
        return jnp.concatenate([jnp.broadcast_to(v[h:h + 1, :], (hd, q)) for h in range(nh)], axis=0)

    ones = jnp.ones((nh, q), F32)
    zeros = jnp.zeros((nh, q), F32)

    def prep_body(c, carry):
        sl = pl.ds(pl.multiple_of(c * q, q), q)
        a_row = ar_ref[:, sl]
        d_row = dr_ref[:, sl]
        parts = jnp.concatenate([p.astype(BF16) for p in _bf16_parts3(a_row)], axis=0)
        cs = bdot(parts, trio_ref[...])
        cs = cs[0:2 * nh] + cs[2 * nh:4 * nh] + cs[4 * nh:6 * nh]
        i_f = cs[0:nh, 0:q]
        e_b = cs[nh:, 0:q] - a_row[nh:]
        tot_f = cs[0:nh, q:]
        tot_b = cs[nh:, q:]
        ih, im, il = _bf16_parts3(i_f)
        eh, em, el = _bf16_parts3(e_b)
        pad = [zeros] * (q // nh - DECAY_SLOTS)
        p_t = jnp.concatenate([ih, im, il, ones, ones, ones, -eh, -em, -el, ones, ones, ones] + pad, axis=0)
        qf_t = jnp.concatenate([ones, ones, ones, -ih, -im, -il] + [zeros] * 6 + pad, axis=0)
        qb_t = jnp.concatenate([zeros] * 6 + [ones, ones, ones, eh, em, el] + pad, axis=0)
        pq_ref[sl, :] = p_t.T.astype(BF16)
        qt_ref[c] = jnp.concatenate([qf_t, qb_t], axis=1).astype(BF16)
        sc_ref[0 * nh:1 * nh, sl] = jnp.exp2(i_f)
        sc_ref[1 * nh:2 * nh, sl] = jnp.exp2(tot_b - e_b)
        sc_ref[2 * nh:3 * nh, sl] = jnp.exp2(tot_f)
        sc_ref[3 * nh:4 * nh, sl] = jnp.exp2(tot_b)
        xt = x_ref[sl, :].T
        xtb_ref[:, sl] = xt.astype(BF16)
        xdf_ref[:, sl] = (xt * head_rows(jnp.exp2(tot_f - i_f) * d_row[0:nh])).astype(BF16)
        xdb_ref[:, sl] = (xt * head_rows(jnp.exp2(e_b) * d_row[nh:])).astype(BF16)
        return carry

    lax.fori_loop(0, nc, prep_body, 0, unroll=2)

    hf_ref[...] = jnp.zeros_like(hf_ref)
    hb_ref[...] = jnp.zeros_like(hb_ref)
    zero_half = jnp.zeros((hd, q), BF16)

    def fwd_body(c, carry):
        sl = pl.ds(pl.multiple_of(c * q, q), q)
        bc = b_ref[sl, :].astype(BF16)
        cc = c_ref[sl, :].astype(BF16)
        d_row = dr_ref[:, sl]
        p_all = pq_ref[sl, :]
        q_t = qt_ref[c]
        xtb = xtb_ref[:, sl]
        cb = ntdot(cc, bc)
        g2s = [bdot(jnp.where(slot_head == h, p_all, jnp.zeros_like(p_all)), q_t) for h in range(nh)]
        ms = []
        for h in range(nh):
            arg = jnp.where(lower, g2s[h][:, 0:q], g2s[h][:, q:])
            wgt = (jnp.where(lower, d_row[h:h + 1, :], 0.0)
                   + jnp.where(upper, d_row[nh + h:nh + h + 1, :], 0.0))
            ms.append((cb * jnp.exp2(arg) * wgt).astype(BF16))
        yd = []
        for h0 in range(0, nh, 2):
            lhs = jnp.concatenate(
                [jnp.concatenate([xtb[h0 * hd:(h0 + 1) * hd], zero_half], axis=0),
                 jnp.concatenate([zero_half, xtb[(h0 + 1) * hd:(h0 + 2) * hd]], axis=0)], axis=1)
            yd.append(ntdot(lhs, jnp.concatenate(ms[h0:h0 + 2], axis=1)))
        states = bdot(xdf_ref[:, sl], bc)
        h_prev = hf_ref[...]
        y_off = ntdot(h_prev.astype(BF16), cc) * head_rows(sc_ref[0 * nh:1 * nh, sl])
        hf_ref[...] = h_prev * head_rows(sc_ref[2 * nh:3 * nh, sl]) + states
        yt_ref[:, sl] = jnp.concatenate(yd, axis=0) + y_off
        return carry

    lax.fori_loop(0, nc, fwd_body, 0, unroll=4)

    def bwd_body(t, carry):
        c = nc - 1 - t
        sl = pl.ds(pl.multiple_of(c * q, q), q)
        bc = b_ref[sl, :].astype(BF16)
        cc = c_ref[sl, :].astype(BF16)
        states = bdot(xdb_ref[:, sl], bc)
        h_prev = hb_ref[...]
        y_off = ntdot(h_prev.astype(BF16), cc) * head_rows(sc_ref[1 * nh:2 * nh, sl])
        hb_ref[...] = h_prev * head_rows(sc_ref[3 * nh:4 * nh, sl]) + states
        y_ref[sl, :] = ((yt_ref[:, sl] + y_off).T + dx_ref[...] * x_ref[sl, :]).astype(y_ref.dtype)
        return carry

    lax.fori_loop(0, nc, bwd_body, 0, unroll=4)


def ssd_scan_bidir(xs, bc, dt, dt_bias_f, dt_bias_b, a_log_f, a_log_b, d_skip):
    bsz, s, _ = xs.shape
    nheads = dt.shape[2] // 2
    nh = nheads // SSM_GROUPS
    gw = nh * SSM_HEAD_DIM
    d_inner = nheads * SSM_HEAD_DIM
    q = SSD_CHUNK
    dt_row = jnp.transpose(dt.reshape(bsz, s, 2 * SSM_GROUPS, nh), (0, 2, 3, 1))
    prm = jnp.stack([jnp.concatenate([dt_bias_f, dt_bias_b]), jnp.concatenate([a_log_f, a_log_b])])
    p_row = jnp.transpose(prm.reshape(2, 2 * SSM_GROUPS, nh), (1, 2, 0))
    dx = jnp.repeat(d_skip, SSM_HEAD_DIM).reshape(SSM_GROUPS, 1, gw)
    assert DECAY_SLOTS * nh <= q and q % nh == 0 and D_STATE == q
    trio = jnp.asarray(np.concatenate([np.triu(np.ones((q, q), np.float32)), np.ones((q, q), np.float32)],
                                      axis=1), BF16)
    G = SSM_GROUPS
    nc = s // q

    return pl.pallas_call(
        _ssd_t_kernel,
        grid=(bsz, SSM_GROUPS),
        in_specs=[pl.BlockSpec((None, s, gw), lambda b, g: (b, 0, g)),
                  pl.BlockSpec((None, s, D_STATE), lambda b, g: (b, 0, g)),
                  pl.BlockSpec((None, s, D_STATE), lambda b, g: (b, 0, G + g)),
                  pl.BlockSpec((None, None, nh, s), lambda b, g: (b, g, 0, 0)),
                  pl.BlockSpec((None, None, nh, s), lambda b, g: (b, G + g, 0, 0)),
                  pl.BlockSpec((None, nh, 2), lambda b, g: (g, 0, 0)),
                  pl.BlockSpec((None, nh, 2), lambda b, g: (G + g, 0, 0)),
                  pl.BlockSpec((None, 1, gw), lambda b, g: (g, 0, 0)),
                  pl.BlockSpec((q, 2 * q), lambda b, g: (0, 0))],
        out_specs=pl.BlockSpec((None, s, gw), lambda b, g: (b, 0, g)),
        out_shape=jax.ShapeDtypeStruct((bsz, s, d_inner), BF16),
        scratch_shapes=[pltpu.VMEM((2 * nh, s), F32), pltpu.VMEM((2 * nh, s), F32),
                        pltpu.VMEM((4 * nh, s), F32),
                        pltpu.VMEM((s, q), BF16), pltpu.VMEM((nc, q, 2 * q), BF16),
                        pltpu.VMEM((gw, s), BF16), pltpu.VMEM((gw, s), BF16), pltpu.VMEM((gw, s), BF16),
                        pltpu.VMEM((gw, s), F32),
                        pltpu.VMEM((gw, D_STATE), F32), pltpu.VMEM((gw, D_STATE), F32)],
        compiler_params=_cparams("parallel", "parallel"),
        name="ssd_scan",
    )(xs, bc, bc, dt_row, dt_row, p_row, p_row, dx, trio)


def _gated_proj_route_kernel(y_ref, z_ref, gw_ref, w_ref, x_ref, g_ref,
                             nw_ref, sh_ref, sc_ref, rw_ref, rb_ref, lt_ref,
                             o_ref, ld_ref, wgt_ref, cnt_ref, acc_ref):
    _gated_proj_kernel(y_ref, z_ref, gw_ref, w_ref, x_ref, g_ref, o_ref, acc_ref)
    h = _modnorm(o_ref[...], nw_ref[...], sh_ref[...], sc_ref[...])
    rows, gates, total = _route_tile(h, rw_ref[...], rb_ref[...], lt_ref[...])
    ld_ref[...] = rows
    wgt_ref[...] = gates
    cnt_ref[...] = total[:, 0:cnt_ref.shape[1]]


def _gated_proj_kernel(y_ref, z_ref, gw_ref, w_ref, x_ref, g_ref, o_ref, acc_ref):
    k = y_ref.shape[1]
    ss = jnp.zeros((y_ref.shape[0], 1), F32)
    for lo in range(0, k, PROJ_CHUNK):
        hi = lo + PROJ_CHUNK
        t = y_ref[:, lo:hi].astype(F32) * _silu(z_ref[:, lo:hi].astype(F32))
        ss = ss + jnp.sum(t * t, axis=-1, keepdims=True)
        contrib = jnp.dot((t * gw_ref[:, lo:hi]).astype(BF16), w_ref[lo:hi, :], preferred_element_type=F32)
        if lo == 0:
            acc_ref[...] = contrib
        else:
            acc_ref[...] += contrib
    o_ref[...] = x_ref[...] + g_ref[...] * (acc_ref[...] * lax.rsqrt(ss * (1.0 / k) + EPS))


def gated_proj_route(y, z, gw, w, x, g, nw, sh, sc, router_w, router_b, *, ts):
    bsz, s, d = x.shape
    k = y.shape[2]
    ne = router_w.shape[1]
    nt = s // ts
    cum = jnp.asarray(np.concatenate([np.triu(np.ones((ts, ts), np.float32)), np.ones((ts, ts), np.float32)],
                                     axis=1), BF16)
    vec = pl.BlockSpec((None, 1, d), lambda b, i: (b, 0, 0))
    tok_spec = pl.BlockSpec((TOP_K, ts), lambda b, i: (0, b * nt + i))
    call = pl.pallas_call(
        _gated_proj_route_kernel,
        grid=(bsz, nt),
        in_specs=[pl.BlockSpec((None, ts, k), lambda b, i: (b, i, 0)),
                  pl.BlockSpec((None, ts, k), lambda b, i: (b, i, 0)),
                  pl.BlockSpec((1, k), lambda b, i: (0, 0)),
                  _resident((k, d)),
                  pl.BlockSpec((None, ts, d), lambda b, i: (b, i, 0)),
                  vec,
                  pl.BlockSpec((1, d), lambda b, i: (0, 0)), vec, vec,
                  _resident((ne, d)), pl.BlockSpec((ne, 1), lambda b, i: (0, 0)), _resident((ts, 2 * ts))],
        out_specs=[pl.BlockSpec((None, ts, d), lambda b, i: (b, i, 0)), tok_spec, tok_spec,
                   pl.BlockSpec((None, ne, 128), lambda b, i: (b * nt + i, 0, 0))],
        out_shape=[jax.ShapeDtypeStruct((bsz, s, d), F32),
                   jax.ShapeDtypeStruct((TOP_K, bsz * s), jnp.int32),
                   jax.ShapeDtypeStruct((TOP_K, bsz * s), F32),
                   jax.ShapeDtypeStruct((bsz * nt, ne, 128), jnp.int32)],
        scratch_shapes=[pltpu.VMEM((ts, d), F32)],
        compiler_params=_cparams("parallel", "parallel"),
        name="ssd_out_proj_route",
    )
    x_new, ldest, wgt, cnt = call(y, z, gw.reshape(1, k), w, x, g, nw.reshape(1, d), sh, sc,
                                  router_w.T, router_b.reshape(ne, 1), cum)
    return x_new, ldest, wgt, cnt[:, :, 0]


SEG_ROWS = 16
SEG_FIELDS = 3


def _route_tile(h, rw_t, rb_col, cum):
    def nt(a, b):
        return lax.dot_general(a, b, (((1,), (1,)), ((), ())), preferred_element_type=F32)

    h_hi = h.astype(BF16)
    h_lo = (h - h_hi.astype(F32)).astype(BF16)
    rw_hi = rw_t.astype(BF16)
    rw_lo = (rw_t - rw_hi.astype(F32)).astype(BF16)
    logits = nt(rw_hi, h_hi) + (nt(rw_hi, h_lo) + nt(rw_lo, h_hi)) + rb_col
    ne, ts = logits.shape
    eid = lax.broadcasted_iota(jnp.int32, (ne, ts), 0)
    m1 = jnp.max(logits, axis=0, keepdims=True)
    i1 = jnp.min(jnp.where(logits == m1, eid, ne), axis=0, keepdims=True)
    rest = jnp.where(eid == i1, -jnp.inf, logits)
    m2 = jnp.max(rest, axis=0, keepdims=True)
    i2 = jnp.min(jnp.where(rest == m2, eid, ne), axis=0, keepdims=True)
    e2 = jnp.exp(m2 - m1)
    w1 = 1.0 / (1.0 + e2)
    w2 = e2 / (1.0 + e2)
    oh1 = (eid == i1).astype(F32)
    oh2 = (eid == i2).astype(F32)
    chosen = oh1 + oh2
    both = jnp.dot(chosen.astype(BF16), cum, preferred_element_type=F32)
    before = both[:, 0:ts] - chosen
    total = both[:, ts:].astype(jnp.int32)
    seg_len = jnp.bitwise_and(total + (SEG_ROWS - 1), -SEG_ROWS).astype(F32)
    rows = [jnp.sum(jnp.where(eid < idx, seg_len, 0.0) + onehot * before, axis=0, keepdims=True)
            for onehot, idx in ((oh1, i1), (oh2, i2))]
    return jnp.concatenate(rows, axis=0).astype(jnp.int32), jnp.concatenate([w1, w2], axis=0), total


def _segment_copies(seg_ref, tile, n_experts, make_copy, *, wait):
    for e in range(n_experts):
        base = (tile * n_experts + e) * SEG_FIELDS
        local0 = seg_ref[base]
        global0 = seg_ref[base + 1]

        def body(i, carry, local0=local0, global0=global0):
            cp = make_copy(pl.multiple_of(local0 + i * SEG_ROWS, SEG_ROWS),
                           pl.multiple_of(global0 + i * SEG_ROWS, SEG_ROWS))
            if wait:
                cp.wait()
            else:
                cp.start()
            return carry

        lax.fori_loop(0, seg_ref[base + 2], body, 0)


def _dispatch_kernel(seg_ref, x_ref, nw_ref, sh_ref, sc_ref, ld_ref, hs_ref, buf_ref, zero_ref, sem,
                     *, n_token_tiles):
    tt = x_ref.shape[0]
    lc = buf_ref.shape[1]
    ne = N_EXPERTS
    tile = pl.program_id(0) * pl.num_programs(1) + pl.program_id(1)
    last = pl.num_programs(0) * pl.num_programs(1) - 1
    slot = tile % 2

    def copies(t, sl, wait):
        def make_copy(lo, go):
            return pltpu.make_async_copy(buf_ref.at[sl, pl.ds(lo, SEG_ROWS), :],
                                         hs_ref.at[pl.ds(go, SEG_ROWS), :], sem.at[sl])
        _segment_copies(seg_ref, t, ne, make_copy, wait=wait)

    h = _modnorm(x_ref[...], nw_ref[...], sh_ref[...], sc_ref[...]).astype(BF16)
    ld = ld_ref[...]
    rows = lax.broadcasted_iota(jnp.int32, (lc, tt), 0)
    perm = jnp.where(rows == ld[0:1, :], 1.0, jnp.where(rows == ld[1:2, :], 1.0, 0.0)).astype(BF16)
    buf_ref[slot] = jnp.dot(perm, h, preferred_element_type=F32).astype(BF16)
    copies(tile, slot, wait=False)

    @pl.when(tile > 0)
    def _():
        copies(tile - 1, 1 - slot, wait=True)

    @pl.when(tile == last)
    def _():
        copies(tile, slot, wait=True)
        zero_ref[...] = jnp.zeros_like(zero_ref)
        tails = n_token_tiles * ne * SEG_FIELDS
        for wait in (False, True):
            for e in range(ne):
                start = seg_ref[tails + 2 * e]

                def body(i, carry, start=start, wait=wait):
                    cp = pltpu.make_async_copy(
                        zero_ref, hs_ref.at[pl.ds(pl.multiple_of(start + i * SEG_ROWS, SEG_ROWS), SEG_ROWS), :],
                        sem.at[2])
                    if wait:
                        cp.wait()
                    else:
                        cp.start()
                    return carry

                lax.fori_loop(0, seg_ref[tails + 2 * e + 1], body, 0)


def moe_dispatch(x, nw, sh, sc, seg, ldest_rows, n_rows, *, tt, lc):
    bsz, s, d = x.shape
    nt = s // tt
    grid_spec = pltpu.PrefetchScalarGridSpec(
        num_scalar_prefetch=1,
        grid=(bsz, nt),
        in_specs=[pl.BlockSpec((None, tt, d), lambda b, i, sref: (b, i, 0)),
                  pl.BlockSpec((1, d), lambda b, i, sref: (0, 0)),
                  pl.BlockSpec((None, 1, d), lambda b, i, sref: (b, 0, 0)),
                  pl.BlockSpec((None, 1, d), lambda b, i, sref: (b, 0, 0)),
                  pl.BlockSpec((TOP_K, tt), lambda b, i, sref: (0, b * nt + i))],
        out_specs=pl.BlockSpec(memory_space=pl.ANY),
        scratch_shapes=[pltpu.VMEM((2, lc, d), BF16), pltpu.VMEM((SEG_ROWS, d), BF16),
                        pltpu.SemaphoreType.DMA((3,))],
    )
    return pl.pallas_call(
        functools.partial(_dispatch_kernel, n_token_tiles=bsz * nt),
        grid_spec=grid_spec,
        out_shape=jax.ShapeDtypeStruct((n_rows, d), BF16),
        compiler_params=_cparams("arbitrary", "arbitrary"),
        name="moe_dispatch",
    )(seg, x, nw.reshape(1, d), sh, sc, ldest_rows)


def _moe_kernel(te_ref, nu_ref, hs_ref, w1_ref, w3_ref, w2_ref, o_ref, acc_ref):
    i = pl.program_id(0)
    f = pl.program_id(1)

    @pl.when(i < nu_ref[0])
    def _():
        @pl.when(f == 0)
        def _():
            acc_ref[...] = jnp.zeros_like(acc_ref)

        h = hs_ref[...]
        for lo in range(0, w1_ref.shape[1], MOE_HIDDEN_CHUNK):
            hi = lo + MOE_HIDDEN_CHUNK
            a = jnp.dot(h, w1_ref[:, lo:hi].astype(BF16), preferred_element_type=F32)
            b = jnp.dot(h, w3_ref[:, lo:hi].astype(BF16), preferred_element_type=F32)
            t = (_silu(a) * b).astype(BF16)
            acc_ref[...] += jnp.dot(t, w2_ref[lo:hi, :].astype(BF16), preferred_element_type=F32)

        @pl.when(f == pl.num_programs(1) - 1)
        def _():
            o_ref[...] = acc_ref[...].astype(o_ref.dtype)

    @pl.when((i >= nu_ref[0]) & (f == 0))
    def _():
        o_ref[...] = jnp.zeros_like(o_ref)


def moe_experts(hs, tile_expert, n_used, w1, w3, w2, *, tm, tf):
    n_rows, d = hs.shape
    dff = w1.shape[2]
    nf = dff // tf
    n_tiles = n_rows // tm

    def last_used(i, nu):
        return jnp.maximum(jnp.minimum(i, nu[0] - 1), 0)

    def row_map(i, f, te, nu):
        return (last_used(i, nu), 0)

    def hidden_block(i, f, nu):
        t = last_used(i, nu)
        step = jnp.where(i < nu[0], f, nf - 1)
        return jnp.where(t % 2 == 0, step, nf - 1 - step)

    def w_in_map(i, f, te, nu):
        return (te[last_used(i, nu)], 0, hidden_block(i, f, nu))

    def w_out_map(i, f, te, nu):
        return (te[last_used(i, nu)], hidden_block(i, f, nu), 0)

    grid_spec = pltpu.PrefetchScalarGridSpec(
        num_scalar_prefetch=2,
        grid=(n_tiles, nf),
        in_specs=[pl.BlockSpec((tm, d), row_map),
                  pl.BlockSpec((None, d, tf), w_in_map),
                  pl.BlockSpec((None, d, tf), w_in_map),
                  pl.BlockSpec((None, tf, d), w_out_map)],
        out_specs=pl.BlockSpec((tm, d), lambda i, f, te, nu: (i, 0)),
        scratch_shapes=[pltpu.VMEM((tm, d), F32)],
    )
    return pl.pallas_call(
        _moe_kernel,
        grid_spec=grid_spec,
        out_shape=jax.ShapeDtypeStruct((n_rows, d), BF16),
        compiler_params=_cparams("arbitrary", "arbitrary"),
        name="moe_experts",
    )(tile_expert, n_used, hs, w1, w3, w2)


def _combine_kernel(seg_ref, ys_ref, x_ref, g_ref, wgt_ref, ld_ref, o_ref, buf_ref, sem):
    tt = x_ref.shape[0]
    lc = buf_ref.shape[1]
    ne = N_EXPERTS
    tile = pl.program_id(0) * pl.num_programs(1) + pl.program_id(1)
    last = pl.num_programs(0) * pl.num_programs(1) - 1
    slot = tile % 2

    def copies(t, sl, wait):
        def make_copy(lo, go):
            return pltpu.make_async_copy(ys_ref.at[pl.ds(go, SEG_ROWS), :],
                                         buf_ref.at[sl, pl.ds(lo, SEG_ROWS), :], sem.at[sl])
        _segment_copies(seg_ref, t, ne, make_copy, wait=wait)

    @pl.when(tile == 0)
    def _():
        buf_ref[...] = jnp.zeros_like(buf_ref)
        copies(tile, slot, wait=False)

    @pl.when(tile < last)
    def _():
        copies(tile + 1, 1 - slot, wait=False)

    copies(tile, slot, wait=True)

    ld = ld_ref[...]
    cols = lax.broadcasted_iota(jnp.int32, (tt, lc), 1)
    pick = jnp.concatenate([jnp.where(cols == ld[:, k:k + 1], 1.0, 0.0) for k in range(TOP_K)],
                           axis=0).astype(BF16)
    z = jnp.dot(pick, buf_ref[slot], preferred_element_type=F32)
    w = wgt_ref[...]
    mix = w[:, 0:1] * z[0:tt] + w[:, 1:2] * z[tt:]
    o_ref[...] = x_ref[...] + g_ref[...] * mix


def moe_combine(ys, seg, ldest, wgt, x, g, *, tt, lc):
    bsz, s, d = x.shape
    nt = s // tt
    tok_spec = pl.BlockSpec((tt, TOP_K), lambda b, i, sref: (b * nt + i, 0))
    grid_spec = pltpu.PrefetchScalarGridSpec(
        num_scalar_prefetch=1,
        grid=(bsz, nt),
        in_specs=[pl.BlockSpec(memory_space=pl.ANY),
                  pl.BlockSpec((None, tt, d), lambda b, i, sref: (b, i, 0)),
                  pl.BlockSpec((None, 1, d), lambda b, i, sref: (b, 0, 0)),
                  tok_spec, tok_spec],
        out_specs=pl.BlockSpec((None, tt, d), lambda b, i, sref: (b, i, 0)),
        scratch_shapes=[pltpu.VMEM((2, lc, d), BF16), pltpu.SemaphoreType.DMA((2,))],
    )
    return pl.pallas_call(
        _combine_kernel,
        grid_spec=grid_spec,
        out_shape=jax.ShapeDtypeStruct((bsz, s, d), F32),
        compiler_params=_cparams("arbitrary", "arbitrary"),
        name="moe_combine",
    )(seg, ys, x, g, wgt, ldest)


def _round_up(v, m):
    return ((v + m - 1) // m) * m


def moe_token_tile(s):
    return min(512, s)


def moe_residual(x, nw, sh, sc, g, ldest, wgt, cnt, w1, w3, w2, *, tm=MOE_TILE_ROWS):
    bsz, s, d = x.shape
    n_tok = bsz * s
    ne = w1.shape[0]
    tt = moe_token_tile(s)
    n_tt = n_tok // tt
    lc = _round_up(TOP_K * tt + ne * SEG_ROWS, 128)
    seg_len = _round_up(cnt.reshape(n_tt, ne), SEG_ROWS)
    local_start = jnp.cumsum(seg_len, axis=1) - seg_len
    padded = _round_up(jnp.sum(seg_len, axis=0), tm)
    ends = jnp.cumsum(padded)
    global_start = (ends - padded)[None, :] + jnp.cumsum(seg_len, axis=0) - seg_len
    n_rows = _round_up(n_tok * TOP_K + n_tt * ne * SEG_ROWS + ne * tm, tm)
    used_end = (ends - padded) + jnp.sum(seg_len, axis=0)
    next_start = jnp.concatenate([ends[:-1], jnp.full((1,), n_rows, ends.dtype)])
    tails = jnp.stack([used_end, (next_start - used_end) // SEG_ROWS], axis=-1)
    seg = jnp.concatenate([jnp.stack([local_start, global_start, seg_len // SEG_ROWS], axis=-1).reshape(-1),
                           tails.reshape(-1)]).astype(jnp.int32)
    n_tiles = n_rows // tm
    tile_start = jnp.arange(n_tiles, dtype=jnp.int32) * tm
    tile_expert = jnp.minimum(jnp.sum(tile_start[:, None] >= ends[None, :], axis=1), ne - 1).astype(jnp.int32)
    n_used = (ends[ne - 1:ne] // tm).astype(jnp.int32)
    hs = moe_dispatch(x, nw, sh, sc, seg, ldest, n_rows, tt=tt, lc=lc)
    ys = moe_experts(hs, tile_expert, n_used, w1, w3, w2, tm=tm, tf=w1.shape[2] // 2)
    return moe_combine(ys, seg, ldest.T, wgt.T, x, g, tt=tt, lc=lc)


def _split_mod(mod):
    return [m[:, None, :] for m in jnp.split(mod, 6, axis=-1)]


def even_layer(x, c, rel_bias, ada_w, ada_b, norm1_w, in_w, q_norm_w, k_norm_w, sink, out_w,
               norm2_w, w1, w3, w2):
    s = x.shape[1]
    sh1, sc1, g1, sh2, sc2, g2 = _split_mod(ada_mod(c, ada_w, ada_b))
    proj = norm_mod_matmul(x, norm1_w, sh1, sc1, in_w.astype(BF16), ts=min(1024, s), tn=256, name="even_in_proj",
                           out_dtype=BF16)
    yf = fourier_mix(proj, tq=min(512, s))
    ya = window_attention(proj, band_bias(rel_bias), q_norm_w, k_norm_w, sink)
    x = cat_proj_residual(yf, ya, out_w.astype(BF16), x, g1, ts=min(1024, s))
    return ffn_residual(x, norm2_w, sh2, sc2, g2, w1.astype(BF16), w3.astype(BF16), w2,
                        ts=min(1024, s), tf=256)


def odd_layer(x, c, ada_w, ada_b, norm1_w, in_w, conv_w, conv_b, dt_bias_f, dt_bias_b, a_log_f, a_log_b,
              d_skip, gnorm_w, out_w, norm2_w, router_w, router_b, w1, w3, w2):
    s = x.shape[1]
    sh1, sc1, g1, sh2, sc2, g2 = _split_mod(ada_mod(c, ada_w, ada_b))
    d_inner = gnorm_w.shape[0]
    cdim = conv_w.shape[1]
    wide = d_inner + cdim
    in_w = in_w.astype(BF16)
    z, xs, bc, dt = ssm_in_proj_conv(x, norm1_w, sh1, sc1, in_w[:, :d_inner], in_w[:, d_inner:wide], in_w[:, wide:],
                                     conv_w, conv_b, n_x=d_inner, ts=min(512, s))
    y = ssd_scan_bidir(xs, bc, dt, dt_bias_f, dt_bias_b, a_log_f, a_log_b, d_skip)
    x, ldest, wgt, cnt = gated_proj_route(y, z, gnorm_w, out_w.astype(BF16), x, g1, norm2_w, sh2, sc2,
                                          router_w, router_b, ts=moe_token_tile(s))
    return moe_residual(x, norm2_w, sh2, sc2, g2, ldest, wgt, cnt,
                        w1.astype(BF16), w3, w2)


def kernel(x, c, rel_bias, ev_ada_w, ev_ada_b, ev_norm1_w, ev_in_w, ev_q_norm_w, ev_k_norm_w, ev_sink, ev_out_w, ev_norm2_w, ev_ffn_w1, ev_ffn_w3, ev_ffn_w2, od_ada_w, od_ada_b, od_norm1_w, od_in_w, od_conv_w, od_conv_b, od_dt_bias_f, od_dt_bias_b, od_A_log_f, od_A_log_b, od_D, od_gnorm_w, od_out_w, od_norm2_w, od_router_w, od_router_b, od_moe_w1, od_moe_w3, od_moe_w2):
    depth = ev_ada_w.shape[0] + od_ada_w.shape[0]
    for i in range(depth):
        j = i // 2
        if i % 2 == 0:
            x = even_layer(x, c, rel_bias, ev_ada_w[j], ev_ada_b[j], ev_norm1_w[j], ev_in_w[j],
                           ev_q_norm_w[j], ev_k_norm_w[j], ev_sink[j], ev_out_w[j], ev_norm2_w[j],
                           ev_ffn_w1[j], ev_ffn_w3[j], ev_ffn_w2[j])
        else:
            x = odd_layer(x, c, od_ada_w[j], od_ada_b[j], od_norm1_w[j], od_in_w[j], od_conv_w[j],
                          od_conv_b[j], od_dt_bias_f[j], od_dt_bias_b[j], od_A_log_f[j], od_A_log_b[j],
                          od_D[j], od_gnorm_w[j], od_out_w[j], od_norm2_w[j], od_router_w[j],
                          od_router_b[j], od_moe_w1[j], od_moe_w3[j], od_moe_w2[j])
    return x
```

```python
import functools

import numpy as np
import jax
import jax.numpy as jnp
from jax import lax
from jax.experimental import pallas as pl
from jax.experimental.pallas import tpu as pltpu

F32 = jnp.float32
BF16 = jnp.bfloat16

EPS = 1e-6
FNET_GROUPS = 4
FNET_GROUP_DIM = 128
FNET_WIDTH = FNET_GROUPS * FNET_GROUP_DIM
ATTN_HEADS = 8
ATTN_KV_HEADS = 2
HEAD_DIM = 64
ATTN_WIDTH = ATTN_HEADS * HEAD_DIM
KV_WIDTH = ATTN_KV_HEADS * HEAD_DIM
WINDOW = 128
BLOCK = 128
REL_BUCKETS = 32
REL_MAX_DIST = 128
SSM_HEAD_DIM = 64
SSM_GROUPS = 4
D_STATE = 128
CONV_WIDTH = 5
SSD_CHUNK = 128
N_EXPERTS = 8
TOP_K = 2
NEG_BIG = -1e30
ATTN_QBLOCKS = 2

V7X_VMEM_LIMIT_BYTES = 56 * 1024 * 1024
MOE_TILE_ROWS = 768
MOE_HIDDEN_CHUNK = 256


def _cparams(*sem):
    return pltpu.CompilerParams(dimension_semantics=sem, vmem_limit_bytes=V7X_VMEM_LIMIT_BYTES)


def _modnorm(x, nw, sh, sc):
    ms = jnp.mean(x * x, axis=-1, keepdims=True)
    return x * lax.rsqrt(ms + EPS) * nw * (1.0 + sc) + sh


def _silu(x):
    return x * (1.0 / (1.0 + jnp.exp(-x)))


def _ada_kernel(c_ref, w_ref, b_ref, o_ref):
    cs = _silu(c_ref[...]).astype(BF16)
    o_ref[...] = jnp.dot(cs, w_ref[...].astype(BF16), preferred_element_type=F32) + b_ref[...]


def ada_mod(c, w, b):
    bsz, d = c.shape
    n = w.shape[1]
    tn = 1536
    return pl.pallas_call(
        _ada_kernel,
        grid=(n // tn,),
        in_specs=[pl.BlockSpec((bsz, d), lambda j: (0, 0)),
                  pl.BlockSpec((d, tn), lambda j: (0, j)),
                  pl.BlockSpec((1, tn), lambda j: (0, j))],
        out_specs=pl.BlockSpec((bsz, tn), lambda j: (0, j)),
        out_shape=jax.ShapeDtypeStruct((bsz, n), F32),
        compiler_params=_cparams("arbitrary"),
        name="ada_mod",
    )(c, w, b.reshape(1, n))


def _nmm_kernel(x_ref, nw_ref, sh_ref, sc_ref, w_ref, o_ref, *, tn):
    h = _modnorm(x_ref[...], nw_ref[...], sh_ref[...], sc_ref[...]).astype(BF16)
    n = w_ref.shape[1]
    for lo in range(0, n, tn):
        o_ref[:, lo:lo + tn] = jnp.dot(h, w_ref[:, lo:lo + tn],
                                       preferred_element_type=F32).astype(o_ref.dtype)


def _resident(shape):
    return pl.BlockSpec(shape, lambda *_: tuple(0 for _ in shape), pipeline_mode=pl.Buffered(1))


def norm_mod_matmul(x, nw, sh, sc, w, *, ts, tn, name, out_dtype=F32):
    bsz, s, d = x.shape
    n = w.shape[1]
    assert n % tn == 0
    return pl.pallas_call(
        functools.partial(_nmm_kernel, tn=tn),
        grid=(bsz, s // ts),
        in_specs=[pl.BlockSpec((None, ts, d), lambda b, i: (b, i, 0)),
                  pl.BlockSpec((1, d), lambda b, i: (0, 0)),
                  pl.BlockSpec((None, 1, d), lambda b, i: (b, 0, 0)),
                  pl.BlockSpec((None, 1, d), lambda b, i: (b, 0, 0)),
                  _resident((d, n))],
        out_specs=pl.BlockSpec((None, ts, n), lambda b, i: (b, i, 0)),
        out_shape=jax.ShapeDtypeStruct((bsz, s, n), out_dtype),
        compiler_params=_cparams("parallel", "parallel"),
        name=name,
    )(x, nw.reshape(1, d), sh, sc, w)


def _dft_cos_sin(n):
    k = np.arange(n, dtype=np.int64)
    ang = ((k[:, None] * k[None, :]) % n).astype(np.float64) * (2.0 * np.pi / n)
    scale = 1.0 / np.sqrt(n)
    return np.cos(ang) * scale, np.sin(ang) * scale


def _fourier_kernel(u_ref, chan_ref, seq_ref, o_ref, ab_ref):
    s = u_ref.shape[0]

    @pl.when(pl.program_id(1) == 0)
    def _():
        for g in range(FNET_GROUPS):
            lo, hi = g * FNET_GROUP_DIM, (g + 1) * FNET_GROUP_DIM
            ug = u_ref[:, lo:hi].astype(BF16)
            cs = jnp.dot(ug, chan_ref[...], preferred_element_type=F32)
            ab_ref[0:s, lo:hi] = cs[:, :FNET_GROUP_DIM].astype(BF16)
            ab_ref[s:2 * s, lo:hi] = cs[:, FNET_GROUP_DIM:].astype(BF16)

    o_ref[...] = jnp.dot(seq_ref[...], ab_ref[...], preferred_element_type=F32).astype(o_ref.dtype)


def fourier_mix(proj, *, tq):
    bsz, s, _ = proj.shape
    cc, sc = _dft_cos_sin(FNET_GROUP_DIM)
    chan = jnp.asarray(np.concatenate([cc, sc], axis=1), BF16)
    cs, ss = _dft_cos_sin(s)
    seq = jnp.asarray(np.concatenate([cs, -ss], axis=1), BF16)
    return pl.pallas_call(
        _fourier_kernel,
        grid=(bsz, s // tq),
        in_specs=[pl.BlockSpec((None, s, FNET_WIDTH), lambda b, i: (b, 0, 0)),
                  pl.BlockSpec((FNET_GROUP_DIM, 2 * FNET_GROUP_DIM), lambda b, i: (0, 0)),
                  pl.BlockSpec((tq, 2 * s), lambda b, i: (i, 0))],
        out_specs=pl.BlockSpec((None, tq, FNET_WIDTH), lambda b, i: (b, i, 0)),
        out_shape=jax.ShapeDtypeStruct((bsz, s, FNET_WIDTH), BF16),
        scratch_shapes=[pltpu.VMEM((2 * s, FNET_WIDTH), BF16)],
        compiler_params=_cparams("parallel", "arbitrary"),
        name="fourier_mix",
    )(proj, chan, seq)


def _band_bucket_table():
    i = np.arange(BLOCK)[:, None]
    j = np.arange(3 * BLOCK)[None, :]
    rel = (j - BLOCK) - i
    half = REL_BUCKETS // 2
    max_exact = half // 2
    n = np.abs(rel)
    large = max_exact + (np.log(np.maximum(n, 1) / max_exact)
                         / np.log(REL_MAX_DIST / max_exact) * (half - max_exact)).astype(np.int32)
    large = np.minimum(large, half - 1)
    bucket = (rel > 0).astype(np.int32) * half + np.where(n < max_exact, n, large)
    return np.where(n <= WINDOW, bucket, -1).astype(np.int32)


def _bias_kernel(rb_ref, bucket_ref, o_ref):
    h = pl.program_id(0)
    bucket = bucket_ref[...]
    acc = jnp.full(bucket.shape, NEG_BIG, F32)
    for bkt in range(REL_BUCKETS):
        acc = jnp.where(bucket == bkt, rb_ref[bkt * ATTN_HEADS + h], acc)
    o_ref[...] = acc


def band_bias(rel_bias):
    bucket = jnp.asarray(_band_bucket_table())
    return pl.pallas_call(
        _bias_kernel,
        grid=(ATTN_HEADS,),
        in_specs=[pl.BlockSpec(memory_space=pltpu.SMEM),
                  pl.BlockSpec((BLOCK, 3 * BLOCK), lambda h: (0, 0))],
        out_specs=pl.BlockSpec((None, BLOCK, 3 * BLOCK), lambda h: (h, 0, 0)),
        out_shape=jax.ShapeDtypeStruct((ATTN_HEADS, BLOCK, 3 * BLOCK), F32),
        compiler_params=_cparams("arbitrary"),
        name="band_bias",
    )(rel_bias.reshape(-1), bucket)


def _head_mean_matrix(width):
    m = np.zeros((width, width), np.float32)
    for h in range(width // HEAD_DIM):
        m[h * HEAD_DIM:(h + 1) * HEAD_DIM, h * HEAD_DIM:(h + 1) * HEAD_DIM] = 1.0 / HEAD_DIM
    return m


def _heads_rms(t, mean_mat, w):
    sq = t * t
    hi = sq.astype(BF16)
    lo = (sq - hi.astype(F32)).astype(BF16)
    ms = (jnp.dot(hi, mean_mat, preferred_element_type=F32)
          + jnp.dot(lo, mean_mat, preferred_element_type=F32))
    return t * lax.rsqrt(ms + EPS) * w


def _attn_kernel(sink_ref, q_ref, kl_ref, kc_ref, kr_ref, vl_ref, vc_ref, vr_ref,
                 bias_ref, qnw_ref, knw_ref, qmean_ref, kmean_ref, o_ref):
    n = pl.program_id(1)
    nb = pl.num_programs(1) * ATTN_QBLOCKS
    k = jnp.concatenate([kl_ref[...], kc_ref[...], kr_ref[...]], axis=0).astype(F32)
    v = jnp.concatenate([vl_ref[...], vc_ref[...], vr_ref[...]], axis=0).astype(F32)
    col = lax.broadcasted_iota(jnp.int32, (1, 3 * BLOCK), 1)
    qn = _heads_rms(q_ref[...].astype(F32), qmean_ref[...], qnw_ref[...])
    kn = _heads_rms(k, kmean_ref[...], knw_ref[...])
    low = lax.broadcasted_iota(jnp.int32, (1, 2 * HEAD_DIM), 1) < HEAD_DIM
    kn_sw = pltpu.roll(kn, HEAD_DIM, axis=1)
    v_sw = pltpu.roll(v, HEAD_DIM, axis=1)
    k_dup = [jnp.where(low, kn, kn_sw).astype(BF16), jnp.where(low, kn_sw, kn).astype(BF16)]
    ones = jnp.ones((k.shape[0], 2 * HEAD_DIM), BF16)
    v_ext = [jnp.concatenate([v.astype(BF16), ones], axis=1),
             jnp.concatenate([v_sw.astype(BF16), ones], axis=1)]
    g = ATTN_HEADS // ATTN_KV_HEADS
    for qb in range(ATTN_QBLOCKS):
        blk = n * ATTN_QBLOCKS + qb
        band = slice(qb * BLOCK, (qb + 3) * BLOCK)
        first_key = jnp.where(blk == 0, BLOCK, 0)
        end_key = jnp.where(blk == nb - 1, 2 * BLOCK, 3 * BLOCK)
        outside = (col < first_key) | (col >= end_key)
        raw = []
        for h in range(ATTN_HEADS):
            m, idx, j = h // 2, h % 2, h // g
            qp = qn[qb * BLOCK:(qb + 1) * BLOCK, m * 2 * HEAD_DIM:(m + 1) * 2 * HEAD_DIM]
            qm = jnp.where(low if idx == 0 else jnp.logical_not(low), qp, 0.0).astype(BF16)
            raw.append(lax.dot_general(qm, k_dup[j][band], (((1,), (1,)), ((), ())),
                                       preferred_element_type=F32))
        probs, tails = [], []
        for h in range(ATTN_HEADS):
            logits = jnp.where(outside, NEG_BIG, raw[h] + bias_ref[h])
            sk = sink_ref[h]
            mx = jnp.maximum(jnp.max(logits, axis=-1, keepdims=True), sk)
            probs.append(jnp.exp(logits - mx).astype(BF16))
            tails.append(jnp.exp(sk - mx))
        res = []
        for h in range(ATTN_HEADS):
            idx, j = h % 2, h // g
            r = jnp.dot(probs[h], v_ext[idx if j == 0 else 1 - idx][band], preferred_element_type=F32)
            res.append(r[:, :2 * HEAD_DIM] / (r[:, 2 * HEAD_DIM:] + tails[h]))
        pairs = [jnp.where(low, res[2 * m], res[2 * m + 1]) for m in range(ATTN_HEADS // 2)]
        o_ref[qb * BLOCK:(qb + 1) * BLOCK, :] = jnp.concatenate(pairs, axis=-1).astype(o_ref.dtype)


def window_attention(proj, bias, q_norm_w, k_norm_w, sink):
    bsz, s, _ = proj.shape
    nb = s // BLOCK
    qcol = FNET_WIDTH // ATTN_WIDTH
    kcol = (FNET_WIDTH + ATTN_WIDTH) // KV_WIDTH
    vcol = kcol + 1

    qb = ATTN_QBLOCKS
    assert nb % qb == 0

    def kv_specs(col):
        return [pl.BlockSpec((None, BLOCK, KV_WIDTH), lambda b, n: (b, jnp.maximum(n * qb - 1, 0), col)),
                pl.BlockSpec((None, qb * BLOCK, KV_WIDTH), lambda b, n: (b, n, col)),
                pl.BlockSpec((None, BLOCK, KV_WIDTH), lambda b, n: (b, jnp.minimum((n + 1) * qb, nb - 1), col))]

    return pl.pallas_call(
        _attn_kernel,
        grid=(bsz, nb // qb),
        in_specs=[pl.BlockSpec(memory_space=pltpu.SMEM),
                  pl.BlockSpec((None, qb * BLOCK, ATTN_WIDTH), lambda b, n: (b, n, qcol)),
                  *kv_specs(kcol), *kv_specs(vcol),
                  pl.BlockSpec((ATTN_HEADS, BLOCK, 3 * BLOCK), lambda b, n: (0, 0, 0)),
                  pl.BlockSpec((1, ATTN_WIDTH), lambda b, n: (0, 0)),
                  pl.BlockSpec((1, KV_WIDTH), lambda b, n: (0, 0)),
                  pl.BlockSpec((ATTN_WIDTH, ATTN_WIDTH), lambda b, n: (0, 0)),
                  pl.BlockSpec((KV_WIDTH, KV_WIDTH), lambda b, n: (0, 0))],
        out_specs=pl.BlockSpec((None, qb * BLOCK, ATTN_WIDTH), lambda b, n: (b, n, 0)),
        out_shape=jax.ShapeDtypeStruct((bsz, s, ATTN_WIDTH), BF16),
        compiler_params=_cparams("parallel", "arbitrary"),
        name="window_attention",
    )(sink, proj, proj, proj, proj, proj, proj, proj, bias,
      (jnp.tile(q_norm_w, ATTN_HEADS) * (HEAD_DIM ** -0.5)).reshape(1, ATTN_WIDTH),
      jnp.tile(k_norm_w, ATTN_KV_HEADS).reshape(1, KV_WIDTH),
      jnp.asarray(_head_mean_matrix(ATTN_WIDTH), BF16), jnp.asarray(_head_mean_matrix(KV_WIDTH), BF16))


def _cat_proj_kernel(a1_ref, a2_ref, w_ref, x_ref, g_ref, o_ref):
    k1 = a1_ref.shape[1]
    y = jnp.dot(a1_ref[...].astype(BF16), w_ref[0:k1, :], preferred_element_type=F32)
    y = y + jnp.dot(a2_ref[...].astype(BF16), w_ref[k1:, :], preferred_element_type=F32)
    o_ref[...] = x_ref[...] + g_ref[...] * y


def cat_proj_residual(a1, a2, w, x, g, *, ts):
    bsz, s, d = x.shape
    k1, k2 = a1.shape[2], a2.shape[2]
    return pl.pallas_call(
        _cat_proj_kernel,
        grid=(bsz, s // ts),
        in_specs=[pl.BlockSpec((None, ts, k1), lambda b, i: (b, i, 0)),
                  pl.BlockSpec((None, ts, k2), lambda b, i: (b, i, 0)),
                  pl.BlockSpec((k1 + k2, d), lambda b, i: (0, 0)),
                  pl.BlockSpec((None, ts, d), lambda b, i: (b, i, 0)),
                  pl.BlockSpec((None, 1, d), lambda b, i: (b, 0, 0))],
        out_specs=pl.BlockSpec((None, ts, d), lambda b, i: (b, i, 0)),
        out_shape=jax.ShapeDtypeStruct((bsz, s, d), F32),
        compiler_params=_cparams("parallel", "parallel"),
        name="mixer_out_proj",
    )(a1, a2, w, x, g)


def _ffn_kernel(x_ref, nw_ref, sh_ref, sc_ref, g_ref, w1_ref, w3_ref, w2_ref, o_ref, acc_ref, *, tf):
    h = _modnorm(x_ref[...], nw_ref[...], sh_ref[...], sc_ref[...]).astype(BF16)
    dff = w1_ref.shape[1]
    for lo in range(0, dff, tf):
        a = jnp.dot(h, w1_ref[:, lo:lo + tf], preferred_element_type=F32)
        b = jnp.dot(h, w3_ref[:, lo:lo + tf], preferred_element_type=F32)
        t = (_silu(a) * b).astype(BF16)
        contrib = jnp.dot(t, w2_ref[lo:lo + tf, :].astype(BF16), preferred_element_type=F32)
        if lo == 0:
            acc_ref[...] = contrib
        else:
            acc_ref[...] += contrib
    o_ref[...] = x_ref[...] + g_ref[...] * acc_ref[...]


def ffn_residual(x, nw, sh, sc, g, w1, w3, w2, *, ts, tf):
    bsz, s, d = x.shape
    dff = w1.shape[1]
    assert dff % tf == 0
    vec = pl.BlockSpec((None, 1, d), lambda b, i: (b, 0, 0))
    return pl.pallas_call(
        functools.partial(_ffn_kernel, tf=tf),
        grid=(bsz, s // ts),
        in_specs=[pl.BlockSpec((None, ts, d), lambda b, i: (b, i, 0)),
                  pl.BlockSpec((1, d), lambda b, i: (0, 0)),
                  vec, vec, vec,
                  _resident((d, dff)), _resident((d, dff)), _resident((dff, d))],
        out_specs=pl.BlockSpec((None, ts, d), lambda b, i: (b, i, 0)),
        out_shape=jax.ShapeDtypeStruct((bsz, s, d), F32),
        scratch_shapes=[pltpu.VMEM((ts, d), F32)],
        compiler_params=_cparams("parallel", "parallel"),
        name="ffn_swiglu",
    )(x, nw.reshape(1, d), sh, sc, g, w1, w3, w2)


HALO_ROWS = 16
PROJ_CHUNK = 256


def _proj_conv_kernel(x_ref, xp_ref, xn_ref, nw_ref, sh_ref, sc_ref, wz_ref, wc_ref, wd_ref, cw_ref, cb_ref,
                      z_ref, xs_ref, bc_ref, dt_ref):
    i = pl.program_id(1)
    ts = x_ref.shape[0]
    half = CONV_WIDTH // 2

    def hnorm(ref):
        return _modnorm(ref[...], nw_ref[...], sh_ref[...], sc_ref[...])

    h = hnorm(x_ref).astype(BF16)
    h_prev = jnp.where(i == 0, 0.0, hnorm(xp_ref)).astype(BF16)
    h_next = jnp.where(i == pl.num_programs(1) - 1, 0.0, hnorm(xn_ref)).astype(BF16)
    h_ext = jnp.concatenate([h_prev, h, h_next], axis=0)

    n_x = xs_ref.shape[1]
    rows = h_ext.shape[0]

    def project(lo):
        return jnp.dot(h_ext, wc_ref[:, lo:lo + PROJ_CHUNK], preferred_element_type=F32)

    def z_chunk(lo):
        hi = lo + PROJ_CHUNK
        z_ref[:, lo:hi] = jnp.dot(h, wz_ref[:, lo:hi], preferred_element_type=F32).astype(z_ref.dtype)

    z_los = list(range(0, wz_ref.shape[1], PROJ_CHUNK))
    c_los = list(range(0, wc_ref.shape[1], PROJ_CHUNK))
    p_next = project(c_los[0])
    for n, lo in enumerate(c_los):
        hi = lo + PROJ_CHUNK
        p = p_next
        if n + 1 < len(c_los):
            p_next = project(c_los[n + 1])
        if z_los:
            z_chunk(z_los.pop(0))
        acc = jnp.zeros((ts, PROJ_CHUNK), F32) + cb_ref[:, lo:hi]
        for kk in range(CONV_WIDTH):
            shifted = p if kk == half else pltpu.roll(p, (half - kk) % rows, axis=0)
            acc = acc + shifted[HALO_ROWS:HALO_ROWS + ts, :] * cw_ref[kk:kk + 1, lo:hi]
        out = _silu(acc)
        if lo < n_x:
            xs_ref[:, lo:hi] = out
        else:
            bc_ref[:, lo - n_x:hi - n_x] = out.astype(bc_ref.dtype)
    for lo in z_los:
        z_chunk(lo)
    dt_ref[...] = jnp.dot(h, wd_ref[...], preferred_element_type=F32)


def ssm_in_proj_conv(x, nw, sh, sc, w_z, w_xbc, w_dt, conv_w, conv_b, *, n_x, ts):
    bsz, s, d = x.shape
    dz = w_z.shape[1]
    dc = w_xbc.shape[1]
    ddt = w_dt.shape[1]
    assert ts % HALO_ROWS == 0 and n_x % PROJ_CHUNK == 0 and dc % PROJ_CHUNK == 0 and dz % PROJ_CHUNK == 0
    r = ts // HALO_ROWS
    last = s // HALO_ROWS - 1
    vec = pl.BlockSpec((None, 1, d), lambda b, i: (b, 0, 0))
    return pl.pallas_call(
        _proj_conv_kernel,
        grid=(bsz, s // ts),
        in_specs=[pl.BlockSpec((None, ts, d), lambda b, i: (b, i, 0)),
                  pl.BlockSpec((None, HALO_ROWS, d), lambda b, i: (b, jnp.maximum(i * r - 1, 0), 0)),
                  pl.BlockSpec((None, HALO_ROWS, d), lambda b, i: (b, jnp.minimum((i + 1) * r, last), 0)),
                  pl.BlockSpec((1, d), lambda b, i: (0, 0)), vec, vec,
                  _resident((d, dz)), _resident((d, dc)), _resident((d, ddt)),
                  _resident((CONV_WIDTH, dc)), _resident((1, dc))],
        out_specs=[pl.BlockSpec((None, ts, dz), lambda b, i: (b, i, 0)),
                   pl.BlockSpec((None, ts, n_x), lambda b, i: (b, i, 0)),
                   pl.BlockSpec((None, ts, dc - n_x), lambda b, i: (b, i, 0)),
                   pl.BlockSpec((None, ts, ddt), lambda b, i: (b, i, 0))],
        out_shape=[jax.ShapeDtypeStruct((bsz, s, dz), BF16),
                   jax.ShapeDtypeStruct((bsz, s, n_x), F32),
                   jax.ShapeDtypeStruct((bsz, s, dc - n_x), BF16),
                   jax.ShapeDtypeStruct((bsz, s, ddt), F32)],
        compiler_params=_cparams("parallel", "parallel"),
        name="odd_in_proj_conv",
    )(x, x, x, nw.reshape(1, d), sh, sc, w_z, w_xbc, w_dt, conv_w, conv_b.reshape(1, dc))


def _softplus(x):
    return jnp.maximum(x, 0.0) + jnp.log(1.0 + jnp.exp(-jnp.abs(x)))


LOG2E = 1.4426950408889634
DECAY_SLOTS = 12


def _bf16_parts3(v):
    hi = v.astype(BF16).astype(F32)
    r = v - hi
    mid = r.astype(BF16).astype(F32)
    lo = (r - mid).astype(BF16).astype(F32)
    return hi, mid, lo


def _ssd_t_kernel(x_ref, b_ref, c_ref, dt_f_ref, dt_b_ref, p_f_ref, p_b_ref, dx_ref, trio_ref, y_ref,
                  ar_ref, dr_ref, sc_ref, pq_ref, qt_ref, xtb_ref, xdf_ref, xdb_ref, yt_ref, hf_ref, hb_ref):
    s = x_ref.shape[0]
    q = SSD_CHUNK
    nh = p_f_ref.shape[0]
    nc = s // q
    hd = SSM_HEAD_DIM

    def row_params(raw_ref, p_ref):
        dt = _softplus(raw_ref[...] + p_ref[:, 0:1])
        return dt, (-LOG2E) * jnp.exp(p_ref[:, 1:2]) * dt

    dtf, af = row_params(dt_f_ref, p_f_ref)
    dtb, ab = row_params(dt_b_ref, p_b_ref)
    dr_ref[0:nh, :] = dtf
    dr_ref[nh:, :] = dtb
    ar_ref[0:nh, :] = af
    ar_ref[nh:, :] = ab

    li = lax.broadcasted_iota(jnp.int32, (q, q), 0)
    si = lax.broadcasted_iota(jnp.int32, (q, q), 1)
    lower = li >= si
    upper = li <= si
    slot_head = lax.broadcasted_iota(jnp.int32, (1, q), 1) % nh

    def bdot(a, b):
        return jnp.dot(a, b, preferred_element_type=F32)

    def ntdot(a, b):
        return lax.dot_general(a, b, (((1,), (1,)), ((), ())), preferred_element_type=F32)

    def head_rows(v):
        return jnp.concatenate([jnp.broadcast_to(v[h:h + 1, :], (hd, q)) for h in range(nh)], axis=0)

    ones = jnp.ones((nh, q), F32)
    zeros = jnp.zeros((nh, q), F32)

    def prep_body(c, carry):
        sl = pl.ds(pl.multiple_of(c * q, q), q)
        a_row = ar_ref[:, sl]
        d_row = dr_ref[:, sl]
        parts = jnp.concatenate([p.astype(BF16) for p in _bf16_parts3(a_row)], axis=0)
        cs = bdot(parts, trio_ref[...])
        cs = cs[0:2 * nh] + cs[2 * nh:4 * nh] + cs[4 * nh:6 * nh]
        i_f = cs[0:nh, 0:q]
        e_b = cs[nh:, 0:q] - a_row[nh:]
        tot_f = cs[0:nh, q:]
        tot_b = cs[nh:, q:]
        ih, im, il = _bf16_parts3(i_f)
        eh, em, el = _bf16_parts3(e_b)
        pad = [zeros] * (q // nh - DECAY_SLOTS)
        p_t = jnp.concatenate([ih, im, il, ones, ones, ones, -eh, -em, -el, ones, ones, ones] + pad, axis=0)
        qf_t = jnp.concatenate([ones, ones, ones, -ih, -im, -il] + [zeros] * 6 + pad, axis=0)
        qb_t = jnp.concatenate([zeros] * 6 + [ones, ones, ones, eh, em, el] + pad, axis=0)
        pq_ref[sl, :] = p_t.T.astype(BF16)
        qt_ref[c] = jnp.concatenate([qf_t, qb_t], axis=1).astype(BF16)
        sc_ref[0 * nh:1 * nh, sl] = jnp.exp2(i_f)
        sc_ref[1 * nh:2 * nh, sl] = jnp.exp2(tot_b - e_b)
        sc_ref[2 * nh:3 * nh, sl] = jnp.exp2(tot_f)
        sc_ref[3 * nh:4 * nh, sl] = jnp.exp2(tot_b)
        xt = x_ref[sl, :].T
        xtb_ref[:, sl] = xt.astype(BF16)
        xdf_ref[:, sl] = (xt * head_rows(jnp.exp2(tot_f - i_f) * d_row[0:nh])).astype(BF16)
        xdb_ref[:, sl] = (xt * head_rows(jnp.exp2(e_b) * d_row[nh:])).astype(BF16)
        return carry

    lax.fori_loop(0, nc, prep_body, 0, unroll=2)

    hf_ref[...] = jnp.zeros_like(hf_ref)
    hb_ref[...] = jnp.zeros_like(hb_ref)
    zero_half = jnp.zeros((hd, q), BF16)

    def fwd_body(c, carry):
        sl = pl.ds(pl.multiple_of(c * q, q), q)
        bc = b_ref[sl, :].astype(BF16)
        cc = c_ref[sl, :].astype(BF16)
        d_row = dr_ref[:, sl]
        p_all = pq_ref[sl, :]
        q_t = qt_ref[c]
        xtb = xtb_ref[:, sl]
        cb = ntdot(cc, bc)
        g2s = [bdot(jnp.where(slot_head == h, p_all, jnp.zeros_like(p_all)), q_t) for h in range(nh)]
        ms = []
        for h in range(nh):
            arg = jnp.where(lower, g2s[h][:, 0:q], g2s[h][:, q:])
            wgt = (jnp.where(lower, d_row[h:h + 1, :], 0.0)
                   + jnp.where(upper, d_row[nh + h:nh + h + 1, :], 0.0))
            ms.append((cb * jnp.exp2(arg) * wgt).astype(BF16))
        yd = []
        for h0 in range(0, nh, 2):
            lhs = jnp.concatenate(
                [jnp.concatenate([xtb[h0 * hd:(h0 + 1) * hd], zero_half], axis=0),
                 jnp.concatenate([zero_half, xtb[(h0 + 1) * hd:(h0 + 2) * hd]], axis=0)], axis=1)
            yd.append(ntdot(lhs, jnp.concatenate(ms[h0:h0 + 2], axis=1)))
        states = bdot(xdf_ref[:, sl], bc)
        h_prev = hf_ref[...]
        y_off = ntdot(h_prev.astype(BF16), cc) * head_rows(sc_ref[0 * nh:1 * nh, sl])
        hf_ref[...] = h_prev * head_rows(sc_ref[2 * nh:3 * nh, sl]) + states
        yt_ref[:, sl] = jnp.concatenate(yd, axis=0) + y_off
        return carry

    lax.fori_loop(0, nc, fwd_body, 0, unroll=4)

    def bwd_body(t, carry):
        c = nc - 1 - t
        sl = pl.ds(pl.multiple_of(c * q, q), q)
        bc = b_ref[sl, :].astype(BF16)
        cc = c_ref[sl, :].astype(BF16)
        states = bdot(xdb_ref[:, sl], bc)
        h_prev = hb_ref[...]
        y_off = ntdot(h_prev.astype(BF16), cc) * head_rows(sc_ref[1 * nh:2 * nh, sl])
        hb_ref[...] = h_prev * head_rows(sc_ref[3 * nh:4 * nh, sl]) + states
        y_ref[sl, :] = ((yt_ref[:, sl] + y_off).T + dx_ref[...] * x_ref[sl, :]).astype(y_ref.dtype)
        return carry

    lax.fori_loop(0, nc, bwd_body, 0, unroll=4)


def ssd_scan_bidir(xs, bc, dt, dt_bias_f, dt_bias_b, a_log_f, a_log_b, d_skip):
    bsz, s, _ = xs.shape
    nheads = dt.shape[2] // 2
    nh = nheads // SSM_GROUPS
    gw = nh * SSM_HEAD_DIM
    d_inner = nheads * SSM_HEAD_DIM
    q = SSD_CHUNK
    dt_row = jnp.transpose(dt.reshape(bsz, s, 2 * SSM_GROUPS, nh), (0, 2, 3, 1))
    prm = jnp.stack([jnp.concatenate([dt_bias_f, dt_bias_b]), jnp.concatenate([a_log_f, a_log_b])])
    p_row = jnp.transpose(prm.reshape(2, 2 * SSM_GROUPS, nh), (1, 2, 0))
    dx = jnp.repeat(d_skip, SSM_HEAD_DIM).reshape(SSM_GROUPS, 1, gw)
    assert DECAY_SLOTS * nh <= q and q % nh == 0 and D_STATE == q
    trio = jnp.asarray(np.concatenate([np.triu(np.ones((q, q), np.float32)), np.ones((q, q), np.float32)],
                                      axis=1), BF16)
    G = SSM_GROUPS
    nc = s // q

    return pl.pallas_call(
        _ssd_t_kernel,
        grid=(bsz, SSM_GROUPS),
        in_specs=[pl.BlockSpec((None, s, gw), lambda b, g: (b, 0, g)),
                  pl.BlockSpec((None, s, D_STATE), lambda b, g: (b, 0, g)),
                  pl.BlockSpec((None, s, D_STATE), lambda b, g: (b, 0, G + g)),
                  pl.BlockSpec((None, None, nh, s), lambda b, g: (b, g, 0, 0)),
                  pl.BlockSpec((None, None, nh, s), lambda b, g: (b, G + g, 0, 0)),
                  pl.BlockSpec((None, nh, 2), lambda b, g: (g, 0, 0)),
                  pl.BlockSpec((None, nh, 2), lambda b, g: (G + g, 0, 0)),
                  pl.BlockSpec((None, 1, gw), lambda b, g: (g, 0, 0)),
                  pl.BlockSpec((q, 2 * q), lambda b, g: (0, 0))],
        out_specs=pl.BlockSpec((None, s, gw), lambda b, g: (b, 0, g)),
        out_shape=jax.ShapeDtypeStruct((bsz, s, d_inner), BF16),
        scratch_shapes=[pltpu.VMEM((2 * nh, s), F32), pltpu.VMEM((2 * nh, s), F32),
                        pltpu.VMEM((4 * nh, s), F32),
                        pltpu.VMEM((s, q), BF16), pltpu.VMEM((nc, q, 2 * q), BF16),
                        pltpu.VMEM((gw, s), BF16), pltpu.VMEM((gw, s), BF16), pltpu.VMEM((gw, s), BF16),
                        pltpu.VMEM((gw, s), F32),
                        pltpu.VMEM((gw, D_STATE), F32), pltpu.VMEM((gw, D_STATE), F32)],
        compiler_params=_cparams("parallel", "parallel"),
        name="ssd_scan",
    )(xs, bc, bc, dt_row, dt_row, p_row, p_row, dx, trio)


def _gated_proj_route_kernel(y_ref, z_ref, gw_ref, w_ref, x_ref, g_ref,
                             nw_ref, sh_ref, sc_ref, rw_ref, rb_ref, lt_ref,
                             o_ref, ld_ref, wgt_ref, cnt_ref, acc_ref):
    _gated_proj_kernel(y_ref, z_ref, gw_ref, w_ref, x_ref, g_ref, o_ref, acc_ref)
    h = _modnorm(o_ref[...], nw_ref[...], sh_ref[...], sc_ref[...])
    rows, gates, total = _route_tile(h, rw_ref[...], rb_ref[...], lt_ref[...])
    ld_ref[...] = rows
    wgt_ref[...] = gates
    cnt_ref[...] = total[:, 0:cnt_ref.shape[1]]


def _gated_proj_kernel(y_ref, z_ref, gw_ref, w_ref, x_ref, g_ref, o_ref, acc_ref):
    k = y_ref.shape[1]
    ss = jnp.zeros((y_ref.shape[0], 1), F32)
    for lo in range(0, k, PROJ_CHUNK):
        hi = lo + PROJ_CHUNK
        t = y_ref[:, lo:hi].astype(F32) * _silu(z_ref[:, lo:hi].astype(F32))
        ss = ss + jnp.sum(t * t, axis=-1, keepdims=True)
        contrib = jnp.dot((t * gw_ref[:, lo:hi]).astype(BF16), w_ref[lo:hi, :], preferred_element_type=F32)
        if lo == 0:
            acc_ref[...] = contrib
        else:
            acc_ref[...] += contrib
    o_ref[...] = x_ref[...] + g_ref[...] * (acc_ref[...] * lax.rsqrt(ss * (1.0 / k) + EPS))


def gated_proj_route(y, z, gw, w, x, g, nw, sh, sc, router_w, router_b, *, ts):
    bsz, s, d = x.shape
    k = y.shape[2]
    ne = router_w.shape[1]
    nt = s // ts
    cum = jnp.asarray(np.concatenate([np.triu(np.ones((ts, ts), np.float32)), np.ones((ts, ts), np.float32)],
                                     axis=1), BF16)
    vec = pl.BlockSpec((None, 1, d), lambda b, i: (b, 0, 0))
    tok_spec = pl.BlockSpec((TOP_K, ts), lambda b, i: (0, b * nt + i))
    call = pl.pallas_call(
        _gated_proj_route_kernel,
        grid=(bsz, nt),
        in_specs=[pl.BlockSpec((None, ts, k), lambda b, i: (b, i, 0)),
                  pl.BlockSpec((None, ts, k), lambda b, i: (b, i, 0)),
                  pl.BlockSpec((1, k), lambda b, i: (0, 0)),
                  _resident((k, d)),
                  pl.BlockSpec((None, ts, d), lambda b, i: (b, i, 0)),
                  vec,
                  pl.BlockSpec((1, d), lambda b, i: (0, 0)), vec, vec,
                  _resident((ne, d)), pl.BlockSpec((ne, 1), lambda b, i: (0, 0)), _resident((ts, 2 * ts))],
        out_specs=[pl.BlockSpec((None, ts, d), lambda b, i: (b, i, 0)), tok_spec, tok_spec,
                   pl.BlockSpec((None, ne, 128), lambda b, i: (b * nt + i, 0, 0))],
        out_shape=[jax.ShapeDtypeStruct((bsz, s, d), F32),
                   jax.ShapeDtypeStruct((TOP_K, bsz * s), jnp.int32),
                   jax.ShapeDtypeStruct((TOP_K, bsz * s), F32),
                   jax.ShapeDtypeStruct((bsz * nt, ne, 128), jnp.int32)],
        scratch_shapes=[pltpu.VMEM((ts, d), F32)],
        compiler_params=_cparams("parallel", "parallel"),
        name="ssd_out_proj_route",
    )
    x_new, ldest, wgt, cnt = call(y, z, gw.reshape(1, k), w, x, g, nw.reshape(1, d), sh, sc,
                                  router_w.T, router_b.reshape(ne, 1), cum)
    return x_new, ldest, wgt, cnt[:, :, 0]


SEG_ROWS = 16
SEG_FIELDS = 3


def _route_tile(h, rw_t, rb_col, cum):
    def nt(a, b):
        return lax.dot_general(a, b, (((1,), (1,)), ((), ())), preferred_element_type=F32)

    h_hi = h.astype(BF16)
    h_lo = (h - h_hi.astype(F32)).astype(BF16)
    rw_hi = rw_t.astype(BF16)
    rw_lo = (rw_t - rw_hi.astype(F32)).astype(BF16)
    logits = nt(rw_hi, h_hi) + (nt(rw_hi, h_lo) + nt(rw_lo, h_hi)) + rb_col
    ne, ts = logits.shape
    eid = lax.broadcasted_iota(jnp.int32, (ne, ts), 0)
    m1 = jnp.max(logits, axis=0, keepdims=True)
    i1 = jnp.min(jnp.where(logits == m1, eid, ne), axis=0, keepdims=True)
    rest = jnp.where(eid == i1, -jnp.inf, logits)
    m2 = jnp.max(rest, axis=0, keepdims=True)
    i2 = jnp.min(jnp.where(rest == m2, eid, ne), axis=0, keepdims=True)
    e2 = jnp.exp(m2 - m1)
    w1 = 1.0 / (1.0 + e2)
    w2 = e2 / (1.0 + e2)
    oh1 = (eid == i1).astype(F32)
    oh2 = (eid == i2).astype(F32)
    chosen = oh1 + oh2
    both = jnp.dot(chosen.astype(BF16), cum, preferred_element_type=F32)
    before = both[:, 0:ts] - chosen
    total = both[:, ts:].astype(jnp.int32)
    seg_len = jnp.bitwise_and(total + (SEG_ROWS - 1), -SEG_ROWS).astype(F32)
    rows = [jnp.sum(jnp.where(eid < idx, seg_len, 0.0) + onehot * before, axis=0, keepdims=True)
            for onehot, idx in ((oh1, i1), (oh2, i2))]
    return jnp.concatenate(rows, axis=0).astype(jnp.int32), jnp.concatenate([w1, w2], axis=0), total


def _segment_copies(seg_ref, tile, n_experts, make_copy, *, wait):
    for e in range(n_experts):
        base = (tile * n_experts + e) * SEG_FIELDS
        local0 = seg_ref[base]
        global0 = seg_ref[base + 1]

        def body(i, carry, local0=local0, global0=global0):
            cp = make_copy(pl.multiple_of(local0 + i * SEG_ROWS, SEG_ROWS),
                           pl.multiple_of(global0 + i * SEG_ROWS, SEG_ROWS))
            if wait:
                cp.wait()
            else:
                cp.start()
            return carry

        lax.fori_loop(0, seg_ref[base + 2], body, 0)


def _dispatch_kernel(seg_ref, x_ref, nw_ref, sh_ref, sc_ref, ld_ref, hs_ref, buf_ref, zero_ref, sem,
                     *, n_token_tiles):
    tt = x_ref.shape[0]
    lc = buf_ref.shape[1]
    ne = N_EXPERTS
    tile = pl.program_id(0) * pl.num_programs(1) + pl.program_id(1)
    last = pl.num_programs(0) * pl.num_programs(1) - 1
    slot = tile % 2

    def copies(t, sl, wait):
        def make_copy(lo, go):
            return pltpu.make_async_copy(buf_ref.at[sl, pl.ds(lo, SEG_ROWS), :],
                                         hs_ref.at[pl.ds(go, SEG_ROWS), :], sem.at[sl])
        _segment_copies(seg_ref, t, ne, make_copy, wait=wait)

    h = _modnorm(x_ref[...], nw_ref[...], sh_ref[...], sc_ref[...]).astype(BF16)
    ld = ld_ref[...]
    rows = lax.broadcasted_iota(jnp.int32, (lc, tt), 0)
    perm = jnp.where(rows == ld[0:1, :], 1.0, jnp.where(rows == ld[1:2, :], 1.0, 0.0)).astype(BF16)
    buf_ref[slot] = jnp.dot(perm, h, preferred_element_type=F32).astype(BF16)
    copies(tile, slot, wait=False)

    @pl.when(tile > 0)
    def _():
        copies(tile - 1, 1 - slot, wait=True)

    @pl.when(tile == last)
    def _():
        copies(tile, slot, wait=True)
        zero_ref[...] = jnp.zeros_like(zero_ref)
        tails = n_token_tiles * ne * SEG_FIELDS
        for wait in (False, True):
            for e in range(ne):
                start = seg_ref[tails + 2 * e]

                def body(i, carry, start=start, wait=wait):
                    cp = pltpu.make_async_copy(
                        zero_ref, hs_ref.at[pl.ds(pl.multiple_of(start + i * SEG_ROWS, SEG_ROWS), SEG_ROWS), :],
                        sem.at[2])
                    if wait:
                        cp.wait()
                    else:
                        cp.start()
                    return carry

                lax.fori_loop(0, seg_ref[tails + 2 * e + 1], body, 0)


def moe_dispatch(x, nw, sh, sc, seg, ldest_rows, n_rows, *, tt, lc):
    bsz, s, d = x.shape
    nt = s // tt
    grid_spec = pltpu.PrefetchScalarGridSpec(
        num_scalar_prefetch=1,
        grid=(bsz, nt),
        in_specs=[pl.BlockSpec((None, tt, d), lambda b, i, sref: (b, i, 0)),
                  pl.BlockSpec((1, d), lambda b, i, sref: (0, 0)),
                  pl.BlockSpec((None, 1, d), lambda b, i, sref: (b, 0, 0)),
                  pl.BlockSpec((None, 1, d), lambda b, i, sref: (b, 0, 0)),
                  pl.BlockSpec((TOP_K, tt), lambda b, i, sref: (0, b * nt + i))],
        out_specs=pl.BlockSpec(memory_space=pl.ANY),
        scratch_shapes=[pltpu.VMEM((2, lc, d), BF16), pltpu.VMEM((SEG_ROWS, d), BF16),
                        pltpu.SemaphoreType.DMA((3,))],
    )
    return pl.pallas_call(
        functools.partial(_dispatch_kernel, n_token_tiles=bsz * nt),
        grid_spec=grid_spec,
        out_shape=jax.ShapeDtypeStruct((n_rows, d), BF16),
        compiler_params=_cparams("arbitrary", "arbitrary"),
        name="moe_dispatch",
    )(seg, x, nw.reshape(1, d), sh, sc, ldest_rows)


def _moe_kernel(te_ref, nu_ref, hs_ref, w1_ref, w3_ref, w2_ref, o_ref, acc_ref):
    i = pl.program_id(0)
    f = pl.program_id(1)

    @pl.when(i < nu_ref[0])
    def _():
        @pl.when(f == 0)
        def _():
            acc_ref[...] = jnp.zeros_like(acc_ref)

        h = hs_ref[...]
        for lo in range(0, w1_ref.shape[1], MOE_HIDDEN_CHUNK):
            hi = min(lo + MOE_HIDDEN_CHUNK, w1_ref.shape[1])
            a = jnp.dot(h, w1_ref[:, lo:hi].astype(BF16), preferred_element_type=F32)
            b = jnp.dot(h, w3_ref[:, lo:hi].astype(BF16), preferred_element_type=F32)
            t = (_silu(a) * b).astype(BF16)
            acc_ref[...] += jnp.dot(t, w2_ref[lo:hi, :].astype(BF16), preferred_element_type=F32)

        @pl.when(f == pl.num_programs(1) - 1)
        def _():
            o_ref[...] = acc_ref[...].astype(o_ref.dtype)

    @pl.when((i >= nu_ref[0]) & (f == 0))
    def _():
        o_ref[...] = jnp.zeros_like(o_ref)


def moe_experts(hs, tile_expert, n_used, w1, w3, w2, *, tm, tf):
    n_rows, d = hs.shape
    dff = w1.shape[2]
    nf = dff // tf
    n_tiles = n_rows // tm

    def last_used(i, nu):
        return jnp.maximum(jnp.minimum(i, nu[0] - 1), 0)

    def row_map(i, f, te, nu):
        return (last_used(i, nu), 0)

    def hidden_block(i, f, nu):
        t = last_used(i, nu)
        step = jnp.where(i < nu[0], f, nf - 1)
        return jnp.where(t % 2 == 0, step, nf - 1 - step)

    def w_in_map(i, f, te, nu):
        return (te[last_used(i, nu)], 0, hidden_block(i, f, nu))

    def w_out_map(i, f, te, nu):
        return (te[last_used(i, nu)], hidden_block(i, f, nu), 0)

    grid_spec = pltpu.PrefetchScalarGridSpec(
        num_scalar_prefetch=2,
        grid=(n_tiles, nf),
        in_specs=[pl.BlockSpec((tm, d), row_map),
                  pl.BlockSpec((None, d, tf), w_in_map),
                  pl.BlockSpec((None, d, tf), w_in_map),
                  pl.BlockSpec((None, tf, d), w_out_map)],
        out_specs=pl.BlockSpec((tm, d), lambda i, f, te, nu: (i, 0)),
        scratch_shapes=[pltpu.VMEM((tm, d), F32)],
    )
    return pl.pallas_call(
        _moe_kernel,
        grid_spec=grid_spec,
        out_shape=jax.ShapeDtypeStruct((n_rows, d), BF16),
        compiler_params=_cparams("arbitrary", "arbitrary"),
        name="moe_experts",
    )(tile_expert, n_used, hs, w1, w3, w2)


def _combine_kernel(seg_ref, ys_ref, x_ref, g_ref, wgt_ref, ld_ref, o_ref, buf_ref, sem):
    tt = x_ref.shape[0]
    lc = buf_ref.shape[1]
    ne = N_EXPERTS
    tile = pl.program_id(0) * pl.num_programs(1) + pl.program_id(1)
    last = pl.num_programs(0) * pl.num_programs(1) - 1
    slot = tile % 2

    def copies(t, sl, wait):
        def make_copy(lo, go):
            return pltpu.make_async_copy(ys_ref.at[pl.ds(go, SEG_ROWS), :],
                                         buf_ref.at[sl, pl.ds(lo, SEG_ROWS), :], sem.at[sl])
        _segment_copies(seg_ref, t, ne, make_copy, wait=wait)

    @pl.when(tile == 0)
    def _():
        buf_ref[...] = jnp.zeros_like(buf_ref)
        copies(tile, slot, wait=False)

    @pl.when(tile < last)
    def _():
        copies(tile + 1, 1 - slot, wait=False)

    copies(tile, slot, wait=True)

    ld = ld_ref[...]
    cols = lax.broadcasted_iota(jnp.int32, (tt, lc), 1)
    pick = jnp.concatenate([jnp.where(cols == ld[:, k:k + 1], 1.0, 0.0) for k in range(TOP_K)],
                           axis=0).astype(BF16)
    z = jnp.dot(pick, buf_ref[slot], preferred_element_type=F32)
    w = wgt_ref[...]
    mix = w[:, 0:1] * z[0:tt] + w[:, 1:2] * z[tt:]
    o_ref[...] = x_ref[...] + g_ref[...] * mix


def moe_combine(ys, seg, ldest, wgt, x, g, *, tt, lc):
    bsz, s, d = x.shape
    nt = s // tt
    tok_spec = pl.BlockSpec((tt, TOP_K), lambda b, i, sref: (b * nt + i, 0))
    grid_spec = pltpu.PrefetchScalarGridSpec(
        num_scalar_prefetch=1,
        grid=(bsz, nt),
        in_specs=[pl.BlockSpec(memory_space=pl.ANY),
                  pl.BlockSpec((None, tt, d), lambda b, i, sref: (b, i, 0)),
                  pl.BlockSpec((None, 1, d), lambda b, i, sref: (b, 0, 0)),
                  tok_spec, tok_spec],
        out_specs=pl.BlockSpec((None, tt, d), lambda b, i, sref: (b, i, 0)),
        scratch_shapes=[pltpu.VMEM((2, lc, d), BF16), pltpu.SemaphoreType.DMA((2,))],
    )
    return pl.pallas_call(
        _combine_kernel,
        grid_spec=grid_spec,
        out_shape=jax.ShapeDtypeStruct((bsz, s, d), F32),
        compiler_params=_cparams("arbitrary", "arbitrary"),
        name="moe_combine",
    )(seg, ys, x, g, wgt, ldest)


def _round_up(v, m):
    return ((v + m - 1) // m) * m


def moe_token_tile(s):
    return min(512, s)


def moe_residual(x, nw, sh, sc, g, ldest, wgt, cnt, w1, w3, w2, *, tm=MOE_TILE_ROWS):
    bsz, s, d = x.shape
    n_tok = bsz * s
    ne = w1.shape[0]
    tt = moe_token_tile(s)
    n_tt = n_tok // tt
    lc = _round_up(TOP_K * tt + ne * SEG_ROWS, 128)
    seg_len = _round_up(cnt.reshape(n_tt, ne), SEG_ROWS)
    local_start = jnp.cumsum(seg_len, axis=1) - seg_len
    padded = _round_up(jnp.sum(seg_len, axis=0), tm)
    ends = jnp.cumsum(padded)
    global_start = (ends - padded)[None, :] + jnp.cumsum(seg_len, axis=0) - seg_len
    n_rows = _round_up(n_tok * TOP_K + n_tt * ne * SEG_ROWS + ne * tm, tm)
    used_end = (ends - padded) + jnp.sum(seg_len, axis=0)
    next_start = jnp.concatenate([ends[:-1], jnp.full((1,), n_rows, ends.dtype)])
    tails = jnp.stack([used_end, (next_start - used_end) // SEG_ROWS], axis=-1)
    seg = jnp.concatenate([jnp.stack([local_start, global_start, seg_len // SEG_ROWS], axis=-1).reshape(-1),
                           tails.reshape(-1)]).astype(jnp.int32)
    n_tiles = n_rows // tm
    tile_start = jnp.arange(n_tiles, dtype=jnp.int32) * tm
    tile_expert = jnp.minimum(jnp.sum(tile_start[:, None] >= ends[None, :], axis=1), ne - 1).astype(jnp.int32)
    n_used = (ends[ne - 1:ne] // tm).astype(jnp.int32)
    hs = moe_dispatch(x, nw, sh, sc, seg, ldest, n_rows, tt=tt, lc=lc)
    ys = moe_experts(hs, tile_expert, n_used, w1, w3, w2, tm=tm, tf=w1.shape[2] // 2)
    return moe_combine(ys, seg, ldest.T, wgt.T, x, g, tt=tt, lc=lc)


def _split_mod(mod):
    return [m[:, None, :] for m in jnp.split(mod, 6, axis=-1)]


def even_layer(x, c, rel_bias, ada_w, ada_b, norm1_w, in_w, q_norm_w, k_norm_w, sink, out_w,
               norm2_w, w1, w3, w2):
    s = x.shape[1]
    sh1, sc1, g1, sh2, sc2, g2 = _split_mod(ada_mod(c, ada_w, ada_b))
    proj = norm_mod_matmul(x, norm1_w, sh1, sc1, in_w.astype(BF16), ts=min(1024, s), tn=256, name="even_in_proj",
                           out_dtype=BF16)
    yf = fourier_mix(proj, tq=min(512, s))
    ya = window_attention(proj, band_bias(rel_bias), q_norm_w, k_norm_w, sink)
    x = cat_proj_residual(yf, ya, out_w.astype(BF16), x, g1, ts=min(1024, s))
    return ffn_residual(x, norm2_w, sh2, sc2, g2, w1.astype(BF16), w3.astype(BF16), w2,
                        ts=min(1024, s), tf=256)


def odd_layer(x, c, ada_w, ada_b, norm1_w, in_w, conv_w, conv_b, dt_bias_f, dt_bias_b, a_log_f, a_log_b,
              d_skip, gnorm_w, out_w, norm2_w, router_w, router_b, w1, w3, w2):
    s = x.shape[1]
    sh1, sc1, g1, sh2, sc2, g2 = _split_mod(ada_mod(c, ada_w, ada_b))
    d_inner = gnorm_w.shape[0]
    cdim = conv_w.shape[1]
    wide = d_inner + cdim
    in_w = in_w.astype(BF16)
    z, xs, bc, dt = ssm_in_proj_conv(x, norm1_w, sh1, sc1, in_w[:, :d_inner], in_w[:, d_inner:wide], in_w[:, wide:],
                                     conv_w, conv_b, n_x=d_inner, ts=min(512, s))
    y = ssd_scan_bidir(xs, bc, dt, dt_bias_f, dt_bias_b, a_log_f, a_log_b, d_skip)
    x, ldest, wgt, cnt = gated_proj_route(y, z, gnorm_w, out_w.astype(BF16), x, g1, norm2_w, sh2, sc2,
                                          router_w, router_b, ts=moe_token_tile(s))
    return moe_residual(x, norm2_w, sh2, sc2, g2, ldest, wgt, cnt,
                        w1.astype(BF16), w3, w2)


def kernel(x, c, rel_bias, ev_ada_w, ev_ada_b, ev_norm1_w, ev_in_w, ev_q_norm_w, ev_k_norm_w, ev_sink, ev_out_w, ev_norm2_w, ev_ffn_w1, ev_ffn_w3, ev_ffn_w2, od_ada_w, od_ada_b, od_norm1_w, od_in_w, od_conv_w, od_conv_b, od_dt_bias_f, od_dt_bias_b, od_A_log_f, od_A_log_b, od_D, od_gnorm_w, od_out_w, od_norm2_w, od_router_w, od_router_b, od_moe_w1, od_moe_w3, od_moe_w2):
    depth = ev_ada_w.shape[0] + od_ada_w.shape[0]
    for i in range(depth):
        j = i // 2
        if i % 2 == 0:
            x = even_layer(x, c, rel_bias, ev_ada_w[j], ev_ada_b[j], ev_norm1_w[j], ev_in_w[j],
                           ev_q_norm_w[j], ev_k_norm_w[j], ev_sink[j], ev_out_w[j], ev_norm2_w[j],
                           ev_ffn_w1[j], ev_ffn_w3[j], ev_ffn_w2[j])
        else:
            x = odd_layer(x, c, od_ada_w[j], od_ada_b[j], od_norm1_w[j], od_in_w[j], od_conv_w[j],
                          od_conv_b[j], od_dt_bias_f[j], od_dt_bias_b[j], od_A_log_f[j], od_A_log_b[j],
                          od_D[j], od_gnorm_w[j], od_out_w[j], od_norm2_w[j], od_router_w[j],
                          od_router_b[j], od_moe_w1[j], od_moe_w3[j], od_moe_w2[j])
    return x
```

```python
import functools

import numpy as np
import jax
import jax.numpy as jnp
from jax import lax
from jax.experimental import pallas as pl
from jax.experimental.pallas import tpu as pltpu

F32 = jnp.float32
BF16 = jnp.bfloat16

EPS = 1e-6
FNET_GROUPS = 4
FNET_GROUP_DIM = 128
FNET_WIDTH = FNET_GROUPS * FNET_GROUP_DIM
ATTN_HEADS = 8
ATTN_KV_HEADS = 2
HEAD_DIM = 64
ATTN_WIDTH = ATTN_HEADS * HEAD_DIM
KV_WIDTH = ATTN_KV_HEADS * HEAD_DIM
WINDOW = 128
BLOCK = 128
REL_BUCKETS = 32
REL_MAX_DIST = 128
SSM_HEAD_DIM = 64
SSM_GROUPS = 4
D_STATE = 128
CONV_WIDTH = 5
SSD_CHUNK = 128
N_EXPERTS = 8
TOP_K = 2
NEG_BIG = -1e30
ATTN_QBLOCKS = 2

V7X_VMEM_LIMIT_BYTES = 56 * 1024 * 1024
MOE_TILE_ROWS = 768
MOE_HIDDEN_CHUNK = 256


def _cparams(*sem):
    return pltpu.CompilerParams(dimension_semantics=sem, vmem_limit_bytes=V7X_VMEM_LIMIT_BYTES)


def _modnorm(x, nw, sh, sc):
    ms = jnp.mean(x * x, axis=-1, keepdims=True)
    return x * lax.rsqrt(ms + EPS) * nw * (1.0 + sc) + sh


def _silu(x):
    return x * (1.0 / (1.0 + jnp.exp(-x)))


def _ada_kernel(c_ref, w_ref, b_ref, o_ref):
    cs = _silu(c_ref[...]).astype(BF16)
    o_ref[...] = jnp.dot(cs, w_ref[...].astype(BF16), preferred_element_type=F32) + b_ref[...]


def ada_mod(c, w, b):
    bsz, d = c.shape
    n = w.shape[1]
    tn = 1536
    return pl.pallas_call(
        _ada_kernel,
        grid=(n // tn,),
        in_specs=[pl.BlockSpec((bsz, d), lambda j: (0, 0)),
                  pl.BlockSpec((d, tn), lambda j: (0, j)),
                  pl.BlockSpec((1, tn), lambda j: (0, j))],
        out_specs=pl.BlockSpec((bsz, tn), lambda j: (0, j)),
        out_shape=jax.ShapeDtypeStruct((bsz, n), F32),
        compiler_params=_cparams("arbitrary"),
        name="ada_mod",
    )(c, w, b.reshape(1, n))


def _nmm_kernel(x_ref, nw_ref, sh_ref, sc_ref, w_ref, o_ref, *, tn):
    h = _modnorm(x_ref[...], nw_ref[...], sh_ref[...], sc_ref[...]).astype(BF16)
    n = w_ref.shape[1]
    for lo in range(0, n, tn):
        o_ref[:, lo:lo + tn] = jnp.dot(h, w_ref[:, lo:lo + tn],
                                       preferred_element_type=F32).astype(o_ref.dtype)


def _resident(shape):
    return pl.BlockSpec(shape, lambda *_: tuple(0 for _ in shape), pipeline_mode=pl.Buffered(1))


def norm_mod_matmul(x, nw, sh, sc, w, *, ts, tn, name, out_dtype=F32):
    bsz, s, d = x.shape
    n = w.shape[1]
    assert n % tn == 0
    return pl.pallas_call(
        functools.partial(_nmm_kernel, tn=tn),
        grid=(bsz, s // ts),
        in_specs=[pl.BlockSpec((None, ts, d), lambda b, i: (b, i, 0)),
                  pl.BlockSpec((1, d), lambda b, i: (0, 0)),
                  pl.BlockSpec((None, 1, d), lambda b, i: (b, 0, 0)),
                  pl.BlockSpec((None, 1, d), lambda b, i: (b, 0, 0)),
                  _resident((d, n))],
        out_specs=pl.BlockSpec((None, ts, n), lambda b, i: (b, i, 0)),
        out_shape=jax.ShapeDtypeStruct((bsz, s, n), out_dtype),
        compiler_params=_cparams("parallel", "parallel"),
        name=name,
    )(x, nw.reshape(1, d), sh, sc, w)


def _dft_cos_sin(n):
    k = np.arange(n, dtype=np.int64)
    ang = ((k[:, None] * k[None, :]) % n).astype(np.float64) * (2.0 * np.pi / n)
    scale = 1.0 / np.sqrt(n)
    return np.cos(ang) * scale, np.sin(ang) * scale


def _fourier_kernel(u_ref, chan_ref, seq_ref, o_ref, ab_ref):
    s = u_ref.shape[0]

    @pl.when(pl.program_id(1) == 0)
    def _():
        for g in range(FNET_GROUPS):
            lo, hi = g * FNET_GROUP_DIM, (g + 1) * FNET_GROUP_DIM
            ug = u_ref[:, lo:hi].astype(BF16)
            cs = jnp.dot(ug, chan_ref[...], preferred_element_type=F32)
            ab_ref[0:s, lo:hi] = cs[:, :FNET_GROUP_DIM].astype(BF16)
            ab_ref[s:2 * s, lo:hi] = cs[:, FNET_GROUP_DIM:].astype(BF16)

    o_ref[...] = jnp.dot(seq_ref[...], ab_ref[...], preferred_element_type=F32).astype(o_ref.dtype)


def fourier_mix(proj, *, tq):
    bsz, s, _ = proj.shape
    cc, sc = _dft_cos_sin(FNET_GROUP_DIM)
    chan = jnp.asarray(np.concatenate([cc, sc], axis=1), BF16)
    cs, ss = _dft_cos_sin(s)
    seq = jnp.asarray(np.concatenate([cs, -ss], axis=1), BF16)
    return pl.pallas_call(
        _fourier_kernel,
        grid=(bsz, s // tq),
        in_specs=[pl.BlockSpec((None, s, FNET_WIDTH), lambda b, i: (b, 0, 0)),
                  pl.BlockSpec((FNET_GROUP_DIM, 2 * FNET_GROUP_DIM), lambda b, i: (0, 0)),
                  pl.BlockSpec((tq, 2 * s), lambda b, i: (i, 0))],
        out_specs=pl.BlockSpec((None, tq, FNET_WIDTH), lambda b, i: (b, i, 0)),
        out_shape=jax.ShapeDtypeStruct((bsz, s, FNET_WIDTH), BF16),
        scratch_shapes=[pltpu.VMEM((2 * s, FNET_WIDTH), BF16)],
        compiler_params=_cparams("parallel", "arbitrary"),
        name="fourier_mix",
    )(proj, chan, seq)


def _band_bucket_table():
    i = np.arange(BLOCK)[:, None]
    j = np.arange(3 * BLOCK)[None, :]
    rel = (j - BLOCK) - i
    half = REL_BUCKETS // 2
    max_exact = half // 2
    n = np.abs(rel)
    large = max_exact + (np.log(np.maximum(n, 1) / max_exact)
                         / np.log(REL_MAX_DIST / max_exact) * (half - max_exact)).astype(np.int32)
    large = np.minimum(large, half - 1)
    bucket = (rel > 0).astype(np.int32) * half + np.where(n < max_exact, n, large)
    return np.where(n <= WINDOW, bucket, -1).astype(np.int32)


def _bias_kernel(rb_ref, bucket_ref, o_ref):
    h = pl.program_id(0)
    bucket = bucket_ref[...]
    acc = jnp.full(bucket.shape, NEG_BIG, F32)
    for bkt in range(REL_BUCKETS):
        acc = jnp.where(bucket == bkt, rb_ref[bkt * ATTN_HEADS + h], acc)
    o_ref[...] = acc


def band_bias(rel_bias):
    bucket = jnp.asarray(_band_bucket_table())
    return pl.pallas_call(
        _bias_kernel,
        grid=(ATTN_HEADS,),
        in_specs=[pl.BlockSpec(memory_space=pltpu.SMEM),
                  pl.BlockSpec((BLOCK, 3 * BLOCK), lambda h: (0, 0))],
        out_specs=pl.BlockSpec((None, BLOCK, 3 * BLOCK), lambda h: (h, 0, 0)),
        out_shape=jax.ShapeDtypeStruct((ATTN_HEADS, BLOCK, 3 * BLOCK), F32),
        compiler_params=_cparams("arbitrary"),
        name="band_bias",
    )(rel_bias.reshape(-1), bucket)


def _head_mean_matrix(width):
    m = np.zeros((width, width), np.float32)
    for h in range(width // HEAD_DIM):
        m[h * HEAD_DIM:(h + 1) * HEAD_DIM, h * HEAD_DIM:(h + 1) * HEAD_DIM] = 1.0 / HEAD_DIM
    return m


def _heads_rms(t, mean_mat, w):
    sq = t * t
    hi = sq.astype(BF16)
    lo = (sq - hi.astype(F32)).astype(BF16)
    ms = (jnp.dot(hi, mean_mat, preferred_element_type=F32)
          + jnp.dot(lo, mean_mat, preferred_element_type=F32))
    return t * lax.rsqrt(ms + EPS) * w


def _attn_kernel(sink_ref, q_ref, kl_ref, kc_ref, kr_ref, vl_ref, vc_ref, vr_ref,
                 bias_ref, qnw_ref, knw_ref, qmean_ref, kmean_ref, o_ref):
    n = pl.program_id(1)
    nb = pl.num_programs(1) * ATTN_QBLOCKS
    k = jnp.concatenate([kl_ref[...], kc_ref[...], kr_ref[...]], axis=0).astype(F32)
    v = jnp.concatenate([vl_ref[...], vc_ref[...], vr_ref[...]], axis=0).astype(F32)
    col = lax.broadcasted_iota(jnp.int32, (1, 3 * BLOCK), 1)
    qn = _heads_rms(q_ref[...].astype(F32), qmean_ref[...], qnw_ref[...])
    kn = _heads_rms(k, kmean_ref[...], knw_ref[...])
    low = lax.broadcasted_iota(jnp.int32, (1, 2 * HEAD_DIM), 1) < HEAD_DIM
    kn_sw = pltpu.roll(kn, HEAD_DIM, axis=1)
    v_sw = pltpu.roll(v, HEAD_DIM, axis=1)
    k_dup = [jnp.where(low, kn, kn_sw).astype(BF16), jnp.where(low, kn_sw, kn).astype(BF16)]
    ones = jnp.ones((k.shape[0], 2 * HEAD_DIM), BF16)
    v_ext = [jnp.concatenate([v.astype(BF16), ones], axis=1),
             jnp.concatenate([v_sw.astype(BF16), ones], axis=1)]
    g = ATTN_HEADS // ATTN_KV_HEADS
    for qb in range(ATTN_QBLOCKS):
        blk = n * ATTN_QBLOCKS + qb
        band = slice(qb * BLOCK, (qb + 3) * BLOCK)
        first_key = jnp.where(blk == 0, BLOCK, 0)
        end_key = jnp.where(blk == nb - 1, 2 * BLOCK, 3 * BLOCK)
        outside = (col < first_key) | (col >= end_key)
        raw = []
        for h in range(ATTN_HEADS):
            m, idx, j = h // 2, h % 2, h // g
            qp = qn[qb * BLOCK:(qb + 1) * BLOCK, m * 2 * HEAD_DIM:(m + 1) * 2 * HEAD_DIM]
            qm = jnp.where(low if idx == 0 else jnp.logical_not(low), qp, 0.0).astype(BF16)
            raw.append(lax.dot_general(qm, k_dup[j][band], (((1,), (1,)), ((), ())),
                                       preferred_element_type=F32))
        probs, tails = [], []
        for h in range(ATTN_HEADS):
            logits = jnp.where(outside, NEG_BIG, raw[h] + bias_ref[h])
            sk = sink_ref[h]
            mx = jnp.maximum(jnp.max(logits, axis=-1, keepdims=True), sk)
            probs.append(jnp.exp(logits - mx).astype(BF16))
            tails.append(jnp.exp(sk - mx))
        res = []
        for h in range(ATTN_HEADS):
            idx, j = h % 2, h // g
            r = jnp.dot(probs[h], v_ext[idx if j == 0 else 1 - idx][band], preferred_element_type=F32)
            res.append(r[:, :2 * HEAD_DIM] / (r[:, 2 * HEAD_DIM:] + tails[h]))
        pairs = [jnp.where(low, res[2 * m], res[2 * m + 1]) for m in range(ATTN_HEADS // 2)]
        o_ref[qb * BLOCK:(qb + 1) * BLOCK, :] = jnp.concatenate(pairs, axis=-1).astype(o_ref.dtype)


def window_attention(proj, bias, q_norm_w, k_norm_w, sink):
    bsz, s, _ = proj.shape
    nb = s // BLOCK
    qcol = FNET_WIDTH // ATTN_WIDTH
    kcol = (FNET_WIDTH + ATTN_WIDTH) // KV_WIDTH
    vcol = kcol + 1

    qb = ATTN_QBLOCKS
    assert nb % qb == 0

    def kv_specs(col):
        return [pl.BlockSpec((None, BLOCK, KV_WIDTH), lambda b, n: (b, jnp.maximum(n * qb - 1, 0), col)),
                pl.BlockSpec((None, qb * BLOCK, KV_WIDTH), lambda b, n: (b, n, col)),
                pl.BlockSpec((None, BLOCK, KV_WIDTH), lambda b, n: (b, jnp.minimum((n + 1) * qb, nb - 1), col))]

    return pl.pallas_call(
        _attn_kernel,
        grid=(bsz, nb // qb),
        in_specs=[pl.BlockSpec(memory_space=pltpu.SMEM),
                  pl.BlockSpec((None, qb * BLOCK, ATTN_WIDTH), lambda b, n: (b, n, qcol)),
                  *kv_specs(kcol), *kv_specs(vcol),
                  pl.BlockSpec((ATTN_HEADS, BLOCK, 3 * BLOCK), lambda b, n: (0, 0, 0)),
                  pl.BlockSpec((1, ATTN_WIDTH), lambda b, n: (0, 0)),
                  pl.BlockSpec((1, KV_WIDTH), lambda b, n: (0, 0)),
                  pl.BlockSpec((ATTN_WIDTH, ATTN_WIDTH), lambda b, n: (0, 0)),
                  pl.BlockSpec((KV_WIDTH, KV_WIDTH), lambda b, n: (0, 0))],
        out_specs=pl.BlockSpec((None, qb * BLOCK, ATTN_WIDTH), lambda b, n: (b, n, 0)),
        out_shape=jax.ShapeDtypeStruct((bsz, s, ATTN_WIDTH), BF16),
        compiler_params=_cparams("parallel", "arbitrary"),
        name="window_attention",
    )(sink, proj, proj, proj, proj, proj, proj, proj, bias,
      (jnp.tile(q_norm_w, ATTN_HEADS) * (HEAD_DIM ** -0.5)).reshape(1, ATTN_WIDTH),
      jnp.tile(k_norm_w, ATTN_KV_HEADS).reshape(1, KV_WIDTH),
      jnp.asarray(_head_mean_matrix(ATTN_WIDTH), BF16), jnp.asarray(_head_mean_matrix(KV_WIDTH), BF16))


def _cat_proj_kernel(a1_ref, a2_ref, w_ref, x_ref, g_ref, o_ref):
    k1 = a1_ref.shape[1]
    y = jnp.dot(a1_ref[...].astype(BF16), w_ref[0:k1, :], preferred_element_type=F32)
    y = y + jnp.dot(a2_ref[...].astype(BF16), w_ref[k1:, :], preferred_element_type=F32)
    o_ref[...] = x_ref[...] + g_ref[...] * y


def cat_proj_residual(a1, a2, w, x, g, *, ts):
    bsz, s, d = x.shape
    k1, k2 = a1.shape[2], a2.shape[2]
    return pl.pallas_call(
        _cat_proj_kernel,
        grid=(bsz, s // ts),
        in_specs=[pl.BlockSpec((None, ts, k1), lambda b, i: (b, i, 0)),
                  pl.BlockSpec((None, ts, k2), lambda b, i: (b, i, 0)),
                  pl.BlockSpec((k1 + k2, d), lambda b, i: (0, 0)),
                  pl.BlockSpec((None, ts, d), lambda b, i: (b, i, 0)),
                  pl.BlockSpec((None, 1, d), lambda b, i: (b, 0, 0))],
        out_specs=pl.BlockSpec((None, ts, d), lambda b, i: (b, i, 0)),
        out_shape=jax.ShapeDtypeStruct((bsz, s, d), F32),
        compiler_params=_cparams("parallel", "parallel"),
        name="mixer_out_proj",
    )(a1, a2, w, x, g)


def _ffn_kernel(x_ref, nw_ref, sh_ref, sc_ref, g_ref, w1_ref, w3_ref, w2_ref, o_ref, acc_ref, *, tf):
    h = _modnorm(x_ref[...], nw_ref[...], sh_ref[...], sc_ref[...]).astype(BF16)
    dff = w1_ref.shape[1]
    for lo in range(0, dff, tf):
        a = jnp.dot(h, w1_ref[:, lo:lo + tf], preferred_element_type=F32)
        b = jnp.dot(h, w3_ref[:, lo:lo + tf], preferred_element_type=F32)
        t = (_silu(a) * b).astype(BF16)
        contrib = jnp.dot(t, w2_ref[lo:lo + tf, :].astype(BF16), preferred_element_type=F32)
        if lo == 0:
            acc_ref[...] = contrib
        else:
            acc_ref[...] += contrib
    o_ref[...] = x_ref[...] + g_ref[...] * acc_ref[...]


def ffn_residual(x, nw, sh, sc, g, w1, w3, w2, *, ts, tf):
    bsz, s, d = x.shape
    dff = w1.shape[1]
    assert dff % tf == 0
    vec = pl.BlockSpec((None, 1, d), lambda b, i: (b, 0, 0))
    return pl.pallas_call(
        functools.partial(_ffn_kernel, tf=tf),
        grid=(bsz, s // ts),
        in_specs=[pl.BlockSpec((None, ts, d), lambda b, i: (b, i, 0)),
                  pl.BlockSpec((1, d), lambda b, i: (0, 0)),
                  vec, vec, vec,
                  _resident((d, dff)), _resident((d, dff)), _resident((dff, d))],
        out_specs=pl.BlockSpec((None, ts, d), lambda b, i: (b, i, 0)),
        out_shape=jax.ShapeDtypeStruct((bsz, s, d), F32),
        scratch_shapes=[pltpu.VMEM((ts, d), F32)],
        compiler_params=_cparams("parallel", "parallel"),
        name="ffn_swiglu",
    )(x, nw.reshape(1, d), sh, sc, g, w1, w3, w2)


HALO_ROWS = 16
PROJ_CHUNK = 256


def _proj_conv_kernel(x_ref, xp_ref, xn_ref, nw_ref, sh_ref, sc_ref, wz_ref, wc_ref, wd_ref, cw_ref, cb_ref,
                      z_ref, xs_ref, bc_ref, dt_ref):
    i = pl.program_id(1)
    ts = x_ref.shape[0]
    half = CONV_WIDTH // 2

    def hnorm(ref):
        return _modnorm(ref[...], nw_ref[...], sh_ref[...], sc_ref[...])

    h = hnorm(x_ref).astype(BF16)
    h_prev = jnp.where(i == 0, 0.0, hnorm(xp_ref)).astype(BF16)
    h_next = jnp.where(i == pl.num_programs(1) - 1, 0.0, hnorm(xn_ref)).astype(BF16)
    h_ext = jnp.concatenate([h_prev, h, h_next], axis=0)

    n_x = xs_ref.shape[1]
    rows = h_ext.shape[0]

    def project(lo):
        return jnp.dot(h_ext, wc_ref[:, lo:lo + PROJ_CHUNK], preferred_element_type=F32)

    def z_chunk(lo):
        hi = lo + PROJ_CHUNK
        z_ref[:, lo:hi] = jnp.dot(h, wz_ref[:, lo:hi], preferred_element_type=F32).astype(z_ref.dtype)

    z_los = list(range(0, wz_ref.shape[1], PROJ_CHUNK))
    c_los = list(range(0, wc_ref.shape[1], PROJ_CHUNK))
    p_next = project(c_los[0])
    for n, lo in enumerate(c_los):
        hi = lo + PROJ_CHUNK
        p = p_next
        if n + 1 < len(c_los):
            p_next = project(c_los[n + 1])
        if z_los:
            z_chunk(z_los.pop(0))
        acc = jnp.zeros((ts, PROJ_CHUNK), F32) + cb_ref[:, lo:hi]
        for kk in range(CONV_WIDTH):
            shifted = p if kk == half else pltpu.roll(p, (half - kk) % rows, axis=0)
            acc = acc + shifted[HALO_ROWS:HALO_ROWS + ts, :] * cw_ref[kk:kk + 1, lo:hi]
        out = _silu(acc)
        if lo < n_x:
            xs_ref[:, lo:hi] = out
        else:
            bc_ref[:, lo - n_x:hi - n_x] = out.astype(bc_ref.dtype)
    for lo in z_los:
        z_chunk(lo)
    dt_ref[...] = jnp.dot(h, wd_ref[...], preferred_element_type=F32)


def ssm_in_proj_conv(x, nw, sh, sc, w_z, w_xbc, w_dt, conv_w, conv_b, *, n_x, ts):
    bsz, s, d = x.shape
    dz = w_z.shape[1]
    dc = w_xbc.shape[1]
    ddt = w_dt.shape[1]
    assert ts % HALO_ROWS == 0 and n_x % PROJ_CHUNK == 0 and dc % PROJ_CHUNK == 0 and dz % PROJ_CHUNK == 0
    r = ts // HALO_ROWS
    last = s // HALO_ROWS - 1
    vec = pl.BlockSpec((None, 1, d), lambda b, i: (b, 0, 0))
    return pl.pallas_call(
        _proj_conv_kernel,
        grid=(bsz, s // ts),
        in_specs=[pl.BlockSpec((None, ts, d), lambda b, i: (b, i, 0)),
                  pl.BlockSpec((None, HALO_ROWS, d), lambda b, i: (b, jnp.maximum(i * r - 1, 0), 0)),
                  pl.BlockSpec((None, HALO_ROWS, d), lambda b, i: (b, jnp.minimum((i + 1) * r, last), 0)),
                  pl.BlockSpec((1, d), lambda b, i: (0, 0)), vec, vec,
                  _resident((d, dz)), _resident((d, dc)), _resident((d, ddt)),
                  _resident((CONV_WIDTH, dc)), _resident((1, dc))],
        out_specs=[pl.BlockSpec((None, ts, dz), lambda b, i: (b, i, 0)),
                   pl.BlockSpec((None, ts, n_x), lambda b, i: (b, i, 0)),
                   pl.BlockSpec((None, ts, dc - n_x), lambda b, i: (b, i, 0)),
                   pl.BlockSpec((None, ts, ddt), lambda b, i: (b, i, 0))],
        out_shape=[jax.ShapeDtypeStruct((bsz, s, dz), BF16),
                   jax.ShapeDtypeStruct((bsz, s, n_x), F32),
                   jax.ShapeDtypeStruct((bsz, s, dc - n_x), BF16),
                   jax.ShapeDtypeStruct((bsz, s, ddt), F32)],
        compiler_params=_cparams("parallel", "parallel"),
        name="odd_in_proj_conv",
    )(x, x, x, nw.reshape(1, d), sh, sc, w_z, w_xbc, w_dt, conv_w, conv_b.reshape(1, dc))


def _softplus(x):
    return jnp.maximum(x, 0.0) + jnp.log(1.0 + jnp.exp(-jnp.abs(x)))


LOG2E = 1.4426950408889634
DECAY_SLOTS = 12


def _bf16_parts3(v):
    hi = v.astype(BF16).astype(F32)
    r = v - hi
    mid = r.astype(BF16).astype(F32)
    lo = (r - mid).astype(BF16).astype(F32)
    return hi, mid, lo


def _ssd_t_kernel(x_ref, b_ref, c_ref, dt_f_ref, dt_b_ref, p_f_ref, p_b_ref, dx_ref, trio_ref, y_ref,
                  ar_ref, dr_ref, sc_ref, pq_ref, qt_ref, xtb_ref, xdf_ref, xdb_ref, yt_ref, hf_ref, hb_ref):
    s = x_ref.shape[0]
    q = SSD_CHUNK
    nh = p_f_ref.shape[0]
    nc = s // q
    hd = SSM_HEAD_DIM

    def row_params(raw_ref, p_ref):
        dt = _softplus(raw_ref[...] + p_ref[:, 0:1])
        return dt, (-LOG2E) * jnp.exp(p_ref[:, 1:2]) * dt

    dtf, af = row_params(dt_f_ref, p_f_ref)
    dtb, ab = row_params(dt_b_ref, p_b_ref)
    dr_ref[0:nh, :] = dtf
    dr_ref[nh:, :] = dtb
    ar_ref[0:nh, :] = af
    ar_ref[nh:, :] = ab

    li = lax.broadcasted_iota(jnp.int32, (q, q), 0)
    si = lax.broadcasted_iota(jnp.int32, (q, q), 1)
    lower = li >= si
    upper = li <= si
    slot_head = lax.broadcasted_iota(jnp.int32, (1, q), 1) % nh

    def bdot(a, b):
        return jnp.dot(a, b, preferred_element_type=F32)

    def ntdot(a, b):
        return lax.dot_general(a, b, (((1,), (1,)), ((), ())), preferred_element_type=F32)

    def head_rows(v):
        return jnp.concatenate([jnp.broadcast_to(v[h:h + 1, :], (hd, q)) for h in range(nh)], axis=0)

    ones = jnp.ones((nh, q), F32)
    zeros = jnp.zeros((nh, q), F32)

    def prep_body(c, carry):
        sl = pl.ds(pl.multiple_of(c * q, q), q)
        a_row = ar_ref[:, sl]
        d_row = dr_ref[:, sl]
        parts = jnp.concatenate([p.astype(BF16) for p in _bf16_parts3(a_row)], axis=0)
        cs = bdot(parts, trio_ref[...])
        cs = cs[0:2 * nh] + cs[2 * nh:4 * nh] + cs[4 * nh:6 * nh]
        i_f = cs[0:nh, 0:q]
        e_b = cs[nh:, 0:q] - a_row[nh:]
        tot_f = cs[0:nh, q:]
        tot_b = cs[nh:, q:]
        ih, im, il = _bf16_parts3(i_f)
        eh, em, el = _bf16_parts3(e_b)
        pad = [zeros] * (q // nh - DECAY_SLOTS)
        p_t = jnp.concatenate([ih, im, il, ones, ones, ones, -eh, -em, -el, ones, ones, ones] + pad, axis=0)
        qf_t = jnp.concatenate([ones, ones, ones, -ih, -im, -il] + [zeros] * 6 + pad, axis=0)
        qb_t = jnp.concatenate([zeros] * 6 + [ones, ones, ones, eh, em, el] + pad, axis=0)
        pq_ref[sl, :] = p_t.T.astype(BF16)
        qt_ref[c] = jnp.concatenate([qf_t, qb_t], axis=1).astype(BF16)
        sc_ref[0 * nh:1 * nh, sl] = jnp.exp2(i_f)
        sc_ref[1 * nh:2 * nh, sl] = jnp.exp2(tot_b - e_b)
        sc_ref[2 * nh:3 * nh, sl] = jnp.exp2(tot_f)
        sc_ref[3 * nh:4 * nh, sl] = jnp.exp2(tot_b)
        xt = x_ref[sl, :].T
        xtb_ref[:, sl] = xt.astype(BF16)
        xdf_ref[:, sl] = (xt * head_rows(jnp.exp2(tot_f - i_f) * d_row[0:nh])).astype(BF16)
        xdb_ref[:, sl] = (xt * head_rows(jnp.exp2(e_b) * d_row[nh:])).astype(BF16)
        return carry

    lax.fori_loop(0, nc, prep_body, 0, unroll=2)

    hf_ref[...] = jnp.zeros_like(hf_ref)
    hb_ref[...] = jnp.zeros_like(hb_ref)
    zero_half = jnp.zeros((hd, q), BF16)

    def fwd_body(c, carry):
        sl = pl.ds(pl.multiple_of(c * q, q), q)
        bc = b_ref[sl, :].astype(BF16)
        cc = c_ref[sl, :].astype(BF16)
        d_row = dr_ref[:, sl]
        p_all = pq_ref[sl, :]
        q_t = qt_ref[c]
        xtb = xtb_ref[:, sl]
        cb = ntdot(cc, bc)
        g2s = [bdot(jnp.where(slot_head == h, p_all, jnp.zeros_like(p_all)), q_t) for h in range(nh)]
        ms = []
        for h in range(nh):
            arg = jnp.where(lower, g2s[h][:, 0:q], g2s[h][:, q:])
            wgt = (jnp.where(lower, d_row[h:h + 1, :], 0.0)
                   + jnp.where(upper, d_row[nh + h:nh + h + 1, :], 0.0))
            ms.append((cb * jnp.exp2(arg) * wgt).astype(BF16))
        yd = []
        for h0 in range(0, nh, 2):
            lhs = jnp.concatenate(
                [jnp.concatenate([xtb[h0 * hd:(h0 + 1) * hd], zero_half], axis=0),
                 jnp.concatenate([zero_half, xtb[(h0 + 1) * hd:(h0 + 2) * hd]], axis=0)], axis=1)
            yd.append(ntdot(lhs, jnp.concatenate(ms[h0:h0 + 2], axis=1)))
        states = bdot(xdf_ref[:, sl], bc)
        h_prev = hf_ref[...]
        y_off = ntdot(h_prev.astype(BF16), cc) * head_rows(sc_ref[0 * nh:1 * nh, sl])
        hf_ref[...] = h_prev * head_rows(sc_ref[2 * nh:3 * nh, sl]) + states
        yt_ref[:, sl] = jnp.concatenate(yd, axis=0) + y_off
        return carry

    lax.fori_loop(0, nc, fwd_body, 0, unroll=4)

    def bwd_body(t, carry):
        c = nc - 1 - t
        sl = pl.ds(pl.multiple_of(c * q, q), q)
        bc = b_ref[sl, :].astype(BF16)
        cc = c_ref[sl, :].astype(BF16)
        states = bdot(xdb_ref[:, sl], bc)
        h_prev = hb_ref[...]
        y_off = ntdot(h_prev.astype(BF16), cc) * head_rows(sc_ref[1 * nh:2 * nh, sl])
        hb_ref[...] = h_prev * head_rows(sc_ref[3 * nh:4 * nh, sl]) + states
        y_ref[sl, :] = ((yt_ref[:, sl] + y_off).T + dx_ref[...] * x_ref[sl, :]).astype(y_ref.dtype)
        return carry

    lax.fori_loop(0, nc, bwd_body, 0, unroll=4)


def ssd_scan_bidir(xs, bc, dt, dt_bias_f, dt_bias_b, a_log_f, a_log_b, d_skip):
    bsz, s, _ = xs.shape
    nheads = dt.shape[2] // 2
    nh = nheads // SSM_GROUPS
    gw = nh * SSM_HEAD_DIM
    d_inner = nheads * SSM_HEAD_DIM
    q = SSD_CHUNK
    dt_row = jnp.transpose(dt.reshape(bsz, s, 2 * SSM_GROUPS, nh), (0, 2, 3, 1))
    prm = jnp.stack([jnp.concatenate([dt_bias_f, dt_bias_b]), jnp.concatenate([a_log_f, a_log_b])])
    p_row = jnp.transpose(prm.reshape(2, 2 * SSM_GROUPS, nh), (1, 2, 0))
    dx = jnp.repeat(d_skip, SSM_HEAD_DIM).reshape(SSM_GROUPS, 1, gw)
    assert DECAY_SLOTS * nh <= q and q % nh == 0 and D_STATE == q
    trio = jnp.asarray(np.concatenate([np.triu(np.ones((q, q), np.float32)), np.ones((q, q), np.float32)],
                                      axis=1), BF16)
    G = SSM_GROUPS
    nc = s // q

    return pl.pallas_call(
        _ssd_t_kernel,
        grid=(bsz, SSM_GROUPS),
        in_specs=[pl.BlockSpec((None, s, gw), lambda b, g: (b, 0, g)),
                  pl.BlockSpec((None, s, D_STATE), lambda b, g: (b, 0, g)),
                  pl.BlockSpec((None, s, D_STATE), lambda b, g: (b, 0, G + g)),
                  pl.BlockSpec((None, None, nh, s), lambda b, g: (b, g, 0, 0)),
                  pl.BlockSpec((None, None, nh, s), lambda b, g: (b, G + g, 0, 0)),
                  pl.BlockSpec((None, nh, 2), lambda b, g: (g, 0, 0)),
                  pl.BlockSpec((None, nh, 2), lambda b, g: (G + g, 0, 0)),
                  pl.BlockSpec((None, 1, gw), lambda b, g: (g, 0, 0)),
                  pl.BlockSpec((q, 2 * q), lambda b, g: (0, 0))],
        out_specs=pl.BlockSpec((None, s, gw), lambda b, g: (b, 0, g)),
        out_shape=jax.ShapeDtypeStruct((bsz, s, d_inner), BF16),
        scratch_shapes=[pltpu.VMEM((2 * nh, s), F32), pltpu.VMEM((2 * nh, s), F32),
                        pltpu.VMEM((4 * nh, s), F32),
                        pltpu.VMEM((s, q), BF16), pltpu.VMEM((nc, q, 2 * q), BF16),
                        pltpu.VMEM((gw, s), BF16), pltpu.VMEM((gw, s), BF16), pltpu.VMEM((gw, s), BF16),
                        pltpu.VMEM((gw, s), F32),
                        pltpu.VMEM((gw, D_STATE), F32), pltpu.VMEM((gw, D_STATE), F32)],
        compiler_params=_cparams("parallel", "parallel"),
        name="ssd_scan",
    )(xs, bc, bc, dt_row, dt_row, p_row, p_row, dx, trio)


def _gated_proj_route_kernel(y_ref, z_ref, gw_ref, w_ref, x_ref, g_ref,
                             nw_ref, sh_ref, sc_ref, rw_ref, rb_ref, lt_ref,
                             o_ref, ld_ref, wgt_ref, cnt_ref, acc_ref):
    _gated_proj_kernel(y_ref, z_ref, gw_ref, w_ref, x_ref, g_ref, o_ref, acc_ref)
    h = _modnorm(o_ref[...], nw_ref[...], sh_ref[...], sc_ref[...])
    rows, gates, total = _route_tile(h, rw_ref[...], rb_ref[...], lt_ref[...])
    ld_ref[...] = rows
    wgt_ref[...] = gates
    cnt_ref[...] = total[:, 0:cnt_ref.shape[1]]


def _gated_proj_kernel(y_ref, z_ref, gw_ref, w_ref, x_ref, g_ref, o_ref, acc_ref):
    k = y_ref.shape[1]
    ss = jnp.zeros((y_ref.shape[0], 1), F32)
    for lo in range(0, k, PROJ_CHUNK):
        hi = lo + PROJ_CHUNK
        t = y_ref[:, lo:hi].astype(F32) * _silu(z_ref[:, lo:hi].astype(F32))
        ss = ss + jnp.sum(t * t, axis=-1, keepdims=True)
        contrib = jnp.dot((t * gw_ref[:, lo:hi]).astype(BF16), w_ref[lo:hi, :], preferred_element_type=F32)
        if lo == 0:
            acc_ref[...] = contrib
        else:
            acc_ref[...] += contrib
    o_ref[...] = x_ref[...] + g_ref[...] * (acc_ref[...] * lax.rsqrt(ss * (1.0 / k) + EPS))


def gated_proj_route(y, z, gw, w, x, g, nw, sh, sc, router_w, router_b, *, ts):
    bsz, s, d = x.shape
    k = y.shape[2]
    ne = router_w.shape[1]
    nt = s // ts
    cum = jnp.asarray(np.concatenate([np.triu(np.ones((ts, ts), np.float32)), np.ones((ts, ts), np.float32)],
                                     axis=1), BF16)
    vec = pl.BlockSpec((None, 1, d), lambda b, i: (b, 0, 0))
    tok_spec = pl.BlockSpec((TOP_K, ts), lambda b, i: (0, b * nt + i))
    call = pl.pallas_call(
        _gated_proj_route_kernel,
        grid=(bsz, nt),
        in_specs=[pl.BlockSpec((None, ts, k), lambda b, i: (b, i, 0)),
                  pl.BlockSpec((None, ts, k), lambda b, i: (b, i, 0)),
                  pl.BlockSpec((1, k), lambda b, i: (0, 0)),
                  _resident((k, d)),
                  pl.BlockSpec((None, ts, d), lambda b, i: (b, i, 0)),
                  vec,
                  pl.BlockSpec((1, d), lambda b, i: (0, 0)), vec, vec,
                  _resident((ne, d)), pl.BlockSpec((ne, 1), lambda b, i: (0, 0)), _resident((ts, 2 * ts))],
        out_specs=[pl.BlockSpec((None, ts, d), lambda b, i: (b, i, 0)), tok_spec, tok_spec,
                   pl.BlockSpec((None, ne, 128), lambda b, i: (b * nt + i, 0, 0))],
        out_shape=[jax.ShapeDtypeStruct((bsz, s, d), F32),
                   jax.ShapeDtypeStruct((TOP_K, bsz * s), jnp.int32),
                   jax.ShapeDtypeStruct((TOP_K, bsz * s), F32),
                   jax.ShapeDtypeStruct((bsz * nt, ne, 128), jnp.int32)],
        scratch_shapes=[pltpu.VMEM((ts, d), F32)],
        compiler_params=_cparams("parallel", "parallel"),
        name="ssd_out_proj_route",
    )
    x_new, ldest, wgt, cnt = call(y, z, gw.reshape(1, k), w, x, g, nw.reshape(1, d), sh, sc,
                                  router_w.T, router_b.reshape(ne, 1), cum)
    return x_new, ldest, wgt, cnt[:, :, 0]


SEG_ROWS = 16
SEG_FIELDS = 3


def _route_tile(h, rw_t, rb_col, cum):
    def nt(a, b):
        return lax.dot_general(a, b, (((1,), (1,)), ((), ())), preferred_element_type=F32)

    h_hi = h.astype(BF16)
    h_lo = (h - h_hi.astype(F32)).astype(BF16)
    rw_hi = rw_t.astype(BF16)
    rw_lo = (rw_t - rw_hi.astype(F32)).astype(BF16)
    logits = nt(rw_hi, h_hi) + (nt(rw_hi, h_lo) + nt(rw_lo, h_hi)) + rb_col
    ne, ts = logits.shape
    eid = lax.broadcasted_iota(jnp.int32, (ne, ts), 0)
    m1 = jnp.max(logits, axis=0, keepdims=True)
    i1 = jnp.min(jnp.where(logits == m1, eid, ne), axis=0, keepdims=True)
    rest = jnp.where(eid == i1, -jnp.inf, logits)
    m2 = jnp.max(rest, axis=0, keepdims=True)
    i2 = jnp.min(jnp.where(rest == m2, eid, ne), axis=0, keepdims=True)
    e2 = jnp.exp(m2 - m1)
    w1 = 1.0 / (1.0 + e2)
    w2 = e2 / (1.0 + e2)
    oh1 = (eid == i1).astype(F32)
    oh2 = (eid == i2).astype(F32)
    chosen = oh1 + oh2
    both = jnp.dot(chosen.astype(BF16), cum, preferred_element_type=F32)
    before = both[:, 0:ts] - chosen
    total = both[:, ts:].astype(jnp.int32)
    seg_len = jnp.bitwise_and(total + (SEG_ROWS - 1), -SEG_ROWS).astype(F32)
    rows = [jnp.sum(jnp.where(eid < idx, seg_len, 0.0) + onehot * before, axis=0, keepdims=True)
            for onehot, idx in ((oh1, i1), (oh2, i2))]
    return jnp.concatenate(rows, axis=0).astype(jnp.int32), jnp.concatenate([w1, w2], axis=0), total


def _segment_copies(seg_ref, tile, n_experts, make_copy, *, wait):
    for e in range(n_experts):
        base = (tile * n_experts + e) * SEG_FIELDS
        local0 = seg_ref[base]
        global0 = seg_ref[base + 1]

        def body(i, carry, local0=local0, global0=global0):
            cp = make_copy(pl.multiple_of(local0 + i * SEG_ROWS, SEG_ROWS),
                           pl.multiple_of(global0 + i * SEG_ROWS, SEG_ROWS))
            if wait:
                cp.wait()
            else:
                cp.start()
            return carry

        lax.fori_loop(0, seg_ref[base + 2], body, 0)


def _dispatch_kernel(seg_ref, x_ref, nw_ref, sh_ref, sc_ref, ld_ref, hs_ref, buf_ref, zero_ref, sem,
                     *, n_token_tiles):
    tt = x_ref.shape[0]
    lc = buf_ref.shape[1]
    ne = N_EXPERTS
    tile = pl.program_id(0) * pl.num_programs(1) + pl.program_id(1)
    last = pl.num_programs(0) * pl.num_programs(1) - 1
    slot = tile % 2

    def copies(t, sl, wait):
        def make_copy(lo, go):
            return pltpu.make_async_copy(buf_ref.at[sl, pl.ds(lo, SEG_ROWS), :],
                                         hs_ref.at[pl.ds(go, SEG_ROWS), :], sem.at[sl])
        _segment_copies(seg_ref, t, ne, make_copy, wait=wait)

    h = _modnorm(x_ref[...], nw_ref[...], sh_ref[...], sc_ref[...]).astype(BF16)
    ld = ld_ref[...]
    rows = lax.broadcasted_iota(jnp.int32, (lc, tt), 0)
    perm = jnp.where(rows == ld[0:1, :], 1.0, jnp.where(rows == ld[1:2, :], 1.0, 0.0)).astype(BF16)
    buf_ref[slot] = jnp.dot(perm, h, preferred_element_type=F32).astype(BF16)
    copies(tile, slot, wait=False)

    @pl.when(tile > 0)
    def _():
        copies(tile - 1, 1 - slot, wait=True)

    @pl.when(tile == last)
    def _():
        copies(tile, slot, wait=True)
        zero_ref[...] = jnp.zeros_like(zero_ref)
        tails = n_token_tiles * ne * SEG_FIELDS
        for wait in (False, True):
            for e in range(ne):
                start = seg_ref[tails + 2 * e]

                def body(i, carry, start=start, wait=wait):
                    cp = pltpu.make_async_copy(
                        zero_ref, hs_ref.at[pl.ds(pl.multiple_of(start + i * SEG_ROWS, SEG_ROWS), SEG_ROWS), :],
                        sem.at[2])
                    if wait:
                        cp.wait()
                    else:
                        cp.start()
                    return carry

                lax.fori_loop(0, seg_ref[tails + 2 * e + 1], body, 0)


def moe_dispatch(x, nw, sh, sc, seg, ldest_rows, n_rows, *, tt, lc):
    bsz, s, d = x.shape
    nt = s // tt
    grid_spec = pltpu.PrefetchScalarGridSpec(
        num_scalar_prefetch=1,
        grid=(bsz, nt),
        in_specs=[pl.BlockSpec((None, tt, d), lambda b, i, sref: (b, i, 0)),
                  pl.BlockSpec((1, d), lambda b, i, sref: (0, 0)),
                  pl.BlockSpec((None, 1, d), lambda b, i, sref: (b, 0, 0)),
                  pl.BlockSpec((None, 1, d), lambda b, i, sref: (b, 0, 0)),
                  pl.BlockSpec((TOP_K, tt), lambda b, i, sref: (0, b * nt + i))],
        out_specs=pl.BlockSpec(memory_space=pl.ANY),
        scratch_shapes=[pltpu.VMEM((2, lc, d), BF16), pltpu.VMEM((SEG_ROWS, d), BF16),
                        pltpu.SemaphoreType.DMA((3,))],
    )
    return pl.pallas_call(
        functools.partial(_dispatch_kernel, n_token_tiles=bsz * nt),
        grid_spec=grid_spec,
        out_shape=jax.ShapeDtypeStruct((n_rows, d), BF16),
        compiler_params=_cparams("arbitrary", "arbitrary"),
        name="moe_dispatch",
    )(seg, x, nw.reshape(1, d), sh, sc, ldest_rows)


def _moe_kernel(te_ref, nu_ref, hs_ref, w1_ref, w3_ref, w2_ref, o_ref, acc_ref):
    i = pl.program_id(0)
    f = pl.program_id(1)

    @pl.when(i < nu_ref[0])
    def _():
        @pl.when(f == 0)
        def _():
            acc_ref[...] = jnp.zeros_like(acc_ref)

        h = hs_ref[...]
        for lo in range(0, w1_ref.shape[1], MOE_HIDDEN_CHUNK):
            hi = min(lo + MOE_HIDDEN_CHUNK, w1_ref.shape[1])
            a = jnp.dot(h, w1_ref[:, lo:hi].astype(BF16), preferred_element_type=F32)
            b = jnp.dot(h, w3_ref[:, lo:hi].astype(BF16), preferred_element_type=F32)
            t = (_silu(a) * b).astype(BF16)
            acc_ref[...] += jnp.dot(t, w2_ref[lo:hi, :].astype(BF16), preferred_element_type=F32)

        @pl.when(f == pl.num_programs(1) - 1)
        def _():
            o_ref[...] = acc_ref[...].astype(o_ref.dtype)

    @pl.when((i >= nu_ref[0]) & (f == 0))
    def _():
        o_ref[...] = jnp.zeros_like(o_ref)


def moe_experts(hs, tile_expert, n_used, w1, w3, w2, *, tm, tf):
    n_rows, d = hs.shape
    dff = w1.shape[2]
    nf = dff // tf
    n_tiles = n_rows // tm

    def last_used(i, nu):
        return jnp.maximum(jnp.minimum(i, nu[0] - 1), 0)

    def row_map(i, f, te, nu):
        return (last_used(i, nu), 0)

    def hidden_block(i, f, nu):
        t = last_used(i, nu)
        step = jnp.where(i < nu[0], f, nf - 1)
        return jnp.where(t % 2 == 0, step, nf - 1 - step)

    def w_in_map(i, f, te, nu):
        return (te[last_used(i, nu)], 0, hidden_block(i, f, nu))

    def w_out_map(i, f, te, nu):
        return (te[last_used(i, nu)], hidden_block(i, f, nu), 0)

    grid_spec = pltpu.PrefetchScalarGridSpec(
        num_scalar_prefetch=2,
        grid=(n_tiles, nf),
        in_specs=[pl.BlockSpec((tm, d), row_map),
                  pl.BlockSpec((None, d, tf), w_in_map),
                  pl.BlockSpec((None, d, tf), w_in_map),
                  pl.BlockSpec((None, tf, d), w_out_map)],
        out_specs=pl.BlockSpec((tm, d), lambda i, f, te, nu: (i, 0)),
        scratch_shapes=[pltpu.VMEM((tm, d), F32)],
    )
    return pl.pallas_call(
        _moe_kernel,
        grid_spec=grid_spec,
        out_shape=jax.ShapeDtypeStruct((n_rows, d), BF16),
        compiler_params=_cparams("arbitrary", "arbitrary"),
        name="moe_experts",
    )(tile_expert, n_used, hs, w1, w3, w2)


def _combine_kernel(seg_ref, ys_ref, x_ref, g_ref, wgt_ref, ld_ref, o_ref, buf_ref, sem):
    tt = x_ref.shape[0]
    lc = buf_ref.shape[1]
    ne = N_EXPERTS
    tile = pl.program_id(0) * pl.num_programs(1) + pl.program_id(1)
    last = pl.num_programs(0) * pl.num_programs(1) - 1
    slot = tile % 2

    def copies(t, sl, wait):
        def make_copy(lo, go):
            return pltpu.make_async_copy(ys_ref.at[pl.ds(go, SEG_ROWS), :],
                                         buf_ref.at[sl, pl.ds(lo, SEG_ROWS), :], sem.at[sl])
        _segment_copies(seg_ref, t, ne, make_copy, wait=wait)

    @pl.when(tile == 0)
    def _():
        buf_ref[...] = jnp.zeros_like(buf_ref)
        copies(tile, slot, wait=False)

    @pl.when(tile < last)
    def _():
        copies(tile + 1, 1 - slot, wait=False)

    copies(tile, slot, wait=True)

    ld = ld_ref[...]
    cols = lax.broadcasted_iota(jnp.int32, (tt, lc), 1)
    pick = jnp.concatenate([jnp.where(cols == ld[:, k:k + 1], 1.0, 0.0) for k in range(TOP_K)],
                           axis=0).astype(BF16)
    z = jnp.dot(pick, buf_ref[slot], preferred_element_type=F32)
    w = wgt_ref[...]
    mix = w[:, 0:1] * z[0:tt] + w[:, 1:2] * z[tt:]
    o_ref[...] = x_ref[...] + g_ref[...] * mix


def moe_combine(ys, seg, ldest, wgt, x, g, *, tt, lc):
    bsz, s, d = x.shape
    nt = s // tt
    tok_spec = pl.BlockSpec((tt, TOP_K), lambda b, i, sref: (b * nt + i, 0))
    grid_spec = pltpu.PrefetchScalarGridSpec(
        num_scalar_prefetch=1,
        grid=(bsz, nt),
        in_specs=[pl.BlockSpec(memory_space=pl.ANY),
                  pl.BlockSpec((None, tt, d), lambda b, i, sref: (b, i, 0)),
                  pl.BlockSpec((None, 1, d), lambda b, i, sref: (b, 0, 0)),
                  tok_spec, tok_spec],
        out_specs=pl.BlockSpec((None, tt, d), lambda b, i, sref: (b, i, 0)),
        scratch_shapes=[pltpu.VMEM((2, lc, d), BF16), pltpu.SemaphoreType.DMA((2,))],
    )
    return pl.pallas_call(
        _combine_kernel,
        grid_spec=grid_spec,
        out_shape=jax.ShapeDtypeStruct((bsz, s, d), F32),
        compiler_params=_cparams("arbitrary", "arbitrary"),
        name="moe_combine",
    )(seg, ys, x, g, wgt, ldest)


def _round_up(v, m):
    return ((v + m - 1) // m) * m


def moe_token_tile(s):
    return min(512, s)


def moe_residual(x, nw, sh, sc, g, ldest, wgt, cnt, w1, w3, w2, *, tm=MOE_TILE_ROWS):
    bsz, s, d = x.shape
    n_tok = bsz * s
    ne = w1.shape[0]
    tt = moe_token_tile(s)
    n_tt = n_tok // tt
    lc = _round_up(TOP_K * tt + ne * SEG_ROWS, 128)
    seg_len = _round_up(cnt.reshape(n_tt, ne), SEG_ROWS)
    local_start = jnp.cumsum(seg_len, axis=1) - seg_len
    padded = _round_up(jnp.sum(seg_len, axis=0), tm)
    ends = jnp.cumsum(padded)
    global_start = (ends - padded)[None, :] + jnp.cumsum(seg_len, axis=0) - seg_len
    n_rows = _round_up(n_tok * TOP_K + n_tt * ne * SEG_ROWS + ne * tm, tm)
    used_end = (ends - padded) + jnp.sum(seg_len, axis=0)
    next_start = jnp.concatenate([ends[:-1], jnp.full((1,), n_rows, ends.dtype)])
    tails = jnp.stack([used_end, (next_start - used_end) // SEG_ROWS], axis=-1)
    seg = jnp.concatenate([jnp.stack([local_start, global_start, seg_len // SEG_ROWS], axis=-1).reshape(-1),
                           tails.reshape(-1)]).astype(jnp.int32)
    n_tiles = n_rows // tm
    tile_start = jnp.arange(n_tiles, dtype=jnp.int32) * tm
    tile_expert = jnp.minimum(jnp.sum(tile_start[:, None] >= ends[None, :], axis=1), ne - 1).astype(jnp.int32)
    n_used = (ends[ne - 1:ne] // tm).astype(jnp.int32)
    hs = moe_dispatch(x, nw, sh, sc, seg, ldest, n_rows, tt=tt, lc=lc)
    ys = moe_experts(hs, tile_expert, n_used, w1, w3, w2, tm=tm, tf=w1.shape[2] // 2)
    return moe_combine(ys, seg, ldest.T, wgt.T, x, g, tt=tt, lc=lc)


def _split_mod(mod):
    return [m[:, None, :] for m in jnp.split(mod, 6, axis=-1)]


def even_layer(x, c, rel_bias, ada_w, ada_b, norm1_w, in_w, q_norm_w, k_norm_w, sink, out_w,
               norm2_w, w1, w3, w2):
    s = x.shape[1]
    sh1, sc1, g1, sh2, sc2, g2 = _split_mod(ada_mod(c, ada_w, ada_b))
    proj = norm_mod_matmul(x, norm1_w, sh1, sc1, in_w.astype(BF16), ts=min(1024, s), tn=256, name="even_in_proj",
                           out_dtype=BF16)
    yf = fourier_mix(proj, tq=min(512, s))
    ya = window_attention(proj, band_bias(rel_bias), q_norm_w, k_norm_w, sink)
    x = cat_proj_residual(yf, ya, out_w.astype(BF16), x, g1, ts=min(1024, s))
    return ffn_residual(x, norm2_w, sh2, sc2, g2, w1.astype(BF16), w3.astype(BF16), w2,
                        ts=min(1024, s), tf=256)


def odd_layer(x, c, ada_w, ada_b, norm1_w, in_w, conv_w, conv_b, dt_bias_f, dt_bias_b, a_log_f, a_log_b,
              d_skip, gnorm_w, out_w, norm2_w, router_w, router_b, w1, w3, w2):
    s = x.shape[1]
    sh1, sc1, g1, sh2, sc2, g2 = _split_mod(ada_mod(c, ada_w, ada_b))
    d_inner = gnorm_w.shape[0]
    cdim = conv_w.shape[1]
    wide = d_inner + cdim
    in_w = in_w.astype(BF16)
    z, xs, bc, dt = ssm_in_proj_conv(x, norm1_w, sh1, sc1, in_w[:, :d_inner], in_w[:, d_inner:wide], in_w[:, wide:],
                                     conv_w, conv_b, n_x=d_inner, ts=min(512, s))
    y = ssd_scan_bidir(xs, bc, dt, dt_bias_f, dt_bias_b, a_log_f, a_log_b, d_skip)
    x, ldest, wgt, cnt = gated_proj_route(y, z, gnorm_w, out_w.astype(BF16), x, g1, norm2_w, sh2, sc2,
                                          router_w, router_b, ts=moe_token_tile(s))
    return moe_residual(x, norm2_w, sh2, sc2, g2, ldest, wgt, cnt,
                        w1, w3, w2)


def kernel(x, c, rel_bias, ev_ada_w, ev_ada_b, ev_norm1_w, ev_in_w, ev_q_norm_w, ev_k_norm_w, ev_sink, ev_out_w, ev_norm2_w, ev_ffn_w1, ev_ffn_w3, ev_ffn_w2, od_ada_w, od_ada_b, od_norm1_w, od_in_w, od_conv_w, od_conv_b, od_dt_bias_f, od_dt_bias_b, od_A_log_f, od_A_log_b, od_D, od_gnorm_w, od_out_w, od_norm2_w, od_router_w, od_router_b, od_moe_w1, od_moe_w3, od_moe_w2):
    depth = ev_ada_w.shape[0] + od_ada_w.shape[0]
    for i in range(depth):
        j = i // 2
        if i % 2 == 0:
            x = even_layer(x, c, rel_bias, ev_ada_w[j], ev_ada_b[j], ev_norm1_w[j], ev_in_w[j],
                           ev_q_norm_w[j], ev_k_norm_w[j], ev_sink[j], ev_out_w[j], ev_norm2_w[j],
                           ev_ffn_w1[j], ev_ffn_w3[j], ev_ffn_w2[j])
        else:
            x = odd_layer(x, c, od_ada_w[j], od_ada_b[j], od_norm1_w[j], od_in_w[j], od_conv_w[j],
                          od_conv_b[j], od_dt_bias_f[j], od_dt_bias_b[j], od_A_log_f[j], od_A_log_b[j],
                          od_D[j], od_gnorm_w[j], od_out_w[j], od_norm2_w[j], od_router_w[j],
                          od_router_b[j], od_moe_w1[j], od_moe_w3[j], od_moe_w2[j])
    return x
```

```python
import functools

import numpy as np
import jax
import jax.numpy as jnp
from jax import lax
from jax.experimental import pallas as pl
from jax.experimental.pallas import tpu as pltpu

F32 = jnp.float32
BF16 = jnp.bfloat16

EPS = 1e-6
FNET_GROUPS = 4
FNET_GROUP_DIM = 128
FNET_WIDTH = FNET_GROUPS * FNET_GROUP_DIM
ATTN_HEADS = 8
ATTN_KV_HEADS = 2
HEAD_DIM = 64
ATTN_WIDTH = ATTN_HEADS * HEAD_DIM
KV_WIDTH = ATTN_KV_HEADS * HEAD_DIM
WINDOW = 128
BLOCK = 128
REL_BUCKETS = 32
REL_MAX_DIST = 128
SSM_HEAD_DIM = 64
SSM_GROUPS = 4
D_STATE = 128
CONV_WIDTH = 5
SSD_CHUNK = 128
N_EXPERTS = 8
TOP_K = 2
NEG_BIG = -1e30
ATTN_QBLOCKS = 2

V7X_VMEM_LIMIT_BYTES = 56 * 1024 * 1024
MOE_TILE_ROWS = 768
MOE_HIDDEN_CHUNK = 256


def _cparams(*sem):
    return pltpu.CompilerParams(dimension_semantics=sem, vmem_limit_bytes=V7X_VMEM_LIMIT_BYTES)


def _modnorm(x, nw, sh, sc):
    ms = jnp.mean(x * x, axis=-1, keepdims=True)
    return x * lax.rsqrt(ms + EPS) * nw * (1.0 + sc) + sh


def _silu(x):
    return x * (1.0 / (1.0 + jnp.exp(-x)))


def _ada_kernel(c_ref, w_ref, b_ref, o_ref):
    cs = _silu(c_ref[...]).astype(BF16)
    val = jnp.dot(cs, w_ref[...].astype(BF16), preferred_element_type=F32) + b_ref[...]
    for row in range(val.shape[0]):
        o_ref[row] = val[row:row + 1, :]


def ada_mod(c, w, b, *, n_chunks=6):
    bsz, d = c.shape
    n = w.shape[1]
    tn = n // n_chunks
    return pl.pallas_call(
        _ada_kernel,
        grid=(n_chunks,),
        in_specs=[pl.BlockSpec((bsz, d), lambda j: (0, 0)),
                  pl.BlockSpec((d, tn), lambda j: (0, j)),
                  pl.BlockSpec((1, tn), lambda j: (0, j))],
        out_specs=pl.BlockSpec((None, bsz, 1, tn), lambda j: (j, 0, 0, 0)),
        out_shape=jax.ShapeDtypeStruct((n_chunks, bsz, 1, tn), F32),
        compiler_params=_cparams("arbitrary"),
        name="ada_mod",
    )(c, w, b.reshape(1, n))


def _nmm_kernel(x_ref, nw_ref, sh_ref, sc_ref, w_ref, o_ref, *, tn):
    h = _modnorm(x_ref[...], nw_ref[...], sh_ref[...], sc_ref[...]).astype(BF16)
    n = w_ref.shape[1]
    for lo in range(0, n, tn):
        o_ref[:, lo:lo + tn] = jnp.dot(h, w_ref[:, lo:lo + tn],
                                       preferred_element_type=F32).astype(o_ref.dtype)


def _resident(shape):
    return pl.BlockSpec(shape, lambda *_: tuple(0 for _ in shape), pipeline_mode=pl.Buffered(1))


def norm_mod_matmul(x, nw, sh, sc, w, *, ts, tn, name, out_dtype=F32):
    bsz, s, d = x.shape
    n = w.shape[1]
    assert n % tn == 0
    return pl.pallas_call(
        functools.partial(_nmm_kernel, tn=tn),
        grid=(bsz, s // ts),
        in_specs=[pl.BlockSpec((None, ts, d), lambda b, i: (b, i, 0)),
                  pl.BlockSpec((1, d), lambda b, i: (0, 0)),
                  pl.BlockSpec((None, 1, d), lambda b, i: (b, 0, 0)),
                  pl.BlockSpec((None, 1, d), lambda b, i: (b, 0, 0)),
                  _resident((d, n))],
        out_specs=pl.BlockSpec((None, ts, n), lambda b, i: (b, i, 0)),
        out_shape=jax.ShapeDtypeStruct((bsz, s, n), out_dtype),
        compiler_params=_cparams("parallel", "parallel"),
        name=name,
    )(x, nw.reshape(1, d), sh, sc, w)


def _dft_cos_sin(n):
    k = np.arange(n, dtype=np.int64)
    ang = ((k[:, None] * k[None, :]) % n).astype(np.float64) * (2.0 * np.pi / n)
    scale = 1.0 / np.sqrt(n)
    return np.cos(ang) * scale, np.sin(ang) * scale


def _fourier_kernel(u_ref, chan_ref, seq_ref, o_ref, ab_ref):
    s = u_ref.shape[0]

    @pl.when(pl.program_id(1) == 0)
    def _():
        for g in range(FNET_GROUPS):
            lo, hi = g * FNET_GROUP_DIM, (g + 1) * FNET_GROUP_DIM
            ug = u_ref[:, lo:hi].astype(BF16)
            cs = jnp.dot(ug, chan_ref[...], preferred_element_type=F32)
            ab_ref[0:s, lo:hi] = cs[:, :FNET_GROUP_DIM].astype(BF16)
            ab_ref[s:2 * s, lo:hi] = cs[:, FNET_GROUP_DIM:].astype(BF16)

    o_ref[...] = jnp.dot(seq_ref[...], ab_ref[...], preferred_element_type=F32).astype(o_ref.dtype)


def fourier_mix(proj, *, tq):
    bsz, s, _ = proj.shape
    cc, sc = _dft_cos_sin(FNET_GROUP_DIM)
    chan = jnp.asarray(np.concatenate([cc, sc], axis=1), BF16)
    cs, ss = _dft_cos_sin(s)
    seq = jnp.asarray(np.concatenate([cs, -ss], axis=1), BF16)
    return pl.pallas_call(
        _fourier_kernel,
        grid=(bsz, s // tq),
        in_specs=[pl.BlockSpec((None, s, FNET_WIDTH), lambda b, i: (b, 0, 0)),
                  pl.BlockSpec((FNET_GROUP_DIM, 2 * FNET_GROUP_DIM), lambda b, i: (0, 0)),
                  pl.BlockSpec((tq, 2 * s), lambda b, i: (i, 0))],
        out_specs=pl.BlockSpec((None, tq, FNET_WIDTH), lambda b, i: (b, i, 0)),
        out_shape=jax.ShapeDtypeStruct((bsz, s, FNET_WIDTH), BF16),
        scratch_shapes=[pltpu.VMEM((2 * s, FNET_WIDTH), BF16)],
        compiler_params=_cparams("parallel", "arbitrary"),
        name="fourier_mix",
    )(proj, chan, seq)


def _band_bucket_table():
    i = np.arange(BLOCK)[:, None]
    j = np.arange(3 * BLOCK)[None, :]
    rel = (j - BLOCK) - i
    half = REL_BUCKETS // 2
    max_exact = half // 2
    n = np.abs(rel)
    large = max_exact + (np.log(np.maximum(n, 1) / max_exact)
                         / np.log(REL_MAX_DIST / max_exact) * (half - max_exact)).astype(np.int32)
    large = np.minimum(large, half - 1)
    bucket = (rel > 0).astype(np.int32) * half + np.where(n < max_exact, n, large)
    return np.where(n <= WINDOW, bucket, -1).astype(np.int32)


def _bias_kernel(rb_ref, bucket_ref, o_ref):
    h = pl.program_id(0)
    bucket = bucket_ref[...]
    acc = jnp.full(bucket.shape, NEG_BIG, F32)
    for bkt in range(REL_BUCKETS):
        acc = jnp.where(bucket == bkt, rb_ref[bkt * ATTN_HEADS + h], acc)
    o_ref[...] = acc


def band_bias(rel_bias):
    bucket = jnp.asarray(_band_bucket_table())
    return pl.pallas_call(
        _bias_kernel,
        grid=(ATTN_HEADS,),
        in_specs=[pl.BlockSpec(memory_space=pltpu.SMEM),
                  pl.BlockSpec((BLOCK, 3 * BLOCK), lambda h: (0, 0))],
        out_specs=pl.BlockSpec((None, BLOCK, 3 * BLOCK), lambda h: (h, 0, 0)),
        out_shape=jax.ShapeDtypeStruct((ATTN_HEADS, BLOCK, 3 * BLOCK), F32),
        compiler_params=_cparams("arbitrary"),
        name="band_bias",
    )(rel_bias.reshape(-1), bucket)


def _head_mean_matrix(width):
    m = np.zeros((width, width), np.float32)
    for h in range(width // HEAD_DIM):
        m[h * HEAD_DIM:(h + 1) * HEAD_DIM, h * HEAD_DIM:(h + 1) * HEAD_DIM] = 1.0 / HEAD_DIM
    return m


def _heads_rms(t, mean_mat, w):
    sq = t * t
    hi = sq.astype(BF16)
    lo = (sq - hi.astype(F32)).astype(BF16)
    ms = (jnp.dot(hi, mean_mat, preferred_element_type=F32)
          + jnp.dot(lo, mean_mat, preferred_element_type=F32))
    return t * lax.rsqrt(ms + EPS) * w


def _attn_kernel(sink_ref, q_ref, kl_ref, kc_ref, kr_ref, vl_ref, vc_ref, vr_ref,
                 bias_ref, qnw_ref, knw_ref, qmean_ref, kmean_ref, o_ref):
    n = pl.program_id(1)
    nb = pl.num_programs(1) * ATTN_QBLOCKS
    k = jnp.concatenate([kl_ref[...], kc_ref[...], kr_ref[...]], axis=0).astype(F32)
    v = jnp.concatenate([vl_ref[...], vc_ref[...], vr_ref[...]], axis=0).astype(F32)
    col = lax.broadcasted_iota(jnp.int32, (1, 3 * BLOCK), 1)
    qn = _heads_rms(q_ref[...].astype(F32), qmean_ref[...], qnw_ref[...])
    kn = _heads_rms(k, kmean_ref[...], knw_ref[...])
    low = lax.broadcasted_iota(jnp.int32, (1, 2 * HEAD_DIM), 1) < HEAD_DIM
    kn_sw = pltpu.roll(kn, HEAD_DIM, axis=1)
    v_sw = pltpu.roll(v, HEAD_DIM, axis=1)
    k_dup = [jnp.where(low, kn, kn_sw).astype(BF16), jnp.where(low, kn_sw, kn).astype(BF16)]
    ones = jnp.ones((k.shape[0], 2 * HEAD_DIM), BF16)
    v_ext = [jnp.concatenate([v.astype(BF16), ones], axis=1),
             jnp.concatenate([v_sw.astype(BF16), ones], axis=1)]
    g = ATTN_HEADS // ATTN_KV_HEADS
    for qb in range(ATTN_QBLOCKS):
        blk = n * ATTN_QBLOCKS + qb
        band = slice(qb * BLOCK, (qb + 3) * BLOCK)
        first_key = jnp.where(blk == 0, BLOCK, 0)
        end_key = jnp.where(blk == nb - 1, 2 * BLOCK, 3 * BLOCK)
        outside = (col < first_key) | (col >= end_key)
        raw = []
        for h in range(ATTN_HEADS):
            m, idx, j = h // 2, h % 2, h // g
            qp = qn[qb * BLOCK:(qb + 1) * BLOCK, m * 2 * HEAD_DIM:(m + 1) * 2 * HEAD_DIM]
            qm = jnp.where(low if idx == 0 else jnp.logical_not(low), qp, 0.0).astype(BF16)
            raw.append(lax.dot_general(qm, k_dup[j][band], (((1,), (1,)), ((), ())),
                                       preferred_element_type=F32))
        probs, tails = [], []
        for h in range(ATTN_HEADS):
            logits = jnp.where(outside, NEG_BIG, raw[h] + bias_ref[h])
            sk = sink_ref[h]
            mx = jnp.maximum(jnp.max(logits, axis=-1, keepdims=True), sk)
            probs.append(jnp.exp(logits - mx).astype(BF16))
            tails.append(jnp.exp(sk - mx))
        res = []
        for h in range(ATTN_HEADS):
            idx, j = h % 2, h // g
            r = jnp.dot(probs[h], v_ext[idx if j == 0 else 1 - idx][band], preferred_element_type=F32)
            res.append(r[:, :2 * HEAD_DIM] / (r[:, 2 * HEAD_DIM:] + tails[h]))
        pairs = [jnp.where(low, res[2 * m], res[2 * m + 1]) for m in range(ATTN_HEADS // 2)]
        o_ref[qb * BLOCK:(qb + 1) * BLOCK, :] = jnp.concatenate(pairs, axis=-1).astype(o_ref.dtype)


def window_attention(proj, bias, q_norm_w, k_norm_w, sink):
    bsz, s, _ = proj.shape
    nb = s // BLOCK
    qcol = FNET_WIDTH // ATTN_WIDTH
    kcol = (FNET_WIDTH + ATTN_WIDTH) // KV_WIDTH
    vcol = kcol + 1

    qb = ATTN_QBLOCKS
    assert nb % qb == 0

    def kv_specs(col):
        return [pl.BlockSpec((None, BLOCK, KV_WIDTH), lambda b, n: (b, jnp.maximum(n * qb - 1, 0), col)),
                pl.BlockSpec((None, qb * BLOCK, KV_WIDTH), lambda b, n: (b, n, col)),
                pl.BlockSpec((None, BLOCK, KV_WIDTH), lambda b, n: (b, jnp.minimum((n + 1) * qb, nb - 1), col))]

    return pl.pallas_call(
        _attn_kernel,
        grid=(bsz, nb // qb),
        in_specs=[pl.BlockSpec(memory_space=pltpu.SMEM),
                  pl.BlockSpec((None, qb * BLOCK, ATTN_WIDTH), lambda b, n: (b, n, qcol)),
                  *kv_specs(kcol), *kv_specs(vcol),
                  pl.BlockSpec((ATTN_HEADS, BLOCK, 3 * BLOCK), lambda b, n: (0, 0, 0)),
                  pl.BlockSpec((1, ATTN_WIDTH), lambda b, n: (0, 0)),
                  pl.BlockSpec((1, KV_WIDTH), lambda b, n: (0, 0)),
                  pl.BlockSpec((ATTN_WIDTH, ATTN_WIDTH), lambda b, n: (0, 0)),
                  pl.BlockSpec((KV_WIDTH, KV_WIDTH), lambda b, n: (0, 0))],
        out_specs=pl.BlockSpec((None, qb * BLOCK, ATTN_WIDTH), lambda b, n: (b, n, 0)),
        out_shape=jax.ShapeDtypeStruct((bsz, s, ATTN_WIDTH), BF16),
        compiler_params=_cparams("parallel", "arbitrary"),
        name="window_attention",
    )(sink, proj, proj, proj, proj, proj, proj, proj, bias,
      (jnp.tile(q_norm_w, ATTN_HEADS) * (HEAD_DIM ** -0.5)).reshape(1, ATTN_WIDTH),
      jnp.tile(k_norm_w, ATTN_KV_HEADS).reshape(1, KV_WIDTH),
      jnp.asarray(_head_mean_matrix(ATTN_WIDTH), BF16), jnp.asarray(_head_mean_matrix(KV_WIDTH), BF16))


def _cat_proj_kernel(a1_ref, a2_ref, w_ref, x_ref, g_ref, o_ref):
    k1 = a1_ref.shape[1]
    y = jnp.dot(a1_ref[...].astype(BF16), w_ref[0:k1, :], preferred_element_type=F32)
    y = y + jnp.dot(a2_ref[...].astype(BF16), w_ref[k1:, :], preferred_element_type=F32)
    o_ref[...] = x_ref[...] + g_ref[...] * y


def cat_proj_residual(a1, a2, w, x, g, *, ts):
    bsz, s, d = x.shape
    k1, k2 = a1.shape[2], a2.shape[2]
    return pl.pallas_call(
        _cat_proj_kernel,
        grid=(bsz, s // ts),
        in_specs=[pl.BlockSpec((None, ts, k1), lambda b, i: (b, i, 0)),
                  pl.BlockSpec((None, ts, k2), lambda b, i: (b, i, 0)),
                  pl.BlockSpec((k1 + k2, d), lambda b, i: (0, 0)),
                  pl.BlockSpec((None, ts, d), lambda b, i: (b, i, 0)),
                  pl.BlockSpec((None, 1, d), lambda b, i: (b, 0, 0))],
        out_specs=pl.BlockSpec((None, ts, d), lambda b, i: (b, i, 0)),
        out_shape=jax.ShapeDtypeStruct((bsz, s, d), F32),
        compiler_params=_cparams("parallel", "parallel"),
        name="mixer_out_proj",
    )(a1, a2, w, x, g)


def _ffn_kernel(x_ref, nw_ref, sh_ref, sc_ref, g_ref, w1_ref, w3_ref, w2_ref, o_ref, acc_ref, *, tf):
    h = _modnorm(x_ref[...], nw_ref[...], sh_ref[...], sc_ref[...]).astype(BF16)
    dff = w1_ref.shape[1]
    for lo in range(0, dff, tf):
        a = jnp.dot(h, w1_ref[:, lo:lo + tf], preferred_element_type=F32)
        b = jnp.dot(h, w3_ref[:, lo:lo + tf], preferred_element_type=F32)
        t = (_silu(a) * b).astype(BF16)
        contrib = jnp.dot(t, w2_ref[lo:lo + tf, :].astype(BF16), preferred_element_type=F32)
        if lo == 0:
            acc_ref[...] = contrib
        else:
            acc_ref[...] += contrib
    o_ref[...] = x_ref[...] + g_ref[...] * acc_ref[...]


def ffn_residual(x, nw, sh, sc, g, w1, w3, w2, *, ts, tf):
    bsz, s, d = x.shape
    dff = w1.shape[1]
    assert dff % tf == 0
    vec = pl.BlockSpec((None, 1, d), lambda b, i: (b, 0, 0))
    return pl.pallas_call(
        functools.partial(_ffn_kernel, tf=tf),
        grid=(bsz, s // ts),
        in_specs=[pl.BlockSpec((None, ts, d), lambda b, i: (b, i, 0)),
                  pl.BlockSpec((1, d), lambda b, i: (0, 0)),
                  vec, vec, vec,
                  _resident((d, dff)), _resident((d, dff)), _resident((dff, d))],
        out_specs=pl.BlockSpec((None, ts, d), lambda b, i: (b, i, 0)),
        out_shape=jax.ShapeDtypeStruct((bsz, s, d), F32),
        scratch_shapes=[pltpu.VMEM((ts, d), F32)],
        compiler_params=_cparams("parallel", "parallel"),
        name="ffn_swiglu",
    )(x, nw.reshape(1, d), sh, sc, g, w1, w3, w2)


HALO_ROWS = 16
PROJ_CHUNK = 256
W_VIEW = 1024


def _proj_conv_kernel(x_ref, xp_ref, xn_ref, nw_ref, sh_ref, sc_ref, *rest, n_views):
    w_views = rest[:n_views]
    wd_ref, cw_ref, cb_ref, z_ref, xs_ref, bc_ref, dt_ref = rest[n_views:]
    i = pl.program_id(1)
    ts = x_ref.shape[0]
    half = CONV_WIDTH // 2
    dz = z_ref.shape[1]
    dc = cw_ref.shape[1]

    def wcols(c0):
        return w_views[c0 // W_VIEW][:, c0 % W_VIEW:c0 % W_VIEW + PROJ_CHUNK]

    def hnorm(ref):
        return _modnorm(ref[...], nw_ref[...], sh_ref[...], sc_ref[...])

    h = hnorm(x_ref).astype(BF16)
    h_prev = jnp.where(i == 0, 0.0, hnorm(xp_ref)).astype(BF16)
    h_next = jnp.where(i == pl.num_programs(1) - 1, 0.0, hnorm(xn_ref)).astype(BF16)
    h_ext = jnp.concatenate([h_prev, h, h_next], axis=0)

    n_x = xs_ref.shape[1]
    rows = h_ext.shape[0]

    def project(lo):
        return jnp.dot(h_ext, wcols(dz + lo), preferred_element_type=F32)

    def z_chunk(lo):
        hi = lo + PROJ_CHUNK
        z_ref[:, lo:hi] = jnp.dot(h, wcols(lo), preferred_element_type=F32).astype(z_ref.dtype)

    z_los = list(range(0, dz, PROJ_CHUNK))
    c_los = list(range(0, dc, PROJ_CHUNK))
    p_next = project(c_los[0])
    for n, lo in enumerate(c_los):
        hi = lo + PROJ_CHUNK
        p = p_next
        if n + 1 < len(c_los):
            p_next = project(c_los[n + 1])
        if z_los:
            z_chunk(z_los.pop(0))
        acc = jnp.zeros((ts, PROJ_CHUNK), F32) + cb_ref[:, lo:hi]
        for kk in range(CONV_WIDTH):
            shifted = p if kk == half else pltpu.roll(p, (half - kk) % rows, axis=0)
            acc = acc + shifted[HALO_ROWS:HALO_ROWS + ts, :] * cw_ref[kk:kk + 1, lo:hi]
        out = _silu(acc)
        if lo < n_x:
            xs_ref[:, lo:hi] = out
        else:
            bc_ref[:, lo - n_x:hi - n_x] = out.astype(bc_ref.dtype)
    for lo in z_los:
        z_chunk(lo)
    dt_ref[...] = lax.dot_general(wd_ref[...], h, (((1,), (1,)), ((), ())), preferred_element_type=F32)


def ssm_in_proj_conv(x, nw, sh, sc, w_in, conv_w, conv_b, *, dz, n_x, ts):
    bsz, s, d = x.shape
    dc = conv_w.shape[1]
    ddt = w_in.shape[1] - dz - dc
    assert ts % HALO_ROWS == 0 and n_x % PROJ_CHUNK == 0 and dc % PROJ_CHUNK == 0 and dz % PROJ_CHUNK == 0
    assert dz % W_VIEW == 0 and dc % W_VIEW == 0 and W_VIEW % PROJ_CHUNK == 0
    n_views = (dz + dc) // W_VIEW
    views = [pl.BlockSpec((d, W_VIEW), lambda b, i, k=k: (0, k), pipeline_mode=pl.Buffered(1))
             for k in range(n_views)]
    r = ts // HALO_ROWS
    last = s // HALO_ROWS - 1
    vec = pl.BlockSpec((None, 1, d), lambda b, i: (b, 0, 0))
    return pl.pallas_call(
        functools.partial(_proj_conv_kernel, n_views=n_views),
        grid=(bsz, s // ts),
        in_specs=[pl.BlockSpec((None, ts, d), lambda b, i: (b, i, 0)),
                  pl.BlockSpec((None, HALO_ROWS, d), lambda b, i: (b, jnp.maximum(i * r - 1, 0), 0)),
                  pl.BlockSpec((None, HALO_ROWS, d), lambda b, i: (b, jnp.minimum((i + 1) * r, last), 0)),
                  pl.BlockSpec((1, d), lambda b, i: (0, 0)), vec, vec,
                  *views, _resident((ddt, d)),
                  _resident((CONV_WIDTH, dc)), _resident((1, dc))],
        out_specs=[pl.BlockSpec((None, ts, dz), lambda b, i: (b, i, 0)),
                   pl.BlockSpec((None, ts, n_x), lambda b, i: (b, i, 0)),
                   pl.BlockSpec((None, ts, dc - n_x), lambda b, i: (b, i, 0)),
                   pl.BlockSpec((None, ddt, ts), lambda b, i: (b, 0, i))],
        out_shape=[jax.ShapeDtypeStruct((bsz, s, dz), BF16),
                   jax.ShapeDtypeStruct((bsz, s, n_x), F32),
                   jax.ShapeDtypeStruct((bsz, s, dc - n_x), BF16),
                   jax.ShapeDtypeStruct((bsz, ddt, s), F32)],
        compiler_params=_cparams("parallel", "parallel"),
        name="odd_in_proj_conv",
    )(x, x, x, nw.reshape(1, d), sh, sc, *([w_in] * n_views), w_in[:, dz + dc:].T, conv_w, conv_b.reshape(1, dc))


def _softplus(x):
    return jnp.maximum(x, 0.0) + jnp.log(1.0 + jnp.exp(-jnp.abs(x)))


LOG2E = 1.4426950408889634
DECAY_SLOTS = 12


def _bf16_parts3(v):
    hi = v.astype(BF16).astype(F32)
    r = v - hi
    mid = r.astype(BF16).astype(F32)
    lo = (r - mid).astype(BF16).astype(F32)
    return hi, mid, lo


def _ssd_t_kernel(x_ref, b_ref, c_ref, dt_f_ref, dt_b_ref, p_f_ref, p_b_ref, dx_ref, trio_ref, y_ref,
                  ar_ref, dr_ref, sc_ref, pq_ref, qt_ref, xtb_ref, xdf_ref, xdb_ref, yt_ref, hf_ref, hb_ref):
    s = x_ref.shape[0]
    q = SSD_CHUNK
    nh = p_f_ref.shape[0]
    nc = s // q
    hd = SSM_HEAD_DIM

    def row_params(raw_ref, p_ref):
        dt = _softplus(raw_ref[...] + p_ref[:, 0:1])
        return dt, (-LOG2E) * jnp.exp(p_ref[:, 1:2]) * dt

    dtf, af = row_params(dt_f_ref, p_f_ref)
    dtb, ab = row_params(dt_b_ref, p_b_ref)
    dr_ref[0:nh, :] = dtf
    dr_ref[nh:, :] = dtb
    ar_ref[0:nh, :] = af
    ar_ref[nh:, :] = ab

    li = lax.broadcasted_iota(jnp.int32, (q, q), 0)
    si = lax.broadcasted_iota(jnp.int32, (q, q), 1)
    lower = li >= si
    upper = li <= si
    slot_head = lax.broadcasted_iota(jnp.int32, (1, q), 1) % nh

    def bdot(a, b):
        return jnp.dot(a, b, preferred_element_type=F32)

    def ntdot(a, b):
        return lax.dot_general(a, b, (((1,), (1,)), ((), ())), preferred_element_type=F32)

    def head_rows(v):
        return jnp.concatenate([jnp.broadcast_to(v[h:h + 1, :], (hd, q)) for h in range(nh)], axis=0)

    ones = jnp.ones((nh, q), F32)
    zeros = jnp.zeros((nh, q), F32)

    def prep_body(c, carry):
        sl = pl.ds(pl.multiple_of(c * q, q), q)
        a_row = ar_ref[:, sl]
        d_row = dr_ref[:, sl]
        parts = jnp.concatenate([p.astype(BF16) for p in _bf16_parts3(a_row)], axis=0)
        cs = bdot(parts, trio_ref[...])
        cs = cs[0:2 * nh] + cs[2 * nh:4 * nh] + cs[4 * nh:6 * nh]
        i_f = cs[0:nh, 0:q]
        e_b = cs[nh:, 0:q] - a_row[nh:]
        tot_f = cs[0:nh, q:]
        tot_b = cs[nh:, q:]
        ih, im, il = _bf16_parts3(i_f)
        eh, em, el = _bf16_parts3(e_b)
        pad = [zeros] * (q // nh - DECAY_SLOTS)
        p_t = jnp.concatenate([ih, im, il, ones, ones, ones, -eh, -em, -el, ones, ones, ones] + pad, axis=0)
        qf_t = jnp.concatenate([ones, ones, ones, -ih, -im, -il] + [zeros] * 6 + pad, axis=0)
        qb_t = jnp.concatenate([zeros] * 6 + [ones, ones, ones, eh, em, el] + pad, axis=0)
        pq_ref[sl, :] = p_t.T.astype(BF16)
        qt_ref[c] = jnp.concatenate([qf_t, qb_t], axis=1).astype(BF16)
        sc_ref[0 * nh:1 * nh, sl] = jnp.exp2(i_f)
        sc_ref[1 * nh:2 * nh, sl] = jnp.exp2(tot_b - e_b)
        sc_ref[2 * nh:3 * nh, sl] = jnp.exp2(tot_f)
        sc_ref[3 * nh:4 * nh, sl] = jnp.exp2(tot_b)
        xt = x_ref[sl, :].T
        xtb_ref[:, sl] = xt.astype(BF16)
        xdf_ref[:, sl] = (xt * head_rows(jnp.exp2(tot_f - i_f) * d_row[0:nh])).astype(BF16)
        xdb_ref[:, sl] = (xt * head_rows(jnp.exp2(e_b) * d_row[nh:])).astype(BF16)
        return carry

    lax.fori_loop(0, nc, prep_body, 0, unroll=2)

    hf_ref[...] = jnp.zeros_like(hf_ref)
    hb_ref[...] = jnp.zeros_like(hb_ref)
    zero_half = jnp.zeros((hd, q), BF16)

    def fwd_body(c, carry):
        sl = pl.ds(pl.multiple_of(c * q, q), q)
        bc = b_ref[sl, :].astype(BF16)
        cc = c_ref[sl, :].astype(BF16)
        d_row = dr_ref[:, sl]
        p_all = pq_ref[sl, :]
        q_t = qt_ref[c]
        xtb = xtb_ref[:, sl]
        cb = ntdot(cc, bc)
        g2s = [bdot(jnp.where(slot_head == h, p_all, jnp.zeros_like(p_all)), q_t) for h in range(nh)]
        ms = []
        for h in range(nh):
            arg = jnp.where(lower, g2s[h][:, 0:q], g2s[h][:, q:])
            wgt = (jnp.where(lower, d_row[h:h + 1, :], 0.0)
                   + jnp.where(upper, d_row[nh + h:nh + h + 1, :], 0.0))
            ms.append((cb * jnp.exp2(arg) * wgt).astype(BF16))
        yd = []
        for h0 in range(0, nh, 2):
            lhs = jnp.concatenate(
                [jnp.concatenate([xtb[h0 * hd:(h0 + 1) * hd], zero_half], axis=0),
                 jnp.concatenate([zero_half, xtb[(h0 + 1) * hd:(h0 + 2) * hd]], axis=0)], axis=1)
            yd.append(ntdot(lhs, jnp.concatenate(ms[h0:h0 + 2], axis=1)))
        states = bdot(xdf_ref[:, sl], bc)
        h_prev = hf_ref[...]
        y_off = ntdot(h_prev.astype(BF16), cc) * head_rows(sc_ref[0 * nh:1 * nh, sl])
        hf_ref[...] = h_prev * head_rows(sc_ref[2 * nh:3 * nh, sl]) + states
        yt_ref[:, sl] = jnp.concatenate(yd, axis=0) + y_off
        return carry

    lax.fori_loop(0, nc, fwd_body, 0, unroll=4)

    def bwd_body(t, carry):
        c = nc - 1 - t
        sl = pl.ds(pl.multiple_of(c * q, q), q)
        bc = b_ref[sl, :].astype(BF16)
        cc = c_ref[sl, :].astype(BF16)
        states = bdot(xdb_ref[:, sl], bc)
        h_prev = hb_ref[...]
        y_off = ntdot(h_prev.astype(BF16), cc) * head_rows(sc_ref[1 * nh:2 * nh, sl])
        hb_ref[...] = h_prev * head_rows(sc_ref[3 * nh:4 * nh, sl]) + states
        y_ref[sl, :] = ((yt_ref[:, sl] + y_off).T + dx_ref[...] * x_ref[sl, :]).astype(y_ref.dtype)
        return carry

    lax.fori_loop(0, nc, bwd_body, 0, unroll=4)


def ssd_scan_bidir(xs, bc, dt, dt_bias_f, dt_bias_b, a_log_f, a_log_b, d_skip):
    bsz, s, _ = xs.shape
    nheads = dt.shape[1] // 2
    nh = nheads // SSM_GROUPS
    gw = nh * SSM_HEAD_DIM
    d_inner = nheads * SSM_HEAD_DIM
    q = SSD_CHUNK
    dt_row = dt.reshape(bsz, 2 * SSM_GROUPS, nh, s)
    prm = jnp.stack([jnp.concatenate([dt_bias_f, dt_bias_b]), jnp.concatenate([a_log_f, a_log_b])])
    p_row = jnp.transpose(prm.reshape(2, 2 * SSM_GROUPS, nh), (1, 2, 0))
    dx = jnp.repeat(d_skip, SSM_HEAD_DIM).reshape(SSM_GROUPS, 1, gw)
    assert DECAY_SLOTS * nh <= q and q % nh == 0 and D_STATE == q
    trio = jnp.asarray(np.concatenate([np.triu(np.ones((q, q), np.float32)), np.ones((q, q), np.float32)],
                                      axis=1), BF16)
    G = SSM_GROUPS
    nc = s // q

    return pl.pallas_call(
        _ssd_t_kernel,
        grid=(bsz, SSM_GROUPS),
        in_specs=[pl.BlockSpec((None, s, gw), lambda b, g: (b, 0, g)),
                  pl.BlockSpec((None, s, D_STATE), lambda b, g: (b, 0, g)),
                  pl.BlockSpec((None, s, D_STATE), lambda b, g: (b, 0, G + g)),
                  pl.BlockSpec((None, None, nh, s), lambda b, g: (b, g, 0, 0)),
                  pl.BlockSpec((None, None, nh, s), lambda b, g: (b, G + g, 0, 0)),
                  pl.BlockSpec((None, nh, 2), lambda b, g: (g, 0, 0)),
                  pl.BlockSpec((None, nh, 2), lambda b, g: (G + g, 0, 0)),
                  pl.BlockSpec((None, 1, gw), lambda b, g: (g, 0, 0)),
                  pl.BlockSpec((q, 2 * q), lambda b, g: (0, 0))],
        out_specs=pl.BlockSpec((None, s, gw), lambda b, g: (b, 0, g)),
        out_shape=jax.ShapeDtypeStruct((bsz, s, d_inner), BF16),
        scratch_shapes=[pltpu.VMEM((2 * nh, s), F32), pltpu.VMEM((2 * nh, s), F32),
                        pltpu.VMEM((4 * nh, s), F32),
                        pltpu.VMEM((s, q), BF16), pltpu.VMEM((nc, q, 2 * q), BF16),
                        pltpu.VMEM((gw, s), BF16), pltpu.VMEM((gw, s), BF16), pltpu.VMEM((gw, s), BF16),
                        pltpu.VMEM((gw, s), F32),
                        pltpu.VMEM((gw, D_STATE), F32), pltpu.VMEM((gw, D_STATE), F32)],
        compiler_params=_cparams("parallel", "parallel"),
        name="ssd_scan",
    )(xs, bc, bc, dt_row, dt_row, p_row, p_row, dx, trio)


def _gated_proj_route_kernel(y_ref, z_ref, gw_ref, w_ref, x_ref, g_ref,
                             nw_ref, sh_ref, sc_ref, rw_ref, rb_ref, lt_ref,
                             o_ref, ld_ref, wgt_ref, cnt_ref, acc_ref):
    _gated_proj_kernel(y_ref, z_ref, gw_ref, w_ref, x_ref, g_ref, o_ref, acc_ref)
    h = _modnorm(o_ref[...], nw_ref[...], sh_ref[...], sc_ref[...])
    rows, gates, total = _route_tile(h, rw_ref[...], rb_ref[...], lt_ref[...])
    ld_ref[...] = rows
    wgt_ref[...] = gates
    cnt_ref[...] = total[:, 0:cnt_ref.shape[1]]


def _gated_proj_kernel(y_ref, z_ref, gw_ref, w_ref, x_ref, g_ref, o_ref, acc_ref):
    k = y_ref.shape[1]
    ss = jnp.zeros((y_ref.shape[0], 1), F32)
    for lo in range(0, k, PROJ_CHUNK):
        hi = lo + PROJ_CHUNK
        t = y_ref[:, lo:hi].astype(F32) * _silu(z_ref[:, lo:hi].astype(F32))
        ss = ss + jnp.sum(t * t, axis=-1, keepdims=True)
        contrib = jnp.dot((t * gw_ref[:, lo:hi]).astype(BF16), w_ref[lo:hi, :], preferred_element_type=F32)
        if lo == 0:
            acc_ref[...] = contrib
        else:
            acc_ref[...] += contrib
    o_ref[...] = x_ref[...] + g_ref[...] * (acc_ref[...] * lax.rsqrt(ss * (1.0 / k) + EPS))


def gated_proj_route(y, z, gw, w, x, g, nw, sh, sc, router_w, router_b, *, ts):
    bsz, s, d = x.shape
    k = y.shape[2]
    ne = router_w.shape[1]
    nt = s // ts
    cum = jnp.asarray(np.concatenate([np.triu(np.ones((ts, ts), np.float32)), np.ones((ts, ts), np.float32)],
                                     axis=1), BF16)
    vec = pl.BlockSpec((None, 1, d), lambda b, i: (b, 0, 0))
    tok_spec = pl.BlockSpec((TOP_K, ts), lambda b, i: (0, b * nt + i))
    call = pl.pallas_call(
        _gated_proj_route_kernel,
        grid=(bsz, nt),
        in_specs=[pl.BlockSpec((None, ts, k), lambda b, i: (b, i, 0)),
                  pl.BlockSpec((None, ts, k), lambda b, i: (b, i, 0)),
                  pl.BlockSpec((1, k), lambda b, i: (0, 0)),
                  _resident((k, d)),
                  pl.BlockSpec((None, ts, d), lambda b, i: (b, i, 0)),
                  vec,
                  pl.BlockSpec((1, d), lambda b, i: (0, 0)), vec, vec,
                  _resident((ne, d)), pl.BlockSpec((ne, 1), lambda b, i: (0, 0)), _resident((ts, 2 * ts))],
        out_specs=[pl.BlockSpec((None, ts, d), lambda b, i: (b, i, 0)), tok_spec, tok_spec,
                   pl.BlockSpec((None, ne, 128), lambda b, i: (b * nt + i, 0, 0))],
        out_shape=[jax.ShapeDtypeStruct((bsz, s, d), F32),
                   jax.ShapeDtypeStruct((TOP_K, bsz * s), jnp.int32),
                   jax.ShapeDtypeStruct((TOP_K, bsz * s), F32),
                   jax.ShapeDtypeStruct((bsz * nt, ne, 128), jnp.int32)],
        scratch_shapes=[pltpu.VMEM((ts, d), F32)],
        compiler_params=_cparams("parallel", "parallel"),
        name="ssd_out_proj_route",
    )
    x_new, ldest, wgt, cnt = call(y, z, gw.reshape(1, k), w, x, g, nw.reshape(1, d), sh, sc,
                                  router_w.T, router_b.reshape(ne, 1), cum)
    return x_new, ldest, wgt, cnt[:, :, 0]


SEG_ROWS = 16
SEG_FIELDS = 3


def _route_tile(h, rw_t, rb_col, cum):
    def nt(a, b):
        return lax.dot_general(a, b, (((1,), (1,)), ((), ())), preferred_element_type=F32)

    h_hi = h.astype(BF16)
    h_lo = (h - h_hi.astype(F32)).astype(BF16)
    rw_hi = rw_t.astype(BF16)
    rw_lo = (rw_t - rw_hi.astype(F32)).astype(BF16)
    logits = nt(rw_hi, h_hi) + (nt(rw_hi, h_lo) + nt(rw_lo, h_hi)) + rb_col
    ne, ts = logits.shape
    eid = lax.broadcasted_iota(jnp.int32, (ne, ts), 0)
    m1 = jnp.max(logits, axis=0, keepdims=True)
    i1 = jnp.min(jnp.where(logits == m1, eid, ne), axis=0, keepdims=True)
    rest = jnp.where(eid == i1, -jnp.inf, logits)
    m2 = jnp.max(rest, axis=0, keepdims=True)
    i2 = jnp.min(jnp.where(rest == m2, eid, ne), axis=0, keepdims=True)
    e2 = jnp.exp(m2 - m1)
    w1 = 1.0 / (1.0 + e2)
    w2 = e2 / (1.0 + e2)
    oh1 = (eid == i1).astype(F32)
    oh2 = (eid == i2).astype(F32)
    chosen = oh1 + oh2
    both = jnp.dot(chosen.astype(BF16), cum, preferred_element_type=F32)
    before = both[:, 0:ts] - chosen
    total = both[:, ts:].astype(jnp.int32)
    seg_len = jnp.bitwise_and(total + (SEG_ROWS - 1), -SEG_ROWS).astype(F32)
    rows = [jnp.sum(jnp.where(eid < idx, seg_len, 0.0) + onehot * before, axis=0, keepdims=True)
            for onehot, idx in ((oh1, i1), (oh2, i2))]
    return jnp.concatenate(rows, axis=0).astype(jnp.int32), jnp.concatenate([w1, w2], axis=0), total


def _segment_copies(seg_ref, tile, n_experts, make_copy, *, wait):
    for e in range(n_experts):
        base = (tile * n_experts + e) * SEG_FIELDS
        local0 = seg_ref[base]
        global0 = seg_ref[base + 1]

        def body(i, carry, local0=local0, global0=global0):
            cp = make_copy(pl.multiple_of(local0 + i * SEG_ROWS, SEG_ROWS),
                           pl.multiple_of(global0 + i * SEG_ROWS, SEG_ROWS))
            if wait:
                cp.wait()
            else:
                cp.start()
            return carry

        lax.fori_loop(0, seg_ref[base + 2], body, 0)


def _dispatch_kernel(seg_ref, x_ref, nw_ref, sh_ref, sc_ref, ld_ref, hs_ref, buf_ref, zero_ref, sem,
                     *, n_token_tiles):
    tt = x_ref.shape[0]
    lc = buf_ref.shape[1]
    ne = N_EXPERTS
    tile = pl.program_id(0) * pl.num_programs(1) + pl.program_id(1)
    last = pl.num_programs(0) * pl.num_programs(1) - 1
    slot = tile % 2

    def copies(t, sl, wait):
        def make_copy(lo, go):
            return pltpu.make_async_copy(buf_ref.at[sl, pl.ds(lo, SEG_ROWS), :],
                                         hs_ref.at[pl.ds(go, SEG_ROWS), :], sem.at[sl])
        _segment_copies(seg_ref, t, ne, make_copy, wait=wait)

    h = _modnorm(x_ref[...], nw_ref[...], sh_ref[...], sc_ref[...]).astype(BF16)
    ld = ld_ref[...]
    rows = lax.broadcasted_iota(jnp.int32, (lc, tt), 0)
    perm = jnp.where(rows == ld[0:1, :], 1.0, jnp.where(rows == ld[1:2, :], 1.0, 0.0)).astype(BF16)
    buf_ref[slot] = jnp.dot(perm, h, preferred_element_type=F32).astype(BF16)
    copies(tile, slot, wait=False)

    @pl.when(tile > 0)
    def _():
        copies(tile - 1, 1 - slot, wait=True)

    @pl.when(tile == last)
    def _():
        copies(tile, slot, wait=True)
        zero_ref[...] = jnp.zeros_like(zero_ref)
        tails = n_token_tiles * ne * SEG_FIELDS
        for wait in (False, True):
            for e in range(ne):
                start = seg_ref[tails + 2 * e]

                def body(i, carry, start=start, wait=wait):
                    cp = pltpu.make_async_copy(
                        zero_ref, hs_ref.at[pl.ds(pl.multiple_of(start + i * SEG_ROWS, SEG_ROWS), SEG_ROWS), :],
                        sem.at[2])
                    if wait:
                        cp.wait()
                    else:
                        cp.start()
                    return carry

                lax.fori_loop(0, seg_ref[tails + 2 * e + 1], body, 0)


def moe_dispatch(x, nw, sh, sc, seg, ldest_rows, n_rows, *, tt, lc):
    bsz, s, d = x.shape
    nt = s // tt
    grid_spec = pltpu.PrefetchScalarGridSpec(
        num_scalar_prefetch=1,
        grid=(bsz, nt),
        in_specs=[pl.BlockSpec((None, tt, d), lambda b, i, sref: (b, i, 0)),
                  pl.BlockSpec((1, d), lambda b, i, sref: (0, 0)),
                  pl.BlockSpec((None, 1, d), lambda b, i, sref: (b, 0, 0)),
                  pl.BlockSpec((None, 1, d), lambda b, i, sref: (b, 0, 0)),
                  pl.BlockSpec((TOP_K, tt), lambda b, i, sref: (0, b * nt + i))],
        out_specs=pl.BlockSpec(memory_space=pl.ANY),
        scratch_shapes=[pltpu.VMEM((2, lc, d), BF16), pltpu.VMEM((SEG_ROWS, d), BF16),
                        pltpu.SemaphoreType.DMA((3,))],
    )
    return pl.pallas_call(
        functools.partial(_dispatch_kernel, n_token_tiles=bsz * nt),
        grid_spec=grid_spec,
        out_shape=jax.ShapeDtypeStruct((n_rows, d), BF16),
        compiler_params=_cparams("arbitrary", "arbitrary"),
        name="moe_dispatch",
    )(seg, x, nw.reshape(1, d), sh, sc, ldest_rows)


def _moe_kernel(te_ref, nu_ref, hs_ref, w1_ref, w3_ref, w2_ref, o_ref, acc_ref):
    i = pl.program_id(0)
    f = pl.program_id(1)

    @pl.when(i < nu_ref[0])
    def _():
        @pl.when(f == 0)
        def _():
            acc_ref[...] = jnp.zeros_like(acc_ref)

        h = hs_ref[...]
        for lo in range(0, w1_ref.shape[1], MOE_HIDDEN_CHUNK):
            hi = min(lo + MOE_HIDDEN_CHUNK, w1_ref.shape[1])
            a = jnp.dot(h, w1_ref[:, lo:hi].astype(BF16), preferred_element_type=F32)
            b = jnp.dot(h, w3_ref[:, lo:hi].astype(BF16), preferred_element_type=F32)
            t = (_silu(a) * b).astype(BF16)
            acc_ref[...] += jnp.dot(t, w2_ref[lo:hi, :].astype(BF16), preferred_element_type=F32)

        @pl.when(f == pl.num_programs(1) - 1)
        def _():
            o_ref[...] = acc_ref[...].astype(o_ref.dtype)

    @pl.when((i >= nu_ref[0]) & (f == 0))
    def _():
        o_ref[...] = jnp.zeros_like(o_ref)


def moe_experts(hs, tile_expert, n_used, w1, w3, w2, *, tm, tf):
    n_rows, d = hs.shape
    dff = w1.shape[2]
    nf = dff // tf
    n_tiles = n_rows // tm

    def last_used(i, nu):
        return jnp.maximum(jnp.minimum(i, nu[0] - 1), 0)

    def row_map(i, f, te, nu):
        return (last_used(i, nu), 0)

    def hidden_block(i, f, nu):
        t = last_used(i, nu)
        step = jnp.where(i < nu[0], f, nf - 1)
        return jnp.where(t % 2 == 0, step, nf - 1 - step)

    def w_in_map(i, f, te, nu):
        return (te[last_used(i, nu)], 0, hidden_block(i, f, nu))

    def w_out_map(i, f, te, nu):
        return (te[last_used(i, nu)], hidden_block(i, f, nu), 0)

    grid_spec = pltpu.PrefetchScalarGridSpec(
        num_scalar_prefetch=2,
        grid=(n_tiles, nf),
        in_specs=[pl.BlockSpec((tm, d), row_map),
                  pl.BlockSpec((None, d, tf), w_in_map),
                  pl.BlockSpec((None, d, tf), w_in_map),
                  pl.BlockSpec((None, tf, d), w_out_map)],
        out_specs=pl.BlockSpec((tm, d), lambda i, f, te, nu: (i, 0)),
        scratch_shapes=[pltpu.VMEM((tm, d), F32)],
    )
    return pl.pallas_call(
        _moe_kernel,
        grid_spec=grid_spec,
        out_shape=jax.ShapeDtypeStruct((n_rows, d), BF16),
        compiler_params=_cparams("arbitrary", "arbitrary"),
        name="moe_experts",
    )(tile_expert, n_used, hs, w1, w3, w2)


def _combine_kernel(seg_ref, ys_ref, x_ref, g_ref, wgt_ref, ld_ref, o_ref, buf_ref, sem):
    tt = x_ref.shape[0]
    lc = buf_ref.shape[1]
    ne = N_EXPERTS
    tile = pl.program_id(0) * pl.num_programs(1) + pl.program_id(1)
    last = pl.num_programs(0) * pl.num_programs(1) - 1
    slot = tile % 2

    def copies(t, sl, wait):
        def make_copy(lo, go):
            return pltpu.make_async_copy(ys_ref.at[pl.ds(go, SEG_ROWS), :],
                                         buf_ref.at[sl, pl.ds(lo, SEG_ROWS), :], sem.at[sl])
        _segment_copies(seg_ref, t, ne, make_copy, wait=wait)

    @pl.when(tile == 0)
    def _():
        buf_ref[...] = jnp.zeros_like(buf_ref)
        copies(tile, slot, wait=False)

    @pl.when(tile < last)
    def _():
        copies(tile + 1, 1 - slot, wait=False)

    copies(tile, slot, wait=True)

    ld = ld_ref[...]
    cols = lax.broadcasted_iota(jnp.int32, (tt, lc), 1)
    pick = jnp.concatenate([jnp.where(cols == ld[:, k:k + 1], 1.0, 0.0) for k in range(TOP_K)],
                           axis=0).astype(BF16)
    z = jnp.dot(pick, buf_ref[slot], preferred_element_type=F32)
    w = wgt_ref[...]
    mix = w[:, 0:1] * z[0:tt] + w[:, 1:2] * z[tt:]
    o_ref[...] = x_ref[...] + g_ref[...] * mix


def moe_combine(ys, seg, ldest, wgt, x, g, *, tt, lc):
    bsz, s, d = x.shape
    nt = s // tt
    tok_spec = pl.BlockSpec((tt, TOP_K), lambda b, i, sref: (b * nt + i, 0))
    grid_spec = pltpu.PrefetchScalarGridSpec(
        num_scalar_prefetch=1,
        grid=(bsz, nt),
        in_specs=[pl.BlockSpec(memory_space=pl.ANY),
                  pl.BlockSpec((None, tt, d), lambda b, i, sref: (b, i, 0)),
                  pl.BlockSpec((None, 1, d), lambda b, i, sref: (b, 0, 0)),
                  tok_spec, tok_spec],
        out_specs=pl.BlockSpec((None, tt, d), lambda b, i, sref: (b, i, 0)),
        scratch_shapes=[pltpu.VMEM((2, lc, d), BF16), pltpu.SemaphoreType.DMA((2,))],
    )
    return pl.pallas_call(
        _combine_kernel,
        grid_spec=grid_spec,
        out_shape=jax.ShapeDtypeStruct((bsz, s, d), F32),
        compiler_params=_cparams("arbitrary", "arbitrary"),
        name="moe_combine",
    )(seg, ys, x, g, wgt, ldest)


def _round_up(v, m):
    return ((v + m - 1) // m) * m


def moe_token_tile(s):
    return min(512, s)


def moe_residual(x, nw, sh, sc, g, ldest, wgt, cnt, w1, w3, w2, *, tm=MOE_TILE_ROWS):
    bsz, s, d = x.shape
    n_tok = bsz * s
    ne = w1.shape[0]
    tt = moe_token_tile(s)
    n_tt = n_tok // tt
    lc = _round_up(TOP_K * tt + ne * SEG_ROWS, 128)
    seg_len = _round_up(cnt.reshape(n_tt, ne), SEG_ROWS)
    local_start = jnp.cumsum(seg_len, axis=1) - seg_len
    padded = _round_up(jnp.sum(seg_len, axis=0), tm)
    ends = jnp.cumsum(padded)
    global_start = (ends - padded)[None, :] + jnp.cumsum(seg_len, axis=0) - seg_len
    n_rows = _round_up(n_tok * TOP_K + n_tt * ne * SEG_ROWS + ne * tm, tm)
    used_end = (ends - padded) + jnp.sum(seg_len, axis=0)
    next_start = jnp.concatenate([ends[:-1], jnp.full((1,), n_rows, ends.dtype)])
    tails = jnp.stack([used_end, (next_start - used_end) // SEG_ROWS], axis=-1)
    seg = jnp.concatenate([jnp.stack([local_start, global_start, seg_len // SEG_ROWS], axis=-1).reshape(-1),
                           tails.reshape(-1)]).astype(jnp.int32)
    n_tiles = n_rows // tm
    tile_start = jnp.arange(n_tiles, dtype=jnp.int32) * tm
    tile_expert = jnp.minimum(jnp.sum(tile_start[:, None] >= ends[None, :], axis=1), ne - 1).astype(jnp.int32)
    n_used = (ends[ne - 1:ne] // tm).astype(jnp.int32)
    hs = moe_dispatch(x, nw, sh, sc, seg, ldest, n_rows, tt=tt, lc=lc)
    ys = moe_experts(hs, tile_expert, n_used, w1, w3, w2, tm=tm, tf=w1.shape[2] // 2)
    return moe_combine(ys, seg, ldest.T, wgt.T, x, g, tt=tt, lc=lc)


def _split_mod(mod):
    return [mod[k] for k in range(mod.shape[0])]


def even_layer(x, c, rel_bias, ada_w, ada_b, norm1_w, in_w, q_norm_w, k_norm_w, sink, out_w,
               norm2_w, w1, w3, w2):
    s = x.shape[1]
    sh1, sc1, g1, sh2, sc2, g2 = _split_mod(ada_mod(c, ada_w, ada_b))
    proj = norm_mod_matmul(x, norm1_w, sh1, sc1, in_w.astype(BF16), ts=min(1024, s), tn=256, name="even_in_proj",
                           out_dtype=BF16)
    yf = fourier_mix(proj, tq=min(512, s))
    ya = window_attention(proj, band_bias(rel_bias), q_norm_w, k_norm_w, sink)
    x = cat_proj_residual(yf, ya, out_w.astype(BF16), x, g1, ts=min(1024, s))
    return ffn_residual(x, norm2_w, sh2, sc2, g2, w1.astype(BF16), w3.astype(BF16), w2,
                        ts=min(1024, s), tf=256)


def odd_layer(x, c, ada_w, ada_b, norm1_w, in_w, conv_w, conv_b, dt_bias_f, dt_bias_b, a_log_f, a_log_b,
              d_skip, gnorm_w, out_w, norm2_w, router_w, router_b, w1, w3, w2):
    s = x.shape[1]
    sh1, sc1, g1, sh2, sc2, g2 = _split_mod(ada_mod(c, ada_w, ada_b))
    d_inner = gnorm_w.shape[0]
    z, xs, bc, dt = ssm_in_proj_conv(x, norm1_w, sh1, sc1, in_w.astype(BF16), conv_w, conv_b,
                                     dz=d_inner, n_x=d_inner, ts=min(512, s))
    y = ssd_scan_bidir(xs, bc, dt, dt_bias_f, dt_bias_b, a_log_f, a_log_b, d_skip)
    x, ldest, wgt, cnt = gated_proj_route(y, z, gnorm_w, out_w.astype(BF16), x, g1, norm2_w, sh2, sc2,
                                          router_w, router_b, ts=moe_token_tile(s))
    return moe_residual(x, norm2_w, sh2, sc2, g2, ldest, wgt, cnt,
                        w1, w3, w2)


def kernel(x, c, rel_bias, ev_ada_w, ev_ada_b, ev_norm1_w, ev_in_w, ev_q_norm_w, ev_k_norm_w, ev_sink, ev_out_w, ev_norm2_w, ev_ffn_w1, ev_ffn_w3, ev_ffn_w2, od_ada_w, od_ada_b, od_norm1_w, od_in_w, od_conv_w, od_conv_b, od_dt_bias_f, od_dt_bias_b, od_A_log_f, od_A_log_b, od_D, od_gnorm_w, od_out_w, od_norm2_w, od_router_w, od_router_b, od_moe_w1, od_moe_w3, od_moe_w2):
    depth = ev_ada_w.shape[0] + od_ada_w.shape[0]
    for i in range(depth):
        j = i // 2
        if i % 2 == 0:
            x = even_layer(x, c, rel_bias, ev_ada_w[j], ev_ada_b[j], ev_norm1_w[j], ev_in_w[j],
                           ev_q_norm_w[j], ev_k_norm_w[j], ev_sink[j], ev_out_w[j], ev_norm2_w[j],
                           ev_ffn_w1[j], ev_ffn_w3[j], ev_ffn_w2[j])
        else:
            x = odd_layer(x, c, od_ada_w[j], od_ada_b[j], od_norm1_w[j], od_in_w[j], od_conv_w[j],
                          od_conv_b[j], od_dt_bias_f[j], od_dt_bias_b[j], od_A_log_f[j], od_A_log_b[j],
                          od_D[j], od_gnorm_w[j], od_out_w[j], od_norm2_w[j], od_router_w[j],
                          od_router_b[j], od_moe_w1[j], od_moe_w3[j], od_moe_w2[j])
    return x
```

```python
import functools

import numpy as np
import jax
import jax.numpy as jnp
from jax import lax
from jax.experimental import pallas as pl
from jax.experimental.pallas import tpu as pltpu

F32 = jnp.float32
BF16 = jnp.bfloat16

EPS = 1e-6
FNET_GROUPS = 4
FNET_GROUP_DIM = 128
FNET_WIDTH = FNET_GROUPS * FNET_GROUP_DIM
ATTN_HEADS = 8
ATTN_KV_HEADS = 2
HEAD_DIM = 64
ATTN_WIDTH = ATTN_HEADS * HEAD_DIM
KV_WIDTH = ATTN_KV_HEADS * HEAD_DIM
WINDOW = 128
BLOCK = 128
REL_BUCKETS = 32
REL_MAX_DIST = 128
SSM_HEAD_DIM = 64
SSM_GROUPS = 4
D_STATE = 128
CONV_WIDTH = 5
SSD_CHUNK = 128
N_EXPERTS = 8
TOP_K = 2
NEG_BIG = -1e30
ATTN_QBLOCKS = 8

V7X_VMEM_LIMIT_BYTES = 56 * 1024 * 1024
MOE_TILE_ROWS = 768
MOE_HIDDEN_CHUNK = 256


def _cparams(*sem):
    return pltpu.CompilerParams(dimension_semantics=sem, vmem_limit_bytes=V7X_VMEM_LIMIT_BYTES)


def _modnorm(x, nw, sh, sc):
    ms = jnp.mean(x * x, axis=-1, keepdims=True)
    return x * lax.rsqrt(ms + EPS) * nw * (1.0 + sc) + sh


def _silu(x):
    return x * (1.0 / (1.0 + jnp.exp(-x)))


def _ada_kernel(c_ref, w_ref, b_ref, o_ref):
    cs = _silu(c_ref[...]).astype(BF16)
    val = jnp.dot(cs, w_ref[...].astype(BF16), preferred_element_type=F32) + b_ref[...]
    for row in range(val.shape[0]):
        o_ref[row] = val[row:row + 1, :]


def ada_mod(c, w, b, *, n_chunks=6):
    bsz, d = c.shape
    n = w.shape[1]
    tn = n // n_chunks
    return pl.pallas_call(
        _ada_kernel,
        grid=(n_chunks,),
        in_specs=[pl.BlockSpec((bsz, d), lambda j: (0, 0)),
                  pl.BlockSpec((d, tn), lambda j: (0, j)),
                  pl.BlockSpec((1, tn), lambda j: (0, j))],
        out_specs=pl.BlockSpec((None, bsz, 1, tn), lambda j: (j, 0, 0, 0)),
        out_shape=jax.ShapeDtypeStruct((n_chunks, bsz, 1, tn), F32),
        compiler_params=_cparams("arbitrary"),
        name="ada_mod",
    )(c, w, b.reshape(1, n))


def _nmm_kernel(x_ref, nw_ref, sh_ref, sc_ref, w_ref, o_ref, *, tn):
    h = _modnorm(x_ref[...], nw_ref[...], sh_ref[...], sc_ref[...]).astype(BF16)
    n = w_ref.shape[1]
    for lo in range(0, n, tn):
        o_ref[:, lo:lo + tn] = jnp.dot(h, w_ref[:, lo:lo + tn],
                                       preferred_element_type=F32).astype(o_ref.dtype)


def _resident(shape):
    return pl.BlockSpec(shape, lambda *_: tuple(0 for _ in shape), pipeline_mode=pl.Buffered(1))


def norm_mod_matmul(x, nw, sh, sc, w, *, ts, tn, name, out_dtype=F32):
    bsz, s, d = x.shape
    n = w.shape[1]
    assert n % tn == 0
    return pl.pallas_call(
        functools.partial(_nmm_kernel, tn=tn),
        grid=(bsz, s // ts),
        in_specs=[pl.BlockSpec((None, ts, d), lambda b, i: (b, i, 0)),
                  pl.BlockSpec((1, d), lambda b, i: (0, 0)),
                  pl.BlockSpec((None, 1, d), lambda b, i: (b, 0, 0)),
                  pl.BlockSpec((None, 1, d), lambda b, i: (b, 0, 0)),
                  _resident((d, n))],
        out_specs=pl.BlockSpec((None, ts, n), lambda b, i: (b, i, 0)),
        out_shape=jax.ShapeDtypeStruct((bsz, s, n), out_dtype),
        compiler_params=_cparams("parallel", "parallel"),
        name=name,
    )(x, nw.reshape(1, d), sh, sc, w)


def _dft_cos_sin(n):
    k = np.arange(n, dtype=np.int64)
    ang = ((k[:, None] * k[None, :]) % n).astype(np.float64) * (2.0 * np.pi / n)
    scale = 1.0 / np.sqrt(n)
    return np.cos(ang) * scale, np.sin(ang) * scale


def _fourier_kernel(u_ref, chan_ref, seq_ref, o_ref, ab_ref):
    s = u_ref.shape[0]

    @pl.when(pl.program_id(1) == 0)
    def _():
        for g in range(FNET_GROUPS):
            lo, hi = g * FNET_GROUP_DIM, (g + 1) * FNET_GROUP_DIM
            ug = u_ref[:, lo:hi].astype(BF16)
            cs = jnp.dot(ug, chan_ref[...], preferred_element_type=F32)
            ab_ref[0:s, lo:hi] = cs[:, :FNET_GROUP_DIM].astype(BF16)
            ab_ref[s:2 * s, lo:hi] = cs[:, FNET_GROUP_DIM:].astype(BF16)

    o_ref[...] = jnp.dot(seq_ref[...], ab_ref[...], preferred_element_type=F32).astype(o_ref.dtype)


def fourier_mix(proj, *, tq):
    bsz, s, _ = proj.shape
    cc, sc = _dft_cos_sin(FNET_GROUP_DIM)
    chan = jnp.asarray(np.concatenate([cc, sc], axis=1), BF16)
    cs, ss = _dft_cos_sin(s)
    seq = jnp.asarray(np.concatenate([cs, -ss], axis=1), BF16)
    return pl.pallas_call(
        _fourier_kernel,
        grid=(bsz, s // tq),
        in_specs=[pl.BlockSpec((None, s, FNET_WIDTH), lambda b, i: (b, 0, 0)),
                  pl.BlockSpec((FNET_GROUP_DIM, 2 * FNET_GROUP_DIM), lambda b, i: (0, 0)),
                  pl.BlockSpec((tq, 2 * s), lambda b, i: (i, 0))],
        out_specs=pl.BlockSpec((None, tq, FNET_WIDTH), lambda b, i: (b, i, 0)),
        out_shape=jax.ShapeDtypeStruct((bsz, s, FNET_WIDTH), BF16),
        scratch_shapes=[pltpu.VMEM((2 * s, FNET_WIDTH), BF16)],
        compiler_params=_cparams("parallel", "arbitrary"),
        name="fourier_mix",
    )(proj, chan, seq)


def _band_bucket_table():
    i = np.arange(BLOCK)[:, None]
    j = np.arange(3 * BLOCK)[None, :]
    rel = (j - BLOCK) - i
    half = REL_BUCKETS // 2
    max_exact = half // 2
    n = np.abs(rel)
    large = max_exact + (np.log(np.maximum(n, 1) / max_exact)
                         / np.log(REL_MAX_DIST / max_exact) * (half - max_exact)).astype(np.int32)
    large = np.minimum(large, half - 1)
    bucket = (rel > 0).astype(np.int32) * half + np.where(n < max_exact, n, large)
    return np.where(n <= WINDOW, bucket, -1).astype(np.int32)


def _bias_kernel(rb_ref, bucket_ref, o_ref):
    h = pl.program_id(0)
    bucket = bucket_ref[...]
    acc = jnp.full(bucket.shape, NEG_BIG, F32)
    for bkt in range(REL_BUCKETS):
        acc = jnp.where(bucket == bkt, rb_ref[bkt * ATTN_HEADS + h], acc)
    o_ref[...] = acc


def band_bias(rel_bias):
    bucket = jnp.asarray(_band_bucket_table())
    return pl.pallas_call(
        _bias_kernel,
        grid=(ATTN_HEADS,),
        in_specs=[pl.BlockSpec(memory_space=pltpu.SMEM),
                  pl.BlockSpec((BLOCK, 3 * BLOCK), lambda h: (0, 0))],
        out_specs=pl.BlockSpec((None, BLOCK, 3 * BLOCK), lambda h: (h, 0, 0)),
        out_shape=jax.ShapeDtypeStruct((ATTN_HEADS, BLOCK, 3 * BLOCK), F32),
        compiler_params=_cparams("arbitrary"),
        name="band_bias",
    )(rel_bias.reshape(-1), bucket)


def _head_mean_matrix(width):
    m = np.zeros((width, width), np.float32)
    for h in range(width // HEAD_DIM):
        m[h * HEAD_DIM:(h + 1) * HEAD_DIM, h * HEAD_DIM:(h + 1) * HEAD_DIM] = 1.0 / HEAD_DIM
    return m


def _heads_rms(t, mean_mat, w):
    sq = t * t
    hi = sq.astype(BF16)
    lo = (sq - hi.astype(F32)).astype(BF16)
    ms = (jnp.dot(hi, mean_mat, preferred_element_type=F32)
          + jnp.dot(lo, mean_mat, preferred_element_type=F32))
    return t * lax.rsqrt(ms + EPS) * w


def _attn_kernel(sink_ref, q_ref, kl_ref, kc_ref, kr_ref, vl_ref, vc_ref, vr_ref,
                 bias_ref, qnw_ref, knw_ref, qmean_ref, kmean_ref, o_ref):
    n = pl.program_id(1)
    nb = pl.num_programs(1) * ATTN_QBLOCKS
    k = jnp.concatenate([kl_ref[...], kc_ref[...], kr_ref[...]], axis=0).astype(F32)
    v = jnp.concatenate([vl_ref[...], vc_ref[...], vr_ref[...]], axis=0).astype(F32)
    col = lax.broadcasted_iota(jnp.int32, (1, 3 * BLOCK), 1)
    qn = _heads_rms(q_ref[...].astype(F32), qmean_ref[...], qnw_ref[...])
    kn = _heads_rms(k, kmean_ref[...], knw_ref[...])
    low = lax.broadcasted_iota(jnp.int32, (1, 2 * HEAD_DIM), 1) < HEAD_DIM
    kn_sw = pltpu.roll(kn, HEAD_DIM, axis=1)
    v_sw = pltpu.roll(v, HEAD_DIM, axis=1)
    k_dup = [jnp.where(low, kn, kn_sw).astype(BF16), jnp.where(low, kn_sw, kn).astype(BF16)]
    ones = jnp.ones((k.shape[0], 2 * HEAD_DIM), BF16)
    v_ext = [jnp.concatenate([v.astype(BF16), ones], axis=1),
             jnp.concatenate([v_sw.astype(BF16), ones], axis=1)]
    g = ATTN_HEADS // ATTN_KV_HEADS
    for qb in range(ATTN_QBLOCKS):
        blk = n * ATTN_QBLOCKS + qb
        band = slice(qb * BLOCK, (qb + 3) * BLOCK)
        first_key = jnp.where(blk == 0, BLOCK, 0)
        end_key = jnp.where(blk == nb - 1, 2 * BLOCK, 3 * BLOCK)
        outside = (col < first_key) | (col >= end_key)
        raw = []
        for h in range(ATTN_HEADS):
            m, idx, j = h // 2, h % 2, h // g
            qp = qn[qb * BLOCK:(qb + 1) * BLOCK, m * 2 * HEAD_DIM:(m + 1) * 2 * HEAD_DIM]
            qm = jnp.where(low if idx == 0 else jnp.logical_not(low), qp, 0.0).astype(BF16)
            raw.append(lax.dot_general(qm, k_dup[j][band], (((1,), (1,)), ((), ())),
                                       preferred_element_type=F32))
        probs, tails = [], []
        for h in range(ATTN_HEADS):
            logits = jnp.where(outside, NEG_BIG, raw[h] + bias_ref[h])
            sk = sink_ref[h]
            mx = jnp.maximum(jnp.max(logits, axis=-1, keepdims=True), sk)
            probs.append(jnp.exp(logits - mx).astype(BF16))
            tails.append(jnp.exp(sk - mx))
        res = []
        for h in range(ATTN_HEADS):
            idx, j = h % 2, h // g
            r = jnp.dot(probs[h], v_ext[idx if j == 0 else 1 - idx][band], preferred_element_type=F32)
            res.append(r[:, :2 * HEAD_DIM] / (r[:, 2 * HEAD_DIM:] + tails[h]))
        pairs = [jnp.where(low, res[2 * m], res[2 * m + 1]) for m in range(ATTN_HEADS // 2)]
        o_ref[qb * BLOCK:(qb + 1) * BLOCK, :] = jnp.concatenate(pairs, axis=-1).astype(o_ref.dtype)


def window_attention(proj, bias, q_norm_w, k_norm_w, sink):
    bsz, s, _ = proj.shape
    nb = s // BLOCK
    qcol = FNET_WIDTH // ATTN_WIDTH
    kcol = (FNET_WIDTH + ATTN_WIDTH) // KV_WIDTH
    vcol = kcol + 1

    qb = ATTN_QBLOCKS
    assert nb % qb == 0

    def kv_specs(col):
        return [pl.BlockSpec((None, BLOCK, KV_WIDTH), lambda b, n: (b, jnp.maximum(n * qb - 1, 0), col)),
                pl.BlockSpec((None, qb * BLOCK, KV_WIDTH), lambda b, n: (b, n, col)),
                pl.BlockSpec((None, BLOCK, KV_WIDTH), lambda b, n: (b, jnp.minimum((n + 1) * qb, nb - 1), col))]

    return pl.pallas_call(
        _attn_kernel,
        grid=(bsz, nb // qb),
        in_specs=[pl.BlockSpec(memory_space=pltpu.SMEM),
                  pl.BlockSpec((None, qb * BLOCK, ATTN_WIDTH), lambda b, n: (b, n, qcol)),
                  *kv_specs(kcol), *kv_specs(vcol),
                  pl.BlockSpec((ATTN_HEADS, BLOCK, 3 * BLOCK), lambda b, n: (0, 0, 0)),
                  pl.BlockSpec((1, ATTN_WIDTH), lambda b, n: (0, 0)),
                  pl.BlockSpec((1, KV_WIDTH), lambda b, n: (0, 0)),
                  pl.BlockSpec((ATTN_WIDTH, ATTN_WIDTH), lambda b, n: (0, 0)),
                  pl.BlockSpec((KV_WIDTH, KV_WIDTH), lambda b, n: (0, 0))],
        out_specs=pl.BlockSpec((None, qb * BLOCK, ATTN_WIDTH), lambda b, n: (b, n, 0)),
        out_shape=jax.ShapeDtypeStruct((bsz, s, ATTN_WIDTH), BF16),
        compiler_params=_cparams("parallel", "arbitrary"),
        name="window_attention",
    )(sink, proj, proj, proj, proj, proj, proj, proj, bias,
      (jnp.tile(q_norm_w, ATTN_HEADS) * (HEAD_DIM ** -0.5)).reshape(1, ATTN_WIDTH),
      jnp.tile(k_norm_w, ATTN_KV_HEADS).reshape(1, KV_WIDTH),
      jnp.asarray(_head_mean_matrix(ATTN_WIDTH), BF16), jnp.asarray(_head_mean_matrix(KV_WIDTH), BF16))


def _cat_proj_kernel(a1_ref, a2_ref, w_ref, x_ref, g_ref, o_ref):
    k1 = a1_ref.shape[1]
    y = jnp.dot(a1_ref[...].astype(BF16), w_ref[0:k1, :], preferred_element_type=F32)
    y = y + jnp.dot(a2_ref[...].astype(BF16), w_ref[k1:, :], preferred_element_type=F32)
    o_ref[...] = x_ref[...] + g_ref[...] * y


def cat_proj_residual(a1, a2, w, x, g, *, ts):
    bsz, s, d = x.shape
    k1, k2 = a1.shape[2], a2.shape[2]
    return pl.pallas_call(
        _cat_proj_kernel,
        grid=(bsz, s // ts),
        in_specs=[pl.BlockSpec((None, ts, k1), lambda b, i: (b, i, 0)),
                  pl.BlockSpec((None, ts, k2), lambda b, i: (b, i, 0)),
                  pl.BlockSpec((k1 + k2, d), lambda b, i: (0, 0)),
                  pl.BlockSpec((None, ts, d), lambda b, i: (b, i, 0)),
                  pl.BlockSpec((None, 1, d), lambda b, i: (b, 0, 0))],
        out_specs=pl.BlockSpec((None, ts, d), lambda b, i: (b, i, 0)),
        out_shape=jax.ShapeDtypeStruct((bsz, s, d), F32),
        compiler_params=_cparams("parallel", "parallel"),
        name="mixer_out_proj",
    )(a1, a2, w, x, g)


def _ffn_kernel(x_ref, nw_ref, sh_ref, sc_ref, g_ref, w1_ref, w3_ref, w2_ref, o_ref, acc_ref, *, tf):
    h = _modnorm(x_ref[...], nw_ref[...], sh_ref[...], sc_ref[...]).astype(BF16)
    dff = w1_ref.shape[1]
    for lo in range(0, dff, tf):
        a = jnp.dot(h, w1_ref[:, lo:lo + tf], preferred_element_type=F32)
        b = jnp.dot(h, w3_ref[:, lo:lo + tf], preferred_element_type=F32)
        t = (_silu(a) * b).astype(BF16)
        contrib = jnp.dot(t, w2_ref[lo:lo + tf, :].astype(BF16), preferred_element_type=F32)
        if lo == 0:
            acc_ref[...] = contrib
        else:
            acc_ref[...] += contrib
    o_ref[...] = x_ref[...] + g_ref[...] * acc_ref[...]


def ffn_residual(x, nw, sh, sc, g, w1, w3, w2, *, ts, tf):
    bsz, s, d = x.shape
    dff = w1.shape[1]
    assert dff % tf == 0
    vec = pl.BlockSpec((None, 1, d), lambda b, i: (b, 0, 0))
    return pl.pallas_call(
        functools.partial(_ffn_kernel, tf=tf),
        grid=(bsz, s // ts),
        in_specs=[pl.BlockSpec((None, ts, d), lambda b, i: (b, i, 0)),
                  pl.BlockSpec((1, d), lambda b, i: (0, 0)),
                  vec, vec, vec,
                  _resident((d, dff)), _resident((d, dff)), _resident((dff, d))],
        out_specs=pl.BlockSpec((None, ts, d), lambda b, i: (b, i, 0)),
        out_shape=jax.ShapeDtypeStruct((bsz, s, d), F32),
        scratch_shapes=[pltpu.VMEM((ts, d), F32)],
        compiler_params=_cparams("parallel", "parallel"),
        name="ffn_swiglu",
    )(x, nw.reshape(1, d), sh, sc, g, w1, w3, w2)


HALO_ROWS = 16
PROJ_CHUNK = 256
W_VIEW = 1024


def _proj_conv_kernel(x_ref, xp_ref, xn_ref, nw_ref, sh_ref, sc_ref, *rest, n_views):
    w_views = rest[:n_views]
    wd_ref, cw_ref, cb_ref, z_ref, xs_ref, bc_ref, dt_ref = rest[n_views:]
    i = pl.program_id(1)
    ts = x_ref.shape[0]
    half = CONV_WIDTH // 2
    dz = z_ref.shape[1]
    dc = cw_ref.shape[1]

    def wcols(c0):
        return w_views[c0 // W_VIEW][:, c0 % W_VIEW:c0 % W_VIEW + PROJ_CHUNK]

    def hnorm(ref):
        return _modnorm(ref[...], nw_ref[...], sh_ref[...], sc_ref[...])

    h = hnorm(x_ref).astype(BF16)
    h_prev = jnp.where(i == 0, 0.0, hnorm(xp_ref)).astype(BF16)
    h_next = jnp.where(i == pl.num_programs(1) - 1, 0.0, hnorm(xn_ref)).astype(BF16)
    h_ext = jnp.concatenate([h_prev, h, h_next], axis=0)

    n_x = xs_ref.shape[1]
    rows = h_ext.shape[0]

    def project(lo):
        return jnp.dot(h_ext, wcols(dz + lo), preferred_element_type=F32)

    def z_chunk(lo):
        hi = lo + PROJ_CHUNK
        z_ref[:, lo:hi] = jnp.dot(h, wcols(lo), preferred_element_type=F32).astype(z_ref.dtype)

    z_los = list(range(0, dz, PROJ_CHUNK))
    c_los = list(range(0, dc, PROJ_CHUNK))
    p_next = project(c_los[0])
    for n, lo in enumerate(c_los):
        hi = lo + PROJ_CHUNK
        p = p_next
        if n + 1 < len(c_los):
            p_next = project(c_los[n + 1])
        if z_los:
            z_chunk(z_los.pop(0))
        acc = jnp.zeros((ts, PROJ_CHUNK), F32) + cb_ref[:, lo:hi]
        for kk in range(CONV_WIDTH):
            shifted = p if kk == half else pltpu.roll(p, (half - kk) % rows, axis=0)
            acc = acc + shifted[HALO_ROWS:HALO_ROWS + ts, :] * cw_ref[kk:kk + 1, lo:hi]
        out = _silu(acc)
        if lo < n_x:
            xs_ref[:, lo:hi] = out
        else:
            bc_ref[:, lo - n_x:hi - n_x] = out.astype(bc_ref.dtype)
    for lo in z_los:
        z_chunk(lo)
    dt_ref[...] = lax.dot_general(wd_ref[...], h, (((1,), (1,)), ((), ())), preferred_element_type=F32)


def ssm_in_proj_conv(x, nw, sh, sc, w_in, conv_w, conv_b, *, dz, n_x, ts):
    bsz, s, d = x.shape
    dc = conv_w.shape[1]
    ddt = w_in.shape[1] - dz - dc
    assert ts % HALO_ROWS == 0 and n_x % PROJ_CHUNK == 0 and dc % PROJ_CHUNK == 0 and dz % PROJ_CHUNK == 0
    assert dz % W_VIEW == 0 and dc % W_VIEW == 0 and W_VIEW % PROJ_CHUNK == 0
    n_views = (dz + dc) // W_VIEW
    views = [pl.BlockSpec((d, W_VIEW), lambda b, i, k=k: (0, k), pipeline_mode=pl.Buffered(1))
             for k in range(n_views)]
    r = ts // HALO_ROWS
    last = s // HALO_ROWS - 1
    vec = pl.BlockSpec((None, 1, d), lambda b, i: (b, 0, 0))
    return pl.pallas_call(
        functools.partial(_proj_conv_kernel, n_views=n_views),
        grid=(bsz, s // ts),
        in_specs=[pl.BlockSpec((None, ts, d), lambda b, i: (b, i, 0)),
                  pl.BlockSpec((None, HALO_ROWS, d), lambda b, i: (b, jnp.maximum(i * r - 1, 0), 0)),
                  pl.BlockSpec((None, HALO_ROWS, d), lambda b, i: (b, jnp.minimum((i + 1) * r, last), 0)),
                  pl.BlockSpec((1, d), lambda b, i: (0, 0)), vec, vec,
                  *views, _resident((ddt, d)),
                  _resident((CONV_WIDTH, dc)), _resident((1, dc))],
        out_specs=[pl.BlockSpec((None, ts, dz), lambda b, i: (b, i, 0)),
                   pl.BlockSpec((None, ts, n_x), lambda b, i: (b, i, 0)),
                   pl.BlockSpec((None, ts, dc - n_x), lambda b, i: (b, i, 0)),
                   pl.BlockSpec((None, ddt, ts), lambda b, i: (b, 0, i))],
        out_shape=[jax.ShapeDtypeStruct((bsz, s, dz), BF16),
                   jax.ShapeDtypeStruct((bsz, s, n_x), F32),
                   jax.ShapeDtypeStruct((bsz, s, dc - n_x), BF16),
                   jax.ShapeDtypeStruct((bsz, ddt, s), F32)],
        compiler_params=_cparams("parallel", "parallel"),
        name="odd_in_proj_conv",
    )(x, x, x, nw.reshape(1, d), sh, sc, *([w_in] * n_views), w_in[:, dz + dc:].T, conv_w, conv_b.reshape(1, dc))


def _softplus(x):
    return jnp.maximum(x, 0.0) + jnp.log(1.0 + jnp.exp(-jnp.abs(x)))


LOG2E = 1.4426950408889634
DECAY_SLOTS = 12


def _bf16_parts3(v):
    hi = v.astype(BF16).astype(F32)
    r = v - hi
    mid = r.astype(BF16).astype(F32)
    lo = (r - mid).astype(BF16).astype(F32)
    return hi, mid, lo


def _ssd_t_kernel(x_ref, b_ref, c_ref, dt_f_ref, dt_b_ref, p_f_ref, p_b_ref, dx_ref, trio_ref, y_ref,
                  ar_ref, dr_ref, sc_ref, pq_ref, qt_ref, xtb_ref, xdf_ref, xdb_ref, yt_ref, hf_ref, hb_ref):
    s = x_ref.shape[0]
    q = SSD_CHUNK
    nh = p_f_ref.shape[0]
    nc = s // q
    hd = SSM_HEAD_DIM

    def row_params(raw_ref, p_ref):
        dt = _softplus(raw_ref[...] + p_ref[:, 0:1])
        return dt, (-LOG2E) * jnp.exp(p_ref[:, 1:2]) * dt

    dtf, af = row_params(dt_f_ref, p_f_ref)
    dtb, ab = row_params(dt_b_ref, p_b_ref)
    dr_ref[0:nh, :] = dtf
    dr_ref[nh:, :] = dtb
    ar_ref[0:nh, :] = af
    ar_ref[nh:, :] = ab

    li = lax.broadcasted_iota(jnp.int32, (q, q), 0)
    si = lax.broadcasted_iota(jnp.int32, (q, q), 1)
    lower = li >= si
    upper = li <= si
    slot_head = lax.broadcasted_iota(jnp.int32, (1, q), 1) % nh

    def bdot(a, b):
        return jnp.dot(a, b, preferred_element_type=F32)

    def ntdot(a, b):
        return lax.dot_general(a, b, (((1,), (1,)), ((), ())), preferred_element_type=F32)

    def head_rows(v):
        return jnp.concatenate([jnp.broadcast_to(v[h:h + 1, :], (hd, q)) for h in range(nh)], axis=0)

    ones = jnp.ones((nh, q), F32)
    zeros = jnp.zeros((nh, q), F32)

    def prep_body(c, carry):
        sl = pl.ds(pl.multiple_of(c * q, q), q)
        a_row = ar_ref[:, sl]
        d_row = dr_ref[:, sl]
        parts = jnp.concatenate([p.astype(BF16) for p in _bf16_parts3(a_row)], axis=0)
        cs = bdot(parts, trio_ref[...])
        cs = cs[0:2 * nh] + cs[2 * nh:4 * nh] + cs[4 * nh:6 * nh]
        i_f = cs[0:nh, 0:q]
        e_b = cs[nh:, 0:q] - a_row[nh:]
        tot_f = cs[0:nh, q:]
        tot_b = cs[nh:, q:]
        ih, im, il = _bf16_parts3(i_f)
        eh, em, el = _bf16_parts3(e_b)
        pad = [zeros] * (q // nh - DECAY_SLOTS)
        p_t = jnp.concatenate([ih, im, il, ones, ones, ones, -eh, -em, -el, ones, ones, ones] + pad, axis=0)
        qf_t = jnp.concatenate([ones, ones, ones, -ih, -im, -il] + [zeros] * 6 + pad, axis=0)
        qb_t = jnp.concatenate([zeros] * 6 + [ones, ones, ones, eh, em, el] + pad, axis=0)
        pq_ref[sl, :] = p_t.T.astype(BF16)
        qt_ref[c] = jnp.concatenate([qf_t, qb_t], axis=1).astype(BF16)
        sc_ref[0 * nh:1 * nh, sl] = jnp.exp2(i_f)
        sc_ref[1 * nh:2 * nh, sl] = jnp.exp2(tot_b - e_b)
        sc_ref[2 * nh:3 * nh, sl] = jnp.exp2(tot_f)
        sc_ref[3 * nh:4 * nh, sl] = jnp.exp2(tot_b)
        xt = x_ref[sl, :].T
        xtb_ref[:, sl] = xt.astype(BF16)
        xdf_ref[:, sl] = (xt * head_rows(jnp.exp2(tot_f - i_f) * d_row[0:nh])).astype(BF16)
        xdb_ref[:, sl] = (xt * head_rows(jnp.exp2(e_b) * d_row[nh:])).astype(BF16)
        return carry

    lax.fori_loop(0, nc, prep_body, 0, unroll=8)

    hf_ref[...] = jnp.zeros_like(hf_ref)
    hb_ref[...] = jnp.zeros_like(hb_ref)
    zero_half = jnp.zeros((hd, q), BF16)

    def fwd_body(c, carry):
        sl = pl.ds(pl.multiple_of(c * q, q), q)
        bc = b_ref[sl, :].astype(BF16)
        cc = c_ref[sl, :].astype(BF16)
        d_row = dr_ref[:, sl]
        p_all = pq_ref[sl, :]
        q_t = qt_ref[c]
        xtb = xtb_ref[:, sl]
        cb = ntdot(cc, bc)
        g2s = [bdot(jnp.where(slot_head == h, p_all, jnp.zeros_like(p_all)), q_t) for h in range(nh)]
        ms = []
        for h in range(nh):
            arg = jnp.where(lower, g2s[h][:, 0:q], g2s[h][:, q:])
            wgt = (jnp.where(lower, d_row[h:h + 1, :], 0.0)
                   + jnp.where(upper, d_row[nh + h:nh + h + 1, :], 0.0))
            ms.append((cb * jnp.exp2(arg) * wgt).astype(BF16))
        yd = []
        for h0 in range(0, nh, 2):
            lhs = jnp.concatenate(
                [jnp.concatenate([xtb[h0 * hd:(h0 + 1) * hd], zero_half], axis=0),
                 jnp.concatenate([zero_half, xtb[(h0 + 1) * hd:(h0 + 2) * hd]], axis=0)], axis=1)
            yd.append(ntdot(lhs, jnp.concatenate(ms[h0:h0 + 2], axis=1)))
        states = bdot(xdf_ref[:, sl], bc)
        h_prev = hf_ref[...]
        y_off = ntdot(h_prev.astype(BF16), cc) * head_rows(sc_ref[0 * nh:1 * nh, sl])
        hf_ref[...] = h_prev * head_rows(sc_ref[2 * nh:3 * nh, sl]) + states
        yt_ref[:, sl] = jnp.concatenate(yd, axis=0) + y_off
        return carry

    lax.fori_loop(0, nc, fwd_body, 0, unroll=4)

    def bwd_body(t, carry):
        c = nc - 1 - t
        sl = pl.ds(pl.multiple_of(c * q, q), q)
        bc = b_ref[sl, :].astype(BF16)
        cc = c_ref[sl, :].astype(BF16)
        states = bdot(xdb_ref[:, sl], bc)
        h_prev = hb_ref[...]
        y_off = ntdot(h_prev.astype(BF16), cc) * head_rows(sc_ref[1 * nh:2 * nh, sl])
        hb_ref[...] = h_prev * head_rows(sc_ref[3 * nh:4 * nh, sl]) + states
        y_ref[sl, :] = ((yt_ref[:, sl] + y_off).T + dx_ref[...] * x_ref[sl, :]).astype(y_ref.dtype)
        return carry

    lax.fori_loop(0, nc, bwd_body, 0, unroll=8)


def ssd_scan_bidir(xs, bc, dt, dt_bias_f, dt_bias_b, a_log_f, a_log_b, d_skip):
    bsz, s, _ = xs.shape
    nheads = dt.shape[1] // 2
    nh = nheads // SSM_GROUPS
    gw = nh * SSM_HEAD_DIM
    d_inner = nheads * SSM_HEAD_DIM
    q = SSD_CHUNK
    dt_row = dt.reshape(bsz, 2 * SSM_GROUPS, nh, s)
    prm = jnp.stack([jnp.concatenate([dt_bias_f, dt_bias_b]), jnp.concatenate([a_log_f, a_log_b])])
    p_row = jnp.transpose(prm.reshape(2, 2 * SSM_GROUPS, nh), (1, 2, 0))
    dx = jnp.repeat(d_skip, SSM_HEAD_DIM).reshape(SSM_GROUPS, 1, gw)
    assert DECAY_SLOTS * nh <= q and q % nh == 0 and D_STATE == q
    trio = jnp.asarray(np.concatenate([np.triu(np.ones((q, q), np.float32)), np.ones((q, q), np.float32)],
                                      axis=1), BF16)
    G = SSM_GROUPS
    nc = s // q

    return pl.pallas_call(
        _ssd_t_kernel,
        grid=(bsz, SSM_GROUPS),
        in_specs=[pl.BlockSpec((None, s, gw), lambda b, g: (b, 0, g)),
                  pl.BlockSpec((None, s, D_STATE), lambda b, g: (b, 0, g)),
                  pl.BlockSpec((None, s, D_STATE), lambda b, g: (b, 0, G + g)),
                  pl.BlockSpec((None, None, nh, s), lambda b, g: (b, g, 0, 0)),
                  pl.BlockSpec((None, None, nh, s), lambda b, g: (b, G + g, 0, 0)),
                  pl.BlockSpec((None, nh, 2), lambda b, g: (g, 0, 0)),
                  pl.BlockSpec((None, nh, 2), lambda b, g: (G + g, 0, 0)),
                  pl.BlockSpec((None, 1, gw), lambda b, g: (g, 0, 0)),
                  pl.BlockSpec((q, 2 * q), lambda b, g: (0, 0))],
        out_specs=pl.BlockSpec((None, s, gw), lambda b, g: (b, 0, g)),
        out_shape=jax.ShapeDtypeStruct((bsz, s, d_inner), BF16),
        scratch_shapes=[pltpu.VMEM((2 * nh, s), F32), pltpu.VMEM((2 * nh, s), F32),
                        pltpu.VMEM((4 * nh, s), F32),
                        pltpu.VMEM((s, q), BF16), pltpu.VMEM((nc, q, 2 * q), BF16),
                        pltpu.VMEM((gw, s), BF16), pltpu.VMEM((gw, s), BF16), pltpu.VMEM((gw, s), BF16),
                        pltpu.VMEM((gw, s), F32),
                        pltpu.VMEM((gw, D_STATE), F32), pltpu.VMEM((gw, D_STATE), F32)],
        compiler_params=_cparams("parallel", "parallel"),
        name="ssd_scan",
    )(xs, bc, bc, dt_row, dt_row, p_row, p_row, dx, trio)


def _gated_proj_route_kernel(y_ref, z_ref, gw_ref, w_ref, x_ref, g_ref,
                             nw_ref, sh_ref, sc_ref, rw_ref, rb_ref, lt_ref,
                             o_ref, ld_ref, wgt_ref, cnt_ref, acc_ref):
    _gated_proj_kernel(y_ref, z_ref, gw_ref, w_ref, x_ref, g_ref, o_ref, acc_ref)
    h = _modnorm(o_ref[...], nw_ref[...], sh_ref[...], sc_ref[...])
    rows, gates, total = _route_tile(h, rw_ref[...], rb_ref[...], lt_ref[...])
    ld_ref[...] = rows
    wgt_ref[...] = gates
    cnt_ref[...] = total[:, 0:cnt_ref.shape[1]]


def _gated_proj_kernel(y_ref, z_ref, gw_ref, w_ref, x_ref, g_ref, o_ref, acc_ref):
    k = y_ref.shape[1]
    ss = jnp.zeros((y_ref.shape[0], 1), F32)
    for lo in range(0, k, PROJ_CHUNK):
        hi = lo + PROJ_CHUNK
        t = y_ref[:, lo:hi].astype(F32) * _silu(z_ref[:, lo:hi].astype(F32))
        ss = ss + jnp.sum(t * t, axis=-1, keepdims=True)
        contrib = jnp.dot((t * gw_ref[:, lo:hi]).astype(BF16), w_ref[lo:hi, :], preferred_element_type=F32)
        if lo == 0:
            acc_ref[...] = contrib
        else:
            acc_ref[...] += contrib
    o_ref[...] = x_ref[...] + g_ref[...] * (acc_ref[...] * lax.rsqrt(ss * (1.0 / k) + EPS))


def gated_proj_route(y, z, gw, w, x, g, nw, sh, sc, router_w, router_b, *, ts):
    bsz, s, d = x.shape
    k = y.shape[2]
    ne = router_w.shape[1]
    nt = s // ts
    cum = jnp.asarray(np.concatenate([np.triu(np.ones((ts, ts), np.float32)), np.ones((ts, ts), np.float32)],
                                     axis=1), BF16)
    vec = pl.BlockSpec((None, 1, d), lambda b, i: (b, 0, 0))
    tok_spec = pl.BlockSpec((TOP_K, ts), lambda b, i: (0, b * nt + i))
    call = pl.pallas_call(
        _gated_proj_route_kernel,
        grid=(bsz, nt),
        in_specs=[pl.BlockSpec((None, ts, k), lambda b, i: (b, i, 0)),
                  pl.BlockSpec((None, ts, k), lambda b, i: (b, i, 0)),
                  pl.BlockSpec((1, k), lambda b, i: (0, 0)),
                  _resident((k, d)),
                  pl.BlockSpec((None, ts, d), lambda b, i: (b, i, 0)),
                  vec,
                  pl.BlockSpec((1, d), lambda b, i: (0, 0)), vec, vec,
                  _resident((ne, d)), pl.BlockSpec((ne, 1), lambda b, i: (0, 0)), _resident((ts, 2 * ts))],
        out_specs=[pl.BlockSpec((None, ts, d), lambda b, i: (b, i, 0)), tok_spec, tok_spec,
                   pl.BlockSpec((None, ne, 128), lambda b, i: (b * nt + i, 0, 0))],
        out_shape=[jax.ShapeDtypeStruct((bsz, s, d), F32),
                   jax.ShapeDtypeStruct((TOP_K, bsz * s), jnp.int32),
                   jax.ShapeDtypeStruct((TOP_K, bsz * s), F32),
                   jax.ShapeDtypeStruct((bsz * nt, ne, 128), jnp.int32)],
        scratch_shapes=[pltpu.VMEM((ts, d), F32)],
        compiler_params=_cparams("parallel", "parallel"),
        name="ssd_out_proj_route",
    )
    x_new, ldest, wgt, cnt = call(y, z, gw.reshape(1, k), w, x, g, nw.reshape(1, d), sh, sc,
                                  router_w.T, router_b.reshape(ne, 1), cum)
    return x_new, ldest, wgt, cnt[:, :, 0]


SEG_ROWS = 16
SEG_FIELDS = 3


def _route_tile(h, rw_t, rb_col, cum):
    def nt(a, b):
        return lax.dot_general(a, b, (((1,), (1,)), ((), ())), preferred_element_type=F32)

    h_hi = h.astype(BF16)
    h_lo = (h - h_hi.astype(F32)).astype(BF16)
    rw_hi = rw_t.astype(BF16)
    rw_lo = (rw_t - rw_hi.astype(F32)).astype(BF16)
    logits = nt(rw_hi, h_hi) + (nt(rw_hi, h_lo) + nt(rw_lo, h_hi)) + rb_col
    ne, ts = logits.shape
    eid = lax.broadcasted_iota(jnp.int32, (ne, ts), 0)
    m1 = jnp.max(logits, axis=0, keepdims=True)
    i1 = jnp.min(jnp.where(logits == m1, eid, ne), axis=0, keepdims=True)
    rest = jnp.where(eid == i1, -jnp.inf, logits)
    m2 = jnp.max(rest, axis=0, keepdims=True)
    i2 = jnp.min(jnp.where(rest == m2, eid, ne), axis=0, keepdims=True)
    e2 = jnp.exp(m2 - m1)
    w1 = 1.0 / (1.0 + e2)
    w2 = e2 / (1.0 + e2)
    oh1 = (eid == i1).astype(F32)
    oh2 = (eid == i2).astype(F32)
    chosen = oh1 + oh2
    both = jnp.dot(chosen.astype(BF16), cum, preferred_element_type=F32)
    before = both[:, 0:ts] - chosen
    total = both[:, ts:].astype(jnp.int32)
    seg_len = jnp.bitwise_and(total + (SEG_ROWS - 1), -SEG_ROWS).astype(F32)
    rows = [jnp.sum(jnp.where(eid < idx, seg_len, 0.0) + onehot * before, axis=0, keepdims=True)
            for onehot, idx in ((oh1, i1), (oh2, i2))]
    return jnp.concatenate(rows, axis=0).astype(jnp.int32), jnp.concatenate([w1, w2], axis=0), total


def _segment_copies(seg_ref, tile, n_experts, make_copy, *, wait):
    for e in range(n_experts):
        base = (tile * n_experts + e) * SEG_FIELDS
        local0 = seg_ref[base]
        global0 = seg_ref[base + 1]

        def body(i, carry, local0=local0, global0=global0):
            cp = make_copy(pl.multiple_of(local0 + i * SEG_ROWS, SEG_ROWS),
                           pl.multiple_of(global0 + i * SEG_ROWS, SEG_ROWS))
            if wait:
                cp.wait()
            else:
                cp.start()
            return carry

        lax.fori_loop(0, seg_ref[base + 2], body, 0)


def _dispatch_kernel(seg_ref, x_ref, nw_ref, sh_ref, sc_ref, ld_ref, hs_ref, buf_ref, zero_ref, sem,
                     *, n_token_tiles):
    tt = x_ref.shape[0]
    lc = buf_ref.shape[1]
    ne = N_EXPERTS
    tile = pl.program_id(0) * pl.num_programs(1) + pl.program_id(1)
    last = pl.num_programs(0) * pl.num_programs(1) - 1
    slot = tile % 2

    def copies(t, sl, wait):
        def make_copy(lo, go):
            return pltpu.make_async_copy(buf_ref.at[sl, pl.ds(lo, SEG_ROWS), :],
                                         hs_ref.at[pl.ds(go, SEG_ROWS), :], sem.at[sl])
        _segment_copies(seg_ref, t, ne, make_copy, wait=wait)

    h = _modnorm(x_ref[...], nw_ref[...], sh_ref[...], sc_ref[...]).astype(BF16)
    ld = ld_ref[...]
    rows = lax.broadcasted_iota(jnp.int32, (lc, tt), 0)
    perm = jnp.where(rows == ld[0:1, :], 1.0, jnp.where(rows == ld[1:2, :], 1.0, 0.0)).astype(BF16)
    buf_ref[slot] = jnp.dot(perm, h, preferred_element_type=F32).astype(BF16)
    copies(tile, slot, wait=False)

    @pl.when(tile > 0)
    def _():
        copies(tile - 1, 1 - slot, wait=True)

    @pl.when(tile == last)
    def _():
        copies(tile, slot, wait=True)
        zero_ref[...] = jnp.zeros_like(zero_ref)
        tails = n_token_tiles * ne * SEG_FIELDS
        for wait in (False, True):
            for e in range(ne):
                start = seg_ref[tails + 2 * e]

                def body(i, carry, start=start, wait=wait):
                    cp = pltpu.make_async_copy(
                        zero_ref, hs_ref.at[pl.ds(pl.multiple_of(start + i * SEG_ROWS, SEG_ROWS), SEG_ROWS), :],
                        sem.at[2])
                    if wait:
                        cp.wait()
                    else:
                        cp.start()
                    return carry

                lax.fori_loop(0, seg_ref[tails + 2 * e + 1], body, 0)


def moe_dispatch(x, nw, sh, sc, seg, ldest_rows, n_rows, *, tt, lc):
    bsz, s, d = x.shape
    nt = s // tt
    grid_spec = pltpu.PrefetchScalarGridSpec(
        num_scalar_prefetch=1,
        grid=(bsz, nt),
        in_specs=[pl.BlockSpec((None, tt, d), lambda b, i, sref: (b, i, 0)),
                  pl.BlockSpec((1, d), lambda b, i, sref: (0, 0)),
                  pl.BlockSpec((None, 1, d), lambda b, i, sref: (b, 0, 0)),
                  pl.BlockSpec((None, 1, d), lambda b, i, sref: (b, 0, 0)),
                  pl.BlockSpec((TOP_K, tt), lambda b, i, sref: (0, b * nt + i))],
        out_specs=pl.BlockSpec(memory_space=pl.ANY),
        scratch_shapes=[pltpu.VMEM((2, lc, d), BF16), pltpu.VMEM((SEG_ROWS, d), BF16),
                        pltpu.SemaphoreType.DMA((3,))],
    )
    return pl.pallas_call(
        functools.partial(_dispatch_kernel, n_token_tiles=bsz * nt),
        grid_spec=grid_spec,
        out_shape=jax.ShapeDtypeStruct((n_rows, d), BF16),
        compiler_params=_cparams("arbitrary", "arbitrary"),
        name="moe_dispatch",
    )(seg, x, nw.reshape(1, d), sh, sc, ldest_rows)


def _moe_kernel(te_ref, nu_ref, hs_ref, w1_ref, w3_ref, w2_ref, o_ref, acc_ref):
    i = pl.program_id(0)
    f = pl.program_id(1)

    @pl.when(i < nu_ref[0])
    def _():
        @pl.when(f == 0)
        def _():
            acc_ref[...] = jnp.zeros_like(acc_ref)

        h = hs_ref[...]
        for lo in range(0, w1_ref.shape[1], MOE_HIDDEN_CHUNK):
            hi = min(lo + MOE_HIDDEN_CHUNK, w1_ref.shape[1])
            a = jnp.dot(h, w1_ref[:, lo:hi].astype(BF16), preferred_element_type=F32)
            b = jnp.dot(h, w3_ref[:, lo:hi].astype(BF16), preferred_element_type=F32)
            t = (_silu(a) * b).astype(BF16)
            acc_ref[...] += jnp.dot(t, w2_ref[lo:hi, :].astype(BF16), preferred_element_type=F32)

        @pl.when(f == pl.num_programs(1) - 1)
        def _():
            o_ref[...] = acc_ref[...].astype(o_ref.dtype)

    @pl.when((i >= nu_ref[0]) & (f == 0))
    def _():
        o_ref[...] = jnp.zeros_like(o_ref)


def moe_experts(hs, tile_expert, n_used, w1, w3, w2, *, tm, tf):
    n_rows, d = hs.shape
    dff = w1.shape[2]
    nf = dff // tf
    n_tiles = n_rows // tm

    def last_used(i, nu):
        return jnp.maximum(jnp.minimum(i, nu[0] - 1), 0)

    def row_map(i, f, te, nu):
        return (last_used(i, nu), 0)

    def hidden_block(i, f, nu):
        t = last_used(i, nu)
        step = jnp.where(i < nu[0], f, nf - 1)
        return jnp.where(t % 2 == 0, step, nf - 1 - step)

    def w_in_map(i, f, te, nu):
        return (te[last_used(i, nu)], 0, hidden_block(i, f, nu))

    def w_out_map(i, f, te, nu):
        return (te[last_used(i, nu)], hidden_block(i, f, nu), 0)

    grid_spec = pltpu.PrefetchScalarGridSpec(
        num_scalar_prefetch=2,
        grid=(n_tiles, nf),
        in_specs=[pl.BlockSpec((tm, d), row_map),
                  pl.BlockSpec((None, d, tf), w_in_map),
                  pl.BlockSpec((None, d, tf), w_in_map),
                  pl.BlockSpec((None, tf, d), w_out_map)],
        out_specs=pl.BlockSpec((tm, d), lambda i, f, te, nu: (i, 0)),
        scratch_shapes=[pltpu.VMEM((tm, d), F32)],
    )
    return pl.pallas_call(
        _moe_kernel,
        grid_spec=grid_spec,
        out_shape=jax.ShapeDtypeStruct((n_rows, d), BF16),
        compiler_params=_cparams("arbitrary", "arbitrary"),
        name="moe_experts",
    )(tile_expert, n_used, hs, w1, w3, w2)


def _combine_kernel(seg_ref, ys_ref, x_ref, g_ref, wgt_ref, ld_ref, o_ref, buf_ref, sem):
    tt = x_ref.shape[0]
    lc = buf_ref.shape[1]
    ne = N_EXPERTS
    tile = pl.program_id(0) * pl.num_programs(1) + pl.program_id(1)
    last = pl.num_programs(0) * pl.num_programs(1) - 1
    slot = tile % 2

    def copies(t, sl, wait):
        def make_copy(lo, go):
            return pltpu.make_async_copy(ys_ref.at[pl.ds(go, SEG_ROWS), :],
                                         buf_ref.at[sl, pl.ds(lo, SEG_ROWS), :], sem.at[sl])
        _segment_copies(seg_ref, t, ne, make_copy, wait=wait)

    @pl.when(tile == 0)
    def _():
        buf_ref[...] = jnp.zeros_like(buf_ref)
        copies(tile, slot, wait=False)

    @pl.when(tile < last)
    def _():
        copies(tile + 1, 1 - slot, wait=False)

    copies(tile, slot, wait=True)

    ld = ld_ref[...]
    cols = lax.broadcasted_iota(jnp.int32, (tt, lc), 1)
    pick = jnp.concatenate([jnp.where(cols == ld[:, k:k + 1], 1.0, 0.0) for k in range(TOP_K)],
                           axis=0).astype(BF16)
    z = jnp.dot(pick, buf_ref[slot], preferred_element_type=F32)
    w = wgt_ref[...]
    mix = w[:, 0:1] * z[0:tt] + w[:, 1:2] * z[tt:]
    o_ref[...] = x_ref[...] + g_ref[...] * mix


def moe_combine(ys, seg, ldest, wgt, x, g, *, tt, lc):
    bsz, s, d = x.shape
    nt = s // tt
    tok_spec = pl.BlockSpec((tt, TOP_K), lambda b, i, sref: (b * nt + i, 0))
    grid_spec = pltpu.PrefetchScalarGridSpec(
        num_scalar_prefetch=1,
        grid=(bsz, nt),
        in_specs=[pl.BlockSpec(memory_space=pl.ANY),
                  pl.BlockSpec((None, tt, d), lambda b, i, sref: (b, i, 0)),
                  pl.BlockSpec((None, 1, d), lambda b, i, sref: (b, 0, 0)),
                  tok_spec, tok_spec],
        out_specs=pl.BlockSpec((None, tt, d), lambda b, i, sref: (b, i, 0)),
        scratch_shapes=[pltpu.VMEM((2, lc, d), BF16), pltpu.SemaphoreType.DMA((2,))],
    )
    return pl.pallas_call(
        _combine_kernel,
        grid_spec=grid_spec,
        out_shape=jax.ShapeDtypeStruct((bsz, s, d), F32),
        compiler_params=_cparams("arbitrary", "arbitrary"),
        name="moe_combine",
    )(seg, ys, x, g, wgt, ldest)


def _round_up(v, m):
    return ((v + m - 1) // m) * m


def moe_token_tile(s):
    return min(512, s)


def moe_residual(x, nw, sh, sc, g, ldest, wgt, cnt, w1, w3, w2, *, tm=MOE_TILE_ROWS):
    bsz, s, d = x.shape
    n_tok = bsz * s
    ne = w1.shape[0]
    tt = moe_token_tile(s)
    n_tt = n_tok // tt
    lc = _round_up(TOP_K * tt + ne * SEG_ROWS, 128)
    seg_len = _round_up(cnt.reshape(n_tt, ne), SEG_ROWS)
    local_start = jnp.cumsum(seg_len, axis=1) - seg_len
    padded = _round_up(jnp.sum(seg_len, axis=0), tm)
    ends = jnp.cumsum(padded)
    global_start = (ends - padded)[None, :] + jnp.cumsum(seg_len, axis=0) - seg_len
    n_rows = _round_up(n_tok * TOP_K + n_tt * ne * SEG_ROWS + ne * tm, tm)
    used_end = (ends - padded) + jnp.sum(seg_len, axis=0)
    next_start = jnp.concatenate([ends[:-1], jnp.full((1,), n_rows, ends.dtype)])
    tails = jnp.stack([used_end, (next_start - used_end) // SEG_ROWS], axis=-1)
    seg = jnp.concatenate([jnp.stack([local_start, global_start, seg_len // SEG_ROWS], axis=-1).reshape(-1),
                           tails.reshape(-1)]).astype(jnp.int32)
    n_tiles = n_rows // tm
    tile_start = jnp.arange(n_tiles, dtype=jnp.int32) * tm
    tile_expert = jnp.minimum(jnp.sum(tile_start[:, None] >= ends[None, :], axis=1), ne - 1).astype(jnp.int32)
    n_used = (ends[ne - 1:ne] // tm).astype(jnp.int32)
    hs = moe_dispatch(x, nw, sh, sc, seg, ldest, n_rows, tt=tt, lc=lc)
    ys = moe_experts(hs, tile_expert, n_used, w1, w3, w2, tm=tm, tf=w1.shape[2] // 2)
    return moe_combine(ys, seg, ldest.T, wgt.T, x, g, tt=tt, lc=lc)


def _split_mod(mod):
    return [mod[k] for k in range(mod.shape[0])]


def even_layer(x, c, rel_bias, ada_w, ada_b, norm1_w, in_w, q_norm_w, k_norm_w, sink, out_w,
               norm2_w, w1, w3, w2):
    s = x.shape[1]
    sh1, sc1, g1, sh2, sc2, g2 = _split_mod(ada_mod(c, ada_w, ada_b))
    proj = norm_mod_matmul(x, norm1_w, sh1, sc1, in_w.astype(BF16), ts=min(1024, s), tn=256, name="even_in_proj",
                           out_dtype=BF16)
    yf = fourier_mix(proj, tq=min(512, s))
    ya = window_attention(proj, band_bias(rel_bias), q_norm_w, k_norm_w, sink)
    x = cat_proj_residual(yf, ya, out_w.astype(BF16), x, g1, ts=min(1024, s))
    return ffn_residual(x, norm2_w, sh2, sc2, g2, w1.astype(BF16), w3.astype(BF16), w2,
                        ts=min(1024, s), tf=256)


def odd_layer(x, c, ada_w, ada_b, norm1_w, in_w, conv_w, conv_b, dt_bias_f, dt_bias_b, a_log_f, a_log_b,
              d_skip, gnorm_w, out_w, norm2_w, router_w, router_b, w1, w3, w2):
    s = x.shape[1]
    sh1, sc1, g1, sh2, sc2, g2 = _split_mod(ada_mod(c, ada_w, ada_b))
    d_inner = gnorm_w.shape[0]
    z, xs, bc, dt = ssm_in_proj_conv(x, norm1_w, sh1, sc1, in_w.astype(BF16), conv_w, conv_b,
                                     dz=d_inner, n_x=d_inner, ts=min(1024, s))
    y = ssd_scan_bidir(xs, bc, dt, dt_bias_f, dt_bias_b, a_log_f, a_log_b, d_skip)
    x, ldest, wgt, cnt = gated_proj_route(y, z, gnorm_w, out_w.astype(BF16), x, g1, norm2_w, sh2, sc2,
                                          router_w, router_b, ts=moe_token_tile(s))
    return moe_residual(x, norm2_w, sh2, sc2, g2, ldest, wgt, cnt,
                        w1, w3, w2)


def kernel(x, c, rel_bias, ev_ada_w, ev_ada_b, ev_norm1_w, ev_in_w, ev_q_norm_w, ev_k_norm_w, ev_sink, ev_out_w, ev_norm2_w, ev_ffn_w1, ev_ffn_w3, ev_ffn_w2, od_ada_w, od_ada_b, od_norm1_w, od_in_w, od_conv_w, od_conv_b, od_dt_bias_f, od_dt_bias_b, od_A_log_f, od_A_log_b, od_D, od_gnorm_w, od_out_w, od_norm2_w, od_router_w, od_router_b, od_moe_w1, od_moe_w3, od_moe_w2):
    depth = ev_ada_w.shape[0] + od_ada_w.shape[0]
    for i in range(depth):
        j = i // 2
        if i % 2 == 0:
            x = even_layer(x, c, rel_bias, ev_ada_w[j], ev_ada_b[j], ev_norm1_w[j], ev_in_w[j],
                           ev_q_norm_w[j], ev_k_norm_w[j], ev_sink[j], ev_out_w[j], ev_norm2_w[j],
                           ev_ffn_w1[j], ev_ffn_w3[j], ev_ffn_w2[j])
        else:
            x = odd_layer(x, c, od_ada_w[j], od_ada_b[j], od_norm1_w[j], od_in_w[j], od_conv_w[j],
                          od_conv_b[j], od_dt_bias_f[j], od_dt_bias_b[j], od_A_log_f[j], od_A_log_b[j],
                          od_D[j], od_gnorm_w[j], od_out_w[j], od_norm2_w[j], od_router_w[j],
                          od_router_b[j], od_moe_w1[j], od_moe_w3[j], od_moe_w2[j])
    return x
```

```python
import functools

import numpy as np
import jax
import jax.numpy as jnp
from jax import lax
from jax.experimental import pallas as pl
from jax.experimental.pallas import tpu as pltpu

F32 = jnp.float32
BF16 = jnp.bfloat16

EPS = 1e-6
FNET_GROUPS = 4
FNET_GROUP_DIM = 128
FNET_WIDTH = FNET_GROUPS * FNET_GROUP_DIM
ATTN_HEADS = 8
ATTN_KV_HEADS = 2
HEAD_DIM = 64
ATTN_WIDTH = ATTN_HEADS * HEAD_DIM
KV_WIDTH = ATTN_KV_HEADS * HEAD_DIM
WINDOW = 128
BLOCK = 128
REL_BUCKETS = 32
REL_MAX_DIST = 128
SSM_HEAD_DIM = 64
SSM_GROUPS = 4
D_STATE = 128
CONV_WIDTH = 5
SSD_CHUNK = 128
N_EXPERTS = 8
TOP_K = 2
NEG_BIG = -1e30
ATTN_QBLOCKS = 8

V7X_VMEM_LIMIT_BYTES = 56 * 1024 * 1024
MOE_TILE_ROWS = 768
MOE_HIDDEN_CHUNK = 256


def _cparams(*sem):
    return pltpu.CompilerParams(dimension_semantics=sem, vmem_limit_bytes=V7X_VMEM_LIMIT_BYTES)


def _modnorm(x, nw, sh, sc):
    ms = jnp.mean(x * x, axis=-1, keepdims=True)
    return x * lax.rsqrt(ms + EPS) * nw * (1.0 + sc) + sh


def _silu(x):
    return x * (1.0 / (1.0 + jnp.exp(-x)))


def _ada_kernel(c_ref, w_ref, b_ref, o_ref):
    cs = _silu(c_ref[...]).astype(BF16)
    val = jnp.dot(cs, w_ref[...].astype(BF16), preferred_element_type=F32) + b_ref[...]
    for row in range(val.shape[0]):
        o_ref[row] = val[row:row + 1, :]


def ada_mod(c, w, b, *, n_chunks=6):
    bsz, d = c.shape
    n = w.shape[1]
    tn = n // n_chunks
    return pl.pallas_call(
        _ada_kernel,
        grid=(n_chunks,),
        in_specs=[pl.BlockSpec((bsz, d), lambda j: (0, 0)),
                  pl.BlockSpec((d, tn), lambda j: (0, j)),
                  pl.BlockSpec((1, tn), lambda j: (0, j))],
        out_specs=pl.BlockSpec((None, bsz, 1, tn), lambda j: (j, 0, 0, 0)),
        out_shape=jax.ShapeDtypeStruct((n_chunks, bsz, 1, tn), F32),
        compiler_params=_cparams("arbitrary"),
        name="ada_mod",
    )(c, w, b.reshape(1, n))


def _nmm_kernel(x_ref, nw_ref, sh_ref, sc_ref, w_ref, o_ref, *, tn):
    h = _modnorm(x_ref[...], nw_ref[...], sh_ref[...], sc_ref[...]).astype(BF16)
    n = w_ref.shape[1]
    for lo in range(0, n, tn):
        o_ref[:, lo:lo + tn] = jnp.dot(h, w_ref[:, lo:lo + tn],
                                       preferred_element_type=F32).astype(o_ref.dtype)


def _resident(shape):
    return pl.BlockSpec(shape, lambda *_: tuple(0 for _ in shape), pipeline_mode=pl.Buffered(1))


def norm_mod_matmul(x, nw, sh, sc, w, *, ts, tn, name, out_dtype=F32):
    bsz, s, d = x.shape
    n = w.shape[1]
    assert n % tn == 0
    return pl.pallas_call(
        functools.partial(_nmm_kernel, tn=tn),
        grid=(bsz, s // ts),
        in_specs=[pl.BlockSpec((None, ts, d), lambda b, i: (b, i, 0)),
                  pl.BlockSpec((1, d), lambda b, i: (0, 0)),
                  pl.BlockSpec((None, 1, d), lambda b, i: (b, 0, 0)),
                  pl.BlockSpec((None, 1, d), lambda b, i: (b, 0, 0)),
                  _resident((d, n))],
        out_specs=pl.BlockSpec((None, ts, n), lambda b, i: (b, i, 0)),
        out_shape=jax.ShapeDtypeStruct((bsz, s, n), out_dtype),
        compiler_params=_cparams("parallel", "parallel"),
        name=name,
    )(x, nw.reshape(1, d), sh, sc, w)


def _dft_cos_sin(n):
    k = np.arange(n, dtype=np.int64)
    ang = ((k[:, None] * k[None, :]) % n).astype(np.float64) * (2.0 * np.pi / n)
    scale = 1.0 / np.sqrt(n)
    return np.cos(ang) * scale, np.sin(ang) * scale


def _fourier_kernel(u_ref, chan_ref, seq_ref, o_ref, ab_ref):
    s = u_ref.shape[0]

    @pl.when(pl.program_id(1) == 0)
    def _():
        for g in range(FNET_GROUPS):
            lo, hi = g * FNET_GROUP_DIM, (g + 1) * FNET_GROUP_DIM
            ug = u_ref[:, lo:hi].astype(BF16)
            cs = jnp.dot(ug, chan_ref[...], preferred_element_type=F32)
            ab_ref[0:s, lo:hi] = cs[:, :FNET_GROUP_DIM].astype(BF16)
            ab_ref[s:2 * s, lo:hi] = cs[:, FNET_GROUP_DIM:].astype(BF16)

    o_ref[...] = jnp.dot(seq_ref[...], ab_ref[...], preferred_element_type=F32).astype(o_ref.dtype)


def fourier_mix(proj, *, tq):
    bsz, s, _ = proj.shape
    cc, sc = _dft_cos_sin(FNET_GROUP_DIM)
    chan = jnp.asarray(np.concatenate([cc, sc], axis=1), BF16)
    cs, ss = _dft_cos_sin(s)
    seq = jnp.asarray(np.concatenate([cs, -ss], axis=1), BF16)
    return pl.pallas_call(
        _fourier_kernel,
        grid=(bsz, s // tq),
        in_specs=[pl.BlockSpec((None, s, FNET_WIDTH), lambda b, i: (b, 0, 0)),
                  pl.BlockSpec((FNET_GROUP_DIM, 2 * FNET_GROUP_DIM), lambda b, i: (0, 0)),
                  pl.BlockSpec((tq, 2 * s), lambda b, i: (i, 0))],
        out_specs=pl.BlockSpec((None, tq, FNET_WIDTH), lambda b, i: (b, i, 0)),
        out_shape=jax.ShapeDtypeStruct((bsz, s, FNET_WIDTH), BF16),
        scratch_shapes=[pltpu.VMEM((2 * s, FNET_WIDTH), BF16)],
        compiler_params=_cparams("parallel", "arbitrary"),
        name="fourier_mix",
    )(proj, chan, seq)


def _band_bucket_table():
    i = np.arange(BLOCK)[:, None]
    j = np.arange(3 * BLOCK)[None, :]
    rel = (j - BLOCK) - i
    half = REL_BUCKETS // 2
    max_exact = half // 2
    n = np.abs(rel)
    large = max_exact + (np.log(np.maximum(n, 1) / max_exact)
                         / np.log(REL_MAX_DIST / max_exact) * (half - max_exact)).astype(np.int32)
    large = np.minimum(large, half - 1)
    bucket = (rel > 0).astype(np.int32) * half + np.where(n < max_exact, n, large)
    return np.where(n <= WINDOW, bucket, -1).astype(np.int32)


def _bias_kernel(rb_ref, bucket_ref, o_ref):
    h = pl.program_id(0)
    bucket = bucket_ref[...]
    acc = jnp.full(bucket.shape, NEG_BIG, F32)
    for bkt in range(REL_BUCKETS):
        acc = jnp.where(bucket == bkt, rb_ref[bkt * ATTN_HEADS + h], acc)
    o_ref[...] = acc


def band_bias(rel_bias):
    bucket = jnp.asarray(_band_bucket_table())
    return pl.pallas_call(
        _bias_kernel,
        grid=(ATTN_HEADS,),
        in_specs=[pl.BlockSpec(memory_space=pltpu.SMEM),
                  pl.BlockSpec((BLOCK, 3 * BLOCK), lambda h: (0, 0))],
        out_specs=pl.BlockSpec((None, BLOCK, 3 * BLOCK), lambda h: (h, 0, 0)),
        out_shape=jax.ShapeDtypeStruct((ATTN_HEADS, BLOCK, 3 * BLOCK), F32),
        compiler_params=_cparams("arbitrary"),
        name="band_bias",
    )(rel_bias.reshape(-1), bucket)


def _head_mean_matrix(width):
    m = np.zeros((width, width), np.float32)
    for h in range(width // HEAD_DIM):
        m[h * HEAD_DIM:(h + 1) * HEAD_DIM, h * HEAD_DIM:(h + 1) * HEAD_DIM] = 1.0 / HEAD_DIM
    return m


def _heads_rms(t, mean_mat, w):
    sq = t * t
    hi = sq.astype(BF16)
    lo = (sq - hi.astype(F32)).astype(BF16)
    ms = (jnp.dot(hi, mean_mat, preferred_element_type=F32)
          + jnp.dot(lo, mean_mat, preferred_element_type=F32))
    return t * lax.rsqrt(ms + EPS) * w


def _attn_kernel(sink_ref, q_ref, kl_ref, kc_ref, kr_ref, vl_ref, vc_ref, vr_ref,
                 bias_ref, qnw_ref, knw_ref, qmean_ref, kmean_ref, o_ref):
    n = pl.program_id(1)
    nb = pl.num_programs(1) * ATTN_QBLOCKS
    k = jnp.concatenate([kl_ref[...], kc_ref[...], kr_ref[...]], axis=0).astype(F32)
    v = jnp.concatenate([vl_ref[...], vc_ref[...], vr_ref[...]], axis=0).astype(F32)
    col = lax.broadcasted_iota(jnp.int32, (1, 3 * BLOCK), 1)
    qn = _heads_rms(q_ref[...].astype(F32), qmean_ref[...], qnw_ref[...])
    kn = _heads_rms(k, kmean_ref[...], knw_ref[...])
    low = lax.broadcasted_iota(jnp.int32, (1, 2 * HEAD_DIM), 1) < HEAD_DIM
    kn_sw = pltpu.roll(kn, HEAD_DIM, axis=1)
    v_sw = pltpu.roll(v, HEAD_DIM, axis=1)
    k_dup = [jnp.where(low, kn, kn_sw).astype(BF16), jnp.where(low, kn_sw, kn).astype(BF16)]
    ones = jnp.ones((k.shape[0], 2 * HEAD_DIM), BF16)
    v_ext = [jnp.concatenate([v.astype(BF16), ones], axis=1),
             jnp.concatenate([v_sw.astype(BF16), ones], axis=1)]
    g = ATTN_HEADS // ATTN_KV_HEADS
    for qb in range(ATTN_QBLOCKS):
        blk = n * ATTN_QBLOCKS + qb
        band = slice(qb * BLOCK, (qb + 3) * BLOCK)
        first_key = jnp.where(blk == 0, BLOCK, 0)
        end_key = jnp.where(blk == nb - 1, 2 * BLOCK, 3 * BLOCK)
        outside = (col < first_key) | (col >= end_key)
        raw = []
        for h in range(ATTN_HEADS):
            m, idx, j = h // 2, h % 2, h // g
            qp = qn[qb * BLOCK:(qb + 1) * BLOCK, m * 2 * HEAD_DIM:(m + 1) * 2 * HEAD_DIM]
            qm = jnp.where(low if idx == 0 else jnp.logical_not(low), qp, 0.0).astype(BF16)
            raw.append(lax.dot_general(qm, k_dup[j][band], (((1,), (1,)), ((), ())),
                                       preferred_element_type=F32))
        probs, tails = [], []
        for h in range(ATTN_HEADS):
            logits = jnp.where(outside, NEG_BIG, raw[h] + bias_ref[h])
            sk = sink_ref[h]
            mx = jnp.maximum(jnp.max(logits, axis=-1, keepdims=True), sk)
            probs.append(jnp.exp(logits - mx).astype(BF16))
            tails.append(jnp.exp(sk - mx))
        res = []
        for h in range(ATTN_HEADS):
            idx, j = h % 2, h // g
            r = jnp.dot(probs[h], v_ext[idx if j == 0 else 1 - idx][band], preferred_element_type=F32)
            res.append(r[:, :2 * HEAD_DIM] / (r[:, 2 * HEAD_DIM:] + tails[h]))
        pairs = [jnp.where(low, res[2 * m], res[2 * m + 1]) for m in range(ATTN_HEADS // 2)]
        o_ref[qb * BLOCK:(qb + 1) * BLOCK, :] = jnp.concatenate(pairs, axis=-1).astype(o_ref.dtype)


def window_attention(proj, bias, q_norm_w, k_norm_w, sink):
    bsz, s, _ = proj.shape
    nb = s // BLOCK
    qcol = FNET_WIDTH // ATTN_WIDTH
    kcol = (FNET_WIDTH + ATTN_WIDTH) // KV_WIDTH
    vcol = kcol + 1

    qb = ATTN_QBLOCKS
    assert nb % qb == 0

    def kv_specs(col):
        return [pl.BlockSpec((None, BLOCK, KV_WIDTH), lambda b, n: (b, jnp.maximum(n * qb - 1, 0), col)),
                pl.BlockSpec((None, qb * BLOCK, KV_WIDTH), lambda b, n: (b, n, col)),
                pl.BlockSpec((None, BLOCK, KV_WIDTH), lambda b, n: (b, jnp.minimum((n + 1) * qb, nb - 1), col))]

    return pl.pallas_call(
        _attn_kernel,
        grid=(bsz, nb // qb),
        in_specs=[pl.BlockSpec(memory_space=pltpu.SMEM),
                  pl.BlockSpec((None, qb * BLOCK, ATTN_WIDTH), lambda b, n: (b, n, qcol)),
                  *kv_specs(kcol), *kv_specs(vcol),
                  pl.BlockSpec((ATTN_HEADS, BLOCK, 3 * BLOCK), lambda b, n: (0, 0, 0)),
                  pl.BlockSpec((1, ATTN_WIDTH), lambda b, n: (0, 0)),
                  pl.BlockSpec((1, KV_WIDTH), lambda b, n: (0, 0)),
                  pl.BlockSpec((ATTN_WIDTH, ATTN_WIDTH), lambda b, n: (0, 0)),
                  pl.BlockSpec((KV_WIDTH, KV_WIDTH), lambda b, n: (0, 0))],
        out_specs=pl.BlockSpec((None, qb * BLOCK, ATTN_WIDTH), lambda b, n: (b, n, 0)),
        out_shape=jax.ShapeDtypeStruct((bsz, s, ATTN_WIDTH), BF16),
        compiler_params=_cparams("parallel", "arbitrary"),
        name="window_attention",
    )(sink, proj, proj, proj, proj, proj, proj, proj, bias,
      (jnp.tile(q_norm_w, ATTN_HEADS) * (HEAD_DIM ** -0.5)).reshape(1, ATTN_WIDTH),
      jnp.tile(k_norm_w, ATTN_KV_HEADS).reshape(1, KV_WIDTH),
      jnp.asarray(_head_mean_matrix(ATTN_WIDTH), BF16), jnp.asarray(_head_mean_matrix(KV_WIDTH), BF16))


def _cat_proj_kernel(a1_ref, a2_ref, w_ref, x_ref, g_ref, o_ref):
    k1 = a1_ref.shape[1]
    y = jnp.dot(a1_ref[...].astype(BF16), w_ref[0:k1, :], preferred_element_type=F32)
    y = y + jnp.dot(a2_ref[...].astype(BF16), w_ref[k1:, :], preferred_element_type=F32)
    o_ref[...] = x_ref[...] + g_ref[...] * y


def cat_proj_residual(a1, a2, w, x, g, *, ts):
    bsz, s, d = x.shape
    k1, k2 = a1.shape[2], a2.shape[2]
    return pl.pallas_call(
        _cat_proj_kernel,
        grid=(bsz, s // ts),
        in_specs=[pl.BlockSpec((None, ts, k1), lambda b, i: (b, i, 0)),
                  pl.BlockSpec((None, ts, k2), lambda b, i: (b, i, 0)),
                  pl.BlockSpec((k1 + k2, d), lambda b, i: (0, 0)),
                  pl.BlockSpec((None, ts, d), lambda b, i: (b, i, 0)),
                  pl.BlockSpec((None, 1, d), lambda b, i: (b, 0, 0))],
        out_specs=pl.BlockSpec((None, ts, d), lambda b, i: (b, i, 0)),
        out_shape=jax.ShapeDtypeStruct((bsz, s, d), F32),
        compiler_params=_cparams("parallel", "parallel"),
        name="mixer_out_proj",
    )(a1, a2, w, x, g)


def _ffn_kernel(x_ref, nw_ref, sh_ref, sc_ref, g_ref, w1_ref, w3_ref, w2_ref, o_ref, acc_ref, *, tf):
    h = _modnorm(x_ref[...], nw_ref[...], sh_ref[...], sc_ref[...]).astype(BF16)
    dff = w1_ref.shape[1]
    for lo in range(0, dff, tf):
        a = jnp.dot(h, w1_ref[:, lo:lo + tf], preferred_element_type=F32)
        b = jnp.dot(h, w3_ref[:, lo:lo + tf], preferred_element_type=F32)
        t = (_silu(a) * b).astype(BF16)
        contrib = jnp.dot(t, w2_ref[lo:lo + tf, :].astype(BF16), preferred_element_type=F32)
        if lo == 0:
            acc_ref[...] = contrib
        else:
            acc_ref[...] += contrib
    o_ref[...] = x_ref[...] + g_ref[...] * acc_ref[...]


def ffn_residual(x, nw, sh, sc, g, w1, w3, w2, *, ts, tf):
    bsz, s, d = x.shape
    dff = w1.shape[1]
    assert dff % tf == 0
    vec = pl.BlockSpec((None, 1, d), lambda b, i: (b, 0, 0))
    return pl.pallas_call(
        functools.partial(_ffn_kernel, tf=tf),
        grid=(bsz, s // ts),
        in_specs=[pl.BlockSpec((None, ts, d), lambda b, i: (b, i, 0)),
                  pl.BlockSpec((1, d), lambda b, i: (0, 0)),
                  vec, vec, vec,
                  _resident((d, dff)), _resident((d, dff)), _resident((dff, d))],
        out_specs=pl.BlockSpec((None, ts, d), lambda b, i: (b, i, 0)),
        out_shape=jax.ShapeDtypeStruct((bsz, s, d), F32),
        scratch_shapes=[pltpu.VMEM((ts, d), F32)],
        compiler_params=_cparams("parallel", "parallel"),
        name="ffn_swiglu",
    )(x, nw.reshape(1, d), sh, sc, g, w1, w3, w2)


HALO_ROWS = 16
PROJ_CHUNK = 256
W_VIEW = 1024


def _proj_conv_kernel(x_ref, xp_ref, xn_ref, nw_ref, sh_ref, sc_ref, *rest, n_views):
    w_views = rest[:n_views]
    wd_ref, cw_ref, cb_ref, z_ref, xs_ref, bc_ref, dt_ref = rest[n_views:]
    i = pl.program_id(1)
    ts = x_ref.shape[0]
    half = CONV_WIDTH // 2
    dz = z_ref.shape[1]
    dc = cw_ref.shape[1]

    def wcols(c0):
        return w_views[c0 // W_VIEW][:, c0 % W_VIEW:c0 % W_VIEW + PROJ_CHUNK]

    def hnorm(ref):
        return _modnorm(ref[...], nw_ref[...], sh_ref[...], sc_ref[...])

    h = hnorm(x_ref).astype(BF16)
    h_prev = jnp.where(i == 0, 0.0, hnorm(xp_ref)).astype(BF16)
    h_next = jnp.where(i == pl.num_programs(1) - 1, 0.0, hnorm(xn_ref)).astype(BF16)
    h_ext = jnp.concatenate([h_prev, h, h_next], axis=0)

    n_x = xs_ref.shape[1]
    rows = h_ext.shape[0]

    def project(lo):
        return jnp.dot(h_ext, wcols(dz + lo), preferred_element_type=F32)

    def z_chunk(lo):
        hi = lo + PROJ_CHUNK
        z_ref[:, lo:hi] = jnp.dot(h, wcols(lo), preferred_element_type=F32).astype(z_ref.dtype)

    z_los = list(range(0, dz, PROJ_CHUNK))
    c_los = list(range(0, dc, PROJ_CHUNK))
    p_next = project(c_los[0])
    for n, lo in enumerate(c_los):
        hi = lo + PROJ_CHUNK
        p = p_next
        if n + 1 < len(c_los):
            p_next = project(c_los[n + 1])
        if z_los:
            z_chunk(z_los.pop(0))
        acc = jnp.zeros((ts, PROJ_CHUNK), F32) + cb_ref[:, lo:hi]
        for kk in range(CONV_WIDTH):
            shifted = p if kk == half else pltpu.roll(p, (half - kk) % rows, axis=0)
            acc = acc + shifted[HALO_ROWS:HALO_ROWS + ts, :] * cw_ref[kk:kk + 1, lo:hi]
        out = _silu(acc)
        if lo < n_x:
            xs_ref[:, lo:hi] = out
        else:
            bc_ref[:, lo - n_x:hi - n_x] = out.astype(bc_ref.dtype)
    for lo in z_los:
        z_chunk(lo)
    dt_ref[...] = lax.dot_general(wd_ref[...], h, (((1,), (1,)), ((), ())), preferred_element_type=F32)


def ssm_in_proj_conv(x, nw, sh, sc, w_in, conv_w, conv_b, *, dz, n_x, ts):
    bsz, s, d = x.shape
    dc = conv_w.shape[1]
    ddt = w_in.shape[1] - dz - dc
    assert ts % HALO_ROWS == 0 and n_x % PROJ_CHUNK == 0 and dc % PROJ_CHUNK == 0 and dz % PROJ_CHUNK == 0
    assert dz % W_VIEW == 0 and dc % W_VIEW == 0 and W_VIEW % PROJ_CHUNK == 0
    n_views = (dz + dc) // W_VIEW
    views = [pl.BlockSpec((d, W_VIEW), lambda b, i, k=k: (0, k), pipeline_mode=pl.Buffered(1))
             for k in range(n_views)]
    r = ts // HALO_ROWS
    last = s // HALO_ROWS - 1
    vec = pl.BlockSpec((None, 1, d), lambda b, i: (b, 0, 0))
    return pl.pallas_call(
        functools.partial(_proj_conv_kernel, n_views=n_views),
        grid=(bsz, s // ts),
        in_specs=[pl.BlockSpec((None, ts, d), lambda b, i: (b, i, 0)),
                  pl.BlockSpec((None, HALO_ROWS, d), lambda b, i: (b, jnp.maximum(i * r - 1, 0), 0)),
                  pl.BlockSpec((None, HALO_ROWS, d), lambda b, i: (b, jnp.minimum((i + 1) * r, last), 0)),
                  pl.BlockSpec((1, d), lambda b, i: (0, 0)), vec, vec,
                  *views, _resident((ddt, d)),
                  _resident((CONV_WIDTH, dc)), _resident((1, dc))],
        out_specs=[pl.BlockSpec((None, ts, dz), lambda b, i: (b, i, 0)),
                   pl.BlockSpec((None, ts, n_x), lambda b, i: (b, i, 0)),
                   pl.BlockSpec((None, ts, dc - n_x), lambda b, i: (b, i, 0)),
                   pl.BlockSpec((None, ddt, ts), lambda b, i: (b, 0, i))],
        out_shape=[jax.ShapeDtypeStruct((bsz, s, dz), BF16),
                   jax.ShapeDtypeStruct((bsz, s, n_x), F32),
                   jax.ShapeDtypeStruct((bsz, s, dc - n_x), BF16),
                   jax.ShapeDtypeStruct((bsz, ddt, s), F32)],
        compiler_params=_cparams("parallel", "parallel"),
        name="odd_in_proj_conv",
    )(x, x, x, nw.reshape(1, d), sh, sc, *([w_in] * n_views), w_in[:, dz + dc:].T, conv_w, conv_b.reshape(1, dc))


def _softplus(x):
    return jnp.maximum(x, 0.0) + jnp.log(1.0 + jnp.exp(-jnp.abs(x)))


LOG2E = 1.4426950408889634
DECAY_SLOTS = 12


def _bf16_parts3(v):
    hi = v.astype(BF16).astype(F32)
    r = v - hi
    mid = r.astype(BF16).astype(F32)
    lo = (r - mid).astype(BF16).astype(F32)
    return hi, mid, lo


def _ssd_t_kernel(x_ref, b_ref, c_ref, dt_f_ref, dt_b_ref, p_f_ref, p_b_ref, dx_ref, trio_ref, y_ref,
                  ar_ref, dr_ref, sc_ref, pq_ref, qt_ref, xtb_ref, xdf_ref, xdb_ref, yt_ref, hf_ref, hb_ref):
    s = x_ref.shape[0]
    q = SSD_CHUNK
    nh = p_f_ref.shape[0]
    nc = s // q
    hd = SSM_HEAD_DIM

    def row_params(raw_ref, p_ref):
        dt = _softplus(raw_ref[...] + p_ref[:, 0:1])
        return dt, (-LOG2E) * jnp.exp(p_ref[:, 1:2]) * dt

    dtf, af = row_params(dt_f_ref, p_f_ref)
    dtb, ab = row_params(dt_b_ref, p_b_ref)
    dr_ref[0:nh, :] = dtf
    dr_ref[nh:, :] = dtb
    ar_ref[0:nh, :] = af
    ar_ref[nh:, :] = ab

    li = lax.broadcasted_iota(jnp.int32, (q, q), 0)
    si = lax.broadcasted_iota(jnp.int32, (q, q), 1)
    lower = li >= si
    upper = li <= si
    slot_head = lax.broadcasted_iota(jnp.int32, (1, q), 1) % nh

    def bdot(a, b):
        return jnp.dot(a, b, preferred_element_type=F32)

    def ntdot(a, b):
        return lax.dot_general(a, b, (((1,), (1,)), ((), ())), preferred_element_type=F32)

    def head_rows(v):
        return jnp.concatenate([jnp.broadcast_to(v[h:h + 1, :], (hd, q)) for h in range(nh)], axis=0)

    ones = jnp.ones((nh, q), F32)
    zeros = jnp.zeros((nh, q), F32)

    def prep_body(c, carry):
        sl = pl.ds(pl.multiple_of(c * q, q), q)
        a_row = ar_ref[:, sl]
        d_row = dr_ref[:, sl]
        parts = jnp.concatenate([p.astype(BF16) for p in _bf16_parts3(a_row)], axis=0)
        cs = bdot(parts, trio_ref[...])
        cs = cs[0:2 * nh] + cs[2 * nh:4 * nh] + cs[4 * nh:6 * nh]
        i_f = cs[0:nh, 0:q]
        e_b = cs[nh:, 0:q] - a_row[nh:]
        tot_f = cs[0:nh, q:]
        tot_b = cs[nh:, q:]
        ih, im, il = _bf16_parts3(i_f)
        eh, em, el = _bf16_parts3(e_b)
        pad = [zeros] * (q // nh - DECAY_SLOTS)
        p_t = jnp.concatenate([ih, im, il, ones, ones, ones, -eh, -em, -el, ones, ones, ones] + pad, axis=0)
        qf_t = jnp.concatenate([ones, ones, ones, -ih, -im, -il] + [zeros] * 6 + pad, axis=0)
        qb_t = jnp.concatenate([zeros] * 6 + [ones, ones, ones, eh, em, el] + pad, axis=0)
        pq_ref[sl, :] = p_t.T.astype(BF16)
        qt_ref[c] = jnp.concatenate([qf_t, qb_t], axis=1).astype(BF16)
        sc_ref[0 * nh:1 * nh, sl] = jnp.exp2(i_f)
        sc_ref[1 * nh:2 * nh, sl] = jnp.exp2(tot_b - e_b)
        sc_ref[2 * nh:3 * nh, sl] = jnp.exp2(tot_f)
        sc_ref[3 * nh:4 * nh, sl] = jnp.exp2(tot_b)
        xt = x_ref[sl, :].T
        xtb_ref[:, sl] = xt.astype(BF16)
        xdf_ref[:, sl] = (xt * head_rows(jnp.exp2(tot_f - i_f) * d_row[0:nh])).astype(BF16)
        xdb_ref[:, sl] = (xt * head_rows(jnp.exp2(e_b) * d_row[nh:])).astype(BF16)
        return carry

    lax.fori_loop(0, nc, prep_body, 0, unroll=8)

    hf_ref[...] = jnp.zeros_like(hf_ref)
    hb_ref[...] = jnp.zeros_like(hb_ref)
    zero_half = jnp.zeros((hd, q), BF16)

    def fwd_body(c, carry):
        sl = pl.ds(pl.multiple_of(c * q, q), q)
        bc = b_ref[sl, :].astype(BF16)
        cc = c_ref[sl, :].astype(BF16)
        d_row = dr_ref[:, sl]
        p_all = pq_ref[sl, :]
        q_t = qt_ref[c]
        xtb = xtb_ref[:, sl]
        cb = ntdot(cc, bc)
        g2s = [bdot(jnp.where(slot_head == h, p_all, jnp.zeros_like(p_all)), q_t) for h in range(nh)]
        ms = []
        for h in range(nh):
            arg = jnp.where(lower, g2s[h][:, 0:q], g2s[h][:, q:])
            wgt = (jnp.where(lower, d_row[h:h + 1, :], 0.0)
                   + jnp.where(upper, d_row[nh + h:nh + h + 1, :], 0.0))
            ms.append((cb * jnp.exp2(arg) * wgt).astype(BF16))
        yd = []
        for h0 in range(0, nh, 2):
            lhs = jnp.concatenate(
                [jnp.concatenate([xtb[h0 * hd:(h0 + 1) * hd], zero_half], axis=0),
                 jnp.concatenate([zero_half, xtb[(h0 + 1) * hd:(h0 + 2) * hd]], axis=0)], axis=1)
            yd.append(ntdot(lhs, jnp.concatenate(ms[h0:h0 + 2], axis=1)))
        states = bdot(xdf_ref[:, sl], bc)
        h_prev = hf_ref[...]
        y_off = ntdot(h_prev.astype(BF16), cc) * head_rows(sc_ref[0 * nh:1 * nh, sl])
        hf_ref[...] = h_prev * head_rows(sc_ref[2 * nh:3 * nh, sl]) + states
        yt_ref[:, sl] = jnp.concatenate(yd, axis=0) + y_off
        return carry

    lax.fori_loop(0, nc, fwd_body, 0, unroll=4)

    def bwd_body(t, carry):
        c = nc - 1 - t
        sl = pl.ds(pl.multiple_of(c * q, q), q)
        bc = b_ref[sl, :].astype(BF16)
        cc = c_ref[sl, :].astype(BF16)
        states = bdot(xdb_ref[:, sl], bc)
        h_prev = hb_ref[...]
        y_off = ntdot(h_prev.astype(BF16), cc) * head_rows(sc_ref[1 * nh:2 * nh, sl])
        hb_ref[...] = h_prev * head_rows(sc_ref[3 * nh:4 * nh, sl]) + states
        y_ref[sl, :] = ((yt_ref[:, sl] + y_off).T + dx_ref[...] * x_ref[sl, :]).astype(y_ref.dtype)
        return carry

    lax.fori_loop(0, nc, bwd_body, 0, unroll=8)


def ssd_scan_bidir(xs, bc, dt, dt_bias_f, dt_bias_b, a_log_f, a_log_b, d_skip):
    bsz, s, _ = xs.shape
    nheads = dt.shape[1] // 2
    nh = nheads // SSM_GROUPS
    gw = nh * SSM_HEAD_DIM
    d_inner = nheads * SSM_HEAD_DIM
    q = SSD_CHUNK
    dt_row = dt.reshape(bsz, 2 * SSM_GROUPS, nh, s)
    prm = jnp.stack([jnp.concatenate([dt_bias_f, dt_bias_b]), jnp.concatenate([a_log_f, a_log_b])])
    p_row = jnp.transpose(prm.reshape(2, 2 * SSM_GROUPS, nh), (1, 2, 0))
    dx = jnp.repeat(d_skip, SSM_HEAD_DIM).reshape(SSM_GROUPS, 1, gw)
    assert DECAY_SLOTS * nh <= q and q % nh == 0 and D_STATE == q
    trio = jnp.asarray(np.concatenate([np.triu(np.ones((q, q), np.float32)), np.ones((q, q), np.float32)],
                                      axis=1), BF16)
    G = SSM_GROUPS
    nc = s // q

    return pl.pallas_call(
        _ssd_t_kernel,
        grid=(bsz, SSM_GROUPS),
        in_specs=[pl.BlockSpec((None, s, gw), lambda b, g: (b, 0, g)),
                  pl.BlockSpec((None, s, D_STATE), lambda b, g: (b, 0, g)),
                  pl.BlockSpec((None, s, D_STATE), lambda b, g: (b, 0, G + g)),
                  pl.BlockSpec((None, None, nh, s), lambda b, g: (b, g, 0, 0)),
                  pl.BlockSpec((None, None, nh, s), lambda b, g: (b, G + g, 0, 0)),
                  pl.BlockSpec((None, nh, 2), lambda b, g: (g, 0, 0)),
                  pl.BlockSpec((None, nh, 2), lambda b, g: (G + g, 0, 0)),
                  pl.BlockSpec((None, 1, gw), lambda b, g: (g, 0, 0)),
                  pl.BlockSpec((q, 2 * q), lambda b, g: (0, 0))],
        out_specs=pl.BlockSpec((None, s, gw), lambda b, g: (b, 0, g)),
        out_shape=jax.ShapeDtypeStruct((bsz, s, d_inner), BF16),
        scratch_shapes=[pltpu.VMEM((2 * nh, s), F32), pltpu.VMEM((2 * nh, s), F32),
                        pltpu.VMEM((4 * nh, s), F32),
                        pltpu.VMEM((s, q), BF16), pltpu.VMEM((nc, q, 2 * q), BF16),
                        pltpu.VMEM((gw, s), BF16), pltpu.VMEM((gw, s), BF16), pltpu.VMEM((gw, s), BF16),
                        pltpu.VMEM((gw, s), F32),
                        pltpu.VMEM((gw, D_STATE), F32), pltpu.VMEM((gw, D_STATE), F32)],
        compiler_params=_cparams("parallel", "parallel"),
        name="ssd_scan",
    )(xs, bc, bc, dt_row, dt_row, p_row, p_row, dx, trio)


def _gated_proj_route_kernel(y_ref, z_ref, gw_ref, w_ref, x_ref, g_ref,
                             nw_ref, sh_ref, sc_ref, rw_ref, rb_ref, lt_ref,
                             o_ref, ld_ref, wgt_ref, cnt_ref, acc_ref):
    _gated_proj_kernel(y_ref, z_ref, gw_ref, w_ref, x_ref, g_ref, o_ref, acc_ref)
    h = _modnorm(o_ref[...], nw_ref[...], sh_ref[...], sc_ref[...])
    rows, gates, total = _route_tile(h, rw_ref[...], rb_ref[...], lt_ref[...])
    ld_ref[...] = rows
    wgt_ref[...] = gates
    cnt_ref[...] = total[:, 0:cnt_ref.shape[1]]


def _gated_proj_kernel(y_ref, z_ref, gw_ref, w_ref, x_ref, g_ref, o_ref, acc_ref):
    k = y_ref.shape[1]
    ss = jnp.zeros((y_ref.shape[0], 1), F32)
    for lo in range(0, k, PROJ_CHUNK):
        hi = lo + PROJ_CHUNK
        t = y_ref[:, lo:hi].astype(F32) * _silu(z_ref[:, lo:hi].astype(F32))
        ss = ss + jnp.sum(t * t, axis=-1, keepdims=True)
        contrib = jnp.dot((t * gw_ref[:, lo:hi]).astype(BF16), w_ref[lo:hi, :], preferred_element_type=F32)
        if lo == 0:
            acc_ref[...] = contrib
        else:
            acc_ref[...] += contrib
    o_ref[...] = x_ref[...] + g_ref[...] * (acc_ref[...] * lax.rsqrt(ss * (1.0 / k) + EPS))


def gated_proj_route(y, z, gw, w, x, g, nw, sh, sc, router_w, router_b, *, ts):
    bsz, s, d = x.shape
    k = y.shape[2]
    ne = router_w.shape[1]
    nt = s // ts
    cum = jnp.asarray(np.concatenate([np.triu(np.ones((ts, ts), np.float32)), np.ones((ts, ts), np.float32)],
                                     axis=1), BF16)
    vec = pl.BlockSpec((None, 1, d), lambda b, i: (b, 0, 0))
    tok_spec = pl.BlockSpec((TOP_K, ts), lambda b, i: (0, b * nt + i))
    call = pl.pallas_call(
        _gated_proj_route_kernel,
        grid=(bsz, nt),
        in_specs=[pl.BlockSpec((None, ts, k), lambda b, i: (b, i, 0)),
                  pl.BlockSpec((None, ts, k), lambda b, i: (b, i, 0)),
                  pl.BlockSpec((1, k), lambda b, i: (0, 0)),
                  _resident((k, d)),
                  pl.BlockSpec((None, ts, d), lambda b, i: (b, i, 0)),
                  vec,
                  pl.BlockSpec((1, d), lambda b, i: (0, 0)), vec, vec,
                  _resident((ne, d)), pl.BlockSpec((ne, 1), lambda b, i: (0, 0)), _resident((ts, 2 * ts))],
        out_specs=[pl.BlockSpec((None, ts, d), lambda b, i: (b, i, 0)), tok_spec, tok_spec,
                   pl.BlockSpec((None, ne, 128), lambda b, i: (b * nt + i, 0, 0))],
        out_shape=[jax.ShapeDtypeStruct((bsz, s, d), F32),
                   jax.ShapeDtypeStruct((TOP_K, bsz * s), jnp.int32),
                   jax.ShapeDtypeStruct((TOP_K, bsz * s), F32),
                   jax.ShapeDtypeStruct((bsz * nt, ne, 128), jnp.int32)],
        scratch_shapes=[pltpu.VMEM((ts, d), F32)],
        compiler_params=_cparams("parallel", "parallel"),
        name="ssd_out_proj_route",
    )
    x_new, ldest, wgt, cnt = call(y, z, gw.reshape(1, k), w, x, g, nw.reshape(1, d), sh, sc,
                                  router_w.T, router_b.reshape(ne, 1), cum)
    return x_new, ldest, wgt, cnt[:, :, 0]


SEG_ROWS = 16
SEG_FIELDS = 3


def _route_tile(h, rw_t, rb_col, cum):
    def nt(a, b):
        return lax.dot_general(a, b, (((1,), (1,)), ((), ())), preferred_element_type=F32)

    h_hi = h.astype(BF16)
    h_lo = (h - h_hi.astype(F32)).astype(BF16)
    rw_hi = rw_t.astype(BF16)
    rw_lo = (rw_t - rw_hi.astype(F32)).astype(BF16)
    logits = nt(rw_hi, h_hi) + (nt(rw_hi, h_lo) + nt(rw_lo, h_hi)) + rb_col
    ne, ts = logits.shape
    eid = lax.broadcasted_iota(jnp.int32, (ne, ts), 0)
    m1 = jnp.max(logits, axis=0, keepdims=True)
    i1 = jnp.min(jnp.where(logits == m1, eid, ne), axis=0, keepdims=True)
    rest = jnp.where(eid == i1, -jnp.inf, logits)
    m2 = jnp.max(rest, axis=0, keepdims=True)
    i2 = jnp.min(jnp.where(rest == m2, eid, ne), axis=0, keepdims=True)
    e2 = jnp.exp(m2 - m1)
    w1 = 1.0 / (1.0 + e2)
    w2 = e2 / (1.0 + e2)
    oh1 = (eid == i1).astype(F32)
    oh2 = (eid == i2).astype(F32)
    chosen = oh1 + oh2
    both = jnp.dot(chosen.astype(BF16), cum, preferred_element_type=F32)
    before = both[:, 0:ts] - chosen
    total = both[:, ts:].astype(jnp.int32)
    seg_len = jnp.bitwise_and(total + (SEG_ROWS - 1), -SEG_ROWS).astype(F32)
    rows = [jnp.sum(jnp.where(eid < idx, seg_len, 0.0) + onehot * before, axis=0, keepdims=True)
            for onehot, idx in ((oh1, i1), (oh2, i2))]
    return jnp.concatenate(rows, axis=0).astype(jnp.int32), jnp.concatenate([w1, w2], axis=0), total


def _segment_copies(seg_ref, tile, n_experts, make_copy, *, wait):
    for e in range(n_experts):
        base = (tile * n_experts + e) * SEG_FIELDS
        local0 = seg_ref[base]
        global0 = seg_ref[base + 1]

        def body(i, carry, local0=local0, global0=global0):
            cp = make_copy(pl.multiple_of(local0 + i * SEG_ROWS, SEG_ROWS),
                           pl.multiple_of(global0 + i * SEG_ROWS, SEG_ROWS))
            if wait:
                cp.wait()
            else:
                cp.start()
            return carry

        lax.fori_loop(0, seg_ref[base + 2], body, 0)


def _dispatch_kernel(seg_ref, x_ref, nw_ref, sh_ref, sc_ref, ld_ref, hs_ref, buf_ref, zero_ref, sem,
                     *, n_token_tiles):
    tt = x_ref.shape[0]
    lc = buf_ref.shape[1]
    ne = N_EXPERTS
    tile = pl.program_id(0) * pl.num_programs(1) + pl.program_id(1)
    last = pl.num_programs(0) * pl.num_programs(1) - 1
    slot = tile % 2

    def copies(t, sl, wait):
        def make_copy(lo, go):
            return pltpu.make_async_copy(buf_ref.at[sl, pl.ds(lo, SEG_ROWS), :],
                                         hs_ref.at[pl.ds(go, SEG_ROWS), :], sem.at[sl])
        _segment_copies(seg_ref, t, ne, make_copy, wait=wait)

    h = _modnorm(x_ref[...], nw_ref[...], sh_ref[...], sc_ref[...]).astype(BF16)
    ld = ld_ref[...]
    rows = lax.broadcasted_iota(jnp.int32, (lc, tt), 0)
    perm = jnp.where(rows == ld[0:1, :], 1.0, jnp.where(rows == ld[1:2, :], 1.0, 0.0)).astype(BF16)
    buf_ref[slot] = jnp.dot(perm, h, preferred_element_type=F32).astype(BF16)
    copies(tile, slot, wait=False)

    @pl.when(tile > 0)
    def _():
        copies(tile - 1, 1 - slot, wait=True)

    @pl.when(tile == last)
    def _():
        copies(tile, slot, wait=True)
        zero_ref[...] = jnp.zeros_like(zero_ref)
        tails = n_token_tiles * ne * SEG_FIELDS
        for wait in (False, True):
            for e in range(ne):
                start = seg_ref[tails + 2 * e]

                def body(i, carry, start=start, wait=wait):
                    cp = pltpu.make_async_copy(
                        zero_ref, hs_ref.at[pl.ds(pl.multiple_of(start + i * SEG_ROWS, SEG_ROWS), SEG_ROWS), :],
                        sem.at[2])
                    if wait:
                        cp.wait()
                    else:
                        cp.start()
                    return carry

                lax.fori_loop(0, seg_ref[tails + 2 * e + 1], body, 0)


def moe_dispatch(x, nw, sh, sc, seg, ldest_rows, n_rows, *, tt, lc):
    bsz, s, d = x.shape
    nt = s // tt
    grid_spec = pltpu.PrefetchScalarGridSpec(
        num_scalar_prefetch=1,
        grid=(bsz, nt),
        in_specs=[pl.BlockSpec((None, tt, d), lambda b, i, sref: (b, i, 0)),
                  pl.BlockSpec((1, d), lambda b, i, sref: (0, 0)),
                  pl.BlockSpec((None, 1, d), lambda b, i, sref: (b, 0, 0)),
                  pl.BlockSpec((None, 1, d), lambda b, i, sref: (b, 0, 0)),
                  pl.BlockSpec((TOP_K, tt), lambda b, i, sref: (0, b * nt + i))],
        out_specs=pl.BlockSpec(memory_space=pl.ANY),
        scratch_shapes=[pltpu.VMEM((2, lc, d), BF16), pltpu.VMEM((SEG_ROWS, d), BF16),
                        pltpu.SemaphoreType.DMA((3,))],
    )
    return pl.pallas_call(
        functools.partial(_dispatch_kernel, n_token_tiles=bsz * nt),
        grid_spec=grid_spec,
        out_shape=jax.ShapeDtypeStruct((n_rows, d), BF16),
        compiler_params=_cparams("arbitrary", "arbitrary"),
        name="moe_dispatch",
    )(seg, x, nw.reshape(1, d), sh, sc, ldest_rows)


def _moe_kernel(te_ref, nu_ref, hs_ref, w1_ref, w3_ref, w2_ref, o_ref, acc_ref):
    i = pl.program_id(0)
    f = pl.program_id(1)

    @pl.when(i < nu_ref[0])
    def _():
        @pl.when(f == 0)
        def _():
            acc_ref[...] = jnp.zeros_like(acc_ref)

        h = hs_ref[...]
        for lo in range(0, w1_ref.shape[1], MOE_HIDDEN_CHUNK):
            hi = min(lo + MOE_HIDDEN_CHUNK, w1_ref.shape[1])
            a = jnp.dot(h, w1_ref[:, lo:hi].astype(BF16), preferred_element_type=F32)
            b = jnp.dot(h, w3_ref[:, lo:hi].astype(BF16), preferred_element_type=F32)
            t = (_silu(a) * b).astype(BF16)
            acc_ref[...] += jnp.dot(t, w2_ref[lo:hi, :].astype(BF16), preferred_element_type=F32)

        @pl.when(f == pl.num_programs(1) - 1)
        def _():
            o_ref[...] = acc_ref[...].astype(o_ref.dtype)

    @pl.when((i >= nu_ref[0]) & (f == 0))
    def _():
        o_ref[...] = jnp.zeros_like(o_ref)


def moe_experts(hs, tile_expert, n_used, w1, w3, w2, *, tm, tf):
    n_rows, d = hs.shape
    dff = w1.shape[2]
    nf = dff // tf
    n_tiles = n_rows // tm

    def last_used(i, nu):
        return jnp.maximum(jnp.minimum(i, nu[0] - 1), 0)

    def row_map(i, f, te, nu):
        return (last_used(i, nu), 0)

    def hidden_block(i, f, nu):
        t = last_used(i, nu)
        step = jnp.where(i < nu[0], f, nf - 1)
        return jnp.where(t % 2 == 0, step, nf - 1 - step)

    def w_in_map(i, f, te, nu):
        return (te[last_used(i, nu)], 0, hidden_block(i, f, nu))

    def w_out_map(i, f, te, nu):
        return (te[last_used(i, nu)], hidden_block(i, f, nu), 0)

    grid_spec = pltpu.PrefetchScalarGridSpec(
        num_scalar_prefetch=2,
        grid=(n_tiles, nf),
        in_specs=[pl.BlockSpec((tm, d), row_map),
                  pl.BlockSpec((None, d, tf), w_in_map),
                  pl.BlockSpec((None, d, tf), w_in_map),
                  pl.BlockSpec((None, tf, d), w_out_map)],
        out_specs=pl.BlockSpec((tm, d), lambda i, f, te, nu: (i, 0)),
        scratch_shapes=[pltpu.VMEM((tm, d), F32)],
    )
    return pl.pallas_call(
        _moe_kernel,
        grid_spec=grid_spec,
        out_shape=jax.ShapeDtypeStruct((n_rows, d), BF16),
        compiler_params=_cparams("arbitrary", "arbitrary"),
        name="moe_experts",
    )(tile_expert, n_used, hs, w1, w3, w2)


def _combine_kernel(seg_ref, ys_ref, x_ref, g_ref, wgt_ref, ld_ref, o_ref, buf_ref, sem):
    tt = x_ref.shape[0]
    lc = buf_ref.shape[1]
    ne = N_EXPERTS
    tile = pl.program_id(0) * pl.num_programs(1) + pl.program_id(1)
    last = pl.num_programs(0) * pl.num_programs(1) - 1
    slot = tile % 2

    def copies(t, sl, wait):
        def make_copy(lo, go):
            return pltpu.make_async_copy(ys_ref.at[pl.ds(go, SEG_ROWS), :],
                                         buf_ref.at[sl, pl.ds(lo, SEG_ROWS), :], sem.at[sl])
        _segment_copies(seg_ref, t, ne, make_copy, wait=wait)

    @pl.when(tile == 0)
    def _():
        buf_ref[...] = jnp.zeros_like(buf_ref)
        copies(tile, slot, wait=False)

    @pl.when(tile < last)
    def _():
        copies(tile + 1, 1 - slot, wait=False)

    ld = ld_ref[...]
    cols = lax.broadcasted_iota(jnp.int32, (tt, lc), 1)
    pick = jnp.concatenate([jnp.where(cols == ld[:, k:k + 1], 1.0, 0.0) for k in range(TOP_K)],
                           axis=0).astype(BF16)
    copies(tile, slot, wait=True)
    z = jnp.dot(pick, buf_ref[slot], preferred_element_type=F32)
    w = wgt_ref[...]
    mix = w[:, 0:1] * z[0:tt] + w[:, 1:2] * z[tt:]
    o_ref[...] = x_ref[...] + g_ref[...] * mix


def moe_combine(ys, seg, ldest, wgt, x, g, *, tt, lc):
    bsz, s, d = x.shape
    nt = s // tt
    tok_spec = pl.BlockSpec((tt, TOP_K), lambda b, i, sref: (b * nt + i, 0))
    grid_spec = pltpu.PrefetchScalarGridSpec(
        num_scalar_prefetch=1,
        grid=(bsz, nt),
        in_specs=[pl.BlockSpec(memory_space=pl.ANY),
                  pl.BlockSpec((None, tt, d), lambda b, i, sref: (b, i, 0)),
                  pl.BlockSpec((None, 1, d), lambda b, i, sref: (b, 0, 0)),
                  tok_spec, tok_spec],
        out_specs=pl.BlockSpec((None, tt, d), lambda b, i, sref: (b, i, 0)),
        scratch_shapes=[pltpu.VMEM((2, lc, d), BF16), pltpu.SemaphoreType.DMA((2,))],
    )
    return pl.pallas_call(
        _combine_kernel,
        grid_spec=grid_spec,
        out_shape=jax.ShapeDtypeStruct((bsz, s, d), F32),
        compiler_params=_cparams("arbitrary", "arbitrary"),
        name="moe_combine",
    )(seg, ys, x, g, wgt, ldest)


def _round_up(v, m):
    return ((v + m - 1) // m) * m


def moe_token_tile(s):
    return min(512, s)


def moe_residual(x, nw, sh, sc, g, ldest, wgt, cnt, w1, w3, w2, *, tm=MOE_TILE_ROWS):
    bsz, s, d = x.shape
    n_tok = bsz * s
    ne = w1.shape[0]
    tt = moe_token_tile(s)
    n_tt = n_tok // tt
    lc = _round_up(TOP_K * tt + ne * SEG_ROWS, 128)
    seg_len = _round_up(cnt.reshape(n_tt, ne), SEG_ROWS)
    local_start = jnp.cumsum(seg_len, axis=1) - seg_len
    padded = _round_up(jnp.sum(seg_len, axis=0), tm)
    ends = jnp.cumsum(padded)
    global_start = (ends - padded)[None, :] + jnp.cumsum(seg_len, axis=0) - seg_len
    n_rows = _round_up(n_tok * TOP_K + n_tt * ne * SEG_ROWS + ne * tm, tm)
    used_end = (ends - padded) + jnp.sum(seg_len, axis=0)
    next_start = jnp.concatenate([ends[:-1], jnp.full((1,), n_rows, ends.dtype)])
    tails = jnp.stack([used_end, (next_start - used_end) // SEG_ROWS], axis=-1)
    seg = jnp.concatenate([jnp.stack([local_start, global_start, seg_len // SEG_ROWS], axis=-1).reshape(-1),
                           tails.reshape(-1)]).astype(jnp.int32)
    n_tiles = n_rows // tm
    tile_start = jnp.arange(n_tiles, dtype=jnp.int32) * tm
    tile_expert = jnp.minimum(jnp.sum(tile_start[:, None] >= ends[None, :], axis=1), ne - 1).astype(jnp.int32)
    n_used = (ends[ne - 1:ne] // tm).astype(jnp.int32)
    hs = moe_dispatch(x, nw, sh, sc, seg, ldest, n_rows, tt=tt, lc=lc)
    ys = moe_experts(hs, tile_expert, n_used, w1, w3, w2, tm=tm, tf=w1.shape[2] // 2)
    return moe_combine(ys, seg, ldest.T, wgt.T, x, g, tt=tt, lc=lc)


def _split_mod(mod):
    return [mod[k] for k in range(mod.shape[0])]


def even_layer(x, c, rel_bias, ada_w, ada_b, norm1_w, in_w, q_norm_w, k_norm_w, sink, out_w,
               norm2_w, w1, w3, w2):
    s = x.shape[1]
    sh1, sc1, g1, sh2, sc2, g2 = _split_mod(ada_mod(c, ada_w, ada_b))
    proj = norm_mod_matmul(x, norm1_w, sh1, sc1, in_w.astype(BF16), ts=min(1024, s), tn=256, name="even_in_proj",
                           out_dtype=BF16)
    yf = fourier_mix(proj, tq=min(512, s))
    ya = window_attention(proj, band_bias(rel_bias), q_norm_w, k_norm_w, sink)
    x = cat_proj_residual(yf, ya, out_w.astype(BF16), x, g1, ts=min(1024, s))
    return ffn_residual(x, norm2_w, sh2, sc2, g2, w1.astype(BF16), w3.astype(BF16), w2,
                        ts=min(1024, s), tf=256)


def odd_layer(x, c, ada_w, ada_b, norm1_w, in_w, conv_w, conv_b, dt_bias_f, dt_bias_b, a_log_f, a_log_b,
              d_skip, gnorm_w, out_w, norm2_w, router_w, router_b, w1, w3, w2):
    s = x.shape[1]
    sh1, sc1, g1, sh2, sc2, g2 = _split_mod(ada_mod(c, ada_w, ada_b))
    d_inner = gnorm_w.shape[0]
    z, xs, bc, dt = ssm_in_proj_conv(x, norm1_w, sh1, sc1, in_w.astype(BF16), conv_w, conv_b,
                                     dz=d_inner, n_x=d_inner, ts=min(1024, s))
    y = ssd_scan_bidir(xs, bc, dt, dt_bias_f, dt_bias_b, a_log_f, a_log_b, d_skip)
    x, ldest, wgt, cnt = gated_proj_route(y, z, gnorm_w, out_w.astype(BF16), x, g1, norm2_w, sh2, sc2,
                                          router_w, router_b, ts=moe_token_tile(s))
    return moe_residual(x, norm2_w, sh2, sc2, g2, ldest, wgt, cnt,
                        w1, w3, w2)


def kernel(x, c, rel_bias, ev_ada_w, ev_ada_b, ev_norm1_w, ev_in_w, ev_q_norm_w, ev_k_norm_w, ev_sink, ev_out_w, ev_norm2_w, ev_ffn_w1, ev_ffn_w3, ev_ffn_w2, od_ada_w, od_ada_b, od_norm1_w, od_in_w, od_conv_w, od_conv_b, od_dt_bias_f, od_dt_bias_b, od_A_log_f, od_A_log_b, od_D, od_gnorm_w, od_out_w, od_norm2_w, od_router_w, od_router_b, od_moe_w1, od_moe_w3, od_moe_w2):
    depth = ev_ada_w.shape[0] + od_ada_w.shape[0]
    for i in range(depth):
        j = i // 2
        if i % 2 == 0:
            x = even_layer(x, c, rel_bias, ev_ada_w[j], ev_ada_b[j], ev_norm1_w[j], ev_in_w[j],
                           ev_q_norm_w[j], ev_k_norm_w[j], ev_sink[j], ev_out_w[j], ev_norm2_w[j],
                           ev_ffn_w1[j], ev_ffn_w3[j], ev_ffn_w2[j])
        else:
            x = odd_layer(x, c, od_ada_w[j], od_ada_b[j], od_norm1_w[j], od_in_w[j], od_conv_w[j],
                          od_conv_b[j], od_dt_bias_f[j], od_dt_bias_b[j], od_A_log_f[j], od_A_log_b[j],
                          od_D[j], od_gnorm_w[j], od_out_w[j], od_norm2_w[j], od_router_w[j],
                          od_router_b[j], od_moe_w1[j], od_moe_w3[j], od_moe_w2[j])
    return x
```

```python
import functools

import numpy as np
import jax
import jax.numpy as jnp
from jax import lax
from jax.experimental import pallas as pl
from jax.experimental.pallas import tpu as pltpu

F32 = jnp.float32
BF16 = jnp.bfloat16

EPS = 1e-6
FNET_GROUPS = 4
FNET_GROUP_DIM = 128
FNET_WIDTH = FNET_GROUPS * FNET_GROUP_DIM
ATTN_HEADS = 8
ATTN_KV_HEADS = 2
HEAD_DIM = 64
ATTN_WIDTH = ATTN_HEADS * HEAD_DIM
KV_WIDTH = ATTN_KV_HEADS * HEAD_DIM
WINDOW = 128
BLOCK = 128
REL_BUCKETS = 32
REL_MAX_DIST = 128
SSM_HEAD_DIM = 64
SSM_GROUPS = 4
D_STATE = 128
CONV_WIDTH = 5
SSD_CHUNK = 128
N_EXPERTS = 8
TOP_K = 2
NEG_BIG = -1e30
ATTN_QBLOCKS = 16

V7X_VMEM_LIMIT_BYTES = 56 * 1024 * 1024
MOE_TILE_ROWS = 768
MOE_HIDDEN_CHUNK = 256


def _cparams(*sem):
    return pltpu.CompilerParams(dimension_semantics=sem, vmem_limit_bytes=V7X_VMEM_LIMIT_BYTES)


def _modnorm(x, nw, sh, sc):
    ms = jnp.mean(x * x, axis=-1, keepdims=True)
    return x * lax.rsqrt(ms + EPS) * nw * (1.0 + sc) + sh


def _silu(x):
    return x * (1.0 / (1.0 + jnp.exp(-x)))


def _ada_kernel(c_ref, w_ref, b_ref, o_ref):
    cs = _silu(c_ref[...]).astype(BF16)
    val = jnp.dot(cs, w_ref[...].astype(BF16), preferred_element_type=F32) + b_ref[...]
    for row in range(val.shape[0]):
        o_ref[row] = val[row:row + 1, :]


def ada_mod(c, w, b, *, n_chunks=6):
    bsz, d = c.shape
    n = w.shape[1]
    tn = n // n_chunks
    return pl.pallas_call(
        _ada_kernel,
        grid=(n_chunks,),
        in_specs=[pl.BlockSpec((bsz, d), lambda j: (0, 0)),
                  pl.BlockSpec((d, tn), lambda j: (0, j)),
                  pl.BlockSpec((1, tn), lambda j: (0, j))],
        out_specs=pl.BlockSpec((None, bsz, 1, tn), lambda j: (j, 0, 0, 0)),
        out_shape=jax.ShapeDtypeStruct((n_chunks, bsz, 1, tn), F32),
        compiler_params=_cparams("arbitrary"),
        name="ada_mod",
    )(c, w, b.reshape(1, n))


def _nmm_kernel(x_ref, nw_ref, sh_ref, sc_ref, w_ref, o_ref, *, tn):
    h = _modnorm(x_ref[...], nw_ref[...], sh_ref[...], sc_ref[...]).astype(BF16)
    n = w_ref.shape[1]
    for lo in range(0, n, tn):
        o_ref[:, lo:lo + tn] = jnp.dot(h, w_ref[:, lo:lo + tn],
                                       preferred_element_type=F32).astype(o_ref.dtype)


def _resident(shape):
    return pl.BlockSpec(shape, lambda *_: tuple(0 for _ in shape), pipeline_mode=pl.Buffered(1))


def norm_mod_matmul(x, nw, sh, sc, w, *, ts, tn, name, out_dtype=F32):
    bsz, s, d = x.shape
    n = w.shape[1]
    assert n % tn == 0
    return pl.pallas_call(
        functools.partial(_nmm_kernel, tn=tn),
        grid=(bsz, s // ts),
        in_specs=[pl.BlockSpec((None, ts, d), lambda b, i: (b, i, 0)),
                  pl.BlockSpec((1, d), lambda b, i: (0, 0)),
                  pl.BlockSpec((None, 1, d), lambda b, i: (b, 0, 0)),
                  pl.BlockSpec((None, 1, d), lambda b, i: (b, 0, 0)),
                  _resident((d, n))],
        out_specs=pl.BlockSpec((None, ts, n), lambda b, i: (b, i, 0)),
        out_shape=jax.ShapeDtypeStruct((bsz, s, n), out_dtype),
        compiler_params=_cparams("parallel", "parallel"),
        name=name,
    )(x, nw.reshape(1, d), sh, sc, w)


def _dft_cos_sin(n):
    k = np.arange(n, dtype=np.int64)
    ang = ((k[:, None] * k[None, :]) % n).astype(np.float64) * (2.0 * np.pi / n)
    scale = 1.0 / np.sqrt(n)
    return np.cos(ang) * scale, np.sin(ang) * scale


def _fourier_kernel(u_ref, chan_ref, seq_ref, o_ref, ab_ref):
    s = u_ref.shape[0]

    @pl.when(pl.program_id(1) == 0)
    def _():
        for g in range(FNET_GROUPS):
            lo, hi = g * FNET_GROUP_DIM, (g + 1) * FNET_GROUP_DIM
            ug = u_ref[:, lo:hi].astype(BF16)
            cs = jnp.dot(ug, chan_ref[...], preferred_element_type=F32)
            ab_ref[0:s, lo:hi] = cs[:, :FNET_GROUP_DIM].astype(BF16)
            ab_ref[s:2 * s, lo:hi] = cs[:, FNET_GROUP_DIM:].astype(BF16)

    o_ref[...] = jnp.dot(seq_ref[...], ab_ref[...], preferred_element_type=F32).astype(o_ref.dtype)


def fourier_mix(proj, *, tq):
    bsz, s, _ = proj.shape
    cc, sc = _dft_cos_sin(FNET_GROUP_DIM)
    chan = jnp.asarray(np.concatenate([cc, sc], axis=1), BF16)
    cs, ss = _dft_cos_sin(s)
    seq = jnp.asarray(np.concatenate([cs, -ss], axis=1), BF16)
    return pl.pallas_call(
        _fourier_kernel,
        grid=(bsz, s // tq),
        in_specs=[pl.BlockSpec((None, s, FNET_WIDTH), lambda b, i: (b, 0, 0)),
                  pl.BlockSpec((FNET_GROUP_DIM, 2 * FNET_GROUP_DIM), lambda b, i: (0, 0)),
                  pl.BlockSpec((tq, 2 * s), lambda b, i: (i, 0))],
        out_specs=pl.BlockSpec((None, tq, FNET_WIDTH), lambda b, i: (b, i, 0)),
        out_shape=jax.ShapeDtypeStruct((bsz, s, FNET_WIDTH), BF16),
        scratch_shapes=[pltpu.VMEM((2 * s, FNET_WIDTH), BF16)],
        compiler_params=_cparams("parallel", "arbitrary"),
        name="fourier_mix",
    )(proj, chan, seq)


def _band_bucket_table():
    i = np.arange(BLOCK)[:, None]
    j = np.arange(3 * BLOCK)[None, :]
    rel = (j - BLOCK) - i
    half = REL_BUCKETS // 2
    max_exact = half // 2
    n = np.abs(rel)
    large = max_exact + (np.log(np.maximum(n, 1) / max_exact)
                         / np.log(REL_MAX_DIST / max_exact) * (half - max_exact)).astype(np.int32)
    large = np.minimum(large, half - 1)
    bucket = (rel > 0).astype(np.int32) * half + np.where(n < max_exact, n, large)
    return np.where(n <= WINDOW, bucket, -1).astype(np.int32)


def _bias_kernel(rb_ref, bucket_ref, o_ref):
    h = pl.program_id(0)
    bucket = bucket_ref[...]
    acc = jnp.full(bucket.shape, NEG_BIG, F32)
    for bkt in range(REL_BUCKETS):
        acc = jnp.where(bucket == bkt, rb_ref[bkt * ATTN_HEADS + h], acc)
    o_ref[...] = acc


def band_bias(rel_bias):
    bucket = jnp.asarray(_band_bucket_table())
    return pl.pallas_call(
        _bias_kernel,
        grid=(ATTN_HEADS,),
        in_specs=[pl.BlockSpec(memory_space=pltpu.SMEM),
                  pl.BlockSpec((BLOCK, 3 * BLOCK), lambda h: (0, 0))],
        out_specs=pl.BlockSpec((None, BLOCK, 3 * BLOCK), lambda h: (h, 0, 0)),
        out_shape=jax.ShapeDtypeStruct((ATTN_HEADS, BLOCK, 3 * BLOCK), F32),
        compiler_params=_cparams("arbitrary"),
        name="band_bias",
    )(rel_bias.reshape(-1), bucket)


def _head_mean_matrix(width):
    m = np.zeros((width, width), np.float32)
    for h in range(width // HEAD_DIM):
        m[h * HEAD_DIM:(h + 1) * HEAD_DIM, h * HEAD_DIM:(h + 1) * HEAD_DIM] = 1.0 / HEAD_DIM
    return m


def _heads_rms(t, mean_mat, w):
    sq = t * t
    hi = sq.astype(BF16)
    lo = (sq - hi.astype(F32)).astype(BF16)
    ms = (jnp.dot(hi, mean_mat, preferred_element_type=F32)
          + jnp.dot(lo, mean_mat, preferred_element_type=F32))
    return t * lax.rsqrt(ms + EPS) * w


def _attn_kernel(sink_ref, q_ref, kl_ref, kc_ref, kr_ref, vl_ref, vc_ref, vr_ref,
                 bias_ref, qnw_ref, knw_ref, qmean_ref, kmean_ref, o_ref):
    n = pl.program_id(1)
    nb = pl.num_programs(1) * ATTN_QBLOCKS
    k = jnp.concatenate([kl_ref[...], kc_ref[...], kr_ref[...]], axis=0).astype(F32)
    v = jnp.concatenate([vl_ref[...], vc_ref[...], vr_ref[...]], axis=0).astype(F32)
    col = lax.broadcasted_iota(jnp.int32, (1, 3 * BLOCK), 1)
    qn = _heads_rms(q_ref[...].astype(F32), qmean_ref[...], qnw_ref[...])
    kn = _heads_rms(k, kmean_ref[...], knw_ref[...])
    low = lax.broadcasted_iota(jnp.int32, (1, 2 * HEAD_DIM), 1) < HEAD_DIM
    kn_sw = pltpu.roll(kn, HEAD_DIM, axis=1)
    v_sw = pltpu.roll(v, HEAD_DIM, axis=1)
    k_dup = [jnp.where(low, kn, kn_sw).astype(BF16), jnp.where(low, kn_sw, kn).astype(BF16)]
    ones = jnp.ones((k.shape[0], 2 * HEAD_DIM), BF16)
    v_ext = [jnp.concatenate([v.astype(BF16), ones], axis=1),
             jnp.concatenate([v_sw.astype(BF16), ones], axis=1)]
    g = ATTN_HEADS // ATTN_KV_HEADS
    for qb in range(ATTN_QBLOCKS):
        blk = n * ATTN_QBLOCKS + qb
        band = slice(qb * BLOCK, (qb + 3) * BLOCK)
        first_key = jnp.where(blk == 0, BLOCK, 0)
        end_key = jnp.where(blk == nb - 1, 2 * BLOCK, 3 * BLOCK)
        outside = (col < first_key) | (col >= end_key)
        raw = []
        for h in range(ATTN_HEADS):
            m, idx, j = h // 2, h % 2, h // g
            qp = qn[qb * BLOCK:(qb + 1) * BLOCK, m * 2 * HEAD_DIM:(m + 1) * 2 * HEAD_DIM]
            qm = jnp.where(low if idx == 0 else jnp.logical_not(low), qp, 0.0).astype(BF16)
            raw.append(lax.dot_general(qm, k_dup[j][band], (((1,), (1,)), ((), ())),
                                       preferred_element_type=F32))
        probs, tails = [], []
        for h in range(ATTN_HEADS):
            logits = jnp.where(outside, NEG_BIG, raw[h] + bias_ref[h])
            sk = sink_ref[h]
            mx = jnp.maximum(jnp.max(logits, axis=-1, keepdims=True), sk)
            probs.append(jnp.exp(logits - mx).astype(BF16))
            tails.append(jnp.exp(sk - mx))
        res = []
        for h in range(ATTN_HEADS):
            idx, j = h % 2, h // g
            r = jnp.dot(probs[h], v_ext[idx if j == 0 else 1 - idx][band], preferred_element_type=F32)
            res.append(r[:, :2 * HEAD_DIM] / (r[:, 2 * HEAD_DIM:] + tails[h]))
        pairs = [jnp.where(low, res[2 * m], res[2 * m + 1]) for m in range(ATTN_HEADS // 2)]
        o_ref[qb * BLOCK:(qb + 1) * BLOCK, :] = jnp.concatenate(pairs, axis=-1).astype(o_ref.dtype)


def window_attention(proj, bias, q_norm_w, k_norm_w, sink):
    bsz, s, _ = proj.shape
    nb = s // BLOCK
    qcol = FNET_WIDTH // ATTN_WIDTH
    kcol = (FNET_WIDTH + ATTN_WIDTH) // KV_WIDTH
    vcol = kcol + 1

    qb = ATTN_QBLOCKS
    assert nb % qb == 0

    def kv_specs(col):
        return [pl.BlockSpec((None, BLOCK, KV_WIDTH), lambda b, n: (b, jnp.maximum(n * qb - 1, 0), col)),
                pl.BlockSpec((None, qb * BLOCK, KV_WIDTH), lambda b, n: (b, n, col)),
                pl.BlockSpec((None, BLOCK, KV_WIDTH), lambda b, n: (b, jnp.minimum((n + 1) * qb, nb - 1), col))]

    return pl.pallas_call(
        _attn_kernel,
        grid=(bsz, nb // qb),
        in_specs=[pl.BlockSpec(memory_space=pltpu.SMEM),
                  pl.BlockSpec((None, qb * BLOCK, ATTN_WIDTH), lambda b, n: (b, n, qcol)),
                  *kv_specs(kcol), *kv_specs(vcol),
                  pl.BlockSpec((ATTN_HEADS, BLOCK, 3 * BLOCK), lambda b, n: (0, 0, 0)),
                  pl.BlockSpec((1, ATTN_WIDTH), lambda b, n: (0, 0)),
                  pl.BlockSpec((1, KV_WIDTH), lambda b, n: (0, 0)),
                  pl.BlockSpec((ATTN_WIDTH, ATTN_WIDTH), lambda b, n: (0, 0)),
                  pl.BlockSpec((KV_WIDTH, KV_WIDTH), lambda b, n: (0, 0))],
        out_specs=pl.BlockSpec((None, qb * BLOCK, ATTN_WIDTH), lambda b, n: (b, n, 0)),
        out_shape=jax.ShapeDtypeStruct((bsz, s, ATTN_WIDTH), BF16),
        compiler_params=_cparams("parallel", "arbitrary"),
        name="window_attention",
    )(sink, proj, proj, proj, proj, proj, proj, proj, bias,
      (jnp.tile(q_norm_w, ATTN_HEADS) * (HEAD_DIM ** -0.5)).reshape(1, ATTN_WIDTH),
      jnp.tile(k_norm_w, ATTN_KV_HEADS).reshape(1, KV_WIDTH),
      jnp.asarray(_head_mean_matrix(ATTN_WIDTH), BF16), jnp.asarray(_head_mean_matrix(KV_WIDTH), BF16))


def _cat_proj_kernel(a1_ref, a2_ref, w_ref, x_ref, g_ref, o_ref):
    k1 = a1_ref.shape[1]
    y = jnp.dot(a1_ref[...].astype(BF16), w_ref[0:k1, :], preferred_element_type=F32)
    y = y + jnp.dot(a2_ref[...].astype(BF16), w_ref[k1:, :], preferred_element_type=F32)
    o_ref[...] = x_ref[...] + g_ref[...] * y


def cat_proj_residual(a1, a2, w, x, g, *, ts):
    bsz, s, d = x.shape
    k1, k2 = a1.shape[2], a2.shape[2]
    return pl.pallas_call(
        _cat_proj_kernel,
        grid=(bsz, s // ts),
        in_specs=[pl.BlockSpec((None, ts, k1), lambda b, i: (b, i, 0)),
                  pl.BlockSpec((None, ts, k2), lambda b, i: (b, i, 0)),
                  pl.BlockSpec((k1 + k2, d), lambda b, i: (0, 0)),
                  pl.BlockSpec((None, ts, d), lambda b, i: (b, i, 0)),
                  pl.BlockSpec((None, 1, d), lambda b, i: (b, 0, 0))],
        out_specs=pl.BlockSpec((None, ts, d), lambda b, i: (b, i, 0)),
        out_shape=jax.ShapeDtypeStruct((bsz, s, d), F32),
        compiler_params=_cparams("parallel", "parallel"),
        name="mixer_out_proj",
    )(a1, a2, w, x, g)


def _ffn_kernel(x_ref, nw_ref, sh_ref, sc_ref, g_ref, w1_ref, w3_ref, w2_ref, o_ref, acc_ref, *, tf):
    h = _modnorm(x_ref[...], nw_ref[...], sh_ref[...], sc_ref[...]).astype(BF16)
    dff = w1_ref.shape[1]
    for lo in range(0, dff, tf):
        a = jnp.dot(h, w1_ref[:, lo:lo + tf], preferred_element_type=F32)
        b = jnp.dot(h, w3_ref[:, lo:lo + tf], preferred_element_type=F32)
        t = (_silu(a) * b).astype(BF16)
        contrib = jnp.dot(t, w2_ref[lo:lo + tf, :].astype(BF16), preferred_element_type=F32)
        if lo == 0:
            acc_ref[...] = contrib
        else:
            acc_ref[...] += contrib
    o_ref[...] = x_ref[...] + g_ref[...] * acc_ref[...]


def ffn_residual(x, nw, sh, sc, g, w1, w3, w2, *, ts, tf):
    bsz, s, d = x.shape
    dff = w1.shape[1]
    assert dff % tf == 0
    vec = pl.BlockSpec((None, 1, d), lambda b, i: (b, 0, 0))
    return pl.pallas_call(
        functools.partial(_ffn_kernel, tf=tf),
        grid=(bsz, s // ts),
        in_specs=[pl.BlockSpec((None, ts, d), lambda b, i: (b, i, 0)),
                  pl.BlockSpec((1, d), lambda b, i: (0, 0)),
                  vec, vec, vec,
                  _resident((d, dff)), _resident((d, dff)), _resident((dff, d))],
        out_specs=pl.BlockSpec((None, ts, d), lambda b, i: (b, i, 0)),
        out_shape=jax.ShapeDtypeStruct((bsz, s, d), F32),
        scratch_shapes=[pltpu.VMEM((ts, d), F32)],
        compiler_params=_cparams("parallel", "parallel"),
        name="ffn_swiglu",
    )(x, nw.reshape(1, d), sh, sc, g, w1, w3, w2)


HALO_ROWS = 16
PROJ_CHUNK = 256
W_VIEW = 1024


def _proj_conv_kernel(x_ref, xp_ref, xn_ref, nw_ref, sh_ref, sc_ref, *rest, n_views):
    w_views = rest[:n_views]
    wd_ref, cw_ref, cb_ref, z_ref, xs_ref, bc_ref, dt_ref = rest[n_views:]
    i = pl.program_id(1)
    ts = x_ref.shape[0]
    half = CONV_WIDTH // 2
    dz = z_ref.shape[1]
    dc = cw_ref.shape[1]

    def wcols(c0):
        return w_views[c0 // W_VIEW][:, c0 % W_VIEW:c0 % W_VIEW + PROJ_CHUNK]

    def hnorm(ref):
        return _modnorm(ref[...], nw_ref[...], sh_ref[...], sc_ref[...])

    h = hnorm(x_ref).astype(BF16)
    h_prev = jnp.where(i == 0, 0.0, hnorm(xp_ref)).astype(BF16)
    h_next = jnp.where(i == pl.num_programs(1) - 1, 0.0, hnorm(xn_ref)).astype(BF16)
    h_ext = jnp.concatenate([h_prev, h, h_next], axis=0)

    n_x = xs_ref.shape[1]
    rows = h_ext.shape[0]

    def project(lo):
        return jnp.dot(h_ext, wcols(dz + lo), preferred_element_type=F32)

    def z_chunk(lo):
        hi = lo + PROJ_CHUNK
        z_ref[:, lo:hi] = jnp.dot(h, wcols(lo), preferred_element_type=F32).astype(z_ref.dtype)

    z_los = list(range(0, dz, PROJ_CHUNK))
    c_los = list(range(0, dc, PROJ_CHUNK))
    p_next = project(c_los[0])
    for n, lo in enumerate(c_los):
        hi = lo + PROJ_CHUNK
        p = p_next
        if n + 1 < len(c_los):
            p_next = project(c_los[n + 1])
        if z_los:
            z_chunk(z_los.pop(0))
        acc = jnp.zeros((ts, PROJ_CHUNK), F32) + cb_ref[:, lo:hi]
        for kk in range(CONV_WIDTH):
            shifted = p if kk == half else pltpu.roll(p, (half - kk) % rows, axis=0)
            acc = acc + shifted[HALO_ROWS:HALO_ROWS + ts, :] * cw_ref[kk:kk + 1, lo:hi]
        out = _silu(acc)
        if lo < n_x:
            xs_ref[:, lo:hi] = out
        else:
            bc_ref[:, lo - n_x:hi - n_x] = out.astype(bc_ref.dtype)
    for lo in z_los:
        z_chunk(lo)
    dt_ref[...] = lax.dot_general(wd_ref[...], h, (((1,), (1,)), ((), ())), preferred_element_type=F32)


def ssm_in_proj_conv(x, nw, sh, sc, w_in, conv_w, conv_b, *, dz, n_x, ts):
    bsz, s, d = x.shape
    dc = conv_w.shape[1]
    ddt = w_in.shape[1] - dz - dc
    assert ts % HALO_ROWS == 0 and n_x % PROJ_CHUNK == 0 and dc % PROJ_CHUNK == 0 and dz % PROJ_CHUNK == 0
    assert dz % W_VIEW == 0 and dc % W_VIEW == 0 and W_VIEW % PROJ_CHUNK == 0
    n_views = (dz + dc) // W_VIEW
    views = [pl.BlockSpec((d, W_VIEW), lambda b, i, k=k: (0, k), pipeline_mode=pl.Buffered(1))
             for k in range(n_views)]
    r = ts // HALO_ROWS
    last = s // HALO_ROWS - 1
    vec = pl.BlockSpec((None, 1, d), lambda b, i: (b, 0, 0))
    return pl.pallas_call(
        functools.partial(_proj_conv_kernel, n_views=n_views),
        grid=(bsz, s // ts),
        in_specs=[pl.BlockSpec((None, ts, d), lambda b, i: (b, i, 0)),
                  pl.BlockSpec((None, HALO_ROWS, d), lambda b, i: (b, jnp.maximum(i * r - 1, 0), 0)),
                  pl.BlockSpec((None, HALO_ROWS, d), lambda b, i: (b, jnp.minimum((i + 1) * r, last), 0)),
                  pl.BlockSpec((1, d), lambda b, i: (0, 0)), vec, vec,
                  *views, _resident((ddt, d)),
                  _resident((CONV_WIDTH, dc)), _resident((1, dc))],
        out_specs=[pl.BlockSpec((None, ts, dz), lambda b, i: (b, i, 0)),
                   pl.BlockSpec((None, ts, n_x), lambda b, i: (b, i, 0)),
                   pl.BlockSpec((None, ts, dc - n_x), lambda b, i: (b, i, 0)),
                   pl.BlockSpec((None, ddt, ts), lambda b, i: (b, 0, i))],
        out_shape=[jax.ShapeDtypeStruct((bsz, s, dz), BF16),
                   jax.ShapeDtypeStruct((bsz, s, n_x), F32),
                   jax.ShapeDtypeStruct((bsz, s, dc - n_x), BF16),
                   jax.ShapeDtypeStruct((bsz, ddt, s), F32)],
        compiler_params=_cparams("parallel", "parallel"),
        name="odd_in_proj_conv",
    )(x, x, x, nw.reshape(1, d), sh, sc, *([w_in] * n_views), w_in[:, dz + dc:].T, conv_w, conv_b.reshape(1, dc))


def _softplus(x):
    return jnp.maximum(x, 0.0) + jnp.log(1.0 + jnp.exp(-jnp.abs(x)))


LOG2E = 1.4426950408889634
DECAY_SLOTS = 12


def _bf16_parts3(v):
    hi = v.astype(BF16).astype(F32)
    r = v - hi
    mid = r.astype(BF16).astype(F32)
    lo = (r - mid).astype(BF16).astype(F32)
    return hi, mid, lo


def _ssd_t_kernel(x_ref, b_ref, c_ref, dt_f_ref, dt_b_ref, p_f_ref, p_b_ref, dx_ref, trio_ref, y_ref,
                  ar_ref, dr_ref, sc_ref, pq_ref, qt_ref, xtb_ref, xdf_ref, xdb_ref, yt_ref, hf_ref, hb_ref):
    s = x_ref.shape[0]
    q = SSD_CHUNK
    nh = p_f_ref.shape[0]
    nc = s // q
    hd = SSM_HEAD_DIM

    def row_params(raw_ref, p_ref):
        dt = _softplus(raw_ref[...] + p_ref[:, 0:1])
        return dt, (-LOG2E) * jnp.exp(p_ref[:, 1:2]) * dt

    dtf, af = row_params(dt_f_ref, p_f_ref)
    dtb, ab = row_params(dt_b_ref, p_b_ref)
    dr_ref[0:nh, :] = dtf
    dr_ref[nh:, :] = dtb
    ar_ref[0:nh, :] = af
    ar_ref[nh:, :] = ab

    li = lax.broadcasted_iota(jnp.int32, (q, q), 0)
    si = lax.broadcasted_iota(jnp.int32, (q, q), 1)
    lower = li >= si
    upper = li <= si
    slot_head = lax.broadcasted_iota(jnp.int32, (1, q), 1) % nh

    def bdot(a, b):
        return jnp.dot(a, b, preferred_element_type=F32)

    def ntdot(a, b):
        return lax.dot_general(a, b, (((1,), (1,)), ((), ())), preferred_element_type=F32)

    def head_rows(v):
        return jnp.concatenate([jnp.broadcast_to(v[h:h + 1, :], (hd, q)) for h in range(nh)], axis=0)

    ones = jnp.ones((nh, q), F32)
    zeros = jnp.zeros((nh, q), F32)

    def prep_body(c, carry):
        sl = pl.ds(pl.multiple_of(c * q, q), q)
        a_row = ar_ref[:, sl]
        d_row = dr_ref[:, sl]
        parts = jnp.concatenate([p.astype(BF16) for p in _bf16_parts3(a_row)], axis=0)
        cs = bdot(parts, trio_ref[...])
        cs = cs[0:2 * nh] + cs[2 * nh:4 * nh] + cs[4 * nh:6 * nh]
        i_f = cs[0:nh, 0:q]
        e_b = cs[nh:, 0:q] - a_row[nh:]
        tot_f = cs[0:nh, q:]
        tot_b = cs[nh:, q:]
        ih, im, il = _bf16_parts3(i_f)
        eh, em, el = _bf16_parts3(e_b)
        pad = [zeros] * (q // nh - DECAY_SLOTS)
        p_t = jnp.concatenate([ih, im, il, ones, ones, ones, -eh, -em, -el, ones, ones, ones] + pad, axis=0)
        qf_t = jnp.concatenate([ones, ones, ones, -ih, -im, -il] + [zeros] * 6 + pad, axis=0)
        qb_t = jnp.concatenate([zeros] * 6 + [ones, ones, ones, eh, em, el] + pad, axis=0)
        pq_ref[sl, :] = p_t.T.astype(BF16)
        qt_ref[c] = jnp.concatenate([qf_t, qb_t], axis=1).astype(BF16)
        sc_ref[0 * nh:1 * nh, sl] = jnp.exp2(i_f)
        sc_ref[1 * nh:2 * nh, sl] = jnp.exp2(tot_b - e_b)
        sc_ref[2 * nh:3 * nh, sl] = jnp.exp2(tot_f)
        sc_ref[3 * nh:4 * nh, sl] = jnp.exp2(tot_b)
        xt = x_ref[sl, :].T
        xtb_ref[:, sl] = xt.astype(BF16)
        xdf_ref[:, sl] = (xt * head_rows(jnp.exp2(tot_f - i_f) * d_row[0:nh])).astype(BF16)
        xdb_ref[:, sl] = (xt * head_rows(jnp.exp2(e_b) * d_row[nh:])).astype(BF16)
        return carry

    lax.fori_loop(0, nc, prep_body, 0, unroll=8)

    hf_ref[...] = jnp.zeros_like(hf_ref)
    hb_ref[...] = jnp.zeros_like(hb_ref)
    zero_half = jnp.zeros((hd, q), BF16)

    def fwd_body(c, carry):
        sl = pl.ds(pl.multiple_of(c * q, q), q)
        bc = b_ref[sl, :].astype(BF16)
        cc = c_ref[sl, :].astype(BF16)
        d_row = dr_ref[:, sl]
        p_all = pq_ref[sl, :]
        q_t = qt_ref[c]
        xtb = xtb_ref[:, sl]
        cb = ntdot(cc, bc)
        g2s = [bdot(jnp.where(slot_head == h, p_all, jnp.zeros_like(p_all)), q_t) for h in range(nh)]
        ms = []
        for h in range(nh):
            arg = jnp.where(lower, g2s[h][:, 0:q], g2s[h][:, q:])
            wgt = (jnp.where(lower, d_row[h:h + 1, :], 0.0)
                   + jnp.where(upper, d_row[nh + h:nh + h + 1, :], 0.0))
            ms.append((cb * jnp.exp2(arg) * wgt).astype(BF16))
        yd = []
        for h0 in range(0, nh, 2):
            lhs = jnp.concatenate(
                [jnp.concatenate([xtb[h0 * hd:(h0 + 1) * hd], zero_half], axis=0),
                 jnp.concatenate([zero_half, xtb[(h0 + 1) * hd:(h0 + 2) * hd]], axis=0)], axis=1)
            yd.append(ntdot(lhs, jnp.concatenate(ms[h0:h0 + 2], axis=1)))
        states = bdot(xdf_ref[:, sl], bc)
        h_prev = hf_ref[...]
        y_off = ntdot(h_prev.astype(BF16), cc) * head_rows(sc_ref[0 * nh:1 * nh, sl])
        hf_ref[...] = h_prev * head_rows(sc_ref[2 * nh:3 * nh, sl]) + states
        yt_ref[:, sl] = jnp.concatenate(yd, axis=0) + y_off
        return carry

    lax.fori_loop(0, nc, fwd_body, 0, unroll=8)

    def bwd_body(t, carry):
        c = nc - 1 - t
        sl = pl.ds(pl.multiple_of(c * q, q), q)
        bc = b_ref[sl, :].astype(BF16)
        cc = c_ref[sl, :].astype(BF16)
        states = bdot(xdb_ref[:, sl], bc)
        h_prev = hb_ref[...]
        y_off = ntdot(h_prev.astype(BF16), cc) * head_rows(sc_ref[1 * nh:2 * nh, sl])
        hb_ref[...] = h_prev * head_rows(sc_ref[3 * nh:4 * nh, sl]) + states
        y_ref[sl, :] = ((yt_ref[:, sl] + y_off).T + dx_ref[...] * x_ref[sl, :]).astype(y_ref.dtype)
        return carry

    lax.fori_loop(0, nc, bwd_body, 0, unroll=8)


def ssd_scan_bidir(xs, bc, dt, dt_bias_f, dt_bias_b, a_log_f, a_log_b, d_skip):
    bsz, s, _ = xs.shape
    nheads = dt.shape[1] // 2
    nh = nheads // SSM_GROUPS
    gw = nh * SSM_HEAD_DIM
    d_inner = nheads * SSM_HEAD_DIM
    q = SSD_CHUNK
    dt_row = dt.reshape(bsz, 2 * SSM_GROUPS, nh, s)
    prm = jnp.stack([jnp.concatenate([dt_bias_f, dt_bias_b]), jnp.concatenate([a_log_f, a_log_b])])
    p_row = jnp.transpose(prm.reshape(2, 2 * SSM_GROUPS, nh), (1, 2, 0))
    dx = jnp.repeat(d_skip, SSM_HEAD_DIM).reshape(SSM_GROUPS, 1, gw)
    assert DECAY_SLOTS * nh <= q and q % nh == 0 and D_STATE == q
    trio = jnp.asarray(np.concatenate([np.triu(np.ones((q, q), np.float32)), np.ones((q, q), np.float32)],
                                      axis=1), BF16)
    G = SSM_GROUPS
    nc = s // q

    return pl.pallas_call(
        _ssd_t_kernel,
        grid=(bsz, SSM_GROUPS),
        in_specs=[pl.BlockSpec((None, s, gw), lambda b, g: (b, 0, g)),
                  pl.BlockSpec((None, s, D_STATE), lambda b, g: (b, 0, g)),
                  pl.BlockSpec((None, s, D_STATE), lambda b, g: (b, 0, G + g)),
                  pl.BlockSpec((None, None, nh, s), lambda b, g: (b, g, 0, 0)),
                  pl.BlockSpec((None, None, nh, s), lambda b, g: (b, G + g, 0, 0)),
                  pl.BlockSpec((None, nh, 2), lambda b, g: (g, 0, 0)),
                  pl.BlockSpec((None, nh, 2), lambda b, g: (G + g, 0, 0)),
                  pl.BlockSpec((None, 1, gw), lambda b, g: (g, 0, 0)),
                  pl.BlockSpec((q, 2 * q), lambda b, g: (0, 0))],
        out_specs=pl.BlockSpec((None, s, gw), lambda b, g: (b, 0, g)),
        out_shape=jax.ShapeDtypeStruct((bsz, s, d_inner), BF16),
        scratch_shapes=[pltpu.VMEM((2 * nh, s), F32), pltpu.VMEM((2 * nh, s), F32),
                        pltpu.VMEM((4 * nh, s), F32),
                        pltpu.VMEM((s, q), BF16), pltpu.VMEM((nc, q, 2 * q), BF16),
                        pltpu.VMEM((gw, s), BF16), pltpu.VMEM((gw, s), BF16), pltpu.VMEM((gw, s), BF16),
                        pltpu.VMEM((gw, s), F32),
                        pltpu.VMEM((gw, D_STATE), F32), pltpu.VMEM((gw, D_STATE), F32)],
        compiler_params=_cparams("parallel", "parallel"),
        name="ssd_scan",
    )(xs, bc, bc, dt_row, dt_row, p_row, p_row, dx, trio)


def _gated_proj_route_kernel(y_ref, z_ref, gw_ref, w_ref, x_ref, g_ref,
                             nw_ref, sh_ref, sc_ref, rw_ref, rb_ref, lt_ref,
                             o_ref, ld_ref, wgt_ref, cnt_ref, acc_ref):
    _gated_proj_kernel(y_ref, z_ref, gw_ref, w_ref, x_ref, g_ref, o_ref, acc_ref)
    h = _modnorm(o_ref[...], nw_ref[...], sh_ref[...], sc_ref[...])
    rows, gates, total = _route_tile(h, rw_ref[...], rb_ref[...], lt_ref[...])
    ld_ref[...] = rows
    wgt_ref[...] = gates
    cnt_ref[...] = total[:, 0:cnt_ref.shape[1]]


def _gated_proj_kernel(y_ref, z_ref, gw_ref, w_ref, x_ref, g_ref, o_ref, acc_ref):
    k = y_ref.shape[1]
    ss = jnp.zeros((y_ref.shape[0], 1), F32)
    for lo in range(0, k, PROJ_CHUNK):
        hi = lo + PROJ_CHUNK
        t = y_ref[:, lo:hi].astype(F32) * _silu(z_ref[:, lo:hi].astype(F32))
        ss = ss + jnp.sum(t * t, axis=-1, keepdims=True)
        contrib = jnp.dot((t * gw_ref[:, lo:hi]).astype(BF16), w_ref[lo:hi, :], preferred_element_type=F32)
        if lo == 0:
            acc_ref[...] = contrib
        else:
            acc_ref[...] += contrib
    o_ref[...] = x_ref[...] + g_ref[...] * (acc_ref[...] * lax.rsqrt(ss * (1.0 / k) + EPS))


def gated_proj_route(y, z, gw, w, x, g, nw, sh, sc, router_w, router_b, *, ts):
    bsz, s, d = x.shape
    k = y.shape[2]
    ne = router_w.shape[1]
    nt = s // ts
    cum = jnp.asarray(np.concatenate([np.triu(np.ones((ts, ts), np.float32)), np.ones((ts, ts), np.float32)],
                                     axis=1), BF16)
    vec = pl.BlockSpec((None, 1, d), lambda b, i: (b, 0, 0))
    tok_spec = pl.BlockSpec((TOP_K, ts), lambda b, i: (0, b * nt + i))
    call = pl.pallas_call(
        _gated_proj_route_kernel,
        grid=(bsz, nt),
        in_specs=[pl.BlockSpec((None, ts, k), lambda b, i: (b, i, 0)),
                  pl.BlockSpec((None, ts, k), lambda b, i: (b, i, 0)),
                  pl.BlockSpec((1, k), lambda b, i: (0, 0)),
                  _resident((k, d)),
                  pl.BlockSpec((None, ts, d), lambda b, i: (b, i, 0)),
                  vec,
                  pl.BlockSpec((1, d), lambda b, i: (0, 0)), vec, vec,
                  _resident((ne, d)), pl.BlockSpec((ne, 1), lambda b, i: (0, 0)), _resident((ts, 2 * ts))],
        out_specs=[pl.BlockSpec((None, ts, d), lambda b, i: (b, i, 0)), tok_spec, tok_spec,
                   pl.BlockSpec((None, ne, 128), lambda b, i: (b * nt + i, 0, 0))],
        out_shape=[jax.ShapeDtypeStruct((bsz, s, d), F32),
                   jax.ShapeDtypeStruct((TOP_K, bsz * s), jnp.int32),
                   jax.ShapeDtypeStruct((TOP_K, bsz * s), F32),
                   jax.ShapeDtypeStruct((bsz * nt, ne, 128), jnp.int32)],
        scratch_shapes=[pltpu.VMEM((ts, d), F32)],
        compiler_params=_cparams("parallel", "parallel"),
        name="ssd_out_proj_route",
    )
    x_new, ldest, wgt, cnt = call(y, z, gw.reshape(1, k), w, x, g, nw.reshape(1, d), sh, sc,
                                  router_w.T, router_b.reshape(ne, 1), cum)
    return x_new, ldest, wgt, cnt[:, :, 0]


SEG_ROWS = 16
SEG_FIELDS = 3


def _route_tile(h, rw_t, rb_col, cum):
    def nt(a, b):
        return lax.dot_general(a, b, (((1,), (1,)), ((), ())), preferred_element_type=F32)

    h_hi = h.astype(BF16)
    h_lo = (h - h_hi.astype(F32)).astype(BF16)
    rw_hi = rw_t.astype(BF16)
    rw_lo = (rw_t - rw_hi.astype(F32)).astype(BF16)
    logits = nt(rw_hi, h_hi) + (nt(rw_hi, h_lo) + nt(rw_lo, h_hi)) + rb_col
    ne, ts = logits.shape
    eid = lax.broadcasted_iota(jnp.int32, (ne, ts), 0)
    m1 = jnp.max(logits, axis=0, keepdims=True)
    i1 = jnp.min(jnp.where(logits == m1, eid, ne), axis=0, keepdims=True)
    rest = jnp.where(eid == i1, -jnp.inf, logits)
    m2 = jnp.max(rest, axis=0, keepdims=True)
    i2 = jnp.min(jnp.where(rest == m2, eid, ne), axis=0, keepdims=True)
    e2 = jnp.exp(m2 - m1)
    w1 = 1.0 / (1.0 + e2)
    w2 = e2 / (1.0 + e2)
    oh1 = (eid == i1).astype(F32)
    oh2 = (eid == i2).astype(F32)
    chosen = oh1 + oh2
    both = jnp.dot(chosen.astype(BF16), cum, preferred_element_type=F32)
    before = both[:, 0:ts] - chosen
    total = both[:, ts:].astype(jnp.int32)
    seg_len = jnp.bitwise_and(total + (SEG_ROWS - 1), -SEG_ROWS).astype(F32)
    rows = [jnp.sum(jnp.where(eid < idx, seg_len, 0.0) + onehot * before, axis=0, keepdims=True)
            for onehot, idx in ((oh1, i1), (oh2, i2))]
    return jnp.concatenate(rows, axis=0).astype(jnp.int32), jnp.concatenate([w1, w2], axis=0), total


def _segment_copies(seg_ref, tile, n_experts, make_copy, *, wait):
    for e in range(n_experts):
        base = (tile * n_experts + e) * SEG_FIELDS
        local0 = seg_ref[base]
        global0 = seg_ref[base + 1]

        def body(i, carry, local0=local0, global0=global0):
            cp = make_copy(pl.multiple_of(local0 + i * SEG_ROWS, SEG_ROWS),
                           pl.multiple_of(global0 + i * SEG_ROWS, SEG_ROWS))
            if wait:
                cp.wait()
            else:
                cp.start()
            return carry

        lax.fori_loop(0, seg_ref[base + 2], body, 0)


def _dispatch_kernel(seg_ref, x_ref, nw_ref, sh_ref, sc_ref, ld_ref, hs_ref, buf_ref, zero_ref, sem,
                     *, n_token_tiles):
    tt = x_ref.shape[0]
    lc = buf_ref.shape[1]
    ne = N_EXPERTS
    tile = pl.program_id(0) * pl.num_programs(1) + pl.program_id(1)
    last = pl.num_programs(0) * pl.num_programs(1) - 1
    slot = tile % 2

    def copies(t, sl, wait):
        def make_copy(lo, go):
            return pltpu.make_async_copy(buf_ref.at[sl, pl.ds(lo, SEG_ROWS), :],
                                         hs_ref.at[pl.ds(go, SEG_ROWS), :], sem.at[sl])
        _segment_copies(seg_ref, t, ne, make_copy, wait=wait)

    h = _modnorm(x_ref[...], nw_ref[...], sh_ref[...], sc_ref[...]).astype(BF16)
    ld = ld_ref[...]
    rows = lax.broadcasted_iota(jnp.int32, (lc, tt), 0)
    perm = jnp.where(rows == ld[0:1, :], 1.0, jnp.where(rows == ld[1:2, :], 1.0, 0.0)).astype(BF16)
    buf_ref[slot] = jnp.dot(perm, h, preferred_element_type=F32).astype(BF16)
    copies(tile, slot, wait=False)

    @pl.when(tile > 0)
    def _():
        copies(tile - 1, 1 - slot, wait=True)

    @pl.when(tile == last)
    def _():
        copies(tile, slot, wait=True)
        zero_ref[...] = jnp.zeros_like(zero_ref)
        tails = n_token_tiles * ne * SEG_FIELDS
        for wait in (False, True):
            for e in range(ne):
                start = seg_ref[tails + 2 * e]

                def body(i, carry, start=start, wait=wait):
                    cp = pltpu.make_async_copy(
                        zero_ref, hs_ref.at[pl.ds(pl.multiple_of(start + i * SEG_ROWS, SEG_ROWS), SEG_ROWS), :],
                        sem.at[2])
                    if wait:
                        cp.wait()
                    else:
                        cp.start()
                    return carry

                lax.fori_loop(0, seg_ref[tails + 2 * e + 1], body, 0)


def moe_dispatch(x, nw, sh, sc, seg, ldest_rows, n_rows, *, tt, lc):
    bsz, s, d = x.shape
    nt = s // tt
    grid_spec = pltpu.PrefetchScalarGridSpec(
        num_scalar_prefetch=1,
        grid=(bsz, nt),
        in_specs=[pl.BlockSpec((None, tt, d), lambda b, i, sref: (b, i, 0)),
                  pl.BlockSpec((1, d), lambda b, i, sref: (0, 0)),
                  pl.BlockSpec((None, 1, d), lambda b, i, sref: (b, 0, 0)),
                  pl.BlockSpec((None, 1, d), lambda b, i, sref: (b, 0, 0)),
                  pl.BlockSpec((TOP_K, tt), lambda b, i, sref: (0, b * nt + i))],
        out_specs=pl.BlockSpec(memory_space=pl.ANY),
        scratch_shapes=[pltpu.VMEM((2, lc, d), BF16), pltpu.VMEM((SEG_ROWS, d), BF16),
                        pltpu.SemaphoreType.DMA((3,))],
    )
    return pl.pallas_call(
        functools.partial(_dispatch_kernel, n_token_tiles=bsz * nt),
        grid_spec=grid_spec,
        out_shape=jax.ShapeDtypeStruct((n_rows, d), BF16),
        compiler_params=_cparams("arbitrary", "arbitrary"),
        name="moe_dispatch",
    )(seg, x, nw.reshape(1, d), sh, sc, ldest_rows)


def _moe_kernel(te_ref, nu_ref, hs_ref, w1_ref, w3_ref, w2_ref, o_ref, acc_ref):
    i = pl.program_id(0)
    f = pl.program_id(1)

    @pl.when(i < nu_ref[0])
    def _():
        @pl.when(f == 0)
        def _():
            acc_ref[...] = jnp.zeros_like(acc_ref)

        h = hs_ref[...]
        for lo in range(0, w1_ref.shape[1], MOE_HIDDEN_CHUNK):
            hi = min(lo + MOE_HIDDEN_CHUNK, w1_ref.shape[1])
            a = jnp.dot(h, w1_ref[:, lo:hi].astype(BF16), preferred_element_type=F32)
            b = jnp.dot(h, w3_ref[:, lo:hi].astype(BF16), preferred_element_type=F32)
            t = (_silu(a) * b).astype(BF16)
            acc_ref[...] += jnp.dot(t, w2_ref[lo:hi, :].astype(BF16), preferred_element_type=F32)

        @pl.when(f == pl.num_programs(1) - 1)
        def _():
            o_ref[...] = acc_ref[...].astype(o_ref.dtype)

    @pl.when((i >= nu_ref[0]) & (f == 0))
    def _():
        o_ref[...] = jnp.zeros_like(o_ref)


def moe_experts(hs, tile_expert, n_used, w1, w3, w2, *, tm, tf):
    n_rows, d = hs.shape
    dff = w1.shape[2]
    nf = dff // tf
    n_tiles = n_rows // tm

    def last_used(i, nu):
        return jnp.maximum(jnp.minimum(i, nu[0] - 1), 0)

    def row_map(i, f, te, nu):
        return (last_used(i, nu), 0)

    def hidden_block(i, f, nu):
        t = last_used(i, nu)
        step = jnp.where(i < nu[0], f, nf - 1)
        return jnp.where(t % 2 == 0, step, nf - 1 - step)

    def w_in_map(i, f, te, nu):
        return (te[last_used(i, nu)], 0, hidden_block(i, f, nu))

    def w_out_map(i, f, te, nu):
        return (te[last_used(i, nu)], hidden_block(i, f, nu), 0)

    grid_spec = pltpu.PrefetchScalarGridSpec(
        num_scalar_prefetch=2,
        grid=(n_tiles, nf),
        in_specs=[pl.BlockSpec((tm, d), row_map),
                  pl.BlockSpec((None, d, tf), w_in_map),
                  pl.BlockSpec((None, d, tf), w_in_map),
                  pl.BlockSpec((None, tf, d), w_out_map)],
        out_specs=pl.BlockSpec((tm, d), lambda i, f, te, nu: (i, 0)),
        scratch_shapes=[pltpu.VMEM((tm, d), F32)],
    )
    return pl.pallas_call(
        _moe_kernel,
        grid_spec=grid_spec,
        out_shape=jax.ShapeDtypeStruct((n_rows, d), BF16),
        compiler_params=_cparams("arbitrary", "arbitrary"),
        name="moe_experts",
    )(tile_expert, n_used, hs, w1, w3, w2)


def _combine_kernel(seg_ref, ys_ref, x_ref, g_ref, wgt_ref, ld_ref, o_ref, buf_ref, sem):
    tt = x_ref.shape[0]
    lc = buf_ref.shape[1]
    ne = N_EXPERTS
    tile = pl.program_id(0) * pl.num_programs(1) + pl.program_id(1)
    last = pl.num_programs(0) * pl.num_programs(1) - 1
    slot = tile % 2

    def copies(t, sl, wait):
        def make_copy(lo, go):
            return pltpu.make_async_copy(ys_ref.at[pl.ds(go, SEG_ROWS), :],
                                         buf_ref.at[sl, pl.ds(lo, SEG_ROWS), :], sem.at[sl])
        _segment_copies(seg_ref, t, ne, make_copy, wait=wait)

    @pl.when(tile == 0)
    def _():
        buf_ref[...] = jnp.zeros_like(buf_ref)
        copies(tile, slot, wait=False)

    @pl.when(tile < last)
    def _():
        copies(tile + 1, 1 - slot, wait=False)

    copies(tile, slot, wait=True)

    ld = ld_ref[...]
    cols = lax.broadcasted_iota(jnp.int32, (tt, lc), 1)
    pick = jnp.concatenate([jnp.where(cols == ld[:, k:k + 1], 1.0, 0.0) for k in range(TOP_K)],
                           axis=0).astype(BF16)
    z = jnp.dot(pick, buf_ref[slot], preferred_element_type=F32)
    w = wgt_ref[...]
    mix = w[:, 0:1] * z[0:tt] + w[:, 1:2] * z[tt:]
    o_ref[...] = x_ref[...] + g_ref[...] * mix


def moe_combine(ys, seg, ldest, wgt, x, g, *, tt, lc):
    bsz, s, d = x.shape
    nt = s // tt
    tok_spec = pl.BlockSpec((tt, TOP_K), lambda b, i, sref: (b * nt + i, 0))
    grid_spec = pltpu.PrefetchScalarGridSpec(
        num_scalar_prefetch=1,
        grid=(bsz, nt),
        in_specs=[pl.BlockSpec(memory_space=pl.ANY),
                  pl.BlockSpec((None, tt, d), lambda b, i, sref: (b, i, 0)),
                  pl.BlockSpec((None, 1, d), lambda b, i, sref: (b, 0, 0)),
                  tok_spec, tok_spec],
        out_specs=pl.BlockSpec((None, tt, d), lambda b, i, sref: (b, i, 0)),
        scratch_shapes=[pltpu.VMEM((2, lc, d), BF16), pltpu.SemaphoreType.DMA((2,))],
    )
    return pl.pallas_call(
        _combine_kernel,
        grid_spec=grid_spec,
        out_shape=jax.ShapeDtypeStruct((bsz, s, d), F32),
        compiler_params=_cparams("arbitrary", "arbitrary"),
        name="moe_combine",
    )(seg, ys, x, g, wgt, ldest)


def _round_up(v, m):
    return ((v + m - 1) // m) * m


def moe_token_tile(s):
    return min(512, s)


def moe_residual(x, nw, sh, sc, g, ldest, wgt, cnt, w1, w3, w2, *, tm=MOE_TILE_ROWS):
    bsz, s, d = x.shape
    n_tok = bsz * s
    ne = w1.shape[0]
    tt = moe_token_tile(s)
    n_tt = n_tok // tt
    lc = _round_up(TOP_K * tt + ne * SEG_ROWS, 128)
    seg_len = _round_up(cnt.reshape(n_tt, ne), SEG_ROWS)
    local_start = jnp.cumsum(seg_len, axis=1) - seg_len
    padded = _round_up(jnp.sum(seg_len, axis=0), tm)
    ends = jnp.cumsum(padded)
    global_start = (ends - padded)[None, :] + jnp.cumsum(seg_len, axis=0) - seg_len
    n_rows = _round_up(n_tok * TOP_K + n_tt * ne * SEG_ROWS + ne * tm, tm)
    used_end = (ends - padded) + jnp.sum(seg_len, axis=0)
    next_start = jnp.concatenate([ends[:-1], jnp.full((1,), n_rows, ends.dtype)])
    tails = jnp.stack([used_end, (next_start - used_end) // SEG_ROWS], axis=-1)
    seg = jnp.concatenate([jnp.stack([local_start, global_start, seg_len // SEG_ROWS], axis=-1).reshape(-1),
                           tails.reshape(-1)]).astype(jnp.int32)
    n_tiles = n_rows // tm
    tile_start = jnp.arange(n_tiles, dtype=jnp.int32) * tm
    tile_expert = jnp.minimum(jnp.sum(tile_start[:, None] >= ends[None, :], axis=1), ne - 1).astype(jnp.int32)
    n_used = (ends[ne - 1:ne] // tm).astype(jnp.int32)
    hs = moe_dispatch(x, nw, sh, sc, seg, ldest, n_rows, tt=tt, lc=lc)
    ys = moe_experts(hs, tile_expert, n_used, w1, w3, w2, tm=tm, tf=w1.shape[2] // 2)
    return moe_combine(ys, seg, ldest.T, wgt.T, x, g, tt=tt, lc=lc)


def _split_mod(mod):
    return [mod[k] for k in range(mod.shape[0])]


def even_layer(x, c, rel_bias, ada_w, ada_b, norm1_w, in_w, q_norm_w, k_norm_w, sink, out_w,
               norm2_w, w1, w3, w2):
    s = x.shape[1]
    sh1, sc1, g1, sh2, sc2, g2 = _split_mod(ada_mod(c, ada_w, ada_b))
    proj = norm_mod_matmul(x, norm1_w, sh1, sc1, in_w.astype(BF16), ts=min(1024, s), tn=256, name="even_in_proj",
                           out_dtype=BF16)
    yf = fourier_mix(proj, tq=min(512, s))
    ya = window_attention(proj, band_bias(rel_bias), q_norm_w, k_norm_w, sink)
    x = cat_proj_residual(yf, ya, out_w.astype(BF16), x, g1, ts=min(1024, s))
    return ffn_residual(x, norm2_w, sh2, sc2, g2, w1.astype(BF16), w3.astype(BF16), w2,
                        ts=min(1024, s), tf=256)


def odd_layer(x, c, ada_w, ada_b, norm1_w, in_w, conv_w, conv_b, dt_bias_f, dt_bias_b, a_log_f, a_log_b,
              d_skip, gnorm_w, out_w, norm2_w, router_w, router_b, w1, w3, w2):
    s = x.shape[1]
    sh1, sc1, g1, sh2, sc2, g2 = _split_mod(ada_mod(c, ada_w, ada_b))
    d_inner = gnorm_w.shape[0]
    z, xs, bc, dt = ssm_in_proj_conv(x, norm1_w, sh1, sc1, in_w.astype(BF16), conv_w, conv_b,
                                     dz=d_inner, n_x=d_inner, ts=min(1024, s))
    y = ssd_scan_bidir(xs, bc, dt, dt_bias_f, dt_bias_b, a_log_f, a_log_b, d_skip)
    x, ldest, wgt, cnt = gated_proj_route(y, z, gnorm_w, out_w.astype(BF16), x, g1, norm2_w, sh2, sc2,
                                          router_w, router_b, ts=moe_token_tile(s))
    return moe_residual(x, norm2_w, sh2, sc2, g2, ldest, wgt, cnt,
                        w1, w3, w2)


def kernel(x, c, rel_bias, ev_ada_w, ev_ada_b, ev_norm1_w, ev_in_w, ev_q_norm_w, ev_k_norm_w, ev_sink, ev_out_w, ev_norm2_w, ev_ffn_w1, ev_ffn_w3, ev_ffn_w2, od_ada_w, od_ada_b, od_norm1_w, od_in_w, od_conv_w, od_conv_b, od_dt_bias_f, od_dt_bias_b, od_A_log_f, od_A_log_b, od_D, od_gnorm_w, od_out_w, od_norm2_w, od_router_w, od_router_b, od_moe_w1, od_moe_w3, od_moe_w2):
    depth = ev_ada_w.shape[0] + od_ada_w.shape[0]
    for i in range(depth):
        j = i // 2
        if i % 2 == 0:
            x = even_layer(x, c, rel_bias, ev_ada_w[j], ev_ada_b[j], ev_norm1_w[j], ev_in_w[j],
                           ev_q_norm_w[j], ev_k_norm_w[j], ev_sink[j], ev_out_w[j], ev_norm2_w[j],
                           ev_ffn_w1[j], ev_ffn_w3[j], ev_ffn_w2[j])
        else:
            x = odd_layer(x, c, od_ada_w[j], od_ada_b[j], od_norm1_w[j], od_in_w[j], od_conv_w[j],
                          od_conv_b[j], od_dt_bias_f[j], od_dt_bias_b[j], od_A_log_f[j], od_A_log_b[j],
                          od_D[j], od_gnorm_w[j], od_out_w[j], od_norm2_w[j], od_router_w[j],
                          od_router_b[j], od_moe_w1[j], od_moe_w3[j], od_moe_w2[j])
    return x
```

```python
import functools

import numpy as np
import jax
import jax.numpy as jnp
from jax import lax
from jax.experimental import pallas as pl
from jax.experimental.pallas import tpu as pltpu

F32 = jnp.float32
BF16 = jnp.bfloat16

EPS = 1e-6
FNET_GROUPS = 4
FNET_GROUP_DIM = 128
FNET_WIDTH = FNET_GROUPS * FNET_GROUP_DIM
ATTN_HEADS = 8
ATTN_KV_HEADS = 2
HEAD_DIM = 64
ATTN_WIDTH = ATTN_HEADS * HEAD_DIM
KV_WIDTH = ATTN_KV_HEADS * HEAD_DIM
WINDOW = 128
BLOCK = 128
REL_BUCKETS = 32
REL_MAX_DIST = 128
SSM_HEAD_DIM = 64
SSM_GROUPS = 4
D_STATE = 128
CONV_WIDTH = 5
SSD_CHUNK = 128
N_EXPERTS = 8
TOP_K = 2
NEG_BIG = -1e30
ATTN_QBLOCKS = 16

V7X_VMEM_LIMIT_BYTES = 56 * 1024 * 1024
MOE_TILE_ROWS = 768
MOE_HIDDEN_CHUNK = 256


def _cparams(*sem):
    return pltpu.CompilerParams(dimension_semantics=sem, vmem_limit_bytes=V7X_VMEM_LIMIT_BYTES)


def _modnorm(x, nw, sh, sc):
    ms = jnp.mean(x * x, axis=-1, keepdims=True)
    return x * lax.rsqrt(ms + EPS) * nw * (1.0 + sc) + sh


def _silu(x):
    return x * (1.0 / (1.0 + jnp.exp(-x)))


def _ada_kernel(c_ref, w_ref, b_ref, o_ref):
    cs = _silu(c_ref[...]).astype(BF16)
    val = jnp.dot(cs, w_ref[...].astype(BF16), preferred_element_type=F32) + b_ref[...]
    for row in range(val.shape[0]):
        o_ref[row] = val[row:row + 1, :]


def ada_mod(c, w, b, *, n_chunks=6):
    bsz, d = c.shape
    n = w.shape[1]
    tn = n // n_chunks
    return pl.pallas_call(
        _ada_kernel,
        grid=(n_chunks,),
        in_specs=[pl.BlockSpec((bsz, d), lambda j: (0, 0)),
                  pl.BlockSpec((d, tn), lambda j: (0, j)),
                  pl.BlockSpec((1, tn), lambda j: (0, j))],
        out_specs=pl.BlockSpec((None, bsz, 1, tn), lambda j: (j, 0, 0, 0)),
        out_shape=jax.ShapeDtypeStruct((n_chunks, bsz, 1, tn), F32),
        compiler_params=_cparams("arbitrary"),
        name="ada_mod",
    )(c, w, b.reshape(1, n))


def _nmm_kernel(x_ref, nw_ref, sh_ref, sc_ref, w_ref, o_ref, *, tn):
    h = _modnorm(x_ref[...], nw_ref[...], sh_ref[...], sc_ref[...]).astype(BF16)
    n = w_ref.shape[1]
    for lo in range(0, n, tn):
        o_ref[:, lo:lo + tn] = jnp.dot(h, w_ref[:, lo:lo + tn],
                                       preferred_element_type=F32).astype(o_ref.dtype)


def _resident(shape):
    return pl.BlockSpec(shape, lambda *_: tuple(0 for _ in shape), pipeline_mode=pl.Buffered(1))


def norm_mod_matmul(x, nw, sh, sc, w, *, ts, tn, name, out_dtype=F32):
    bsz, s, d = x.shape
    n = w.shape[1]
    assert n % tn == 0
    return pl.pallas_call(
        functools.partial(_nmm_kernel, tn=tn),
        grid=(bsz, s // ts),
        in_specs=[pl.BlockSpec((None, ts, d), lambda b, i: (b, i, 0)),
                  pl.BlockSpec((1, d), lambda b, i: (0, 0)),
                  pl.BlockSpec((None, 1, d), lambda b, i: (b, 0, 0)),
                  pl.BlockSpec((None, 1, d), lambda b, i: (b, 0, 0)),
                  _resident((d, n))],
        out_specs=pl.BlockSpec((None, ts, n), lambda b, i: (b, i, 0)),
        out_shape=jax.ShapeDtypeStruct((bsz, s, n), out_dtype),
        compiler_params=_cparams("parallel", "parallel"),
        name=name,
    )(x, nw.reshape(1, d), sh, sc, w)


def _dft_cos_sin(n):
    k = np.arange(n, dtype=np.int64)
    ang = ((k[:, None] * k[None, :]) % n).astype(np.float64) * (2.0 * np.pi / n)
    scale = 1.0 / np.sqrt(n)
    return np.cos(ang) * scale, np.sin(ang) * scale


def _fourier_kernel(u_ref, chan_ref, seq_ref, o_ref, ab_ref):
    s = u_ref.shape[0]

    @pl.when(pl.program_id(1) == 0)
    def _():
        for g in range(FNET_GROUPS):
            lo, hi = g * FNET_GROUP_DIM, (g + 1) * FNET_GROUP_DIM
            ug = u_ref[:, lo:hi].astype(BF16)
            cs = jnp.dot(ug, chan_ref[...], preferred_element_type=F32)
            ab_ref[0:s, lo:hi] = cs[:, :FNET_GROUP_DIM].astype(BF16)
            ab_ref[s:2 * s, lo:hi] = cs[:, FNET_GROUP_DIM:].astype(BF16)

    o_ref[...] = jnp.dot(seq_ref[...], ab_ref[...], preferred_element_type=F32).astype(o_ref.dtype)


def fourier_mix(proj, *, tq):
    bsz, s, _ = proj.shape
    cc, sc = _dft_cos_sin(FNET_GROUP_DIM)
    chan = jnp.asarray(np.concatenate([cc, sc], axis=1), BF16)
    cs, ss = _dft_cos_sin(s)
    seq = jnp.asarray(np.concatenate([cs, -ss], axis=1), BF16)
    return pl.pallas_call(
        _fourier_kernel,
        grid=(bsz, s // tq),
        in_specs=[pl.BlockSpec((None, s, FNET_WIDTH), lambda b, i: (b, 0, 0)),
                  pl.BlockSpec((FNET_GROUP_DIM, 2 * FNET_GROUP_DIM), lambda b, i: (0, 0)),
                  pl.BlockSpec((tq, 2 * s), lambda b, i: (i, 0))],
        out_specs=pl.BlockSpec((None, tq, FNET_WIDTH), lambda b, i: (b, i, 0)),
        out_shape=jax.ShapeDtypeStruct((bsz, s, FNET_WIDTH), BF16),
        scratch_shapes=[pltpu.VMEM((2 * s, FNET_WIDTH), BF16)],
        compiler_params=_cparams("parallel", "arbitrary"),
        name="fourier_mix",
    )(proj, chan, seq)


def _band_bucket_table():
    i = np.arange(BLOCK)[:, None]
    j = np.arange(3 * BLOCK)[None, :]
    rel = (j - BLOCK) - i
    half = REL_BUCKETS // 2
    max_exact = half // 2
    n = np.abs(rel)
    large = max_exact + (np.log(np.maximum(n, 1) / max_exact)
                         / np.log(REL_MAX_DIST / max_exact) * (half - max_exact)).astype(np.int32)
    large = np.minimum(large, half - 1)
    bucket = (rel > 0).astype(np.int32) * half + np.where(n < max_exact, n, large)
    return np.where(n <= WINDOW, bucket, -1).astype(np.int32)


def _bias_kernel(rb_ref, bucket_ref, o_ref):
    h = pl.program_id(0)
    bucket = bucket_ref[...]
    acc = jnp.full(bucket.shape, NEG_BIG, F32)
    for bkt in range(REL_BUCKETS):
        acc = jnp.where(bucket == bkt, rb_ref[bkt * ATTN_HEADS + h], acc)
    o_ref[...] = acc


def band_bias(rel_bias):
    bucket = jnp.asarray(_band_bucket_table())
    return pl.pallas_call(
        _bias_kernel,
        grid=(ATTN_HEADS,),
        in_specs=[pl.BlockSpec(memory_space=pltpu.SMEM),
                  pl.BlockSpec((BLOCK, 3 * BLOCK), lambda h: (0, 0))],
        out_specs=pl.BlockSpec((None, BLOCK, 3 * BLOCK), lambda h: (h, 0, 0)),
        out_shape=jax.ShapeDtypeStruct((ATTN_HEADS, BLOCK, 3 * BLOCK), F32),
        compiler_params=_cparams("arbitrary"),
        name="band_bias",
    )(rel_bias.reshape(-1), bucket)


def _head_mean_matrix(width):
    m = np.zeros((width, width), np.float32)
    for h in range(width // HEAD_DIM):
        m[h * HEAD_DIM:(h + 1) * HEAD_DIM, h * HEAD_DIM:(h + 1) * HEAD_DIM] = 1.0 / HEAD_DIM
    return m


def _heads_rms(t, mean_mat, w):
    sq = t * t
    hi = sq.astype(BF16)
    lo = (sq - hi.astype(F32)).astype(BF16)
    ms = (jnp.dot(hi, mean_mat, preferred_element_type=F32)
          + jnp.dot(lo, mean_mat, preferred_element_type=F32))
    return t * lax.rsqrt(ms + EPS) * w


def _attn_kernel(sink_ref, q_ref, kl_ref, kc_ref, kr_ref, vl_ref, vc_ref, vr_ref,
                 bias_ref, qnw_ref, knw_ref, qmean_ref, kmean_ref, o_ref):
    n = pl.program_id(1)
    nb = pl.num_programs(1) * ATTN_QBLOCKS
    k = jnp.concatenate([kl_ref[...], kc_ref[...], kr_ref[...]], axis=0).astype(F32)
    v = jnp.concatenate([vl_ref[...], vc_ref[...], vr_ref[...]], axis=0).astype(F32)
    col = lax.broadcasted_iota(jnp.int32, (1, 3 * BLOCK), 1)
    qn = _heads_rms(q_ref[...].astype(F32), qmean_ref[...], qnw_ref[...])
    kn = _heads_rms(k, kmean_ref[...], knw_ref[...])
    low = lax.broadcasted_iota(jnp.int32, (1, 2 * HEAD_DIM), 1) < HEAD_DIM
    kn_sw = pltpu.roll(kn, HEAD_DIM, axis=1)
    v_sw = pltpu.roll(v, HEAD_DIM, axis=1)
    k_dup = [jnp.where(low, kn, kn_sw).astype(BF16), jnp.where(low, kn_sw, kn).astype(BF16)]
    ones = jnp.ones((k.shape[0], 2 * HEAD_DIM), BF16)
    v_ext = [jnp.concatenate([v.astype(BF16), ones], axis=1),
             jnp.concatenate([v_sw.astype(BF16), ones], axis=1)]
    g = ATTN_HEADS // ATTN_KV_HEADS
    for qb in range(ATTN_QBLOCKS):
        blk = n * ATTN_QBLOCKS + qb
        band = slice(qb * BLOCK, (qb + 3) * BLOCK)
        first_key = jnp.where(blk == 0, BLOCK, 0)
        end_key = jnp.where(blk == nb - 1, 2 * BLOCK, 3 * BLOCK)
        outside = (col < first_key) | (col >= end_key)
        raw = []
        for h in range(ATTN_HEADS):
            m, idx, j = h // 2, h % 2, h // g
            qp = qn[qb * BLOCK:(qb + 1) * BLOCK, m * 2 * HEAD_DIM:(m + 1) * 2 * HEAD_DIM]
            qm = jnp.where(low if idx == 0 else jnp.logical_not(low), qp, 0.0).astype(BF16)
            raw.append(lax.dot_general(qm, k_dup[j][band], (((1,), (1,)), ((), ())),
                                       preferred_element_type=F32))
        probs, tails = [], []
        for h in range(ATTN_HEADS):
            logits = jnp.where(outside, NEG_BIG, raw[h] + bias_ref[h])
            sk = sink_ref[h]
            mx = jnp.maximum(jnp.max(logits, axis=-1, keepdims=True), sk)
            probs.append(jnp.exp(logits - mx).astype(BF16))
            tails.append(jnp.exp(sk - mx))
        res = []
        for h in range(ATTN_HEADS):
            idx, j = h % 2, h // g
            r = jnp.dot(probs[h], v_ext[idx if j == 0 else 1 - idx][band], preferred_element_type=F32)
            res.append(r[:, :2 * HEAD_DIM] / (r[:, 2 * HEAD_DIM:] + tails[h]))
        pairs = [jnp.where(low, res[2 * m], res[2 * m + 1]) for m in range(ATTN_HEADS // 2)]
        o_ref[qb * BLOCK:(qb + 1) * BLOCK, :] = jnp.concatenate(pairs, axis=-1).astype(o_ref.dtype)


def window_attention(proj, bias, q_norm_w, k_norm_w, sink):
    bsz, s, _ = proj.shape
    nb = s // BLOCK
    qcol = FNET_WIDTH // ATTN_WIDTH
    kcol = (FNET_WIDTH + ATTN_WIDTH) // KV_WIDTH
    vcol = kcol + 1

    qb = ATTN_QBLOCKS
    assert nb % qb == 0

    def kv_specs(col):
        return [pl.BlockSpec((None, BLOCK, KV_WIDTH), lambda b, n: (b, jnp.maximum(n * qb - 1, 0), col)),
                pl.BlockSpec((None, qb * BLOCK, KV_WIDTH), lambda b, n: (b, n, col)),
                pl.BlockSpec((None, BLOCK, KV_WIDTH), lambda b, n: (b, jnp.minimum((n + 1) * qb, nb - 1), col))]

    return pl.pallas_call(
        _attn_kernel,
        grid=(bsz, nb // qb),
        in_specs=[pl.BlockSpec(memory_space=pltpu.SMEM),
                  pl.BlockSpec((None, qb * BLOCK, ATTN_WIDTH), lambda b, n: (b, n, qcol)),
                  *kv_specs(kcol), *kv_specs(vcol),
                  pl.BlockSpec((ATTN_HEADS, BLOCK, 3 * BLOCK), lambda b, n: (0, 0, 0)),
                  pl.BlockSpec((1, ATTN_WIDTH), lambda b, n: (0, 0)),
                  pl.BlockSpec((1, KV_WIDTH), lambda b, n: (0, 0)),
                  pl.BlockSpec((ATTN_WIDTH, ATTN_WIDTH), lambda b, n: (0, 0)),
                  pl.BlockSpec((KV_WIDTH, KV_WIDTH), lambda b, n: (0, 0))],
        out_specs=pl.BlockSpec((None, qb * BLOCK, ATTN_WIDTH), lambda b, n: (b, n, 0)),
        out_shape=jax.ShapeDtypeStruct((bsz, s, ATTN_WIDTH), BF16),
        compiler_params=_cparams("parallel", "arbitrary"),
        name="window_attention",
    )(sink, proj, proj, proj, proj, proj, proj, proj, bias,
      (jnp.tile(q_norm_w, ATTN_HEADS) * (HEAD_DIM ** -0.5)).reshape(1, ATTN_WIDTH),
      jnp.tile(k_norm_w, ATTN_KV_HEADS).reshape(1, KV_WIDTH),
      jnp.asarray(_head_mean_matrix(ATTN_WIDTH), BF16), jnp.asarray(_head_mean_matrix(KV_WIDTH), BF16))


def _cat_proj_kernel(a1_ref, a2_ref, w_ref, x_ref, g_ref, o_ref):
    k1 = a1_ref.shape[1]
    y = jnp.dot(a1_ref[...].astype(BF16), w_ref[0:k1, :], preferred_element_type=F32)
    y = y + jnp.dot(a2_ref[...].astype(BF16), w_ref[k1:, :], preferred_element_type=F32)
    o_ref[...] = x_ref[...] + g_ref[...] * y


def cat_proj_residual(a1, a2, w, x, g, *, ts):
    bsz, s, d = x.shape
    k1, k2 = a1.shape[2], a2.shape[2]
    return pl.pallas_call(
        _cat_proj_kernel,
        grid=(bsz, s // ts),
        in_specs=[pl.BlockSpec((None, ts, k1), lambda b, i: (b, i, 0)),
                  pl.BlockSpec((None, ts, k2), lambda b, i: (b, i, 0)),
                  pl.BlockSpec((k1 + k2, d), lambda b, i: (0, 0)),
                  pl.BlockSpec((None, ts, d), lambda b, i: (b, i, 0)),
                  pl.BlockSpec((None, 1, d), lambda b, i: (b, 0, 0))],
        out_specs=pl.BlockSpec((None, ts, d), lambda b, i: (b, i, 0)),
        out_shape=jax.ShapeDtypeStruct((bsz, s, d), F32),
        compiler_params=_cparams("parallel", "parallel"),
        name="mixer_out_proj",
    )(a1, a2, w, x, g)


def _ffn_kernel(x_ref, nw_ref, sh_ref, sc_ref, g_ref, w1_ref, w3_ref, w2_ref, o_ref, acc_ref, *, tf):
    h = _modnorm(x_ref[...], nw_ref[...], sh_ref[...], sc_ref[...]).astype(BF16)
    dff = w1_ref.shape[1]
    for lo in range(0, dff, tf):
        a = jnp.dot(h, w1_ref[:, lo:lo + tf], preferred_element_type=F32)
        b = jnp.dot(h, w3_ref[:, lo:lo + tf], preferred_element_type=F32)
        t = (_silu(a) * b).astype(BF16)
        contrib = jnp.dot(t, w2_ref[lo:lo + tf, :].astype(BF16), preferred_element_type=F32)
        if lo == 0:
            acc_ref[...] = contrib
        else:
            acc_ref[...] += contrib
    o_ref[...] = x_ref[...] + g_ref[...] * acc_ref[...]


def ffn_residual(x, nw, sh, sc, g, w1, w3, w2, *, ts, tf):
    bsz, s, d = x.shape
    dff = w1.shape[1]
    assert dff % tf == 0
    vec = pl.BlockSpec((None, 1, d), lambda b, i: (b, 0, 0))
    return pl.pallas_call(
        functools.partial(_ffn_kernel, tf=tf),
        grid=(bsz, s // ts),
        in_specs=[pl.BlockSpec((None, ts, d), lambda b, i: (b, i, 0)),
                  pl.BlockSpec((1, d), lambda b, i: (0, 0)),
                  vec, vec, vec,
                  _resident((d, dff)), _resident((d, dff)), _resident((dff, d))],
        out_specs=pl.BlockSpec((None, ts, d), lambda b, i: (b, i, 0)),
        out_shape=jax.ShapeDtypeStruct((bsz, s, d), F32),
        scratch_shapes=[pltpu.VMEM((ts, d), F32)],
        compiler_params=_cparams("parallel", "parallel"),
        name="ffn_swiglu",
    )(x, nw.reshape(1, d), sh, sc, g, w1, w3, w2)


HALO_ROWS = 16
PROJ_CHUNK = 256
W_VIEW = 1024


def _proj_conv_kernel(x_ref, xp_ref, xn_ref, nw_ref, sh_ref, sc_ref, *rest, n_views):
    w_views = rest[:n_views]
    wd_ref, cw_ref, cb_ref, z_ref, xs_ref, bc_ref, dt_ref = rest[n_views:]
    i = pl.program_id(1)
    ts = x_ref.shape[0]
    half = CONV_WIDTH // 2
    dz = z_ref.shape[1]
    dc = cw_ref.shape[1]

    def wcols(c0):
        return w_views[c0 // W_VIEW][:, c0 % W_VIEW:c0 % W_VIEW + PROJ_CHUNK]

    def hnorm(ref):
        return _modnorm(ref[...], nw_ref[...], sh_ref[...], sc_ref[...])

    h = hnorm(x_ref).astype(BF16)
    h_prev = jnp.where(i == 0, 0.0, hnorm(xp_ref)).astype(BF16)
    h_next = jnp.where(i == pl.num_programs(1) - 1, 0.0, hnorm(xn_ref)).astype(BF16)
    h_ext = jnp.concatenate([h_prev, h, h_next], axis=0)

    n_x = xs_ref.shape[1]
    rows = h_ext.shape[0]

    def project(lo):
        return jnp.dot(h_ext, wcols(dz + lo), preferred_element_type=F32)

    def z_chunk(lo):
        hi = lo + PROJ_CHUNK
        z_ref[:, lo:hi] = jnp.dot(h, wcols(lo), preferred_element_type=F32).astype(z_ref.dtype)

    z_los = list(range(0, dz, PROJ_CHUNK))
    c_los = list(range(0, dc, PROJ_CHUNK))
    p_next = project(c_los[0])
    for n, lo in enumerate(c_los):
        hi = lo + PROJ_CHUNK
        p = p_next
        if n + 1 < len(c_los):
            p_next = project(c_los[n + 1])
        if z_los:
            z_chunk(z_los.pop(0))
        acc = jnp.zeros((ts, PROJ_CHUNK), F32) + cb_ref[:, lo:hi]
        for kk in range(CONV_WIDTH):
            shifted = p if kk == half else pltpu.roll(p, (half - kk) % rows, axis=0)
            acc = acc + shifted[HALO_ROWS:HALO_ROWS + ts, :] * cw_ref[kk:kk + 1, lo:hi]
        out = _silu(acc)
        if lo < n_x:
            xs_ref[:, lo:hi] = out
        else:
            bc_ref[:, lo - n_x:hi - n_x] = out.astype(bc_ref.dtype)
    for lo in z_los:
        z_chunk(lo)
    dt_ref[...] = lax.dot_general(wd_ref[...], h, (((1,), (1,)), ((), ())), preferred_element_type=F32)


def ssm_in_proj_conv(x, nw, sh, sc, w_in, conv_w, conv_b, *, dz, n_x, ts):
    bsz, s, d = x.shape
    dc = conv_w.shape[1]
    ddt = w_in.shape[1] - dz - dc
    assert ts % HALO_ROWS == 0 and n_x % PROJ_CHUNK == 0 and dc % PROJ_CHUNK == 0 and dz % PROJ_CHUNK == 0
    assert dz % W_VIEW == 0 and dc % W_VIEW == 0 and W_VIEW % PROJ_CHUNK == 0
    n_views = (dz + dc) // W_VIEW
    views = [pl.BlockSpec((d, W_VIEW), lambda b, i, k=k: (0, k), pipeline_mode=pl.Buffered(1))
             for k in range(n_views)]
    r = ts // HALO_ROWS
    last = s // HALO_ROWS - 1
    vec = pl.BlockSpec((None, 1, d), lambda b, i: (b, 0, 0))
    return pl.pallas_call(
        functools.partial(_proj_conv_kernel, n_views=n_views),
        grid=(bsz, s // ts),
        in_specs=[pl.BlockSpec((None, ts, d), lambda b, i: (b, i, 0)),
                  pl.BlockSpec((None, HALO_ROWS, d), lambda b, i: (b, jnp.maximum(i * r - 1, 0), 0)),
                  pl.BlockSpec((None, HALO_ROWS, d), lambda b, i: (b, jnp.minimum((i + 1) * r, last), 0)),
                  pl.BlockSpec((1, d), lambda b, i: (0, 0)), vec, vec,
                  *views, _resident((ddt, d)),
                  _resident((CONV_WIDTH, dc)), _resident((1, dc))],
        out_specs=[pl.BlockSpec((None, ts, dz), lambda b, i: (b, i, 0)),
                   pl.BlockSpec((None, ts, n_x), lambda b, i: (b, i, 0)),
                   pl.BlockSpec((None, ts, dc - n_x), lambda b, i: (b, i, 0)),
                   pl.BlockSpec((None, ddt, ts), lambda b, i: (b, 0, i))],
        out_shape=[jax.ShapeDtypeStruct((bsz, s, dz), BF16),
                   jax.ShapeDtypeStruct((bsz, s, n_x), F32),
                   jax.ShapeDtypeStruct((bsz, s, dc - n_x), BF16),
                   jax.ShapeDtypeStruct((bsz, ddt, s), F32)],
        compiler_params=_cparams("parallel", "parallel"),
        name="odd_in_proj_conv",
    )(x, x, x, nw.reshape(1, d), sh, sc, *([w_in] * n_views), w_in[:, dz + dc:].T, conv_w, conv_b.reshape(1, dc))


def _softplus(x):
    return jnp.maximum(x, 0.0) + jnp.log(1.0 + jnp.exp(-jnp.abs(x)))


LOG2E = 1.4426950408889634
DECAY_SLOTS = 12


def _bf16_parts3(v):
    hi = v.astype(BF16).astype(F32)
    r = v - hi
    mid = r.astype(BF16).astype(F32)
    lo = (r - mid).astype(BF16).astype(F32)
    return hi, mid, lo


def _ssd_t_kernel(x_ref, b_ref, c_ref, dt_f_ref, dt_b_ref, p_f_ref, p_b_ref, dx_ref, trio_ref, y_ref,
                  ar_ref, dr_ref, sc_ref, pq_ref, qt_ref, xtb_ref, xdf_ref, xdb_ref, yt_ref, hf_ref, hb_ref):
    s = x_ref.shape[0]
    q = SSD_CHUNK
    nh = p_f_ref.shape[0]
    nc = s // q
    hd = SSM_HEAD_DIM

    def row_params(raw_ref, p_ref):
        dt = _softplus(raw_ref[...] + p_ref[:, 0:1])
        return dt, (-LOG2E) * jnp.exp(p_ref[:, 1:2]) * dt

    dtf, af = row_params(dt_f_ref, p_f_ref)
    dtb, ab = row_params(dt_b_ref, p_b_ref)
    dr_ref[0:nh, :] = dtf
    dr_ref[nh:, :] = dtb
    ar_ref[0:nh, :] = af
    ar_ref[nh:, :] = ab

    li = lax.broadcasted_iota(jnp.int32, (q, q), 0)
    si = lax.broadcasted_iota(jnp.int32, (q, q), 1)
    lower = li >= si
    upper = li <= si
    slot_head = lax.broadcasted_iota(jnp.int32, (1, q), 1) % nh

    def bdot(a, b):
        return jnp.dot(a, b, preferred_element_type=F32)

    def ntdot(a, b):
        return lax.dot_general(a, b, (((1,), (1,)), ((), ())), preferred_element_type=F32)

    def head_rows(v):
        return jnp.concatenate([jnp.broadcast_to(v[h:h + 1, :], (hd, q)) for h in range(nh)], axis=0)

    ones = jnp.ones((nh, q), F32)
    zeros = jnp.zeros((nh, q), F32)

    def prep_body(c, carry):
        sl = pl.ds(pl.multiple_of(c * q, q), q)
        a_row = ar_ref[:, sl]
        d_row = dr_ref[:, sl]
        parts = jnp.concatenate([p.astype(BF16) for p in _bf16_parts3(a_row)], axis=0)
        cs = bdot(parts, trio_ref[...])
        cs = cs[0:2 * nh] + cs[2 * nh:4 * nh] + cs[4 * nh:6 * nh]
        i_f = cs[0:nh, 0:q]
        e_b = cs[nh:, 0:q] - a_row[nh:]
        tot_f = cs[0:nh, q:]
        tot_b = cs[nh:, q:]
        ih, im, il = _bf16_parts3(i_f)
        eh, em, el = _bf16_parts3(e_b)
        pad = [zeros] * (q // nh - DECAY_SLOTS)
        p_t = jnp.concatenate([ih, im, il, ones, ones, ones, -eh, -em, -el, ones, ones, ones] + pad, axis=0)
        qf_t = jnp.concatenate([ones, ones, ones, -ih, -im, -il] + [zeros] * 6 + pad, axis=0)
        qb_t = jnp.concatenate([zeros] * 6 + [ones, ones, ones, eh, em, el] + pad, axis=0)
        pq_ref[sl, :] = p_t.T.astype(BF16)
        qt_ref[c] = jnp.concatenate([qf_t, qb_t], axis=1).astype(BF16)
        sc_ref[0 * nh:1 * nh, sl] = jnp.exp2(i_f)
        sc_ref[1 * nh:2 * nh, sl] = jnp.exp2(tot_b - e_b)
        sc_ref[2 * nh:3 * nh, sl] = jnp.exp2(tot_f)
        sc_ref[3 * nh:4 * nh, sl] = jnp.exp2(tot_b)
        xt = x_ref[sl, :].T
        xtb_ref[:, sl] = xt.astype(BF16)
        xdf_ref[:, sl] = (xt * head_rows(jnp.exp2(tot_f - i_f) * d_row[0:nh])).astype(BF16)
        xdb_ref[:, sl] = (xt * head_rows(jnp.exp2(e_b) * d_row[nh:])).astype(BF16)
        return carry

    lax.fori_loop(0, nc, prep_body, 0, unroll=8)

    hf_ref[...] = jnp.zeros_like(hf_ref)
    hb_ref[...] = jnp.zeros_like(hb_ref)
    zero_half = jnp.zeros((hd, q), BF16)

    def fwd_body(c, carry):
        sl = pl.ds(pl.multiple_of(c * q, q), q)
        bc = b_ref[sl, :].astype(BF16)
        cc = c_ref[sl, :].astype(BF16)
        d_row = dr_ref[:, sl]
        p_all = pq_ref[sl, :]
        q_t = qt_ref[c]
        xtb = xtb_ref[:, sl]
        cb = ntdot(cc, bc)
        g2s = [bdot(jnp.where(slot_head == h, p_all, jnp.zeros_like(p_all)), q_t) for h in range(nh)]
        ms = []
        for h in range(nh):
            arg = jnp.where(lower, g2s[h][:, 0:q], g2s[h][:, q:])
            wgt = (jnp.where(lower, d_row[h:h + 1, :], 0.0)
                   + jnp.where(upper, d_row[nh + h:nh + h + 1, :], 0.0))
            ms.append((cb * jnp.exp2(arg) * wgt).astype(BF16))
        yd = []
        for h0 in range(0, nh, 2):
            lhs = jnp.concatenate(
                [jnp.concatenate([xtb[h0 * hd:(h0 + 1) * hd], zero_half], axis=0),
                 jnp.concatenate([zero_half, xtb[(h0 + 1) * hd:(h0 + 2) * hd]], axis=0)], axis=1)
            yd.append(ntdot(lhs, jnp.concatenate(ms[h0:h0 + 2], axis=1)))
        states = bdot(xdf_ref[:, sl], bc)
        h_prev = hf_ref[...]
        y_off = ntdot(h_prev.astype(BF16), cc) * head_rows(sc_ref[0 * nh:1 * nh, sl])
        hf_ref[...] = h_prev * head_rows(sc_ref[2 * nh:3 * nh, sl]) + states
        yt_ref[:, sl] = jnp.concatenate(yd, axis=0) + y_off
        return carry

    lax.fori_loop(0, nc, fwd_body, 0, unroll=8)

    def bwd_body(t, carry):
        c = nc - 1 - t
        sl = pl.ds(pl.multiple_of(c * q, q), q)
        bc = b_ref[sl, :].astype(BF16)
        cc = c_ref[sl, :].astype(BF16)
        states = bdot(xdb_ref[:, sl], bc)
        h_prev = hb_ref[...]
        y_off = ntdot(h_prev.astype(BF16), cc) * head_rows(sc_ref[1 * nh:2 * nh, sl])
        hb_ref[...] = h_prev * head_rows(sc_ref[3 * nh:4 * nh, sl]) + states
        y_ref[sl, :] = ((yt_ref[:, sl] + y_off).T + dx_ref[...] * x_ref[sl, :]).astype(y_ref.dtype)
        return carry

    lax.fori_loop(0, nc, bwd_body, 0, unroll=16)


def ssd_scan_bidir(xs, bc, dt, dt_bias_f, dt_bias_b, a_log_f, a_log_b, d_skip):
    bsz, s, _ = xs.shape
    nheads = dt.shape[1] // 2
    nh = nheads // SSM_GROUPS
    gw = nh * SSM_HEAD_DIM
    d_inner = nheads * SSM_HEAD_DIM
    q = SSD_CHUNK
    dt_row = dt.reshape(bsz, 2 * SSM_GROUPS, nh, s)
    prm = jnp.stack([jnp.concatenate([dt_bias_f, dt_bias_b]), jnp.concatenate([a_log_f, a_log_b])])
    p_row = jnp.transpose(prm.reshape(2, 2 * SSM_GROUPS, nh), (1, 2, 0))
    dx = jnp.repeat(d_skip, SSM_HEAD_DIM).reshape(SSM_GROUPS, 1, gw)
    assert DECAY_SLOTS * nh <= q and q % nh == 0 and D_STATE == q
    trio = jnp.asarray(np.concatenate([np.triu(np.ones((q, q), np.float32)), np.ones((q, q), np.float32)],
                                      axis=1), BF16)
    G = SSM_GROUPS
    nc = s // q

    return pl.pallas_call(
        _ssd_t_kernel,
        grid=(bsz, SSM_GROUPS),
        in_specs=[pl.BlockSpec((None, s, gw), lambda b, g: (b, 0, g)),
                  pl.BlockSpec((None, s, D_STATE), lambda b, g: (b, 0, g)),
                  pl.BlockSpec((None, s, D_STATE), lambda b, g: (b, 0, G + g)),
                  pl.BlockSpec((None, None, nh, s), lambda b, g: (b, g, 0, 0)),
                  pl.BlockSpec((None, None, nh, s), lambda b, g: (b, G + g, 0, 0)),
                  pl.BlockSpec((None, nh, 2), lambda b, g: (g, 0, 0)),
                  pl.BlockSpec((None, nh, 2), lambda b, g: (G + g, 0, 0)),
                  pl.BlockSpec((None, 1, gw), lambda b, g: (g, 0, 0)),
                  pl.BlockSpec((q, 2 * q), lambda b, g: (0, 0))],
        out_specs=pl.BlockSpec((None, s, gw), lambda b, g: (b, 0, g)),
        out_shape=jax.ShapeDtypeStruct((bsz, s, d_inner), BF16),
        scratch_shapes=[pltpu.VMEM((2 * nh, s), F32), pltpu.VMEM((2 * nh, s), F32),
                        pltpu.VMEM((4 * nh, s), F32),
                        pltpu.VMEM((s, q), BF16), pltpu.VMEM((nc, q, 2 * q), BF16),
                        pltpu.VMEM((gw, s), BF16), pltpu.VMEM((gw, s), BF16), pltpu.VMEM((gw, s), BF16),
                        pltpu.VMEM((gw, s), F32),
                        pltpu.VMEM((gw, D_STATE), F32), pltpu.VMEM((gw, D_STATE), F32)],
        compiler_params=_cparams("parallel", "parallel"),
        name="ssd_scan",
    )(xs, bc, bc, dt_row, dt_row, p_row, p_row, dx, trio)


def _gated_proj_route_kernel(y_ref, z_ref, gw_ref, w_ref, x_ref, g_ref,
                             nw_ref, sh_ref, sc_ref, rw_ref, rb_ref, lt_ref,
                             o_ref, ld_ref, wgt_ref, cnt_ref, acc_ref):
    _gated_proj_kernel(y_ref, z_ref, gw_ref, w_ref, x_ref, g_ref, o_ref, acc_ref)
    h = _modnorm(o_ref[...], nw_ref[...], sh_ref[...], sc_ref[...])
    rows, gates, total = _route_tile(h, rw_ref[...], rb_ref[...], lt_ref[...])
    ld_ref[...] = rows
    wgt_ref[...] = gates
    cnt_ref[...] = total[:, 0:cnt_ref.shape[1]]


def _gated_proj_kernel(y_ref, z_ref, gw_ref, w_ref, x_ref, g_ref, o_ref, acc_ref):
    k = y_ref.shape[1]
    ss = jnp.zeros((y_ref.shape[0], 1), F32)
    for lo in range(0, k, PROJ_CHUNK):
        hi = lo + PROJ_CHUNK
        t = y_ref[:, lo:hi].astype(F32) * _silu(z_ref[:, lo:hi].astype(F32))
        ss = ss + jnp.sum(t * t, axis=-1, keepdims=True)
        contrib = jnp.dot((t * gw_ref[:, lo:hi]).astype(BF16), w_ref[lo:hi, :], preferred_element_type=F32)
        if lo == 0:
            acc_ref[...] = contrib
        else:
            acc_ref[...] += contrib
    o_ref[...] = x_ref[...] + g_ref[...] * (acc_ref[...] * lax.rsqrt(ss * (1.0 / k) + EPS))


def gated_proj_route(y, z, gw, w, x, g, nw, sh, sc, router_w, router_b, *, ts):
    bsz, s, d = x.shape
    k = y.shape[2]
    ne = router_w.shape[1]
    nt = s // ts
    cum = jnp.asarray(np.concatenate([np.triu(np.ones((ts, ts), np.float32)), np.ones((ts, ts), np.float32)],
                                     axis=1), BF16)
    vec = pl.BlockSpec((None, 1, d), lambda b, i: (b, 0, 0))
    tok_spec = pl.BlockSpec((TOP_K, ts), lambda b, i: (0, b * nt + i))
    call = pl.pallas_call(
        _gated_proj_route_kernel,
        grid=(bsz, nt),
        in_specs=[pl.BlockSpec((None, ts, k), lambda b, i: (b, i, 0)),
                  pl.BlockSpec((None, ts, k), lambda b, i: (b, i, 0)),
                  pl.BlockSpec((1, k), lambda b, i: (0, 0)),
                  _resident((k, d)),
                  pl.BlockSpec((None, ts, d), lambda b, i: (b, i, 0)),
                  vec,
                  pl.BlockSpec((1, d), lambda b, i: (0, 0)), vec, vec,
                  _resident((ne, d)), pl.BlockSpec((ne, 1), lambda b, i: (0, 0)), _resident((ts, 2 * ts))],
        out_specs=[pl.BlockSpec((None, ts, d), lambda b, i: (b, i, 0)), tok_spec, tok_spec,
                   pl.BlockSpec((None, ne, 128), lambda b, i: (b * nt + i, 0, 0))],
        out_shape=[jax.ShapeDtypeStruct((bsz, s, d), F32),
                   jax.ShapeDtypeStruct((TOP_K, bsz * s), jnp.int32),
                   jax.ShapeDtypeStruct((TOP_K, bsz * s), F32),
                   jax.ShapeDtypeStruct((bsz * nt, ne, 128), jnp.int32)],
        scratch_shapes=[pltpu.VMEM((ts, d), F32)],
        compiler_params=_cparams("parallel", "parallel"),
        name="ssd_out_proj_route",
    )
    x_new, ldest, wgt, cnt = call(y, z, gw.reshape(1, k), w, x, g, nw.reshape(1, d), sh, sc,
                                  router_w.T, router_b.reshape(ne, 1), cum)
    return x_new, ldest, wgt, cnt[:, :, 0]


SEG_ROWS = 16
SEG_FIELDS = 3


def _route_tile(h, rw_t, rb_col, cum):
    def nt(a, b):
        return lax.dot_general(a, b, (((1,), (1,)), ((), ())), preferred_element_type=F32)

    h_hi = h.astype(BF16)
    h_lo = (h - h_hi.astype(F32)).astype(BF16)
    rw_hi = rw_t.astype(BF16)
    rw_lo = (rw_t - rw_hi.astype(F32)).astype(BF16)
    logits = nt(rw_hi, h_hi) + (nt(rw_hi, h_lo) + nt(rw_lo, h_hi)) + rb_col
    ne, ts = logits.shape
    eid = lax.broadcasted_iota(jnp.int32, (ne, ts), 0)
    m1 = jnp.max(logits, axis=0, keepdims=True)
    i1 = jnp.min(jnp.where(logits == m1, eid, ne), axis=0, keepdims=True)
    rest = jnp.where(eid == i1, -jnp.inf, logits)
    m2 = jnp.max(rest, axis=0, keepdims=True)
    i2 = jnp.min(jnp.where(rest == m2, eid, ne), axis=0, keepdims=True)
    e2 = jnp.exp(m2 - m1)
    w1 = 1.0 / (1.0 + e2)
    w2 = e2 / (1.0 + e2)
    oh1 = (eid == i1).astype(F32)
    oh2 = (eid == i2).astype(F32)
    chosen = oh1 + oh2
    both = jnp.dot(chosen.astype(BF16), cum, preferred_element_type=F32)
    before = both[:, 0:ts] - chosen
    total = both[:, ts:].astype(jnp.int32)
    seg_len = jnp.bitwise_and(total + (SEG_ROWS - 1), -SEG_ROWS).astype(F32)
    rows = [jnp.sum(jnp.where(eid < idx, seg_len, 0.0) + onehot * before, axis=0, keepdims=True)
            for onehot, idx in ((oh1, i1), (oh2, i2))]
    return jnp.concatenate(rows, axis=0).astype(jnp.int32), jnp.concatenate([w1, w2], axis=0), total


def _segment_copies(seg_ref, tile, n_experts, make_copy, *, wait):
    for e in range(n_experts):
        base = (tile * n_experts + e) * SEG_FIELDS
        local0 = seg_ref[base]
        global0 = seg_ref[base + 1]

        def body(i, carry, local0=local0, global0=global0):
            cp = make_copy(pl.multiple_of(local0 + i * SEG_ROWS, SEG_ROWS),
                           pl.multiple_of(global0 + i * SEG_ROWS, SEG_ROWS))
            if wait:
                cp.wait()
            else:
                cp.start()
            return carry

        lax.fori_loop(0, seg_ref[base + 2], body, 0)


def _dispatch_kernel(seg_ref, x_ref, nw_ref, sh_ref, sc_ref, ld_ref, hs_ref, buf_ref, zero_ref, sem,
                     *, n_token_tiles):
    tt = x_ref.shape[0]
    lc = buf_ref.shape[1]
    ne = N_EXPERTS
    tile = pl.program_id(0) * pl.num_programs(1) + pl.program_id(1)
    last = pl.num_programs(0) * pl.num_programs(1) - 1
    slot = tile % 2

    def copies(t, sl, wait):
        def make_copy(lo, go):
            return pltpu.make_async_copy(buf_ref.at[sl, pl.ds(lo, SEG_ROWS), :],
                                         hs_ref.at[pl.ds(go, SEG_ROWS), :], sem.at[sl])
        _segment_copies(seg_ref, t, ne, make_copy, wait=wait)

    h = _modnorm(x_ref[...], nw_ref[...], sh_ref[...], sc_ref[...]).astype(BF16)
    ld = ld_ref[...]
    rows = lax.broadcasted_iota(jnp.int32, (lc, tt), 0)
    perm = jnp.where(rows == ld[0:1, :], 1.0, jnp.where(rows == ld[1:2, :], 1.0, 0.0)).astype(BF16)
    buf_ref[slot] = jnp.dot(perm, h, preferred_element_type=F32).astype(BF16)
    copies(tile, slot, wait=False)

    @pl.when(tile > 0)
    def _():
        copies(tile - 1, 1 - slot, wait=True)

    @pl.when(tile == last)
    def _():
        copies(tile, slot, wait=True)
        zero_ref[...] = jnp.zeros_like(zero_ref)
        tails = n_token_tiles * ne * SEG_FIELDS
        for wait in (False, True):
            for e in range(ne):
                start = seg_ref[tails + 2 * e]

                def body(i, carry, start=start, wait=wait):
                    cp = pltpu.make_async_copy(
                        zero_ref, hs_ref.at[pl.ds(pl.multiple_of(start + i * SEG_ROWS, SEG_ROWS), SEG_ROWS), :],
                        sem.at[2])
                    if wait:
                        cp.wait()
                    else:
                        cp.start()
                    return carry

                lax.fori_loop(0, seg_ref[tails + 2 * e + 1], body, 0)


def moe_dispatch(x, nw, sh, sc, seg, ldest_rows, n_rows, *, tt, lc):
    bsz, s, d = x.shape
    nt = s // tt
    grid_spec = pltpu.PrefetchScalarGridSpec(
        num_scalar_prefetch=1,
        grid=(bsz, nt),
        in_specs=[pl.BlockSpec((None, tt, d), lambda b, i, sref: (b, i, 0)),
                  pl.BlockSpec((1, d), lambda b, i, sref: (0, 0)),
                  pl.BlockSpec((None, 1, d), lambda b, i, sref: (b, 0, 0)),
                  pl.BlockSpec((None, 1, d), lambda b, i, sref: (b, 0, 0)),
                  pl.BlockSpec((TOP_K, tt), lambda b, i, sref: (0, b * nt + i))],
        out_specs=pl.BlockSpec(memory_space=pl.ANY),
        scratch_shapes=[pltpu.VMEM((2, lc, d), BF16), pltpu.VMEM((SEG_ROWS, d), BF16),
                        pltpu.SemaphoreType.DMA((3,))],
    )
    return pl.pallas_call(
        functools.partial(_dispatch_kernel, n_token_tiles=bsz * nt),
        grid_spec=grid_spec,
        out_shape=jax.ShapeDtypeStruct((n_rows, d), BF16),
        compiler_params=_cparams("arbitrary", "arbitrary"),
        name="moe_dispatch",
    )(seg, x, nw.reshape(1, d), sh, sc, ldest_rows)


def _moe_kernel(te_ref, nu_ref, hs_ref, w1_ref, w3_ref, w2_ref, o_ref, acc_ref):
    i = pl.program_id(0)
    f = pl.program_id(1)

    @pl.when(i < nu_ref[0])
    def _():
        @pl.when(f == 0)
        def _():
            acc_ref[...] = jnp.zeros_like(acc_ref)

        h = hs_ref[...]
        for lo in range(0, w1_ref.shape[1], MOE_HIDDEN_CHUNK):
            hi = min(lo + MOE_HIDDEN_CHUNK, w1_ref.shape[1])
            a = jnp.dot(h, w1_ref[:, lo:hi].astype(BF16), preferred_element_type=F32)
            b = jnp.dot(h, w3_ref[:, lo:hi].astype(BF16), preferred_element_type=F32)
            t = (_silu(a) * b).astype(BF16)
            acc_ref[...] += jnp.dot(t, w2_ref[lo:hi, :].astype(BF16), preferred_element_type=F32)

        @pl.when(f == pl.num_programs(1) - 1)
        def _():
            o_ref[...] = acc_ref[...].astype(o_ref.dtype)

    @pl.when((i >= nu_ref[0]) & (f == 0))
    def _():
        o_ref[...] = jnp.zeros_like(o_ref)


def moe_experts(hs, tile_expert, n_used, w1, w3, w2, *, tm, tf):
    n_rows, d = hs.shape
    dff = w1.shape[2]
    nf = dff // tf
    n_tiles = n_rows // tm

    def last_used(i, nu):
        return jnp.maximum(jnp.minimum(i, nu[0] - 1), 0)

    def row_map(i, f, te, nu):
        return (last_used(i, nu), 0)

    def hidden_block(i, f, nu):
        t = last_used(i, nu)
        step = jnp.where(i < nu[0], f, nf - 1)
        return jnp.where(t % 2 == 0, step, nf - 1 - step)

    def w_in_map(i, f, te, nu):
        return (te[last_used(i, nu)], 0, hidden_block(i, f, nu))

    def w_out_map(i, f, te, nu):
        return (te[last_used(i, nu)], hidden_block(i, f, nu), 0)

    grid_spec = pltpu.PrefetchScalarGridSpec(
        num_scalar_prefetch=2,
        grid=(n_tiles, nf),
        in_specs=[pl.BlockSpec((tm, d), row_map),
                  pl.BlockSpec((None, d, tf), w_in_map),
                  pl.BlockSpec((None, d, tf), w_in_map),
                  pl.BlockSpec((None, tf, d), w_out_map)],
        out_specs=pl.BlockSpec((tm, d), lambda i, f, te, nu: (i, 0)),
        scratch_shapes=[pltpu.VMEM((tm, d), F32)],
    )
    return pl.pallas_call(
        _moe_kernel,
        grid_spec=grid_spec,
        out_shape=jax.ShapeDtypeStruct((n_rows, d), BF16),
        compiler_params=_cparams("arbitrary", "arbitrary"),
        name="moe_experts",
    )(tile_expert, n_used, hs, w1, w3, w2)


def _combine_kernel(seg_ref, ys_ref, x_ref, g_ref, wgt_ref, ld_ref, o_ref, buf_ref, sem):
    tt = x_ref.shape[0]
    lc = buf_ref.shape[1]
    ne = N_EXPERTS
    tile = pl.program_id(0) * pl.num_programs(1) + pl.program_id(1)
    last = pl.num_programs(0) * pl.num_programs(1) - 1
    slot = tile % 2

    def copies(t, sl, wait):
        def make_copy(lo, go):
            return pltpu.make_async_copy(ys_ref.at[pl.ds(go, SEG_ROWS), :],
                                         buf_ref.at[sl, pl.ds(lo, SEG_ROWS), :], sem.at[sl])
        _segment_copies(seg_ref, t, ne, make_copy, wait=wait)

    @pl.when(tile == 0)
    def _():
        buf_ref[...] = jnp.zeros_like(buf_ref)
        copies(tile, slot, wait=False)

    @pl.when(tile < last)
    def _():
        copies(tile + 1, 1 - slot, wait=False)

    copies(tile, slot, wait=True)

    ld = ld_ref[...]
    cols = lax.broadcasted_iota(jnp.int32, (tt, lc), 1)
    pick = jnp.concatenate([jnp.where(cols == ld[:, k:k + 1], 1.0, 0.0) for k in range(TOP_K)],
                           axis=0).astype(BF16)
    z = jnp.dot(pick, buf_ref[slot], preferred_element_type=F32)
    w = wgt_ref[...]
    mix = w[:, 0:1] * z[0:tt] + w[:, 1:2] * z[tt:]
    o_ref[...] = x_ref[...] + g_ref[...] * mix


def moe_combine(ys, seg, ldest, wgt, x, g, *, tt, lc):
    bsz, s, d = x.shape
    nt = s // tt
    tok_spec = pl.BlockSpec((tt, TOP_K), lambda b, i, sref: (b * nt + i, 0))
    grid_spec = pltpu.PrefetchScalarGridSpec(
        num_scalar_prefetch=1,
        grid=(bsz, nt),
        in_specs=[pl.BlockSpec(memory_space=pl.ANY),
                  pl.BlockSpec((None, tt, d), lambda b, i, sref: (b, i, 0)),
                  pl.BlockSpec((None, 1, d), lambda b, i, sref: (b, 0, 0)),
                  tok_spec, tok_spec],
        out_specs=pl.BlockSpec((None, tt, d), lambda b, i, sref: (b, i, 0)),
        scratch_shapes=[pltpu.VMEM((2, lc, d), BF16), pltpu.SemaphoreType.DMA((2,))],
    )
    return pl.pallas_call(
        _combine_kernel,
        grid_spec=grid_spec,
        out_shape=jax.ShapeDtypeStruct((bsz, s, d), F32),
        compiler_params=_cparams("arbitrary", "arbitrary"),
        name="moe_combine",
    )(seg, ys, x, g, wgt, ldest)


def _round_up(v, m):
    return ((v + m - 1) // m) * m


def moe_token_tile(s):
    return min(512, s)


def moe_residual(x, nw, sh, sc, g, ldest, wgt, cnt, w1, w3, w2, *, tm=MOE_TILE_ROWS):
    bsz, s, d = x.shape
    n_tok = bsz * s
    ne = w1.shape[0]
    tt = moe_token_tile(s)
    n_tt = n_tok // tt
    lc = _round_up(TOP_K * tt + ne * SEG_ROWS, 128)
    seg_len = _round_up(cnt.reshape(n_tt, ne), SEG_ROWS)
    local_start = jnp.cumsum(seg_len, axis=1) - seg_len
    padded = _round_up(jnp.sum(seg_len, axis=0), tm)
    ends = jnp.cumsum(padded)
    global_start = (ends - padded)[None, :] + jnp.cumsum(seg_len, axis=0) - seg_len
    n_rows = _round_up(n_tok * TOP_K + n_tt * ne * SEG_ROWS + ne * tm, tm)
    used_end = (ends - padded) + jnp.sum(seg_len, axis=0)
    next_start = jnp.concatenate([ends[:-1], jnp.full((1,), n_rows, ends.dtype)])
    tails = jnp.stack([used_end, (next_start - used_end) // SEG_ROWS], axis=-1)
    seg = jnp.concatenate([jnp.stack([local_start, global_start, seg_len // SEG_ROWS], axis=-1).reshape(-1),
                           tails.reshape(-1)]).astype(jnp.int32)
    n_tiles = n_rows // tm
    tile_start = jnp.arange(n_tiles, dtype=jnp.int32) * tm
    tile_expert = jnp.minimum(jnp.sum(tile_start[:, None] >= ends[None, :], axis=1), ne - 1).astype(jnp.int32)
    n_used = (ends[ne - 1:ne] // tm).astype(jnp.int32)
    hs = moe_dispatch(x, nw, sh, sc, seg, ldest, n_rows, tt=tt, lc=lc)
    ys = moe_experts(hs, tile_expert, n_used, w1, w3, w2, tm=tm, tf=w1.shape[2] // 2)
    return moe_combine(ys, seg, ldest.T, wgt.T, x, g, tt=tt, lc=lc)


def _split_mod(mod):
    return [mod[k] for k in range(mod.shape[0])]


def even_layer(x, c, rel_bias, ada_w, ada_b, norm1_w, in_w, q_norm_w, k_norm_w, sink, out_w,
               norm2_w, w1, w3, w2):
    s = x.shape[1]
    sh1, sc1, g1, sh2, sc2, g2 = _split_mod(ada_mod(c, ada_w, ada_b))
    proj = norm_mod_matmul(x, norm1_w, sh1, sc1, in_w.astype(BF16), ts=min(1024, s), tn=256, name="even_in_proj",
                           out_dtype=BF16)
    yf = fourier_mix(proj, tq=min(1024, s))
    ya = window_attention(proj, band_bias(rel_bias), q_norm_w, k_norm_w, sink)
    x = cat_proj_residual(yf, ya, out_w.astype(BF16), x, g1, ts=min(1024, s))
    return ffn_residual(x, norm2_w, sh2, sc2, g2, w1.astype(BF16), w3.astype(BF16), w2,
                        ts=min(1024, s), tf=256)


def odd_layer(x, c, ada_w, ada_b, norm1_w, in_w, conv_w, conv_b, dt_bias_f, dt_bias_b, a_log_f, a_log_b,
              d_skip, gnorm_w, out_w, norm2_w, router_w, router_b, w1, w3, w2):
    s = x.shape[1]
    sh1, sc1, g1, sh2, sc2, g2 = _split_mod(ada_mod(c, ada_w, ada_b))
    d_inner = gnorm_w.shape[0]
    z, xs, bc, dt = ssm_in_proj_conv(x, norm1_w, sh1, sc1, in_w.astype(BF16), conv_w, conv_b,
                                     dz=d_inner, n_x=d_inner, ts=min(1024, s))
    y = ssd_scan_bidir(xs, bc, dt, dt_bias_f, dt_bias_b, a_log_f, a_log_b, d_skip)
    x, ldest, wgt, cnt = gated_proj_route(y, z, gnorm_w, out_w.astype(BF16), x, g1, norm2_w, sh2, sc2,
                                          router_w, router_b, ts=moe_token_tile(s))
    return moe_residual(x, norm2_w, sh2, sc2, g2, ldest, wgt, cnt,
                        w1, w3, w2)


def kernel(x, c, rel_bias, ev_ada_w, ev_ada_b, ev_norm1_w, ev_in_w, ev_q_norm_w, ev_k_norm_w, ev_sink, ev_out_w, ev_norm2_w, ev_ffn_w1, ev_ffn_w3, ev_ffn_w2, od_ada_w, od_ada_b, od_norm1_w, od_in_w, od_conv_w, od_conv_b, od_dt_bias_f, od_dt_bias_b, od_A_log_f, od_A_log_b, od_D, od_gnorm_w, od_out_w, od_norm2_w, od_router_w, od_router_b, od_moe_w1, od_moe_w3, od_moe_w2):
    depth = ev_ada_w.shape[0] + od_ada_w.shape[0]
    for i in range(depth):
        j = i // 2
        if i % 2 == 0:
            x = even_layer(x, c, rel_bias, ev_ada_w[j], ev_ada_b[j], ev_norm1_w[j], ev_in_w[j],
                           ev_q_norm_w[j], ev_k_norm_w[j], ev_sink[j], ev_out_w[j], ev_norm2_w[j],
                           ev_ffn_w1[j], ev_ffn_w3[j], ev_ffn_w2[j])
        else:
            x = odd_layer(x, c, od_ada_w[j], od_ada_b[j], od_norm1_w[j], od_in_w[j], od_conv_w[j],
                          od_conv_b[j], od_dt_bias_f[j], od_dt_bias_b[j], od_A_log_f[j], od_A_log_b[j],
                          od_D[j], od_gnorm_w[j], od_out_w[j], od_norm2_w[j], od_router_w[j],
                          od_router_b[j], od_moe_w1[j], od_moe_w3[j], od_moe_w2[j])
    return x
```
